```python
import jax, jax.numpy as jnp
from jax import lax
import numpy as np

D_MODEL = 1024
BATCH = 8
SEQ = 4096
DEPTH = 2

CHUNK = 64
N_PREV_CHUNKS = 8
BAND = (N_PREV_CHUNKS + 1) * CHUNK
HEAD_DIM = 64
H_A = 8
H_B = 8
W_A = H_A * HEAD_DIM
W_B = H_B * HEAD_DIM
W_MIX = W_A + W_B
W_IN = 3 * W_MIX
REL_CLIP = 128
SB_BLOCK = 128
D_FF = 2816
CONV_WIDTH = 3
N_MOD = 6
EPS = 1e-6

kernel_name = "hybrid_chunked_stickbreaking_convffn"


def rms_norm(x):
    xf = x.astype(jnp.float32)
    y = xf * lax.rsqrt(jnp.mean(xf * xf, axis=-1, keepdims=True) + EPS)
    return y.astype(x.dtype)


def modulate(h, shift, scale):
    return h * (1 + scale[:, None, :]) + shift[:, None, :]


def rel_bias_band(rel_bias):
    q_pos = N_PREV_CHUNKS * CHUNK + jnp.arange(CHUNK)
    k_pos = jnp.arange(BAND)
    dist = jnp.clip(q_pos[:, None] - k_pos[None, :], -REL_CLIP, REL_CLIP) + REL_CLIP
    return rel_bias[:, dist]


def chunked_rel_attention(q, k, v, rel_bias):
    b, s, h, dh = q.shape
    n_chunks = s // CHUNK
    qc = q.reshape(b, n_chunks, CHUNK, h, dh)
    pad = ((0, 0), (N_PREV_CHUNKS, 0), (0, 0), (0, 0), (0, 0))
    kp = jnp.pad(k.reshape(b, n_chunks, CHUNK, h, dh), pad)
    vp = jnp.pad(v.reshape(b, n_chunks, CHUNK, h, dh), pad)
    band_idx = jnp.arange(n_chunks)[:, None] + jnp.arange(N_PREV_CHUNKS + 1)[None, :]
    kb = kp[:, band_idx].reshape(b, n_chunks, BAND, h, dh)
    vb = vp[:, band_idx].reshape(b, n_chunks, BAND, h, dh)
    scores = jnp.einsum("bcqhd,bckhd->bhcqk", qc, kb).astype(jnp.float32) * (dh ** -0.5)
    scores = scores + rel_bias_band(rel_bias).astype(jnp.float32)[None, :, None]
    key_chunk = (jnp.arange(n_chunks)[:, None] - N_PREV_CHUNKS
                 + (jnp.arange(BAND) // CHUNK)[None, :])
    scores = jnp.where((key_chunk >= 0)[None, None, :, None, :], scores, -jnp.inf)
    probs = jax.nn.softmax(scores, axis=-1).astype(v.dtype)
    out = jnp.einsum("bhcqk,bckhd->bcqhd", probs, vb)
    return out.reshape(b, s, h * dh)


def stick_breaking_attention(q, k, v):
    b, s, h, dh = q.shape
    scale = dh ** -0.5
    outs = []
    for blk in range(s // SB_BLOCK):
        q_start = blk * SB_BLOCK
        k_end = q_start + SB_BLOCK
        logits = jnp.einsum("bqhd,bkhd->bhqk", q[:, q_start:k_end],
                            k[:, :k_end]).astype(jnp.float32) * scale
        strict = (q_start + jnp.arange(SB_BLOCK))[:, None] > jnp.arange(k_end)[None, :]
        log_beta = jax.nn.log_sigmoid(logits)
        log_keep = jnp.where(strict, jax.nn.log_sigmoid(-logits), 0.0)
        log_tail = lax.cumsum(log_keep, axis=3, reverse=True) - log_keep
        weights = jnp.where(strict, jnp.exp(log_beta + log_tail), 0.0).astype(v.dtype)
        outs.append(jnp.einsum("bhqk,bkhd->bqhd", weights, v[:, :k_end]))
    return jnp.concatenate(outs, axis=1).reshape(b, s, h * dh)


def causal_depthwise_conv(u, w, bias):
    s = u.shape[1]
    up = jnp.pad(u, ((0, 0), (CONV_WIDTH - 1, 0), (0, 0)))
    y = w[0] * up[:, 0:s]
    for i in range(1, CONV_WIDTH):
        y = y + w[i] * up[:, i:i + s]
    return y + bias


def hybrid_layer(x, c_act, w_ada, b_ada, w_in, rel_bias, g_a, g_b, w_out,
                 w_up, conv_w, conv_b, w_down):
    b, s, _ = x.shape
    mod = c_act @ w_ada + b_ada
    shift_mix, scale_mix, gate_mix, shift_ffn, scale_ffn, gate_ffn = jnp.split(mod, N_MOD, axis=-1)

    h = modulate(rms_norm(x), shift_mix, scale_mix)
    proj = h @ w_in
    cuts = [W_A, 2 * W_A, 3 * W_A, 3 * W_A + W_B, 3 * W_A + 2 * W_B]
    qa, ka, va, qb, kb, vb = jnp.split(proj, cuts, axis=-1)
    oa = chunked_rel_attention(qa.reshape(b, s, H_A, HEAD_DIM), ka.reshape(b, s, H_A, HEAD_DIM),
                               va.reshape(b, s, H_A, HEAD_DIM), rel_bias)
    ob = stick_breaking_attention(qb.reshape(b, s, H_B, HEAD_DIM), kb.reshape(b, s, H_B, HEAD_DIM),
                                  vb.reshape(b, s, H_B, HEAD_DIM))
    mixed = jnp.concatenate([rms_norm(oa) * g_a, rms_norm(ob) * g_b], axis=-1) @ w_out
    x = x + gate_mix[:, None, :] * mixed

    h = modulate(rms_norm(x), shift_ffn, scale_ffn)
    gate, val = jnp.split(causal_depthwise_conv(h @ w_up, conv_w, conv_b), 2, axis=-1)
    x = x + gate_ffn[:, None, :] * ((jax.nn.silu(gate) * val) @ w_down)
    return x


def _fwd_setup_inputs(seed: int = 0) -> dict:
    key = jax.random.key(seed)
    ks = jax.random.split(key, 14)
    f32 = jnp.float32
    nrm = lambda k, shape: jax.random.normal(k, shape, dtype=f32)
    return {
        "x": nrm(ks[0], (BATCH, SEQ, D_MODEL)),
        "c": nrm(ks[1], (BATCH, D_MODEL)),
        "w_ada": nrm(ks[2], (DEPTH, D_MODEL, N_MOD * D_MODEL)) * D_MODEL ** -0.5,
        "b_ada": nrm(ks[3], (DEPTH, N_MOD * D_MODEL)) * 0.02,
        "w_in": nrm(ks[4], (DEPTH, D_MODEL, W_IN)) * D_MODEL ** -0.5,
        "rel_bias": nrm(ks[5], (DEPTH, H_A, 2 * REL_CLIP + 1)) * 0.5,
        "g_a": 1.0 + 0.1 * nrm(ks[6], (DEPTH, W_A)),
        "g_b": 1.0 + 0.1 * nrm(ks[7], (DEPTH, W_B)),
        "w_out": nrm(ks[8], (DEPTH, W_MIX, D_MODEL)) * W_MIX ** -0.5,
        "w_up": nrm(ks[9], (DEPTH, D_MODEL, 2 * D_FF)) * D_MODEL ** -0.5,
        "conv_w": nrm(ks[10], (DEPTH, CONV_WIDTH, 2 * D_FF)) * CONV_WIDTH ** -0.5,
        "conv_b": nrm(ks[11], (DEPTH, 2 * D_FF)) * 0.02,
        "w_down": nrm(ks[12], (DEPTH, D_FF, D_MODEL)) * D_FF ** -0.5,
        "final_g": 1.0 + 0.1 * nrm(ks[13], (D_MODEL,)),
    }


def _fwd_reference(x, c, w_ada, b_ada, w_in, rel_bias, g_a, g_b, w_out, w_up, conv_w,
              conv_b, w_down, final_g):
    c_act = jax.nn.silu(c)
    for l in range(DEPTH):
        x = hybrid_layer(x, c_act, w_ada[l], b_ada[l], w_in[l], rel_bias[l], g_a[l], g_b[l],
                         w_out[l], w_up[l], conv_w[l], conv_b[l], w_down[l])
    return rms_norm(x) * final_g


import jax as _jax
import jax.numpy as _jnp

TWIN_FORMAT = 'train_step'
FWD_PARAMS = ['x', 'c', 'w_ada', 'b_ada', 'w_in', 'rel_bias', 'g_a', 'g_b', 'w_out', 'w_up', 'conv_w', 'conv_b', 'w_down', 'final_g']
TWIN_WEIGHTS = ['w_ada', 'b_ada', 'w_in', 'rel_bias', 'g_a', 'g_b', 'w_out', 'w_up', 'conv_w', 'conv_b', 'w_down', 'final_g']
TWIN_DIFF_INPUT = 'x'
TWIN_INPUTS = ['x', 'c', 'w_ada', 'b_ada', 'w_in', 'rel_bias', 'g_a', 'g_b', 'w_out', 'w_up', 'conv_w', 'conv_b', 'w_down', 'final_g', 'loss_target', 'm_w_ada', 'm_b_ada', 'm_w_in', 'm_rel_bias', 'm_g_a', 'm_g_b', 'm_w_out', 'm_w_up', 'm_conv_w', 'm_conv_b', 'm_w_down', 'm_final_g', 'v_w_ada', 'v_b_ada', 'v_w_in', 'v_rel_bias', 'v_g_a', 'v_g_b', 'v_w_out', 'v_w_up', 'v_conv_w', 'v_conv_b', 'v_w_down', 'v_final_g']
TWIN_OUTPUTS = ['loss', 'grad_x', 'grad_w_ada', 'grad_b_ada', 'grad_w_in', 'grad_rel_bias', 'grad_g_a', 'grad_g_b', 'grad_w_out', 'grad_w_up', 'grad_conv_w', 'grad_conv_b', 'grad_w_down', 'grad_final_g', 'delta_w_ada', 'delta_b_ada', 'delta_w_in', 'delta_rel_bias', 'delta_g_a', 'delta_g_b', 'delta_w_out', 'delta_w_up', 'delta_conv_w', 'delta_conv_b', 'delta_w_down', 'delta_final_g', 'new_m_w_ada', 'new_m_b_ada', 'new_m_w_in', 'new_m_rel_bias', 'new_m_g_a', 'new_m_g_b', 'new_m_w_out', 'new_m_w_up', 'new_m_conv_w', 'new_m_conv_b', 'new_m_w_down', 'new_m_final_g', 'new_v_w_ada', 'new_v_b_ada', 'new_v_w_in', 'new_v_rel_bias', 'new_v_g_a', 'new_v_g_b', 'new_v_w_out', 'new_v_w_up', 'new_v_conv_w', 'new_v_conv_b', 'new_v_w_down', 'new_v_final_g']
TWIN_LEAF_KINDS = {'loss': 'loss', 'grad_x': 'grad_x', 'grad_w_ada': 'grad_w', 'grad_b_ada': 'grad_w', 'grad_w_in': 'grad_w', 'grad_rel_bias': 'grad_w', 'grad_g_a': 'grad_w', 'grad_g_b': 'grad_w', 'grad_w_out': 'grad_w', 'grad_w_up': 'grad_w', 'grad_conv_w': 'grad_w', 'grad_conv_b': 'grad_w', 'grad_w_down': 'grad_w', 'grad_final_g': 'grad_w', 'delta_w_ada': 'delta_w', 'delta_b_ada': 'delta_w', 'delta_w_in': 'delta_w', 'delta_rel_bias': 'delta_w', 'delta_g_a': 'delta_w', 'delta_g_b': 'delta_w', 'delta_w_out': 'delta_w', 'delta_w_up': 'delta_w', 'delta_conv_w': 'delta_w', 'delta_conv_b': 'delta_w', 'delta_w_down': 'delta_w', 'delta_final_g': 'delta_w', 'new_m_w_ada': 'new_m', 'new_m_b_ada': 'new_m', 'new_m_w_in': 'new_m', 'new_m_rel_bias': 'new_m', 'new_m_g_a': 'new_m', 'new_m_g_b': 'new_m', 'new_m_w_out': 'new_m', 'new_m_w_up': 'new_m', 'new_m_conv_w': 'new_m', 'new_m_conv_b': 'new_m', 'new_m_w_down': 'new_m', 'new_m_final_g': 'new_m', 'new_v_w_ada': 'new_v', 'new_v_b_ada': 'new_v', 'new_v_w_in': 'new_v', 'new_v_rel_bias': 'new_v', 'new_v_g_a': 'new_v', 'new_v_g_b': 'new_v', 'new_v_w_out': 'new_v', 'new_v_w_up': 'new_v', 'new_v_conv_w': 'new_v', 'new_v_conv_b': 'new_v', 'new_v_w_down': 'new_v', 'new_v_final_g': 'new_v'}


def _forward(args):
    return _fwd_reference(*[args[k] for k in FWD_PARAMS])


def _output_shape():
    out = _jax.eval_shape(lambda: _forward(_fwd_setup_inputs(0)))
    return out.shape, out.dtype

N_MICROBATCH = 1
ADAM_LR = 0.001
ADAM_B1 = 0.9
ADAM_B2 = 0.999
ADAM_EPS = 1e-08
ADAM_WD = 0.01
ADAM_STEP = 10
PER_EXAMPLE_BATCH_AXIS = {'x': 0, 'c': 0, 'loss_target': 0}
SHARED_INPUTS = []
_WEIGHT_DTYPES = {'w_ada': _jnp.float32, 'b_ada': _jnp.float32, 'w_in': _jnp.float32, 'rel_bias': _jnp.float32, 'g_a': _jnp.float32, 'g_b': _jnp.float32, 'w_out': _jnp.float32, 'w_up': _jnp.float32, 'conv_w': _jnp.float32, 'conv_b': _jnp.float32, 'w_down': _jnp.float32, 'final_g': _jnp.float32}
MOMENT_SCALE = {'w_ada': 4.937254e-01, 'b_ada': 8.592082e-01, 'w_in': 3.753227e-01, 'rel_bias': 1.550148e-02, 'g_a': 7.742828e-01, 'g_b': 5.030268e-01, 'w_out': 7.000143e-01, 'w_up': 1.513246e-01, 'conv_w': 1.650612e-01, 'conv_b': 1.960036e-01, 'w_down': 2.737713e-01, 'final_g': 3.386836e+01}


def _to_microbatches(a, axis):
    t = _jnp.moveaxis(a, axis, 0)
    t = t.reshape((N_MICROBATCH, t.shape[0] // N_MICROBATCH) + t.shape[1:])
    return _jnp.moveaxis(t, 1, axis + 1)


def setup_inputs(seed: int = 0) -> dict:
    inp = _fwd_setup_inputs(seed)
    key = _jax.random.fold_in(_jax.random.key(seed), 7919)
    shape, _ = _output_shape()
    out = dict(inp)
    out["loss_target"] = _jax.random.normal(_jax.random.fold_in(key, 0), shape, _jnp.float32)
    for i, name in enumerate(TWIN_WEIGHTS):
        w = inp[name].astype(_jnp.float32)
        if MOMENT_SCALE is None:
            s = _jnp.sqrt(_jnp.mean(_jnp.square(w)) + 1e-30)
        else:
            s = MOMENT_SCALE[name]
        km, kv = _jax.random.split(_jax.random.fold_in(key, i + 1))
        out[name] = w
        out["m_" + name] = s * _jax.random.normal(km, w.shape, _jnp.float32)
        out["v_" + name] = (s * s) * _jax.random.uniform(kv, w.shape, _jnp.float32, 0.5, 1.5)
    if N_MICROBATCH > 1:
        for name, axis in PER_EXAMPLE_BATCH_AXIS.items():
            out[name] = _to_microbatches(out[name], axis)
    return {'x': out['x'], 'c': out['c'], 'w_ada': out['w_ada'], 'b_ada': out['b_ada'], 'w_in': out['w_in'], 'rel_bias': out['rel_bias'], 'g_a': out['g_a'], 'g_b': out['g_b'], 'w_out': out['w_out'], 'w_up': out['w_up'], 'conv_w': out['conv_w'], 'conv_b': out['conv_b'], 'w_down': out['w_down'], 'final_g': out['final_g'], 'loss_target': out['loss_target'], 'm_w_ada': out['m_w_ada'], 'm_b_ada': out['m_b_ada'], 'm_w_in': out['m_w_in'], 'm_rel_bias': out['m_rel_bias'], 'm_g_a': out['m_g_a'], 'm_g_b': out['m_g_b'], 'm_w_out': out['m_w_out'], 'm_w_up': out['m_w_up'], 'm_conv_w': out['m_conv_w'], 'm_conv_b': out['m_conv_b'], 'm_w_down': out['m_w_down'], 'm_final_g': out['m_final_g'], 'v_w_ada': out['v_w_ada'], 'v_b_ada': out['v_b_ada'], 'v_w_in': out['v_w_in'], 'v_rel_bias': out['v_rel_bias'], 'v_g_a': out['v_g_a'], 'v_g_b': out['v_g_b'], 'v_w_out': out['v_w_out'], 'v_w_up': out['v_w_up'], 'v_conv_w': out['v_conv_w'], 'v_conv_b': out['v_conv_b'], 'v_w_down': out['v_w_down'], 'v_final_g': out['v_final_g']}


def _loss(weights, diff, rest, loss_target):
    with _jax.named_scope("forward"):
        args = {**rest, TWIN_DIFF_INPUT: diff, **{k: w.astype(_WEIGHT_DTYPES[k]) for k, w in weights.items()}}
        y = _forward(args)
    with _jax.named_scope("loss_head"):
        err = _jnp.square(y.astype(_jnp.float32) - loss_target)
        return 0.5 * _jnp.sum(_jnp.mean(err, axis=-1)) if err.ndim else 0.5 * err


def _adamw(w, g, m, v):
    m = ADAM_B1 * m + (1.0 - ADAM_B1) * g
    v = ADAM_B2 * v + (1.0 - ADAM_B2) * _jnp.square(g)
    m_hat = m / (1.0 - ADAM_B1 ** ADAM_STEP)
    v_hat = v / (1.0 - ADAM_B2 ** ADAM_STEP)
    delta = -ADAM_LR * (m_hat / (_jnp.sqrt(v_hat) + ADAM_EPS) + ADAM_WD * w)
    return delta, m, v


def reference(x, c, w_ada, b_ada, w_in, rel_bias, g_a, g_b, w_out, w_up, conv_w, conv_b, w_down, final_g, loss_target, m_w_ada, m_b_ada, m_w_in, m_rel_bias, m_g_a, m_g_b, m_w_out, m_w_up, m_conv_w, m_conv_b, m_w_down, m_final_g, v_w_ada, v_b_ada, v_w_in, v_rel_bias, v_g_a, v_g_b, v_w_out, v_w_up, v_conv_w, v_conv_b, v_w_down, v_final_g):
    given = dict(x=x, c=c, w_ada=w_ada, b_ada=b_ada, w_in=w_in, rel_bias=rel_bias, g_a=g_a, g_b=g_b, w_out=w_out, w_up=w_up, conv_w=conv_w, conv_b=conv_b, w_down=w_down, final_g=final_g, loss_target=loss_target, m_w_ada=m_w_ada, m_b_ada=m_b_ada, m_w_in=m_w_in, m_rel_bias=m_rel_bias, m_g_a=m_g_a, m_g_b=m_g_b, m_w_out=m_w_out, m_w_up=m_w_up, m_conv_w=m_conv_w, m_conv_b=m_conv_b, m_w_down=m_w_down, m_final_g=m_final_g, v_w_ada=v_w_ada, v_b_ada=v_b_ada, v_w_in=v_w_in, v_rel_bias=v_rel_bias, v_g_a=v_g_a, v_g_b=v_g_b, v_w_out=v_w_out, v_w_up=v_w_up, v_conv_w=v_conv_w, v_conv_b=v_conv_b, v_w_down=v_w_down, v_final_g=v_final_g)
    weights = {n: given[n] for n in TWIN_WEIGHTS}
    shared = {n: given[n] for n in SHARED_INPUTS}
    per_example = {n: given[n] for n in ['x', 'c']}
    grad_fn = _jax.value_and_grad(_loss, argnums=(0, 1))

    def one_microbatch(ex, loss_target):
        ex = dict(ex)
        diff = ex.pop(TWIN_DIFF_INPUT)
        return grad_fn(weights, diff, {**shared, **ex}, loss_target)

    if N_MICROBATCH == 1:
        loss, (grad_w, grad_x) = one_microbatch(per_example, given["loss_target"])
    else:
        def body(carry, xs):
            loss_sum, grad_sum = carry
            l_k, (gw_k, gx_k) = one_microbatch(xs[0], xs[1])
            with _jax.named_scope("update"):
                return (loss_sum + l_k, _jax.tree.map(_jnp.add, grad_sum, gw_k)), gx_k

        init = (_jnp.zeros((), _jnp.float32), _jax.tree.map(_jnp.zeros_like, weights))
        (loss, grad_w), grad_x = _jax.lax.scan(body, init, (per_example, given["loss_target"]))
    with _jax.named_scope("update"):
        delta_w, new_m, new_v = {}, {}, {}
        for n in TWIN_WEIGHTS:
            delta_w[n], new_m[n], new_v[n] = _adamw(weights[n], grad_w[n], given["m_" + n], given["v_" + n])
    return (loss, grad_x, *[grad_w[n] for n in TWIN_WEIGHTS], *[delta_w[n] for n in TWIN_WEIGHTS],
            *[new_m[n] for n in TWIN_WEIGHTS], *[new_v[n] for n in TWIN_WEIGHTS])
```

```python
import functools

import jax
import jax.numpy as jnp
from jax import lax
from jax.experimental import pallas as pl
from jax.experimental.pallas import tpu as pltpu

F32 = jnp.float32
BF16 = jnp.bfloat16
MESH = pl.DeviceIdType.MESH
ANY = pl.BlockSpec(memory_space=pl.ANY)
VMEM_FULL = pl.BlockSpec(memory_space=pltpu.VMEM)

HEAD_DIM = 64
N_HEADS = 8
W_GRP = N_HEADS * HEAD_DIM
CHUNK = 64
N_PREV = 8
BAND = (N_PREV + 1) * CHUNK
PAD = N_PREV * CHUNK
REL_CLIP = 128
N_REL = 2 * REL_CLIP + 1
EPS = 1e-6
N_CHIPS = 4
N_DEV = 8
LANES = 128
V7X_VMEM_LIMIT = 56 * 1024 * 1024

ADAM_LR = 0.001
ADAM_B1 = 0.9
ADAM_B2 = 0.999
ADAM_EPS = 1e-08
ADAM_WD = 0.01
ADAM_STEP = 10


def _params(**kw):
    return pltpu.CompilerParams(vmem_limit_bytes=V7X_VMEM_LIMIT, **kw)


def _pick(dim, pref, mult=LANES):
    t = (min(pref, dim) // mult) * mult
    while t >= mult:
        if dim % t == 0:
            return t
        t -= mult
    return dim


def _my_place():
    return lax.axis_index("x"), lax.axis_index("y"), lax.axis_index("c")


def _flip(v, bit):
    return 1 - v if bit else v


def _matmul(a, b, *, form, out_dtype, tm, tn, tk, name, shard_cols=None):
    if form == "nn":
        (m, k), (_, n) = a.shape, b.shape
        a_spec = pl.BlockSpec((tm, tk), lambda i, j, kk: (i, kk))
        b_spec = pl.BlockSpec((tk, tn), lambda i, j, kk: (kk, j))
        dims = (((1,), (0,)), ((), ()))
    elif form == "nt":
        (m, k), (n, _) = a.shape, b.shape
        a_spec = pl.BlockSpec((tm, tk), lambda i, j, kk: (i, kk))
        b_spec = pl.BlockSpec((tn, tk), lambda i, j, kk: (j, kk))
        dims = (((1,), (1,)), ((), ()))
    else:
        (k, m), (_, n) = a.shape, b.shape
        a_spec = pl.BlockSpec((tk, tm), lambda i, j, kk: (kk, i))
        b_spec = pl.BlockSpec((tk, tn), lambda i, j, kk: (kk, j))
        dims = (((0,), (0,)), ((), ()))
    assert m % tm == 0 and n % tn == 0 and k % tk == 0, (name, m, n, k, tm, tn, tk)
    nk = k // tk
    if shard_cols is None:
        out_shape = jax.ShapeDtypeStruct((m, n), out_dtype)
        o_spec = pl.BlockSpec((tm, tn), lambda i, j, kk: (i, j))
    else:
        per = shard_cols // tn
        assert shard_cols % tn == 0
        out_shape = jax.ShapeDtypeStruct((n // shard_cols, m, shard_cols), out_dtype)
        o_spec = pl.BlockSpec((None, tm, tn), lambda i, j, kk: (j // per, i, j % per))

    def body(a_ref, b_ref, o_ref, acc_ref):
        kk = pl.program_id(2)
        part = lax.dot_general(a_ref[...], b_ref[...], dims, preferred_element_type=F32)

        @pl.when(kk == 0)
        def _():
            acc_ref[...] = part

        @pl.when(kk > 0)
        def _():
            acc_ref[...] += part

        @pl.when(kk == nk - 1)
        def _():
            o_ref[...] = acc_ref[...].astype(out_dtype)

    return pl.pallas_call(
        body, name=name, out_shape=out_shape, grid=(m // tm, n // tn, nk),
        in_specs=[a_spec, b_spec], out_specs=o_spec,
        scratch_shapes=[pltpu.VMEM((tm, tn), F32)], compiler_params=_params(),
    )(a, b)


def _row_spec(tr, d):
    return pl.BlockSpec((tr, d), lambda i: (i, 0))


def _vec_spec(d):
    return pl.BlockSpec((1, d), lambda i: (0, 0))


def _rms(xf):
    r = lax.rsqrt(jnp.mean(xf * xf, axis=-1, keepdims=True) + EPS)
    return xf * r, r


def _norm_mod(x, scale, shift, name):
    s, d = x.shape
    tr = _pick(s, 512, 8)

    def body(x_ref, sc_ref, sh_ref, o_ref):
        n, _ = _rms(x_ref[...])
        o_ref[...] = (n * (1.0 + sc_ref[...]) + sh_ref[...]).astype(BF16)

    return pl.pallas_call(
        body, name=name, out_shape=jax.ShapeDtypeStruct((s, d), BF16), grid=(s // tr,),
        in_specs=[_row_spec(tr, d), _vec_spec(d), _vec_spec(d)], out_specs=_row_spec(tr, d),
        compiler_params=_params(),
    )(x, scale, shift)


def _out_norm(oa, ob, g_a, g_b, name):
    s, w = oa.shape
    tr = _pick(s, 512, 8)

    def body(oa_ref, ob_ref, ga_ref, gb_ref, o_ref):
        na, _ = _rms(oa_ref[...])
        nb, _ = _rms(ob_ref[...])
        o_ref[:, :w] = (na * ga_ref[...]).astype(BF16)
        o_ref[:, w:] = (nb * gb_ref[...]).astype(BF16)

    return pl.pallas_call(
        body, name=name, out_shape=jax.ShapeDtypeStruct((s, 2 * w), BF16), grid=(s // tr,),
        in_specs=[_row_spec(tr, w), _row_spec(tr, w), _vec_spec(w), _vec_spec(w)],
        out_specs=_row_spec(tr, 2 * w), compiler_params=_params(),
    )(oa, ob, g_a, g_b)


def _residual(x, gate, m, name):
    s, d = x.shape
    tr = _pick(s, 512, 8)

    def body(x_ref, g_ref, m_ref, o_ref):
        o_ref[...] = x_ref[...] + g_ref[...] * m_ref[...]

    return pl.pallas_call(
        body, name=name, out_shape=jax.ShapeDtypeStruct((s, d), F32), grid=(s // tr,),
        in_specs=[_row_spec(tr, d), _vec_spec(d), _row_spec(tr, d)], out_specs=_row_spec(tr, d),
        compiler_params=_params(),
    )(x, gate, m)


def _shift_down(u, k):
    rows = lax.broadcasted_iota(jnp.int32, u.shape, 0)
    return jnp.where(rows >= k, pltpu.roll(u, k, 0), 0.0)


def _shift_up(u, k):
    s = u.shape[0]
    rows = lax.broadcasted_iota(jnp.int32, u.shape, 0)
    return jnp.where(rows < s - k, pltpu.roll(u, s - k, 0), 0.0)


def _conv(u, w_ref, b_ref):
    return w_ref[0:1, :] * _shift_down(u, 2) + w_ref[1:2, :] * _shift_down(u, 1) + w_ref[2:3, :] * u + b_ref[...]


def _conv_glu(u, conv_w, conv_b, name):
    s, f2 = u.shape
    f = f2 // 2
    tc = LANES
    nb = f // tc

    def body(ug_ref, uv_ref, wg_ref, wv_ref, bg_ref, bv_ref, o_ref):
        g = _conv(ug_ref[...], wg_ref, bg_ref)
        v = _conv(uv_ref[...], wv_ref, bv_ref)
        o_ref[...] = (g * jax.nn.sigmoid(g) * v).astype(BF16)

    col = lambda off: pl.BlockSpec((s, tc), lambda j: (0, j + off))
    wcol = lambda off: pl.BlockSpec((3, tc), lambda j: (0, j + off))
    bcol = lambda off: pl.BlockSpec((1, tc), lambda j: (0, j + off))
    return pl.pallas_call(
        body, name=name, out_shape=jax.ShapeDtypeStruct((s, f), BF16), grid=(nb,),
        in_specs=[col(0), col(nb), wcol(0), wcol(nb), bcol(0), bcol(nb)], out_specs=col(0),
        compiler_params=_params(),
    )(u, u, conv_w, conv_w, conv_b, conv_b)


def _conv_glu_bwd(u, da, conv_w, conv_b, name):
    s, f2 = u.shape
    f = f2 // 2
    tc = LANES
    nb = f // tc

    def body(ug_ref, uv_ref, da_ref, wg_ref, wv_ref, bg_ref, bv_ref, du_ref, dw_ref, db_ref):
        da_ = da_ref[...]
        ug, uv = ug_ref[...], uv_ref[...]
        g = _conv(ug, wg_ref, bg_ref)
        v = _conv(uv, wv_ref, bv_ref)
        sg = jax.nn.sigmoid(g)
        dg = da_ * v * (sg * (1.0 + g * (1.0 - sg)))
        dv = da_ * (g * sg)
        for h, (dy, uu, w_ref) in enumerate(((dg, ug, wg_ref), (dv, uv, wv_ref))):
            du = w_ref[2:3, :] * dy + w_ref[1:2, :] * _shift_up(dy, 1) + w_ref[0:1, :] * _shift_up(dy, 2)
            du_ref[h] = du.astype(BF16)
            dw_ref[h, 0:1, :] = jnp.sum(dy * _shift_down(uu, 2), axis=0, keepdims=True)
            dw_ref[h, 1:2, :] = jnp.sum(dy * _shift_down(uu, 1), axis=0, keepdims=True)
            dw_ref[h, 2:3, :] = jnp.sum(dy * uu, axis=0, keepdims=True)
            db_ref[h] = jnp.sum(dy, axis=0, keepdims=True)

    col = lambda off: pl.BlockSpec((s, tc), lambda j: (0, j + off))
    wcol = lambda off: pl.BlockSpec((3, tc), lambda j: (0, j + off))
    bcol = lambda off: pl.BlockSpec((1, tc), lambda j: (0, j + off))
    return pl.pallas_call(
        body, name=name, grid=(nb,),
        out_shape=(jax.ShapeDtypeStruct((2, s, f), BF16), jax.ShapeDtypeStruct((2, 3, f), F32),
                   jax.ShapeDtypeStruct((2, 1, f), F32)),
        in_specs=[col(0), col(nb), col(0), wcol(0), wcol(nb), bcol(0), bcol(nb)],
        out_specs=(pl.BlockSpec((2, s, tc), lambda j: (0, 0, j)), pl.BlockSpec((2, 3, tc), lambda j: (0, 0, j)),
                   pl.BlockSpec((2, 1, tc), lambda j: (0, 0, j))),
        compiler_params=_params(),
    )(u, u, da, conv_w, conv_w, conv_b, conv_b)


def _accumulate(ref, val):
    @pl.when(pl.program_id(0) == 0)
    def _():
        ref[...] = val

    @pl.when(pl.program_id(0) > 0)
    def _():
        ref[...] += val


def _rms_bwd(n, r, dn):
    return r * (dn - n * jnp.mean(dn * n, axis=-1, keepdims=True))


def _loss_head(x, final_g, target, name):
    s, d = x.shape
    tr = _pick(s, 512, 8)

    def body(x_ref, g_ref, t_ref, loss_ref, dx_ref, dg_ref):
        n, r = _rms(x_ref[...])
        diff = n * g_ref[...] - t_ref[...]
        part = 0.5 * jnp.sum(jnp.sum(diff * diff, axis=1, keepdims=True), axis=0, keepdims=True) / d
        _accumulate(loss_ref, part)
        dy = diff / d
        _accumulate(dg_ref, jnp.sum(dy * n, axis=0, keepdims=True))
        dx_ref[...] = _rms_bwd(n, r, dy * g_ref[...])

    return pl.pallas_call(
        body, name=name, grid=(s // tr,),
        out_shape=(jax.ShapeDtypeStruct((1, 1), F32), jax.ShapeDtypeStruct((s, d), F32), jax.ShapeDtypeStruct((1, d), F32)),
        in_specs=[_row_spec(tr, d), _vec_spec(d), _row_spec(tr, d)],
        out_specs=(pl.BlockSpec((1, 1), lambda i: (0, 0)), _row_spec(tr, d), _vec_spec(d)),
        compiler_params=_params(),
    )(x, final_g, target)


def _gate_bwd(dx, m, gate, name):
    s, d = dx.shape
    tr = _pick(s, 512, 8)

    def body(dx_ref, m_ref, g_ref, dm_ref, dg_ref):
        dxv = dx_ref[...]
        dm_ref[...] = (dxv * g_ref[...]).astype(BF16)
        _accumulate(dg_ref, jnp.sum(dxv * m_ref[...], axis=0, keepdims=True))

    return pl.pallas_call(
        body, name=name, grid=(s // tr,),
        out_shape=(jax.ShapeDtypeStruct((s, d), BF16), jax.ShapeDtypeStruct((1, d), F32)),
        in_specs=[_row_spec(tr, d), _row_spec(tr, d), _vec_spec(d)], out_specs=(_row_spec(tr, d), _vec_spec(d)),
        compiler_params=_params(),
    )(dx, m, gate)


def _norm_mod_bwd(x, dh, dres, scale, name):
    s, d = x.shape
    tr = _pick(s, 512, 8)

    def body(x_ref, dh_ref, dr_ref, sc_ref, dx_ref, dsc_ref, dsh_ref):
        n, r = _rms(x_ref[...])
        dh_ = dh_ref[...]
        _accumulate(dsc_ref, jnp.sum(dh_ * n, axis=0, keepdims=True))
        _accumulate(dsh_ref, jnp.sum(dh_, axis=0, keepdims=True))
        dx_ref[...] = dr_ref[...] + _rms_bwd(n, r, dh_ * (1.0 + sc_ref[...]))

    return pl.pallas_call(
        body, name=name, grid=(s // tr,),
        out_shape=(jax.ShapeDtypeStruct((s, d), F32), jax.ShapeDtypeStruct((1, d), F32), jax.ShapeDtypeStruct((1, d), F32)),
        in_specs=[_row_spec(tr, d), _row_spec(tr, d), _row_spec(tr, d), _vec_spec(d)],
        out_specs=(_row_spec(tr, d), _vec_spec(d), _vec_spec(d)), compiler_params=_params(),
    )(x, dh, dres, scale)


def _out_norm_bwd(oa, ob, dcat, g_a, g_b, name):
    s, w = oa.shape
    tr = _pick(s, 512, 8)

    def body(oa_ref, ob_ref, dc_ref, ga_ref, gb_ref, doa_ref, dob_ref, dga_ref, dgb_ref):
        for o_ref, g_ref, do_ref, dg_ref, lo in ((oa_ref, ga_ref, doa_ref, dga_ref, 0), (ob_ref, gb_ref, dob_ref, dgb_ref, w)):
            n, r = _rms(o_ref[...])
            dc = dc_ref[:, lo:lo + w]
            _accumulate(dg_ref, jnp.sum(dc * n, axis=0, keepdims=True))
            do_ref[...] = _rms_bwd(n, r, dc * g_ref[...])

    return pl.pallas_call(
        body, name=name, grid=(s // tr,),
        out_shape=(jax.ShapeDtypeStruct((s, w), F32), jax.ShapeDtypeStruct((s, w), F32),
                   jax.ShapeDtypeStruct((1, w), F32), jax.ShapeDtypeStruct((1, w), F32)),
        in_specs=[_row_spec(tr, w), _row_spec(tr, w), _row_spec(tr, 2 * w), _vec_spec(w), _vec_spec(w)],
        out_specs=(_row_spec(tr, w), _row_spec(tr, w), _vec_spec(w), _vec_spec(w)), compiler_params=_params(),
    )(oa, ob, dcat, g_a, g_b)


def _head_masks():
    lane = lax.broadcasted_iota(jnp.int32, (1, LANES), 1)
    return lane < HEAD_DIM, lane >= HEAD_DIM


def _nt(a, b):
    return lax.dot_general(a, b, (((1,), (1,)), ((), ())), preferred_element_type=F32)


def _tn(a, b):
    return lax.dot_general(a, b, (((0,), (0,)), ((), ())), preferred_element_type=F32)


def _nn(a, b):
    return jnp.dot(a, b, preferred_element_type=F32)


def _only(mask, v):
    return jnp.where(mask, v, jnp.zeros_like(v))


def _fill_padded(dst_ref, src_ref):
    dst_ref[0:PAD, :] = jnp.zeros((PAD, LANES), dst_ref.dtype)
    dst_ref[PAD:, :] = src_ref[...]


def _chunk_probs(qh, kb, bias, chunk):
    s = _nt(qh, kb) * (HEAD_DIM ** -0.5) + bias
    pos = lax.broadcasted_iota(jnp.int32, (1, BAND), 1)
    s = jnp.where(pos >= (N_PREV - chunk) * CHUNK, s, -1e30)
    e = jnp.exp(s - jnp.max(s, axis=1, keepdims=True))
    return e / jnp.sum(e, axis=1, keepdims=True)


def _attn_a_fwd(proj, band_bias, name):
    s = proj.shape[0]
    cq = 4
    tq = cq * CHUNK
    npair = N_HEADS // 2
    kcol, vcol = W_GRP // LANES, 2 * W_GRP // LANES

    def body(q_ref, k_ref, v_ref, b_ref, o_ref, kpad, vpad):
        i = pl.program_id(1)
        masks = _head_masks()

        @pl.when(i == 0)
        def _():
            _fill_padded(kpad, k_ref)
            _fill_padded(vpad, v_ref)

        for cc in range(cq):
            chunk = i * cq + cc
            start = pl.multiple_of(chunk * CHUNK, CHUNK)
            kb = kpad[pl.ds(start, BAND), :]
            vb = vpad[pl.ds(start, BAND), :]
            qc = q_ref[cc * CHUNK:(cc + 1) * CHUNK, :]
            out = jnp.zeros((CHUNK, LANES), F32)
            for h in range(2):
                p = _chunk_probs(_only(masks[h], qc), kb, b_ref[h], chunk)
                out = out + _nn(p.astype(BF16), _only(masks[h], vb))
            o_ref[cc * CHUNK:(cc + 1) * CHUNK, :] = out

    return pl.pallas_call(
        body, name=name, out_shape=jax.ShapeDtypeStruct((s, W_GRP), F32), grid=(npair, s // tq),
        in_specs=[pl.BlockSpec((tq, LANES), lambda p, i: (i, p)),
                  pl.BlockSpec((s, LANES), lambda p, i: (0, kcol + p)),
                  pl.BlockSpec((s, LANES), lambda p, i: (0, vcol + p)),
                  pl.BlockSpec((2, CHUNK, BAND), lambda p, i: (p, 0, 0))],
        out_specs=pl.BlockSpec((tq, LANES), lambda p, i: (i, p)),
        scratch_shapes=[pltpu.VMEM((s + PAD, LANES), BF16), pltpu.VMEM((s + PAD, LANES), BF16)],
        compiler_params=_params(),
    )(proj, proj, proj, band_bias)


def _attn_a_bwd(proj, band_bias, doa, name):
    s = proj.shape[0]
    cq = 4
    tq = cq * CHUNK
    nq = s // tq
    npair = N_HEADS // 2
    kcol, vcol = W_GRP // LANES, 2 * W_GRP // LANES
    scale = HEAD_DIM ** -0.5

    def body(q_ref, k_ref, v_ref, b_ref, do_ref, dq_ref, dk_ref, dv_ref, db_ref, kpad, vpad, dkpad, dvpad):
        i = pl.program_id(1)
        masks = _head_masks()

        @pl.when(i == 0)
        def _():
            _fill_padded(kpad, k_ref)
            _fill_padded(vpad, v_ref)
            dkpad[...] = jnp.zeros_like(dkpad)
            dvpad[...] = jnp.zeros_like(dvpad)
            db_ref[...] = jnp.zeros_like(db_ref)

        for cc in range(cq):
            chunk = i * cq + cc
            start = pl.multiple_of(chunk * CHUNK, CHUNK)
            kb = kpad[pl.ds(start, BAND), :]
            vb = vpad[pl.ds(start, BAND), :]
            qc = q_ref[cc * CHUNK:(cc + 1) * CHUNK, :]
            doc = do_ref[cc * CHUNK:(cc + 1) * CHUNK, :].astype(BF16)
            dq = jnp.zeros((CHUNK, LANES), F32)
            dk = jnp.zeros((BAND, LANES), F32)
            dv = jnp.zeros((BAND, LANES), F32)
            for h in range(2):
                qh = _only(masks[h], qc)
                doh = _only(masks[h], doc)
                p = _chunk_probs(qh, kb, b_ref[h], chunk)
                dp = _nt(doh, vb)
                ds = p * (dp - jnp.sum(p * dp, axis=1, keepdims=True))
                db_ref[h] += ds
                dsb = (ds * scale).astype(BF16)
                dq = dq + _nn(dsb, _only(masks[h], kb))
                dk = dk + _tn(dsb, qh)
                dv = dv + _tn(p.astype(BF16), doh)
            dq_ref[cc * CHUNK:(cc + 1) * CHUNK, :] = dq
            dkpad[pl.ds(start, BAND), :] += dk
            dvpad[pl.ds(start, BAND), :] += dv

        @pl.when(i == nq - 1)
        def _():
            dk_ref[...] = dkpad[PAD:, :]
            dv_ref[...] = dvpad[PAD:, :]

    blk = pl.BlockSpec((tq, LANES), lambda p, i: (i, p))
    whole = pl.BlockSpec((s, LANES), lambda p, i: (0, p))
    bias_spec = pl.BlockSpec((2, CHUNK, BAND), lambda p, i: (p, 0, 0))
    return pl.pallas_call(
        body, name=name, grid=(npair, nq),
        out_shape=(jax.ShapeDtypeStruct((s, W_GRP), F32),) * 3 + (jax.ShapeDtypeStruct((N_HEADS, CHUNK, BAND), F32),),
        in_specs=[blk, pl.BlockSpec((s, LANES), lambda p, i: (0, kcol + p)),
                  pl.BlockSpec((s, LANES), lambda p, i: (0, vcol + p)), bias_spec, blk],
        out_specs=(blk, whole, whole, bias_spec),
        scratch_shapes=[pltpu.VMEM((s + PAD, LANES), BF16), pltpu.VMEM((s + PAD, LANES), BF16),
                        pltpu.VMEM((s + PAD, LANES), F32), pltpu.VMEM((s + PAD, LANES), F32)],
        compiler_params=_params(),
    )(proj, proj, proj, band_bias, doa)


def _split3(v):
    hi = v.astype(BF16)
    r1 = v - hi.astype(F32)
    mid = r1.astype(BF16)
    lo = (r1 - mid.astype(F32)).astype(BF16)
    return hi, mid, lo


def _rel_bias_grad(dband_t, name):
    width = 3 * LANES

    def body(t_ref, o_ref):
        pos = lax.broadcasted_iota(jnp.int32, (BAND, width), 0)
        col = lax.broadcasted_iota(jnp.int32, (BAND, width), 1)
        acc = jnp.zeros((N_HEADS, width), F32)
        for q in range(CHUNK):
            idx = jnp.minimum(PAD + q - pos, REL_CLIP) + REL_CLIP
            onehot = (col == idx).astype(BF16)
            for part in _split3(t_ref[q]):
                acc = acc + _nn(part, onehot)
        o_ref[...] = acc

    return pl.pallas_call(
        body, name=name, out_shape=jax.ShapeDtypeStruct((N_HEADS, width), F32),
        in_specs=[VMEM_FULL], out_specs=VMEM_FULL, compiler_params=_params(),
    )(dband_t)


def _split2(v):
    hi = v.astype(BF16)
    return hi, (v - hi.astype(F32)).astype(BF16)


def _sb_terms(qh, kb, valid):
    z = _nt(qh, kb) * (HEAD_DIM ** -0.5)
    e = jnp.exp(-jnp.abs(z))
    l1p = jnp.log(1.0 + e)
    lb = jnp.minimum(z, 0.0) - l1p
    lk = jnp.where(valid, lb - z, 0.0)
    return z, e, lb, lk


def _sb_fwd(proj, name):
    s = proj.shape[0]
    tq = _pick(s, 256)
    nq = s // tq
    npair = N_HEADS // 2
    qcol, kcol, vcol = 3 * W_GRP // LANES, 4 * W_GRP // LANES, 5 * W_GRP // LANES

    def body(q_ref, k_ref, v_ref, o_ref, l_ref):
        i = pl.program_id(1)
        masks = _head_masks()
        q2 = q_ref[...]
        qh = [_only(m, q2) for m in masks]
        row = lax.broadcasted_iota(jnp.int32, (tq, tq), 0)
        col = lax.broadcasted_iota(jnp.int32, (tq, tq), 1)
        lower = row > col
        after = lower.astype(BF16)

        def step(jj, carry):
            acc, tails = carry
            ks = pl.multiple_of((i - jj) * tq, tq)
            kb = k_ref[pl.ds(ks, tq), :]
            vb = v_ref[pl.ds(ks, tq), :]
            valid = jnp.logical_or(jj > 0, lower)
            new_tails = []
            for h in range(2):
                _, _, lb, lk = _sb_terms(qh[h], kb, valid)
                hi, lo = _split2(lk)
                tail = tails[h] + _nn(hi, after) + _nn(lo, after)
                a = jnp.where(valid, jnp.exp(lb + tail), 0.0)
                acc = acc + _nn(a.astype(BF16), _only(masks[h], vb))
                new_tails.append(tails[h] + jnp.sum(lk, axis=1, keepdims=True))
            return acc, tuple(new_tails)

        zero = jnp.zeros((tq, 1), F32)
        acc, tails = lax.fori_loop(0, i + 1, step, (jnp.zeros((tq, LANES), F32), (zero, zero)))
        o_ref[...] = acc
        l_ref[:, 0:1] = tails[0]
        l_ref[:, 1:2] = tails[1]

    return pl.pallas_call(
        body, name=name, grid=(npair, nq),
        out_shape=(jax.ShapeDtypeStruct((s, W_GRP), F32), jax.ShapeDtypeStruct((npair, s, 2), F32)),
        in_specs=[pl.BlockSpec((tq, LANES), lambda p, i: (i, qcol + p)),
                  pl.BlockSpec((s, LANES), lambda p, i: (0, kcol + p)),
                  pl.BlockSpec((s, LANES), lambda p, i: (0, vcol + p))],
        out_specs=(pl.BlockSpec((tq, LANES), lambda p, i: (i, p)), pl.BlockSpec((None, tq, 2), lambda p, i: (p, i, 0))),
        compiler_params=_params(),
    )(proj, proj, proj)


def _sb_bwd(proj, ltot, dob, name):
    s = proj.shape[0]
    tq = _pick(s, 256)
    nq = s // tq
    npair = N_HEADS // 2
    qcol, kcol, vcol = 3 * W_GRP // LANES, 4 * W_GRP // LANES, 5 * W_GRP // LANES
    scale = HEAD_DIM ** -0.5

    def body(q_ref, k_ref, v_ref, l_ref, do_ref, dq_ref, dk_ref, dv_ref):
        i = pl.program_id(1)
        masks = _head_masks()

        @pl.when(i == 0)
        def _():
            dk_ref[...] = jnp.zeros_like(dk_ref)
            dv_ref[...] = jnp.zeros_like(dv_ref)

        q2 = q_ref[...]
        do2 = do_ref[...]
        qh = [_only(m, q2) for m in masks]
        doh = [_only(m, do2).astype(BF16) for m in masks]
        ltots = [l_ref[:, 0:1], l_ref[:, 1:2]]
        row = lax.broadcasted_iota(jnp.int32, (tq, tq), 0)
        col = lax.broadcasted_iota(jnp.int32, (tq, tq), 1)
        lower = row > col
        upto = (row <= col).astype(BF16)
        before = (row < col).astype(BF16)

        def step(j, carry):
            dq, heads, gsums = carry
            ks = pl.multiple_of(j * tq, tq)
            kb = k_ref[pl.ds(ks, tq), :]
            vb = v_ref[pl.ds(ks, tq), :]
            valid = jnp.logical_or(j < i, lower)
            dk = jnp.zeros((tq, LANES), F32)
            dv = jnp.zeros((tq, LANES), F32)
            new_heads, new_gsums = [], []
            for h in range(2):
                z, e, lb, lk = _sb_terms(qh[h], kb, valid)
                hi, lo = _split2(lk)
                tail = ltots[h] - (heads[h] + _nn(hi, upto) + _nn(lo, upto))
                a = jnp.where(valid, jnp.exp(lb + tail), 0.0)
                g = a * _nt(doh[h], vb)
                ghi, glo = _split2(g)
                cpre = gsums[h] + _nn(ghi, before) + _nn(glo, before)
                r = 1.0 / (1.0 + e)
                sig = jnp.where(z >= 0, r, e * r)
                dz = jnp.where(valid, g * (1.0 - sig) - cpre * sig, 0.0)
                dzb = (dz * scale).astype(BF16)
                dq = dq + _nn(dzb, _only(masks[h], kb))
                dk = dk + _tn(dzb, qh[h])
                dv = dv + _tn(a.astype(BF16), doh[h])
                new_heads.append(heads[h] + jnp.sum(lk, axis=1, keepdims=True))
                new_gsums.append(gsums[h] + jnp.sum(g, axis=1, keepdims=True))
            dk_ref[pl.ds(ks, tq), :] += dk
            dv_ref[pl.ds(ks, tq), :] += dv
            return dq, tuple(new_heads), tuple(new_gsums)

        zero = jnp.zeros((tq, 1), F32)
        dq, _, _ = lax.fori_loop(0, i + 1, step, (jnp.zeros((tq, LANES), F32), (zero, zero), (zero, zero)))
        dq_ref[...] = dq

    blk = pl.BlockSpec((tq, LANES), lambda p, i: (i, p))
    whole = pl.BlockSpec((s, LANES), lambda p, i: (0, p))
    return pl.pallas_call(
        body, name=name, grid=(npair, nq), out_shape=(jax.ShapeDtypeStruct((s, W_GRP), F32),) * 3,
        in_specs=[pl.BlockSpec((tq, LANES), lambda p, i: (i, qcol + p)),
                  pl.BlockSpec((s, LANES), lambda p, i: (0, kcol + p)),
                  pl.BlockSpec((s, LANES), lambda p, i: (0, vcol + p)),
                  pl.BlockSpec((None, tq, 2), lambda p, i: (p, i, 0)), blk],
        out_specs=(blk, whole, whole), compiler_params=_params(),
    )(proj, proj, proj, ltot, dob)


def _ada_fwd(c_all, w_ada, b_ada, name):
    nl, d, n = w_ada.shape
    tn = _pick(n, 512)

    def body(c_ref, w_ref, b_ref, o_ref):
        cv = c_ref[...]
        act = (cv * jax.nn.sigmoid(cv)).astype(BF16)
        o_ref[...] = _nn(act, w_ref[...].astype(BF16)) + b_ref[...]

    return pl.pallas_call(
        body, name=name, out_shape=jax.ShapeDtypeStruct((nl, N_DEV, n), F32), grid=(nl, n // tn),
        in_specs=[pl.BlockSpec((N_DEV, d), lambda l, j: (0, 0)), pl.BlockSpec((None, d, tn), lambda l, j: (l, 0, j)),
                  pl.BlockSpec((None, 1, tn), lambda l, j: (l, 0, j))],
        out_specs=pl.BlockSpec((None, N_DEV, tn), lambda l, j: (l, 0, j)), compiler_params=_params(),
    )(c_all, w_ada, b_ada)


def _ada_bwd(c_all, dmod, name):
    nl, _, n = dmod.shape
    d = c_all.shape[1]
    tn = _pick(n, 512)

    def body(c_ref, g_ref, o_ref):
        cv = c_ref[...]
        act = (cv * jax.nn.sigmoid(cv)).astype(BF16)
        o_ref[...] = _tn(act, g_ref[...].astype(BF16))

    return pl.pallas_call(
        body, name=name, out_shape=jax.ShapeDtypeStruct((nl, d, n), F32), grid=(nl, n // tn),
        in_specs=[pl.BlockSpec((N_DEV, d), lambda l, j: (0, 0)), pl.BlockSpec((None, N_DEV, tn), lambda l, j: (l, 0, j))],
        out_specs=pl.BlockSpec((None, d, tn), lambda l, j: (l, 0, j)), compiler_params=_params(),
    )(c_all, dmod)


def _adamw(g, w, m, v, name):
    r, c = g.shape
    tr = _pick(r, 512, 8)
    c1 = 1.0 - ADAM_B1 ** ADAM_STEP
    c2 = 1.0 - ADAM_B2 ** ADAM_STEP

    def body(g_ref, w_ref, m_ref, v_ref, d_ref, nm_ref, nv_ref):
        gv = g_ref[...]
        nm = ADAM_B1 * m_ref[...] + (1.0 - ADAM_B1) * gv
        nv = ADAM_B2 * v_ref[...] + (1.0 - ADAM_B2) * (gv * gv)
        d_ref[...] = -ADAM_LR * ((nm / c1) / (jnp.sqrt(nv / c2) + ADAM_EPS) + ADAM_WD * w_ref[...])
        nm_ref[...] = nm
        nv_ref[...] = nv

    spec = pl.BlockSpec((tr, c), lambda i: (i, 0))
    return pl.pallas_call(
        body, name=name, out_shape=(jax.ShapeDtypeStruct((r, c), F32),) * 3, grid=(r // tr,),
        in_specs=[spec] * 4, out_specs=(spec,) * 3, compiler_params=_params(),
    )(g, w, m, v)


def _adamw_nd(g, w, m, v, name):
    shape = w.shape
    two_d = (1, shape[0]) if len(shape) == 1 else (-1, shape[-1])
    outs = _adamw(*(t.reshape(two_d) for t in (g, w, m, v)), name=name)
    return tuple(o.reshape(shape) for o in outs)


def _allgather8(v, name):
    m, n = v.shape

    def body(v_ref, out_ref, send_sems, recv_sems, local_sem):
        x, y, c = _my_place()

        def rows(px, py, pc):
            return out_ref.at[pl.ds(pl.multiple_of((4 * px + 2 * py + pc) * m, 8), m), :]

        def peer(k):
            return _flip(x, k & 4), _flip(y, k & 2), _flip(c, k & 1)

        def copy(k, block):
            return pltpu.make_async_remote_copy(
                src_ref=v_ref, dst_ref=rows(*block), send_sem=send_sems.at[k - 1], recv_sem=recv_sems.at[k - 1],
                device_id=peer(k), device_id_type=MESH)

        mine = pltpu.make_async_copy(v_ref, rows(x, y, c), local_sem)
        mine.start()
        sends = [copy(k, (x, y, c)) for k in range(1, N_DEV)]
        for cp in sends:
            cp.start()
        for k in range(1, N_DEV):
            copy(k, peer(k)).wait_recv()
        for cp in sends:
            cp.wait_send()
        mine.wait()

    return pl.pallas_call(
        body, name=name, out_shape=jax.ShapeDtypeStruct((N_DEV * m, n), v.dtype),
        in_specs=[VMEM_FULL], out_specs=VMEM_FULL,
        scratch_shapes=[pltpu.SemaphoreType.DMA((N_DEV - 1,)), pltpu.SemaphoreType.DMA((N_DEV - 1,)),
                        pltpu.SemaphoreType.DMA],
        compiler_params=_params(),
    )(v)


def _chip_peers(x, y, c):
    out = []
    for k in range(1, N_CHIPS):
        px, py = _flip(x, k & 2), _flip(y, k & 1)
        out.append((2 * px + py, (px, py, c)))
    return out


def _gather_weights(shards, kinds, name):
    nw = len(shards)

    def full_shape(a, kind):
        l, r, n = a.shape
        return (l, r, N_CHIPS * n) if kind == "col" else (l, N_CHIPS * r, n)

    def body(*refs):
        ins, outs = refs[:nw], refs[nw:2 * nw]
        send_sems, recv_sems, local_sems = refs[2 * nw:]
        x, y, c = _my_place()
        chip = 2 * x + y

        def window(w, j):
            _, r, n = shards[w].shape
            if kinds[w] == "col":
                return outs[w].at[:, :, pl.ds(pl.multiple_of(j * n, LANES), n)]
            return outs[w].at[:, pl.ds(pl.multiple_of(j * r, 16), r), :]

        def copy(w, k, j, peer):
            return pltpu.make_async_remote_copy(
                src_ref=ins[w], dst_ref=window(w, j), send_sem=send_sems.at[3 * w + k], recv_sem=recv_sems.at[3 * w + k],
                device_id=peer, device_id_type=MESH)

        local = [pltpu.make_async_copy(ins[w], window(w, chip), local_sems.at[w]) for w in range(nw)]
        for cp in local:
            cp.start()
        peers = _chip_peers(x, y, c)
        sends = [copy(w, k, chip, peer) for w in range(nw) for k, (_, peer) in enumerate(peers)]
        for cp in sends:
            cp.start()
        for w in range(nw):
            for k, (pchip, peer) in enumerate(peers):
                copy(w, k, pchip, peer).wait_recv()
        for cp in sends:
            cp.wait_send()
        for cp in local:
            cp.wait()

    return pl.pallas_call(
        body, name=name,
        out_shape=tuple(jax.ShapeDtypeStruct(full_shape(a, kd), a.dtype) for a, kd in zip(shards, kinds)),
        in_specs=[ANY] * nw, out_specs=(ANY,) * nw,
        scratch_shapes=[pltpu.SemaphoreType.DMA((3 * nw,)), pltpu.SemaphoreType.DMA((3 * nw,)),
                        pltpu.SemaphoreType.DMA((nw,))],
        compiler_params=_params(),
    )(*shards)


def _rs_to_sibling(grads, name):
    nw = len(grads)

    def body(*refs):
        ins, outs = refs[:nw], refs[nw:2 * nw]
        send_sems, recv_sems = refs[2 * nw:]
        x, y, c = _my_place()
        sibling = (x, y, 1 - c)
        copies = [pltpu.make_async_remote_copy(
            src_ref=ins[w].at[j, 1 - c], dst_ref=outs[w].at[j], send_sem=send_sems.at[N_CHIPS * w + j],
            recv_sem=recv_sems.at[N_CHIPS * w + j], device_id=sibling, device_id_type=MESH)
            for w in range(nw) for j in range(N_CHIPS)]
        for cp in copies:
            cp.start()
        for cp in copies:
            cp.wait_recv()
        for cp in copies:
            cp.wait_send()

    return pl.pallas_call(
        body, name=name,
        out_shape=tuple(jax.ShapeDtypeStruct((N_CHIPS,) + g.shape[2:], g.dtype) for g in grads),
        in_specs=[ANY] * nw, out_specs=(ANY,) * nw,
        scratch_shapes=[pltpu.SemaphoreType.DMA((N_CHIPS * nw,)), pltpu.SemaphoreType.DMA((N_CHIPS * nw,))],
        compiler_params=_params(),
    )(*grads)


def _rs_to_chips(parts, name):
    nw = len(parts)

    def body(*refs):
        ins, outs = refs[:nw], refs[nw:2 * nw]
        send_sems, recv_sems, local_sems = refs[2 * nw:]
        x, y, c = _my_place()
        chip = 2 * x + y
        peers = _chip_peers(x, y, c)

        def copy(w, k, src_slab, dst_slab, peer):
            return pltpu.make_async_remote_copy(
                src_ref=ins[w].at[src_slab], dst_ref=outs[w].at[dst_slab], send_sem=send_sems.at[3 * w + k],
                recv_sem=recv_sems.at[3 * w + k], device_id=peer, device_id_type=MESH)

        local = [pltpu.make_async_copy(ins[w].at[chip], outs[w].at[chip], local_sems.at[w]) for w in range(nw)]
        for cp in local:
            cp.start()
        sends = [copy(w, k, pchip, chip, peer) for w in range(nw) for k, (pchip, peer) in enumerate(peers)]
        for cp in sends:
            cp.start()
        for w in range(nw):
            for k, (pchip, peer) in enumerate(peers):
                copy(w, k, chip, pchip, peer).wait_recv()
        for cp in sends:
            cp.wait_send()
        for cp in local:
            cp.wait()

    return pl.pallas_call(
        body, name=name, out_shape=tuple(jax.ShapeDtypeStruct(p.shape, p.dtype) for p in parts),
        in_specs=[ANY] * nw, out_specs=(ANY,) * nw,
        scratch_shapes=[pltpu.SemaphoreType.DMA((3 * nw,)), pltpu.SemaphoreType.DMA((3 * nw,)),
                        pltpu.SemaphoreType.DMA((nw,))],
        compiler_params=_params(),
    )(*parts)


def _rs_share_halves(halves, name):
    nw = len(halves)
    nl = len(halves[0])
    flat = [h for hs in halves for h in hs]

    def body(*refs):
        ins, outs = refs[:nw * nl], refs[nw * nl:nw * nl + nw]
        send_sems, recv_sems, local_sems = refs[nw * nl + nw:]
        x, y, c = _my_place()
        sibling = (x, y, 1 - c)
        local, sends, recvs = [], [], []
        for w in range(nw):
            for l in range(nl):
                n = nl * w + l
                local.append(pltpu.make_async_copy(ins[n], outs[w].at[l, c], local_sems.at[n]))
                sends.append(pltpu.make_async_remote_copy(
                    src_ref=ins[n], dst_ref=outs[w].at[l, c], send_sem=send_sems.at[n], recv_sem=recv_sems.at[n],
                    device_id=sibling, device_id_type=MESH))
                recvs.append(pltpu.make_async_remote_copy(
                    src_ref=ins[n], dst_ref=outs[w].at[l, 1 - c], send_sem=send_sems.at[n], recv_sem=recv_sems.at[n],
                    device_id=sibling, device_id_type=MESH))
        for cp in local + sends:
            cp.start()
        for cp in recvs:
            cp.wait_recv()
        for cp in sends:
            cp.wait_send()
        for cp in local:
            cp.wait()

    return pl.pallas_call(
        body, name=name,
        out_shape=tuple(jax.ShapeDtypeStruct((nl, 2) + hs[0].shape, hs[0].dtype) for hs in halves),
        in_specs=[ANY] * (nw * nl), out_specs=(ANY,) * nw,
        scratch_shapes=[pltpu.SemaphoreType.DMA((nw * nl,)), pltpu.SemaphoreType.DMA((nw * nl,)),
                        pltpu.SemaphoreType.DMA((nw * nl,))],
        compiler_params=_params(),
    )(*flat)


def _add_own_half(grad, got, c_idx, name):
    _, _, r, n = grad.shape
    tr = _pick(r, 256, 8)

    def body(c_ref, g_ref, t_ref, o_ref):
        o_ref[...] = g_ref[...] + t_ref[...]

    return pl.pallas_call(
        body, name=name, out_shape=jax.ShapeDtypeStruct((N_CHIPS, r, n), F32),
        grid_spec=pltpu.PrefetchScalarGridSpec(
            num_scalar_prefetch=1, grid=(N_CHIPS, r // tr),
            in_specs=[pl.BlockSpec((None, None, tr, n), lambda j, i, c_ref: (j, c_ref[0], i, 0)),
                      pl.BlockSpec((None, tr, n), lambda j, i, c_ref: (j, i, 0))],
            out_specs=pl.BlockSpec((None, tr, n), lambda j, i, c_ref: (j, i, 0))),
        compiler_params=_params(),
    )(c_idx, grad, got)


def _sum_slabs(slabs, name):
    ns, r, n = slabs.shape
    tr = _pick(r, 256, 8)

    def body(s_ref, o_ref):
        acc = s_ref[0]
        for j in range(1, ns):
            acc = acc + s_ref[j]
        o_ref[...] = acc

    return pl.pallas_call(
        body, name=name, out_shape=jax.ShapeDtypeStruct((r, n), F32), grid=(r // tr,),
        in_specs=[pl.BlockSpec((ns, tr, n), lambda i: (0, i, 0))], out_specs=pl.BlockSpec((tr, n), lambda i: (i, 0)),
        compiler_params=_params(),
    )(slabs)


def _band_bias(rel_bias):
    q_pos = PAD + jnp.arange(CHUNK)
    dist = jnp.clip(q_pos[:, None] - jnp.arange(BAND)[None, :], -REL_CLIP, REL_CLIP) + REL_CLIP
    return rel_bias[:, dist]


def _pack_rows(pieces):
    flat = jnp.concatenate([p.reshape(-1) for p in pieces])
    rows = -(-flat.shape[0] // (8 * LANES)) * 8
    return jnp.pad(flat, (0, rows * LANES - flat.shape[0])).reshape(rows, LANES)


def _unpack_rows(packed, shapes):
    flat = packed.reshape(-1)
    out, at = [], 0
    for shp in shapes:
        size = 1
        for n in shp:
            size *= n
        out.append(flat[at:at + size].reshape(shp))
        at += size
    return out


def _layer_fwd(x, mod, w, band, tag):
    s, d = x.shape
    row = lambda i: mod[i:i + 1]
    h1 = _norm_mod(x, row(1), row(0), f"norm_mix{tag}")
    proj = _matmul(h1, w["w_in"], form="nn", out_dtype=BF16, tm=_pick(s, 512), tn=_pick(w["w_in"].shape[1], 768),
                   tk=d, name=f"proj{tag}")
    oa = _attn_a_fwd(proj, band, f"attn_a{tag}")
    ob, ltot = _sb_fwd(proj, f"attn_b{tag}")
    cat = _out_norm(oa, ob, w["g_a"], w["g_b"], f"out_norm{tag}")
    mixed = _matmul(cat, w["w_out"], form="nn", out_dtype=F32, tm=_pick(s, 512), tn=_pick(d, 1024),
                    tk=cat.shape[1], name=f"mix_out{tag}")
    x1 = _residual(x, row(2), mixed, f"res_mix{tag}")
    h2 = _norm_mod(x1, row(4), row(3), f"norm_ffn{tag}")
    f2 = w["w_up"].shape[1]
    u = _matmul(h2, w["w_up"], form="nn", out_dtype=F32, tm=_pick(s, 512), tn=_pick(f2, 1408), tk=d, name=f"up{tag}")
    a = _conv_glu(u, w["conv_w"], w["conv_b"], f"conv_glu{tag}")
    f = _matmul(a, w["w_down"], form="nn", out_dtype=F32, tm=_pick(s, 512), tn=_pick(d, 1024),
                tk=_pick(f2 // 2, 1408), name=f"down{tag}")
    x2 = _residual(x1, row(5), f, f"res_ffn{tag}")
    saved = dict(x=x, h1=h1, proj=proj, oa=oa, ob=ob, ltot=ltot, cat=cat, mixed=mixed, x1=x1, h2=h2, u=u, a=a, f=f)
    return x2, saved


def _layer_bwd(dx2, sv, mod, w, band, tag):
    s, d = dx2.shape
    row = lambda i: mod[i:i + 1]
    f2 = w["w_up"].shape[1]
    ff = f2 // 2
    n_in = w["w_in"].shape[1]
    df, dgate_ffn = _gate_bwd(dx2, sv["f"], row(5), f"gate_ffn_bwd{tag}")
    da = _matmul(df, w["w_down"], form="nt", out_dtype=F32, tm=_pick(s, 512), tn=_pick(ff, 1408), tk=d, name=f"down_dx{tag}")
    g_down = _matmul(sv["a"], df, form="tn", out_dtype=F32, tm=_pick(ff, 1408), tn=_pick(d, 512), tk=_pick(s, 512),
                     name=f"down_dw{tag}")
    du2, dcw, dcb = _conv_glu_bwd(sv["u"], da, w["conv_w"], w["conv_b"], f"conv_glu_bwd{tag}")
    du = jnp.concatenate([du2[0], du2[1]], axis=1)
    dh2 = _matmul(du, w["w_up"], form="nt", out_dtype=F32, tm=_pick(s, 512), tn=_pick(d, 1024), tk=_pick(f2, 1408),
                  name=f"up_dx{tag}")
    g_up = _matmul(sv["h2"], du, form="tn", out_dtype=F32, tm=_pick(d, 512), tn=_pick(f2 // N_CHIPS, 1408),
                   tk=_pick(s, 512), name=f"up_dw{tag}", shard_cols=f2 // N_CHIPS)
    dx1, dscale_ffn, dshift_ffn = _norm_mod_bwd(sv["x1"], dh2, dx2, row(4), f"norm_ffn_bwd{tag}")
    dmixed, dgate_mix = _gate_bwd(dx1, sv["mixed"], row(2), f"gate_mix_bwd{tag}")
    dcat = _matmul(dmixed, w["w_out"], form="nt", out_dtype=F32, tm=_pick(s, 512), tn=_pick(2 * W_GRP, 1024), tk=d,
                   name=f"mix_out_dx{tag}")
    g_out = _matmul(sv["cat"], dmixed, form="tn", out_dtype=F32, tm=_pick(2 * W_GRP, 512), tn=_pick(d, 1024),
                    tk=_pick(s, 512), name=f"mix_out_dw{tag}")
    doa, dob, dg_a, dg_b = _out_norm_bwd(sv["oa"], sv["ob"], dcat, w["g_a"], w["g_b"], f"out_norm_bwd{tag}")
    dqa, dka, dva, dband = _attn_a_bwd(sv["proj"], band, doa, f"attn_a_bwd{tag}")
    dqb, dkb, dvb = _sb_bwd(sv["proj"], sv["ltot"], dob, f"attn_b_bwd{tag}")
    drel = _rel_bias_grad(jnp.transpose(dband, (1, 0, 2)), f"rel_bias_bwd{tag}")[:, :N_REL]
    dproj = jnp.concatenate([dqa, dka, dva, dqb, dkb, dvb], axis=1).astype(BF16)
    dh1 = _matmul(dproj, w["w_in"], form="nt", out_dtype=F32, tm=_pick(s, 512), tn=_pick(d, 1024), tk=_pick(n_in, 1024),
                  name=f"proj_dx{tag}")
    g_in = _matmul(sv["h1"], dproj, form="tn", out_dtype=F32, tm=_pick(d, 512), tn=_pick(n_in // N_CHIPS, 768),
                   tk=_pick(s, 512), name=f"proj_dw{tag}", shard_cols=n_in // N_CHIPS)
    dx, dscale_mix, dshift_mix = _norm_mod_bwd(sv["x"], dh1, dx1, row(1), f"norm_mix_bwd{tag}")
    dmod = jnp.concatenate([dshift_mix, dscale_mix, dgate_mix, dshift_ffn, dscale_ffn, dgate_ffn], axis=1)
    big = dict(w_in=g_in, w_out=g_out, w_up=g_up, w_down=g_down)
    dconv_w = jnp.concatenate([dcw[0], dcw[1]], axis=1)
    dconv_b = jnp.concatenate([dcb[0], dcb[1]], axis=1)
    small = dict(dmod=dmod, rel_bias=drel, g_a=dg_a, g_b=dg_b, conv_w=dconv_w, conv_b=dconv_b)
    return dx, big, small


def kernel(x, c, w_ada, b_ada, w_in, rel_bias, g_a, g_b, w_out, w_up, conv_w, conv_b, w_down, final_g, loss_target, m_w_ada, m_b_ada, m_w_in, m_rel_bias, m_g_a, m_g_b, m_w_out, m_w_up, m_conv_w, m_conv_b, m_w_down, m_final_g, v_w_ada, v_b_ada, v_w_in, v_rel_bias, v_g_a, v_g_b, v_w_out, v_w_up, v_conv_w, v_conv_b, v_w_down, v_final_g):
    xi, yi, ci = _my_place()
    chip = 2 * xi + yi
    dev = 4 * xi + 2 * yi + ci
    nl, d, n_ada = w_ada.shape
    s = x.shape[1]
    f2 = N_CHIPS * w_up.shape[2]
    nc = conv_w.shape[2]

    c_pad = jnp.pad(c, ((0, 7), (0, 0)))
    c_all = _allgather8(c_pad, "gather_c")[0::8]
    b_mine = lax.dynamic_slice_in_dim(b_ada, chip * n_ada, n_ada, axis=1)[:, None, :]
    mod_shard = _ada_fwd(c_all, w_ada, b_mine, "ada")
    pack2 = _pack_rows([mod_shard, conv_w])
    got2 = _allgather8(pack2, "gather_mod").reshape(N_DEV, -1)
    mods, convs = [], []
    for j in range(N_CHIPS):
        ms, cw = _unpack_rows(got2[2 * j], [mod_shard.shape, conv_w.shape])
        mods.append(lax.dynamic_index_in_dim(ms, dev, axis=1, keepdims=False))
        convs.append(cw)
    mod = jnp.concatenate(mods, axis=1).reshape(nl, 6, d)
    conv_w_full = jnp.concatenate(convs, axis=2)

    names = ("w_in", "w_out", "w_up", "w_down")
    kinds = ("col", "row", "col", "row")
    shards = dict(w_in=w_in, w_out=w_out, w_up=w_up, w_down=w_down)
    full = _gather_weights([shards[n].astype(BF16) for n in names], kinds, "gather_weights")
    full = dict(zip(names, full))

    xs = x[0]
    layers, saved, bands = [], [], []
    for l in range(nl):
        w = {n: full[n][l] for n in names}
        w.update(g_a=g_a[l:l + 1], g_b=g_b[l:l + 1], conv_w=conv_w_full[l], conv_b=conv_b[l:l + 1])
        band = _band_bias(rel_bias[l])
        xs, sv = _layer_fwd(xs, mod[l], w, band, f"_l{l}")
        layers.append(w)
        bands.append(band)
        saved.append(sv)
    loss_part, dx, dfinal_g = _loss_head(xs, final_g[None, :], loss_target[0], "loss_head")
    loss = lax.psum(loss_part[0, 0], ("x", "y", "c"))

    big, small = [None] * nl, [None] * nl
    for l in reversed(range(nl)):
        dx, big[l], small[l] = _layer_bwd(dx, saved[l], mod[l], layers[l], bands[l], f"_l{l}")

    small_names = ("dmod", "rel_bias", "g_a", "g_b", "conv_w", "conv_b")
    pieces = [small[l][n] for l in range(nl) for n in small_names] + [dfinal_g]
    shapes = [p.shape for p in pieces]
    pack3 = _pack_rows(pieces)
    got3 = _allgather8(pack3, "gather_small").reshape(N_DEV, pack3.shape[0], LANES)
    summed = _unpack_rows(_sum_slabs(got3, "sum_small"), shapes)
    tot = [dict(zip(small_names, summed[len(small_names) * l:len(small_names) * (l + 1)])) for l in range(nl)]
    g_final_g = summed[-1].reshape(-1)
    g_b_ada = jnp.stack([tot[l]["dmod"].reshape(-1) for l in range(nl)])
    g_rel = jnp.stack([tot[l]["rel_bias"] for l in range(nl)])
    g_ga = jnp.stack([tot[l]["g_a"].reshape(-1) for l in range(nl)])
    g_gb = jnp.stack([tot[l]["g_b"].reshape(-1) for l in range(nl)])
    g_conv_b = jnp.stack([tot[l]["conv_b"].reshape(-1) for l in range(nl)])
    g_conv_w = jnp.stack([lax.dynamic_slice_in_dim(tot[l]["conv_w"], chip * nc, nc, axis=1) for l in range(nl)])
    per_dev = [_unpack_rows(got3[j], shapes) for j in range(N_DEV)]
    dmod_all = jnp.stack([jnp.stack([per_dev[j][len(small_names) * l].reshape(-1) for j in range(N_DEV)])
                          for l in range(nl)])
    g_w_ada = _ada_bwd(c_all, lax.dynamic_slice_in_dim(dmod_all, chip * n_ada, n_ada, axis=2), "ada_bwd")

    order = [(n, l) for n in names for l in range(nl)]
    flat_g = [big[l][n].reshape(N_CHIPS, 2, -1, 1024) for n, l in order]
    from_sib = _rs_to_sibling(flat_g, "rs_sibling")
    c_idx = jnp.reshape(ci, (1,)).astype(jnp.int32)
    chip_part = [_add_own_half(g, t, c_idx, f"rs_add_{n}_l{l}") for g, t, (n, l) in zip(flat_g, from_sib, order)]
    from_chips = _rs_to_chips(chip_part, "rs_chips")
    my_half = [_sum_slabs(t, f"rs_sum_{n}_l{l}") for t, (n, l) in zip(from_chips, order)]
    shard_g = _rs_share_halves([[my_half[nl * i + l] for l in range(nl)] for i in range(len(names))], "rs_halves")
    g_big = {n: shard_g[i].reshape(shards[n].shape) for i, n in enumerate(names)}

    grads = dict(w_ada=g_w_ada, b_ada=g_b_ada, w_in=g_big["w_in"], rel_bias=g_rel, g_a=g_ga, g_b=g_gb,
                 w_out=g_big["w_out"], w_up=g_big["w_up"], conv_w=g_conv_w, conv_b=g_conv_b, w_down=g_big["w_down"],
                 final_g=g_final_g)
    weights = dict(w_ada=w_ada, b_ada=b_ada, w_in=w_in, rel_bias=rel_bias, g_a=g_a, g_b=g_b, w_out=w_out, w_up=w_up,
                   conv_w=conv_w, conv_b=conv_b, w_down=w_down, final_g=final_g)
    m_in = dict(w_ada=m_w_ada, b_ada=m_b_ada, w_in=m_w_in, rel_bias=m_rel_bias, g_a=m_g_a, g_b=m_g_b, w_out=m_w_out,
                w_up=m_w_up, conv_w=m_conv_w, conv_b=m_conv_b, w_down=m_w_down, final_g=m_final_g)
    v_in = dict(w_ada=v_w_ada, b_ada=v_b_ada, w_in=v_w_in, rel_bias=v_rel_bias, g_a=v_g_a, g_b=v_g_b, w_out=v_w_out,
                w_up=v_w_up, conv_w=v_conv_w, conv_b=v_conv_b, w_down=v_w_down, final_g=v_final_g)
    order_w = ("w_ada", "b_ada", "w_in", "rel_bias", "g_a", "g_b", "w_out", "w_up", "conv_w", "conv_b", "w_down", "final_g")
    upd = {n: _adamw_nd(grads[n], weights[n], m_in[n], v_in[n], f"adamw_{n}") for n in order_w}
    return (loss, dx[None], *[grads[n] for n in order_w], *[upd[n][0] for n in order_w],
            *[upd[n][1] for n in order_w], *[upd[n][2] for n in order_w])
```

```python
import functools

import jax
import jax.numpy as jnp
from jax import lax
from jax.experimental import pallas as pl
from jax.experimental.pallas import tpu as pltpu

F32 = jnp.float32
BF16 = jnp.bfloat16
MESH = pl.DeviceIdType.MESH
ANY = pl.BlockSpec(memory_space=pl.ANY)
VMEM_FULL = pl.BlockSpec(memory_space=pltpu.VMEM)

HEAD_DIM = 64
N_HEADS = 8
W_GRP = N_HEADS * HEAD_DIM
CHUNK = 64
N_PREV = 8
BAND = (N_PREV + 1) * CHUNK
PAD = N_PREV * CHUNK
REL_CLIP = 128
N_REL = 2 * REL_CLIP + 1
EPS = 1e-6
N_CHIPS = 4
N_DEV = 8
LANES = 128
V7X_VMEM_LIMIT = 56 * 1024 * 1024

ADAM_LR = 0.001
ADAM_B1 = 0.9
ADAM_B2 = 0.999
ADAM_EPS = 1e-08
ADAM_WD = 0.01
ADAM_STEP = 10


def _params(**kw):
    return pltpu.CompilerParams(vmem_limit_bytes=V7X_VMEM_LIMIT, **kw)


def _pick(dim, pref, mult=LANES):
    t = (min(pref, dim) // mult) * mult
    while t >= mult:
        if dim % t == 0:
            return t
        t -= mult
    return dim


def _my_place():
    return lax.axis_index("x"), lax.axis_index("y"), lax.axis_index("c")


def _flip(v, bit):
    return 1 - v if bit else v


def _matmul(a, b, *, form, out_dtype, tm, tn, tk, name, shard_cols=None):
    if form == "nn":
        (m, k), (_, n) = a.shape, b.shape
        a_spec = pl.BlockSpec((tm, tk), lambda i, j, kk: (i, kk))
        b_spec = pl.BlockSpec((tk, tn), lambda i, j, kk: (kk, j))
        dims = (((1,), (0,)), ((), ()))
    elif form == "nt":
        (m, k), (n, _) = a.shape, b.shape
        a_spec = pl.BlockSpec((tm, tk), lambda i, j, kk: (i, kk))
        b_spec = pl.BlockSpec((tn, tk), lambda i, j, kk: (j, kk))
        dims = (((1,), (1,)), ((), ()))
    else:
        (k, m), (_, n) = a.shape, b.shape
        a_spec = pl.BlockSpec((tk, tm), lambda i, j, kk: (kk, i))
        b_spec = pl.BlockSpec((tk, tn), lambda i, j, kk: (kk, j))
        dims = (((0,), (0,)), ((), ()))
    assert m % tm == 0 and n % tn == 0 and k % tk == 0, (name, m, n, k, tm, tn, tk)
    nk = k // tk
    if shard_cols is None:
        out_shape = jax.ShapeDtypeStruct((m, n), out_dtype)
        o_spec = pl.BlockSpec((tm, tn), lambda i, j, kk: (i, j))
    else:
        per = shard_cols // tn
        assert shard_cols % tn == 0
        out_shape = jax.ShapeDtypeStruct((n // shard_cols, m, shard_cols), out_dtype)
        o_spec = pl.BlockSpec((None, tm, tn), lambda i, j, kk: (j // per, i, j % per))

    def body(a_ref, b_ref, o_ref, acc_ref):
        kk = pl.program_id(2)
        part = lax.dot_general(a_ref[...], b_ref[...], dims, preferred_element_type=F32)

        @pl.when(kk == 0)
        def _():
            acc_ref[...] = part

        @pl.when(kk > 0)
        def _():
            acc_ref[...] += part

        @pl.when(kk == nk - 1)
        def _():
            o_ref[...] = acc_ref[...].astype(out_dtype)

    return pl.pallas_call(
        body, name=name, out_shape=out_shape, grid=(m // tm, n // tn, nk),
        in_specs=[a_spec, b_spec], out_specs=o_spec,
        scratch_shapes=[pltpu.VMEM((tm, tn), F32)], compiler_params=_params(),
    )(a, b)


def _row_spec(tr, d):
    return pl.BlockSpec((tr, d), lambda i: (i, 0))


def _vec_spec(d):
    return pl.BlockSpec((1, d), lambda i: (0, 0))


def _rms(xf):
    r = lax.rsqrt(jnp.mean(xf * xf, axis=-1, keepdims=True) + EPS)
    return xf * r, r


def _norm_mod(x, scale, shift, name):
    s, d = x.shape
    tr = _pick(s, 512, 8)

    def body(x_ref, sc_ref, sh_ref, o_ref):
        n, _ = _rms(x_ref[...])
        o_ref[...] = (n * (1.0 + sc_ref[...]) + sh_ref[...]).astype(BF16)

    return pl.pallas_call(
        body, name=name, out_shape=jax.ShapeDtypeStruct((s, d), BF16), grid=(s // tr,),
        in_specs=[_row_spec(tr, d), _vec_spec(d), _vec_spec(d)], out_specs=_row_spec(tr, d),
        compiler_params=_params(),
    )(x, scale, shift)


def _out_norm(oa, ob, g_a, g_b, name):
    s, w = oa.shape
    tr = _pick(s, 512, 8)

    def body(oa_ref, ob_ref, ga_ref, gb_ref, o_ref):
        na, _ = _rms(oa_ref[...])
        nb, _ = _rms(ob_ref[...])
        o_ref[:, :w] = (na * ga_ref[...]).astype(BF16)
        o_ref[:, w:] = (nb * gb_ref[...]).astype(BF16)

    return pl.pallas_call(
        body, name=name, out_shape=jax.ShapeDtypeStruct((s, 2 * w), BF16), grid=(s // tr,),
        in_specs=[_row_spec(tr, w), _row_spec(tr, w), _vec_spec(w), _vec_spec(w)],
        out_specs=_row_spec(tr, 2 * w), compiler_params=_params(),
    )(oa, ob, g_a, g_b)


def _residual(x, gate, m, name):
    s, d = x.shape
    tr = _pick(s, 512, 8)

    def body(x_ref, g_ref, m_ref, o_ref):
        o_ref[...] = x_ref[...] + g_ref[...] * m_ref[...]

    return pl.pallas_call(
        body, name=name, out_shape=jax.ShapeDtypeStruct((s, d), F32), grid=(s // tr,),
        in_specs=[_row_spec(tr, d), _vec_spec(d), _row_spec(tr, d)], out_specs=_row_spec(tr, d),
        compiler_params=_params(),
    )(x, gate, m)


def _shift_down(u, k):
    rows = lax.broadcasted_iota(jnp.int32, u.shape, 0)
    return jnp.where(rows >= k, pltpu.roll(u, k, 0), 0.0)


def _shift_up(u, k):
    s = u.shape[0]
    rows = lax.broadcasted_iota(jnp.int32, u.shape, 0)
    return jnp.where(rows < s - k, pltpu.roll(u, s - k, 0), 0.0)


def _conv(u, w_ref, b_ref):
    return w_ref[0:1, :] * _shift_down(u, 2) + w_ref[1:2, :] * _shift_down(u, 1) + w_ref[2:3, :] * u + b_ref[...]


def _conv_glu(u, conv_w, conv_b, name):
    s, f2 = u.shape
    f = f2 // 2
    tc = LANES
    nb = f // tc

    def body(ug_ref, uv_ref, wg_ref, wv_ref, bg_ref, bv_ref, o_ref):
        g = _conv(ug_ref[...], wg_ref, bg_ref)
        v = _conv(uv_ref[...], wv_ref, bv_ref)
        o_ref[...] = (g * jax.nn.sigmoid(g) * v).astype(BF16)

    col = lambda off: pl.BlockSpec((s, tc), lambda j: (0, j + off))
    wcol = lambda off: pl.BlockSpec((3, tc), lambda j: (0, j + off))
    bcol = lambda off: pl.BlockSpec((1, tc), lambda j: (0, j + off))
    return pl.pallas_call(
        body, name=name, out_shape=jax.ShapeDtypeStruct((s, f), BF16), grid=(nb,),
        in_specs=[col(0), col(nb), wcol(0), wcol(nb), bcol(0), bcol(nb)], out_specs=col(0),
        compiler_params=_params(),
    )(u, u, conv_w, conv_w, conv_b, conv_b)


def _conv_glu_bwd(u, da, conv_w, conv_b, name):
    s, f2 = u.shape
    f = f2 // 2
    tc = LANES
    nb = f // tc

    def body(ug_ref, uv_ref, da_ref, wg_ref, wv_ref, bg_ref, bv_ref, du_ref, dw_ref, db_ref):
        da_ = da_ref[...]
        ug, uv = ug_ref[...], uv_ref[...]
        g = _conv(ug, wg_ref, bg_ref)
        v = _conv(uv, wv_ref, bv_ref)
        sg = jax.nn.sigmoid(g)
        dg = da_ * v * (sg * (1.0 + g * (1.0 - sg)))
        dv = da_ * (g * sg)
        for h, (dy, uu, w_ref) in enumerate(((dg, ug, wg_ref), (dv, uv, wv_ref))):
            du = w_ref[2:3, :] * dy + w_ref[1:2, :] * _shift_up(dy, 1) + w_ref[0:1, :] * _shift_up(dy, 2)
            du_ref[h] = du.astype(BF16)
            dw_ref[h, 0:1, :] = jnp.sum(dy * _shift_down(uu, 2), axis=0, keepdims=True)
            dw_ref[h, 1:2, :] = jnp.sum(dy * _shift_down(uu, 1), axis=0, keepdims=True)
            dw_ref[h, 2:3, :] = jnp.sum(dy * uu, axis=0, keepdims=True)
            db_ref[h] = jnp.sum(dy, axis=0, keepdims=True)

    col = lambda off: pl.BlockSpec((s, tc), lambda j: (0, j + off))
    wcol = lambda off: pl.BlockSpec((3, tc), lambda j: (0, j + off))
    bcol = lambda off: pl.BlockSpec((1, tc), lambda j: (0, j + off))
    return pl.pallas_call(
        body, name=name, grid=(nb,),
        out_shape=(jax.ShapeDtypeStruct((2, s, f), BF16), jax.ShapeDtypeStruct((2, 3, f), F32),
                   jax.ShapeDtypeStruct((2, 1, f), F32)),
        in_specs=[col(0), col(nb), col(0), wcol(0), wcol(nb), bcol(0), bcol(nb)],
        out_specs=(pl.BlockSpec((2, s, tc), lambda j: (0, 0, j)), pl.BlockSpec((2, 3, tc), lambda j: (0, 0, j)),
                   pl.BlockSpec((2, 1, tc), lambda j: (0, 0, j))),
        compiler_params=_params(),
    )(u, u, da, conv_w, conv_w, conv_b, conv_b)


def _accumulate(ref, val):
    @pl.when(pl.program_id(0) == 0)
    def _():
        ref[...] = val

    @pl.when(pl.program_id(0) > 0)
    def _():
        ref[...] += val


def _rms_bwd(n, r, dn):
    return r * (dn - n * jnp.mean(dn * n, axis=-1, keepdims=True))


def _loss_head(x, final_g, target, name):
    s, d = x.shape
    tr = _pick(s, 512, 8)

    def body(x_ref, g_ref, t_ref, loss_ref, dx_ref, dg_ref):
        n, r = _rms(x_ref[...])
        diff = n * g_ref[...] - t_ref[...]
        part = 0.5 * jnp.sum(jnp.sum(diff * diff, axis=1, keepdims=True), axis=0, keepdims=True) / d
        _accumulate(loss_ref, part)
        dy = diff / d
        _accumulate(dg_ref, jnp.sum(dy * n, axis=0, keepdims=True))
        dx_ref[...] = _rms_bwd(n, r, dy * g_ref[...])

    return pl.pallas_call(
        body, name=name, grid=(s // tr,),
        out_shape=(jax.ShapeDtypeStruct((1, 1), F32), jax.ShapeDtypeStruct((s, d), F32), jax.ShapeDtypeStruct((1, d), F32)),
        in_specs=[_row_spec(tr, d), _vec_spec(d), _row_spec(tr, d)],
        out_specs=(pl.BlockSpec((1, 1), lambda i: (0, 0)), _row_spec(tr, d), _vec_spec(d)),
        compiler_params=_params(),
    )(x, final_g, target)


def _gate_bwd(dx, m, gate, name):
    s, d = dx.shape
    tr = _pick(s, 512, 8)

    def body(dx_ref, m_ref, g_ref, dm_ref, dg_ref):
        dxv = dx_ref[...]
        dm_ref[...] = (dxv * g_ref[...]).astype(BF16)
        _accumulate(dg_ref, jnp.sum(dxv * m_ref[...], axis=0, keepdims=True))

    return pl.pallas_call(
        body, name=name, grid=(s // tr,),
        out_shape=(jax.ShapeDtypeStruct((s, d), BF16), jax.ShapeDtypeStruct((1, d), F32)),
        in_specs=[_row_spec(tr, d), _row_spec(tr, d), _vec_spec(d)], out_specs=(_row_spec(tr, d), _vec_spec(d)),
        compiler_params=_params(),
    )(dx, m, gate)


def _norm_mod_bwd(x, dh, dres, scale, name):
    s, d = x.shape
    tr = _pick(s, 512, 8)

    def body(x_ref, dh_ref, dr_ref, sc_ref, dx_ref, dsc_ref, dsh_ref):
        n, r = _rms(x_ref[...])
        dh_ = dh_ref[...]
        _accumulate(dsc_ref, jnp.sum(dh_ * n, axis=0, keepdims=True))
        _accumulate(dsh_ref, jnp.sum(dh_, axis=0, keepdims=True))
        dx_ref[...] = dr_ref[...] + _rms_bwd(n, r, dh_ * (1.0 + sc_ref[...]))

    return pl.pallas_call(
        body, name=name, grid=(s // tr,),
        out_shape=(jax.ShapeDtypeStruct((s, d), F32), jax.ShapeDtypeStruct((1, d), F32), jax.ShapeDtypeStruct((1, d), F32)),
        in_specs=[_row_spec(tr, d), _row_spec(tr, d), _row_spec(tr, d), _vec_spec(d)],
        out_specs=(_row_spec(tr, d), _vec_spec(d), _vec_spec(d)), compiler_params=_params(),
    )(x, dh, dres, scale)


def _out_norm_bwd(oa, ob, dcat, g_a, g_b, name):
    s, w = oa.shape
    tr = _pick(s, 512, 8)

    def body(oa_ref, ob_ref, dc_ref, ga_ref, gb_ref, doa_ref, dob_ref, dga_ref, dgb_ref):
        for o_ref, g_ref, do_ref, dg_ref, lo in ((oa_ref, ga_ref, doa_ref, dga_ref, 0), (ob_ref, gb_ref, dob_ref, dgb_ref, w)):
            n, r = _rms(o_ref[...])
            dc = dc_ref[:, lo:lo + w]
            _accumulate(dg_ref, jnp.sum(dc * n, axis=0, keepdims=True))
            do_ref[...] = _rms_bwd(n, r, dc * g_ref[...])

    return pl.pallas_call(
        body, name=name, grid=(s // tr,),
        out_shape=(jax.ShapeDtypeStruct((s, w), F32), jax.ShapeDtypeStruct((s, w), F32),
                   jax.ShapeDtypeStruct((1, w), F32), jax.ShapeDtypeStruct((1, w), F32)),
        in_specs=[_row_spec(tr, w), _row_spec(tr, w), _row_spec(tr, 2 * w), _vec_spec(w), _vec_spec(w)],
        out_specs=(_row_spec(tr, w), _row_spec(tr, w), _vec_spec(w), _vec_spec(w)), compiler_params=_params(),
    )(oa, ob, dcat, g_a, g_b)


def _head_masks():
    lane = lax.broadcasted_iota(jnp.int32, (1, LANES), 1)
    return lane < HEAD_DIM, lane >= HEAD_DIM


def _nt(a, b):
    return lax.dot_general(a, b, (((1,), (1,)), ((), ())), preferred_element_type=F32)


def _tn(a, b):
    return lax.dot_general(a, b, (((0,), (0,)), ((), ())), preferred_element_type=F32)


def _nn(a, b):
    return jnp.dot(a, b, preferred_element_type=F32)


def _only(mask, v):
    return jnp.where(mask, v, jnp.zeros_like(v))


def _fill_padded(dst_ref, src_ref):
    dst_ref[0:PAD, :] = jnp.zeros((PAD, LANES), dst_ref.dtype)
    dst_ref[PAD:, :] = src_ref[...]


def _chunk_probs(qh, kb, bias, chunk):
    s = _nt(qh, kb) * (HEAD_DIM ** -0.5) + bias
    pos = lax.broadcasted_iota(jnp.int32, (1, BAND), 1)
    s = jnp.where(pos >= (N_PREV - chunk) * CHUNK, s, -1e30)
    e = jnp.exp(s - jnp.max(s, axis=1, keepdims=True))
    return e / jnp.sum(e, axis=1, keepdims=True)


def _attn_a_fwd(proj, band_bias, name):
    s = proj.shape[0]
    cq = 4
    tq = cq * CHUNK
    npair = N_HEADS // 2
    kcol, vcol = W_GRP // LANES, 2 * W_GRP // LANES

    def body(q_ref, k_ref, v_ref, b_ref, o_ref, kpad, vpad):
        i = pl.program_id(1)
        masks = _head_masks()

        @pl.when(i == 0)
        def _():
            _fill_padded(kpad, k_ref)
            _fill_padded(vpad, v_ref)

        for cc in range(cq):
            chunk = i * cq + cc
            start = pl.multiple_of(chunk * CHUNK, CHUNK)
            kb = kpad[pl.ds(start, BAND), :]
            vb = vpad[pl.ds(start, BAND), :]
            qc = q_ref[cc * CHUNK:(cc + 1) * CHUNK, :]
            out = jnp.zeros((CHUNK, LANES), F32)
            for h in range(2):
                p = _chunk_probs(_only(masks[h], qc), kb, b_ref[h], chunk)
                out = out + _nn(p.astype(BF16), _only(masks[h], vb))
            o_ref[cc * CHUNK:(cc + 1) * CHUNK, :] = out

    return pl.pallas_call(
        body, name=name, out_shape=jax.ShapeDtypeStruct((s, W_GRP), F32), grid=(npair, s // tq),
        in_specs=[pl.BlockSpec((tq, LANES), lambda p, i: (i, p)),
                  pl.BlockSpec((s, LANES), lambda p, i: (0, kcol + p)),
                  pl.BlockSpec((s, LANES), lambda p, i: (0, vcol + p)),
                  pl.BlockSpec((2, CHUNK, BAND), lambda p, i: (p, 0, 0))],
        out_specs=pl.BlockSpec((tq, LANES), lambda p, i: (i, p)),
        scratch_shapes=[pltpu.VMEM((s + PAD, LANES), BF16), pltpu.VMEM((s + PAD, LANES), BF16)],
        compiler_params=_params(),
    )(proj, proj, proj, band_bias)


def _attn_a_bwd(proj, band_bias, doa, name, rider=None):
    s = proj.shape[0]
    cq = 4
    tq = cq * CHUNK
    nq = s // tq
    npair = N_HEADS // 2
    kcol, vcol = W_GRP // LANES, 2 * W_GRP // LANES
    scale = HEAD_DIM ** -0.5

    def body(q_ref, k_ref, v_ref, b_ref, do_ref, dq_ref, dk_ref, dv_ref, db_ref, kpad, vpad, dkpad, dvpad):
        i = pl.program_id(1)
        masks = _head_masks()

        @pl.when(i == 0)
        def _():
            _fill_padded(kpad, k_ref)
            _fill_padded(vpad, v_ref)
            dkpad[...] = jnp.zeros_like(dkpad)
            dvpad[...] = jnp.zeros_like(dvpad)
            db_ref[...] = jnp.zeros_like(db_ref)

        for cc in range(cq):
            chunk = i * cq + cc
            start = pl.multiple_of(chunk * CHUNK, CHUNK)
            kb = kpad[pl.ds(start, BAND), :]
            vb = vpad[pl.ds(start, BAND), :]
            qc = q_ref[cc * CHUNK:(cc + 1) * CHUNK, :]
            doc = do_ref[cc * CHUNK:(cc + 1) * CHUNK, :].astype(BF16)
            dq = jnp.zeros((CHUNK, LANES), F32)
            dk = jnp.zeros((BAND, LANES), F32)
            dv = jnp.zeros((BAND, LANES), F32)
            for h in range(2):
                qh = _only(masks[h], qc)
                doh = _only(masks[h], doc)
                p = _chunk_probs(qh, kb, b_ref[h], chunk)
                dp = _nt(doh, vb)
                ds = p * (dp - jnp.sum(p * dp, axis=1, keepdims=True))
                db_ref[h] += ds
                dsb = (ds * scale).astype(BF16)
                dq = dq + _nn(dsb, _only(masks[h], kb))
                dk = dk + _tn(dsb, qh)
                dv = dv + _tn(p.astype(BF16), doh)
            dq_ref[cc * CHUNK:(cc + 1) * CHUNK, :] = dq
            dkpad[pl.ds(start, BAND), :] += dk
            dvpad[pl.ds(start, BAND), :] += dv

        @pl.when(i == nq - 1)
        def _():
            dk_ref[...] = dkpad[PAD:, :]
            dv_ref[...] = dvpad[PAD:, :]

    blk = pl.BlockSpec((tq, LANES), lambda p, i: (i, p))
    whole = pl.BlockSpec((s, LANES), lambda p, i: (0, p))
    bias_spec = pl.BlockSpec((2, CHUNK, BAND), lambda p, i: (p, 0, 0))
    return _call_with_rider(
        body, rider, name=name, grid=(npair, nq),
        out_shape=(jax.ShapeDtypeStruct((s, W_GRP), F32),) * 3 + (jax.ShapeDtypeStruct((N_HEADS, CHUNK, BAND), F32),),
        in_specs=[blk, pl.BlockSpec((s, LANES), lambda p, i: (0, kcol + p)),
                  pl.BlockSpec((s, LANES), lambda p, i: (0, vcol + p)), bias_spec, blk],
        out_specs=(blk, whole, whole, bias_spec),
        scratch_shapes=[pltpu.VMEM((s + PAD, LANES), BF16), pltpu.VMEM((s + PAD, LANES), BF16),
                        pltpu.VMEM((s + PAD, LANES), F32), pltpu.VMEM((s + PAD, LANES), F32)],
        args=(proj, proj, proj, band_bias, doa))


def _split3(v):
    hi = v.astype(BF16)
    r1 = v - hi.astype(F32)
    mid = r1.astype(BF16)
    lo = (r1 - mid.astype(F32)).astype(BF16)
    return hi, mid, lo


def _rel_bias_grad(dband_t, name):
    width = 3 * LANES

    def body(t_ref, o_ref):
        pos = lax.broadcasted_iota(jnp.int32, (BAND, width), 0)
        col = lax.broadcasted_iota(jnp.int32, (BAND, width), 1)
        acc = jnp.zeros((N_HEADS, width), F32)
        for q in range(CHUNK):
            idx = jnp.minimum(PAD + q - pos, REL_CLIP) + REL_CLIP
            onehot = (col == idx).astype(BF16)
            for part in _split3(t_ref[q]):
                acc = acc + _nn(part, onehot)
        o_ref[...] = acc

    return pl.pallas_call(
        body, name=name, out_shape=jax.ShapeDtypeStruct((N_HEADS, width), F32),
        in_specs=[VMEM_FULL], out_specs=VMEM_FULL, compiler_params=_params(),
    )(dband_t)


def _split2_wide(v):
    hi = v.astype(BF16)
    return jnp.concatenate([hi, (v - hi.astype(F32)).astype(BF16)], axis=1)


def _sb_logs(qs, kb, lower):
    z = _nt(qs, kb)
    e = jnp.exp(-jnp.abs(z))
    lb = jnp.minimum(z, 0.0) - jnp.log(1.0 + e)
    lk = lb - z
    if lower is not None:
        lk = jnp.where(lower, lk, 0.0)
    return z, e, lb, lk


def _tri_masks(tq):
    row = lax.broadcasted_iota(jnp.int32, (tq, tq), 0)
    col = lax.broadcasted_iota(jnp.int32, (tq, tq), 1)
    return row, col


def _stack2(m):
    return jnp.concatenate([m, m], axis=0).astype(BF16)


def _sb_fwd(proj, name, rider=None):
    s = proj.shape[0]
    tq = _pick(s, 256)
    nq = s // tq
    npair = N_HEADS // 2
    qcol, kcol, vcol = 3 * W_GRP // LANES, 4 * W_GRP // LANES, 5 * W_GRP // LANES

    def body(q_ref, k_ref, v_ref, o_ref, l_ref):
        i = pl.program_id(1)
        masks = _head_masks()
        q2 = q_ref[...] * (HEAD_DIM ** -0.5)
        qs = [_only(m, q2) for m in masks]
        row, col = _tri_masks(tq)
        lower = row > col
        after2 = _stack2(lower)

        def tile(jj, carry, diag):
            acc, tails = carry
            ks = pl.multiple_of((i - jj) * tq, tq)
            kb = k_ref[pl.ds(ks, tq), :]
            vb = v_ref[pl.ds(ks, tq), :]
            new_tails = []
            for h in range(2):
                _, _, lb, lk = _sb_logs(qs[h], kb, lower if diag else None)
                loc = _nn(_split2_wide(lk), after2)
                a = jnp.exp(lb + (loc + tails[h]))
                if diag:
                    a = jnp.where(lower, a, 0.0)
                acc = acc + _nn(a.astype(BF16), _only(masks[h], vb))
                new_tails.append(tails[h] + (loc[:, 0:1] + lk[:, 0:1]))
            return acc, tuple(new_tails)

        zero = jnp.zeros((tq, 1), F32)
        carry = tile(0, (jnp.zeros((tq, LANES), F32), (zero, zero)), True)
        acc, tails = lax.fori_loop(1, i + 1, lambda jj, cr: tile(jj, cr, False), carry)
        o_ref[...] = acc
        l_ref[:, 0:1] = tails[0]
        l_ref[:, 1:2] = tails[1]

    return _call_with_rider(
        body, rider, name=name, grid=(npair, nq),
        out_shape=(jax.ShapeDtypeStruct((s, W_GRP), F32), jax.ShapeDtypeStruct((npair, s, 2), F32)),
        in_specs=[pl.BlockSpec((tq, LANES), lambda p, i: (i, qcol + p)),
                  pl.BlockSpec((s, LANES), lambda p, i: (0, kcol + p)),
                  pl.BlockSpec((s, LANES), lambda p, i: (0, vcol + p))],
        out_specs=(pl.BlockSpec((tq, LANES), lambda p, i: (i, p)), pl.BlockSpec((None, tq, 2), lambda p, i: (p, i, 0))),
        scratch_shapes=[], args=(proj, proj, proj))


def _sb_bwd(proj, ltot, dob, name, rider=None):
    s = proj.shape[0]
    tq = _pick(s, 256)
    nq = s // tq
    npair = N_HEADS // 2
    qcol, kcol, vcol = 3 * W_GRP // LANES, 4 * W_GRP // LANES, 5 * W_GRP // LANES
    scale = HEAD_DIM ** -0.5

    def body(q_ref, k_ref, v_ref, l_ref, do_ref, dq_ref, dk_ref, dv_ref):
        i = pl.program_id(1)
        masks = _head_masks()

        @pl.when(i == 0)
        def _():
            dk_ref[...] = jnp.zeros_like(dk_ref)
            dv_ref[...] = jnp.zeros_like(dv_ref)

        q2 = q_ref[...] * scale
        do2 = do_ref[...]
        qs = [_only(m, q2) for m in masks]
        doh = [_only(m, do2).astype(BF16) for m in masks]
        ltots = [l_ref[:, 0:1], l_ref[:, 1:2]]
        row, col = _tri_masks(tq)
        lower = row > col
        upto2 = _stack2(row <= col)
        before2 = _stack2(row < col)

        def tile(j, carry, diag):
            dq, heads, gsums = carry
            ks = pl.multiple_of(j * tq, tq)
            kb = k_ref[pl.ds(ks, tq), :]
            vb = v_ref[pl.ds(ks, tq), :]
            dk = jnp.zeros((tq, LANES), F32)
            dv = jnp.zeros((tq, LANES), F32)
            new_heads, new_gsums = [], []
            for h in range(2):
                z, e, lb, lk = _sb_logs(qs[h], kb, lower if diag else None)
                loc = _nn(_split2_wide(lk), upto2)
                a = jnp.exp(lb + (ltots[h] - (heads[h] + loc)))
                if diag:
                    a = jnp.where(lower, a, 0.0)
                g = a * _nt(doh[h], vb)
                gloc = _nn(_split2_wide(g), before2)
                r = 1.0 / (1.0 + e)
                sig = jnp.where(z >= 0, r, e * r)
                dz = g - sig * (g + (gsums[h] + gloc))
                if diag:
                    dz = jnp.where(lower, dz, 0.0)
                dzb = dz.astype(BF16)
                dq = dq + _nn(dzb, _only(masks[h], kb))
                dk = dk + _tn(dzb, qs[h])
                dv = dv + _tn(a.astype(BF16), doh[h])
                new_heads.append(heads[h] + loc[:, tq - 1:tq])
                new_gsums.append(gsums[h] + (gloc[:, tq - 1:tq] + g[:, tq - 1:tq]))
            dk_ref[pl.ds(ks, tq), :] += dk
            dv_ref[pl.ds(ks, tq), :] += dv
            return dq, tuple(new_heads), tuple(new_gsums)

        zero = jnp.zeros((tq, 1), F32)
        carry = (jnp.zeros((tq, LANES), F32), (zero, zero), (zero, zero))
        carry = lax.fori_loop(0, i, lambda j, cr: tile(j, cr, False), carry)
        dq, _, _ = tile(i, carry, True)
        dq_ref[...] = dq * scale

    blk = pl.BlockSpec((tq, LANES), lambda p, i: (i, p))
    whole = pl.BlockSpec((s, LANES), lambda p, i: (0, p))
    return _call_with_rider(
        body, rider, name=name, grid=(npair, nq), out_shape=(jax.ShapeDtypeStruct((s, W_GRP), F32),) * 3,
        in_specs=[pl.BlockSpec((tq, LANES), lambda p, i: (i, qcol + p)),
                  pl.BlockSpec((s, LANES), lambda p, i: (0, kcol + p)),
                  pl.BlockSpec((s, LANES), lambda p, i: (0, vcol + p)),
                  pl.BlockSpec((None, tq, 2), lambda p, i: (p, i, 0)), blk],
        out_specs=(blk, whole, whole), scratch_shapes=[], args=(proj, proj, proj, ltot, dob))


def _ada_fwd(c_all, w_ada, b_ada, name):
    nl, d, n = w_ada.shape
    tn = _pick(n, 512)

    def body(c_ref, w_ref, b_ref, o_ref):
        cv = c_ref[...]
        act = (cv * jax.nn.sigmoid(cv)).astype(BF16)
        o_ref[...] = _nn(act, w_ref[...].astype(BF16)) + b_ref[...]

    return pl.pallas_call(
        body, name=name, out_shape=jax.ShapeDtypeStruct((nl, N_DEV, n), F32), grid=(nl, n // tn),
        in_specs=[pl.BlockSpec((N_DEV, d), lambda l, j: (0, 0)), pl.BlockSpec((None, d, tn), lambda l, j: (l, 0, j)),
                  pl.BlockSpec((None, 1, tn), lambda l, j: (l, 0, j))],
        out_specs=pl.BlockSpec((None, N_DEV, tn), lambda l, j: (l, 0, j)), compiler_params=_params(),
    )(c_all, w_ada, b_ada)


def _ada_bwd(c_all, dmod, name):
    nl, _, n = dmod.shape
    d = c_all.shape[1]
    tn = _pick(n, 512)

    def body(c_ref, g_ref, o_ref):
        cv = c_ref[...]
        act = (cv * jax.nn.sigmoid(cv)).astype(BF16)
        o_ref[...] = _tn(act, g_ref[...].astype(BF16))

    return pl.pallas_call(
        body, name=name, out_shape=jax.ShapeDtypeStruct((nl, d, n), F32), grid=(nl, n // tn),
        in_specs=[pl.BlockSpec((N_DEV, d), lambda l, j: (0, 0)), pl.BlockSpec((None, N_DEV, tn), lambda l, j: (l, 0, j))],
        out_specs=pl.BlockSpec((None, d, tn), lambda l, j: (l, 0, j)), compiler_params=_params(),
    )(c_all, dmod)


def _adamw(g, w, m, v, name):
    r, c = g.shape
    tr = _pick(r, 512, 8)
    c1 = 1.0 - ADAM_B1 ** ADAM_STEP
    c2 = 1.0 - ADAM_B2 ** ADAM_STEP

    def body(g_ref, w_ref, m_ref, v_ref, d_ref, nm_ref, nv_ref):
        gv = g_ref[...]
        nm = ADAM_B1 * m_ref[...] + (1.0 - ADAM_B1) * gv
        nv = ADAM_B2 * v_ref[...] + (1.0 - ADAM_B2) * (gv * gv)
        d_ref[...] = -ADAM_LR * ((nm / c1) / (jnp.sqrt(nv / c2) + ADAM_EPS) + ADAM_WD * w_ref[...])
        nm_ref[...] = nm
        nv_ref[...] = nv

    spec = pl.BlockSpec((tr, c), lambda i: (i, 0))
    return pl.pallas_call(
        body, name=name, out_shape=(jax.ShapeDtypeStruct((r, c), F32),) * 3, grid=(r // tr,),
        in_specs=[spec] * 4, out_specs=(spec,) * 3, compiler_params=_params(),
    )(g, w, m, v)


def _adamw_nd(g, w, m, v, name):
    shape = w.shape
    two_d = (1, shape[0]) if len(shape) == 1 else (-1, shape[-1])
    outs = _adamw(*(t.reshape(two_d) for t in (g, w, m, v)), name=name)
    return tuple(o.reshape(shape) for o in outs)


def _allgather8(v, name):
    m, n = v.shape

    def body(v_ref, out_ref, send_sems, recv_sems, local_sem):
        x, y, c = _my_place()

        def rows(px, py, pc):
            return out_ref.at[pl.ds(pl.multiple_of((4 * px + 2 * py + pc) * m, 8), m), :]

        def peer(k):
            return _flip(x, k & 4), _flip(y, k & 2), _flip(c, k & 1)

        def copy(k, block):
            return pltpu.make_async_remote_copy(
                src_ref=v_ref, dst_ref=rows(*block), send_sem=send_sems.at[k - 1], recv_sem=recv_sems.at[k - 1],
                device_id=peer(k), device_id_type=MESH)

        mine = pltpu.make_async_copy(v_ref, rows(x, y, c), local_sem)
        mine.start()
        sends = [copy(k, (x, y, c)) for k in range(1, N_DEV)]
        for cp in sends:
            cp.start()
        for k in range(1, N_DEV):
            copy(k, peer(k)).wait_recv()
        for cp in sends:
            cp.wait_send()
        mine.wait()

    return pl.pallas_call(
        body, name=name, out_shape=jax.ShapeDtypeStruct((N_DEV * m, n), v.dtype),
        in_specs=[VMEM_FULL], out_specs=VMEM_FULL,
        scratch_shapes=[pltpu.SemaphoreType.DMA((N_DEV - 1,)), pltpu.SemaphoreType.DMA((N_DEV - 1,)),
                        pltpu.SemaphoreType.DMA],
        compiler_params=_params(),
    )(v)


def _chip_peers(x, y, c):
    out = []
    for k in range(1, N_CHIPS):
        px, py = _flip(x, k & 2), _flip(y, k & 1)
        out.append((2 * px + py, (px, py, c)))
    return out


def _gather_weights(shards, kinds, name):
    nw = len(shards)

    def full_shape(a, kind):
        l, r, n = a.shape
        return (l, r, N_CHIPS * n) if kind == "col" else (l, N_CHIPS * r, n)

    def body(*refs):
        ins, outs = refs[:nw], refs[nw:2 * nw]
        send_sems, recv_sems, local_sems = refs[2 * nw:]
        x, y, c = _my_place()
        chip = 2 * x + y

        def window(w, j):
            _, r, n = shards[w].shape
            if kinds[w] == "col":
                return outs[w].at[:, :, pl.ds(pl.multiple_of(j * n, LANES), n)]
            return outs[w].at[:, pl.ds(pl.multiple_of(j * r, 16), r), :]

        def copy(w, k, j, peer):
            return pltpu.make_async_remote_copy(
                src_ref=ins[w], dst_ref=window(w, j), send_sem=send_sems.at[3 * w + k], recv_sem=recv_sems.at[3 * w + k],
                device_id=peer, device_id_type=MESH)

        local = [pltpu.make_async_copy(ins[w], window(w, chip), local_sems.at[w]) for w in range(nw)]
        for cp in local:
            cp.start()
        peers = _chip_peers(x, y, c)
        sends = [copy(w, k, chip, peer) for w in range(nw) for k, (_, peer) in enumerate(peers)]
        for cp in sends:
            cp.start()
        for w in range(nw):
            for k, (pchip, peer) in enumerate(peers):
                copy(w, k, pchip, peer).wait_recv()
        for cp in sends:
            cp.wait_send()
        for cp in local:
            cp.wait()

    return pl.pallas_call(
        body, name=name,
        out_shape=tuple(jax.ShapeDtypeStruct(full_shape(a, kd), a.dtype) for a, kd in zip(shards, kinds)),
        in_specs=[ANY] * nw, out_specs=(ANY,) * nw,
        scratch_shapes=[pltpu.SemaphoreType.DMA((3 * nw,)), pltpu.SemaphoreType.DMA((3 * nw,)),
                        pltpu.SemaphoreType.DMA((nw,))],
        compiler_params=_params(),
    )(*shards)


def _rs_to_sibling(grads, name):
    nw = len(grads)

    def body(*refs):
        ins, outs = refs[:nw], refs[nw:2 * nw]
        send_sems, recv_sems = refs[2 * nw:]
        x, y, c = _my_place()
        sibling = (x, y, 1 - c)
        copies = [pltpu.make_async_remote_copy(
            src_ref=ins[w].at[j, 1 - c], dst_ref=outs[w].at[j], send_sem=send_sems.at[N_CHIPS * w + j],
            recv_sem=recv_sems.at[N_CHIPS * w + j], device_id=sibling, device_id_type=MESH)
            for w in range(nw) for j in range(N_CHIPS)]
        for cp in copies:
            cp.start()
        for cp in copies:
            cp.wait_recv()
        for cp in copies:
            cp.wait_send()

    return pl.pallas_call(
        body, name=name,
        out_shape=tuple(jax.ShapeDtypeStruct((N_CHIPS,) + g.shape[2:], g.dtype) for g in grads),
        in_specs=[ANY] * nw, out_specs=(ANY,) * nw,
        scratch_shapes=[pltpu.SemaphoreType.DMA((N_CHIPS * nw,)), pltpu.SemaphoreType.DMA((N_CHIPS * nw,))],
        compiler_params=_params(),
    )(*grads)


def _rs_to_chips(parts, name):
    nw = len(parts)

    def body(*refs):
        ins, outs = refs[:nw], refs[nw:2 * nw]
        send_sems, recv_sems, local_sems = refs[2 * nw:]
        x, y, c = _my_place()
        chip = 2 * x + y
        peers = _chip_peers(x, y, c)

        def copy(w, k, src_slab, dst_slab, peer):
            return pltpu.make_async_remote_copy(
                src_ref=ins[w].at[src_slab], dst_ref=outs[w].at[dst_slab], send_sem=send_sems.at[3 * w + k],
                recv_sem=recv_sems.at[3 * w + k], device_id=peer, device_id_type=MESH)

        local = [pltpu.make_async_copy(ins[w].at[chip], outs[w].at[chip], local_sems.at[w]) for w in range(nw)]
        for cp in local:
            cp.start()
        sends = [copy(w, k, pchip, chip, peer) for w in range(nw) for k, (pchip, peer) in enumerate(peers)]
        for cp in sends:
            cp.start()
        for w in range(nw):
            for k, (pchip, peer) in enumerate(peers):
                copy(w, k, chip, pchip, peer).wait_recv()
        for cp in sends:
            cp.wait_send()
        for cp in local:
            cp.wait()

    return pl.pallas_call(
        body, name=name, out_shape=tuple(jax.ShapeDtypeStruct(p.shape, p.dtype) for p in parts),
        in_specs=[ANY] * nw, out_specs=(ANY,) * nw,
        scratch_shapes=[pltpu.SemaphoreType.DMA((3 * nw,)), pltpu.SemaphoreType.DMA((3 * nw,)),
                        pltpu.SemaphoreType.DMA((nw,))],
        compiler_params=_params(),
    )(*parts)


def _rs_share_halves(halves, name):
    nw = len(halves)
    nl = len(halves[0])
    flat = [h for hs in halves for h in hs]

    def body(*refs):
        ins, outs = refs[:nw * nl], refs[nw * nl:nw * nl + nw]
        send_sems, recv_sems, local_sems = refs[nw * nl + nw:]
        x, y, c = _my_place()
        sibling = (x, y, 1 - c)
        local, sends, recvs = [], [], []
        for w in range(nw):
            for l in range(nl):
                n = nl * w + l
                local.append(pltpu.make_async_copy(ins[n], outs[w].at[l, c], local_sems.at[n]))
                sends.append(pltpu.make_async_remote_copy(
                    src_ref=ins[n], dst_ref=outs[w].at[l, c], send_sem=send_sems.at[n], recv_sem=recv_sems.at[n],
                    device_id=sibling, device_id_type=MESH))
                recvs.append(pltpu.make_async_remote_copy(
                    src_ref=ins[n], dst_ref=outs[w].at[l, 1 - c], send_sem=send_sems.at[n], recv_sem=recv_sems.at[n],
                    device_id=sibling, device_id_type=MESH))
        for cp in local + sends:
            cp.start()
        for cp in recvs:
            cp.wait_recv()
        for cp in sends:
            cp.wait_send()
        for cp in local:
            cp.wait()

    return pl.pallas_call(
        body, name=name,
        out_shape=tuple(jax.ShapeDtypeStruct((nl, 2) + hs[0].shape, hs[0].dtype) for hs in halves),
        in_specs=[ANY] * (nw * nl), out_specs=(ANY,) * nw,
        scratch_shapes=[pltpu.SemaphoreType.DMA((nw * nl,)), pltpu.SemaphoreType.DMA((nw * nl,)),
                        pltpu.SemaphoreType.DMA((nw * nl,))],
        compiler_params=_params(),
    )(*flat)


def _add_own_half(grad, got, c_idx, name):
    _, _, r, n = grad.shape
    tr = _pick(r, 256, 8)

    def body(c_ref, g_ref, t_ref, o_ref):
        o_ref[...] = g_ref[...] + t_ref[...]

    return pl.pallas_call(
        body, name=name, out_shape=jax.ShapeDtypeStruct((N_CHIPS, r, n), F32),
        grid_spec=pltpu.PrefetchScalarGridSpec(
            num_scalar_prefetch=1, grid=(N_CHIPS, r // tr),
            in_specs=[pl.BlockSpec((None, None, tr, n), lambda j, i, c_ref: (j, c_ref[0], i, 0)),
                      pl.BlockSpec((None, tr, n), lambda j, i, c_ref: (j, i, 0))],
            out_specs=pl.BlockSpec((None, tr, n), lambda j, i, c_ref: (j, i, 0))),
        compiler_params=_params(),
    )(c_idx, grad, got)


def _sum_slabs(slabs, name):
    ns, r, n = slabs.shape
    tr = _pick(r, 256, 8)

    def body(s_ref, o_ref):
        acc = s_ref[0]
        for j in range(1, ns):
            acc = acc + s_ref[j]
        o_ref[...] = acc

    return pl.pallas_call(
        body, name=name, out_shape=jax.ShapeDtypeStruct((r, n), F32), grid=(r // tr,),
        in_specs=[pl.BlockSpec((ns, tr, n), lambda i: (0, i, 0))], out_specs=pl.BlockSpec((tr, n), lambda i: (i, 0)),
        compiler_params=_params(),
    )(slabs)


def _band_bias(rel_bias):
    h = rel_bias.shape[0]
    n_far = PAD - REL_CLIP + CHUNK
    far = jnp.broadcast_to(rel_bias[:, N_REL - 1:N_REL], (h, n_far))
    near = rel_bias[:, REL_CLIP - CHUNK + 1:N_REL - 1][:, ::-1]
    line = jnp.concatenate([far, near], axis=1)
    return jnp.stack([line[:, CHUNK - 1 - q:CHUNK - 1 - q + BAND] for q in range(CHUNK)], axis=1)


def _pack_rows(pieces):
    flat = jnp.concatenate([p.reshape(-1) for p in pieces])
    rows = -(-flat.shape[0] // (8 * LANES)) * 8
    return jnp.pad(flat, (0, rows * LANES - flat.shape[0])).reshape(rows, LANES)


def _unpack_rows(packed, shapes):
    flat = packed.reshape(-1)
    out, at = [], 0
    for shp in shapes:
        size = 1
        for n in shp:
            size *= n
        out.append(flat[at:at + size].reshape(shp))
        at += size
    return out


def _layer_fwd(x, mod, w, band, tag):
    s, d = x.shape
    row = lambda i: mod[i:i + 1]
    h1 = _norm_mod(x, row(1), row(0), f"norm_mix{tag}")
    proj = _matmul(h1, w["w_in"], form="nn", out_dtype=BF16, tm=_pick(s, 512), tn=_pick(w["w_in"].shape[1], 768),
                   tk=d, name=f"proj{tag}")
    oa = _attn_a_fwd(proj, band, f"attn_a{tag}")
    ob, ltot = _sb_fwd(proj, f"attn_b{tag}")
    cat = _out_norm(oa, ob, w["g_a"], w["g_b"], f"out_norm{tag}")
    mixed = _matmul(cat, w["w_out"], form="nn", out_dtype=F32, tm=_pick(s, 512), tn=_pick(d, 1024),
                    tk=cat.shape[1], name=f"mix_out{tag}")
    x1 = _residual(x, row(2), mixed, f"res_mix{tag}")
    h2 = _norm_mod(x1, row(4), row(3), f"norm_ffn{tag}")
    f2 = w["w_up"].shape[1]
    u = _matmul(h2, w["w_up"], form="nn", out_dtype=F32, tm=_pick(s, 512), tn=_pick(f2, 1408), tk=d, name=f"up{tag}")
    a = _conv_glu(u, w["conv_w"], w["conv_b"], f"conv_glu{tag}")
    f = _matmul(a, w["w_down"], form="nn", out_dtype=F32, tm=_pick(s, 512), tn=_pick(d, 1024),
                tk=_pick(f2 // 2, 1408), name=f"down{tag}")
    x2 = _residual(x1, row(5), f, f"res_ffn{tag}")
    saved = dict(x=x, h1=h1, proj=proj, oa=oa, ob=ob, ltot=ltot, cat=cat, mixed=mixed, x1=x1, h2=h2, u=u, a=a, f=f)
    return x2, saved


def _layer_bwd(dx2, sv, mod, w, band, tag):
    s, d = dx2.shape
    row = lambda i: mod[i:i + 1]
    f2 = w["w_up"].shape[1]
    ff = f2 // 2
    n_in = w["w_in"].shape[1]
    df, dgate_ffn = _gate_bwd(dx2, sv["f"], row(5), f"gate_ffn_bwd{tag}")
    da = _matmul(df, w["w_down"], form="nt", out_dtype=F32, tm=_pick(s, 512), tn=_pick(ff, 1408), tk=d, name=f"down_dx{tag}")
    g_down = _matmul(sv["a"], df, form="tn", out_dtype=F32, tm=_pick(ff, 1408), tn=_pick(d, 512), tk=_pick(s, 512),
                     name=f"down_dw{tag}")
    du2, dcw, dcb = _conv_glu_bwd(sv["u"], da, w["conv_w"], w["conv_b"], f"conv_glu_bwd{tag}")
    du = jnp.concatenate([du2[0], du2[1]], axis=1)
    dh2 = _matmul(du, w["w_up"], form="nt", out_dtype=F32, tm=_pick(s, 512), tn=_pick(d, 1024), tk=_pick(f2, 1408),
                  name=f"up_dx{tag}")
    g_up = _matmul(sv["h2"], du, form="tn", out_dtype=F32, tm=_pick(d, 512), tn=_pick(f2 // N_CHIPS, 1408),
                   tk=_pick(s, 512), name=f"up_dw{tag}", shard_cols=f2 // N_CHIPS)
    dx1, dscale_ffn, dshift_ffn = _norm_mod_bwd(sv["x1"], dh2, dx2, row(4), f"norm_ffn_bwd{tag}")
    dmixed, dgate_mix = _gate_bwd(dx1, sv["mixed"], row(2), f"gate_mix_bwd{tag}")
    dcat = _matmul(dmixed, w["w_out"], form="nt", out_dtype=F32, tm=_pick(s, 512), tn=_pick(2 * W_GRP, 1024), tk=d,
                   name=f"mix_out_dx{tag}")
    g_out = _matmul(sv["cat"], dmixed, form="tn", out_dtype=F32, tm=_pick(2 * W_GRP, 512), tn=_pick(d, 1024),
                    tk=_pick(s, 512), name=f"mix_out_dw{tag}")
    doa, dob, dg_a, dg_b = _out_norm_bwd(sv["oa"], sv["ob"], dcat, w["g_a"], w["g_b"], f"out_norm_bwd{tag}")
    dqa, dka, dva, dband = _attn_a_bwd(sv["proj"], band, doa, f"attn_a_bwd{tag}")
    dqb, dkb, dvb = _sb_bwd(sv["proj"], sv["ltot"], dob, f"attn_b_bwd{tag}")
    drel = _rel_bias_grad(jnp.transpose(dband, (1, 0, 2)), f"rel_bias_bwd{tag}")[:, :N_REL]
    dproj = jnp.concatenate([dqa, dka, dva, dqb, dkb, dvb], axis=1).astype(BF16)
    dh1 = _matmul(dproj, w["w_in"], form="nt", out_dtype=F32, tm=_pick(s, 512), tn=_pick(d, 1024), tk=_pick(n_in, 1024),
                  name=f"proj_dx{tag}")
    g_in = _matmul(sv["h1"], dproj, form="tn", out_dtype=F32, tm=_pick(d, 512), tn=_pick(n_in // N_CHIPS, 768),
                   tk=_pick(s, 512), name=f"proj_dw{tag}", shard_cols=n_in // N_CHIPS)
    dx, dscale_mix, dshift_mix = _norm_mod_bwd(sv["x"], dh1, dx1, row(1), f"norm_mix_bwd{tag}")
    dmod = jnp.concatenate([dshift_mix, dscale_mix, dgate_mix, dshift_ffn, dscale_ffn, dgate_ffn], axis=1)
    big = dict(w_in=g_in, w_out=g_out, w_up=g_up, w_down=g_down)
    dconv_w = jnp.concatenate([dcw[0], dcw[1]], axis=1)
    dconv_b = jnp.concatenate([dcb[0], dcb[1]], axis=1)
    small = dict(dmod=dmod, rel_bias=drel, g_a=dg_a, g_b=dg_b, conv_w=dconv_w, conv_b=dconv_b)
    return dx, big, small


def _kernel_unoverlapped(x, c, w_ada, b_ada, w_in, rel_bias, g_a, g_b, w_out, w_up, conv_w, conv_b, w_down, final_g, loss_target, m_w_ada, m_b_ada, m_w_in, m_rel_bias, m_g_a, m_g_b, m_w_out, m_w_up, m_conv_w, m_conv_b, m_w_down, m_final_g, v_w_ada, v_b_ada, v_w_in, v_rel_bias, v_g_a, v_g_b, v_w_out, v_w_up, v_conv_w, v_conv_b, v_w_down, v_final_g):
    xi, yi, ci = _my_place()
    chip = 2 * xi + yi
    dev = 4 * xi + 2 * yi + ci
    nl, d, n_ada = w_ada.shape
    s = x.shape[1]
    f2 = N_CHIPS * w_up.shape[2]
    nc = conv_w.shape[2]

    c_pad = jnp.pad(c, ((0, 7), (0, 0)))
    c_all = _allgather8(c_pad, "gather_c")[0::8]
    b_mine = lax.dynamic_slice_in_dim(b_ada, chip * n_ada, n_ada, axis=1)[:, None, :]
    mod_shard = _ada_fwd(c_all, w_ada, b_mine, "ada")
    pack2 = _pack_rows([mod_shard, conv_w])
    got2 = _allgather8(pack2, "gather_mod").reshape(N_DEV, -1)
    mods, convs = [], []
    for j in range(N_CHIPS):
        ms, cw = _unpack_rows(got2[2 * j], [mod_shard.shape, conv_w.shape])
        mods.append(lax.dynamic_index_in_dim(ms, dev, axis=1, keepdims=False))
        convs.append(cw)
    mod = jnp.concatenate(mods, axis=1).reshape(nl, 6, d)
    conv_w_full = jnp.concatenate(convs, axis=2)

    names = ("w_in", "w_out", "w_up", "w_down")
    kinds = ("col", "row", "col", "row")
    shards = dict(w_in=w_in, w_out=w_out, w_up=w_up, w_down=w_down)
    full = _gather_weights([shards[n].astype(BF16) for n in names], kinds, "gather_weights")
    full = dict(zip(names, full))

    xs = x[0]
    layers, saved, bands = [], [], []
    for l in range(nl):
        w = {n: full[n][l] for n in names}
        w.update(g_a=g_a[l:l + 1], g_b=g_b[l:l + 1], conv_w=conv_w_full[l], conv_b=conv_b[l:l + 1])
        band = _band_bias(rel_bias[l])
        xs, sv = _layer_fwd(xs, mod[l], w, band, f"_l{l}")
        layers.append(w)
        bands.append(band)
        saved.append(sv)
    loss_part, dx, dfinal_g = _loss_head(xs, final_g[None, :], loss_target[0], "loss_head")
    loss = lax.psum(loss_part[0, 0], ("x", "y", "c"))

    big, small = [None] * nl, [None] * nl
    for l in reversed(range(nl)):
        dx, big[l], small[l] = _layer_bwd(dx, saved[l], mod[l], layers[l], bands[l], f"_l{l}")

    small_names = ("dmod", "rel_bias", "g_a", "g_b", "conv_w", "conv_b")
    pieces = [small[l][n] for l in range(nl) for n in small_names] + [dfinal_g]
    shapes = [p.shape for p in pieces]
    pack3 = _pack_rows(pieces)
    got3 = _allgather8(pack3, "gather_small").reshape(N_DEV, pack3.shape[0], LANES)
    summed = _unpack_rows(_sum_slabs(got3, "sum_small"), shapes)
    tot = [dict(zip(small_names, summed[len(small_names) * l:len(small_names) * (l + 1)])) for l in range(nl)]
    g_final_g = summed[-1].reshape(-1)
    g_b_ada = jnp.stack([tot[l]["dmod"].reshape(-1) for l in range(nl)])
    g_rel = jnp.stack([tot[l]["rel_bias"] for l in range(nl)])
    g_ga = jnp.stack([tot[l]["g_a"].reshape(-1) for l in range(nl)])
    g_gb = jnp.stack([tot[l]["g_b"].reshape(-1) for l in range(nl)])
    g_conv_b = jnp.stack([tot[l]["conv_b"].reshape(-1) for l in range(nl)])
    g_conv_w = jnp.stack([lax.dynamic_slice_in_dim(tot[l]["conv_w"], chip * nc, nc, axis=1) for l in range(nl)])
    per_dev = [_unpack_rows(got3[j], shapes) for j in range(N_DEV)]
    dmod_all = jnp.stack([jnp.stack([per_dev[j][len(small_names) * l].reshape(-1) for j in range(N_DEV)])
                          for l in range(nl)])
    g_w_ada = _ada_bwd(c_all, lax.dynamic_slice_in_dim(dmod_all, chip * n_ada, n_ada, axis=2), "ada_bwd")

    order = [(n, l) for n in names for l in range(nl)]
    flat_g = [big[l][n].reshape(N_CHIPS, 2, -1, 1024) for n, l in order]
    from_sib = _rs_to_sibling(flat_g, "rs_sibling")
    c_idx = jnp.reshape(ci, (1,)).astype(jnp.int32)
    chip_part = [_add_own_half(g, t, c_idx, f"rs_add_{n}_l{l}") for g, t, (n, l) in zip(flat_g, from_sib, order)]
    from_chips = _rs_to_chips(chip_part, "rs_chips")
    my_half = [_sum_slabs(t, f"rs_sum_{n}_l{l}") for t, (n, l) in zip(from_chips, order)]
    shard_g = _rs_share_halves([[my_half[nl * i + l] for l in range(nl)] for i in range(len(names))], "rs_halves")
    g_big = {n: shard_g[i].reshape(shards[n].shape) for i, n in enumerate(names)}

    grads = dict(w_ada=g_w_ada, b_ada=g_b_ada, w_in=g_big["w_in"], rel_bias=g_rel, g_a=g_ga, g_b=g_gb,
                 w_out=g_big["w_out"], w_up=g_big["w_up"], conv_w=g_conv_w, conv_b=g_conv_b, w_down=g_big["w_down"],
                 final_g=g_final_g)
    weights = dict(w_ada=w_ada, b_ada=b_ada, w_in=w_in, rel_bias=rel_bias, g_a=g_a, g_b=g_b, w_out=w_out, w_up=w_up,
                   conv_w=conv_w, conv_b=conv_b, w_down=w_down, final_g=final_g)
    m_in = dict(w_ada=m_w_ada, b_ada=m_b_ada, w_in=m_w_in, rel_bias=m_rel_bias, g_a=m_g_a, g_b=m_g_b, w_out=m_w_out,
                w_up=m_w_up, conv_w=m_conv_w, conv_b=m_conv_b, w_down=m_w_down, final_g=m_final_g)
    v_in = dict(w_ada=v_w_ada, b_ada=v_b_ada, w_in=v_w_in, rel_bias=v_rel_bias, g_a=v_g_a, g_b=v_g_b, w_out=v_w_out,
                w_up=v_w_up, conv_w=v_conv_w, conv_b=v_conv_b, w_down=v_w_down, final_g=v_final_g)
    order_w = ("w_ada", "b_ada", "w_in", "rel_bias", "g_a", "g_b", "w_out", "w_up", "conv_w", "conv_b", "w_down", "final_g")
    upd = {n: _adamw_nd(grads[n], weights[n], m_in[n], v_in[n], f"adamw_{n}") for n in order_w}
    return (loss, dx[None], *[grads[n] for n in order_w], *[upd[n][0] for n in order_w],
            *[upd[n][1] for n in order_w], *[upd[n][2] for n in order_w])


class _Rider:
    def __init__(self, ins, out_shapes, n_remote, n_local, parts):
        self.ins = list(ins)
        self.out_shapes = list(out_shapes)
        self.scratch = [pltpu.SemaphoreType.DMA((n_remote,)), pltpu.SemaphoreType.DMA((n_remote,)),
                        pltpu.SemaphoreType.DMA((max(n_local, 1),))]
        self.parts = parts

    def start(self, in_refs, out_refs, sems):
        local, sends, _ = self.parts(in_refs, out_refs, sems)
        for cp in local() + sends():
            cp.start()

    def wait(self, in_refs, out_refs, sems):
        local, sends, recvs = self.parts(in_refs, out_refs, sems)
        for cp in recvs():
            cp.wait_recv()
        for cp in sends():
            cp.wait_send()
        for cp in local():
            cp.wait()


def _call_with_rider(body, rider, *, name, grid, out_shape, in_specs, out_specs, scratch_shapes, args):
    if rider is None:
        return pl.pallas_call(body, name=name, grid=grid, out_shape=tuple(out_shape), in_specs=list(in_specs),
                              out_specs=tuple(out_specs), scratch_shapes=list(scratch_shapes),
                              compiler_params=_params())(*args)
    n_in, n_out, n_scr = len(in_specs), len(out_specs), len(scratch_shapes)
    r_in, r_out = len(rider.ins), len(rider.out_shapes)

    def both(*refs):
        at = 0
        groups = []
        for size in (n_in, r_in, n_out, r_out, n_scr, len(rider.scratch)):
            groups.append(refs[at:at + size])
            at += size
        own_in, ride_in, own_out, ride_out, own_scr, sems = groups
        steps = [pl.program_id(a) for a in range(len(grid))]
        first = functools.reduce(jnp.logical_and, [st == 0 for st in steps])
        last = functools.reduce(jnp.logical_and, [st == g - 1 for st, g in zip(steps, grid)])

        @pl.when(first)
        def _():
            rider.start(ride_in, ride_out, sems)

        body(*own_in, *own_out, *own_scr)

        @pl.when(last)
        def _():
            rider.wait(ride_in, ride_out, sems)

    outs = pl.pallas_call(
        both, name=name, grid=grid, out_shape=tuple(out_shape) + tuple(rider.out_shapes),
        in_specs=list(in_specs) + [ANY] * r_in, out_specs=tuple(out_specs) + (ANY,) * r_out,
        scratch_shapes=list(scratch_shapes) + rider.scratch, compiler_params=_params(),
    )(*args, *rider.ins)
    return tuple(outs[:n_out]) + (list(outs[n_out:]),)


def _run_rider(rider, name):
    r_in, r_out = len(rider.ins), len(rider.out_shapes)

    def body(*refs):
        ins, outs, sems = refs[:r_in], refs[r_in:r_in + r_out], refs[r_in + r_out:]
        rider.start(ins, outs, sems)
        rider.wait(ins, outs, sems)

    return list(pl.pallas_call(
        body, name=name, out_shape=tuple(rider.out_shapes), in_specs=[ANY] * r_in, out_specs=(ANY,) * r_out,
        scratch_shapes=rider.scratch, compiler_params=_params(),
    )(*rider.ins))


def _remote(src, dst, sems, n, peer):
    return pltpu.make_async_remote_copy(src_ref=src, dst_ref=dst, send_sem=sems[0].at[n], recv_sem=sems[1].at[n],
                                        device_id=peer, device_id_type=MESH)


def _gather_rider(shards, kinds):
    nw = len(shards)
    out_shapes = [jax.ShapeDtypeStruct((a.shape[0], N_CHIPS * a.shape[1]) if kd == "col" else
                                       (N_CHIPS * a.shape[0], a.shape[1]), a.dtype) for a, kd in zip(shards, kinds)]

    def parts(ins, outs, sems):
        x, y, c = _my_place()
        chip = 2 * x + y
        peers = _chip_peers(x, y, c)

        def window(w, j):
            r, n = shards[w].shape
            if kinds[w] == "col":
                return outs[w].at[:, pl.ds(pl.multiple_of(j * n, LANES), n)]
            return outs[w].at[pl.ds(pl.multiple_of(j * r, 16), r), :]

        local = lambda: [pltpu.make_async_copy(ins[w], window(w, chip), sems[2].at[w]) for w in range(nw)]
        sends = lambda: [_remote(ins[w], window(w, chip), sems, 3 * w + k, peer)
                         for w in range(nw) for k, (_, peer) in enumerate(peers)]
        recvs = lambda: [_remote(ins[w], window(w, pchip), sems, 3 * w + k, peer)
                         for w in range(nw) for k, (pchip, peer) in enumerate(peers)]
        return local, sends, recvs

    return _Rider(shards, out_shapes, 3 * nw, nw, parts)


def _half(ref3, h, rows):
    return ref3.at[:, pl.ds(pl.multiple_of(h * rows, 8), rows), :]


def _sibling_rider(grads):
    nw = len(grads)
    out_shapes = [jax.ShapeDtypeStruct((g.shape[0], g.shape[1] // 2, g.shape[2]), g.dtype) for g in grads]

    def parts(ins, outs, sems):
        x, y, c = _my_place()
        sibling = (x, y, 1 - c)
        copies = lambda: [_remote(_half(ins[w], 1 - c, grads[w].shape[1] // 2), outs[w], sems, w, sibling)
                          for w in range(nw)]
        return (lambda: []), copies, copies

    return _Rider(grads, out_shapes, nw, 0, parts)


def _chips_rider(parts_in):
    nw = len(parts_in)
    out_shapes = [jax.ShapeDtypeStruct(p.shape, p.dtype) for p in parts_in]

    def parts(ins, outs, sems):
        x, y, c = _my_place()
        chip = 2 * x + y
        peers = _chip_peers(x, y, c)
        local = lambda: [pltpu.make_async_copy(ins[w].at[chip], outs[w].at[chip], sems[2].at[w]) for w in range(nw)]
        sends = lambda: [_remote(ins[w].at[pchip], outs[w].at[chip], sems, 3 * w + k, peer)
                         for w in range(nw) for k, (pchip, peer) in enumerate(peers)]
        recvs = lambda: [_remote(ins[w].at[chip], outs[w].at[pchip], sems, 3 * w + k, peer)
                         for w in range(nw) for k, (pchip, peer) in enumerate(peers)]
        return local, sends, recvs

    return _Rider(parts_in, out_shapes, 3 * nw, nw, parts)


def _halves_rider(halves):
    nw, nl = len(halves), len(halves[0])
    flat = [h for hs in halves for h in hs]
    out_shapes = [jax.ShapeDtypeStruct((nl, 2 * hs[0].shape[0], hs[0].shape[1]), hs[0].dtype) for hs in halves]

    def parts(ins, outs, sems):
        x, y, c = _my_place()
        sibling = (x, y, 1 - c)

        def window(w, l, h):
            rows = halves[w][0].shape[0]
            return outs[w].at[l, pl.ds(pl.multiple_of(h * rows, 8), rows), :]

        pairs = [(w, l) for w in range(nw) for l in range(nl)]
        local = lambda: [pltpu.make_async_copy(ins[nl * w + l], window(w, l, c), sems[2].at[nl * w + l]) for w, l in pairs]
        sends = lambda: [_remote(ins[nl * w + l], window(w, l, c), sems, nl * w + l, sibling) for w, l in pairs]
        recvs = lambda: [_remote(ins[nl * w + l], window(w, l, 1 - c), sems, nl * w + l, sibling) for w, l in pairs]
        return local, sends, recvs

    return _Rider(flat, out_shapes, nw * nl, nw * nl, parts)


def _add_my_half(grad, got, c_idx, name):
    _, r, n = got.shape
    tr = _pick(r, 256, 8)
    nblk = r // tr

    def body(c_ref, g_ref, t_ref, o_ref):
        o_ref[...] = g_ref[...] + t_ref[...]

    return pl.pallas_call(
        body, name=name, out_shape=jax.ShapeDtypeStruct(got.shape, F32),
        grid_spec=pltpu.PrefetchScalarGridSpec(
            num_scalar_prefetch=1, grid=(N_CHIPS, nblk),
            in_specs=[pl.BlockSpec((None, tr, n), lambda j, i, c_ref: (j, c_ref[0] * nblk + i, 0)),
                      pl.BlockSpec((None, tr, n), lambda j, i, c_ref: (j, i, 0))],
            out_specs=pl.BlockSpec((None, tr, n), lambda j, i, c_ref: (j, i, 0))),
        compiler_params=_params(),
    )(c_idx, grad, got)


def _by_chip(g):
    return g if g.ndim == 3 else g.reshape(N_CHIPS, g.shape[0] // N_CHIPS, g.shape[1])


BIG = ("w_in", "w_out", "w_up", "w_down")
BIG_KIND = dict(w_in="col", w_out="row", w_up="col", w_down="row")


def _forward_layer(x, mod, w, band, tag, rider=None, on_arrival=None):
    s, d = x.shape
    row = lambda i: mod[i:i + 1]
    h1 = _norm_mod(x, row(1), row(0), f"norm_mix{tag}")
    proj = _matmul(h1, w["w_in"], form="nn", out_dtype=BF16, tm=_pick(s, 512), tn=_pick(w["w_in"].shape[1], 768),
                   tk=d, name=f"proj{tag}")
    oa = _attn_a_fwd(proj, band, f"attn_a{tag}")
    if rider is None:
        ob, ltot = _sb_fwd(proj, f"attn_b{tag}")
    else:
        ob, ltot, arrived = _sb_fwd(proj, f"attn_b{tag}", rider)
        on_arrival(arrived)
    cat = _out_norm(oa, ob, w["g_a"], w["g_b"], f"out_norm{tag}")
    mixed = _matmul(cat, w["w_out"], form="nn", out_dtype=F32, tm=_pick(s, 512), tn=_pick(d, 1024),
                    tk=cat.shape[1], name=f"mix_out{tag}")
    x1 = _residual(x, row(2), mixed, f"res_mix{tag}")
    h2 = _norm_mod(x1, row(4), row(3), f"norm_ffn{tag}")
    f2 = w["w_up"].shape[1]
    u = _matmul(h2, w["w_up"], form="nn", out_dtype=F32, tm=_pick(s, 512), tn=_pick(f2, 1408), tk=d, name=f"up{tag}")
    a = _conv_glu(u, w["conv_w"], w["conv_b"], f"conv_glu{tag}")
    f = _matmul(a, w["w_down"], form="nn", out_dtype=F32, tm=_pick(s, 512), tn=_pick(d, 1024),
                tk=_pick(f2 // 2, 1408), name=f"down{tag}")
    x2 = _residual(x1, row(5), f, f"res_ffn{tag}")
    saved = dict(x=x, h1=h1, proj=proj, oa=oa, ob=ob, ltot=ltot, cat=cat, mixed=mixed, x1=x1, h2=h2, u=u, a=a, f=f)
    return x2, saved


def _backward_layer(dx2, sv, mod, w, band, tag, c_idx, waiting=None):
    s, d = dx2.shape
    row = lambda i: mod[i:i + 1]
    f2 = w["w_up"].shape[1]
    ff = f2 // 2
    n_in = w["w_in"].shape[1]
    df, dgate_ffn = _gate_bwd(dx2, sv["f"], row(5), f"gate_ffn_bwd{tag}")
    da = _matmul(df, w["w_down"], form="nt", out_dtype=F32, tm=_pick(s, 512), tn=_pick(ff, 1408), tk=d, name=f"down_dx{tag}")
    g_down = _matmul(sv["a"], df, form="tn", out_dtype=F32, tm=_pick(ff, 1408), tn=_pick(d, 512), tk=_pick(s, 512),
                     name=f"down_dw{tag}")
    du2, dcw, dcb = _conv_glu_bwd(sv["u"], da, w["conv_w"], w["conv_b"], f"conv_glu_bwd{tag}")
    du = jnp.concatenate([du2[0], du2[1]], axis=1)
    dh2 = _matmul(du, w["w_up"], form="nt", out_dtype=F32, tm=_pick(s, 512), tn=_pick(d, 1024), tk=_pick(f2, 1408),
                  name=f"up_dx{tag}")
    g_up = _matmul(sv["h2"], du, form="tn", out_dtype=F32, tm=_pick(d, 512), tn=_pick(f2 // N_CHIPS, 1408),
                   tk=_pick(s, 512), name=f"up_dw{tag}", shard_cols=f2 // N_CHIPS)
    dx1, dscale_ffn, dshift_ffn = _norm_mod_bwd(sv["x1"], dh2, dx2, row(4), f"norm_ffn_bwd{tag}")
    dmixed, dgate_mix = _gate_bwd(dx1, sv["mixed"], row(2), f"gate_mix_bwd{tag}")
    dcat = _matmul(dmixed, w["w_out"], form="nt", out_dtype=F32, tm=_pick(s, 512), tn=_pick(2 * W_GRP, 1024), tk=d,
                   name=f"mix_out_dx{tag}")
    g_out = _matmul(sv["cat"], dmixed, form="tn", out_dtype=F32, tm=_pick(2 * W_GRP, 512), tn=_pick(d, 1024),
                    tk=_pick(s, 512), name=f"mix_out_dw{tag}")
    doa, dob, dg_a, dg_b = _out_norm_bwd(sv["oa"], sv["ob"], dcat, w["g_a"], w["g_b"], f"out_norm_bwd{tag}")
    big = {("w_down", tag): _by_chip(g_down), ("w_up", tag): _by_chip(g_up), ("w_out", tag): _by_chip(g_out)}
    if waiting is None:
        dqa, dka, dva, dband = _attn_a_bwd(sv["proj"], band, doa, f"attn_a_bwd{tag}")
        dqb, dkb, dvb = _sb_bwd(sv["proj"], sv["ltot"], dob, f"attn_b_bwd{tag}")
    else:
        raw = {**waiting, **big}
        keys = list(raw)
        dqa, dka, dva, dband, from_sib = _attn_a_bwd(sv["proj"], band, doa, f"attn_a_bwd{tag}",
                                                     _sibling_rider([raw[k] for k in keys]))
        chip_part = [_add_my_half(raw[k], t, c_idx, f"rs_add_{k[0]}{k[1]}") for k, t in zip(keys, from_sib)]
        dqb, dkb, dvb, from_chips = _sb_bwd(sv["proj"], sv["ltot"], dob, f"attn_b_bwd{tag}", _chips_rider(chip_part))
        big = {k: _sum_slabs(t, f"rs_sum_{k[0]}{k[1]}") for k, t in zip(keys, from_chips)}
    drel = _rel_bias_grad(jnp.transpose(dband, (1, 0, 2)), f"rel_bias_bwd{tag}")[:, :N_REL]
    dproj = jnp.concatenate([dqa, dka, dva, dqb, dkb, dvb], axis=1).astype(BF16)
    dh1 = _matmul(dproj, w["w_in"], form="nt", out_dtype=F32, tm=_pick(s, 512), tn=_pick(d, 1024), tk=_pick(n_in, 1024),
                  name=f"proj_dx{tag}")
    g_in = _matmul(sv["h1"], dproj, form="tn", out_dtype=F32, tm=_pick(d, 512), tn=_pick(n_in // N_CHIPS, 768),
                   tk=_pick(s, 512), name=f"proj_dw{tag}", shard_cols=n_in // N_CHIPS)
    dx, dscale_mix, dshift_mix = _norm_mod_bwd(sv["x"], dh1, dx1, row(1), f"norm_mix_bwd{tag}")
    dmod = jnp.concatenate([dshift_mix, dscale_mix, dgate_mix, dshift_ffn, dscale_ffn, dgate_ffn], axis=1)
    dconv_w = jnp.concatenate([dcw[0], dcw[1]], axis=1)
    dconv_b = jnp.concatenate([dcb[0], dcb[1]], axis=1)
    small = dict(dmod=dmod, rel_bias=drel, g_a=dg_a, g_b=dg_b, conv_w=dconv_w, conv_b=dconv_b)
    return dx, big, g_in, small


def kernel(x, c, w_ada, b_ada, w_in, rel_bias, g_a, g_b, w_out, w_up, conv_w, conv_b, w_down, final_g, loss_target, m_w_ada, m_b_ada, m_w_in, m_rel_bias, m_g_a, m_g_b, m_w_out, m_w_up, m_conv_w, m_conv_b, m_w_down, m_final_g, v_w_ada, v_b_ada, v_w_in, v_rel_bias, v_g_a, v_g_b, v_w_out, v_w_up, v_conv_w, v_conv_b, v_w_down, v_final_g):
    xi, yi, ci = _my_place()
    chip = 2 * xi + yi
    dev = 4 * xi + 2 * yi + ci
    c_idx = jnp.reshape(ci, (1,)).astype(jnp.int32)
    nl, d, n_ada = w_ada.shape
    nc = conv_w.shape[2]
    assert nl == 2

    c_pad = jnp.pad(c, ((0, 7), (0, 0)))
    c_all = _allgather8(c_pad, "gather_c")[0::8]
    b_mine = lax.dynamic_slice_in_dim(b_ada, chip * n_ada, n_ada, axis=1)[:, None, :]
    mod_shard = _ada_fwd(c_all, w_ada, b_mine, "ada")
    pack2 = _pack_rows([mod_shard, conv_w])
    got2 = _allgather8(pack2, "gather_mod").reshape(N_DEV, -1)
    mods, convs = [], []
    for j in range(N_CHIPS):
        ms, cw = _unpack_rows(got2[2 * j], [mod_shard.shape, conv_w.shape])
        mods.append(lax.dynamic_index_in_dim(ms, dev, axis=1, keepdims=False))
        convs.append(cw)
    mod = jnp.concatenate(mods, axis=1).reshape(nl, 6, d)
    conv_w_full = jnp.concatenate(convs, axis=2)

    shards = dict(w_in=w_in, w_out=w_out, w_up=w_up, w_down=w_down)
    sh = {(n, l): shards[n][l].astype(BF16) for n in BIG for l in range(nl)}
    early = [("w_in", 0), ("w_out", 0)]
    late = [("w_up", 0), ("w_down", 0)] + [(n, 1) for n in BIG]
    layers = [dict(g_a=g_a[l:l + 1], g_b=g_b[l:l + 1], conv_w=conv_w_full[l], conv_b=conv_b[l:l + 1]) for l in range(nl)]
    got = _run_rider(_gather_rider([sh[k] for k in early], [BIG_KIND[k[0]] for k in early]), "gather_first")
    for (n, l), full in zip(early, got):
        layers[l][n] = full
    late_rider = _gather_rider([sh[k] for k in late], [BIG_KIND[k[0]] for k in late])

    def on_arrival(arrived):
        for (n, l), full in zip(late, arrived):
            layers[l][n] = full

    xs = x[0]
    saved, bands = [], []
    for l in range(nl):
        band = _band_bias(rel_bias[l])
        xs, sv = _forward_layer(xs, mod[l], layers[l], band, f"_l{l}", late_rider if l == 0 else None, on_arrival)
        bands.append(band)
        saved.append(sv)
    loss_part, dx, dfinal_g = _loss_head(xs, final_g[None, :], loss_target[0], "loss_head")
    loss = lax.psum(loss_part[0, 0], ("x", "y", "c"))

    small = [None] * nl
    dx, raw1, g_in1, small[1] = _backward_layer(dx, saved[1], mod[1], layers[1], bands[1], "_l1", c_idx)
    raw1[("w_in", "_l1")] = g_in1
    dx, half, g_in0, small[0] = _backward_layer(dx, saved[0], mod[0], layers[0], bands[0], "_l0", c_idx, raw1)
    key = ("w_in", "_l0")
    (from_sib,) = _run_rider(_sibling_rider([g_in0]), "rs_sibling_last")
    (from_chips,) = _run_rider(_chips_rider([_add_my_half(g_in0, from_sib, c_idx, "rs_add_w_in_l0")]), "rs_chips_last")
    half[key] = _sum_slabs(from_chips, "rs_sum_w_in_l0")
    shard_g = _run_rider(_halves_rider([[half[(n, f"_l{l}")] for l in range(nl)] for n in BIG]), "rs_halves")
    g_big = dict(zip(BIG, shard_g))

    small_names = ("dmod", "rel_bias", "g_a", "g_b", "conv_w", "conv_b")
    pieces = [small[l][n] for l in range(nl) for n in small_names] + [dfinal_g]
    shapes = [p.shape for p in pieces]
    pack3 = _pack_rows(pieces)
    got3 = _allgather8(pack3, "gather_small").reshape(N_DEV, pack3.shape[0], LANES)
    summed = _unpack_rows(_sum_slabs(got3, "sum_small"), shapes)
    tot = [dict(zip(small_names, summed[len(small_names) * l:len(small_names) * (l + 1)])) for l in range(nl)]
    g_final_g = summed[-1].reshape(-1)
    g_b_ada = jnp.stack([tot[l]["dmod"].reshape(-1) for l in range(nl)])
    g_rel = jnp.stack([tot[l]["rel_bias"] for l in range(nl)])
    g_ga = jnp.stack([tot[l]["g_a"].reshape(-1) for l in range(nl)])
    g_gb = jnp.stack([tot[l]["g_b"].reshape(-1) for l in range(nl)])
    g_conv_b = jnp.stack([tot[l]["conv_b"].reshape(-1) for l in range(nl)])
    g_conv_w = jnp.stack([lax.dynamic_slice_in_dim(tot[l]["conv_w"], chip * nc, nc, axis=1) for l in range(nl)])
    per_dev = [_unpack_rows(got3[j], shapes) for j in range(N_DEV)]
    dmod_all = jnp.stack([jnp.stack([per_dev[j][len(small_names) * l].reshape(-1) for j in range(N_DEV)])
                          for l in range(nl)])
    g_w_ada = _ada_bwd(c_all, lax.dynamic_slice_in_dim(dmod_all, chip * n_ada, n_ada, axis=2), "ada_bwd")

    grads = dict(w_ada=g_w_ada, b_ada=g_b_ada, w_in=g_big["w_in"], rel_bias=g_rel, g_a=g_ga, g_b=g_gb,
                 w_out=g_big["w_out"], w_up=g_big["w_up"], conv_w=g_conv_w, conv_b=g_conv_b, w_down=g_big["w_down"],
                 final_g=g_final_g)
    weights = dict(w_ada=w_ada, b_ada=b_ada, w_in=w_in, rel_bias=rel_bias, g_a=g_a, g_b=g_b, w_out=w_out, w_up=w_up,
                   conv_w=conv_w, conv_b=conv_b, w_down=w_down, final_g=final_g)
    m_in = dict(w_ada=m_w_ada, b_ada=m_b_ada, w_in=m_w_in, rel_bias=m_rel_bias, g_a=m_g_a, g_b=m_g_b, w_out=m_w_out,
                w_up=m_w_up, conv_w=m_conv_w, conv_b=m_conv_b, w_down=m_w_down, final_g=m_final_g)
    v_in = dict(w_ada=v_w_ada, b_ada=v_b_ada, w_in=v_w_in, rel_bias=v_rel_bias, g_a=v_g_a, g_b=v_g_b, w_out=v_w_out,
                w_up=v_w_up, conv_w=v_conv_w, conv_b=v_conv_b, w_down=v_w_down, final_g=v_final_g)
    order_w = ("w_ada", "b_ada", "w_in", "rel_bias", "g_a", "g_b", "w_out", "w_up", "conv_w", "conv_b", "w_down", "final_g")
    upd = {n: _adamw_nd(grads[n], weights[n], m_in[n], v_in[n], f"adamw_{n}") for n in order_w}
    return (loss, dx[None], *[grads[n] for n in order_w], *[upd[n][0] for n in order_w],
            *[upd[n][1] for n in order_w], *[upd[n][2] for n in order_w])
```

```python
import functools

import jax
import jax.numpy as jnp
from jax import lax
from jax.experimental import pallas as pl
from jax.experimental.pallas import tpu as pltpu

F32 = jnp.float32
BF16 = jnp.bfloat16
MESH = pl.DeviceIdType.MESH
ANY = pl.BlockSpec(memory_space=pl.ANY)
VMEM_FULL = pl.BlockSpec(memory_space=pltpu.VMEM)

HEAD_DIM = 64
N_HEADS = 8
W_GRP = N_HEADS * HEAD_DIM
CHUNK = 64
N_PREV = 8
BAND = (N_PREV + 1) * CHUNK
PAD = N_PREV * CHUNK
REL_CLIP = 128
N_REL = 2 * REL_CLIP + 1
EPS = 1e-6
N_CHIPS = 4
N_DEV = 8
LANES = 128
V7X_VMEM_LIMIT = 56 * 1024 * 1024

ADAM_LR = 0.001
ADAM_B1 = 0.9
ADAM_B2 = 0.999
ADAM_EPS = 1e-08
ADAM_WD = 0.01
ADAM_STEP = 10


def _params(**kw):
    return pltpu.CompilerParams(vmem_limit_bytes=V7X_VMEM_LIMIT, **kw)


def _pick(dim, pref, mult=LANES):
    t = (min(pref, dim) // mult) * mult
    while t >= mult:
        if dim % t == 0:
            return t
        t -= mult
    return dim


def _my_place():
    return lax.axis_index("x"), lax.axis_index("y"), lax.axis_index("c")


def _flip(v, bit):
    return 1 - v if bit else v


def _matmul(a, b, *, form, out_dtype, tm, tn, tk, name, shard_cols=None):
    if form == "nn":
        (m, k), (_, n) = a.shape, b.shape
        a_spec = pl.BlockSpec((tm, tk), lambda i, j, kk: (i, kk))
        b_spec = pl.BlockSpec((tk, tn), lambda i, j, kk: (kk, j))
        dims = (((1,), (0,)), ((), ()))
    elif form == "nt":
        (m, k), (n, _) = a.shape, b.shape
        a_spec = pl.BlockSpec((tm, tk), lambda i, j, kk: (i, kk))
        b_spec = pl.BlockSpec((tn, tk), lambda i, j, kk: (j, kk))
        dims = (((1,), (1,)), ((), ()))
    else:
        (k, m), (_, n) = a.shape, b.shape
        a_spec = pl.BlockSpec((tk, tm), lambda i, j, kk: (kk, i))
        b_spec = pl.BlockSpec((tk, tn), lambda i, j, kk: (kk, j))
        dims = (((0,), (0,)), ((), ()))
    assert m % tm == 0 and n % tn == 0 and k % tk == 0, (name, m, n, k, tm, tn, tk)
    nk = k // tk
    if shard_cols is None:
        out_shape = jax.ShapeDtypeStruct((m, n), out_dtype)
        o_spec = pl.BlockSpec((tm, tn), lambda i, j, kk: (i, j))
    else:
        per = shard_cols // tn
        assert shard_cols % tn == 0
        out_shape = jax.ShapeDtypeStruct((n // shard_cols, m, shard_cols), out_dtype)
        o_spec = pl.BlockSpec((None, tm, tn), lambda i, j, kk: (j // per, i, j % per))

    def body(a_ref, b_ref, o_ref, acc_ref):
        kk = pl.program_id(2)
        part = lax.dot_general(a_ref[...], b_ref[...], dims, preferred_element_type=F32)

        @pl.when(kk == 0)
        def _():
            acc_ref[...] = part

        @pl.when(kk > 0)
        def _():
            acc_ref[...] += part

        @pl.when(kk == nk - 1)
        def _():
            o_ref[...] = acc_ref[...].astype(out_dtype)

    return pl.pallas_call(
        body, name=name, out_shape=out_shape, grid=(m // tm, n // tn, nk),
        in_specs=[a_spec, b_spec], out_specs=o_spec,
        scratch_shapes=[pltpu.VMEM((tm, tn), F32)], compiler_params=_params(),
    )(a, b)


def _row_spec(tr, d):
    return pl.BlockSpec((tr, d), lambda i: (i, 0))


def _vec_spec(d):
    return pl.BlockSpec((1, d), lambda i: (0, 0))


def _rms(xf):
    r = lax.rsqrt(jnp.mean(xf * xf, axis=-1, keepdims=True) + EPS)
    return xf * r, r


def _norm_mod(x, scale, shift, name):
    s, d = x.shape
    tr = _pick(s, 512, 8)

    def body(x_ref, sc_ref, sh_ref, o_ref):
        n, _ = _rms(x_ref[...])
        o_ref[...] = (n * (1.0 + sc_ref[...]) + sh_ref[...]).astype(BF16)

    return pl.pallas_call(
        body, name=name, out_shape=jax.ShapeDtypeStruct((s, d), BF16), grid=(s // tr,),
        in_specs=[_row_spec(tr, d), _vec_spec(d), _vec_spec(d)], out_specs=_row_spec(tr, d),
        compiler_params=_params(),
    )(x, scale, shift)


def _out_norm(oa, ob, g_a, g_b, name):
    s, w = oa.shape
    tr = _pick(s, 512, 8)

    def body(oa_ref, ob_ref, ga_ref, gb_ref, o_ref):
        na, _ = _rms(oa_ref[...])
        nb, _ = _rms(ob_ref[...])
        o_ref[:, :w] = (na * ga_ref[...]).astype(BF16)
        o_ref[:, w:] = (nb * gb_ref[...]).astype(BF16)

    return pl.pallas_call(
        body, name=name, out_shape=jax.ShapeDtypeStruct((s, 2 * w), BF16), grid=(s // tr,),
        in_specs=[_row_spec(tr, w), _row_spec(tr, w), _vec_spec(w), _vec_spec(w)],
        out_specs=_row_spec(tr, 2 * w), compiler_params=_params(),
    )(oa, ob, g_a, g_b)


def _residual(x, gate, m, name):
    s, d = x.shape
    tr = _pick(s, 512, 8)

    def body(x_ref, g_ref, m_ref, o_ref):
        o_ref[...] = x_ref[...] + g_ref[...] * m_ref[...]

    return pl.pallas_call(
        body, name=name, out_shape=jax.ShapeDtypeStruct((s, d), F32), grid=(s // tr,),
        in_specs=[_row_spec(tr, d), _vec_spec(d), _row_spec(tr, d)], out_specs=_row_spec(tr, d),
        compiler_params=_params(),
    )(x, gate, m)


def _shift_down(u, k):
    rows = lax.broadcasted_iota(jnp.int32, u.shape, 0)
    return jnp.where(rows >= k, pltpu.roll(u, k, 0), 0.0)


def _shift_up(u, k):
    s = u.shape[0]
    rows = lax.broadcasted_iota(jnp.int32, u.shape, 0)
    return jnp.where(rows < s - k, pltpu.roll(u, s - k, 0), 0.0)


def _conv(u, w_ref, b_ref):
    return w_ref[0:1, :] * _shift_down(u, 2) + w_ref[1:2, :] * _shift_down(u, 1) + w_ref[2:3, :] * u + b_ref[...]


def _conv_glu(u, conv_w, conv_b, name):
    s, f2 = u.shape
    f = f2 // 2
    tc = LANES
    nb = f // tc

    def body(ug_ref, uv_ref, wg_ref, wv_ref, bg_ref, bv_ref, o_ref):
        g = _conv(ug_ref[...], wg_ref, bg_ref)
        v = _conv(uv_ref[...], wv_ref, bv_ref)
        o_ref[...] = (g * jax.nn.sigmoid(g) * v).astype(BF16)

    col = lambda off: pl.BlockSpec((s, tc), lambda j: (0, j + off))
    wcol = lambda off: pl.BlockSpec((3, tc), lambda j: (0, j + off))
    bcol = lambda off: pl.BlockSpec((1, tc), lambda j: (0, j + off))
    return pl.pallas_call(
        body, name=name, out_shape=jax.ShapeDtypeStruct((s, f), BF16), grid=(nb,),
        in_specs=[col(0), col(nb), wcol(0), wcol(nb), bcol(0), bcol(nb)], out_specs=col(0),
        compiler_params=_params(),
    )(u, u, conv_w, conv_w, conv_b, conv_b)


def _conv_glu_bwd(u, da, conv_w, conv_b, name):
    s, f2 = u.shape
    f = f2 // 2
    tc = LANES
    nb = f // tc

    def body(ug_ref, uv_ref, da_ref, wg_ref, wv_ref, bg_ref, bv_ref, du_ref, dw_ref, db_ref):
        da_ = da_ref[...]
        ug, uv = ug_ref[...], uv_ref[...]
        g = _conv(ug, wg_ref, bg_ref)
        v = _conv(uv, wv_ref, bv_ref)
        sg = jax.nn.sigmoid(g)
        dg = da_ * v * (sg * (1.0 + g * (1.0 - sg)))
        dv = da_ * (g * sg)
        for h, (dy, uu, w_ref) in enumerate(((dg, ug, wg_ref), (dv, uv, wv_ref))):
            du = w_ref[2:3, :] * dy + w_ref[1:2, :] * _shift_up(dy, 1) + w_ref[0:1, :] * _shift_up(dy, 2)
            du_ref[h] = du.astype(BF16)
            dw_ref[h, 0:1, :] = jnp.sum(dy * _shift_down(uu, 2), axis=0, keepdims=True)
            dw_ref[h, 1:2, :] = jnp.sum(dy * _shift_down(uu, 1), axis=0, keepdims=True)
            dw_ref[h, 2:3, :] = jnp.sum(dy * uu, axis=0, keepdims=True)
            db_ref[h] = jnp.sum(dy, axis=0, keepdims=True)

    col = lambda off: pl.BlockSpec((s, tc), lambda j: (0, j + off))
    wcol = lambda off: pl.BlockSpec((3, tc), lambda j: (0, j + off))
    bcol = lambda off: pl.BlockSpec((1, tc), lambda j: (0, j + off))
    return pl.pallas_call(
        body, name=name, grid=(nb,),
        out_shape=(jax.ShapeDtypeStruct((2, s, f), BF16), jax.ShapeDtypeStruct((2, 3, f), F32),
                   jax.ShapeDtypeStruct((2, 1, f), F32)),
        in_specs=[col(0), col(nb), col(0), wcol(0), wcol(nb), bcol(0), bcol(nb)],
        out_specs=(pl.BlockSpec((2, s, tc), lambda j: (0, 0, j)), pl.BlockSpec((2, 3, tc), lambda j: (0, 0, j)),
                   pl.BlockSpec((2, 1, tc), lambda j: (0, 0, j))),
        compiler_params=_params(),
    )(u, u, da, conv_w, conv_w, conv_b, conv_b)


def _accumulate(ref, val):
    @pl.when(pl.program_id(0) == 0)
    def _():
        ref[...] = val

    @pl.when(pl.program_id(0) > 0)
    def _():
        ref[...] += val


def _rms_bwd(n, r, dn):
    return r * (dn - n * jnp.mean(dn * n, axis=-1, keepdims=True))


def _loss_head(x, final_g, target, name):
    s, d = x.shape
    tr = _pick(s, 512, 8)

    def body(x_ref, g_ref, t_ref, loss_ref, dx_ref, dg_ref):
        n, r = _rms(x_ref[...])
        diff = n * g_ref[...] - t_ref[...]
        part = 0.5 * jnp.sum(jnp.sum(diff * diff, axis=1, keepdims=True), axis=0, keepdims=True) / d
        _accumulate(loss_ref, part)
        dy = diff / d
        _accumulate(dg_ref, jnp.sum(dy * n, axis=0, keepdims=True))
        dx_ref[...] = _rms_bwd(n, r, dy * g_ref[...])

    return pl.pallas_call(
        body, name=name, grid=(s // tr,),
        out_shape=(jax.ShapeDtypeStruct((1, 1), F32), jax.ShapeDtypeStruct((s, d), F32), jax.ShapeDtypeStruct((1, d), F32)),
        in_specs=[_row_spec(tr, d), _vec_spec(d), _row_spec(tr, d)],
        out_specs=(pl.BlockSpec((1, 1), lambda i: (0, 0)), _row_spec(tr, d), _vec_spec(d)),
        compiler_params=_params(),
    )(x, final_g, target)


def _gate_bwd(dx, m, gate, name):
    s, d = dx.shape
    tr = _pick(s, 512, 8)

    def body(dx_ref, m_ref, g_ref, dm_ref, dg_ref):
        dxv = dx_ref[...]
        dm_ref[...] = (dxv * g_ref[...]).astype(BF16)
        _accumulate(dg_ref, jnp.sum(dxv * m_ref[...], axis=0, keepdims=True))

    return pl.pallas_call(
        body, name=name, grid=(s // tr,),
        out_shape=(jax.ShapeDtypeStruct((s, d), BF16), jax.ShapeDtypeStruct((1, d), F32)),
        in_specs=[_row_spec(tr, d), _row_spec(tr, d), _vec_spec(d)], out_specs=(_row_spec(tr, d), _vec_spec(d)),
        compiler_params=_params(),
    )(dx, m, gate)


def _norm_mod_bwd(x, dh, dres, scale, name):
    s, d = x.shape
    tr = _pick(s, 512, 8)

    def body(x_ref, dh_ref, dr_ref, sc_ref, dx_ref, dsc_ref, dsh_ref):
        n, r = _rms(x_ref[...])
        dh_ = dh_ref[...]
        _accumulate(dsc_ref, jnp.sum(dh_ * n, axis=0, keepdims=True))
        _accumulate(dsh_ref, jnp.sum(dh_, axis=0, keepdims=True))
        dx_ref[...] = dr_ref[...] + _rms_bwd(n, r, dh_ * (1.0 + sc_ref[...]))

    return pl.pallas_call(
        body, name=name, grid=(s // tr,),
        out_shape=(jax.ShapeDtypeStruct((s, d), F32), jax.ShapeDtypeStruct((1, d), F32), jax.ShapeDtypeStruct((1, d), F32)),
        in_specs=[_row_spec(tr, d), _row_spec(tr, d), _row_spec(tr, d), _vec_spec(d)],
        out_specs=(_row_spec(tr, d), _vec_spec(d), _vec_spec(d)), compiler_params=_params(),
    )(x, dh, dres, scale)


def _out_norm_bwd(oa, ob, dcat, g_a, g_b, name):
    s, w = oa.shape
    tr = _pick(s, 512, 8)

    def body(oa_ref, ob_ref, dc_ref, ga_ref, gb_ref, doa_ref, dob_ref, dga_ref, dgb_ref):
        for o_ref, g_ref, do_ref, dg_ref, lo in ((oa_ref, ga_ref, doa_ref, dga_ref, 0), (ob_ref, gb_ref, dob_ref, dgb_ref, w)):
            n, r = _rms(o_ref[...])
            dc = dc_ref[:, lo:lo + w]
            _accumulate(dg_ref, jnp.sum(dc * n, axis=0, keepdims=True))
            do_ref[...] = _rms_bwd(n, r, dc * g_ref[...])

    return pl.pallas_call(
        body, name=name, grid=(s // tr,),
        out_shape=(jax.ShapeDtypeStruct((s, w), F32), jax.ShapeDtypeStruct((s, w), F32),
                   jax.ShapeDtypeStruct((1, w), F32), jax.ShapeDtypeStruct((1, w), F32)),
        in_specs=[_row_spec(tr, w), _row_spec(tr, w), _row_spec(tr, 2 * w), _vec_spec(w), _vec_spec(w)],
        out_specs=(_row_spec(tr, w), _row_spec(tr, w), _vec_spec(w), _vec_spec(w)), compiler_params=_params(),
    )(oa, ob, dcat, g_a, g_b)


def _head_masks():
    lane = lax.broadcasted_iota(jnp.int32, (1, LANES), 1)
    return lane < HEAD_DIM, lane >= HEAD_DIM


def _nt(a, b):
    return lax.dot_general(a, b, (((1,), (1,)), ((), ())), preferred_element_type=F32)


def _tn(a, b):
    return lax.dot_general(a, b, (((0,), (0,)), ((), ())), preferred_element_type=F32)


def _nn(a, b):
    return jnp.dot(a, b, preferred_element_type=F32)


def _only(mask, v):
    return jnp.where(mask, v, jnp.zeros_like(v))


def _fill_padded(dst_ref, src_ref):
    dst_ref[0:PAD, :] = jnp.zeros((PAD, LANES), dst_ref.dtype)
    dst_ref[PAD:, :] = src_ref[...]


def _chunk_probs(qh, kb, bias, chunk):
    s = _nt(qh, kb) * (HEAD_DIM ** -0.5) + bias
    pos = lax.broadcasted_iota(jnp.int32, (1, BAND), 1)
    s = jnp.where(pos >= (N_PREV - chunk) * CHUNK, s, -1e30)
    e = jnp.exp(s - jnp.max(s, axis=1, keepdims=True))
    return e / jnp.sum(e, axis=1, keepdims=True)


def _attn_a_fwd(proj, band_bias, name):
    s = proj.shape[0]
    cq = 4
    tq = cq * CHUNK
    npair = N_HEADS // 2
    kcol, vcol = W_GRP // LANES, 2 * W_GRP // LANES

    def body(q_ref, k_ref, v_ref, b_ref, o_ref, kpad, vpad):
        i = pl.program_id(1)
        masks = _head_masks()

        @pl.when(i == 0)
        def _():
            _fill_padded(kpad, k_ref)
            _fill_padded(vpad, v_ref)

        for cc in range(cq):
            chunk = i * cq + cc
            start = pl.multiple_of(chunk * CHUNK, CHUNK)
            kb = kpad[pl.ds(start, BAND), :]
            vb = vpad[pl.ds(start, BAND), :]
            qc = q_ref[cc * CHUNK:(cc + 1) * CHUNK, :]
            out = jnp.zeros((CHUNK, LANES), F32)
            for h in range(2):
                p = _chunk_probs(_only(masks[h], qc), kb, b_ref[h], chunk)
                out = out + _nn(p.astype(BF16), _only(masks[h], vb))
            o_ref[cc * CHUNK:(cc + 1) * CHUNK, :] = out

    return pl.pallas_call(
        body, name=name, out_shape=jax.ShapeDtypeStruct((s, W_GRP), F32), grid=(npair, s // tq),
        in_specs=[pl.BlockSpec((tq, LANES), lambda p, i: (i, p)),
                  pl.BlockSpec((s, LANES), lambda p, i: (0, kcol + p)),
                  pl.BlockSpec((s, LANES), lambda p, i: (0, vcol + p)),
                  pl.BlockSpec((2, CHUNK, BAND), lambda p, i: (p, 0, 0))],
        out_specs=pl.BlockSpec((tq, LANES), lambda p, i: (i, p)),
        scratch_shapes=[pltpu.VMEM((s + PAD, LANES), BF16), pltpu.VMEM((s + PAD, LANES), BF16)],
        compiler_params=_params(),
    )(proj, proj, proj, band_bias)


def _attn_a_bwd(proj, band_bias, doa, name, rider=None):
    s = proj.shape[0]
    cq = 4
    tq = cq * CHUNK
    nq = s // tq
    npair = N_HEADS // 2
    kcol, vcol = W_GRP // LANES, 2 * W_GRP // LANES
    scale = HEAD_DIM ** -0.5

    def body(q_ref, k_ref, v_ref, b_ref, do_ref, dq_ref, dk_ref, dv_ref, db_ref, kpad, vpad, dkpad, dvpad):
        i = pl.program_id(1)
        masks = _head_masks()

        @pl.when(i == 0)
        def _():
            _fill_padded(kpad, k_ref)
            _fill_padded(vpad, v_ref)
            dkpad[...] = jnp.zeros_like(dkpad)
            dvpad[...] = jnp.zeros_like(dvpad)
            db_ref[...] = jnp.zeros_like(db_ref)

        for cc in range(cq):
            chunk = i * cq + cc
            start = pl.multiple_of(chunk * CHUNK, CHUNK)
            kb = kpad[pl.ds(start, BAND), :]
            vb = vpad[pl.ds(start, BAND), :]
            qc = q_ref[cc * CHUNK:(cc + 1) * CHUNK, :]
            doc = do_ref[cc * CHUNK:(cc + 1) * CHUNK, :].astype(BF16)
            dq = jnp.zeros((CHUNK, LANES), F32)
            dk = jnp.zeros((BAND, LANES), F32)
            dv = jnp.zeros((BAND, LANES), F32)
            for h in range(2):
                qh = _only(masks[h], qc)
                doh = _only(masks[h], doc)
                p = _chunk_probs(qh, kb, b_ref[h], chunk)
                dp = _nt(doh, vb)
                ds = p * (dp - jnp.sum(p * dp, axis=1, keepdims=True))
                db_ref[h] += ds
                dsb = (ds * scale).astype(BF16)
                dq = dq + _nn(dsb, _only(masks[h], kb))
                dk = dk + _tn(dsb, qh)
                dv = dv + _tn(p.astype(BF16), doh)
            dq_ref[cc * CHUNK:(cc + 1) * CHUNK, :] = dq
            dkpad[pl.ds(start, BAND), :] += dk
            dvpad[pl.ds(start, BAND), :] += dv

        @pl.when(i == nq - 1)
        def _():
            dk_ref[...] = dkpad[PAD:, :]
            dv_ref[...] = dvpad[PAD:, :]

    blk = pl.BlockSpec((tq, LANES), lambda p, i: (i, p))
    whole = pl.BlockSpec((s, LANES), lambda p, i: (0, p))
    bias_spec = pl.BlockSpec((2, CHUNK, BAND), lambda p, i: (p, 0, 0))
    return _call_with_rider(
        body, rider, name=name, grid=(npair, nq),
        out_shape=(jax.ShapeDtypeStruct((s, W_GRP), F32),) * 3 + (jax.ShapeDtypeStruct((N_HEADS, CHUNK, BAND), F32),),
        in_specs=[blk, pl.BlockSpec((s, LANES), lambda p, i: (0, kcol + p)),
                  pl.BlockSpec((s, LANES), lambda p, i: (0, vcol + p)), bias_spec, blk],
        out_specs=(blk, whole, whole, bias_spec),
        scratch_shapes=[pltpu.VMEM((s + PAD, LANES), BF16), pltpu.VMEM((s + PAD, LANES), BF16),
                        pltpu.VMEM((s + PAD, LANES), F32), pltpu.VMEM((s + PAD, LANES), F32)],
        args=(proj, proj, proj, band_bias, doa))


def _split3(v):
    hi = v.astype(BF16)
    r1 = v - hi.astype(F32)
    mid = r1.astype(BF16)
    lo = (r1 - mid.astype(F32)).astype(BF16)
    return hi, mid, lo


def _rel_bias_grad(dband_t, name):
    width = 3 * LANES

    def body(t_ref, o_ref):
        pos = lax.broadcasted_iota(jnp.int32, (BAND, width), 0)
        col = lax.broadcasted_iota(jnp.int32, (BAND, width), 1)
        acc = jnp.zeros((N_HEADS, width), F32)
        for q in range(CHUNK):
            idx = jnp.minimum(PAD + q - pos, REL_CLIP) + REL_CLIP
            onehot = (col == idx).astype(BF16)
            for part in _split3(t_ref[q]):
                acc = acc + _nn(part, onehot)
        o_ref[...] = acc

    return pl.pallas_call(
        body, name=name, out_shape=jax.ShapeDtypeStruct((N_HEADS, width), F32),
        in_specs=[VMEM_FULL], out_specs=VMEM_FULL, compiler_params=_params(),
    )(dband_t)


def _split2_wide(v):
    hi = v.astype(BF16)
    return jnp.concatenate([hi, (v - hi.astype(F32)).astype(BF16)], axis=1)


def _sb_logs(z, lower):
    e = jnp.exp(-jnp.abs(z))
    lb = jnp.minimum(z, 0.0) - jnp.log(1.0 + e)
    lk = lb - z
    if lower is not None:
        lk = jnp.where(lower, lk, 0.0)
    return z, e, lb, lk


def _tri_masks(tq):
    row = lax.broadcasted_iota(jnp.int32, (tq, tq), 0)
    col = lax.broadcasted_iota(jnp.int32, (tq, tq), 1)
    return row, col


def _stack2(m):
    return jnp.concatenate([m, m], axis=0).astype(BF16)


def _sb_fwd(proj, name, rider=None):
    s = proj.shape[0]
    tq = _pick(s, 256)
    nq = s // tq
    npair = N_HEADS // 2
    qcol, kcol, vcol = 3 * W_GRP // LANES, 4 * W_GRP // LANES, 5 * W_GRP // LANES

    def body(q_ref, k_ref, v_ref, o_ref, l_ref):
        i = pl.program_id(1)
        masks = _head_masks()
        q2 = q_ref[...] * (HEAD_DIM ** -0.5)
        qs = [_only(m, q2) for m in masks]
        row, col = _tri_masks(tq)
        lower = row > col
        after2 = _stack2(lower)

        def tile(jj, carry, diag):
            acc, tails = carry
            ks = pl.multiple_of((i - jj) * tq, tq)
            kb = k_ref[pl.ds(ks, tq), :]
            vb = v_ref[pl.ds(ks, tq), :]
            zs = [_nt(qs[h], kb) for h in range(2)]
            vh = [_only(masks[h], vb) for h in range(2)]
            lbs, lks, locs = [], [], []
            for h in range(2):
                lb, lk = _sb_logs(zs[h], lower if diag else None)[2:]
                lbs.append(lb)
                lks.append(lk)
                locs.append(_nn(_split2_wide(lk), after2))
            new_tails = []
            for h in range(2):
                a = jnp.exp(lbs[h] + (locs[h] + tails[h]))
                if diag:
                    a = jnp.where(lower, a, 0.0)
                acc = acc + _nn(a.astype(BF16), vh[h])
                new_tails.append(tails[h] + (locs[h][:, 0:1] + lks[h][:, 0:1]))
            return acc, tuple(new_tails)

        zero = jnp.zeros((tq, 1), F32)
        carry = tile(0, (jnp.zeros((tq, LANES), F32), (zero, zero)), True)
        acc, tails = lax.fori_loop(1, i + 1, lambda jj, cr: tile(jj, cr, False), carry)
        o_ref[...] = acc
        l_ref[:, 0:1] = tails[0]
        l_ref[:, 1:2] = tails[1]

    return _call_with_rider(
        body, rider, name=name, grid=(npair, nq),
        out_shape=(jax.ShapeDtypeStruct((s, W_GRP), F32), jax.ShapeDtypeStruct((npair, s, 2), F32)),
        in_specs=[pl.BlockSpec((tq, LANES), lambda p, i: (i, qcol + p)),
                  pl.BlockSpec((s, LANES), lambda p, i: (0, kcol + p)),
                  pl.BlockSpec((s, LANES), lambda p, i: (0, vcol + p))],
        out_specs=(pl.BlockSpec((tq, LANES), lambda p, i: (i, p)), pl.BlockSpec((None, tq, 2), lambda p, i: (p, i, 0))),
        scratch_shapes=[], args=(proj, proj, proj))


def _sb_bwd(proj, ltot, dob, name, rider=None):
    s = proj.shape[0]
    tq = _pick(s, 256)
    nq = s // tq
    npair = N_HEADS // 2
    qcol, kcol, vcol = 3 * W_GRP // LANES, 4 * W_GRP // LANES, 5 * W_GRP // LANES
    scale = HEAD_DIM ** -0.5

    def body(q_ref, k_ref, v_ref, l_ref, do_ref, dq_ref, dk_ref, dv_ref):
        i = pl.program_id(1)
        masks = _head_masks()

        @pl.when(i == 0)
        def _():
            dk_ref[...] = jnp.zeros_like(dk_ref)
            dv_ref[...] = jnp.zeros_like(dv_ref)

        q2 = q_ref[...] * scale
        do2 = do_ref[...]
        qs = [_only(m, q2) for m in masks]
        doh = [_only(m, do2).astype(BF16) for m in masks]
        ltots = [l_ref[:, 0:1], l_ref[:, 1:2]]
        row, col = _tri_masks(tq)
        lower = row > col
        upto2 = _stack2(row <= col)
        before2 = _stack2(row < col)

        def tile(j, carry, diag):
            dq, heads, gsums = carry
            ks = pl.multiple_of(j * tq, tq)
            kb = k_ref[pl.ds(ks, tq), :]
            vb = v_ref[pl.ds(ks, tq), :]
            dk = jnp.zeros((tq, LANES), F32)
            dv = jnp.zeros((tq, LANES), F32)
            zs = [_nt(qs[h], kb) for h in range(2)]
            das = [_nt(doh[h], vb) for h in range(2)]
            kh = [_only(masks[h], kb) for h in range(2)]
            sigs, lbs, locs = [], [], []
            for h in range(2):
                z, e, lb, lk = _sb_logs(zs[h], lower if diag else None)
                locs.append(_nn(_split2_wide(lk), upto2))
                r = 1.0 / (1.0 + e)
                sigs.append(jnp.where(z >= 0, r, e * r))
                lbs.append(lb)
            a_s, gs, glocs = [], [], []
            for h in range(2):
                a = jnp.exp(lbs[h] + (ltots[h] - (heads[h] + locs[h])))
                if diag:
                    a = jnp.where(lower, a, 0.0)
                g = a * das[h]
                glocs.append(_nn(_split2_wide(g), before2))
                a_s.append(a.astype(BF16))
                gs.append(g)
            new_heads, new_gsums = [], []
            for h in range(2):
                g = gs[h]
                dz = g - sigs[h] * (g + (gsums[h] + glocs[h]))
                if diag:
                    dz = jnp.where(lower, dz, 0.0)
                dzb = dz.astype(BF16)
                dq = dq + _nn(dzb, kh[h])
                dk = dk + _tn(dzb, qs[h])
                dv = dv + _tn(a_s[h], doh[h])
                new_heads.append(heads[h] + locs[h][:, tq - 1:tq])
                new_gsums.append(gsums[h] + (glocs[h][:, tq - 1:tq] + g[:, tq - 1:tq]))
            dk_ref[pl.ds(ks, tq), :] += dk
            dv_ref[pl.ds(ks, tq), :] += dv
            return dq, tuple(new_heads), tuple(new_gsums)

        zero = jnp.zeros((tq, 1), F32)
        carry = (jnp.zeros((tq, LANES), F32), (zero, zero), (zero, zero))
        carry = lax.fori_loop(0, i, lambda j, cr: tile(j, cr, False), carry)
        dq, _, _ = tile(i, carry, True)
        dq_ref[...] = dq * scale

    blk = pl.BlockSpec((tq, LANES), lambda p, i: (i, p))
    whole = pl.BlockSpec((s, LANES), lambda p, i: (0, p))
    return _call_with_rider(
        body, rider, name=name, grid=(npair, nq), out_shape=(jax.ShapeDtypeStruct((s, W_GRP), F32),) * 3,
        in_specs=[pl.BlockSpec((tq, LANES), lambda p, i: (i, qcol + p)),
                  pl.BlockSpec((s, LANES), lambda p, i: (0, kcol + p)),
                  pl.BlockSpec((s, LANES), lambda p, i: (0, vcol + p)),
                  pl.BlockSpec((None, tq, 2), lambda p, i: (p, i, 0)), blk],
        out_specs=(blk, whole, whole), scratch_shapes=[], args=(proj, proj, proj, ltot, dob))


def _ada_fwd(c_all, w_ada, b_ada, name):
    nl, d, n = w_ada.shape
    tn = _pick(n, 512)

    def body(c_ref, w_ref, b_ref, o_ref):
        cv = c_ref[...]
        act = (cv * jax.nn.sigmoid(cv)).astype(BF16)
        o_ref[...] = _nn(act, w_ref[...].astype(BF16)) + b_ref[...]

    return pl.pallas_call(
        body, name=name, out_shape=jax.ShapeDtypeStruct((nl, N_DEV, n), F32), grid=(nl, n // tn),
        in_specs=[pl.BlockSpec((N_DEV, d), lambda l, j: (0, 0)), pl.BlockSpec((None, d, tn), lambda l, j: (l, 0, j)),
                  pl.BlockSpec((None, 1, tn), lambda l, j: (l, 0, j))],
        out_specs=pl.BlockSpec((None, N_DEV, tn), lambda l, j: (l, 0, j)), compiler_params=_params(),
    )(c_all, w_ada, b_ada)


def _ada_bwd(c_all, dmod, name):
    nl, _, n = dmod.shape
    d = c_all.shape[1]
    tn = _pick(n, 512)

    def body(c_ref, g_ref, o_ref):
        cv = c_ref[...]
        act = (cv * jax.nn.sigmoid(cv)).astype(BF16)
        o_ref[...] = _tn(act, g_ref[...].astype(BF16))

    return pl.pallas_call(
        body, name=name, out_shape=jax.ShapeDtypeStruct((nl, d, n), F32), grid=(nl, n // tn),
        in_specs=[pl.BlockSpec((N_DEV, d), lambda l, j: (0, 0)), pl.BlockSpec((None, N_DEV, tn), lambda l, j: (l, 0, j))],
        out_specs=pl.BlockSpec((None, d, tn), lambda l, j: (l, 0, j)), compiler_params=_params(),
    )(c_all, dmod)


def _adamw(g, w, m, v, name):
    r, c = g.shape
    tr = _pick(r, 512, 8)
    c1 = 1.0 - ADAM_B1 ** ADAM_STEP
    c2 = 1.0 - ADAM_B2 ** ADAM_STEP

    def body(g_ref, w_ref, m_ref, v_ref, d_ref, nm_ref, nv_ref):
        gv = g_ref[...]
        nm = ADAM_B1 * m_ref[...] + (1.0 - ADAM_B1) * gv
        nv = ADAM_B2 * v_ref[...] + (1.0 - ADAM_B2) * (gv * gv)
        d_ref[...] = -ADAM_LR * ((nm / c1) / (jnp.sqrt(nv / c2) + ADAM_EPS) + ADAM_WD * w_ref[...])
        nm_ref[...] = nm
        nv_ref[...] = nv

    spec = pl.BlockSpec((tr, c), lambda i: (i, 0))
    return pl.pallas_call(
        body, name=name, out_shape=(jax.ShapeDtypeStruct((r, c), F32),) * 3, grid=(r // tr,),
        in_specs=[spec] * 4, out_specs=(spec,) * 3, compiler_params=_params(),
    )(g, w, m, v)


def _adamw_nd(g, w, m, v, name):
    shape = w.shape
    two_d = (1, shape[0]) if len(shape) == 1 else (-1, shape[-1])
    outs = _adamw(*(t.reshape(two_d) for t in (g, w, m, v)), name=name)
    return tuple(o.reshape(shape) for o in outs)


def _allgather8(v, name):
    m, n = v.shape

    def body(v_ref, out_ref, send_sems, recv_sems, local_sem):
        x, y, c = _my_place()

        def rows(px, py, pc):
            return out_ref.at[pl.ds(pl.multiple_of((4 * px + 2 * py + pc) * m, 8), m), :]

        def peer(k):
            return _flip(x, k & 4), _flip(y, k & 2), _flip(c, k & 1)

        def copy(k, block):
            return pltpu.make_async_remote_copy(
                src_ref=v_ref, dst_ref=rows(*block), send_sem=send_sems.at[k - 1], recv_sem=recv_sems.at[k - 1],
                device_id=peer(k), device_id_type=MESH)

        mine = pltpu.make_async_copy(v_ref, rows(x, y, c), local_sem)
        mine.start()
        sends = [copy(k, (x, y, c)) for k in range(1, N_DEV)]
        for cp in sends:
            cp.start()
        for k in range(1, N_DEV):
            copy(k, peer(k)).wait_recv()
        for cp in sends:
            cp.wait_send()
        mine.wait()

    return pl.pallas_call(
        body, name=name, out_shape=jax.ShapeDtypeStruct((N_DEV * m, n), v.dtype),
        in_specs=[VMEM_FULL], out_specs=VMEM_FULL,
        scratch_shapes=[pltpu.SemaphoreType.DMA((N_DEV - 1,)), pltpu.SemaphoreType.DMA((N_DEV - 1,)),
                        pltpu.SemaphoreType.DMA],
        compiler_params=_params(),
    )(v)


def _chip_peers(x, y, c):
    out = []
    for k in range(1, N_CHIPS):
        px, py = _flip(x, k & 2), _flip(y, k & 1)
        out.append((2 * px + py, (px, py, c)))
    return out


def _gather_weights(shards, kinds, name):
    nw = len(shards)

    def full_shape(a, kind):
        l, r, n = a.shape
        return (l, r, N_CHIPS * n) if kind == "col" else (l, N_CHIPS * r, n)

    def body(*refs):
        ins, outs = refs[:nw], refs[nw:2 * nw]
        send_sems, recv_sems, local_sems = refs[2 * nw:]
        x, y, c = _my_place()
        chip = 2 * x + y

        def window(w, j):
            _, r, n = shards[w].shape
            if kinds[w] == "col":
                return outs[w].at[:, :, pl.ds(pl.multiple_of(j * n, LANES), n)]
            return outs[w].at[:, pl.ds(pl.multiple_of(j * r, 16), r), :]

        def copy(w, k, j, peer):
            return pltpu.make_async_remote_copy(
                src_ref=ins[w], dst_ref=window(w, j), send_sem=send_sems.at[3 * w + k], recv_sem=recv_sems.at[3 * w + k],
                device_id=peer, device_id_type=MESH)

        local = [pltpu.make_async_copy(ins[w], window(w, chip), local_sems.at[w]) for w in range(nw)]
        for cp in local:
            cp.start()
        peers = _chip_peers(x, y, c)
        sends = [copy(w, k, chip, peer) for w in range(nw) for k, (_, peer) in enumerate(peers)]
        for cp in sends:
            cp.start()
        for w in range(nw):
            for k, (pchip, peer) in enumerate(peers):
                copy(w, k, pchip, peer).wait_recv()
        for cp in sends:
            cp.wait_send()
        for cp in local:
            cp.wait()

    return pl.pallas_call(
        body, name=name,
        out_shape=tuple(jax.ShapeDtypeStruct(full_shape(a, kd), a.dtype) for a, kd in zip(shards, kinds)),
        in_specs=[ANY] * nw, out_specs=(ANY,) * nw,
        scratch_shapes=[pltpu.SemaphoreType.DMA((3 * nw,)), pltpu.SemaphoreType.DMA((3 * nw,)),
                        pltpu.SemaphoreType.DMA((nw,))],
        compiler_params=_params(),
    )(*shards)


def _rs_to_sibling(grads, name):
    nw = len(grads)

    def body(*refs):
        ins, outs = refs[:nw], refs[nw:2 * nw]
        send_sems, recv_sems = refs[2 * nw:]
        x, y, c = _my_place()
        sibling = (x, y, 1 - c)
        copies = [pltpu.make_async_remote_copy(
            src_ref=ins[w].at[j, 1 - c], dst_ref=outs[w].at[j], send_sem=send_sems.at[N_CHIPS * w + j],
            recv_sem=recv_sems.at[N_CHIPS * w + j], device_id=sibling, device_id_type=MESH)
            for w in range(nw) for j in range(N_CHIPS)]
        for cp in copies:
            cp.start()
        for cp in copies:
            cp.wait_recv()
        for cp in copies:
            cp.wait_send()

    return pl.pallas_call(
        body, name=name,
        out_shape=tuple(jax.ShapeDtypeStruct((N_CHIPS,) + g.shape[2:], g.dtype) for g in grads),
        in_specs=[ANY] * nw, out_specs=(ANY,) * nw,
        scratch_shapes=[pltpu.SemaphoreType.DMA((N_CHIPS * nw,)), pltpu.SemaphoreType.DMA((N_CHIPS * nw,))],
        compiler_params=_params(),
    )(*grads)


def _rs_to_chips(parts, name):
    nw = len(parts)

    def body(*refs):
        ins, outs = refs[:nw], refs[nw:2 * nw]
        send_sems, recv_sems, local_sems = refs[2 * nw:]
        x, y, c = _my_place()
        chip = 2 * x + y
        peers = _chip_peers(x, y, c)

        def copy(w, k, src_slab, dst_slab, peer):
            return pltpu.make_async_remote_copy(
                src_ref=ins[w].at[src_slab], dst_ref=outs[w].at[dst_slab], send_sem=send_sems.at[3 * w + k],
                recv_sem=recv_sems.at[3 * w + k], device_id=peer, device_id_type=MESH)

        local = [pltpu.make_async_copy(ins[w].at[chip], outs[w].at[chip], local_sems.at[w]) for w in range(nw)]
        for cp in local:
            cp.start()
        sends = [copy(w, k, pchip, chip, peer) for w in range(nw) for k, (pchip, peer) in enumerate(peers)]
        for cp in sends:
            cp.start()
        for w in range(nw):
            for k, (pchip, peer) in enumerate(peers):
                copy(w, k, chip, pchip, peer).wait_recv()
        for cp in sends:
            cp.wait_send()
        for cp in local:
            cp.wait()

    return pl.pallas_call(
        body, name=name, out_shape=tuple(jax.ShapeDtypeStruct(p.shape, p.dtype) for p in parts),
        in_specs=[ANY] * nw, out_specs=(ANY,) * nw,
        scratch_shapes=[pltpu.SemaphoreType.DMA((3 * nw,)), pltpu.SemaphoreType.DMA((3 * nw,)),
                        pltpu.SemaphoreType.DMA((nw,))],
        compiler_params=_params(),
    )(*parts)


def _rs_share_halves(halves, name):
    nw = len(halves)
    nl = len(halves[0])
    flat = [h for hs in halves for h in hs]

    def body(*refs):
        ins, outs = refs[:nw * nl], refs[nw * nl:nw * nl + nw]
        send_sems, recv_sems, local_sems = refs[nw * nl + nw:]
        x, y, c = _my_place()
        sibling = (x, y, 1 - c)
        local, sends, recvs = [], [], []
        for w in range(nw):
            for l in range(nl):
                n = nl * w + l
                local.append(pltpu.make_async_copy(ins[n], outs[w].at[l, c], local_sems.at[n]))
                sends.append(pltpu.make_async_remote_copy(
                    src_ref=ins[n], dst_ref=outs[w].at[l, c], send_sem=send_sems.at[n], recv_sem=recv_sems.at[n],
                    device_id=sibling, device_id_type=MESH))
                recvs.append(pltpu.make_async_remote_copy(
                    src_ref=ins[n], dst_ref=outs[w].at[l, 1 - c], send_sem=send_sems.at[n], recv_sem=recv_sems.at[n],
                    device_id=sibling, device_id_type=MESH))
        for cp in local + sends:
            cp.start()
        for cp in recvs:
            cp.wait_recv()
        for cp in sends:
            cp.wait_send()
        for cp in local:
            cp.wait()

    return pl.pallas_call(
        body, name=name,
        out_shape=tuple(jax.ShapeDtypeStruct((nl, 2) + hs[0].shape, hs[0].dtype) for hs in halves),
        in_specs=[ANY] * (nw * nl), out_specs=(ANY,) * nw,
        scratch_shapes=[pltpu.SemaphoreType.DMA((nw * nl,)), pltpu.SemaphoreType.DMA((nw * nl,)),
                        pltpu.SemaphoreType.DMA((nw * nl,))],
        compiler_params=_params(),
    )(*flat)


def _add_own_half(grad, got, c_idx, name):
    _, _, r, n = grad.shape
    tr = _pick(r, 256, 8)

    def body(c_ref, g_ref, t_ref, o_ref):
        o_ref[...] = g_ref[...] + t_ref[...]

    return pl.pallas_call(
        body, name=name, out_shape=jax.ShapeDtypeStruct((N_CHIPS, r, n), F32),
        grid_spec=pltpu.PrefetchScalarGridSpec(
            num_scalar_prefetch=1, grid=(N_CHIPS, r // tr),
            in_specs=[pl.BlockSpec((None, None, tr, n), lambda j, i, c_ref: (j, c_ref[0], i, 0)),
                      pl.BlockSpec((None, tr, n), lambda j, i, c_ref: (j, i, 0))],
            out_specs=pl.BlockSpec((None, tr, n), lambda j, i, c_ref: (j, i, 0))),
        compiler_params=_params(),
    )(c_idx, grad, got)


def _sum_slabs(slabs, name):
    ns, r, n = slabs.shape
    tr = _pick(r, 256, 8)

    def body(s_ref, o_ref):
        acc = s_ref[0]
        for j in range(1, ns):
            acc = acc + s_ref[j]
        o_ref[...] = acc

    return pl.pallas_call(
        body, name=name, out_shape=jax.ShapeDtypeStruct((r, n), F32), grid=(r // tr,),
        in_specs=[pl.BlockSpec((ns, tr, n), lambda i: (0, i, 0))], out_specs=pl.BlockSpec((tr, n), lambda i: (i, 0)),
        compiler_params=_params(),
    )(slabs)


def _band_bias(rel_bias):
    h = rel_bias.shape[0]
    n_far = PAD - REL_CLIP + CHUNK
    far = jnp.broadcast_to(rel_bias[:, N_REL - 1:N_REL], (h, n_far))
    near = rel_bias[:, REL_CLIP - CHUNK + 1:N_REL - 1][:, ::-1]
    line = jnp.concatenate([far, near], axis=1)
    return jnp.stack([line[:, CHUNK - 1 - q:CHUNK - 1 - q + BAND] for q in range(CHUNK)], axis=1)


def _pack_rows(pieces):
    flat = jnp.concatenate([p.reshape(-1) for p in pieces])
    rows = -(-flat.shape[0] // (8 * LANES)) * 8
    return jnp.pad(flat, (0, rows * LANES - flat.shape[0])).reshape(rows, LANES)


def _unpack_rows(packed, shapes):
    flat = packed.reshape(-1)
    out, at = [], 0
    for shp in shapes:
        size = 1
        for n in shp:
            size *= n
        out.append(flat[at:at + size].reshape(shp))
        at += size
    return out


def _layer_fwd(x, mod, w, band, tag):
    s, d = x.shape
    row = lambda i: mod[i:i + 1]
    h1 = _norm_mod(x, row(1), row(0), f"norm_mix{tag}")
    proj = _matmul(h1, w["w_in"], form="nn", out_dtype=BF16, tm=_pick(s, 512), tn=_pick(w["w_in"].shape[1], 768),
                   tk=d, name=f"proj{tag}")
    oa = _attn_a_fwd(proj, band, f"attn_a{tag}")
    ob, ltot = _sb_fwd(proj, f"attn_b{tag}")
    cat = _out_norm(oa, ob, w["g_a"], w["g_b"], f"out_norm{tag}")
    mixed = _matmul(cat, w["w_out"], form="nn", out_dtype=F32, tm=_pick(s, 512), tn=_pick(d, 1024),
                    tk=cat.shape[1], name=f"mix_out{tag}")
    x1 = _residual(x, row(2), mixed, f"res_mix{tag}")
    h2 = _norm_mod(x1, row(4), row(3), f"norm_ffn{tag}")
    f2 = w["w_up"].shape[1]
    u = _matmul(h2, w["w_up"], form="nn", out_dtype=F32, tm=_pick(s, 512), tn=_pick(f2, 1408), tk=d, name=f"up{tag}")
    a = _conv_glu(u, w["conv_w"], w["conv_b"], f"conv_glu{tag}")
    f = _matmul(a, w["w_down"], form="nn", out_dtype=F32, tm=_pick(s, 512), tn=_pick(d, 1024),
                tk=_pick(f2 // 2, 1408), name=f"down{tag}")
    x2 = _residual(x1, row(5), f, f"res_ffn{tag}")
    saved = dict(x=x, h1=h1, proj=proj, oa=oa, ob=ob, ltot=ltot, cat=cat, mixed=mixed, x1=x1, h2=h2, u=u, a=a, f=f)
    return x2, saved


def _layer_bwd(dx2, sv, mod, w, band, tag):
    s, d = dx2.shape
    row = lambda i: mod[i:i + 1]
    f2 = w["w_up"].shape[1]
    ff = f2 // 2
    n_in = w["w_in"].shape[1]
    df, dgate_ffn = _gate_bwd(dx2, sv["f"], row(5), f"gate_ffn_bwd{tag}")
    da = _matmul(df, w["w_down"], form="nt", out_dtype=F32, tm=_pick(s, 512), tn=_pick(ff, 1408), tk=d, name=f"down_dx{tag}")
    g_down = _matmul(sv["a"], df, form="tn", out_dtype=F32, tm=_pick(ff, 1408), tn=_pick(d, 512), tk=_pick(s, 512),
                     name=f"down_dw{tag}")
    du2, dcw, dcb = _conv_glu_bwd(sv["u"], da, w["conv_w"], w["conv_b"], f"conv_glu_bwd{tag}")
    du = jnp.concatenate([du2[0], du2[1]], axis=1)
    dh2 = _matmul(du, w["w_up"], form="nt", out_dtype=F32, tm=_pick(s, 512), tn=_pick(d, 1024), tk=_pick(f2, 1408),
                  name=f"up_dx{tag}")
    g_up = _matmul(sv["h2"], du, form="tn", out_dtype=F32, tm=_pick(d, 512), tn=_pick(f2 // N_CHIPS, 1408),
                   tk=_pick(s, 512), name=f"up_dw{tag}", shard_cols=f2 // N_CHIPS)
    dx1, dscale_ffn, dshift_ffn = _norm_mod_bwd(sv["x1"], dh2, dx2, row(4), f"norm_ffn_bwd{tag}")
    dmixed, dgate_mix = _gate_bwd(dx1, sv["mixed"], row(2), f"gate_mix_bwd{tag}")
    dcat = _matmul(dmixed, w["w_out"], form="nt", out_dtype=F32, tm=_pick(s, 512), tn=_pick(2 * W_GRP, 1024), tk=d,
                   name=f"mix_out_dx{tag}")
    g_out = _matmul(sv["cat"], dmixed, form="tn", out_dtype=F32, tm=_pick(2 * W_GRP, 512), tn=_pick(d, 1024),
                    tk=_pick(s, 512), name=f"mix_out_dw{tag}")
    doa, dob, dg_a, dg_b = _out_norm_bwd(sv["oa"], sv["ob"], dcat, w["g_a"], w["g_b"], f"out_norm_bwd{tag}")
    dqa, dka, dva, dband = _attn_a_bwd(sv["proj"], band, doa, f"attn_a_bwd{tag}")
    dqb, dkb, dvb = _sb_bwd(sv["proj"], sv["ltot"], dob, f"attn_b_bwd{tag}")
    drel = _rel_bias_grad(jnp.transpose(dband, (1, 0, 2)), f"rel_bias_bwd{tag}")[:, :N_REL]
    dproj = jnp.concatenate([dqa, dka, dva, dqb, dkb, dvb], axis=1).astype(BF16)
    dh1 = _matmul(dproj, w["w_in"], form="nt", out_dtype=F32, tm=_pick(s, 512), tn=_pick(d, 1024), tk=_pick(n_in, 1024),
                  name=f"proj_dx{tag}")
    g_in = _matmul(sv["h1"], dproj, form="tn", out_dtype=F32, tm=_pick(d, 512), tn=_pick(n_in // N_CHIPS, 768),
                   tk=_pick(s, 512), name=f"proj_dw{tag}", shard_cols=n_in // N_CHIPS)
    dx, dscale_mix, dshift_mix = _norm_mod_bwd(sv["x"], dh1, dx1, row(1), f"norm_mix_bwd{tag}")
    dmod = jnp.concatenate([dshift_mix, dscale_mix, dgate_mix, dshift_ffn, dscale_ffn, dgate_ffn], axis=1)
    big = dict(w_in=g_in, w_out=g_out, w_up=g_up, w_down=g_down)
    dconv_w = jnp.concatenate([dcw[0], dcw[1]], axis=1)
    dconv_b = jnp.concatenate([dcb[0], dcb[1]], axis=1)
    small = dict(dmod=dmod, rel_bias=drel, g_a=dg_a, g_b=dg_b, conv_w=dconv_w, conv_b=dconv_b)
    return dx, big, small


def _kernel_unoverlapped(x, c, w_ada, b_ada, w_in, rel_bias, g_a, g_b, w_out, w_up, conv_w, conv_b, w_down, final_g, loss_target, m_w_ada, m_b_ada, m_w_in, m_rel_bias, m_g_a, m_g_b, m_w_out, m_w_up, m_conv_w, m_conv_b, m_w_down, m_final_g, v_w_ada, v_b_ada, v_w_in, v_rel_bias, v_g_a, v_g_b, v_w_out, v_w_up, v_conv_w, v_conv_b, v_w_down, v_final_g):
    xi, yi, ci = _my_place()
    chip = 2 * xi + yi
    dev = 4 * xi + 2 * yi + ci
    nl, d, n_ada = w_ada.shape
    s = x.shape[1]
    f2 = N_CHIPS * w_up.shape[2]
    nc = conv_w.shape[2]

    c_pad = jnp.pad(c, ((0, 7), (0, 0)))
    c_all = _allgather8(c_pad, "gather_c")[0::8]
    b_mine = lax.dynamic_slice_in_dim(b_ada, chip * n_ada, n_ada, axis=1)[:, None, :]
    mod_shard = _ada_fwd(c_all, w_ada, b_mine, "ada")
    pack2 = _pack_rows([mod_shard, conv_w])
    got2 = _allgather8(pack2, "gather_mod").reshape(N_DEV, -1)
    mods, convs = [], []
    for j in range(N_CHIPS):
        ms, cw = _unpack_rows(got2[2 * j], [mod_shard.shape, conv_w.shape])
        mods.append(lax.dynamic_index_in_dim(ms, dev, axis=1, keepdims=False))
        convs.append(cw)
    mod = jnp.concatenate(mods, axis=1).reshape(nl, 6, d)
    conv_w_full = jnp.concatenate(convs, axis=2)

    names = ("w_in", "w_out", "w_up", "w_down")
    kinds = ("col", "row", "col", "row")
    shards = dict(w_in=w_in, w_out=w_out, w_up=w_up, w_down=w_down)
    full = _gather_weights([shards[n].astype(BF16) for n in names], kinds, "gather_weights")
    full = dict(zip(names, full))

    xs = x[0]
    layers, saved, bands = [], [], []
    for l in range(nl):
        w = {n: full[n][l] for n in names}
        w.update(g_a=g_a[l:l + 1], g_b=g_b[l:l + 1], conv_w=conv_w_full[l], conv_b=conv_b[l:l + 1])
        band = _band_bias(rel_bias[l])
        xs, sv = _layer_fwd(xs, mod[l], w, band, f"_l{l}")
        layers.append(w)
        bands.append(band)
        saved.append(sv)
    loss_part, dx, dfinal_g = _loss_head(xs, final_g[None, :], loss_target[0], "loss_head")
    loss = lax.psum(loss_part[0, 0], ("x", "y", "c"))

    big, small = [None] * nl, [None] * nl
    for l in reversed(range(nl)):
        dx, big[l], small[l] = _layer_bwd(dx, saved[l], mod[l], layers[l], bands[l], f"_l{l}")

    small_names = ("dmod", "rel_bias", "g_a", "g_b", "conv_w", "conv_b")
    pieces = [small[l][n] for l in range(nl) for n in small_names] + [dfinal_g]
    shapes = [p.shape for p in pieces]
    pack3 = _pack_rows(pieces)
    got3 = _allgather8(pack3, "gather_small").reshape(N_DEV, pack3.shape[0], LANES)
    summed = _unpack_rows(_sum_slabs(got3, "sum_small"), shapes)
    tot = [dict(zip(small_names, summed[len(small_names) * l:len(small_names) * (l + 1)])) for l in range(nl)]
    g_final_g = summed[-1].reshape(-1)
    g_b_ada = jnp.stack([tot[l]["dmod"].reshape(-1) for l in range(nl)])
    g_rel = jnp.stack([tot[l]["rel_bias"] for l in range(nl)])
    g_ga = jnp.stack([tot[l]["g_a"].reshape(-1) for l in range(nl)])
    g_gb = jnp.stack([tot[l]["g_b"].reshape(-1) for l in range(nl)])
    g_conv_b = jnp.stack([tot[l]["conv_b"].reshape(-1) for l in range(nl)])
    g_conv_w = jnp.stack([lax.dynamic_slice_in_dim(tot[l]["conv_w"], chip * nc, nc, axis=1) for l in range(nl)])
    per_dev = [_unpack_rows(got3[j], shapes) for j in range(N_DEV)]
    dmod_all = jnp.stack([jnp.stack([per_dev[j][len(small_names) * l].reshape(-1) for j in range(N_DEV)])
                          for l in range(nl)])
    g_w_ada = _ada_bwd(c_all, lax.dynamic_slice_in_dim(dmod_all, chip * n_ada, n_ada, axis=2), "ada_bwd")

    order = [(n, l) for n in names for l in range(nl)]
    flat_g = [big[l][n].reshape(N_CHIPS, 2, -1, 1024) for n, l in order]
    from_sib = _rs_to_sibling(flat_g, "rs_sibling")
    c_idx = jnp.reshape(ci, (1,)).astype(jnp.int32)
    chip_part = [_add_own_half(g, t, c_idx, f"rs_add_{n}_l{l}") for g, t, (n, l) in zip(flat_g, from_sib, order)]
    from_chips = _rs_to_chips(chip_part, "rs_chips")
    my_half = [_sum_slabs(t, f"rs_sum_{n}_l{l}") for t, (n, l) in zip(from_chips, order)]
    shard_g = _rs_share_halves([[my_half[nl * i + l] for l in range(nl)] for i in range(len(names))], "rs_halves")
    g_big = {n: shard_g[i].reshape(shards[n].shape) for i, n in enumerate(names)}

    grads = dict(w_ada=g_w_ada, b_ada=g_b_ada, rel_bias=g_rel, g_a=g_ga, g_b=g_gb, conv_w=g_conv_w, conv_b=g_conv_b,
                 final_g=g_final_g)
    weights = dict(w_ada=w_ada, b_ada=b_ada, w_in=w_in, rel_bias=rel_bias, g_a=g_a, g_b=g_b, w_out=w_out, w_up=w_up,
                   conv_w=conv_w, conv_b=conv_b, w_down=w_down, final_g=final_g)
    m_in = dict(w_ada=m_w_ada, b_ada=m_b_ada, w_in=m_w_in, rel_bias=m_rel_bias, g_a=m_g_a, g_b=m_g_b, w_out=m_w_out,
                w_up=m_w_up, conv_w=m_conv_w, conv_b=m_conv_b, w_down=m_w_down, final_g=m_final_g)
    v_in = dict(w_ada=v_w_ada, b_ada=v_b_ada, w_in=v_w_in, rel_bias=v_rel_bias, g_a=v_g_a, g_b=v_g_b, w_out=v_w_out,
                w_up=v_w_up, conv_w=v_conv_w, conv_b=v_conv_b, w_down=v_w_down, final_g=v_final_g)
    order_w = ("w_ada", "b_ada", "w_in", "rel_bias", "g_a", "g_b", "w_out", "w_up", "conv_w", "conv_b", "w_down", "final_g")
    upd = {n: _adamw_nd(grads[n], weights[n], m_in[n], v_in[n], f"adamw_{n}") for n in grads}
    for n, mn, th in zip(BIG, mine, theirs):
        grads[n], *upd[n] = _adamw_halves(mn, th, c_idx, weights[n], m_in[n], v_in[n], f"adamw_{n}")
    return (loss, dx[None], *[grads[n] for n in order_w], *[upd[n][0] for n in order_w],
            *[upd[n][1] for n in order_w], *[upd[n][2] for n in order_w])


class _Rider:
    def __init__(self, ins, out_shapes, n_remote, n_local, parts):
        self.ins = list(ins)
        self.out_shapes = list(out_shapes)
        self.scratch = [pltpu.SemaphoreType.DMA((n_remote,)), pltpu.SemaphoreType.DMA((n_remote,)),
                        pltpu.SemaphoreType.DMA((max(n_local, 1),))]
        self.parts = parts

    def start(self, in_refs, out_refs, sems):
        local, sends, _ = self.parts(in_refs, out_refs, sems)
        for cp in local() + sends():
            cp.start()

    def wait(self, in_refs, out_refs, sems):
        local, sends, recvs = self.parts(in_refs, out_refs, sems)
        for cp in recvs():
            cp.wait_recv()
        for cp in sends():
            cp.wait_send()
        for cp in local():
            cp.wait()


def _call_with_rider(body, rider, *, name, grid, out_shape, in_specs, out_specs, scratch_shapes, args):
    if rider is None:
        return pl.pallas_call(body, name=name, grid=grid, out_shape=tuple(out_shape), in_specs=list(in_specs),
                              out_specs=tuple(out_specs), scratch_shapes=list(scratch_shapes),
                              compiler_params=_params())(*args)
    n_in, n_out, n_scr = len(in_specs), len(out_specs), len(scratch_shapes)
    r_in, r_out = len(rider.ins), len(rider.out_shapes)

    def both(*refs):
        at = 0
        groups = []
        for size in (n_in, r_in, n_out, r_out, n_scr, len(rider.scratch)):
            groups.append(refs[at:at + size])
            at += size
        own_in, ride_in, own_out, ride_out, own_scr, sems = groups
        steps = [pl.program_id(a) for a in range(len(grid))]
        first = functools.reduce(jnp.logical_and, [st == 0 for st in steps])
        last = functools.reduce(jnp.logical_and, [st == g - 1 for st, g in zip(steps, grid)])

        @pl.when(first)
        def _():
            rider.start(ride_in, ride_out, sems)

        body(*own_in, *own_out, *own_scr)

        @pl.when(last)
        def _():
            rider.wait(ride_in, ride_out, sems)

    outs = pl.pallas_call(
        both, name=name, grid=grid, out_shape=tuple(out_shape) + tuple(rider.out_shapes),
        in_specs=list(in_specs) + [ANY] * r_in, out_specs=tuple(out_specs) + (ANY,) * r_out,
        scratch_shapes=list(scratch_shapes) + rider.scratch, compiler_params=_params(),
    )(*args, *rider.ins)
    return tuple(outs[:n_out]) + (list(outs[n_out:]),)


def _run_rider(rider, name):
    r_in, r_out = len(rider.ins), len(rider.out_shapes)

    def body(*refs):
        ins, outs, sems = refs[:r_in], refs[r_in:r_in + r_out], refs[r_in + r_out:]
        rider.start(ins, outs, sems)
        rider.wait(ins, outs, sems)

    return list(pl.pallas_call(
        body, name=name, out_shape=tuple(rider.out_shapes), in_specs=[ANY] * r_in, out_specs=(ANY,) * r_out,
        scratch_shapes=rider.scratch, compiler_params=_params(),
    )(*rider.ins))


def _remote(src, dst, sems, n, peer):
    return pltpu.make_async_remote_copy(src_ref=src, dst_ref=dst, send_sem=sems[0].at[n], recv_sem=sems[1].at[n],
                                        device_id=peer, device_id_type=MESH)


def _gather_rider(shards, kinds):
    nw = len(shards)
    out_shapes = [jax.ShapeDtypeStruct((a.shape[0], N_CHIPS * a.shape[1]) if kd == "col" else
                                       (N_CHIPS * a.shape[0], a.shape[1]), a.dtype) for a, kd in zip(shards, kinds)]

    def parts(ins, outs, sems):
        x, y, c = _my_place()
        chip = 2 * x + y
        peers = _chip_peers(x, y, c)

        def window(w, j):
            r, n = shards[w].shape
            if kinds[w] == "col":
                return outs[w].at[:, pl.ds(pl.multiple_of(j * n, LANES), n)]
            return outs[w].at[pl.ds(pl.multiple_of(j * r, 16), r), :]

        local = lambda: [pltpu.make_async_copy(ins[w], window(w, chip), sems[2].at[w]) for w in range(nw)]
        sends = lambda: [_remote(ins[w], window(w, chip), sems, 3 * w + k, peer)
                         for w in range(nw) for k, (_, peer) in enumerate(peers)]
        recvs = lambda: [_remote(ins[w], window(w, pchip), sems, 3 * w + k, peer)
                         for w in range(nw) for k, (pchip, peer) in enumerate(peers)]
        return local, sends, recvs

    return _Rider(shards, out_shapes, 3 * nw, nw, parts)


def _half(ref3, h, rows):
    return ref3.at[:, pl.ds(pl.multiple_of(h * rows, 8), rows), :]


def _sibling_rider(grads):
    nw = len(grads)
    out_shapes = [jax.ShapeDtypeStruct((g.shape[0], g.shape[1] // 2, g.shape[2]), g.dtype) for g in grads]

    def parts(ins, outs, sems):
        x, y, c = _my_place()
        sibling = (x, y, 1 - c)
        copies = lambda: [_remote(_half(ins[w], 1 - c, grads[w].shape[1] // 2), outs[w], sems, w, sibling)
                          for w in range(nw)]
        return (lambda: []), copies, copies

    return _Rider(grads, out_shapes, nw, 0, parts)


def _chips_rider(parts_in):
    nw = len(parts_in)
    out_shapes = [jax.ShapeDtypeStruct(p.shape, p.dtype) for p in parts_in]

    def parts(ins, outs, sems):
        x, y, c = _my_place()
        chip = 2 * x + y
        peers = _chip_peers(x, y, c)
        local = lambda: [pltpu.make_async_copy(ins[w].at[chip], outs[w].at[chip], sems[2].at[w]) for w in range(nw)]
        sends = lambda: [_remote(ins[w].at[pchip], outs[w].at[chip], sems, 3 * w + k, peer)
                         for w in range(nw) for k, (pchip, peer) in enumerate(peers)]
        recvs = lambda: [_remote(ins[w].at[chip], outs[w].at[pchip], sems, 3 * w + k, peer)
                         for w in range(nw) for k, (pchip, peer) in enumerate(peers)]
        return local, sends, recvs

    return _Rider(parts_in, out_shapes, 3 * nw, nw, parts)


def _halves_rider(halves):
    nw, nl = len(halves), len(halves[0])
    flat = [h for hs in halves for h in hs]
    out_shapes = [jax.ShapeDtypeStruct((nl, 2 * hs[0].shape[0], hs[0].shape[1]), hs[0].dtype) for hs in halves]

    def parts(ins, outs, sems):
        x, y, c = _my_place()
        sibling = (x, y, 1 - c)

        def window(w, l, h):
            rows = halves[w][0].shape[0]
            return outs[w].at[l, pl.ds(pl.multiple_of(h * rows, 8), rows), :]

        pairs = [(w, l) for w in range(nw) for l in range(nl)]
        local = lambda: [pltpu.make_async_copy(ins[nl * w + l], window(w, l, c), sems[2].at[nl * w + l]) for w, l in pairs]
        sends = lambda: [_remote(ins[nl * w + l], window(w, l, c), sems, nl * w + l, sibling) for w, l in pairs]
        recvs = lambda: [_remote(ins[nl * w + l], window(w, l, 1 - c), sems, nl * w + l, sibling) for w, l in pairs]
        return local, sends, recvs

    return _Rider(flat, out_shapes, nw * nl, nw * nl, parts)


def _add_my_half(grad, got, c_idx, name):
    _, r, n = got.shape
    tr = _pick(r, 256, 8)
    nblk = r // tr

    def body(c_ref, g_ref, t_ref, o_ref):
        o_ref[...] = g_ref[...] + t_ref[...]

    return pl.pallas_call(
        body, name=name, out_shape=jax.ShapeDtypeStruct(got.shape, F32),
        grid_spec=pltpu.PrefetchScalarGridSpec(
            num_scalar_prefetch=1, grid=(N_CHIPS, nblk),
            in_specs=[pl.BlockSpec((None, tr, n), lambda j, i, c_ref: (j, c_ref[0] * nblk + i, 0)),
                      pl.BlockSpec((None, tr, n), lambda j, i, c_ref: (j, i, 0))],
            out_specs=pl.BlockSpec((None, tr, n), lambda j, i, c_ref: (j, i, 0))),
        compiler_params=_params(),
    )(c_idx, grad, got)


def _swap_rider(mine):
    nw = len(mine)
    out_shapes = [jax.ShapeDtypeStruct(a.shape, a.dtype) for a in mine]

    def parts(ins, outs, sems):
        x, y, c = _my_place()
        copies = lambda: [_remote(ins[w], outs[w], sems, w, (x, y, 1 - c)) for w in range(nw)]
        return (lambda: []), copies, copies

    return _Rider(mine, out_shapes, nw, 0, parts)


def _sum_layers(slabs, name):
    nl = len(slabs)
    ns, r, n = slabs[0].shape
    tr = _pick(r, 256, 8)

    def body(*refs):
        o_ref = refs[nl]
        for l in range(nl):
            acc = refs[l][0]
            for j in range(1, ns):
                acc = acc + refs[l][j]
            o_ref[l] = acc

    return pl.pallas_call(
        body, name=name, out_shape=jax.ShapeDtypeStruct((nl, r, n), F32), grid=(r // tr,),
        in_specs=[pl.BlockSpec((ns, tr, n), lambda i: (0, i, 0))] * nl,
        out_specs=pl.BlockSpec((nl, tr, n), lambda i: (0, i, 0)), compiler_params=_params(),
    )(*slabs)


def _adam_math(gv, w, m, v):
    c1 = 1.0 - ADAM_B1 ** ADAM_STEP
    c2 = 1.0 - ADAM_B2 ** ADAM_STEP
    nm = ADAM_B1 * m + (1.0 - ADAM_B1) * gv
    nv = ADAM_B2 * v + (1.0 - ADAM_B2) * (gv * gv)
    return -ADAM_LR * ((nm / c1) / (jnp.sqrt(nv / c2) + ADAM_EPS) + ADAM_WD * w), nm, nv


def _adamw_halves(mine, theirs, c_idx, w, m, v, name):
    nl, r, n = mine.shape
    tr = _pick(r, 256, 8)
    nblk = r // tr

    def body(c_ref, mine_ref, theirs_ref, w_ref, m_ref, v_ref, g_ref, d_ref, nm_ref, nv_ref):
        gv = jnp.where(pl.program_id(1) == c_ref[0], mine_ref[...], theirs_ref[...])
        g_ref[...] = gv
        d_ref[...], nm_ref[...], nv_ref[...] = _adam_math(gv, w_ref[...], m_ref[...], v_ref[...])

    half = pl.BlockSpec((None, tr, n), lambda l, h, i, c_ref: (l, i, 0))
    full = pl.BlockSpec((None, tr, n), lambda l, h, i, c_ref: (l, h * nblk + i, 0))
    return pl.pallas_call(
        body, name=name, out_shape=(jax.ShapeDtypeStruct(w.shape, F32),) * 4,
        grid_spec=pltpu.PrefetchScalarGridSpec(
            num_scalar_prefetch=1, grid=(nl, 2, nblk), in_specs=[half, half, full, full, full],
            out_specs=(full,) * 4),
        compiler_params=_params(),
    )(c_idx, mine, theirs, w, m, v)


def _by_chip(g):
    return g if g.ndim == 3 else g.reshape(N_CHIPS, g.shape[0] // N_CHIPS, g.shape[1])


BIG = ("w_in", "w_out", "w_up", "w_down")
BIG_KIND = dict(w_in="col", w_out="row", w_up="col", w_down="row")


def _forward_layer(x, mod, w, band, tag, rider=None, on_arrival=None):
    s, d = x.shape
    row = lambda i: mod[i:i + 1]
    h1 = _norm_mod(x, row(1), row(0), f"norm_mix{tag}")
    proj = _matmul(h1, w["w_in"], form="nn", out_dtype=BF16, tm=_pick(s, 512), tn=_pick(w["w_in"].shape[1], 768),
                   tk=d, name=f"proj{tag}")
    oa = _attn_a_fwd(proj, band, f"attn_a{tag}")
    if rider is None:
        ob, ltot = _sb_fwd(proj, f"attn_b{tag}")
    else:
        ob, ltot, arrived = _sb_fwd(proj, f"attn_b{tag}", rider)
        on_arrival(arrived)
    cat = _out_norm(oa, ob, w["g_a"], w["g_b"], f"out_norm{tag}")
    mixed = _matmul(cat, w["w_out"], form="nn", out_dtype=F32, tm=_pick(s, 512), tn=_pick(d, 1024),
                    tk=cat.shape[1], name=f"mix_out{tag}")
    x1 = _residual(x, row(2), mixed, f"res_mix{tag}")
    h2 = _norm_mod(x1, row(4), row(3), f"norm_ffn{tag}")
    f2 = w["w_up"].shape[1]
    u = _matmul(h2, w["w_up"], form="nn", out_dtype=F32, tm=_pick(s, 512), tn=_pick(f2, 1408), tk=d, name=f"up{tag}")
    a = _conv_glu(u, w["conv_w"], w["conv_b"], f"conv_glu{tag}")
    f = _matmul(a, w["w_down"], form="nn", out_dtype=F32, tm=_pick(s, 512), tn=_pick(d, 1024),
                tk=_pick(f2 // 2, 1408), name=f"down{tag}")
    x2 = _residual(x1, row(5), f, f"res_ffn{tag}")
    saved = dict(x=x, h1=h1, proj=proj, oa=oa, ob=ob, ltot=ltot, cat=cat, mixed=mixed, x1=x1, h2=h2, u=u, a=a, f=f)
    return x2, saved


def _backward_layer(dx2, sv, mod, w, band, tag, c_idx, waiting=None):
    s, d = dx2.shape
    row = lambda i: mod[i:i + 1]
    f2 = w["w_up"].shape[1]
    ff = f2 // 2
    n_in = w["w_in"].shape[1]
    df, dgate_ffn = _gate_bwd(dx2, sv["f"], row(5), f"gate_ffn_bwd{tag}")
    da = _matmul(df, w["w_down"], form="nt", out_dtype=F32, tm=_pick(s, 512), tn=_pick(ff, 1408), tk=d, name=f"down_dx{tag}")
    g_down = _matmul(sv["a"], df, form="tn", out_dtype=F32, tm=_pick(ff, 1408), tn=_pick(d, 512), tk=_pick(s, 512),
                     name=f"down_dw{tag}")
    du2, dcw, dcb = _conv_glu_bwd(sv["u"], da, w["conv_w"], w["conv_b"], f"conv_glu_bwd{tag}")
    du = jnp.concatenate([du2[0], du2[1]], axis=1)
    dh2 = _matmul(du, w["w_up"], form="nt", out_dtype=F32, tm=_pick(s, 512), tn=_pick(d, 1024), tk=_pick(f2, 1408),
                  name=f"up_dx{tag}")
    g_up = _matmul(sv["h2"], du, form="tn", out_dtype=F32, tm=_pick(d, 512), tn=_pick(f2 // N_CHIPS, 1408),
                   tk=_pick(s, 512), name=f"up_dw{tag}", shard_cols=f2 // N_CHIPS)
    dx1, dscale_ffn, dshift_ffn = _norm_mod_bwd(sv["x1"], dh2, dx2, row(4), f"norm_ffn_bwd{tag}")
    dmixed, dgate_mix = _gate_bwd(dx1, sv["mixed"], row(2), f"gate_mix_bwd{tag}")
    dcat = _matmul(dmixed, w["w_out"], form="nt", out_dtype=F32, tm=_pick(s, 512), tn=_pick(2 * W_GRP, 1024), tk=d,
                   name=f"mix_out_dx{tag}")
    g_out = _matmul(sv["cat"], dmixed, form="tn", out_dtype=F32, tm=_pick(2 * W_GRP, 512), tn=_pick(d, 1024),
                    tk=_pick(s, 512), name=f"mix_out_dw{tag}")
    doa, dob, dg_a, dg_b = _out_norm_bwd(sv["oa"], sv["ob"], dcat, w["g_a"], w["g_b"], f"out_norm_bwd{tag}")
    big = {("w_down", tag): _by_chip(g_down), ("w_up", tag): _by_chip(g_up), ("w_out", tag): _by_chip(g_out)}
    if waiting is None:
        dqa, dka, dva, dband = _attn_a_bwd(sv["proj"], band, doa, f"attn_a_bwd{tag}")
        dqb, dkb, dvb = _sb_bwd(sv["proj"], sv["ltot"], dob, f"attn_b_bwd{tag}")
    else:
        raw = {**waiting, **big}
        keys = list(raw)
        dqa, dka, dva, dband, from_sib = _attn_a_bwd(sv["proj"], band, doa, f"attn_a_bwd{tag}",
                                                     _sibling_rider([raw[k] for k in keys]))
        chip_part = [_add_my_half(raw[k], t, c_idx, f"rs_add_{k[0]}{k[1]}") for k, t in zip(keys, from_sib)]
        dqb, dkb, dvb, from_chips = _sb_bwd(sv["proj"], sv["ltot"], dob, f"attn_b_bwd{tag}", _chips_rider(chip_part))
        big = dict(zip(keys, from_chips))
    drel = _rel_bias_grad(jnp.transpose(dband, (1, 0, 2)), f"rel_bias_bwd{tag}")[:, :N_REL]
    dproj = jnp.concatenate([dqa, dka, dva, dqb, dkb, dvb], axis=1).astype(BF16)
    dh1 = _matmul(dproj, w["w_in"], form="nt", out_dtype=F32, tm=_pick(s, 512), tn=_pick(d, 1024), tk=_pick(n_in, 1024),
                  name=f"proj_dx{tag}")
    g_in = _matmul(sv["h1"], dproj, form="tn", out_dtype=F32, tm=_pick(d, 512), tn=_pick(n_in // N_CHIPS, 768),
                   tk=_pick(s, 512), name=f"proj_dw{tag}", shard_cols=n_in // N_CHIPS)
    dx, dscale_mix, dshift_mix = _norm_mod_bwd(sv["x"], dh1, dx1, row(1), f"norm_mix_bwd{tag}")
    dmod = jnp.concatenate([dshift_mix, dscale_mix, dgate_mix, dshift_ffn, dscale_ffn, dgate_ffn], axis=1)
    dconv_w = jnp.concatenate([dcw[0], dcw[1]], axis=1)
    dconv_b = jnp.concatenate([dcb[0], dcb[1]], axis=1)
    small = dict(dmod=dmod, rel_bias=drel, g_a=dg_a, g_b=dg_b, conv_w=dconv_w, conv_b=dconv_b)
    return dx, big, g_in, small


def kernel(x, c, w_ada, b_ada, w_in, rel_bias, g_a, g_b, w_out, w_up, conv_w, conv_b, w_down, final_g, loss_target, m_w_ada, m_b_ada, m_w_in, m_rel_bias, m_g_a, m_g_b, m_w_out, m_w_up, m_conv_w, m_conv_b, m_w_down, m_final_g, v_w_ada, v_b_ada, v_w_in, v_rel_bias, v_g_a, v_g_b, v_w_out, v_w_up, v_conv_w, v_conv_b, v_w_down, v_final_g):
    xi, yi, ci = _my_place()
    chip = 2 * xi + yi
    dev = 4 * xi + 2 * yi + ci
    c_idx = jnp.reshape(ci, (1,)).astype(jnp.int32)
    nl, d, n_ada = w_ada.shape
    nc = conv_w.shape[2]
    assert nl == 2

    c_pad = jnp.pad(c, ((0, 7), (0, 0)))
    c_all = _allgather8(c_pad, "gather_c")[0::8]
    b_mine = lax.dynamic_slice_in_dim(b_ada, chip * n_ada, n_ada, axis=1)[:, None, :]
    mod_shard = _ada_fwd(c_all, w_ada, b_mine, "ada")
    pack2 = _pack_rows([mod_shard, conv_w])
    got2 = _allgather8(pack2, "gather_mod").reshape(N_DEV, -1)
    mods, convs = [], []
    for j in range(N_CHIPS):
        ms, cw = _unpack_rows(got2[2 * j], [mod_shard.shape, conv_w.shape])
        mods.append(lax.dynamic_index_in_dim(ms, dev, axis=1, keepdims=False))
        convs.append(cw)
    mod = jnp.concatenate(mods, axis=1).reshape(nl, 6, d)
    conv_w_full = jnp.concatenate(convs, axis=2)

    shards = dict(w_in=w_in, w_out=w_out, w_up=w_up, w_down=w_down)
    sh = {(n, l): shards[n][l].astype(BF16) for n in BIG for l in range(nl)}
    early = [("w_in", 0), ("w_out", 0)]
    late = [("w_up", 0), ("w_down", 0)] + [(n, 1) for n in BIG]
    layers = [dict(g_a=g_a[l:l + 1], g_b=g_b[l:l + 1], conv_w=conv_w_full[l], conv_b=conv_b[l:l + 1]) for l in range(nl)]
    got = _run_rider(_gather_rider([sh[k] for k in early], [BIG_KIND[k[0]] for k in early]), "gather_first")
    for (n, l), full in zip(early, got):
        layers[l][n] = full
    late_rider = _gather_rider([sh[k] for k in late], [BIG_KIND[k[0]] for k in late])

    def on_arrival(arrived):
        for (n, l), full in zip(late, arrived):
            layers[l][n] = full

    xs = x[0]
    saved, bands = [], []
    for l in range(nl):
        band = _band_bias(rel_bias[l])
        xs, sv = _forward_layer(xs, mod[l], layers[l], band, f"_l{l}", late_rider if l == 0 else None, on_arrival)
        bands.append(band)
        saved.append(sv)
    loss_part, dx, dfinal_g = _loss_head(xs, final_g[None, :], loss_target[0], "loss_head")
    loss = lax.psum(loss_part[0, 0], ("x", "y", "c"))

    small = [None] * nl
    dx, raw1, g_in1, small[1] = _backward_layer(dx, saved[1], mod[1], layers[1], bands[1], "_l1", c_idx)
    raw1[("w_in", "_l1")] = g_in1
    dx, parts, g_in0, small[0] = _backward_layer(dx, saved[0], mod[0], layers[0], bands[0], "_l0", c_idx, raw1)
    (from_sib,) = _run_rider(_sibling_rider([g_in0]), "rs_sibling_last")
    (parts[("w_in", "_l0")],) = _run_rider(
        _chips_rider([_add_my_half(g_in0, from_sib, c_idx, "rs_add_w_in_l0")]), "rs_chips_last")
    mine = [_sum_layers([parts[(n, f"_l{l}")] for l in range(nl)], f"rs_sum_{n}") for n in BIG]
    theirs = _run_rider(_swap_rider(mine), "rs_swap")

    small_names = ("dmod", "rel_bias", "g_a", "g_b", "conv_w", "conv_b")
    pieces = [small[l][n] for l in range(nl) for n in small_names] + [dfinal_g]
    shapes = [p.shape for p in pieces]
    pack3 = _pack_rows(pieces)
    got3 = _allgather8(pack3, "gather_small").reshape(N_DEV, pack3.shape[0], LANES)
    summed = _unpack_rows(_sum_slabs(got3, "sum_small"), shapes)
    tot = [dict(zip(small_names, summed[len(small_names) * l:len(small_names) * (l + 1)])) for l in range(nl)]
    g_final_g = summed[-1].reshape(-1)
    g_b_ada = jnp.stack([tot[l]["dmod"].reshape(-1) for l in range(nl)])
    g_rel = jnp.stack([tot[l]["rel_bias"] for l in range(nl)])
    g_ga = jnp.stack([tot[l]["g_a"].reshape(-1) for l in range(nl)])
    g_gb = jnp.stack([tot[l]["g_b"].reshape(-1) for l in range(nl)])
    g_conv_b = jnp.stack([tot[l]["conv_b"].reshape(-1) for l in range(nl)])
    g_conv_w = jnp.stack([lax.dynamic_slice_in_dim(tot[l]["conv_w"], chip * nc, nc, axis=1) for l in range(nl)])
    per_dev = [_unpack_rows(got3[j], shapes) for j in range(N_DEV)]
    dmod_all = jnp.stack([jnp.stack([per_dev[j][len(small_names) * l].reshape(-1) for j in range(N_DEV)])
                          for l in range(nl)])
    g_w_ada = _ada_bwd(c_all, lax.dynamic_slice_in_dim(dmod_all, chip * n_ada, n_ada, axis=2), "ada_bwd")

    grads = dict(w_ada=g_w_ada, b_ada=g_b_ada, rel_bias=g_rel, g_a=g_ga, g_b=g_gb, conv_w=g_conv_w, conv_b=g_conv_b,
                 final_g=g_final_g)
    weights = dict(w_ada=w_ada, b_ada=b_ada, w_in=w_in, rel_bias=rel_bias, g_a=g_a, g_b=g_b, w_out=w_out, w_up=w_up,
                   conv_w=conv_w, conv_b=conv_b, w_down=w_down, final_g=final_g)
    m_in = dict(w_ada=m_w_ada, b_ada=m_b_ada, w_in=m_w_in, rel_bias=m_rel_bias, g_a=m_g_a, g_b=m_g_b, w_out=m_w_out,
                w_up=m_w_up, conv_w=m_conv_w, conv_b=m_conv_b, w_down=m_w_down, final_g=m_final_g)
    v_in = dict(w_ada=v_w_ada, b_ada=v_b_ada, w_in=v_w_in, rel_bias=v_rel_bias, g_a=v_g_a, g_b=v_g_b, w_out=v_w_out,
                w_up=v_w_up, conv_w=v_conv_w, conv_b=v_conv_b, w_down=v_w_down, final_g=v_final_g)
    order_w = ("w_ada", "b_ada", "w_in", "rel_bias", "g_a", "g_b", "w_out", "w_up", "conv_w", "conv_b", "w_down", "final_g")
    upd = {n: _adamw_nd(grads[n], weights[n], m_in[n], v_in[n], f"adamw_{n}") for n in grads}
    for n, mn, th in zip(BIG, mine, theirs):
        grads[n], *upd[n] = _adamw_halves(mn, th, c_idx, weights[n], m_in[n], v_in[n], f"adamw_{n}")
    return (loss, dx[None], *[grads[n] for n in order_w], *[upd[n][0] for n in order_w],
            *[upd[n][1] for n in order_w], *[upd[n][2] for n in order_w])
```

```python
import functools

import jax
import jax.numpy as jnp
from jax import lax
from jax.experimental import pallas as pl
from jax.experimental.pallas import tpu as pltpu

F32 = jnp.float32
BF16 = jnp.bfloat16
MESH = pl.DeviceIdType.MESH
ANY = pl.BlockSpec(memory_space=pl.ANY)
VMEM_FULL = pl.BlockSpec(memory_space=pltpu.VMEM)

HEAD_DIM = 64
N_HEADS = 8
W_GRP = N_HEADS * HEAD_DIM
CHUNK = 64
N_PREV = 8
BAND = (N_PREV + 1) * CHUNK
PAD = N_PREV * CHUNK
REL_CLIP = 128
N_REL = 2 * REL_CLIP + 1
EPS = 1e-6
N_CHIPS = 4
N_DEV = 8
LANES = 128
V7X_VMEM_LIMIT = 56 * 1024 * 1024

ADAM_LR = 0.001
ADAM_B1 = 0.9
ADAM_B2 = 0.999
ADAM_EPS = 1e-08
ADAM_WD = 0.01
ADAM_STEP = 10


def _params(**kw):
    return pltpu.CompilerParams(vmem_limit_bytes=V7X_VMEM_LIMIT, **kw)


def _pick(dim, pref, mult=LANES):
    t = (min(pref, dim) // mult) * mult
    while t >= mult:
        if dim % t == 0:
            return t
        t -= mult
    return dim


def _my_place():
    return lax.axis_index("x"), lax.axis_index("y"), lax.axis_index("c")


def _flip(v, bit):
    return 1 - v if bit else v


def _matmul(a, b, *, form, out_dtype, tm, tn, tk, name, shard_cols=None):
    if form == "nn":
        (m, k), (_, n) = a.shape, b.shape
        a_spec = pl.BlockSpec((tm, tk), lambda i, j, kk: (i, kk))
        b_spec = pl.BlockSpec((tk, tn), lambda i, j, kk: (kk, j))
        dims = (((1,), (0,)), ((), ()))
    elif form == "nt":
        (m, k), (n, _) = a.shape, b.shape
        a_spec = pl.BlockSpec((tm, tk), lambda i, j, kk: (i, kk))
        b_spec = pl.BlockSpec((tn, tk), lambda i, j, kk: (j, kk))
        dims = (((1,), (1,)), ((), ()))
    else:
        (k, m), (_, n) = a.shape, b.shape
        a_spec = pl.BlockSpec((tk, tm), lambda i, j, kk: (kk, i))
        b_spec = pl.BlockSpec((tk, tn), lambda i, j, kk: (kk, j))
        dims = (((0,), (0,)), ((), ()))
    assert m % tm == 0 and n % tn == 0 and k % tk == 0, (name, m, n, k, tm, tn, tk)
    nk = k // tk
    if shard_cols is None:
        out_shape = jax.ShapeDtypeStruct((m, n), out_dtype)
        o_spec = pl.BlockSpec((tm, tn), lambda i, j, kk: (i, j))
    else:
        per = shard_cols // tn
        assert shard_cols % tn == 0
        out_shape = jax.ShapeDtypeStruct((n // shard_cols, m, shard_cols), out_dtype)
        o_spec = pl.BlockSpec((None, tm, tn), lambda i, j, kk: (j // per, i, j % per))

    def body(a_ref, b_ref, o_ref, acc_ref):
        kk = pl.program_id(2)
        part = lax.dot_general(a_ref[...], b_ref[...], dims, preferred_element_type=F32)

        @pl.when(kk == 0)
        def _():
            acc_ref[...] = part

        @pl.when(kk > 0)
        def _():
            acc_ref[...] += part

        @pl.when(kk == nk - 1)
        def _():
            o_ref[...] = acc_ref[...].astype(out_dtype)

    return pl.pallas_call(
        body, name=name, out_shape=out_shape, grid=(m // tm, n // tn, nk),
        in_specs=[a_spec, b_spec], out_specs=o_spec,
        scratch_shapes=[pltpu.VMEM((tm, tn), F32)], compiler_params=_params(),
    )(a, b)


def _row_spec(tr, d):
    return pl.BlockSpec((tr, d), lambda i: (i, 0))


def _vec_spec(d):
    return pl.BlockSpec((1, d), lambda i: (0, 0))


def _rms(xf):
    r = lax.rsqrt(jnp.mean(xf * xf, axis=-1, keepdims=True) + EPS)
    return xf * r, r


def _norm_mod(x, scale, shift, name):
    s, d = x.shape
    tr = _pick(s, 512, 8)

    def body(x_ref, sc_ref, sh_ref, o_ref):
        n, _ = _rms(x_ref[...])
        o_ref[...] = (n * (1.0 + sc_ref[...]) + sh_ref[...]).astype(BF16)

    return pl.pallas_call(
        body, name=name, out_shape=jax.ShapeDtypeStruct((s, d), BF16), grid=(s // tr,),
        in_specs=[_row_spec(tr, d), _vec_spec(d), _vec_spec(d)], out_specs=_row_spec(tr, d),
        compiler_params=_params(),
    )(x, scale, shift)


def _out_norm(oa, ob, g_a, g_b, name):
    s, w = oa.shape
    tr = _pick(s, 512, 8)

    def body(oa_ref, ob_ref, ga_ref, gb_ref, o_ref):
        na, _ = _rms(oa_ref[...])
        nb, _ = _rms(ob_ref[...])
        o_ref[:, :w] = (na * ga_ref[...]).astype(BF16)
        o_ref[:, w:] = (nb * gb_ref[...]).astype(BF16)

    return pl.pallas_call(
        body, name=name, out_shape=jax.ShapeDtypeStruct((s, 2 * w), BF16), grid=(s // tr,),
        in_specs=[_row_spec(tr, w), _row_spec(tr, w), _vec_spec(w), _vec_spec(w)],
        out_specs=_row_spec(tr, 2 * w), compiler_params=_params(),
    )(oa, ob, g_a, g_b)


def _residual(x, gate, m, name):
    s, d = x.shape
    tr = _pick(s, 512, 8)

    def body(x_ref, g_ref, m_ref, o_ref):
        o_ref[...] = x_ref[...] + g_ref[...] * m_ref[...]

    return pl.pallas_call(
        body, name=name, out_shape=jax.ShapeDtypeStruct((s, d), F32), grid=(s // tr,),
        in_specs=[_row_spec(tr, d), _vec_spec(d), _row_spec(tr, d)], out_specs=_row_spec(tr, d),
        compiler_params=_params(),
    )(x, gate, m)


def _shift_down(u, k):
    rows = lax.broadcasted_iota(jnp.int32, u.shape, 0)
    return jnp.where(rows >= k, pltpu.roll(u, k, 0), 0.0)


def _shift_up(u, k):
    s = u.shape[0]
    rows = lax.broadcasted_iota(jnp.int32, u.shape, 0)
    return jnp.where(rows < s - k, pltpu.roll(u, s - k, 0), 0.0)


def _conv(u, w_ref, b_ref):
    return w_ref[0:1, :] * _shift_down(u, 2) + w_ref[1:2, :] * _shift_down(u, 1) + w_ref[2:3, :] * u + b_ref[...]


def _conv_glu(u, conv_w, conv_b, name):
    s, f2 = u.shape
    f = f2 // 2
    tc = LANES
    nb = f // tc

    def body(ug_ref, uv_ref, wg_ref, wv_ref, bg_ref, bv_ref, o_ref):
        g = _conv(ug_ref[...], wg_ref, bg_ref)
        v = _conv(uv_ref[...], wv_ref, bv_ref)
        o_ref[...] = (g * jax.nn.sigmoid(g) * v).astype(BF16)

    col = lambda off: pl.BlockSpec((s, tc), lambda j: (0, j + off))
    wcol = lambda off: pl.BlockSpec((3, tc), lambda j: (0, j + off))
    bcol = lambda off: pl.BlockSpec((1, tc), lambda j: (0, j + off))
    return pl.pallas_call(
        body, name=name, out_shape=jax.ShapeDtypeStruct((s, f), BF16), grid=(nb,),
        in_specs=[col(0), col(nb), wcol(0), wcol(nb), bcol(0), bcol(nb)], out_specs=col(0),
        compiler_params=_params(),
    )(u, u, conv_w, conv_w, conv_b, conv_b)


def _conv_glu_bwd(u, da, conv_w, conv_b, name):
    s, f2 = u.shape
    f = f2 // 2
    tc = LANES
    nb = f // tc

    def body(ug_ref, uv_ref, da_ref, wg_ref, wv_ref, bg_ref, bv_ref, du_ref, dw_ref, db_ref):
        da_ = da_ref[...]
        ug, uv = ug_ref[...], uv_ref[...]
        g = _conv(ug, wg_ref, bg_ref)
        v = _conv(uv, wv_ref, bv_ref)
        sg = jax.nn.sigmoid(g)
        dg = da_ * v * (sg * (1.0 + g * (1.0 - sg)))
        dv = da_ * (g * sg)
        for h, (dy, uu, w_ref) in enumerate(((dg, ug, wg_ref), (dv, uv, wv_ref))):
            du = w_ref[2:3, :] * dy + w_ref[1:2, :] * _shift_up(dy, 1) + w_ref[0:1, :] * _shift_up(dy, 2)
            du_ref[h] = du.astype(BF16)
            dw_ref[h, 0:1, :] = jnp.sum(dy * _shift_down(uu, 2), axis=0, keepdims=True)
            dw_ref[h, 1:2, :] = jnp.sum(dy * _shift_down(uu, 1), axis=0, keepdims=True)
            dw_ref[h, 2:3, :] = jnp.sum(dy * uu, axis=0, keepdims=True)
            db_ref[h] = jnp.sum(dy, axis=0, keepdims=True)

    col = lambda off: pl.BlockSpec((s, tc), lambda j: (0, j + off))
    wcol = lambda off: pl.BlockSpec((3, tc), lambda j: (0, j + off))
    bcol = lambda off: pl.BlockSpec((1, tc), lambda j: (0, j + off))
    return pl.pallas_call(
        body, name=name, grid=(nb,),
        out_shape=(jax.ShapeDtypeStruct((2, s, f), BF16), jax.ShapeDtypeStruct((2, 3, f), F32),
                   jax.ShapeDtypeStruct((2, 1, f), F32)),
        in_specs=[col(0), col(nb), col(0), wcol(0), wcol(nb), bcol(0), bcol(nb)],
        out_specs=(pl.BlockSpec((2, s, tc), lambda j: (0, 0, j)), pl.BlockSpec((2, 3, tc), lambda j: (0, 0, j)),
                   pl.BlockSpec((2, 1, tc), lambda j: (0, 0, j))),
        compiler_params=_params(),
    )(u, u, da, conv_w, conv_w, conv_b, conv_b)


def _accumulate(ref, val):
    @pl.when(pl.program_id(0) == 0)
    def _():
        ref[...] = val

    @pl.when(pl.program_id(0) > 0)
    def _():
        ref[...] += val


def _rms_bwd(n, r, dn):
    return r * (dn - n * jnp.mean(dn * n, axis=-1, keepdims=True))


def _loss_head(x, final_g, target, name):
    s, d = x.shape
    tr = _pick(s, 512, 8)

    def body(x_ref, g_ref, t_ref, loss_ref, dx_ref, dg_ref):
        n, r = _rms(x_ref[...])
        diff = n * g_ref[...] - t_ref[...]
        part = 0.5 * jnp.sum(jnp.sum(diff * diff, axis=1, keepdims=True), axis=0, keepdims=True) / d
        _accumulate(loss_ref, part)
        dy = diff / d
        _accumulate(dg_ref, jnp.sum(dy * n, axis=0, keepdims=True))
        dx_ref[...] = _rms_bwd(n, r, dy * g_ref[...])

    return pl.pallas_call(
        body, name=name, grid=(s // tr,),
        out_shape=(jax.ShapeDtypeStruct((1, 1), F32), jax.ShapeDtypeStruct((s, d), F32), jax.ShapeDtypeStruct((1, d), F32)),
        in_specs=[_row_spec(tr, d), _vec_spec(d), _row_spec(tr, d)],
        out_specs=(pl.BlockSpec((1, 1), lambda i: (0, 0)), _row_spec(tr, d), _vec_spec(d)),
        compiler_params=_params(),
    )(x, final_g, target)


def _gate_bwd(dx, m, gate, name):
    s, d = dx.shape
    tr = _pick(s, 512, 8)

    def body(dx_ref, m_ref, g_ref, dm_ref, dg_ref):
        dxv = dx_ref[...]
        dm_ref[...] = (dxv * g_ref[...]).astype(BF16)
        _accumulate(dg_ref, jnp.sum(dxv * m_ref[...], axis=0, keepdims=True))

    return pl.pallas_call(
        body, name=name, grid=(s // tr,),
        out_shape=(jax.ShapeDtypeStruct((s, d), BF16), jax.ShapeDtypeStruct((1, d), F32)),
        in_specs=[_row_spec(tr, d), _row_spec(tr, d), _vec_spec(d)], out_specs=(_row_spec(tr, d), _vec_spec(d)),
        compiler_params=_params(),
    )(dx, m, gate)


def _norm_mod_bwd(x, dh, dres, scale, name):
    s, d = x.shape
    tr = _pick(s, 512, 8)

    def body(x_ref, dh_ref, dr_ref, sc_ref, dx_ref, dsc_ref, dsh_ref):
        n, r = _rms(x_ref[...])
        dh_ = dh_ref[...]
        _accumulate(dsc_ref, jnp.sum(dh_ * n, axis=0, keepdims=True))
        _accumulate(dsh_ref, jnp.sum(dh_, axis=0, keepdims=True))
        dx_ref[...] = dr_ref[...] + _rms_bwd(n, r, dh_ * (1.0 + sc_ref[...]))

    return pl.pallas_call(
        body, name=name, grid=(s // tr,),
        out_shape=(jax.ShapeDtypeStruct((s, d), F32), jax.ShapeDtypeStruct((1, d), F32), jax.ShapeDtypeStruct((1, d), F32)),
        in_specs=[_row_spec(tr, d), _row_spec(tr, d), _row_spec(tr, d), _vec_spec(d)],
        out_specs=(_row_spec(tr, d), _vec_spec(d), _vec_spec(d)), compiler_params=_params(),
    )(x, dh, dres, scale)


def _out_norm_bwd(oa, ob, dcat, g_a, g_b, name):
    s, w = oa.shape
    tr = _pick(s, 512, 8)

    def body(oa_ref, ob_ref, dc_ref, ga_ref, gb_ref, doa_ref, dob_ref, dga_ref, dgb_ref):
        for o_ref, g_ref, do_ref, dg_ref, lo in ((oa_ref, ga_ref, doa_ref, dga_ref, 0), (ob_ref, gb_ref, dob_ref, dgb_ref, w)):
            n, r = _rms(o_ref[...])
            dc = dc_ref[:, lo:lo + w]
            _accumulate(dg_ref, jnp.sum(dc * n, axis=0, keepdims=True))
            do_ref[...] = _rms_bwd(n, r, dc * g_ref[...])

    return pl.pallas_call(
        body, name=name, grid=(s // tr,),
        out_shape=(jax.ShapeDtypeStruct((s, w), F32), jax.ShapeDtypeStruct((s, w), F32),
                   jax.ShapeDtypeStruct((1, w), F32), jax.ShapeDtypeStruct((1, w), F32)),
        in_specs=[_row_spec(tr, w), _row_spec(tr, w), _row_spec(tr, 2 * w), _vec_spec(w), _vec_spec(w)],
        out_specs=(_row_spec(tr, w), _row_spec(tr, w), _vec_spec(w), _vec_spec(w)), compiler_params=_params(),
    )(oa, ob, dcat, g_a, g_b)


def _head_masks():
    lane = lax.broadcasted_iota(jnp.int32, (1, LANES), 1)
    return lane < HEAD_DIM, lane >= HEAD_DIM


def _nt(a, b):
    return lax.dot_general(a, b, (((1,), (1,)), ((), ())), preferred_element_type=F32)


def _tn(a, b):
    return lax.dot_general(a, b, (((0,), (0,)), ((), ())), preferred_element_type=F32)


def _nn(a, b):
    return jnp.dot(a, b, preferred_element_type=F32)


def _only(mask, v):
    return jnp.where(mask, v, jnp.zeros_like(v))


def _fill_padded(dst_ref, src_ref):
    dst_ref[0:PAD, :] = jnp.zeros((PAD, LANES), dst_ref.dtype)
    dst_ref[PAD:, :] = src_ref[...]


def _chunk_probs(s, bias, chunk):
    pos = lax.broadcasted_iota(jnp.int32, (1, BAND), 1)
    s = jnp.where(pos >= (N_PREV - chunk) * CHUNK, s + bias, -1e30)
    e = jnp.exp(s - jnp.max(s, axis=1, keepdims=True))
    return e / jnp.sum(e, axis=1, keepdims=True)


def _band_windows(i, cq, kpad, vpad):
    chunks = [i * cq + cc for cc in range(cq)]
    starts = [pl.multiple_of(ch * CHUNK, CHUNK) for ch in chunks]
    return chunks, starts, [kpad[pl.ds(st, BAND), :] for st in starts], [vpad[pl.ds(st, BAND), :] for st in starts]


def _attn_a_fwd(proj, band_bias, name):
    s = proj.shape[0]
    cq = 4
    tq = cq * CHUNK
    npair = N_HEADS // 2
    kcol, vcol = W_GRP // LANES, 2 * W_GRP // LANES

    def body(q_ref, k_ref, v_ref, b_ref, o_ref, kpad, vpad):
        i = pl.program_id(1)
        masks = _head_masks()

        @pl.when(i == 0)
        def _():
            _fill_padded(kpad, k_ref)
            _fill_padded(vpad, v_ref)

        chunks, _, kbs, vbs = _band_windows(i, cq, kpad, vpad)
        q2 = q_ref[...] * (HEAD_DIM ** -0.5)
        units = [(cc, h) for cc in range(cq) for h in range(2)]
        ss = [_nt(_only(masks[h], q2[cc * CHUNK:(cc + 1) * CHUNK]), kbs[cc]) for cc, h in units]
        ps = [_chunk_probs(s_, b_ref[h], chunks[cc]).astype(BF16) for s_, (cc, h) in zip(ss, units)]
        for cc in range(cq):
            o_ref[cc * CHUNK:(cc + 1) * CHUNK, :] = (_nn(ps[2 * cc], _only(masks[0], vbs[cc]))
                                                     + _nn(ps[2 * cc + 1], _only(masks[1], vbs[cc])))

    return pl.pallas_call(
        body, name=name, out_shape=jax.ShapeDtypeStruct((s, W_GRP), F32), grid=(npair, s // tq),
        in_specs=[pl.BlockSpec((tq, LANES), lambda p, i: (i, p)),
                  pl.BlockSpec((s, LANES), lambda p, i: (0, kcol + p)),
                  pl.BlockSpec((s, LANES), lambda p, i: (0, vcol + p)),
                  pl.BlockSpec((2, CHUNK, BAND), lambda p, i: (p, 0, 0))],
        out_specs=pl.BlockSpec((tq, LANES), lambda p, i: (i, p)),
        scratch_shapes=[pltpu.VMEM((s + PAD, LANES), BF16), pltpu.VMEM((s + PAD, LANES), BF16)],
        compiler_params=_params(),
    )(proj, proj, proj, band_bias)


def _attn_a_bwd(proj, band_bias, doa, name, rider=None):
    s = proj.shape[0]
    cq = 4
    tq = cq * CHUNK
    nq = s // tq
    npair = N_HEADS // 2
    kcol, vcol = W_GRP // LANES, 2 * W_GRP // LANES
    scale = HEAD_DIM ** -0.5

    def body(q_ref, k_ref, v_ref, b_ref, do_ref, dq_ref, dk_ref, dv_ref, db_ref, kpad, vpad, dkpad, dvpad):
        i = pl.program_id(1)
        masks = _head_masks()

        @pl.when(i == 0)
        def _():
            _fill_padded(kpad, k_ref)
            _fill_padded(vpad, v_ref)
            dkpad[...] = jnp.zeros_like(dkpad)
            dvpad[...] = jnp.zeros_like(dvpad)
            db_ref[...] = jnp.zeros_like(db_ref)

        chunks, starts, kbs, vbs = _band_windows(i, cq, kpad, vpad)
        q2 = q_ref[...] * scale
        do2 = do_ref[...].astype(BF16)
        units = [(cc, h) for cc in range(cq) for h in range(2)]
        qhs = [_only(masks[h], q2[cc * CHUNK:(cc + 1) * CHUNK]) for cc, h in units]
        dohs = [_only(masks[h], do2[cc * CHUNK:(cc + 1) * CHUNK]) for cc, h in units]
        ss = [_nt(qh, kbs[cc]) for qh, (cc, h) in zip(qhs, units)]
        dps = [_nt(doh, vbs[cc]) for doh, (cc, h) in zip(dohs, units)]
        ps = [_chunk_probs(s_, b_ref[h], chunks[cc]) for s_, (cc, h) in zip(ss, units)]
        dss = [p * (dp - jnp.sum(p * dp, axis=1, keepdims=True)) for p, dp in zip(ps, dps)]
        for h in range(2):
            db_ref[h] += functools.reduce(jnp.add, [dss[2 * cc + h] for cc in range(cq)])
        for cc in range(cq):
            u0, u1 = 2 * cc, 2 * cc + 1
            dsb = [dss[u0].astype(BF16), dss[u1].astype(BF16)]
            dq = _nn(dsb[0], _only(masks[0], kbs[cc])) + _nn(dsb[1], _only(masks[1], kbs[cc]))
            dq_ref[cc * CHUNK:(cc + 1) * CHUNK, :] = dq * scale
            dkpad[pl.ds(starts[cc], BAND), :] += _tn(jnp.concatenate(dsb, axis=0), jnp.concatenate([qhs[u0], qhs[u1]], axis=0))
            dvpad[pl.ds(starts[cc], BAND), :] += _tn(jnp.concatenate([ps[u0].astype(BF16), ps[u1].astype(BF16)], axis=0),
                                                     jnp.concatenate([dohs[u0], dohs[u1]], axis=0))

        @pl.when(i == nq - 1)
        def _():
            dk_ref[...] = dkpad[PAD:, :]
            dv_ref[...] = dvpad[PAD:, :]

    blk = pl.BlockSpec((tq, LANES), lambda p, i: (i, p))
    whole = pl.BlockSpec((s, LANES), lambda p, i: (0, p))
    bias_spec = pl.BlockSpec((2, CHUNK, BAND), lambda p, i: (p, 0, 0))
    return _call_with_rider(
        body, rider, name=name, grid=(npair, nq),
        out_shape=(jax.ShapeDtypeStruct((s, W_GRP), F32),) * 3 + (jax.ShapeDtypeStruct((N_HEADS, CHUNK, BAND), F32),),
        in_specs=[blk, pl.BlockSpec((s, LANES), lambda p, i: (0, kcol + p)),
                  pl.BlockSpec((s, LANES), lambda p, i: (0, vcol + p)), bias_spec, blk],
        out_specs=(blk, whole, whole, bias_spec),
        scratch_shapes=[pltpu.VMEM((s + PAD, LANES), BF16), pltpu.VMEM((s + PAD, LANES), BF16),
                        pltpu.VMEM((s + PAD, LANES), F32), pltpu.VMEM((s + PAD, LANES), F32)],
        args=(proj, proj, proj, band_bias, doa))


def _split3(v):
    hi = v.astype(BF16)
    r1 = v - hi.astype(F32)
    mid = r1.astype(BF16)
    lo = (r1 - mid.astype(F32)).astype(BF16)
    return hi, mid, lo


def _rel_bias_grad(dband_t, name):
    width = 3 * LANES

    def body(t_ref, o_ref):
        pos = lax.broadcasted_iota(jnp.int32, (BAND, width), 0)
        col = lax.broadcasted_iota(jnp.int32, (BAND, width), 1)
        acc = jnp.zeros((N_HEADS, width), F32)
        for q in range(CHUNK):
            idx = jnp.minimum(PAD + q - pos, REL_CLIP) + REL_CLIP
            onehot = (col == idx).astype(BF16)
            for part in _split3(t_ref[q]):
                acc = acc + _nn(part, onehot)
        o_ref[...] = acc

    return pl.pallas_call(
        body, name=name, out_shape=jax.ShapeDtypeStruct((N_HEADS, width), F32),
        in_specs=[VMEM_FULL], out_specs=VMEM_FULL, compiler_params=_params(),
    )(dband_t)


def _split2_wide(v):
    hi = v.astype(BF16)
    return jnp.concatenate([hi, (v - hi.astype(F32)).astype(BF16)], axis=1)


def _sb_logs(z, lower):
    e = jnp.exp(-jnp.abs(z))
    lb = jnp.minimum(z, 0.0) - jnp.log(1.0 + e)
    lk = lb - z
    if lower is not None:
        lk = jnp.where(lower, lk, 0.0)
    return z, e, lb, lk


def _tri_masks(tq):
    row = lax.broadcasted_iota(jnp.int32, (tq, tq), 0)
    col = lax.broadcasted_iota(jnp.int32, (tq, tq), 1)
    return row, col


def _stack2(m):
    return jnp.concatenate([m, m], axis=0).astype(BF16)


def _sb_fwd(proj, name, rider=None):
    s = proj.shape[0]
    tq = _pick(s, 256)
    nq = s // tq
    npair = N_HEADS // 2
    qcol, kcol, vcol = 3 * W_GRP // LANES, 4 * W_GRP // LANES, 5 * W_GRP // LANES

    assert nq % 2 == 0

    def body(q_ref, k_ref, v_ref, o_ref, l_ref):
        i = pl.program_id(1)
        masks = _head_masks()
        q2 = q_ref[...] * (HEAD_DIM ** -0.5)
        qs = [[_only(m, q2[c * tq:(c + 1) * tq]) for m in masks] for c in range(2)]
        row, col = _tri_masks(tq)
        lower = row > col
        after2 = _stack2(lower)

        def tile(kblock, chains, carry):
            accs, tails = [list(t) for t in carry[0]], [list(t) for t in carry[1]]
            ks = pl.multiple_of(kblock * tq, tq)
            kb = k_ref[pl.ds(ks, tq), :]
            vb = v_ref[pl.ds(ks, tq), :]
            units = [(c, h, diag) for c, diag in chains for h in range(2)]
            zs = [_nt(qs[c][h], kb) for c, h, _ in units]
            vh = [_only(masks[h], vb) for h in range(2)]
            lbs, lks, locs = [], [], []
            for z, (c, h, diag) in zip(zs, units):
                lb, lk = _sb_logs(z, lower if diag else None)[2:]
                lbs.append(lb)
                lks.append(lk)
                locs.append(_nn(_split2_wide(lk), after2))
            for lb, lk, loc, (c, h, diag) in zip(lbs, lks, locs, units):
                a = jnp.exp(lb + (loc + tails[c][h]))
                if diag:
                    a = jnp.where(lower, a, 0.0)
                accs[c][0] = accs[c][0] + _nn(a.astype(BF16), vh[h])
                tails[c][h] = tails[c][h] + (loc[:, 0:1] + lk[:, 0:1])
            return tuple(tuple(t) for t in accs), tuple(tuple(t) for t in tails)

        zero = jnp.zeros((tq, 1), F32)
        acc0 = jnp.zeros((tq, LANES), F32)
        carry = (((acc0,), (acc0,)), ((zero, zero), (zero, zero)))
        carry = tile(2 * i + 1, [(1, True)], carry)
        carry = tile(2 * i, [(0, True), (1, False)], carry)
        accs, tails = lax.fori_loop(1, 2 * i + 1, lambda jj, cr: tile(2 * i - jj, [(0, False), (1, False)], cr), carry)
        for c in range(2):
            o_ref[c * tq:(c + 1) * tq, :] = accs[c][0]
            l_ref[c * tq:(c + 1) * tq, 0:1] = tails[c][0]
            l_ref[c * tq:(c + 1) * tq, 1:2] = tails[c][1]

    return _call_with_rider(
        body, rider, name=name, grid=(npair, nq // 2),
        out_shape=(jax.ShapeDtypeStruct((s, W_GRP), F32), jax.ShapeDtypeStruct((npair, s, 2), F32)),
        in_specs=[pl.BlockSpec((2 * tq, LANES), lambda p, i: (i, qcol + p)),
                  pl.BlockSpec((s, LANES), lambda p, i: (0, kcol + p)),
                  pl.BlockSpec((s, LANES), lambda p, i: (0, vcol + p))],
        out_specs=(pl.BlockSpec((2 * tq, LANES), lambda p, i: (i, p)),
                   pl.BlockSpec((None, 2 * tq, 2), lambda p, i: (p, i, 0))),
        scratch_shapes=[], args=(proj, proj, proj))


def _sb_bwd(proj, ltot, dob, name, rider=None):
    s = proj.shape[0]
    tq = _pick(s, 256)
    nq = s // tq
    npair = N_HEADS // 2
    qcol, kcol, vcol = 3 * W_GRP // LANES, 4 * W_GRP // LANES, 5 * W_GRP // LANES
    scale = HEAD_DIM ** -0.5

    def body(q_ref, k_ref, v_ref, l_ref, do_ref, dq_ref, dk_ref, dv_ref):
        i = pl.program_id(1)
        masks = _head_masks()

        @pl.when(i == 0)
        def _():
            dk_ref[...] = jnp.zeros_like(dk_ref)
            dv_ref[...] = jnp.zeros_like(dv_ref)

        q2 = q_ref[...] * scale
        do2 = do_ref[...]
        part = lambda v, c: v[c * tq:(c + 1) * tq]
        qs = [[_only(m, part(q2, c)) for m in masks] for c in range(2)]
        doh = [[_only(m, part(do2, c)).astype(BF16) for m in masks] for c in range(2)]
        ltots = [[l_ref[c * tq:(c + 1) * tq, h:h + 1] for h in range(2)] for c in range(2)]
        row, col = _tri_masks(tq)
        lower = row > col
        upto2 = _stack2(row <= col)
        before2 = _stack2(row < col)

        def tile(kblock, chains, carry):
            dqs, heads, gsums = [[list(t) for t in part_] for part_ in carry]
            ks = pl.multiple_of(kblock * tq, tq)
            kb = k_ref[pl.ds(ks, tq), :]
            vb = v_ref[pl.ds(ks, tq), :]
            units = [(c, h, diag) for c, diag in chains for h in range(2)]
            zs = [_nt(qs[c][h], kb) for c, h, _ in units]
            das = [_nt(doh[c][h], vb) for c, h, _ in units]
            kh = [_only(masks[h], kb) for h in range(2)]
            sigs, lbs, locs = [], [], []
            for z_, (c, h, diag) in zip(zs, units):
                z, e, lb, lk = _sb_logs(z_, lower if diag else None)
                locs.append(_nn(_split2_wide(lk), upto2))
                r = 1.0 / (1.0 + e)
                sigs.append(jnp.where(z >= 0, r, e * r))
                lbs.append(lb)
            a_s, gs, glocs = [], [], []
            for lb, loc, da, (c, h, diag) in zip(lbs, locs, das, units):
                a = jnp.exp(lb + (ltots[c][h] - (heads[c][h] + loc)))
                if diag:
                    a = jnp.where(lower, a, 0.0)
                g = a * da
                glocs.append(_nn(_split2_wide(g), before2))
                a_s.append(a.astype(BF16))
                gs.append(g)
            dzbs = []
            for g, sig, loc, gloc, (c, h, diag) in zip(gs, sigs, locs, glocs, units):
                dz = g - sig * (g + (gsums[c][h] + gloc))
                if diag:
                    dz = jnp.where(lower, dz, 0.0)
                dzb = dz.astype(BF16)
                dzbs.append(dzb)
                dqs[c][0] = dqs[c][0] + _nn(dzb, kh[h])
                heads[c][h] = heads[c][h] + loc[:, tq - 1:tq]
                gsums[c][h] = gsums[c][h] + (gloc[:, tq - 1:tq] + g[:, tq - 1:tq])
            stack = lambda vs: vs[0] if len(vs) == 1 else jnp.concatenate(vs, axis=0)
            dk_ref[pl.ds(ks, tq), :] += _tn(stack(dzbs), stack([qs[c][h] for c, h, _ in units]))
            dv_ref[pl.ds(ks, tq), :] += _tn(stack(a_s), stack([doh[c][h] for c, h, _ in units]))
            return tuple(tuple(tuple(t) for t in part_) for part_ in (dqs, heads, gsums))

        zero = jnp.zeros((tq, 1), F32)
        acc0 = jnp.zeros((tq, LANES), F32)
        carry = (((acc0,), (acc0,)), ((zero, zero), (zero, zero)), ((zero, zero), (zero, zero)))
        carry = lax.fori_loop(0, 2 * i, lambda j, cr: tile(j, [(0, False), (1, False)], cr), carry)
        carry = tile(2 * i, [(0, True), (1, False)], carry)
        dqs, _, _ = tile(2 * i + 1, [(1, True)], carry)
        for c in range(2):
            dq_ref[c * tq:(c + 1) * tq, :] = dqs[c][0] * scale

    blk = pl.BlockSpec((2 * tq, LANES), lambda p, i: (i, p))
    whole = pl.BlockSpec((s, LANES), lambda p, i: (0, p))
    return _call_with_rider(
        body, rider, name=name, grid=(npair, nq // 2), out_shape=(jax.ShapeDtypeStruct((s, W_GRP), F32),) * 3,
        in_specs=[pl.BlockSpec((2 * tq, LANES), lambda p, i: (i, qcol + p)),
                  pl.BlockSpec((s, LANES), lambda p, i: (0, kcol + p)),
                  pl.BlockSpec((s, LANES), lambda p, i: (0, vcol + p)),
                  pl.BlockSpec((None, 2 * tq, 2), lambda p, i: (p, i, 0)), blk],
        out_specs=(blk, whole, whole), scratch_shapes=[], args=(proj, proj, proj, ltot, dob))


def _ada_fwd(c_all, w_ada, b_ada, name):
    nl, d, n = w_ada.shape
    tn = _pick(n, 512)

    def body(c_ref, w_ref, b_ref, o_ref):
        cv = c_ref[...]
        act = (cv * jax.nn.sigmoid(cv)).astype(BF16)
        o_ref[...] = _nn(act, w_ref[...].astype(BF16)) + b_ref[...]

    return pl.pallas_call(
        body, name=name, out_shape=jax.ShapeDtypeStruct((nl, N_DEV, n), F32), grid=(nl, n // tn),
        in_specs=[pl.BlockSpec((N_DEV, d), lambda l, j: (0, 0)), pl.BlockSpec((None, d, tn), lambda l, j: (l, 0, j)),
                  pl.BlockSpec((None, 1, tn), lambda l, j: (l, 0, j))],
        out_specs=pl.BlockSpec((None, N_DEV, tn), lambda l, j: (l, 0, j)), compiler_params=_params(),
    )(c_all, w_ada, b_ada)


def _ada_bwd(c_all, dmod, name):
    nl, _, n = dmod.shape
    d = c_all.shape[1]
    tn = _pick(n, 512)

    def body(c_ref, g_ref, o_ref):
        cv = c_ref[...]
        act = (cv * jax.nn.sigmoid(cv)).astype(BF16)
        o_ref[...] = _tn(act, g_ref[...].astype(BF16))

    return pl.pallas_call(
        body, name=name, out_shape=jax.ShapeDtypeStruct((nl, d, n), F32), grid=(nl, n // tn),
        in_specs=[pl.BlockSpec((N_DEV, d), lambda l, j: (0, 0)), pl.BlockSpec((None, N_DEV, tn), lambda l, j: (l, 0, j))],
        out_specs=pl.BlockSpec((None, d, tn), lambda l, j: (l, 0, j)), compiler_params=_params(),
    )(c_all, dmod)


def _adamw(g, w, m, v, name):
    r, c = g.shape
    tr = _pick(r, 512, 8)
    c1 = 1.0 - ADAM_B1 ** ADAM_STEP
    c2 = 1.0 - ADAM_B2 ** ADAM_STEP

    def body(g_ref, w_ref, m_ref, v_ref, d_ref, nm_ref, nv_ref):
        gv = g_ref[...]
        nm = ADAM_B1 * m_ref[...] + (1.0 - ADAM_B1) * gv
        nv = ADAM_B2 * v_ref[...] + (1.0 - ADAM_B2) * (gv * gv)
        d_ref[...] = -ADAM_LR * ((nm / c1) / (jnp.sqrt(nv / c2) + ADAM_EPS) + ADAM_WD * w_ref[...])
        nm_ref[...] = nm
        nv_ref[...] = nv

    spec = pl.BlockSpec((tr, c), lambda i: (i, 0))
    return pl.pallas_call(
        body, name=name, out_shape=(jax.ShapeDtypeStruct((r, c), F32),) * 3, grid=(r // tr,),
        in_specs=[spec] * 4, out_specs=(spec,) * 3, compiler_params=_params(),
    )(g, w, m, v)


def _adamw_nd(g, w, m, v, name):
    shape = w.shape
    two_d = (1, shape[0]) if len(shape) == 1 else (-1, shape[-1])
    outs = _adamw(*(t.reshape(two_d) for t in (g, w, m, v)), name=name)
    return tuple(o.reshape(shape) for o in outs)


def _allgather8(v, name):
    m, n = v.shape

    def body(v_ref, out_ref, send_sems, recv_sems, local_sem):
        x, y, c = _my_place()

        def rows(px, py, pc):
            return out_ref.at[pl.ds(pl.multiple_of((4 * px + 2 * py + pc) * m, 8), m), :]

        def peer(k):
            return _flip(x, k & 4), _flip(y, k & 2), _flip(c, k & 1)

        def copy(k, block):
            return pltpu.make_async_remote_copy(
                src_ref=v_ref, dst_ref=rows(*block), send_sem=send_sems.at[k - 1], recv_sem=recv_sems.at[k - 1],
                device_id=peer(k), device_id_type=MESH)

        mine = pltpu.make_async_copy(v_ref, rows(x, y, c), local_sem)
        mine.start()
        sends = [copy(k, (x, y, c)) for k in range(1, N_DEV)]
        for cp in sends:
            cp.start()
        for k in range(1, N_DEV):
            copy(k, peer(k)).wait_recv()
        for cp in sends:
            cp.wait_send()
        mine.wait()

    return pl.pallas_call(
        body, name=name, out_shape=jax.ShapeDtypeStruct((N_DEV * m, n), v.dtype),
        in_specs=[VMEM_FULL], out_specs=VMEM_FULL,
        scratch_shapes=[pltpu.SemaphoreType.DMA((N_DEV - 1,)), pltpu.SemaphoreType.DMA((N_DEV - 1,)),
                        pltpu.SemaphoreType.DMA],
        compiler_params=_params(),
    )(v)


def _chip_peers(x, y, c):
    out = []
    for k in range(1, N_CHIPS):
        px, py = _flip(x, k & 2), _flip(y, k & 1)
        out.append((2 * px + py, (px, py, c)))
    return out


def _gather_weights(shards, kinds, name):
    nw = len(shards)

    def full_shape(a, kind):
        l, r, n = a.shape
        return (l, r, N_CHIPS * n) if kind == "col" else (l, N_CHIPS * r, n)

    def body(*refs):
        ins, outs = refs[:nw], refs[nw:2 * nw]
        send_sems, recv_sems, local_sems = refs[2 * nw:]
        x, y, c = _my_place()
        chip = 2 * x + y

        def window(w, j):
            _, r, n = shards[w].shape
            if kinds[w] == "col":
                return outs[w].at[:, :, pl.ds(pl.multiple_of(j * n, LANES), n)]
            return outs[w].at[:, pl.ds(pl.multiple_of(j * r, 16), r), :]

        def copy(w, k, j, peer):
            return pltpu.make_async_remote_copy(
                src_ref=ins[w], dst_ref=window(w, j), send_sem=send_sems.at[3 * w + k], recv_sem=recv_sems.at[3 * w + k],
                device_id=peer, device_id_type=MESH)

        local = [pltpu.make_async_copy(ins[w], window(w, chip), local_sems.at[w]) for w in range(nw)]
        for cp in local:
            cp.start()
        peers = _chip_peers(x, y, c)
        sends = [copy(w, k, chip, peer) for w in range(nw) for k, (_, peer) in enumerate(peers)]
        for cp in sends:
            cp.start()
        for w in range(nw):
            for k, (pchip, peer) in enumerate(peers):
                copy(w, k, pchip, peer).wait_recv()
        for cp in sends:
            cp.wait_send()
        for cp in local:
            cp.wait()

    return pl.pallas_call(
        body, name=name,
        out_shape=tuple(jax.ShapeDtypeStruct(full_shape(a, kd), a.dtype) for a, kd in zip(shards, kinds)),
        in_specs=[ANY] * nw, out_specs=(ANY,) * nw,
        scratch_shapes=[pltpu.SemaphoreType.DMA((3 * nw,)), pltpu.SemaphoreType.DMA((3 * nw,)),
                        pltpu.SemaphoreType.DMA((nw,))],
        compiler_params=_params(),
    )(*shards)


def _rs_to_sibling(grads, name):
    nw = len(grads)

    def body(*refs):
        ins, outs = refs[:nw], refs[nw:2 * nw]
        send_sems, recv_sems = refs[2 * nw:]
        x, y, c = _my_place()
        sibling = (x, y, 1 - c)
        copies = [pltpu.make_async_remote_copy(
            src_ref=ins[w].at[j, 1 - c], dst_ref=outs[w].at[j], send_sem=send_sems.at[N_CHIPS * w + j],
            recv_sem=recv_sems.at[N_CHIPS * w + j], device_id=sibling, device_id_type=MESH)
            for w in range(nw) for j in range(N_CHIPS)]
        for cp in copies:
            cp.start()
        for cp in copies:
            cp.wait_recv()
        for cp in copies:
            cp.wait_send()

    return pl.pallas_call(
        body, name=name,
        out_shape=tuple(jax.ShapeDtypeStruct((N_CHIPS,) + g.shape[2:], g.dtype) for g in grads),
        in_specs=[ANY] * nw, out_specs=(ANY,) * nw,
        scratch_shapes=[pltpu.SemaphoreType.DMA((N_CHIPS * nw,)), pltpu.SemaphoreType.DMA((N_CHIPS * nw,))],
        compiler_params=_params(),
    )(*grads)


def _rs_to_chips(parts, name):
    nw = len(parts)

    def body(*refs):
        ins, outs = refs[:nw], refs[nw:2 * nw]
        send_sems, recv_sems, local_sems = refs[2 * nw:]
        x, y, c = _my_place()
        chip = 2 * x + y
        peers = _chip_peers(x, y, c)

        def copy(w, k, src_slab, dst_slab, peer):
            return pltpu.make_async_remote_copy(
                src_ref=ins[w].at[src_slab], dst_ref=outs[w].at[dst_slab], send_sem=send_sems.at[3 * w + k],
                recv_sem=recv_sems.at[3 * w + k], device_id=peer, device_id_type=MESH)

        local = [pltpu.make_async_copy(ins[w].at[chip], outs[w].at[chip], local_sems.at[w]) for w in range(nw)]
        for cp in local:
            cp.start()
        sends = [copy(w, k, pchip, chip, peer) for w in range(nw) for k, (pchip, peer) in enumerate(peers)]
        for cp in sends:
            cp.start()
        for w in range(nw):
            for k, (pchip, peer) in enumerate(peers):
                copy(w, k, chip, pchip, peer).wait_recv()
        for cp in sends:
            cp.wait_send()
        for cp in local:
            cp.wait()

    return pl.pallas_call(
        body, name=name, out_shape=tuple(jax.ShapeDtypeStruct(p.shape, p.dtype) for p in parts),
        in_specs=[ANY] * nw, out_specs=(ANY,) * nw,
        scratch_shapes=[pltpu.SemaphoreType.DMA((3 * nw,)), pltpu.SemaphoreType.DMA((3 * nw,)),
                        pltpu.SemaphoreType.DMA((nw,))],
        compiler_params=_params(),
    )(*parts)


def _rs_share_halves(halves, name):
    nw = len(halves)
    nl = len(halves[0])
    flat = [h for hs in halves for h in hs]

    def body(*refs):
        ins, outs = refs[:nw * nl], refs[nw * nl:nw * nl + nw]
        send_sems, recv_sems, local_sems = refs[nw * nl + nw:]
        x, y, c = _my_place()
        sibling = (x, y, 1 - c)
        local, sends, recvs = [], [], []
        for w in range(nw):
            for l in range(nl):
                n = nl * w + l
                local.append(pltpu.make_async_copy(ins[n], outs[w].at[l, c], local_sems.at[n]))
                sends.append(pltpu.make_async_remote_copy(
                    src_ref=ins[n], dst_ref=outs[w].at[l, c], send_sem=send_sems.at[n], recv_sem=recv_sems.at[n],
                    device_id=sibling, device_id_type=MESH))
                recvs.append(pltpu.make_async_remote_copy(
                    src_ref=ins[n], dst_ref=outs[w].at[l, 1 - c], send_sem=send_sems.at[n], recv_sem=recv_sems.at[n],
                    device_id=sibling, device_id_type=MESH))
        for cp in local + sends:
            cp.start()
        for cp in recvs:
            cp.wait_recv()
        for cp in sends:
            cp.wait_send()
        for cp in local:
            cp.wait()

    return pl.pallas_call(
        body, name=name,
        out_shape=tuple(jax.ShapeDtypeStruct((nl, 2) + hs[0].shape, hs[0].dtype) for hs in halves),
        in_specs=[ANY] * (nw * nl), out_specs=(ANY,) * nw,
        scratch_shapes=[pltpu.SemaphoreType.DMA((nw * nl,)), pltpu.SemaphoreType.DMA((nw * nl,)),
                        pltpu.SemaphoreType.DMA((nw * nl,))],
        compiler_params=_params(),
    )(*flat)


def _add_own_half(grad, got, c_idx, name):
    _, _, r, n = grad.shape
    tr = _pick(r, 256, 8)

    def body(c_ref, g_ref, t_ref, o_ref):
        o_ref[...] = g_ref[...] + t_ref[...]

    return pl.pallas_call(
        body, name=name, out_shape=jax.ShapeDtypeStruct((N_CHIPS, r, n), F32),
        grid_spec=pltpu.PrefetchScalarGridSpec(
            num_scalar_prefetch=1, grid=(N_CHIPS, r // tr),
            in_specs=[pl.BlockSpec((None, None, tr, n), lambda j, i, c_ref: (j, c_ref[0], i, 0)),
                      pl.BlockSpec((None, tr, n), lambda j, i, c_ref: (j, i, 0))],
            out_specs=pl.BlockSpec((None, tr, n), lambda j, i, c_ref: (j, i, 0))),
        compiler_params=_params(),
    )(c_idx, grad, got)


def _sum_slabs(slabs, name):
    ns, r, n = slabs.shape
    tr = _pick(r, 256, 8)

    def body(s_ref, o_ref):
        acc = s_ref[0]
        for j in range(1, ns):
            acc = acc + s_ref[j]
        o_ref[...] = acc

    return pl.pallas_call(
        body, name=name, out_shape=jax.ShapeDtypeStruct((r, n), F32), grid=(r // tr,),
        in_specs=[pl.BlockSpec((ns, tr, n), lambda i: (0, i, 0))], out_specs=pl.BlockSpec((tr, n), lambda i: (i, 0)),
        compiler_params=_params(),
    )(slabs)


def _band_bias(rel_bias):
    h = rel_bias.shape[0]
    n_far = PAD - REL_CLIP + CHUNK
    far = jnp.broadcast_to(rel_bias[:, N_REL - 1:N_REL], (h, n_far))
    near = rel_bias[:, REL_CLIP - CHUNK + 1:N_REL - 1][:, ::-1]
    line = jnp.concatenate([far, near], axis=1)
    return jnp.stack([line[:, CHUNK - 1 - q:CHUNK - 1 - q + BAND] for q in range(CHUNK)], axis=1)


def _pack_rows(pieces):
    flat = jnp.concatenate([p.reshape(-1) for p in pieces])
    rows = -(-flat.shape[0] // (8 * LANES)) * 8
    return jnp.pad(flat, (0, rows * LANES - flat.shape[0])).reshape(rows, LANES)


def _unpack_rows(packed, shapes):
    flat = packed.reshape(-1)
    out, at = [], 0
    for shp in shapes:
        size = 1
        for n in shp:
            size *= n
        out.append(flat[at:at + size].reshape(shp))
        at += size
    return out


def _layer_fwd(x, mod, w, band, tag):
    s, d = x.shape
    row = lambda i: mod[i:i + 1]
    h1 = _norm_mod(x, row(1), row(0), f"norm_mix{tag}")
    proj = _matmul(h1, w["w_in"], form="nn", out_dtype=BF16, tm=_pick(s, 512), tn=_pick(w["w_in"].shape[1], 768),
                   tk=d, name=f"proj{tag}")
    oa = _attn_a_fwd(proj, band, f"attn_a{tag}")
    ob, ltot = _sb_fwd(proj, f"attn_b{tag}")
    cat = _out_norm(oa, ob, w["g_a"], w["g_b"], f"out_norm{tag}")
    mixed = _matmul(cat, w["w_out"], form="nn", out_dtype=F32, tm=_pick(s, 512), tn=_pick(d, 1024),
                    tk=cat.shape[1], name=f"mix_out{tag}")
    x1 = _residual(x, row(2), mixed, f"res_mix{tag}")
    h2 = _norm_mod(x1, row(4), row(3), f"norm_ffn{tag}")
    f2 = w["w_up"].shape[1]
    u = _matmul(h2, w["w_up"], form="nn", out_dtype=F32, tm=_pick(s, 512), tn=_pick(f2, 1408), tk=d, name=f"up{tag}")
    a = _conv_glu(u, w["conv_w"], w["conv_b"], f"conv_glu{tag}")
    f = _matmul(a, w["w_down"], form="nn", out_dtype=F32, tm=_pick(s, 512), tn=_pick(d, 1024),
                tk=_pick(f2 // 2, 1408), name=f"down{tag}")
    x2 = _residual(x1, row(5), f, f"res_ffn{tag}")
    saved = dict(x=x, h1=h1, proj=proj, oa=oa, ob=ob, ltot=ltot, cat=cat, mixed=mixed, x1=x1, h2=h2, u=u, a=a, f=f)
    return x2, saved


def _layer_bwd(dx2, sv, mod, w, band, tag):
    s, d = dx2.shape
    row = lambda i: mod[i:i + 1]
    f2 = w["w_up"].shape[1]
    ff = f2 // 2
    n_in = w["w_in"].shape[1]
    df, dgate_ffn = _gate_bwd(dx2, sv["f"], row(5), f"gate_ffn_bwd{tag}")
    da = _matmul(df, w["w_down"], form="nt", out_dtype=F32, tm=_pick(s, 512), tn=_pick(ff, 1408), tk=d, name=f"down_dx{tag}")
    g_down = _matmul(sv["a"], df, form="tn", out_dtype=F32, tm=_pick(ff, 1408), tn=_pick(d, 512), tk=_pick(s, 512),
                     name=f"down_dw{tag}")
    du2, dcw, dcb = _conv_glu_bwd(sv["u"], da, w["conv_w"], w["conv_b"], f"conv_glu_bwd{tag}")
    du = jnp.concatenate([du2[0], du2[1]], axis=1)
    dh2 = _matmul(du, w["w_up"], form="nt", out_dtype=F32, tm=_pick(s, 512), tn=_pick(d, 1024), tk=_pick(f2, 1408),
                  name=f"up_dx{tag}")
    g_up = _matmul(sv["h2"], du, form="tn", out_dtype=F32, tm=_pick(d, 512), tn=_pick(f2 // N_CHIPS, 1408),
                   tk=_pick(s, 512), name=f"up_dw{tag}", shard_cols=f2 // N_CHIPS)
    dx1, dscale_ffn, dshift_ffn = _norm_mod_bwd(sv["x1"], dh2, dx2, row(4), f"norm_ffn_bwd{tag}")
    dmixed, dgate_mix = _gate_bwd(dx1, sv["mixed"], row(2), f"gate_mix_bwd{tag}")
    dcat = _matmul(dmixed, w["w_out"], form="nt", out_dtype=F32, tm=_pick(s, 512), tn=_pick(2 * W_GRP, 1024), tk=d,
                   name=f"mix_out_dx{tag}")
    g_out = _matmul(sv["cat"], dmixed, form="tn", out_dtype=F32, tm=_pick(2 * W_GRP, 512), tn=_pick(d, 1024),
                    tk=_pick(s, 512), name=f"mix_out_dw{tag}")
    doa, dob, dg_a, dg_b = _out_norm_bwd(sv["oa"], sv["ob"], dcat, w["g_a"], w["g_b"], f"out_norm_bwd{tag}")
    dqa, dka, dva, dband = _attn_a_bwd(sv["proj"], band, doa, f"attn_a_bwd{tag}")
    dqb, dkb, dvb = _sb_bwd(sv["proj"], sv["ltot"], dob, f"attn_b_bwd{tag}")
    drel = _rel_bias_grad(jnp.transpose(dband, (1, 0, 2)), f"rel_bias_bwd{tag}")[:, :N_REL]
    dproj = jnp.concatenate([dqa, dka, dva, dqb, dkb, dvb], axis=1).astype(BF16)
    dh1 = _matmul(dproj, w["w_in"], form="nt", out_dtype=F32, tm=_pick(s, 512), tn=_pick(d, 1024), tk=_pick(n_in, 1024),
                  name=f"proj_dx{tag}")
    g_in = _matmul(sv["h1"], dproj, form="tn", out_dtype=F32, tm=_pick(d, 512), tn=_pick(n_in // N_CHIPS, 768),
                   tk=_pick(s, 512), name=f"proj_dw{tag}", shard_cols=n_in // N_CHIPS)
    dx, dscale_mix, dshift_mix = _norm_mod_bwd(sv["x"], dh1, dx1, row(1), f"norm_mix_bwd{tag}")
    dmod = jnp.concatenate([dshift_mix, dscale_mix, dgate_mix, dshift_ffn, dscale_ffn, dgate_ffn], axis=1)
    big = dict(w_in=g_in, w_out=g_out, w_up=g_up, w_down=g_down)
    dconv_w = jnp.concatenate([dcw[0], dcw[1]], axis=1)
    dconv_b = jnp.concatenate([dcb[0], dcb[1]], axis=1)
    small = dict(dmod=dmod, rel_bias=drel, g_a=dg_a, g_b=dg_b, conv_w=dconv_w, conv_b=dconv_b)
    return dx, big, small


def _kernel_unoverlapped(x, c, w_ada, b_ada, w_in, rel_bias, g_a, g_b, w_out, w_up, conv_w, conv_b, w_down, final_g, loss_target, m_w_ada, m_b_ada, m_w_in, m_rel_bias, m_g_a, m_g_b, m_w_out, m_w_up, m_conv_w, m_conv_b, m_w_down, m_final_g, v_w_ada, v_b_ada, v_w_in, v_rel_bias, v_g_a, v_g_b, v_w_out, v_w_up, v_conv_w, v_conv_b, v_w_down, v_final_g):
    xi, yi, ci = _my_place()
    chip = 2 * xi + yi
    dev = 4 * xi + 2 * yi + ci
    nl, d, n_ada = w_ada.shape
    s = x.shape[1]
    f2 = N_CHIPS * w_up.shape[2]
    nc = conv_w.shape[2]

    c_pad = jnp.pad(c, ((0, 7), (0, 0)))
    c_all = _allgather8(c_pad, "gather_c")[0::8]
    b_mine = lax.dynamic_slice_in_dim(b_ada, chip * n_ada, n_ada, axis=1)[:, None, :]
    mod_shard = _ada_fwd(c_all, w_ada, b_mine, "ada")
    pack2 = _pack_rows([mod_shard, conv_w])
    got2 = _allgather8(pack2, "gather_mod").reshape(N_DEV, -1)
    mods, convs = [], []
    for j in range(N_CHIPS):
        ms, cw = _unpack_rows(got2[2 * j], [mod_shard.shape, conv_w.shape])
        mods.append(lax.dynamic_index_in_dim(ms, dev, axis=1, keepdims=False))
        convs.append(cw)
    mod = jnp.concatenate(mods, axis=1).reshape(nl, 6, d)
    conv_w_full = jnp.concatenate(convs, axis=2)

    names = ("w_in", "w_out", "w_up", "w_down")
    kinds = ("col", "row", "col", "row")
    shards = dict(w_in=w_in, w_out=w_out, w_up=w_up, w_down=w_down)
    full = _gather_weights([shards[n].astype(BF16) for n in names], kinds, "gather_weights")
    full = dict(zip(names, full))

    xs = x[0]
    layers, saved, bands = [], [], []
    for l in range(nl):
        w = {n: full[n][l] for n in names}
        w.update(g_a=g_a[l:l + 1], g_b=g_b[l:l + 1], conv_w=conv_w_full[l], conv_b=conv_b[l:l + 1])
        band = _band_bias(rel_bias[l])
        xs, sv = _layer_fwd(xs, mod[l], w, band, f"_l{l}")
        layers.append(w)
        bands.append(band)
        saved.append(sv)
    loss_part, dx, dfinal_g = _loss_head(xs, final_g[None, :], loss_target[0], "loss_head")
    loss = lax.psum(loss_part[0, 0], ("x", "y", "c"))

    big, small = [None] * nl, [None] * nl
    for l in reversed(range(nl)):
        dx, big[l], small[l] = _layer_bwd(dx, saved[l], mod[l], layers[l], bands[l], f"_l{l}")

    small_names = ("dmod", "rel_bias", "g_a", "g_b", "conv_w", "conv_b")
    pieces = [small[l][n] for l in range(nl) for n in small_names] + [dfinal_g]
    shapes = [p.shape for p in pieces]
    pack3 = _pack_rows(pieces)
    got3 = _allgather8(pack3, "gather_small").reshape(N_DEV, pack3.shape[0], LANES)
    summed = _unpack_rows(_sum_slabs(got3, "sum_small"), shapes)
    tot = [dict(zip(small_names, summed[len(small_names) * l:len(small_names) * (l + 1)])) for l in range(nl)]
    g_final_g = summed[-1].reshape(-1)
    g_b_ada = jnp.stack([tot[l]["dmod"].reshape(-1) for l in range(nl)])
    g_rel = jnp.stack([tot[l]["rel_bias"] for l in range(nl)])
    g_ga = jnp.stack([tot[l]["g_a"].reshape(-1) for l in range(nl)])
    g_gb = jnp.stack([tot[l]["g_b"].reshape(-1) for l in range(nl)])
    g_conv_b = jnp.stack([tot[l]["conv_b"].reshape(-1) for l in range(nl)])
    g_conv_w = jnp.stack([lax.dynamic_slice_in_dim(tot[l]["conv_w"], chip * nc, nc, axis=1) for l in range(nl)])
    per_dev = [_unpack_rows(got3[j], shapes) for j in range(N_DEV)]
    dmod_all = jnp.stack([jnp.stack([per_dev[j][len(small_names) * l].reshape(-1) for j in range(N_DEV)])
                          for l in range(nl)])
    g_w_ada = _ada_bwd(c_all, lax.dynamic_slice_in_dim(dmod_all, chip * n_ada, n_ada, axis=2), "ada_bwd")

    order = [(n, l) for n in names for l in range(nl)]
    flat_g = [big[l][n].reshape(N_CHIPS, 2, -1, 1024) for n, l in order]
    from_sib = _rs_to_sibling(flat_g, "rs_sibling")
    c_idx = jnp.reshape(ci, (1,)).astype(jnp.int32)
    chip_part = [_add_own_half(g, t, c_idx, f"rs_add_{n}_l{l}") for g, t, (n, l) in zip(flat_g, from_sib, order)]
    from_chips = _rs_to_chips(chip_part, "rs_chips")
    my_half = [_sum_slabs(t, f"rs_sum_{n}_l{l}") for t, (n, l) in zip(from_chips, order)]
    shard_g = _rs_share_halves([[my_half[nl * i + l] for l in range(nl)] for i in range(len(names))], "rs_halves")
    g_big = {n: shard_g[i].reshape(shards[n].shape) for i, n in enumerate(names)}

    grads = dict(w_ada=g_w_ada, b_ada=g_b_ada, rel_bias=g_rel, g_a=g_ga, g_b=g_gb, conv_w=g_conv_w, conv_b=g_conv_b,
                 final_g=g_final_g)
    weights = dict(w_ada=w_ada, b_ada=b_ada, w_in=w_in, rel_bias=rel_bias, g_a=g_a, g_b=g_b, w_out=w_out, w_up=w_up,
                   conv_w=conv_w, conv_b=conv_b, w_down=w_down, final_g=final_g)
    m_in = dict(w_ada=m_w_ada, b_ada=m_b_ada, w_in=m_w_in, rel_bias=m_rel_bias, g_a=m_g_a, g_b=m_g_b, w_out=m_w_out,
                w_up=m_w_up, conv_w=m_conv_w, conv_b=m_conv_b, w_down=m_w_down, final_g=m_final_g)
    v_in = dict(w_ada=v_w_ada, b_ada=v_b_ada, w_in=v_w_in, rel_bias=v_rel_bias, g_a=v_g_a, g_b=v_g_b, w_out=v_w_out,
                w_up=v_w_up, conv_w=v_conv_w, conv_b=v_conv_b, w_down=v_w_down, final_g=v_final_g)
    order_w = ("w_ada", "b_ada", "w_in", "rel_bias", "g_a", "g_b", "w_out", "w_up", "conv_w", "conv_b", "w_down", "final_g")
    upd = {n: _adamw_nd(grads[n], weights[n], m_in[n], v_in[n], f"adamw_{n}") for n in grads}
    for n, mn, th in zip(BIG, mine, theirs):
        grads[n], *upd[n] = _adamw_halves(mn, th, c_idx, weights[n], m_in[n], v_in[n], f"adamw_{n}")
    return (loss, dx[None], *[grads[n] for n in order_w], *[upd[n][0] for n in order_w],
            *[upd[n][1] for n in order_w], *[upd[n][2] for n in order_w])


class _Rider:
    def __init__(self, ins, out_shapes, n_remote, n_local, parts):
        self.ins = list(ins)
        self.out_shapes = list(out_shapes)
        self.scratch = [pltpu.SemaphoreType.DMA((n_remote,)), pltpu.SemaphoreType.DMA((n_remote,)),
                        pltpu.SemaphoreType.DMA((max(n_local, 1),))]
        self.parts = parts

    def start(self, in_refs, out_refs, sems):
        local, sends, _ = self.parts(in_refs, out_refs, sems)
        for cp in local() + sends():
            cp.start()

    def wait(self, in_refs, out_refs, sems):
        local, sends, recvs = self.parts(in_refs, out_refs, sems)
        for cp in recvs():
            cp.wait_recv()
        for cp in sends():
            cp.wait_send()
        for cp in local():
            cp.wait()


def _call_with_rider(body, rider, *, name, grid, out_shape, in_specs, out_specs, scratch_shapes, args):
    if rider is None:
        return pl.pallas_call(body, name=name, grid=grid, out_shape=tuple(out_shape), in_specs=list(in_specs),
                              out_specs=tuple(out_specs), scratch_shapes=list(scratch_shapes),
                              compiler_params=_params())(*args)
    n_in, n_out, n_scr = len(in_specs), len(out_specs), len(scratch_shapes)
    r_in, r_out = len(rider.ins), len(rider.out_shapes)

    def both(*refs):
        at = 0
        groups = []
        for size in (n_in, r_in, n_out, r_out, n_scr, len(rider.scratch)):
            groups.append(refs[at:at + size])
            at += size
        own_in, ride_in, own_out, ride_out, own_scr, sems = groups
        steps = [pl.program_id(a) for a in range(len(grid))]
        first = functools.reduce(jnp.logical_and, [st == 0 for st in steps])
        last = functools.reduce(jnp.logical_and, [st == g - 1 for st, g in zip(steps, grid)])

        @pl.when(first)
        def _():
            rider.start(ride_in, ride_out, sems)

        body(*own_in, *own_out, *own_scr)

        @pl.when(last)
        def _():
            rider.wait(ride_in, ride_out, sems)

    outs = pl.pallas_call(
        both, name=name, grid=grid, out_shape=tuple(out_shape) + tuple(rider.out_shapes),
        in_specs=list(in_specs) + [ANY] * r_in, out_specs=tuple(out_specs) + (ANY,) * r_out,
        scratch_shapes=list(scratch_shapes) + rider.scratch, compiler_params=_params(),
    )(*args, *rider.ins)
    return tuple(outs[:n_out]) + (list(outs[n_out:]),)


def _run_rider(rider, name):
    r_in, r_out = len(rider.ins), len(rider.out_shapes)

    def body(*refs):
        ins, outs, sems = refs[:r_in], refs[r_in:r_in + r_out], refs[r_in + r_out:]
        rider.start(ins, outs, sems)
        rider.wait(ins, outs, sems)

    return list(pl.pallas_call(
        body, name=name, out_shape=tuple(rider.out_shapes), in_specs=[ANY] * r_in, out_specs=(ANY,) * r_out,
        scratch_shapes=rider.scratch, compiler_params=_params(),
    )(*rider.ins))


def _remote(src, dst, sems, n, peer):
    return pltpu.make_async_remote_copy(src_ref=src, dst_ref=dst, send_sem=sems[0].at[n], recv_sem=sems[1].at[n],
                                        device_id=peer, device_id_type=MESH)


def _gather_rider(shards, kinds):
    nw = len(shards)
    out_shapes = [jax.ShapeDtypeStruct((a.shape[0], N_CHIPS * a.shape[1]) if kd == "col" else
                                       (N_CHIPS * a.shape[0], a.shape[1]), a.dtype) for a, kd in zip(shards, kinds)]

    def parts(ins, outs, sems):
        x, y, c = _my_place()
        chip = 2 * x + y
        peers = _chip_peers(x, y, c)

        def window(w, j):
            r, n = shards[w].shape
            if kinds[w] == "col":
                return outs[w].at[:, pl.ds(pl.multiple_of(j * n, LANES), n)]
            return outs[w].at[pl.ds(pl.multiple_of(j * r, 16), r), :]

        local = lambda: [pltpu.make_async_copy(ins[w], window(w, chip), sems[2].at[w]) for w in range(nw)]
        sends = lambda: [_remote(ins[w], window(w, chip), sems, 3 * w + k, peer)
                         for w in range(nw) for k, (_, peer) in enumerate(peers)]
        recvs = lambda: [_remote(ins[w], window(w, pchip), sems, 3 * w + k, peer)
                         for w in range(nw) for k, (pchip, peer) in enumerate(peers)]
        return local, sends, recvs

    return _Rider(shards, out_shapes, 3 * nw, nw, parts)


def _half(ref3, h, rows):
    return ref3.at[:, pl.ds(pl.multiple_of(h * rows, 8), rows), :]


def _sibling_rider(grads):
    nw = len(grads)
    out_shapes = [jax.ShapeDtypeStruct((g.shape[0], g.shape[1] // 2, g.shape[2]), g.dtype) for g in grads]

    def parts(ins, outs, sems):
        x, y, c = _my_place()
        sibling = (x, y, 1 - c)
        copies = lambda: [_remote(_half(ins[w], 1 - c, grads[w].shape[1] // 2), outs[w], sems, w, sibling)
                          for w in range(nw)]
        return (lambda: []), copies, copies

    return _Rider(grads, out_shapes, nw, 0, parts)


def _chips_rider(parts_in):
    nw = len(parts_in)
    out_shapes = [jax.ShapeDtypeStruct(p.shape, p.dtype) for p in parts_in]

    def parts(ins, outs, sems):
        x, y, c = _my_place()
        chip = 2 * x + y
        peers = _chip_peers(x, y, c)
        local = lambda: [pltpu.make_async_copy(ins[w].at[chip], outs[w].at[chip], sems[2].at[w]) for w in range(nw)]
        sends = lambda: [_remote(ins[w].at[pchip], outs[w].at[chip], sems, 3 * w + k, peer)
                         for w in range(nw) for k, (pchip, peer) in enumerate(peers)]
        recvs = lambda: [_remote(ins[w].at[chip], outs[w].at[pchip], sems, 3 * w + k, peer)
                         for w in range(nw) for k, (pchip, peer) in enumerate(peers)]
        return local, sends, recvs

    return _Rider(parts_in, out_shapes, 3 * nw, nw, parts)


def _halves_rider(halves):
    nw, nl = len(halves), len(halves[0])
    flat = [h for hs in halves for h in hs]
    out_shapes = [jax.ShapeDtypeStruct((nl, 2 * hs[0].shape[0], hs[0].shape[1]), hs[0].dtype) for hs in halves]

    def parts(ins, outs, sems):
        x, y, c = _my_place()
        sibling = (x, y, 1 - c)

        def window(w, l, h):
            rows = halves[w][0].shape[0]
            return outs[w].at[l, pl.ds(pl.multiple_of(h * rows, 8), rows), :]

        pairs = [(w, l) for w in range(nw) for l in range(nl)]
        local = lambda: [pltpu.make_async_copy(ins[nl * w + l], window(w, l, c), sems[2].at[nl * w + l]) for w, l in pairs]
        sends = lambda: [_remote(ins[nl * w + l], window(w, l, c), sems, nl * w + l, sibling) for w, l in pairs]
        recvs = lambda: [_remote(ins[nl * w + l], window(w, l, 1 - c), sems, nl * w + l, sibling) for w, l in pairs]
        return local, sends, recvs

    return _Rider(flat, out_shapes, nw * nl, nw * nl, parts)


def _add_my_half(grad, got, c_idx, name):
    _, r, n = got.shape
    tr = _pick(r, 256, 8)
    nblk = r // tr

    def body(c_ref, g_ref, t_ref, o_ref):
        o_ref[...] = g_ref[...] + t_ref[...]

    return pl.pallas_call(
        body, name=name, out_shape=jax.ShapeDtypeStruct(got.shape, F32),
        grid_spec=pltpu.PrefetchScalarGridSpec(
            num_scalar_prefetch=1, grid=(N_CHIPS, nblk),
            in_specs=[pl.BlockSpec((None, tr, n), lambda j, i, c_ref: (j, c_ref[0] * nblk + i, 0)),
                      pl.BlockSpec((None, tr, n), lambda j, i, c_ref: (j, i, 0))],
            out_specs=pl.BlockSpec((None, tr, n), lambda j, i, c_ref: (j, i, 0))),
        compiler_params=_params(),
    )(c_idx, grad, got)


def _swap_rider(mine):
    nw = len(mine)
    out_shapes = [jax.ShapeDtypeStruct(a.shape, a.dtype) for a in mine]

    def parts(ins, outs, sems):
        x, y, c = _my_place()
        copies = lambda: [_remote(ins[w], outs[w], sems, w, (x, y, 1 - c)) for w in range(nw)]
        return (lambda: []), copies, copies

    return _Rider(mine, out_shapes, nw, 0, parts)


def _sum_layers(slabs, name):
    nl = len(slabs)
    ns, r, n = slabs[0].shape
    tr = _pick(r, 256, 8)

    def body(*refs):
        o_ref = refs[nl]
        for l in range(nl):
            acc = refs[l][0]
            for j in range(1, ns):
                acc = acc + refs[l][j]
            o_ref[l] = acc

    return pl.pallas_call(
        body, name=name, out_shape=jax.ShapeDtypeStruct((nl, r, n), F32), grid=(r // tr,),
        in_specs=[pl.BlockSpec((ns, tr, n), lambda i: (0, i, 0))] * nl,
        out_specs=pl.BlockSpec((nl, tr, n), lambda i: (0, i, 0)), compiler_params=_params(),
    )(*slabs)


def _adam_math(gv, w, m, v):
    c1 = 1.0 - ADAM_B1 ** ADAM_STEP
    c2 = 1.0 - ADAM_B2 ** ADAM_STEP
    nm = ADAM_B1 * m + (1.0 - ADAM_B1) * gv
    nv = ADAM_B2 * v + (1.0 - ADAM_B2) * (gv * gv)
    return -ADAM_LR * ((nm / c1) / (jnp.sqrt(nv / c2) + ADAM_EPS) + ADAM_WD * w), nm, nv


def _adamw_halves(mine, theirs, c_idx, w, m, v, name):
    nl, r, n = mine.shape
    tr = _pick(r, 256, 8)
    nblk = r // tr

    def body(c_ref, mine_ref, theirs_ref, w_ref, m_ref, v_ref, g_ref, d_ref, nm_ref, nv_ref):
        gv = jnp.where(pl.program_id(1) == c_ref[0], mine_ref[...], theirs_ref[...])
        g_ref[...] = gv
        d_ref[...], nm_ref[...], nv_ref[...] = _adam_math(gv, w_ref[...], m_ref[...], v_ref[...])

    half = pl.BlockSpec((None, tr, n), lambda l, h, i, c_ref: (l, i, 0))
    full = pl.BlockSpec((None, tr, n), lambda l, h, i, c_ref: (l, h * nblk + i, 0))
    return pl.pallas_call(
        body, name=name, out_shape=(jax.ShapeDtypeStruct(w.shape, F32),) * 4,
        grid_spec=pltpu.PrefetchScalarGridSpec(
            num_scalar_prefetch=1, grid=(nl, 2, nblk), in_specs=[half, half, full, full, full],
            out_specs=(full,) * 4),
        compiler_params=_params(),
    )(c_idx, mine, theirs, w, m, v)


def _by_chip(g):
    return g if g.ndim == 3 else g.reshape(N_CHIPS, g.shape[0] // N_CHIPS, g.shape[1])


BIG = ("w_in", "w_out", "w_up", "w_down")
BIG_KIND = dict(w_in="col", w_out="row", w_up="col", w_down="row")


def _forward_layer(x, mod, w, band, tag, rider=None, on_arrival=None):
    s, d = x.shape
    row = lambda i: mod[i:i + 1]
    h1 = _norm_mod(x, row(1), row(0), f"norm_mix{tag}")
    proj = _matmul(h1, w["w_in"], form="nn", out_dtype=BF16, tm=_pick(s, 512), tn=_pick(w["w_in"].shape[1], 768),
                   tk=d, name=f"proj{tag}")
    oa = _attn_a_fwd(proj, band, f"attn_a{tag}")
    if rider is None:
        ob, ltot = _sb_fwd(proj, f"attn_b{tag}")
    else:
        ob, ltot, arrived = _sb_fwd(proj, f"attn_b{tag}", rider)
        on_arrival(arrived)
    cat = _out_norm(oa, ob, w["g_a"], w["g_b"], f"out_norm{tag}")
    mixed = _matmul(cat, w["w_out"], form="nn", out_dtype=F32, tm=_pick(s, 512), tn=_pick(d, 1024),
                    tk=cat.shape[1], name=f"mix_out{tag}")
    x1 = _residual(x, row(2), mixed, f"res_mix{tag}")
    h2 = _norm_mod(x1, row(4), row(3), f"norm_ffn{tag}")
    f2 = w["w_up"].shape[1]
    u = _matmul(h2, w["w_up"], form="nn", out_dtype=F32, tm=_pick(s, 512), tn=_pick(f2, 1408), tk=d, name=f"up{tag}")
    a = _conv_glu(u, w["conv_w"], w["conv_b"], f"conv_glu{tag}")
    f = _matmul(a, w["w_down"], form="nn", out_dtype=F32, tm=_pick(s, 512), tn=_pick(d, 1024),
                tk=_pick(f2 // 2, 1408), name=f"down{tag}")
    x2 = _residual(x1, row(5), f, f"res_ffn{tag}")
    saved = dict(x=x, h1=h1, proj=proj, oa=oa, ob=ob, ltot=ltot, cat=cat, mixed=mixed, x1=x1, h2=h2, u=u, a=a, f=f)
    return x2, saved


def _backward_layer(dx2, sv, mod, w, band, tag, c_idx, waiting=None):
    s, d = dx2.shape
    row = lambda i: mod[i:i + 1]
    f2 = w["w_up"].shape[1]
    ff = f2 // 2
    n_in = w["w_in"].shape[1]
    df, dgate_ffn = _gate_bwd(dx2, sv["f"], row(5), f"gate_ffn_bwd{tag}")
    da = _matmul(df, w["w_down"], form="nt", out_dtype=F32, tm=_pick(s, 512), tn=_pick(ff, 1408), tk=d, name=f"down_dx{tag}")
    g_down = _matmul(sv["a"], df, form="tn", out_dtype=F32, tm=_pick(ff, 1408), tn=_pick(d, 512), tk=_pick(s, 512),
                     name=f"down_dw{tag}")
    du2, dcw, dcb = _conv_glu_bwd(sv["u"], da, w["conv_w"], w["conv_b"], f"conv_glu_bwd{tag}")
    du = jnp.concatenate([du2[0], du2[1]], axis=1)
    dh2 = _matmul(du, w["w_up"], form="nt", out_dtype=F32, tm=_pick(s, 512), tn=_pick(d, 1024), tk=_pick(f2, 1408),
                  name=f"up_dx{tag}")
    g_up = _matmul(sv["h2"], du, form="tn", out_dtype=F32, tm=_pick(d, 512), tn=_pick(f2 // N_CHIPS, 1408),
                   tk=_pick(s, 512), name=f"up_dw{tag}", shard_cols=f2 // N_CHIPS)
    dx1, dscale_ffn, dshift_ffn = _norm_mod_bwd(sv["x1"], dh2, dx2, row(4), f"norm_ffn_bwd{tag}")
    dmixed, dgate_mix = _gate_bwd(dx1, sv["mixed"], row(2), f"gate_mix_bwd{tag}")
    dcat = _matmul(dmixed, w["w_out"], form="nt", out_dtype=F32, tm=_pick(s, 512), tn=_pick(2 * W_GRP, 1024), tk=d,
                   name=f"mix_out_dx{tag}")
    g_out = _matmul(sv["cat"], dmixed, form="tn", out_dtype=F32, tm=_pick(2 * W_GRP, 512), tn=_pick(d, 1024),
                    tk=_pick(s, 512), name=f"mix_out_dw{tag}")
    doa, dob, dg_a, dg_b = _out_norm_bwd(sv["oa"], sv["ob"], dcat, w["g_a"], w["g_b"], f"out_norm_bwd{tag}")
    big = {("w_down", tag): _by_chip(g_down), ("w_up", tag): _by_chip(g_up), ("w_out", tag): _by_chip(g_out)}
    if waiting is None:
        dqa, dka, dva, dband = _attn_a_bwd(sv["proj"], band, doa, f"attn_a_bwd{tag}")
        dqb, dkb, dvb = _sb_bwd(sv["proj"], sv["ltot"], dob, f"attn_b_bwd{tag}")
    else:
        raw = {**waiting, **big}
        keys = list(raw)
        dqa, dka, dva, dband, from_sib = _attn_a_bwd(sv["proj"], band, doa, f"attn_a_bwd{tag}",
                                                     _sibling_rider([raw[k] for k in keys]))
        chip_part = [_add_my_half(raw[k], t, c_idx, f"rs_add_{k[0]}{k[1]}") for k, t in zip(keys, from_sib)]
        dqb, dkb, dvb, from_chips = _sb_bwd(sv["proj"], sv["ltot"], dob, f"attn_b_bwd{tag}", _chips_rider(chip_part))
        big = dict(zip(keys, from_chips))
    drel = _rel_bias_grad(jnp.transpose(dband, (1, 0, 2)), f"rel_bias_bwd{tag}")[:, :N_REL]
    dproj = jnp.concatenate([dqa, dka, dva, dqb, dkb, dvb], axis=1).astype(BF16)
    dh1 = _matmul(dproj, w["w_in"], form="nt", out_dtype=F32, tm=_pick(s, 512), tn=_pick(d, 1024), tk=_pick(n_in, 1024),
                  name=f"proj_dx{tag}")
    g_in = _matmul(sv["h1"], dproj, form="tn", out_dtype=F32, tm=_pick(d, 512), tn=_pick(n_in // N_CHIPS, 768),
                   tk=_pick(s, 512), name=f"proj_dw{tag}", shard_cols=n_in // N_CHIPS)
    dx, dscale_mix, dshift_mix = _norm_mod_bwd(sv["x"], dh1, dx1, row(1), f"norm_mix_bwd{tag}")
    dmod = jnp.concatenate([dshift_mix, dscale_mix, dgate_mix, dshift_ffn, dscale_ffn, dgate_ffn], axis=1)
    dconv_w = jnp.concatenate([dcw[0], dcw[1]], axis=1)
    dconv_b = jnp.concatenate([dcb[0], dcb[1]], axis=1)
    small = dict(dmod=dmod, rel_bias=drel, g_a=dg_a, g_b=dg_b, conv_w=dconv_w, conv_b=dconv_b)
    return dx, big, g_in, small


def kernel(x, c, w_ada, b_ada, w_in, rel_bias, g_a, g_b, w_out, w_up, conv_w, conv_b, w_down, final_g, loss_target, m_w_ada, m_b_ada, m_w_in, m_rel_bias, m_g_a, m_g_b, m_w_out, m_w_up, m_conv_w, m_conv_b, m_w_down, m_final_g, v_w_ada, v_b_ada, v_w_in, v_rel_bias, v_g_a, v_g_b, v_w_out, v_w_up, v_conv_w, v_conv_b, v_w_down, v_final_g):
    xi, yi, ci = _my_place()
    chip = 2 * xi + yi
    dev = 4 * xi + 2 * yi + ci
    c_idx = jnp.reshape(ci, (1,)).astype(jnp.int32)
    nl, d, n_ada = w_ada.shape
    nc = conv_w.shape[2]
    assert nl == 2

    c_pad = jnp.pad(c, ((0, 7), (0, 0)))
    c_all = _allgather8(c_pad, "gather_c")[0::8]
    b_mine = lax.dynamic_slice_in_dim(b_ada, chip * n_ada, n_ada, axis=1)[:, None, :]
    mod_shard = _ada_fwd(c_all, w_ada, b_mine, "ada")
    pack2 = _pack_rows([mod_shard, conv_w])
    got2 = _allgather8(pack2, "gather_mod").reshape(N_DEV, -1)
    mods, convs = [], []
    for j in range(N_CHIPS):
        ms, cw = _unpack_rows(got2[2 * j], [mod_shard.shape, conv_w.shape])
        mods.append(lax.dynamic_index_in_dim(ms, dev, axis=1, keepdims=False))
        convs.append(cw)
    mod = jnp.concatenate(mods, axis=1).reshape(nl, 6, d)
    conv_w_full = jnp.concatenate(convs, axis=2)

    shards = dict(w_in=w_in, w_out=w_out, w_up=w_up, w_down=w_down)
    sh = {(n, l): shards[n][l].astype(BF16) for n in BIG for l in range(nl)}
    early = [("w_in", 0), ("w_out", 0)]
    late = [("w_up", 0), ("w_down", 0)] + [(n, 1) for n in BIG]
    layers = [dict(g_a=g_a[l:l + 1], g_b=g_b[l:l + 1], conv_w=conv_w_full[l], conv_b=conv_b[l:l + 1]) for l in range(nl)]
    got = _run_rider(_gather_rider([sh[k] for k in early], [BIG_KIND[k[0]] for k in early]), "gather_first")
    for (n, l), full in zip(early, got):
        layers[l][n] = full
    late_rider = _gather_rider([sh[k] for k in late], [BIG_KIND[k[0]] for k in late])

    def on_arrival(arrived):
        for (n, l), full in zip(late, arrived):
            layers[l][n] = full

    xs = x[0]
    saved, bands = [], []
    for l in range(nl):
        band = _band_bias(rel_bias[l])
        xs, sv = _forward_layer(xs, mod[l], layers[l], band, f"_l{l}", late_rider if l == 0 else None, on_arrival)
        bands.append(band)
        saved.append(sv)
    loss_part, dx, dfinal_g = _loss_head(xs, final_g[None, :], loss_target[0], "loss_head")
    loss = lax.psum(loss_part[0, 0], ("x", "y", "c"))

    small = [None] * nl
    dx, raw1, g_in1, small[1] = _backward_layer(dx, saved[1], mod[1], layers[1], bands[1], "_l1", c_idx)
    raw1[("w_in", "_l1")] = g_in1
    dx, parts, g_in0, small[0] = _backward_layer(dx, saved[0], mod[0], layers[0], bands[0], "_l0", c_idx, raw1)
    (from_sib,) = _run_rider(_sibling_rider([g_in0]), "rs_sibling_last")
    (parts[("w_in", "_l0")],) = _run_rider(
        _chips_rider([_add_my_half(g_in0, from_sib, c_idx, "rs_add_w_in_l0")]), "rs_chips_last")
    mine = [_sum_layers([parts[(n, f"_l{l}")] for l in range(nl)], f"rs_sum_{n}") for n in BIG]
    theirs = _run_rider(_swap_rider(mine), "rs_swap")

    small_names = ("dmod", "rel_bias", "g_a", "g_b", "conv_w", "conv_b")
    pieces = [small[l][n] for l in range(nl) for n in small_names] + [dfinal_g]
    shapes = [p.shape for p in pieces]
    pack3 = _pack_rows(pieces)
    got3 = _allgather8(pack3, "gather_small").reshape(N_DEV, pack3.shape[0], LANES)
    summed = _unpack_rows(_sum_slabs(got3, "sum_small"), shapes)
    tot = [dict(zip(small_names, summed[len(small_names) * l:len(small_names) * (l + 1)])) for l in range(nl)]
    g_final_g = summed[-1].reshape(-1)
    g_b_ada = jnp.stack([tot[l]["dmod"].reshape(-1) for l in range(nl)])
    g_rel = jnp.stack([tot[l]["rel_bias"] for l in range(nl)])
    g_ga = jnp.stack([tot[l]["g_a"].reshape(-1) for l in range(nl)])
    g_gb = jnp.stack([tot[l]["g_b"].reshape(-1) for l in range(nl)])
    g_conv_b = jnp.stack([tot[l]["conv_b"].reshape(-1) for l in range(nl)])
    g_conv_w = jnp.stack([lax.dynamic_slice_in_dim(tot[l]["conv_w"], chip * nc, nc, axis=1) for l in range(nl)])
    per_dev = [_unpack_rows(got3[j], shapes) for j in range(N_DEV)]
    dmod_all = jnp.stack([jnp.stack([per_dev[j][len(small_names) * l].reshape(-1) for j in range(N_DEV)])
                          for l in range(nl)])
    g_w_ada = _ada_bwd(c_all, lax.dynamic_slice_in_dim(dmod_all, chip * n_ada, n_ada, axis=2), "ada_bwd")

    grads = dict(w_ada=g_w_ada, b_ada=g_b_ada, rel_bias=g_rel, g_a=g_ga, g_b=g_gb, conv_w=g_conv_w, conv_b=g_conv_b,
                 final_g=g_final_g)
    weights = dict(w_ada=w_ada, b_ada=b_ada, w_in=w_in, rel_bias=rel_bias, g_a=g_a, g_b=g_b, w_out=w_out, w_up=w_up,
                   conv_w=conv_w, conv_b=conv_b, w_down=w_down, final_g=final_g)
    m_in = dict(w_ada=m_w_ada, b_ada=m_b_ada, w_in=m_w_in, rel_bias=m_rel_bias, g_a=m_g_a, g_b=m_g_b, w_out=m_w_out,
                w_up=m_w_up, conv_w=m_conv_w, conv_b=m_conv_b, w_down=m_w_down, final_g=m_final_g)
    v_in = dict(w_ada=v_w_ada, b_ada=v_b_ada, w_in=v_w_in, rel_bias=v_rel_bias, g_a=v_g_a, g_b=v_g_b, w_out=v_w_out,
                w_up=v_w_up, conv_w=v_conv_w, conv_b=v_conv_b, w_down=v_w_down, final_g=v_final_g)
    order_w = ("w_ada", "b_ada", "w_in", "rel_bias", "g_a", "g_b", "w_out", "w_up", "conv_w", "conv_b", "w_down", "final_g")
    upd = {n: _adamw_nd(grads[n], weights[n], m_in[n], v_in[n], f"adamw_{n}") for n in grads}
    for n, mn, th in zip(BIG, mine, theirs):
        grads[n], *upd[n] = _adamw_halves(mn, th, c_idx, weights[n], m_in[n], v_in[n], f"adamw_{n}")
    return (loss, dx[None], *[grads[n] for n in order_w], *[upd[n][0] for n in order_w],
            *[upd[n][1] for n in order_w], *[upd[n][2] for n in order_w])
```

```python
import functools

import jax
import jax.numpy as jnp
from jax import lax
from jax.experimental import pallas as pl
from jax.experimental.pallas import tpu as pltpu

F32 = jnp.float32
BF16 = jnp.bfloat16
MESH = pl.DeviceIdType.MESH
ANY = pl.BlockSpec(memory_space=pl.ANY)
VMEM_FULL = pl.BlockSpec(memory_space=pltpu.VMEM)

HEAD_DIM = 64
N_HEADS = 8
W_GRP = N_HEADS * HEAD_DIM
CHUNK = 64
N_PREV = 8
BAND = (N_PREV + 1) * CHUNK
PAD = N_PREV * CHUNK
REL_CLIP = 128
N_REL = 2 * REL_CLIP + 1
EPS = 1e-6
N_CHIPS = 4
N_DEV = 8
LANES = 128
V7X_VMEM_LIMIT = 56 * 1024 * 1024

ADAM_LR = 0.001
ADAM_B1 = 0.9
ADAM_B2 = 0.999
ADAM_EPS = 1e-08
ADAM_WD = 0.01
ADAM_STEP = 10


def _params(**kw):
    return pltpu.CompilerParams(vmem_limit_bytes=V7X_VMEM_LIMIT, **kw)


def _pick(dim, pref, mult=LANES):
    t = (min(pref, dim) // mult) * mult
    while t >= mult:
        if dim % t == 0:
            return t
        t -= mult
    return dim


def _my_place():
    return lax.axis_index("x"), lax.axis_index("y"), lax.axis_index("c")


def _flip(v, bit):
    return 1 - v if bit else v


def _matmul(a, b, *, form, out_dtype, tm, tn, tk, name, shard_cols=None, halves=False):
    if form == "nn":
        (m, k), (_, n) = a.shape, b.shape
        a_map, a_blk = (lambda i, j, kk: (i, kk)), (tm, tk)
        b_map, b_blk = (lambda i, j, kk: (kk, j)), (tk, tn)
        dims = (((1,), (0,)), ((), ()))
    elif form == "nt":
        m, k = (a.shape[1], 2 * a.shape[2]) if halves else a.shape
        n = b.shape[0]
        if halves:
            per_half = k // 2 // tk
            a_map, a_blk = (lambda i, j, kk: (kk // per_half, i, kk % per_half)), (None, tm, tk)
        else:
            a_map, a_blk = (lambda i, j, kk: (i, kk)), (tm, tk)
        b_map, b_blk = (lambda i, j, kk: (j, kk)), (tn, tk)
        dims = (((1,), (1,)), ((), ()))
    else:
        k, m = a.shape
        n = 2 * b.shape[2] if halves else b.shape[1]
        a_map, a_blk = (lambda i, j, kk: (kk, i)), (tk, tm)
        if halves:
            per_half = n // 2 // tn
            b_map, b_blk = (lambda i, j, kk: (j // per_half, kk, j % per_half)), (None, tk, tn)
        else:
            b_map, b_blk = (lambda i, j, kk: (kk, j)), (tk, tn)
        dims = (((0,), (0,)), ((), ()))
    assert m % tm == 0 and n % tn == 0 and k % tk == 0, (name, m, n, k, tm, tn, tk)
    nk = k // tk
    if shard_cols is None:
        out_shape = jax.ShapeDtypeStruct((m, n), out_dtype)
        o_map, o_blk = (lambda i, j, kk: (i, j)), (tm, tn)
    else:
        per = shard_cols // tn
        assert shard_cols % tn == 0
        out_shape = jax.ShapeDtypeStruct((n // shard_cols, m, shard_cols), out_dtype)
        o_map, o_blk = (lambda i, j, kk: (j // per, i, j % per)), (None, tm, tn)
    a_bytes, b_bytes = a.size * a.dtype.itemsize, b.size * b.dtype.itemsize
    rows_outer = nk > 1 or (m // tm) * b_bytes + a_bytes <= (n // tn) * a_bytes + b_bytes
    grid = (m // tm, n // tn, nk) if rows_outer else (n // tn, m // tm, nk)
    order = (lambda f: f) if rows_outer else (lambda f: (lambda g0, g1, kk: f(g1, g0, kk)))

    def body(a_ref, b_ref, o_ref, *acc):
        part = lax.dot_general(a_ref[...], b_ref[...], dims, preferred_element_type=F32)
        if nk == 1:
            o_ref[...] = part.astype(out_dtype)
            return
        acc_ref, = acc
        kk = pl.program_id(2)

        @pl.when(kk == 0)
        def _():
            acc_ref[...] = part

        @pl.when(jnp.logical_and(kk > 0, kk < nk - 1))
        def _():
            acc_ref[...] += part

        @pl.when(kk == nk - 1)
        def _():
            o_ref[...] = (acc_ref[...] + part).astype(out_dtype)

    return pl.pallas_call(
        body, name=name, out_shape=out_shape, grid=grid,
        in_specs=[pl.BlockSpec(a_blk, order(a_map)), pl.BlockSpec(b_blk, order(b_map))],
        out_specs=pl.BlockSpec(o_blk, order(o_map)),
        scratch_shapes=[pltpu.VMEM((tm, tn), F32)] if nk > 1 else [], compiler_params=_params(),
    )(a, b)


def _row_spec(tr, d):
    return pl.BlockSpec((tr, d), lambda i: (i, 0))


def _vec_spec(d):
    return pl.BlockSpec((1, d), lambda i: (0, 0))


def _rms(xf):
    r = lax.rsqrt(jnp.mean(xf * xf, axis=-1, keepdims=True) + EPS)
    return xf * r, r


def _norm_mod(x, scale, shift, name):
    s, d = x.shape
    tr = _pick(s, 512, 8)

    def body(x_ref, sc_ref, sh_ref, o_ref):
        n, _ = _rms(x_ref[...])
        o_ref[...] = (n * (1.0 + sc_ref[...]) + sh_ref[...]).astype(BF16)

    return pl.pallas_call(
        body, name=name, out_shape=jax.ShapeDtypeStruct((s, d), BF16), grid=(s // tr,),
        in_specs=[_row_spec(tr, d), _vec_spec(d), _vec_spec(d)], out_specs=_row_spec(tr, d),
        compiler_params=_params(),
    )(x, scale, shift)


def _out_norm(oa, ob, g_a, g_b, name):
    s, w = oa.shape
    tr = _pick(s, 512, 8)

    def body(oa_ref, ob_ref, ga_ref, gb_ref, o_ref):
        na, _ = _rms(oa_ref[...])
        nb, _ = _rms(ob_ref[...])
        o_ref[:, :w] = (na * ga_ref[...]).astype(BF16)
        o_ref[:, w:] = (nb * gb_ref[...]).astype(BF16)

    return pl.pallas_call(
        body, name=name, out_shape=jax.ShapeDtypeStruct((s, 2 * w), BF16), grid=(s // tr,),
        in_specs=[_row_spec(tr, w), _row_spec(tr, w), _vec_spec(w), _vec_spec(w)],
        out_specs=_row_spec(tr, 2 * w), compiler_params=_params(),
    )(oa, ob, g_a, g_b)


def _residual(x, gate, m, name):
    s, d = x.shape
    tr = _pick(s, 512, 8)

    def body(x_ref, g_ref, m_ref, o_ref):
        o_ref[...] = x_ref[...] + g_ref[...] * m_ref[...]

    return pl.pallas_call(
        body, name=name, out_shape=jax.ShapeDtypeStruct((s, d), F32), grid=(s // tr,),
        in_specs=[_row_spec(tr, d), _vec_spec(d), _row_spec(tr, d)], out_specs=_row_spec(tr, d),
        compiler_params=_params(),
    )(x, gate, m)


def _shift_down(u, k):
    rows = lax.broadcasted_iota(jnp.int32, u.shape, 0)
    return jnp.where(rows >= k, pltpu.roll(u, k, 0), 0.0)


def _shift_up(u, k):
    s = u.shape[0]
    rows = lax.broadcasted_iota(jnp.int32, u.shape, 0)
    return jnp.where(rows < s - k, pltpu.roll(u, s - k, 0), 0.0)


def _conv(u, w_ref, b_ref):
    return w_ref[0:1, :] * _shift_down(u, 2) + w_ref[1:2, :] * _shift_down(u, 1) + w_ref[2:3, :] * u + b_ref[...]


def _conv_glu(u, conv_w, conv_b, name):
    s, f2 = u.shape
    f = f2 // 2
    tc = LANES
    nb = f // tc

    def body(ug_ref, uv_ref, wg_ref, wv_ref, bg_ref, bv_ref, o_ref):
        g = _conv(ug_ref[...], wg_ref, bg_ref)
        v = _conv(uv_ref[...], wv_ref, bv_ref)
        o_ref[...] = (g * jax.nn.sigmoid(g) * v).astype(BF16)

    col = lambda off: pl.BlockSpec((s, tc), lambda j: (0, j + off))
    wcol = lambda off: pl.BlockSpec((3, tc), lambda j: (0, j + off))
    bcol = lambda off: pl.BlockSpec((1, tc), lambda j: (0, j + off))
    return pl.pallas_call(
        body, name=name, out_shape=jax.ShapeDtypeStruct((s, f), BF16), grid=(nb,),
        in_specs=[col(0), col(nb), wcol(0), wcol(nb), bcol(0), bcol(nb)], out_specs=col(0),
        compiler_params=_params(),
    )(u, u, conv_w, conv_w, conv_b, conv_b)


def _conv_glu_bwd(u, da, conv_w, conv_b, name):
    s, f2 = u.shape
    f = f2 // 2
    tc = LANES
    nb = f // tc

    def body(ug_ref, uv_ref, da_ref, wg_ref, wv_ref, bg_ref, bv_ref, du_ref, dw_ref, db_ref):
        da_ = da_ref[...]
        ug, uv = ug_ref[...], uv_ref[...]
        g = _conv(ug, wg_ref, bg_ref)
        v = _conv(uv, wv_ref, bv_ref)
        sg = jax.nn.sigmoid(g)
        dg = da_ * v * (sg * (1.0 + g * (1.0 - sg)))
        dv = da_ * (g * sg)
        for h, (dy, uu, w_ref) in enumerate(((dg, ug, wg_ref), (dv, uv, wv_ref))):
            du = w_ref[2:3, :] * dy + w_ref[1:2, :] * _shift_up(dy, 1) + w_ref[0:1, :] * _shift_up(dy, 2)
            du_ref[h] = du.astype(BF16)
            dw_ref[h, 0:1, :] = jnp.sum(dy * _shift_down(uu, 2), axis=0, keepdims=True)
            dw_ref[h, 1:2, :] = jnp.sum(dy * _shift_down(uu, 1), axis=0, keepdims=True)
            dw_ref[h, 2:3, :] = jnp.sum(dy * uu, axis=0, keepdims=True)
            db_ref[h] = jnp.sum(dy, axis=0, keepdims=True)

    col = lambda off: pl.BlockSpec((s, tc), lambda j: (0, j + off))
    wcol = lambda off: pl.BlockSpec((3, tc), lambda j: (0, j + off))
    bcol = lambda off: pl.BlockSpec((1, tc), lambda j: (0, j + off))
    return pl.pallas_call(
        body, name=name, grid=(nb,),
        out_shape=(jax.ShapeDtypeStruct((2, s, f), BF16), jax.ShapeDtypeStruct((2, 3, f), F32),
                   jax.ShapeDtypeStruct((2, 1, f), F32)),
        in_specs=[col(0), col(nb), col(0), wcol(0), wcol(nb), bcol(0), bcol(nb)],
        out_specs=(pl.BlockSpec((2, s, tc), lambda j: (0, 0, j)), pl.BlockSpec((2, 3, tc), lambda j: (0, 0, j)),
                   pl.BlockSpec((2, 1, tc), lambda j: (0, 0, j))),
        compiler_params=_params(),
    )(u, u, da, conv_w, conv_w, conv_b, conv_b)


def _accumulate(ref, val):
    @pl.when(pl.program_id(0) == 0)
    def _():
        ref[...] = val

    @pl.when(pl.program_id(0) > 0)
    def _():
        ref[...] += val


def _rms_bwd(n, r, dn):
    return r * (dn - n * jnp.mean(dn * n, axis=-1, keepdims=True))


def _loss_head(x, final_g, target, name):
    s, d = x.shape
    tr = _pick(s, 512, 8)

    def body(x_ref, g_ref, t_ref, loss_ref, dx_ref, dg_ref):
        n, r = _rms(x_ref[...])
        diff = n * g_ref[...] - t_ref[...]
        part = 0.5 * jnp.sum(jnp.sum(diff * diff, axis=1, keepdims=True), axis=0, keepdims=True) / d
        _accumulate(loss_ref, part)
        dy = diff / d
        _accumulate(dg_ref, jnp.sum(dy * n, axis=0, keepdims=True))
        dx_ref[...] = _rms_bwd(n, r, dy * g_ref[...])

    return pl.pallas_call(
        body, name=name, grid=(s // tr,),
        out_shape=(jax.ShapeDtypeStruct((1, 1), F32), jax.ShapeDtypeStruct((s, d), F32), jax.ShapeDtypeStruct((1, d), F32)),
        in_specs=[_row_spec(tr, d), _vec_spec(d), _row_spec(tr, d)],
        out_specs=(pl.BlockSpec((1, 1), lambda i: (0, 0)), _row_spec(tr, d), _vec_spec(d)),
        compiler_params=_params(),
    )(x, final_g, target)


def _gate_bwd(dx, m, gate, name):
    s, d = dx.shape
    tr = _pick(s, 512, 8)

    def body(dx_ref, m_ref, g_ref, dm_ref, dg_ref):
        dxv = dx_ref[...]
        dm_ref[...] = (dxv * g_ref[...]).astype(BF16)
        _accumulate(dg_ref, jnp.sum(dxv * m_ref[...], axis=0, keepdims=True))

    return pl.pallas_call(
        body, name=name, grid=(s // tr,),
        out_shape=(jax.ShapeDtypeStruct((s, d), BF16), jax.ShapeDtypeStruct((1, d), F32)),
        in_specs=[_row_spec(tr, d), _row_spec(tr, d), _vec_spec(d)], out_specs=(_row_spec(tr, d), _vec_spec(d)),
        compiler_params=_params(),
    )(dx, m, gate)


def _norm_mod_bwd(x, dh, dres, scale, name):
    s, d = x.shape
    tr = _pick(s, 512, 8)

    def body(x_ref, dh_ref, dr_ref, sc_ref, dx_ref, dsc_ref, dsh_ref):
        n, r = _rms(x_ref[...])
        dh_ = dh_ref[...]
        _accumulate(dsc_ref, jnp.sum(dh_ * n, axis=0, keepdims=True))
        _accumulate(dsh_ref, jnp.sum(dh_, axis=0, keepdims=True))
        dx_ref[...] = dr_ref[...] + _rms_bwd(n, r, dh_ * (1.0 + sc_ref[...]))

    return pl.pallas_call(
        body, name=name, grid=(s // tr,),
        out_shape=(jax.ShapeDtypeStruct((s, d), F32), jax.ShapeDtypeStruct((1, d), F32), jax.ShapeDtypeStruct((1, d), F32)),
        in_specs=[_row_spec(tr, d), _row_spec(tr, d), _row_spec(tr, d), _vec_spec(d)],
        out_specs=(_row_spec(tr, d), _vec_spec(d), _vec_spec(d)), compiler_params=_params(),
    )(x, dh, dres, scale)


def _out_norm_bwd(oa, ob, dcat, g_a, g_b, name):
    s, w = oa.shape
    tr = _pick(s, 512, 8)

    def body(oa_ref, ob_ref, dc_ref, ga_ref, gb_ref, doa_ref, dob_ref, dga_ref, dgb_ref):
        for o_ref, g_ref, do_ref, dg_ref, lo in ((oa_ref, ga_ref, doa_ref, dga_ref, 0), (ob_ref, gb_ref, dob_ref, dgb_ref, w)):
            n, r = _rms(o_ref[...])
            dc = dc_ref[:, lo:lo + w]
            _accumulate(dg_ref, jnp.sum(dc * n, axis=0, keepdims=True))
            do_ref[...] = _rms_bwd(n, r, dc * g_ref[...])

    return pl.pallas_call(
        body, name=name, grid=(s // tr,),
        out_shape=(jax.ShapeDtypeStruct((s, w), F32), jax.ShapeDtypeStruct((s, w), F32),
                   jax.ShapeDtypeStruct((1, w), F32), jax.ShapeDtypeStruct((1, w), F32)),
        in_specs=[_row_spec(tr, w), _row_spec(tr, w), _row_spec(tr, 2 * w), _vec_spec(w), _vec_spec(w)],
        out_specs=(_row_spec(tr, w), _row_spec(tr, w), _vec_spec(w), _vec_spec(w)), compiler_params=_params(),
    )(oa, ob, dcat, g_a, g_b)


def _head_masks():
    lane = lax.broadcasted_iota(jnp.int32, (1, LANES), 1)
    return lane < HEAD_DIM, lane >= HEAD_DIM


def _nt(a, b):
    return lax.dot_general(a, b, (((1,), (1,)), ((), ())), preferred_element_type=F32)


def _tn(a, b):
    return lax.dot_general(a, b, (((0,), (0,)), ((), ())), preferred_element_type=F32)


def _nn(a, b):
    return jnp.dot(a, b, preferred_element_type=F32)


def _only(mask, v):
    return jnp.where(mask, v, jnp.zeros_like(v))


def _fill_padded(dst_ref, src_ref):
    dst_ref[0:PAD, :] = jnp.zeros((PAD, LANES), dst_ref.dtype)
    dst_ref[PAD:, :] = src_ref[...]


def _chunk_probs(s, bias, chunk):
    pos = lax.broadcasted_iota(jnp.int32, (1, BAND), 1)
    s = jnp.where(pos >= (N_PREV - chunk) * CHUNK, s + bias, -1e30)
    e = jnp.exp(s - jnp.max(s, axis=1, keepdims=True))
    return e / jnp.sum(e, axis=1, keepdims=True)


def _band_windows(i, cq, kpad, vpad):
    chunks = [i * cq + cc for cc in range(cq)]
    starts = [pl.multiple_of(ch * CHUNK, CHUNK) for ch in chunks]
    return chunks, starts, [kpad[pl.ds(st, BAND), :] for st in starts], [vpad[pl.ds(st, BAND), :] for st in starts]


def _attn_a_fwd(proj, band_bias, name):
    s = proj.shape[0]
    cq = 4
    tq = cq * CHUNK
    npair = N_HEADS // 2
    kcol, vcol = W_GRP // LANES, 2 * W_GRP // LANES

    def body(q_ref, k_ref, v_ref, b_ref, o_ref, kpad, vpad):
        i = pl.program_id(1)
        masks = _head_masks()

        @pl.when(i == 0)
        def _():
            _fill_padded(kpad, k_ref)
            _fill_padded(vpad, v_ref)

        chunks, _, kbs, vbs = _band_windows(i, cq, kpad, vpad)
        q2 = q_ref[...] * (HEAD_DIM ** -0.5)
        units = [(cc, h) for cc in range(cq) for h in range(2)]
        ss = [_nt(_only(masks[h], q2[cc * CHUNK:(cc + 1) * CHUNK]), kbs[cc]) for cc, h in units]
        ps = [_chunk_probs(s_, b_ref[h], chunks[cc]).astype(BF16) for s_, (cc, h) in zip(ss, units)]
        for cc in range(cq):
            o_ref[cc * CHUNK:(cc + 1) * CHUNK, :] = (_nn(ps[2 * cc], _only(masks[0], vbs[cc]))
                                                     + _nn(ps[2 * cc + 1], _only(masks[1], vbs[cc])))

    return pl.pallas_call(
        body, name=name, out_shape=jax.ShapeDtypeStruct((s, W_GRP), F32), grid=(npair, s // tq),
        in_specs=[pl.BlockSpec((tq, LANES), lambda p, i: (i, p)),
                  pl.BlockSpec((s, LANES), lambda p, i: (0, kcol + p)),
                  pl.BlockSpec((s, LANES), lambda p, i: (0, vcol + p)),
                  pl.BlockSpec((2, CHUNK, BAND), lambda p, i: (p, 0, 0))],
        out_specs=pl.BlockSpec((tq, LANES), lambda p, i: (i, p)),
        scratch_shapes=[pltpu.VMEM((s + PAD, LANES), BF16), pltpu.VMEM((s + PAD, LANES), BF16)],
        compiler_params=_params(),
    )(proj, proj, proj, band_bias)


def _attn_a_bwd(proj, band_bias, doa, name, rider=None):
    s = proj.shape[0]
    cq = 4
    tq = cq * CHUNK
    nq = s // tq
    npair = N_HEADS // 2
    kcol, vcol = W_GRP // LANES, 2 * W_GRP // LANES
    scale = HEAD_DIM ** -0.5

    def body(q_ref, k_ref, v_ref, b_ref, do_ref, dq_ref, dk_ref, dv_ref, db_ref, kpad, vpad, dkpad, dvpad):
        i = pl.program_id(1)
        masks = _head_masks()

        @pl.when(i == 0)
        def _():
            _fill_padded(kpad, k_ref)
            _fill_padded(vpad, v_ref)
            dkpad[...] = jnp.zeros_like(dkpad)
            dvpad[...] = jnp.zeros_like(dvpad)
            db_ref[...] = jnp.zeros_like(db_ref)

        chunks, starts, kbs, vbs = _band_windows(i, cq, kpad, vpad)
        q2 = q_ref[...] * scale
        do2 = do_ref[...].astype(BF16)
        units = [(cc, h) for cc in range(cq) for h in range(2)]
        qhs = [_only(masks[h], q2[cc * CHUNK:(cc + 1) * CHUNK]) for cc, h in units]
        dohs = [_only(masks[h], do2[cc * CHUNK:(cc + 1) * CHUNK]) for cc, h in units]
        ss = [_nt(qh, kbs[cc]) for qh, (cc, h) in zip(qhs, units)]
        dps = [_nt(doh, vbs[cc]) for doh, (cc, h) in zip(dohs, units)]
        ps = [_chunk_probs(s_, b_ref[h], chunks[cc]) for s_, (cc, h) in zip(ss, units)]
        dss = [p * (dp - jnp.sum(p * dp, axis=1, keepdims=True)) for p, dp in zip(ps, dps)]
        for h in range(2):
            db_ref[h] += functools.reduce(jnp.add, [dss[2 * cc + h] for cc in range(cq)])
        for cc in range(cq):
            u0, u1 = 2 * cc, 2 * cc + 1
            dsb = [dss[u0].astype(BF16), dss[u1].astype(BF16)]
            dq = _nn(dsb[0], _only(masks[0], kbs[cc])) + _nn(dsb[1], _only(masks[1], kbs[cc]))
            dq_ref[cc * CHUNK:(cc + 1) * CHUNK, :] = dq * scale
            dkpad[pl.ds(starts[cc], BAND), :] += _tn(jnp.concatenate(dsb, axis=0), jnp.concatenate([qhs[u0], qhs[u1]], axis=0))
            dvpad[pl.ds(starts[cc], BAND), :] += _tn(jnp.concatenate([ps[u0].astype(BF16), ps[u1].astype(BF16)], axis=0),
                                                     jnp.concatenate([dohs[u0], dohs[u1]], axis=0))

        @pl.when(i == nq - 1)
        def _():
            dk_ref[...] = dkpad[PAD:, :]
            dv_ref[...] = dvpad[PAD:, :]

    blk = pl.BlockSpec((tq, LANES), lambda p, i: (i, p))
    whole = pl.BlockSpec((s, LANES), lambda p, i: (0, p))
    bias_spec = pl.BlockSpec((2, CHUNK, BAND), lambda p, i: (p, 0, 0))
    return _call_with_rider(
        body, rider, name=name, grid=(npair, nq),
        out_shape=(jax.ShapeDtypeStruct((s, W_GRP), F32),) * 3 + (jax.ShapeDtypeStruct((N_HEADS, CHUNK, BAND), F32),),
        in_specs=[blk, pl.BlockSpec((s, LANES), lambda p, i: (0, kcol + p)),
                  pl.BlockSpec((s, LANES), lambda p, i: (0, vcol + p)), bias_spec, blk],
        out_specs=(blk, whole, whole, bias_spec),
        scratch_shapes=[pltpu.VMEM((s + PAD, LANES), BF16), pltpu.VMEM((s + PAD, LANES), BF16),
                        pltpu.VMEM((s + PAD, LANES), F32), pltpu.VMEM((s + PAD, LANES), F32)],
        args=(proj, proj, proj, band_bias, doa))


def _split3(v):
    hi = v.astype(BF16)
    r1 = v - hi.astype(F32)
    mid = r1.astype(BF16)
    lo = (r1 - mid.astype(F32)).astype(BF16)
    return hi, mid, lo


def _rel_bias_grad(dband_t, name):
    width = 3 * LANES

    def body(t_ref, o_ref):
        pos = lax.broadcasted_iota(jnp.int32, (BAND, width), 0)
        col = lax.broadcasted_iota(jnp.int32, (BAND, width), 1)
        acc = jnp.zeros((N_HEADS, width), F32)
        for q in range(CHUNK):
            idx = jnp.minimum(PAD + q - pos, REL_CLIP) + REL_CLIP
            onehot = (col == idx).astype(BF16)
            for part in _split3(t_ref[q]):
                acc = acc + _nn(part, onehot)
        o_ref[...] = acc

    return pl.pallas_call(
        body, name=name, out_shape=jax.ShapeDtypeStruct((N_HEADS, width), F32),
        in_specs=[VMEM_FULL], out_specs=VMEM_FULL, compiler_params=_params(),
    )(dband_t)


def _split2_wide(v):
    hi = v.astype(BF16)
    return jnp.concatenate([hi, (v - hi.astype(F32)).astype(BF16)], axis=1)


def _sb_logs(z, lower):
    e = jnp.exp(-jnp.abs(z))
    lb = jnp.minimum(z, 0.0) - jnp.log(1.0 + e)
    lk = lb - z
    if lower is not None:
        lk = jnp.where(lower, lk, 0.0)
    return z, e, lb, lk


def _tri_masks(tq):
    row = lax.broadcasted_iota(jnp.int32, (tq, tq), 0)
    col = lax.broadcasted_iota(jnp.int32, (tq, tq), 1)
    return row, col


def _stack2(m):
    return jnp.concatenate([m, m], axis=0).astype(BF16)


def _sb_fwd(proj, name, rider=None):
    s = proj.shape[0]
    tq = _pick(s, 256)
    nq = s // tq
    npair = N_HEADS // 2
    qcol, kcol, vcol = 3 * W_GRP // LANES, 4 * W_GRP // LANES, 5 * W_GRP // LANES

    assert nq % 2 == 0

    def body(q_ref, k_ref, v_ref, o_ref, l_ref):
        i = pl.program_id(1)
        masks = _head_masks()
        q2 = q_ref[...] * (HEAD_DIM ** -0.5)
        qs = [[_only(m, q2[c * tq:(c + 1) * tq]) for m in masks] for c in range(2)]
        row, col = _tri_masks(tq)
        lower = row > col
        after2 = _stack2(lower)

        def tile(kblock, chains, carry):
            accs, tails = [list(t) for t in carry[0]], [list(t) for t in carry[1]]
            ks = pl.multiple_of(kblock * tq, tq)
            kb = k_ref[pl.ds(ks, tq), :]
            vb = v_ref[pl.ds(ks, tq), :]
            units = [(c, h, diag) for c, diag in chains for h in range(2)]
            zs = [_nt(qs[c][h], kb) for c, h, _ in units]
            vh = [_only(masks[h], vb) for h in range(2)]
            lbs, lks, locs = [], [], []
            for z, (c, h, diag) in zip(zs, units):
                lb, lk = _sb_logs(z, lower if diag else None)[2:]
                lbs.append(lb)
                lks.append(lk)
                locs.append(_nn(_split2_wide(lk), after2))
            for lb, lk, loc, (c, h, diag) in zip(lbs, lks, locs, units):
                a = jnp.exp(lb + (loc + tails[c][h]))
                if diag:
                    a = jnp.where(lower, a, 0.0)
                accs[c][0] = accs[c][0] + _nn(a.astype(BF16), vh[h])
                tails[c][h] = tails[c][h] + (loc[:, 0:1] + lk[:, 0:1])
            return tuple(tuple(t) for t in accs), tuple(tuple(t) for t in tails)

        zero = jnp.zeros((tq, 1), F32)
        acc0 = jnp.zeros((tq, LANES), F32)
        carry = (((acc0,), (acc0,)), ((zero, zero), (zero, zero)))
        carry = tile(2 * i + 1, [(1, True)], carry)
        carry = tile(2 * i, [(0, True), (1, False)], carry)
        accs, tails = lax.fori_loop(1, 2 * i + 1, lambda jj, cr: tile(2 * i - jj, [(0, False), (1, False)], cr), carry)
        for c in range(2):
            o_ref[c * tq:(c + 1) * tq, :] = accs[c][0]
            l_ref[c * tq:(c + 1) * tq, 0:1] = tails[c][0]
            l_ref[c * tq:(c + 1) * tq, 1:2] = tails[c][1]

    return _call_with_rider(
        body, rider, name=name, grid=(npair, nq // 2),
        out_shape=(jax.ShapeDtypeStruct((s, W_GRP), F32), jax.ShapeDtypeStruct((npair, s, 2), F32)),
        in_specs=[pl.BlockSpec((2 * tq, LANES), lambda p, i: (i, qcol + p)),
                  pl.BlockSpec((s, LANES), lambda p, i: (0, kcol + p)),
                  pl.BlockSpec((s, LANES), lambda p, i: (0, vcol + p))],
        out_specs=(pl.BlockSpec((2 * tq, LANES), lambda p, i: (i, p)),
                   pl.BlockSpec((None, 2 * tq, 2), lambda p, i: (p, i, 0))),
        scratch_shapes=[], args=(proj, proj, proj))


def _sb_bwd(proj, ltot, dob, name, rider=None):
    s = proj.shape[0]
    tq = _pick(s, 256)
    nq = s // tq
    npair = N_HEADS // 2
    qcol, kcol, vcol = 3 * W_GRP // LANES, 4 * W_GRP // LANES, 5 * W_GRP // LANES
    scale = HEAD_DIM ** -0.5

    def body(q_ref, k_ref, v_ref, l_ref, do_ref, dq_ref, dk_ref, dv_ref):
        i = pl.program_id(1)
        masks = _head_masks()

        @pl.when(i == 0)
        def _():
            dk_ref[...] = jnp.zeros_like(dk_ref)
            dv_ref[...] = jnp.zeros_like(dv_ref)

        q2 = q_ref[...] * scale
        do2 = do_ref[...]
        part = lambda v, c: v[c * tq:(c + 1) * tq]
        qs = [[_only(m, part(q2, c)) for m in masks] for c in range(2)]
        doh = [[_only(m, part(do2, c)).astype(BF16) for m in masks] for c in range(2)]
        ltots = [[l_ref[c * tq:(c + 1) * tq, h:h + 1] for h in range(2)] for c in range(2)]
        row, col = _tri_masks(tq)
        lower = row > col
        upto2 = _stack2(row <= col)
        before2 = _stack2(row < col)

        def tile(kblock, chains, carry):
            dqs, heads, gsums = [[list(t) for t in part_] for part_ in carry]
            ks = pl.multiple_of(kblock * tq, tq)
            kb = k_ref[pl.ds(ks, tq), :]
            vb = v_ref[pl.ds(ks, tq), :]
            units = [(c, h, diag) for c, diag in chains for h in range(2)]
            zs = [_nt(qs[c][h], kb) for c, h, _ in units]
            das = [_nt(doh[c][h], vb) for c, h, _ in units]
            kh = [_only(masks[h], kb) for h in range(2)]
            sigs, lbs, locs = [], [], []
            for z_, (c, h, diag) in zip(zs, units):
                z, e, lb, lk = _sb_logs(z_, lower if diag else None)
                locs.append(_nn(_split2_wide(lk), upto2))
                r = 1.0 / (1.0 + e)
                sigs.append(jnp.where(z >= 0, r, e * r))
                lbs.append(lb)
            a_s, gs, glocs = [], [], []
            for lb, loc, da, (c, h, diag) in zip(lbs, locs, das, units):
                a = jnp.exp(lb + (ltots[c][h] - (heads[c][h] + loc)))
                if diag:
                    a = jnp.where(lower, a, 0.0)
                g = a * da
                glocs.append(_nn(_split2_wide(g), before2))
                a_s.append(a.astype(BF16))
                gs.append(g)
            dzbs = []
            for g, sig, loc, gloc, (c, h, diag) in zip(gs, sigs, locs, glocs, units):
                dz = g - sig * (g + (gsums[c][h] + gloc))
                if diag:
                    dz = jnp.where(lower, dz, 0.0)
                dzb = dz.astype(BF16)
                dzbs.append(dzb)
                dqs[c][0] = dqs[c][0] + _nn(dzb, kh[h])
                heads[c][h] = heads[c][h] + loc[:, tq - 1:tq]
                gsums[c][h] = gsums[c][h] + (gloc[:, tq - 1:tq] + g[:, tq - 1:tq])
            stack = lambda vs: vs[0] if len(vs) == 1 else jnp.concatenate(vs, axis=0)
            dk_ref[pl.ds(ks, tq), :] += _tn(stack(dzbs), stack([qs[c][h] for c, h, _ in units]))
            dv_ref[pl.ds(ks, tq), :] += _tn(stack(a_s), stack([doh[c][h] for c, h, _ in units]))
            return tuple(tuple(tuple(t) for t in part_) for part_ in (dqs, heads, gsums))

        zero = jnp.zeros((tq, 1), F32)
        acc0 = jnp.zeros((tq, LANES), F32)
        carry = (((acc0,), (acc0,)), ((zero, zero), (zero, zero)), ((zero, zero), (zero, zero)))
        carry = lax.fori_loop(0, 2 * i, lambda j, cr: tile(j, [(0, False), (1, False)], cr), carry)
        carry = tile(2 * i, [(0, True), (1, False)], carry)
        dqs, _, _ = tile(2 * i + 1, [(1, True)], carry)
        for c in range(2):
            dq_ref[c * tq:(c + 1) * tq, :] = dqs[c][0] * scale

    blk = pl.BlockSpec((2 * tq, LANES), lambda p, i: (i, p))
    whole = pl.BlockSpec((s, LANES), lambda p, i: (0, p))
    return _call_with_rider(
        body, rider, name=name, grid=(npair, nq // 2), out_shape=(jax.ShapeDtypeStruct((s, W_GRP), F32),) * 3,
        in_specs=[pl.BlockSpec((2 * tq, LANES), lambda p, i: (i, qcol + p)),
                  pl.BlockSpec((s, LANES), lambda p, i: (0, kcol + p)),
                  pl.BlockSpec((s, LANES), lambda p, i: (0, vcol + p)),
                  pl.BlockSpec((None, 2 * tq, 2), lambda p, i: (p, i, 0)), blk],
        out_specs=(blk, whole, whole), scratch_shapes=[], args=(proj, proj, proj, ltot, dob))


def _ada_fwd(c_all, w_ada, b_ada, name):
    nl, d, n = w_ada.shape
    tn = _pick(n, 512)

    def body(c_ref, w_ref, b_ref, o_ref):
        cv = c_ref[...]
        act = (cv * jax.nn.sigmoid(cv)).astype(BF16)
        o_ref[...] = _nn(act, w_ref[...].astype(BF16)) + b_ref[...]

    return pl.pallas_call(
        body, name=name, out_shape=jax.ShapeDtypeStruct((nl, N_DEV, n), F32), grid=(nl, n // tn),
        in_specs=[pl.BlockSpec((N_DEV, d), lambda l, j: (0, 0)), pl.BlockSpec((None, d, tn), lambda l, j: (l, 0, j)),
                  pl.BlockSpec((None, 1, tn), lambda l, j: (l, 0, j))],
        out_specs=pl.BlockSpec((None, N_DEV, tn), lambda l, j: (l, 0, j)), compiler_params=_params(),
    )(c_all, w_ada, b_ada)


def _ada_bwd(c_all, dmod, name):
    nl, _, n = dmod.shape
    d = c_all.shape[1]
    tn = _pick(n, 512)

    def body(c_ref, g_ref, o_ref):
        cv = c_ref[...]
        act = (cv * jax.nn.sigmoid(cv)).astype(BF16)
        o_ref[...] = _tn(act, g_ref[...].astype(BF16))

    return pl.pallas_call(
        body, name=name, out_shape=jax.ShapeDtypeStruct((nl, d, n), F32), grid=(nl, n // tn),
        in_specs=[pl.BlockSpec((N_DEV, d), lambda l, j: (0, 0)), pl.BlockSpec((None, N_DEV, tn), lambda l, j: (l, 0, j))],
        out_specs=pl.BlockSpec((None, d, tn), lambda l, j: (l, 0, j)), compiler_params=_params(),
    )(c_all, dmod)


def _adamw(g, w, m, v, name):
    r, c = g.shape
    tr = _pick(r, 512, 8)
    c1 = 1.0 - ADAM_B1 ** ADAM_STEP
    c2 = 1.0 - ADAM_B2 ** ADAM_STEP

    def body(g_ref, w_ref, m_ref, v_ref, d_ref, nm_ref, nv_ref):
        gv = g_ref[...]
        nm = ADAM_B1 * m_ref[...] + (1.0 - ADAM_B1) * gv
        nv = ADAM_B2 * v_ref[...] + (1.0 - ADAM_B2) * (gv * gv)
        d_ref[...] = -ADAM_LR * ((nm / c1) / (jnp.sqrt(nv / c2) + ADAM_EPS) + ADAM_WD * w_ref[...])
        nm_ref[...] = nm
        nv_ref[...] = nv

    spec = pl.BlockSpec((tr, c), lambda i: (i, 0))
    return pl.pallas_call(
        body, name=name, out_shape=(jax.ShapeDtypeStruct((r, c), F32),) * 3, grid=(r // tr,),
        in_specs=[spec] * 4, out_specs=(spec,) * 3, compiler_params=_params(),
    )(g, w, m, v)


def _adamw_nd(g, w, m, v, name):
    shape = w.shape
    two_d = (1, shape[0]) if len(shape) == 1 else (-1, shape[-1])
    outs = _adamw(*(t.reshape(two_d) for t in (g, w, m, v)), name=name)
    return tuple(o.reshape(shape) for o in outs)


def _allgather8(v, name):
    m, n = v.shape

    def body(v_ref, out_ref, send_sems, recv_sems, local_sem):
        x, y, c = _my_place()

        def rows(px, py, pc):
            return out_ref.at[pl.ds(pl.multiple_of((4 * px + 2 * py + pc) * m, 8), m), :]

        def peer(k):
            return _flip(x, k & 4), _flip(y, k & 2), _flip(c, k & 1)

        def copy(k, block):
            return pltpu.make_async_remote_copy(
                src_ref=v_ref, dst_ref=rows(*block), send_sem=send_sems.at[k - 1], recv_sem=recv_sems.at[k - 1],
                device_id=peer(k), device_id_type=MESH)

        mine = pltpu.make_async_copy(v_ref, rows(x, y, c), local_sem)
        mine.start()
        sends = [copy(k, (x, y, c)) for k in range(1, N_DEV)]
        for cp in sends:
            cp.start()
        for k in range(1, N_DEV):
            copy(k, peer(k)).wait_recv()
        for cp in sends:
            cp.wait_send()
        mine.wait()

    return pl.pallas_call(
        body, name=name, out_shape=jax.ShapeDtypeStruct((N_DEV * m, n), v.dtype),
        in_specs=[VMEM_FULL], out_specs=VMEM_FULL,
        scratch_shapes=[pltpu.SemaphoreType.DMA((N_DEV - 1,)), pltpu.SemaphoreType.DMA((N_DEV - 1,)),
                        pltpu.SemaphoreType.DMA],
        compiler_params=_params(),
    )(v)


def _chip_peers(x, y, c):
    out = []
    for k in range(1, N_CHIPS):
        px, py = _flip(x, k & 2), _flip(y, k & 1)
        out.append((2 * px + py, (px, py, c)))
    return out


def _gather_weights(shards, kinds, name):
    nw = len(shards)

    def full_shape(a, kind):
        l, r, n = a.shape
        return (l, r, N_CHIPS * n) if kind == "col" else (l, N_CHIPS * r, n)

    def body(*refs):
        ins, outs = refs[:nw], refs[nw:2 * nw]
        send_sems, recv_sems, local_sems = refs[2 * nw:]
        x, y, c = _my_place()
        chip = 2 * x + y

        def window(w, j):
            _, r, n = shards[w].shape
            if kinds[w] == "col":
                return outs[w].at[:, :, pl.ds(pl.multiple_of(j * n, LANES), n)]
            return outs[w].at[:, pl.ds(pl.multiple_of(j * r, 16), r), :]

        def copy(w, k, j, peer):
            return pltpu.make_async_remote_copy(
                src_ref=ins[w], dst_ref=window(w, j), send_sem=send_sems.at[3 * w + k], recv_sem=recv_sems.at[3 * w + k],
                device_id=peer, device_id_type=MESH)

        local = [pltpu.make_async_copy(ins[w], window(w, chip), local_sems.at[w]) for w in range(nw)]
        for cp in local:
            cp.start()
        peers = _chip_peers(x, y, c)
        sends = [copy(w, k, chip, peer) for w in range(nw) for k, (_, peer) in enumerate(peers)]
        for cp in sends:
            cp.start()
        for w in range(nw):
            for k, (pchip, peer) in enumerate(peers):
                copy(w, k, pchip, peer).wait_recv()
        for cp in sends:
            cp.wait_send()
        for cp in local:
            cp.wait()

    return pl.pallas_call(
        body, name=name,
        out_shape=tuple(jax.ShapeDtypeStruct(full_shape(a, kd), a.dtype) for a, kd in zip(shards, kinds)),
        in_specs=[ANY] * nw, out_specs=(ANY,) * nw,
        scratch_shapes=[pltpu.SemaphoreType.DMA((3 * nw,)), pltpu.SemaphoreType.DMA((3 * nw,)),
                        pltpu.SemaphoreType.DMA((nw,))],
        compiler_params=_params(),
    )(*shards)


def _rs_to_sibling(grads, name):
    nw = len(grads)

    def body(*refs):
        ins, outs = refs[:nw], refs[nw:2 * nw]
        send_sems, recv_sems = refs[2 * nw:]
        x, y, c = _my_place()
        sibling = (x, y, 1 - c)
        copies = [pltpu.make_async_remote_copy(
            src_ref=ins[w].at[j, 1 - c], dst_ref=outs[w].at[j], send_sem=send_sems.at[N_CHIPS * w + j],
            recv_sem=recv_sems.at[N_CHIPS * w + j], device_id=sibling, device_id_type=MESH)
            for w in range(nw) for j in range(N_CHIPS)]
        for cp in copies:
            cp.start()
        for cp in copies:
            cp.wait_recv()
        for cp in copies:
            cp.wait_send()

    return pl.pallas_call(
        body, name=name,
        out_shape=tuple(jax.ShapeDtypeStruct((N_CHIPS,) + g.shape[2:], g.dtype) for g in grads),
        in_specs=[ANY] * nw, out_specs=(ANY,) * nw,
        scratch_shapes=[pltpu.SemaphoreType.DMA((N_CHIPS * nw,)), pltpu.SemaphoreType.DMA((N_CHIPS * nw,))],
        compiler_params=_params(),
    )(*grads)


def _rs_to_chips(parts, name):
    nw = len(parts)

    def body(*refs):
        ins, outs = refs[:nw], refs[nw:2 * nw]
        send_sems, recv_sems, local_sems = refs[2 * nw:]
        x, y, c = _my_place()
        chip = 2 * x + y
        peers = _chip_peers(x, y, c)

        def copy(w, k, src_slab, dst_slab, peer):
            return pltpu.make_async_remote_copy(
                src_ref=ins[w].at[src_slab], dst_ref=outs[w].at[dst_slab], send_sem=send_sems.at[3 * w + k],
                recv_sem=recv_sems.at[3 * w + k], device_id=peer, device_id_type=MESH)

        local = [pltpu.make_async_copy(ins[w].at[chip], outs[w].at[chip], local_sems.at[w]) for w in range(nw)]
        for cp in local:
            cp.start()
        sends = [copy(w, k, pchip, chip, peer) for w in range(nw) for k, (pchip, peer) in enumerate(peers)]
        for cp in sends:
            cp.start()
        for w in range(nw):
            for k, (pchip, peer) in enumerate(peers):
                copy(w, k, chip, pchip, peer).wait_recv()
        for cp in sends:
            cp.wait_send()
        for cp in local:
            cp.wait()

    return pl.pallas_call(
        body, name=name, out_shape=tuple(jax.ShapeDtypeStruct(p.shape, p.dtype) for p in parts),
        in_specs=[ANY] * nw, out_specs=(ANY,) * nw,
        scratch_shapes=[pltpu.SemaphoreType.DMA((3 * nw,)), pltpu.SemaphoreType.DMA((3 * nw,)),
                        pltpu.SemaphoreType.DMA((nw,))],
        compiler_params=_params(),
    )(*parts)


def _rs_share_halves(halves, name):
    nw = len(halves)
    nl = len(halves[0])
    flat = [h for hs in halves for h in hs]

    def body(*refs):
        ins, outs = refs[:nw * nl], refs[nw * nl:nw * nl + nw]
        send_sems, recv_sems, local_sems = refs[nw * nl + nw:]
        x, y, c = _my_place()
        sibling = (x, y, 1 - c)
        local, sends, recvs = [], [], []
        for w in range(nw):
            for l in range(nl):
                n = nl * w + l
                local.append(pltpu.make_async_copy(ins[n], outs[w].at[l, c], local_sems.at[n]))
                sends.append(pltpu.make_async_remote_copy(
                    src_ref=ins[n], dst_ref=outs[w].at[l, c], send_sem=send_sems.at[n], recv_sem=recv_sems.at[n],
                    device_id=sibling, device_id_type=MESH))
                recvs.append(pltpu.make_async_remote_copy(
                    src_ref=ins[n], dst_ref=outs[w].at[l, 1 - c], send_sem=send_sems.at[n], recv_sem=recv_sems.at[n],
                    device_id=sibling, device_id_type=MESH))
        for cp in local + sends:
            cp.start()
        for cp in recvs:
            cp.wait_recv()
        for cp in sends:
            cp.wait_send()
        for cp in local:
            cp.wait()

    return pl.pallas_call(
        body, name=name,
        out_shape=tuple(jax.ShapeDtypeStruct((nl, 2) + hs[0].shape, hs[0].dtype) for hs in halves),
        in_specs=[ANY] * (nw * nl), out_specs=(ANY,) * nw,
        scratch_shapes=[pltpu.SemaphoreType.DMA((nw * nl,)), pltpu.SemaphoreType.DMA((nw * nl,)),
                        pltpu.SemaphoreType.DMA((nw * nl,))],
        compiler_params=_params(),
    )(*flat)


def _add_own_half(grad, got, c_idx, name):
    _, _, r, n = grad.shape
    tr = _pick(r, 256, 8)

    def body(c_ref, g_ref, t_ref, o_ref):
        o_ref[...] = g_ref[...] + t_ref[...]

    return pl.pallas_call(
        body, name=name, out_shape=jax.ShapeDtypeStruct((N_CHIPS, r, n), F32),
        grid_spec=pltpu.PrefetchScalarGridSpec(
            num_scalar_prefetch=1, grid=(N_CHIPS, r // tr),
            in_specs=[pl.BlockSpec((None, None, tr, n), lambda j, i, c_ref: (j, c_ref[0], i, 0)),
                      pl.BlockSpec((None, tr, n), lambda j, i, c_ref: (j, i, 0))],
            out_specs=pl.BlockSpec((None, tr, n), lambda j, i, c_ref: (j, i, 0))),
        compiler_params=_params(),
    )(c_idx, grad, got)


def _sum_slabs(slabs, name):
    ns, r, n = slabs.shape
    tr = _pick(r, 256, 8)

    def body(s_ref, o_ref):
        acc = s_ref[0]
        for j in range(1, ns):
            acc = acc + s_ref[j]
        o_ref[...] = acc

    return pl.pallas_call(
        body, name=name, out_shape=jax.ShapeDtypeStruct((r, n), F32), grid=(r // tr,),
        in_specs=[pl.BlockSpec((ns, tr, n), lambda i: (0, i, 0))], out_specs=pl.BlockSpec((tr, n), lambda i: (i, 0)),
        compiler_params=_params(),
    )(slabs)


def _band_bias(rel_bias):
    h = rel_bias.shape[0]
    n_far = PAD - REL_CLIP + CHUNK
    far = jnp.broadcast_to(rel_bias[:, N_REL - 1:N_REL], (h, n_far))
    near = rel_bias[:, REL_CLIP - CHUNK + 1:N_REL - 1][:, ::-1]
    line = jnp.concatenate([far, near], axis=1)
    return jnp.stack([line[:, CHUNK - 1 - q:CHUNK - 1 - q + BAND] for q in range(CHUNK)], axis=1)


def _pack_rows(pieces):
    flat = jnp.concatenate([p.reshape(-1) for p in pieces])
    rows = -(-flat.shape[0] // (8 * LANES)) * 8
    return jnp.pad(flat, (0, rows * LANES - flat.shape[0])).reshape(rows, LANES)


def _unpack_rows(packed, shapes):
    flat = packed.reshape(-1)
    out, at = [], 0
    for shp in shapes:
        size = 1
        for n in shp:
            size *= n
        out.append(flat[at:at + size].reshape(shp))
        at += size
    return out


def _layer_fwd(x, mod, w, band, tag):
    s, d = x.shape
    row = lambda i: mod[i:i + 1]
    h1 = _norm_mod(x, row(1), row(0), f"norm_mix{tag}")
    proj = _matmul(h1, w["w_in"], form="nn", out_dtype=BF16, tm=_pick(s, 512), tn=_pick(w["w_in"].shape[1], 768),
                   tk=d, name=f"proj{tag}")
    oa = _attn_a_fwd(proj, band, f"attn_a{tag}")
    ob, ltot = _sb_fwd(proj, f"attn_b{tag}")
    cat = _out_norm(oa, ob, w["g_a"], w["g_b"], f"out_norm{tag}")
    mixed = _matmul(cat, w["w_out"], form="nn", out_dtype=F32, tm=_pick(s, 512), tn=_pick(d, 1024),
                    tk=cat.shape[1], name=f"mix_out{tag}")
    x1 = _residual(x, row(2), mixed, f"res_mix{tag}")
    h2 = _norm_mod(x1, row(4), row(3), f"norm_ffn{tag}")
    f2 = w["w_up"].shape[1]
    u = _matmul(h2, w["w_up"], form="nn", out_dtype=F32, tm=_pick(s, 512), tn=_pick(f2, 1408), tk=d, name=f"up{tag}")
    a = _conv_glu(u, w["conv_w"], w["conv_b"], f"conv_glu{tag}")
    f = _matmul(a, w["w_down"], form="nn", out_dtype=F32, tm=_pick(s, 512), tn=_pick(d, 1024),
                tk=_pick(f2 // 2, 1408), name=f"down{tag}")
    x2 = _residual(x1, row(5), f, f"res_ffn{tag}")
    saved = dict(x=x, h1=h1, proj=proj, oa=oa, ob=ob, ltot=ltot, cat=cat, mixed=mixed, x1=x1, h2=h2, u=u, a=a, f=f)
    return x2, saved


def _layer_bwd(dx2, sv, mod, w, band, tag):
    s, d = dx2.shape
    row = lambda i: mod[i:i + 1]
    f2 = w["w_up"].shape[1]
    ff = f2 // 2
    n_in = w["w_in"].shape[1]
    df, dgate_ffn = _gate_bwd(dx2, sv["f"], row(5), f"gate_ffn_bwd{tag}")
    da = _matmul(df, w["w_down"], form="nt", out_dtype=F32, tm=_pick(s, 512), tn=_pick(ff, 1408), tk=d, name=f"down_dx{tag}")
    g_down = _matmul(sv["a"], df, form="tn", out_dtype=F32, tm=_pick(ff, 1408), tn=_pick(d, 512), tk=_pick(s, 2048),
                     name=f"down_dw{tag}")
    du2, dcw, dcb = _conv_glu_bwd(sv["u"], da, w["conv_w"], w["conv_b"], f"conv_glu_bwd{tag}")
    dh2 = _matmul(du2, w["w_up"], form="nt", out_dtype=F32, tm=_pick(s, 512), tn=_pick(d, 1024), tk=_pick(ff, 1408),
                  name=f"up_dx{tag}", halves=True)
    g_up = _matmul(sv["h2"], du2, form="tn", out_dtype=F32, tm=_pick(d, 512), tn=_pick(f2 // N_CHIPS, 1408),
                   tk=_pick(s, 2048), name=f"up_dw{tag}", shard_cols=f2 // N_CHIPS, halves=True)
    dx1, dscale_ffn, dshift_ffn = _norm_mod_bwd(sv["x1"], dh2, dx2, row(4), f"norm_ffn_bwd{tag}")
    dmixed, dgate_mix = _gate_bwd(dx1, sv["mixed"], row(2), f"gate_mix_bwd{tag}")
    dcat = _matmul(dmixed, w["w_out"], form="nt", out_dtype=F32, tm=_pick(s, 512), tn=_pick(2 * W_GRP, 1024), tk=d,
                   name=f"mix_out_dx{tag}")
    g_out = _matmul(sv["cat"], dmixed, form="tn", out_dtype=F32, tm=_pick(2 * W_GRP, 512), tn=_pick(d, 1024),
                    tk=_pick(s, 2048), name=f"mix_out_dw{tag}")
    doa, dob, dg_a, dg_b = _out_norm_bwd(sv["oa"], sv["ob"], dcat, w["g_a"], w["g_b"], f"out_norm_bwd{tag}")
    dqa, dka, dva, dband = _attn_a_bwd(sv["proj"], band, doa, f"attn_a_bwd{tag}")
    dqb, dkb, dvb = _sb_bwd(sv["proj"], sv["ltot"], dob, f"attn_b_bwd{tag}")
    drel = _rel_bias_grad(jnp.transpose(dband, (1, 0, 2)), f"rel_bias_bwd{tag}")[:, :N_REL]
    dproj = jnp.concatenate([dqa, dka, dva, dqb, dkb, dvb], axis=1).astype(BF16)
    dh1 = _matmul(dproj, w["w_in"], form="nt", out_dtype=F32, tm=_pick(s, 512), tn=_pick(d, 1024), tk=_pick(n_in, 1024),
                  name=f"proj_dx{tag}")
    g_in = _matmul(sv["h1"], dproj, form="tn", out_dtype=F32, tm=_pick(d, 512), tn=_pick(n_in // N_CHIPS, 768),
                   tk=_pick(s, 2048), name=f"proj_dw{tag}", shard_cols=n_in // N_CHIPS)
    dx, dscale_mix, dshift_mix = _norm_mod_bwd(sv["x"], dh1, dx1, row(1), f"norm_mix_bwd{tag}")
    dmod = jnp.concatenate([dshift_mix, dscale_mix, dgate_mix, dshift_ffn, dscale_ffn, dgate_ffn], axis=1)
    big = dict(w_in=g_in, w_out=g_out, w_up=g_up, w_down=g_down)
    dconv_w = jnp.concatenate([dcw[0], dcw[1]], axis=1)
    dconv_b = jnp.concatenate([dcb[0], dcb[1]], axis=1)
    small = dict(dmod=dmod, rel_bias=drel, g_a=dg_a, g_b=dg_b, conv_w=dconv_w, conv_b=dconv_b)
    return dx, big, small


def _kernel_unoverlapped(x, c, w_ada, b_ada, w_in, rel_bias, g_a, g_b, w_out, w_up, conv_w, conv_b, w_down, final_g, loss_target, m_w_ada, m_b_ada, m_w_in, m_rel_bias, m_g_a, m_g_b, m_w_out, m_w_up, m_conv_w, m_conv_b, m_w_down, m_final_g, v_w_ada, v_b_ada, v_w_in, v_rel_bias, v_g_a, v_g_b, v_w_out, v_w_up, v_conv_w, v_conv_b, v_w_down, v_final_g):
    xi, yi, ci = _my_place()
    chip = 2 * xi + yi
    dev = 4 * xi + 2 * yi + ci
    nl, d, n_ada = w_ada.shape
    s = x.shape[1]
    f2 = N_CHIPS * w_up.shape[2]
    nc = conv_w.shape[2]

    c_pad = jnp.pad(c, ((0, 7), (0, 0)))
    c_all = _allgather8(c_pad, "gather_c")[0::8]
    b_mine = lax.dynamic_slice_in_dim(b_ada, chip * n_ada, n_ada, axis=1)[:, None, :]
    mod_shard = _ada_fwd(c_all, w_ada, b_mine, "ada")
    pack2 = _pack_rows([mod_shard, conv_w])
    got2 = _allgather8(pack2, "gather_mod").reshape(N_DEV, -1)
    mods, convs = [], []
    for j in range(N_CHIPS):
        ms, cw = _unpack_rows(got2[2 * j], [mod_shard.shape, conv_w.shape])
        mods.append(lax.dynamic_index_in_dim(ms, dev, axis=1, keepdims=False))
        convs.append(cw)
    mod = jnp.concatenate(mods, axis=1).reshape(nl, 6, d)
    conv_w_full = jnp.concatenate(convs, axis=2)

    names = ("w_in", "w_out", "w_up", "w_down")
    kinds = ("col", "row", "col", "row")
    shards = dict(w_in=w_in, w_out=w_out, w_up=w_up, w_down=w_down)
    full = _gather_weights([shards[n].astype(BF16) for n in names], kinds, "gather_weights")
    full = dict(zip(names, full))

    xs = x[0]
    layers, saved, bands = [], [], []
    for l in range(nl):
        w = {n: full[n][l] for n in names}
        w.update(g_a=g_a[l:l + 1], g_b=g_b[l:l + 1], conv_w=conv_w_full[l], conv_b=conv_b[l:l + 1])
        band = _band_bias(rel_bias[l])
        xs, sv = _layer_fwd(xs, mod[l], w, band, f"_l{l}")
        layers.append(w)
        bands.append(band)
        saved.append(sv)
    loss_part, dx, dfinal_g = _loss_head(xs, final_g[None, :], loss_target[0], "loss_head")
    loss = lax.psum(loss_part[0, 0], ("x", "y", "c"))

    big, small = [None] * nl, [None] * nl
    for l in reversed(range(nl)):
        dx, big[l], small[l] = _layer_bwd(dx, saved[l], mod[l], layers[l], bands[l], f"_l{l}")

    small_names = ("dmod", "rel_bias", "g_a", "g_b", "conv_w", "conv_b")
    pieces = [small[l][n] for l in range(nl) for n in small_names] + [dfinal_g]
    shapes = [p.shape for p in pieces]
    pack3 = _pack_rows(pieces)
    got3 = _allgather8(pack3, "gather_small").reshape(N_DEV, pack3.shape[0], LANES)
    summed = _unpack_rows(_sum_slabs(got3, "sum_small"), shapes)
    tot = [dict(zip(small_names, summed[len(small_names) * l:len(small_names) * (l + 1)])) for l in range(nl)]
    g_final_g = summed[-1].reshape(-1)
    g_b_ada = jnp.stack([tot[l]["dmod"].reshape(-1) for l in range(nl)])
    g_rel = jnp.stack([tot[l]["rel_bias"] for l in range(nl)])
    g_ga = jnp.stack([tot[l]["g_a"].reshape(-1) for l in range(nl)])
    g_gb = jnp.stack([tot[l]["g_b"].reshape(-1) for l in range(nl)])
    g_conv_b = jnp.stack([tot[l]["conv_b"].reshape(-1) for l in range(nl)])
    g_conv_w = jnp.stack([lax.dynamic_slice_in_dim(tot[l]["conv_w"], chip * nc, nc, axis=1) for l in range(nl)])
    per_dev = [_unpack_rows(got3[j], shapes) for j in range(N_DEV)]
    dmod_all = jnp.stack([jnp.stack([per_dev[j][len(small_names) * l].reshape(-1) for j in range(N_DEV)])
                          for l in range(nl)])
    g_w_ada = _ada_bwd(c_all, lax.dynamic_slice_in_dim(dmod_all, chip * n_ada, n_ada, axis=2), "ada_bwd")

    order = [(n, l) for n in names for l in range(nl)]
    flat_g = [big[l][n].reshape(N_CHIPS, 2, -1, 1024) for n, l in order]
    from_sib = _rs_to_sibling(flat_g, "rs_sibling")
    c_idx = jnp.reshape(ci, (1,)).astype(jnp.int32)
    chip_part = [_add_own_half(g, t, c_idx, f"rs_add_{n}_l{l}") for g, t, (n, l) in zip(flat_g, from_sib, order)]
    from_chips = _rs_to_chips(chip_part, "rs_chips")
    my_half = [_sum_slabs(t, f"rs_sum_{n}_l{l}") for t, (n, l) in zip(from_chips, order)]
    shard_g = _rs_share_halves([[my_half[nl * i + l] for l in range(nl)] for i in range(len(names))], "rs_halves")
    g_big = {n: shard_g[i].reshape(shards[n].shape) for i, n in enumerate(names)}

    grads = dict(w_ada=g_w_ada, b_ada=g_b_ada, rel_bias=g_rel, g_a=g_ga, g_b=g_gb, conv_w=g_conv_w, conv_b=g_conv_b,
                 final_g=g_final_g)
    weights = dict(w_ada=w_ada, b_ada=b_ada, w_in=w_in, rel_bias=rel_bias, g_a=g_a, g_b=g_b, w_out=w_out, w_up=w_up,
                   conv_w=conv_w, conv_b=conv_b, w_down=w_down, final_g=final_g)
    m_in = dict(w_ada=m_w_ada, b_ada=m_b_ada, w_in=m_w_in, rel_bias=m_rel_bias, g_a=m_g_a, g_b=m_g_b, w_out=m_w_out,
                w_up=m_w_up, conv_w=m_conv_w, conv_b=m_conv_b, w_down=m_w_down, final_g=m_final_g)
    v_in = dict(w_ada=v_w_ada, b_ada=v_b_ada, w_in=v_w_in, rel_bias=v_rel_bias, g_a=v_g_a, g_b=v_g_b, w_out=v_w_out,
                w_up=v_w_up, conv_w=v_conv_w, conv_b=v_conv_b, w_down=v_w_down, final_g=v_final_g)
    order_w = ("w_ada", "b_ada", "w_in", "rel_bias", "g_a", "g_b", "w_out", "w_up", "conv_w", "conv_b", "w_down", "final_g")
    upd = {n: _adamw_nd(grads[n], weights[n], m_in[n], v_in[n], f"adamw_{n}") for n in grads}
    for n, mn, th in zip(BIG, mine, theirs):
        grads[n], *upd[n] = _adamw_halves(mn, th, c_idx, weights[n], m_in[n], v_in[n], f"adamw_{n}")
    return (loss, dx[None], *[grads[n] for n in order_w], *[upd[n][0] for n in order_w],
            *[upd[n][1] for n in order_w], *[upd[n][2] for n in order_w])


class _Rider:
    def __init__(self, ins, out_shapes, n_remote, n_local, parts):
        self.ins = list(ins)
        self.out_shapes = list(out_shapes)
        self.scratch = [pltpu.SemaphoreType.DMA((n_remote,)), pltpu.SemaphoreType.DMA((n_remote,)),
                        pltpu.SemaphoreType.DMA((max(n_local, 1),))]
        self.parts = parts

    def start(self, in_refs, out_refs, sems):
        local, sends, _ = self.parts(in_refs, out_refs, sems)
        for cp in local() + sends():
            cp.start()

    def wait(self, in_refs, out_refs, sems):
        local, sends, recvs = self.parts(in_refs, out_refs, sems)
        for cp in recvs():
            cp.wait_recv()
        for cp in sends():
            cp.wait_send()
        for cp in local():
            cp.wait()


def _call_with_rider(body, rider, *, name, grid, out_shape, in_specs, out_specs, scratch_shapes, args):
    if rider is None:
        return pl.pallas_call(body, name=name, grid=grid, out_shape=tuple(out_shape), in_specs=list(in_specs),
                              out_specs=tuple(out_specs), scratch_shapes=list(scratch_shapes),
                              compiler_params=_params())(*args)
    n_in, n_out, n_scr = len(in_specs), len(out_specs), len(scratch_shapes)
    r_in, r_out = len(rider.ins), len(rider.out_shapes)

    def both(*refs):
        at = 0
        groups = []
        for size in (n_in, r_in, n_out, r_out, n_scr, len(rider.scratch)):
            groups.append(refs[at:at + size])
            at += size
        own_in, ride_in, own_out, ride_out, own_scr, sems = groups
        steps = [pl.program_id(a) for a in range(len(grid))]
        first = functools.reduce(jnp.logical_and, [st == 0 for st in steps])
        last = functools.reduce(jnp.logical_and, [st == g - 1 for st, g in zip(steps, grid)])

        @pl.when(first)
        def _():
            rider.start(ride_in, ride_out, sems)

        body(*own_in, *own_out, *own_scr)

        @pl.when(last)
        def _():
            rider.wait(ride_in, ride_out, sems)

    outs = pl.pallas_call(
        both, name=name, grid=grid, out_shape=tuple(out_shape) + tuple(rider.out_shapes),
        in_specs=list(in_specs) + [ANY] * r_in, out_specs=tuple(out_specs) + (ANY,) * r_out,
        scratch_shapes=list(scratch_shapes) + rider.scratch, compiler_params=_params(),
    )(*args, *rider.ins)
    return tuple(outs[:n_out]) + (list(outs[n_out:]),)


def _run_rider(rider, name):
    r_in, r_out = len(rider.ins), len(rider.out_shapes)

    def body(*refs):
        ins, outs, sems = refs[:r_in], refs[r_in:r_in + r_out], refs[r_in + r_out:]
        rider.start(ins, outs, sems)
        rider.wait(ins, outs, sems)

    return list(pl.pallas_call(
        body, name=name, out_shape=tuple(rider.out_shapes), in_specs=[ANY] * r_in, out_specs=(ANY,) * r_out,
        scratch_shapes=rider.scratch, compiler_params=_params(),
    )(*rider.ins))


def _remote(src, dst, sems, n, peer):
    return pltpu.make_async_remote_copy(src_ref=src, dst_ref=dst, send_sem=sems[0].at[n], recv_sem=sems[1].at[n],
                                        device_id=peer, device_id_type=MESH)


def _gather_rider(shards, kinds):
    nw = len(shards)
    out_shapes = [jax.ShapeDtypeStruct((a.shape[0], N_CHIPS * a.shape[1]) if kd == "col" else
                                       (N_CHIPS * a.shape[0], a.shape[1]), a.dtype) for a, kd in zip(shards, kinds)]

    def parts(ins, outs, sems):
        x, y, c = _my_place()
        chip = 2 * x + y
        peers = _chip_peers(x, y, c)

        def window(w, j):
            r, n = shards[w].shape
            if kinds[w] == "col":
                return outs[w].at[:, pl.ds(pl.multiple_of(j * n, LANES), n)]
            return outs[w].at[pl.ds(pl.multiple_of(j * r, 16), r), :]

        local = lambda: [pltpu.make_async_copy(ins[w], window(w, chip), sems[2].at[w]) for w in range(nw)]
        sends = lambda: [_remote(ins[w], window(w, chip), sems, 3 * w + k, peer)
                         for w in range(nw) for k, (_, peer) in enumerate(peers)]
        recvs = lambda: [_remote(ins[w], window(w, pchip), sems, 3 * w + k, peer)
                         for w in range(nw) for k, (pchip, peer) in enumerate(peers)]
        return local, sends, recvs

    return _Rider(shards, out_shapes, 3 * nw, nw, parts)


def _half(ref3, h, rows):
    return ref3.at[:, pl.ds(pl.multiple_of(h * rows, 8), rows), :]


def _sibling_rider(grads):
    nw = len(grads)
    out_shapes = [jax.ShapeDtypeStruct((g.shape[0], g.shape[1] // 2, g.shape[2]), g.dtype) for g in grads]

    def parts(ins, outs, sems):
        x, y, c = _my_place()
        sibling = (x, y, 1 - c)
        copies = lambda: [_remote(_half(ins[w], 1 - c, grads[w].shape[1] // 2), outs[w], sems, w, sibling)
                          for w in range(nw)]
        return (lambda: []), copies, copies

    return _Rider(grads, out_shapes, nw, 0, parts)


def _chips_rider(parts_in):
    nw = len(parts_in)
    out_shapes = [jax.ShapeDtypeStruct(p.shape, p.dtype) for p in parts_in]

    def parts(ins, outs, sems):
        x, y, c = _my_place()
        chip = 2 * x + y
        peers = _chip_peers(x, y, c)
        local = lambda: [pltpu.make_async_copy(ins[w].at[chip], outs[w].at[chip], sems[2].at[w]) for w in range(nw)]
        sends = lambda: [_remote(ins[w].at[pchip], outs[w].at[chip], sems, 3 * w + k, peer)
                         for w in range(nw) for k, (pchip, peer) in enumerate(peers)]
        recvs = lambda: [_remote(ins[w].at[chip], outs[w].at[pchip], sems, 3 * w + k, peer)
                         for w in range(nw) for k, (pchip, peer) in enumerate(peers)]
        return local, sends, recvs

    return _Rider(parts_in, out_shapes, 3 * nw, nw, parts)


def _halves_rider(halves):
    nw, nl = len(halves), len(halves[0])
    flat = [h for hs in halves for h in hs]
    out_shapes = [jax.ShapeDtypeStruct((nl, 2 * hs[0].shape[0], hs[0].shape[1]), hs[0].dtype) for hs in halves]

    def parts(ins, outs, sems):
        x, y, c = _my_place()
        sibling = (x, y, 1 - c)

        def window(w, l, h):
            rows = halves[w][0].shape[0]
            return outs[w].at[l, pl.ds(pl.multiple_of(h * rows, 8), rows), :]

        pairs = [(w, l) for w in range(nw) for l in range(nl)]
        local = lambda: [pltpu.make_async_copy(ins[nl * w + l], window(w, l, c), sems[2].at[nl * w + l]) for w, l in pairs]
        sends = lambda: [_remote(ins[nl * w + l], window(w, l, c), sems, nl * w + l, sibling) for w, l in pairs]
        recvs = lambda: [_remote(ins[nl * w + l], window(w, l, 1 - c), sems, nl * w + l, sibling) for w, l in pairs]
        return local, sends, recvs

    return _Rider(flat, out_shapes, nw * nl, nw * nl, parts)


def _add_my_half(grad, got, c_idx, name):
    _, r, n = got.shape
    tr = _pick(r, 256, 8)
    nblk = r // tr

    def body(c_ref, g_ref, t_ref, o_ref):
        o_ref[...] = g_ref[...] + t_ref[...]

    return pl.pallas_call(
        body, name=name, out_shape=jax.ShapeDtypeStruct(got.shape, F32),
        grid_spec=pltpu.PrefetchScalarGridSpec(
            num_scalar_prefetch=1, grid=(N_CHIPS, nblk),
            in_specs=[pl.BlockSpec((None, tr, n), lambda j, i, c_ref: (j, c_ref[0] * nblk + i, 0)),
                      pl.BlockSpec((None, tr, n), lambda j, i, c_ref: (j, i, 0))],
            out_specs=pl.BlockSpec((None, tr, n), lambda j, i, c_ref: (j, i, 0))),
        compiler_params=_params(),
    )(c_idx, grad, got)


def _swap_rider(mine):
    nw = len(mine)
    out_shapes = [jax.ShapeDtypeStruct(a.shape, a.dtype) for a in mine]

    def parts(ins, outs, sems):
        x, y, c = _my_place()
        copies = lambda: [_remote(ins[w], outs[w], sems, w, (x, y, 1 - c)) for w in range(nw)]
        return (lambda: []), copies, copies

    return _Rider(mine, out_shapes, nw, 0, parts)


def _sum_layers(slabs, name):
    nl = len(slabs)
    ns, r, n = slabs[0].shape
    tr = _pick(r, 256, 8)

    def body(*refs):
        o_ref = refs[nl]
        for l in range(nl):
            acc = refs[l][0]
            for j in range(1, ns):
                acc = acc + refs[l][j]
            o_ref[l] = acc

    return pl.pallas_call(
        body, name=name, out_shape=jax.ShapeDtypeStruct((nl, r, n), F32), grid=(r // tr,),
        in_specs=[pl.BlockSpec((ns, tr, n), lambda i: (0, i, 0))] * nl,
        out_specs=pl.BlockSpec((nl, tr, n), lambda i: (0, i, 0)), compiler_params=_params(),
    )(*slabs)


def _adam_math(gv, w, m, v):
    c1 = 1.0 - ADAM_B1 ** ADAM_STEP
    c2 = 1.0 - ADAM_B2 ** ADAM_STEP
    nm = ADAM_B1 * m + (1.0 - ADAM_B1) * gv
    nv = ADAM_B2 * v + (1.0 - ADAM_B2) * (gv * gv)
    return -ADAM_LR * ((nm / c1) / (jnp.sqrt(nv / c2) + ADAM_EPS) + ADAM_WD * w), nm, nv


def _adamw_halves(mine, theirs, c_idx, w, m, v, name):
    nl, r, n = mine.shape
    tr = _pick(r, 256, 8)
    nblk = r // tr

    def body(c_ref, mine_ref, theirs_ref, w_ref, m_ref, v_ref, g_ref, d_ref, nm_ref, nv_ref):
        gv = jnp.where(pl.program_id(1) == c_ref[0], mine_ref[...], theirs_ref[...])
        g_ref[...] = gv
        d_ref[...], nm_ref[...], nv_ref[...] = _adam_math(gv, w_ref[...], m_ref[...], v_ref[...])

    half = pl.BlockSpec((None, tr, n), lambda l, h, i, c_ref: (l, i, 0))
    full = pl.BlockSpec((None, tr, n), lambda l, h, i, c_ref: (l, h * nblk + i, 0))
    return pl.pallas_call(
        body, name=name, out_shape=(jax.ShapeDtypeStruct(w.shape, F32),) * 4,
        grid_spec=pltpu.PrefetchScalarGridSpec(
            num_scalar_prefetch=1, grid=(nl, 2, nblk), in_specs=[half, half, full, full, full],
            out_specs=(full,) * 4),
        compiler_params=_params(),
    )(c_idx, mine, theirs, w, m, v)


def _by_chip(g):
    return g if g.ndim == 3 else g.reshape(N_CHIPS, g.shape[0] // N_CHIPS, g.shape[1])


BIG = ("w_in", "w_out", "w_up", "w_down")
BIG_KIND = dict(w_in="col", w_out="row", w_up="col", w_down="row")


def _forward_layer(x, mod, w, band, tag, rider=None, on_arrival=None):
    s, d = x.shape
    row = lambda i: mod[i:i + 1]
    h1 = _norm_mod(x, row(1), row(0), f"norm_mix{tag}")
    proj = _matmul(h1, w["w_in"], form="nn", out_dtype=BF16, tm=_pick(s, 512), tn=_pick(w["w_in"].shape[1], 768),
                   tk=d, name=f"proj{tag}")
    oa = _attn_a_fwd(proj, band, f"attn_a{tag}")
    if rider is None:
        ob, ltot = _sb_fwd(proj, f"attn_b{tag}")
    else:
        ob, ltot, arrived = _sb_fwd(proj, f"attn_b{tag}", rider)
        on_arrival(arrived)
    cat = _out_norm(oa, ob, w["g_a"], w["g_b"], f"out_norm{tag}")
    mixed = _matmul(cat, w["w_out"], form="nn", out_dtype=F32, tm=_pick(s, 512), tn=_pick(d, 1024),
                    tk=cat.shape[1], name=f"mix_out{tag}")
    x1 = _residual(x, row(2), mixed, f"res_mix{tag}")
    h2 = _norm_mod(x1, row(4), row(3), f"norm_ffn{tag}")
    f2 = w["w_up"].shape[1]
    u = _matmul(h2, w["w_up"], form="nn", out_dtype=F32, tm=_pick(s, 512), tn=_pick(f2, 1408), tk=d, name=f"up{tag}")
    a = _conv_glu(u, w["conv_w"], w["conv_b"], f"conv_glu{tag}")
    f = _matmul(a, w["w_down"], form="nn", out_dtype=F32, tm=_pick(s, 512), tn=_pick(d, 1024),
                tk=_pick(f2 // 2, 1408), name=f"down{tag}")
    x2 = _residual(x1, row(5), f, f"res_ffn{tag}")
    saved = dict(x=x, h1=h1, proj=proj, oa=oa, ob=ob, ltot=ltot, cat=cat, mixed=mixed, x1=x1, h2=h2, u=u, a=a, f=f)
    return x2, saved


def _backward_layer(dx2, sv, mod, w, band, tag, c_idx, waiting=None):
    s, d = dx2.shape
    row = lambda i: mod[i:i + 1]
    f2 = w["w_up"].shape[1]
    ff = f2 // 2
    n_in = w["w_in"].shape[1]
    df, dgate_ffn = _gate_bwd(dx2, sv["f"], row(5), f"gate_ffn_bwd{tag}")
    da = _matmul(df, w["w_down"], form="nt", out_dtype=F32, tm=_pick(s, 512), tn=_pick(ff, 1408), tk=d, name=f"down_dx{tag}")
    g_down = _matmul(sv["a"], df, form="tn", out_dtype=F32, tm=_pick(ff, 1408), tn=_pick(d, 512), tk=_pick(s, 2048),
                     name=f"down_dw{tag}")
    du2, dcw, dcb = _conv_glu_bwd(sv["u"], da, w["conv_w"], w["conv_b"], f"conv_glu_bwd{tag}")
    dh2 = _matmul(du2, w["w_up"], form="nt", out_dtype=F32, tm=_pick(s, 512), tn=_pick(d, 1024), tk=_pick(ff, 1408),
                  name=f"up_dx{tag}", halves=True)
    g_up = _matmul(sv["h2"], du2, form="tn", out_dtype=F32, tm=_pick(d, 512), tn=_pick(f2 // N_CHIPS, 1408),
                   tk=_pick(s, 2048), name=f"up_dw{tag}", shard_cols=f2 // N_CHIPS, halves=True)
    dx1, dscale_ffn, dshift_ffn = _norm_mod_bwd(sv["x1"], dh2, dx2, row(4), f"norm_ffn_bwd{tag}")
    dmixed, dgate_mix = _gate_bwd(dx1, sv["mixed"], row(2), f"gate_mix_bwd{tag}")
    dcat = _matmul(dmixed, w["w_out"], form="nt", out_dtype=F32, tm=_pick(s, 512), tn=_pick(2 * W_GRP, 1024), tk=d,
                   name=f"mix_out_dx{tag}")
    g_out = _matmul(sv["cat"], dmixed, form="tn", out_dtype=F32, tm=_pick(2 * W_GRP, 512), tn=_pick(d, 1024),
                    tk=_pick(s, 2048), name=f"mix_out_dw{tag}")
    doa, dob, dg_a, dg_b = _out_norm_bwd(sv["oa"], sv["ob"], dcat, w["g_a"], w["g_b"], f"out_norm_bwd{tag}")
    big = {("w_down", tag): _by_chip(g_down), ("w_up", tag): _by_chip(g_up), ("w_out", tag): _by_chip(g_out)}
    if waiting is None:
        dqa, dka, dva, dband = _attn_a_bwd(sv["proj"], band, doa, f"attn_a_bwd{tag}")
        dqb, dkb, dvb = _sb_bwd(sv["proj"], sv["ltot"], dob, f"attn_b_bwd{tag}")
    else:
        raw = {**waiting, **big}
        keys = list(raw)
        dqa, dka, dva, dband, from_sib = _attn_a_bwd(sv["proj"], band, doa, f"attn_a_bwd{tag}",
                                                     _sibling_rider([raw[k] for k in keys]))
        chip_part = [_add_my_half(raw[k], t, c_idx, f"rs_add_{k[0]}{k[1]}") for k, t in zip(keys, from_sib)]
        dqb, dkb, dvb, from_chips = _sb_bwd(sv["proj"], sv["ltot"], dob, f"attn_b_bwd{tag}", _chips_rider(chip_part))
        big = dict(zip(keys, from_chips))
    drel = _rel_bias_grad(jnp.transpose(dband, (1, 0, 2)), f"rel_bias_bwd{tag}")[:, :N_REL]
    dproj = jnp.concatenate([dqa, dka, dva, dqb, dkb, dvb], axis=1).astype(BF16)
    dh1 = _matmul(dproj, w["w_in"], form="nt", out_dtype=F32, tm=_pick(s, 512), tn=_pick(d, 1024), tk=_pick(n_in, 1024),
                  name=f"proj_dx{tag}")
    g_in = _matmul(sv["h1"], dproj, form="tn", out_dtype=F32, tm=_pick(d, 512), tn=_pick(n_in // N_CHIPS, 768),
                   tk=_pick(s, 2048), name=f"proj_dw{tag}", shard_cols=n_in // N_CHIPS)
    dx, dscale_mix, dshift_mix = _norm_mod_bwd(sv["x"], dh1, dx1, row(1), f"norm_mix_bwd{tag}")
    dmod = jnp.concatenate([dshift_mix, dscale_mix, dgate_mix, dshift_ffn, dscale_ffn, dgate_ffn], axis=1)
    dconv_w = jnp.concatenate([dcw[0], dcw[1]], axis=1)
    dconv_b = jnp.concatenate([dcb[0], dcb[1]], axis=1)
    small = dict(dmod=dmod, rel_bias=drel, g_a=dg_a, g_b=dg_b, conv_w=dconv_w, conv_b=dconv_b)
    return dx, big, g_in, small


def kernel(x, c, w_ada, b_ada, w_in, rel_bias, g_a, g_b, w_out, w_up, conv_w, conv_b, w_down, final_g, loss_target, m_w_ada, m_b_ada, m_w_in, m_rel_bias, m_g_a, m_g_b, m_w_out, m_w_up, m_conv_w, m_conv_b, m_w_down, m_final_g, v_w_ada, v_b_ada, v_w_in, v_rel_bias, v_g_a, v_g_b, v_w_out, v_w_up, v_conv_w, v_conv_b, v_w_down, v_final_g):
    xi, yi, ci = _my_place()
    chip = 2 * xi + yi
    dev = 4 * xi + 2 * yi + ci
    c_idx = jnp.reshape(ci, (1,)).astype(jnp.int32)
    nl, d, n_ada = w_ada.shape
    nc = conv_w.shape[2]
    assert nl == 2

    c_pad = jnp.pad(c, ((0, 7), (0, 0)))
    c_all = _allgather8(c_pad, "gather_c")[0::8]
    b_mine = lax.dynamic_slice_in_dim(b_ada, chip * n_ada, n_ada, axis=1)[:, None, :]
    mod_shard = _ada_fwd(c_all, w_ada, b_mine, "ada")
    pack2 = _pack_rows([mod_shard, conv_w])
    got2 = _allgather8(pack2, "gather_mod").reshape(N_DEV, -1)
    mods, convs = [], []
    for j in range(N_CHIPS):
        ms, cw = _unpack_rows(got2[2 * j], [mod_shard.shape, conv_w.shape])
        mods.append(lax.dynamic_index_in_dim(ms, dev, axis=1, keepdims=False))
        convs.append(cw)
    mod = jnp.concatenate(mods, axis=1).reshape(nl, 6, d)
    conv_w_full = jnp.concatenate(convs, axis=2)

    shards = dict(w_in=w_in, w_out=w_out, w_up=w_up, w_down=w_down)
    sh = {(n, l): shards[n][l].astype(BF16) for n in BIG for l in range(nl)}
    early = [("w_in", 0)]
    riding = [[("w_out", 0), ("w_up", 0), ("w_down", 0), ("w_in", 1)], [("w_out", 1), ("w_up", 1), ("w_down", 1)]]
    layers = [dict(g_a=g_a[l:l + 1], g_b=g_b[l:l + 1], conv_w=conv_w_full[l], conv_b=conv_b[l:l + 1]) for l in range(nl)]

    def gather_rider(keys):
        return _gather_rider([sh[k] for k in keys], [BIG_KIND[k[0]] for k in keys])

    def arrival(keys):
        def fill(arrived):
            for (n, l), full in zip(keys, arrived):
                layers[l][n] = full
        return fill

    arrival(early)(_run_rider(gather_rider(early), "gather_first"))

    xs = x[0]
    saved, bands = [], []
    for l in range(nl):
        band = _band_bias(rel_bias[l])
        xs, sv = _forward_layer(xs, mod[l], layers[l], band, f"_l{l}", gather_rider(riding[l]), arrival(riding[l]))
        bands.append(band)
        saved.append(sv)
    loss_part, dx, dfinal_g = _loss_head(xs, final_g[None, :], loss_target[0], "loss_head")
    loss = lax.psum(loss_part[0, 0], ("x", "y", "c"))

    small = [None] * nl
    dx, raw1, g_in1, small[1] = _backward_layer(dx, saved[1], mod[1], layers[1], bands[1], "_l1", c_idx)
    raw1[("w_in", "_l1")] = g_in1
    dx, parts, g_in0, small[0] = _backward_layer(dx, saved[0], mod[0], layers[0], bands[0], "_l0", c_idx, raw1)
    (from_sib,) = _run_rider(_sibling_rider([g_in0]), "rs_sibling_last")
    (parts[("w_in", "_l0")],) = _run_rider(
        _chips_rider([_add_my_half(g_in0, from_sib, c_idx, "rs_add_w_in_l0")]), "rs_chips_last")
    mine = [_sum_layers([parts[(n, f"_l{l}")] for l in range(nl)], f"rs_sum_{n}") for n in BIG]
    theirs = _run_rider(_swap_rider(mine), "rs_swap")

    small_names = ("dmod", "rel_bias", "g_a", "g_b", "conv_w", "conv_b")
    pieces = [small[l][n] for l in range(nl) for n in small_names] + [dfinal_g]
    shapes = [p.shape for p in pieces]
    pack3 = _pack_rows(pieces)
    got3 = _allgather8(pack3, "gather_small").reshape(N_DEV, pack3.shape[0], LANES)
    summed = _unpack_rows(_sum_slabs(got3, "sum_small"), shapes)
    tot = [dict(zip(small_names, summed[len(small_names) * l:len(small_names) * (l + 1)])) for l in range(nl)]
    g_final_g = summed[-1].reshape(-1)
    g_b_ada = jnp.stack([tot[l]["dmod"].reshape(-1) for l in range(nl)])
    g_rel = jnp.stack([tot[l]["rel_bias"] for l in range(nl)])
    g_ga = jnp.stack([tot[l]["g_a"].reshape(-1) for l in range(nl)])
    g_gb = jnp.stack([tot[l]["g_b"].reshape(-1) for l in range(nl)])
    g_conv_b = jnp.stack([tot[l]["conv_b"].reshape(-1) for l in range(nl)])
    g_conv_w = jnp.stack([lax.dynamic_slice_in_dim(tot[l]["conv_w"], chip * nc, nc, axis=1) for l in range(nl)])
    per_dev = [_unpack_rows(got3[j], shapes) for j in range(N_DEV)]
    dmod_all = jnp.stack([jnp.stack([per_dev[j][len(small_names) * l].reshape(-1) for j in range(N_DEV)])
                          for l in range(nl)])
    g_w_ada = _ada_bwd(c_all, lax.dynamic_slice_in_dim(dmod_all, chip * n_ada, n_ada, axis=2), "ada_bwd")

    grads = dict(w_ada=g_w_ada, b_ada=g_b_ada, rel_bias=g_rel, g_a=g_ga, g_b=g_gb, conv_w=g_conv_w, conv_b=g_conv_b,
                 final_g=g_final_g)
    weights = dict(w_ada=w_ada, b_ada=b_ada, w_in=w_in, rel_bias=rel_bias, g_a=g_a, g_b=g_b, w_out=w_out, w_up=w_up,
                   conv_w=conv_w, conv_b=conv_b, w_down=w_down, final_g=final_g)
    m_in = dict(w_ada=m_w_ada, b_ada=m_b_ada, w_in=m_w_in, rel_bias=m_rel_bias, g_a=m_g_a, g_b=m_g_b, w_out=m_w_out,
                w_up=m_w_up, conv_w=m_conv_w, conv_b=m_conv_b, w_down=m_w_down, final_g=m_final_g)
    v_in = dict(w_ada=v_w_ada, b_ada=v_b_ada, w_in=v_w_in, rel_bias=v_rel_bias, g_a=v_g_a, g_b=v_g_b, w_out=v_w_out,
                w_up=v_w_up, conv_w=v_conv_w, conv_b=v_conv_b, w_down=v_w_down, final_g=v_final_g)
    order_w = ("w_ada", "b_ada", "w_in", "rel_bias", "g_a", "g_b", "w_out", "w_up", "conv_w", "conv_b", "w_down", "final_g")
    upd = {n: _adamw_nd(grads[n], weights[n], m_in[n], v_in[n], f"adamw_{n}") for n in grads}
    for n, mn, th in zip(BIG, mine, theirs):
        grads[n], *upd[n] = _adamw_halves(mn, th, c_idx, weights[n], m_in[n], v_in[n], f"adamw_{n}")
    return (loss, dx[None], *[grads[n] for n in order_w], *[upd[n][0] for n in order_w],
            *[upd[n][1] for n in order_w], *[upd[n][2] for n in order_w])
```

```python
import functools

import jax
import jax.numpy as jnp
from jax import lax
from jax.experimental import pallas as pl
from jax.experimental.pallas import tpu as pltpu

F32 = jnp.float32
BF16 = jnp.bfloat16
MESH = pl.DeviceIdType.MESH
ANY = pl.BlockSpec(memory_space=pl.ANY)
VMEM_FULL = pl.BlockSpec(memory_space=pltpu.VMEM)

HEAD_DIM = 64
N_HEADS = 8
W_GRP = N_HEADS * HEAD_DIM
CHUNK = 64
N_PREV = 8
BAND = (N_PREV + 1) * CHUNK
PAD = N_PREV * CHUNK
REL_CLIP = 128
N_REL = 2 * REL_CLIP + 1
EPS = 1e-6
N_CHIPS = 4
N_DEV = 8
LANES = 128
V7X_VMEM_LIMIT = 56 * 1024 * 1024

ADAM_LR = 0.001
ADAM_B1 = 0.9
ADAM_B2 = 0.999
ADAM_EPS = 1e-08
ADAM_WD = 0.01
ADAM_STEP = 10


def _params(**kw):
    return pltpu.CompilerParams(vmem_limit_bytes=V7X_VMEM_LIMIT, **kw)


def _pick(dim, pref, mult=LANES):
    t = (min(pref, dim) // mult) * mult
    while t >= mult:
        if dim % t == 0:
            return t
        t -= mult
    return dim


def _my_place():
    return lax.axis_index("x"), lax.axis_index("y"), lax.axis_index("c")


def _flip(v, bit):
    return 1 - v if bit else v


def _matmul(a, b, *, form, out_dtype, tm, tn, tk, name, shard_cols=None, halves=False):
    if form == "nn":
        (m, k), (_, n) = a.shape, b.shape
        a_map, a_blk = (lambda i, j, kk: (i, kk)), (tm, tk)
        b_map, b_blk = (lambda i, j, kk: (kk, j)), (tk, tn)
        dims = (((1,), (0,)), ((), ()))
    elif form == "nt":
        m, k = (a.shape[1], 2 * a.shape[2]) if halves else a.shape
        n = b.shape[0]
        if halves:
            per_half = k // 2 // tk
            a_map, a_blk = (lambda i, j, kk: (kk // per_half, i, kk % per_half)), (None, tm, tk)
        else:
            a_map, a_blk = (lambda i, j, kk: (i, kk)), (tm, tk)
        b_map, b_blk = (lambda i, j, kk: (j, kk)), (tn, tk)
        dims = (((1,), (1,)), ((), ()))
    else:
        k, m = a.shape
        n = 2 * b.shape[2] if halves else b.shape[1]
        a_map, a_blk = (lambda i, j, kk: (kk, i)), (tk, tm)
        if halves:
            per_half = n // 2 // tn
            b_map, b_blk = (lambda i, j, kk: (j // per_half, kk, j % per_half)), (None, tk, tn)
        else:
            b_map, b_blk = (lambda i, j, kk: (kk, j)), (tk, tn)
        dims = (((0,), (0,)), ((), ()))
    assert m % tm == 0 and n % tn == 0 and k % tk == 0, (name, m, n, k, tm, tn, tk)
    nk = k // tk
    if shard_cols is None:
        out_shape = jax.ShapeDtypeStruct((m, n), out_dtype)
        o_map, o_blk = (lambda i, j, kk: (i, j)), (tm, tn)
    else:
        per = shard_cols // tn
        assert shard_cols % tn == 0
        out_shape = jax.ShapeDtypeStruct((n // shard_cols, m, shard_cols), out_dtype)
        o_map, o_blk = (lambda i, j, kk: (j // per, i, j % per)), (None, tm, tn)
    a_bytes, b_bytes = a.size * a.dtype.itemsize, b.size * b.dtype.itemsize
    rows_outer = nk > 1 or (m // tm) * b_bytes + a_bytes <= (n // tn) * a_bytes + b_bytes
    grid = (m // tm, n // tn, nk) if rows_outer else (n // tn, m // tm, nk)
    order = (lambda f: f) if rows_outer else (lambda f: (lambda g0, g1, kk: f(g1, g0, kk)))

    def body(a_ref, b_ref, o_ref, *acc):
        part = lax.dot_general(a_ref[...], b_ref[...], dims, preferred_element_type=F32)
        if nk == 1:
            o_ref[...] = part.astype(out_dtype)
            return
        acc_ref, = acc
        kk = pl.program_id(2)

        @pl.when(kk == 0)
        def _():
            acc_ref[...] = part

        @pl.when(jnp.logical_and(kk > 0, kk < nk - 1))
        def _():
            acc_ref[...] += part

        @pl.when(kk == nk - 1)
        def _():
            o_ref[...] = (acc_ref[...] + part).astype(out_dtype)

    return pl.pallas_call(
        body, name=name, out_shape=out_shape, grid=grid,
        in_specs=[pl.BlockSpec(a_blk, order(a_map)), pl.BlockSpec(b_blk, order(b_map))],
        out_specs=pl.BlockSpec(o_blk, order(o_map)),
        scratch_shapes=[pltpu.VMEM((tm, tn), F32)] if nk > 1 else [], compiler_params=_params(),
    )(a, b)


def _row_spec(tr, d):
    return pl.BlockSpec((tr, d), lambda i: (i, 0))


def _vec_spec(d):
    return pl.BlockSpec((1, d), lambda i: (0, 0))


def _rms(xf):
    r = lax.rsqrt(jnp.mean(xf * xf, axis=-1, keepdims=True) + EPS)
    return xf * r, r


def _norm_mod(x, scale, shift, name):
    s, d = x.shape
    tr = _pick(s, 512, 8)

    def body(x_ref, sc_ref, sh_ref, o_ref):
        n, _ = _rms(x_ref[...])
        o_ref[...] = (n * (1.0 + sc_ref[...]) + sh_ref[...]).astype(BF16)

    return pl.pallas_call(
        body, name=name, out_shape=jax.ShapeDtypeStruct((s, d), BF16), grid=(s // tr,),
        in_specs=[_row_spec(tr, d), _vec_spec(d), _vec_spec(d)], out_specs=_row_spec(tr, d),
        compiler_params=_params(),
    )(x, scale, shift)


def _out_norm(oa, ob, g_a, g_b, name):
    s, w = oa.shape
    tr = _pick(s, 512, 8)

    def body(oa_ref, ob_ref, ga_ref, gb_ref, o_ref):
        na, _ = _rms(oa_ref[...])
        nb, _ = _rms(ob_ref[...])
        o_ref[:, :w] = (na * ga_ref[...]).astype(BF16)
        o_ref[:, w:] = (nb * gb_ref[...]).astype(BF16)

    return pl.pallas_call(
        body, name=name, out_shape=jax.ShapeDtypeStruct((s, 2 * w), BF16), grid=(s // tr,),
        in_specs=[_row_spec(tr, w), _row_spec(tr, w), _vec_spec(w), _vec_spec(w)],
        out_specs=_row_spec(tr, 2 * w), compiler_params=_params(),
    )(oa, ob, g_a, g_b)


def _residual(x, gate, m, name):
    s, d = x.shape
    tr = _pick(s, 512, 8)

    def body(x_ref, g_ref, m_ref, o_ref):
        o_ref[...] = x_ref[...] + g_ref[...] * m_ref[...]

    return pl.pallas_call(
        body, name=name, out_shape=jax.ShapeDtypeStruct((s, d), F32), grid=(s // tr,),
        in_specs=[_row_spec(tr, d), _vec_spec(d), _row_spec(tr, d)], out_specs=_row_spec(tr, d),
        compiler_params=_params(),
    )(x, gate, m)


def _shift_down(u, k):
    rows = lax.broadcasted_iota(jnp.int32, u.shape, 0)
    return jnp.where(rows >= k, pltpu.roll(u, k, 0), 0.0)


def _shift_up(u, k):
    s = u.shape[0]
    rows = lax.broadcasted_iota(jnp.int32, u.shape, 0)
    return jnp.where(rows < s - k, pltpu.roll(u, s - k, 0), 0.0)


def _conv(u, w_ref, b_ref):
    return w_ref[0:1, :] * _shift_down(u, 2) + w_ref[1:2, :] * _shift_down(u, 1) + w_ref[2:3, :] * u + b_ref[...]


STRIP = 256


def _conv_strip(u_ref, t, r0, w_ref, b_ref):
    cur = u_ref[pl.ds(r0, STRIP), :]
    prev = u_ref[pl.ds(pl.multiple_of(jnp.maximum(r0 - 8, 0), 8), 8), :]
    ext = jnp.concatenate([jnp.where(t > 0, prev, 0.0), cur], axis=0)
    u1 = pltpu.roll(ext, 1, 0)[8:]
    u2 = pltpu.roll(ext, 2, 0)[8:]
    return w_ref[0:1, :] * u2 + w_ref[1:2, :] * u1 + w_ref[2:3, :] * cur + b_ref[...], u2, u1, cur


def _fold8(v):
    return functools.reduce(jnp.add, [v[r:r + 8] for r in range(0, STRIP, 8)])


def _conv_glu(u, conv_w, conv_b, name):
    s, f2 = u.shape
    f = f2 // 2
    tc = LANES
    nb = f // tc

    def body(ug_ref, uv_ref, wg_ref, wv_ref, bg_ref, bv_ref, o_ref):
        def strip(t, _):
            r0 = pl.multiple_of(t * STRIP, STRIP)
            g = _conv_strip(ug_ref, t, r0, wg_ref, bg_ref)[0]
            v = _conv_strip(uv_ref, t, r0, wv_ref, bv_ref)[0]
            o_ref[pl.ds(r0, STRIP), :] = (g * jax.nn.sigmoid(g) * v).astype(BF16)
            return 0

        lax.fori_loop(0, s // STRIP, strip, 0)

    col = lambda off: pl.BlockSpec((s, tc), lambda j: (0, j + off))
    wcol = lambda off: pl.BlockSpec((3, tc), lambda j: (0, j + off))
    bcol = lambda off: pl.BlockSpec((1, tc), lambda j: (0, j + off))
    return pl.pallas_call(
        body, name=name, out_shape=jax.ShapeDtypeStruct((s, f), BF16), grid=(nb,),
        in_specs=[col(0), col(nb), wcol(0), wcol(nb), bcol(0), bcol(nb)], out_specs=col(0),
        compiler_params=_params(),
    )(u, u, conv_w, conv_w, conv_b, conv_b)


def _conv_glu_bwd(u, da, conv_w, conv_b, name):
    s, f2 = u.shape
    f = f2 // 2
    tc = LANES
    nb = f // tc

    def body(ug_ref, uv_ref, da_ref, wg_ref, wv_ref, bg_ref, bv_ref, du_ref, dw_ref, db_ref, dyg_ref, dyv_ref):
        def strip1(t, acc):
            r0 = pl.multiple_of(t * STRIP, STRIP)
            g, *ug = _conv_strip(ug_ref, t, r0, wg_ref, bg_ref)
            v, *uv = _conv_strip(uv_ref, t, r0, wv_ref, bv_ref)
            da_ = da_ref[pl.ds(r0, STRIP), :]
            sg = jax.nn.sigmoid(g)
            dg = da_ * v * (sg * (1.0 + g * (1.0 - sg)))
            dv = da_ * (g * sg)
            dyg_ref[pl.ds(r0, STRIP), :] = dg
            dyv_ref[pl.ds(r0, STRIP), :] = dv
            new = [_fold8(dy * uu) for dy, us in ((dg, ug), (dv, uv)) for uu in us] + [_fold8(dg), _fold8(dv)]
            return tuple(a + n for a, n in zip(acc, new))

        zero = jnp.zeros((8, LANES), F32)
        acc = lax.fori_loop(0, s // STRIP, strip1, (zero,) * 8)
        for h in range(2):
            for tap in range(3):
                dw_ref[h, tap:tap + 1, :] = jnp.sum(acc[3 * h + tap], axis=0, keepdims=True)
            db_ref[h] = jnp.sum(acc[6 + h], axis=0, keepdims=True)
        dyg_ref[s:, :] = zero
        dyv_ref[s:, :] = zero

        def strip2(t, _):
            r0 = pl.multiple_of(t * STRIP, STRIP)
            for h, (dy_ref, w_ref) in enumerate(((dyg_ref, wg_ref), (dyv_ref, wv_ref))):
                cur = dy_ref[pl.ds(r0, STRIP), :]
                ext = jnp.concatenate([cur, dy_ref[pl.ds(r0 + STRIP, 8), :]], axis=0)
                d1 = pltpu.roll(ext, STRIP + 7, 0)[:STRIP]
                d2 = pltpu.roll(ext, STRIP + 6, 0)[:STRIP]
                du = w_ref[2:3, :] * cur + w_ref[1:2, :] * d1 + w_ref[0:1, :] * d2
                du_ref[h, pl.ds(r0, STRIP), :] = du.astype(BF16)
            return 0

        lax.fori_loop(0, s // STRIP, strip2, 0)

    col = lambda off: pl.BlockSpec((s, tc), lambda j: (0, j + off))
    wcol = lambda off: pl.BlockSpec((3, tc), lambda j: (0, j + off))
    bcol = lambda off: pl.BlockSpec((1, tc), lambda j: (0, j + off))
    return pl.pallas_call(
        body, name=name, grid=(nb,),
        out_shape=(jax.ShapeDtypeStruct((2, s, f), BF16), jax.ShapeDtypeStruct((2, 3, f), F32),
                   jax.ShapeDtypeStruct((2, 1, f), F32)),
        in_specs=[col(0), col(nb), col(0), wcol(0), wcol(nb), bcol(0), bcol(nb)],
        out_specs=(pl.BlockSpec((2, s, tc), lambda j: (0, 0, j)), pl.BlockSpec((2, 3, tc), lambda j: (0, 0, j)),
                   pl.BlockSpec((2, 1, tc), lambda j: (0, 0, j))),
        scratch_shapes=[pltpu.VMEM((s + 8, tc), F32), pltpu.VMEM((s + 8, tc), F32)],
        compiler_params=_params(),
    )(u, u, da, conv_w, conv_w, conv_b, conv_b)


def _accumulate(ref, val):
    @pl.when(pl.program_id(0) == 0)
    def _():
        ref[...] = val

    @pl.when(pl.program_id(0) > 0)
    def _():
        ref[...] += val


def _rms_bwd(n, r, dn):
    return r * (dn - n * jnp.mean(dn * n, axis=-1, keepdims=True))


def _loss_head(x, final_g, target, name):
    s, d = x.shape
    tr = _pick(s, 512, 8)

    def body(x_ref, g_ref, t_ref, loss_ref, dx_ref, dg_ref):
        n, r = _rms(x_ref[...])
        diff = n * g_ref[...] - t_ref[...]
        part = 0.5 * jnp.sum(jnp.sum(diff * diff, axis=1, keepdims=True), axis=0, keepdims=True) / d
        _accumulate(loss_ref, part)
        dy = diff / d
        _accumulate(dg_ref, jnp.sum(dy * n, axis=0, keepdims=True))
        dx_ref[...] = _rms_bwd(n, r, dy * g_ref[...])

    return pl.pallas_call(
        body, name=name, grid=(s // tr,),
        out_shape=(jax.ShapeDtypeStruct((1, 1), F32), jax.ShapeDtypeStruct((s, d), F32), jax.ShapeDtypeStruct((1, d), F32)),
        in_specs=[_row_spec(tr, d), _vec_spec(d), _row_spec(tr, d)],
        out_specs=(pl.BlockSpec((1, 1), lambda i: (0, 0)), _row_spec(tr, d), _vec_spec(d)),
        compiler_params=_params(),
    )(x, final_g, target)


def _gate_bwd(dx, m, gate, name):
    s, d = dx.shape
    tr = _pick(s, 512, 8)

    def body(dx_ref, m_ref, g_ref, dm_ref, dg_ref):
        dxv = dx_ref[...]
        dm_ref[...] = (dxv * g_ref[...]).astype(BF16)
        _accumulate(dg_ref, jnp.sum(dxv * m_ref[...], axis=0, keepdims=True))

    return pl.pallas_call(
        body, name=name, grid=(s // tr,),
        out_shape=(jax.ShapeDtypeStruct((s, d), BF16), jax.ShapeDtypeStruct((1, d), F32)),
        in_specs=[_row_spec(tr, d), _row_spec(tr, d), _vec_spec(d)], out_specs=(_row_spec(tr, d), _vec_spec(d)),
        compiler_params=_params(),
    )(dx, m, gate)


def _norm_mod_bwd(x, dh, dres, scale, name):
    s, d = x.shape
    tr = _pick(s, 512, 8)

    def body(x_ref, dh_ref, dr_ref, sc_ref, dx_ref, dsc_ref, dsh_ref):
        n, r = _rms(x_ref[...])
        dh_ = dh_ref[...]
        _accumulate(dsc_ref, jnp.sum(dh_ * n, axis=0, keepdims=True))
        _accumulate(dsh_ref, jnp.sum(dh_, axis=0, keepdims=True))
        dx_ref[...] = dr_ref[...] + _rms_bwd(n, r, dh_ * (1.0 + sc_ref[...]))

    return pl.pallas_call(
        body, name=name, grid=(s // tr,),
        out_shape=(jax.ShapeDtypeStruct((s, d), F32), jax.ShapeDtypeStruct((1, d), F32), jax.ShapeDtypeStruct((1, d), F32)),
        in_specs=[_row_spec(tr, d), _row_spec(tr, d), _row_spec(tr, d), _vec_spec(d)],
        out_specs=(_row_spec(tr, d), _vec_spec(d), _vec_spec(d)), compiler_params=_params(),
    )(x, dh, dres, scale)


def _out_norm_bwd(oa, ob, dcat, g_a, g_b, name):
    s, w = oa.shape
    tr = _pick(s, 512, 8)

    def body(oa_ref, ob_ref, dc_ref, ga_ref, gb_ref, doa_ref, dob_ref, dga_ref, dgb_ref):
        for o_ref, g_ref, do_ref, dg_ref, lo in ((oa_ref, ga_ref, doa_ref, dga_ref, 0), (ob_ref, gb_ref, dob_ref, dgb_ref, w)):
            n, r = _rms(o_ref[...])
            dc = dc_ref[:, lo:lo + w]
            _accumulate(dg_ref, jnp.sum(dc * n, axis=0, keepdims=True))
            do_ref[...] = _rms_bwd(n, r, dc * g_ref[...])

    return pl.pallas_call(
        body, name=name, grid=(s // tr,),
        out_shape=(jax.ShapeDtypeStruct((s, w), F32), jax.ShapeDtypeStruct((s, w), F32),
                   jax.ShapeDtypeStruct((1, w), F32), jax.ShapeDtypeStruct((1, w), F32)),
        in_specs=[_row_spec(tr, w), _row_spec(tr, w), _row_spec(tr, 2 * w), _vec_spec(w), _vec_spec(w)],
        out_specs=(_row_spec(tr, w), _row_spec(tr, w), _vec_spec(w), _vec_spec(w)), compiler_params=_params(),
    )(oa, ob, dcat, g_a, g_b)


def _head_masks():
    lane = lax.broadcasted_iota(jnp.int32, (1, LANES), 1)
    return lane < HEAD_DIM, lane >= HEAD_DIM


def _nt(a, b):
    return lax.dot_general(a, b, (((1,), (1,)), ((), ())), preferred_element_type=F32)


def _tn(a, b):
    return lax.dot_general(a, b, (((0,), (0,)), ((), ())), preferred_element_type=F32)


def _nn(a, b):
    return jnp.dot(a, b, preferred_element_type=F32)


def _only(mask, v):
    return jnp.where(mask, v, jnp.zeros_like(v))


def _fill_padded(dst_ref, src_ref):
    dst_ref[0:PAD, :] = jnp.zeros((PAD, LANES), dst_ref.dtype)
    dst_ref[PAD:, :] = src_ref[...]


def _chunk_probs(s, bias, chunk):
    pos = lax.broadcasted_iota(jnp.int32, (1, BAND), 1)
    s = jnp.where(pos >= (N_PREV - chunk) * CHUNK, s + bias, -1e30)
    e = jnp.exp(s - jnp.max(s, axis=1, keepdims=True))
    return e / jnp.sum(e, axis=1, keepdims=True)


def _band_windows(i, cq, kpad, vpad):
    chunks = [i * cq + cc for cc in range(cq)]
    starts = [pl.multiple_of(ch * CHUNK, CHUNK) for ch in chunks]
    return chunks, starts, [kpad[pl.ds(st, BAND), :] for st in starts], [vpad[pl.ds(st, BAND), :] for st in starts]


def _attn_a_fwd(proj, band_bias, name):
    s = proj.shape[0]
    cq = 4
    tq = cq * CHUNK
    npair = N_HEADS // 2
    kcol, vcol = W_GRP // LANES, 2 * W_GRP // LANES

    def body(q_ref, k_ref, v_ref, b_ref, o_ref, kpad, vpad):
        i = pl.program_id(1)
        masks = _head_masks()

        @pl.when(i == 0)
        def _():
            _fill_padded(kpad, k_ref)
            _fill_padded(vpad, v_ref)

        chunks, _, kbs, vbs = _band_windows(i, cq, kpad, vpad)
        q2 = q_ref[...] * (HEAD_DIM ** -0.5)
        units = [(cc, h) for cc in range(cq) for h in range(2)]
        ss = [_nt(_only(masks[h], q2[cc * CHUNK:(cc + 1) * CHUNK]), kbs[cc]) for cc, h in units]
        ps = [_chunk_probs(s_, b_ref[h], chunks[cc]).astype(BF16) for s_, (cc, h) in zip(ss, units)]
        for cc in range(cq):
            o_ref[cc * CHUNK:(cc + 1) * CHUNK, :] = (_nn(ps[2 * cc], _only(masks[0], vbs[cc]))
                                                     + _nn(ps[2 * cc + 1], _only(masks[1], vbs[cc])))

    return pl.pallas_call(
        body, name=name, out_shape=jax.ShapeDtypeStruct((s, W_GRP), F32), grid=(npair, s // tq),
        in_specs=[pl.BlockSpec((tq, LANES), lambda p, i: (i, p)),
                  pl.BlockSpec((s, LANES), lambda p, i: (0, kcol + p)),
                  pl.BlockSpec((s, LANES), lambda p, i: (0, vcol + p)),
                  pl.BlockSpec((2, CHUNK, BAND), lambda p, i: (p, 0, 0))],
        out_specs=pl.BlockSpec((tq, LANES), lambda p, i: (i, p)),
        scratch_shapes=[pltpu.VMEM((s + PAD, LANES), BF16), pltpu.VMEM((s + PAD, LANES), BF16)],
        compiler_params=_params(),
    )(proj, proj, proj, band_bias)


def _attn_a_bwd(proj, band_bias, doa, name, rider=None):
    s = proj.shape[0]
    cq = 4
    tq = cq * CHUNK
    nq = s // tq
    npair = N_HEADS // 2
    kcol, vcol = W_GRP // LANES, 2 * W_GRP // LANES
    scale = HEAD_DIM ** -0.5

    def body(q_ref, k_ref, v_ref, b_ref, do_ref, dq_ref, dk_ref, dv_ref, db_ref, kpad, vpad, dkpad, dvpad):
        i = pl.program_id(1)
        masks = _head_masks()

        @pl.when(i == 0)
        def _():
            _fill_padded(kpad, k_ref)
            _fill_padded(vpad, v_ref)
            dkpad[...] = jnp.zeros_like(dkpad)
            dvpad[...] = jnp.zeros_like(dvpad)
            db_ref[...] = jnp.zeros_like(db_ref)

        chunks, starts, kbs, vbs = _band_windows(i, cq, kpad, vpad)
        q2 = q_ref[...] * scale
        do2 = do_ref[...].astype(BF16)
        units = [(cc, h) for cc in range(cq) for h in range(2)]
        qhs = [_only(masks[h], q2[cc * CHUNK:(cc + 1) * CHUNK]) for cc, h in units]
        dohs = [_only(masks[h], do2[cc * CHUNK:(cc + 1) * CHUNK]) for cc, h in units]
        ss = [_nt(qh, kbs[cc]) for qh, (cc, h) in zip(qhs, units)]
        dps = [_nt(doh, vbs[cc]) for doh, (cc, h) in zip(dohs, units)]
        ps = [_chunk_probs(s_, b_ref[h], chunks[cc]) for s_, (cc, h) in zip(ss, units)]
        dss = [p * (dp - jnp.sum(p * dp, axis=1, keepdims=True)) for p, dp in zip(ps, dps)]
        for h in range(2):
            db_ref[h] += functools.reduce(jnp.add, [dss[2 * cc + h] for cc in range(cq)])
        for cc in range(cq):
            u0, u1 = 2 * cc, 2 * cc + 1
            dsb = [dss[u0].astype(BF16), dss[u1].astype(BF16)]
            dq = _nn(dsb[0], _only(masks[0], kbs[cc])) + _nn(dsb[1], _only(masks[1], kbs[cc]))
            dq_ref[cc * CHUNK:(cc + 1) * CHUNK, :] = dq * scale
            dkpad[pl.ds(starts[cc], BAND), :] += _tn(jnp.concatenate(dsb, axis=0), jnp.concatenate([qhs[u0], qhs[u1]], axis=0))
            dvpad[pl.ds(starts[cc], BAND), :] += _tn(jnp.concatenate([ps[u0].astype(BF16), ps[u1].astype(BF16)], axis=0),
                                                     jnp.concatenate([dohs[u0], dohs[u1]], axis=0))

        @pl.when(i == nq - 1)
        def _():
            dk_ref[...] = dkpad[PAD:, :]
            dv_ref[...] = dvpad[PAD:, :]

    blk = pl.BlockSpec((tq, LANES), lambda p, i: (i, p))
    whole = pl.BlockSpec((s, LANES), lambda p, i: (0, p))
    bias_spec = pl.BlockSpec((2, CHUNK, BAND), lambda p, i: (p, 0, 0))
    return _call_with_rider(
        body, rider, name=name, grid=(npair, nq),
        out_shape=(jax.ShapeDtypeStruct((s, W_GRP), F32),) * 3 + (jax.ShapeDtypeStruct((N_HEADS, CHUNK, BAND), F32),),
        in_specs=[blk, pl.BlockSpec((s, LANES), lambda p, i: (0, kcol + p)),
                  pl.BlockSpec((s, LANES), lambda p, i: (0, vcol + p)), bias_spec, blk],
        out_specs=(blk, whole, whole, bias_spec),
        scratch_shapes=[pltpu.VMEM((s + PAD, LANES), BF16), pltpu.VMEM((s + PAD, LANES), BF16),
                        pltpu.VMEM((s + PAD, LANES), F32), pltpu.VMEM((s + PAD, LANES), F32)],
        args=(proj, proj, proj, band_bias, doa))


def _split3(v):
    hi = v.astype(BF16)
    r1 = v - hi.astype(F32)
    mid = r1.astype(BF16)
    lo = (r1 - mid.astype(F32)).astype(BF16)
    return hi, mid, lo


def _rel_bias_grad(dband_t, name):
    width = 3 * LANES

    def body(t_ref, o_ref):
        pos = lax.broadcasted_iota(jnp.int32, (BAND, width), 0)
        col = lax.broadcasted_iota(jnp.int32, (BAND, width), 1)
        acc = jnp.zeros((N_HEADS, width), F32)
        for q in range(CHUNK):
            idx = jnp.minimum(PAD + q - pos, REL_CLIP) + REL_CLIP
            onehot = (col == idx).astype(BF16)
            for part in _split3(t_ref[q]):
                acc = acc + _nn(part, onehot)
        o_ref[...] = acc

    return pl.pallas_call(
        body, name=name, out_shape=jax.ShapeDtypeStruct((N_HEADS, width), F32),
        in_specs=[VMEM_FULL], out_specs=VMEM_FULL, compiler_params=_params(),
    )(dband_t)


def _split2_wide(v):
    hi = v.astype(BF16)
    return jnp.concatenate([hi, (v - hi.astype(F32)).astype(BF16)], axis=1)


def _sb_logs(z, lower):
    e = jnp.exp(-jnp.abs(z))
    lb = jnp.minimum(z, 0.0) - jnp.log(1.0 + e)
    lk = lb - z
    if lower is not None:
        lk = jnp.where(lower, lk, 0.0)
    return z, e, lb, lk


def _tri_masks(tq):
    row = lax.broadcasted_iota(jnp.int32, (tq, tq), 0)
    col = lax.broadcasted_iota(jnp.int32, (tq, tq), 1)
    return row, col


def _stack2(m):
    return jnp.concatenate([m, m], axis=0).astype(BF16)


def _sb_fwd(proj, name, rider=None):
    s = proj.shape[0]
    tq = _pick(s, 256)
    nq = s // tq
    npair = N_HEADS // 2
    qcol, kcol, vcol = 3 * W_GRP // LANES, 4 * W_GRP // LANES, 5 * W_GRP // LANES

    assert nq % 2 == 0

    def body(q_ref, k_ref, v_ref, o_ref, l_ref):
        i = pl.program_id(1)
        masks = _head_masks()
        q2 = q_ref[...] * (HEAD_DIM ** -0.5)
        qs = [[_only(m, q2[c * tq:(c + 1) * tq]) for m in masks] for c in range(2)]
        row, col = _tri_masks(tq)
        lower = row > col
        after2 = _stack2(lower)

        def tile(kblock, chains, carry):
            accs, tails = [list(t) for t in carry[0]], [list(t) for t in carry[1]]
            ks = pl.multiple_of(kblock * tq, tq)
            kb = k_ref[pl.ds(ks, tq), :]
            vb = v_ref[pl.ds(ks, tq), :]
            units = [(c, h, diag) for c, diag in chains for h in range(2)]
            zs = [_nt(qs[c][h], kb) for c, h, _ in units]
            vh = [_only(masks[h], vb) for h in range(2)]
            lbs, lks, locs = [], [], []
            for z, (c, h, diag) in zip(zs, units):
                lb, lk = _sb_logs(z, lower if diag else None)[2:]
                lbs.append(lb)
                lks.append(lk)
                locs.append(_nn(_split2_wide(lk), after2))
            for lb, lk, loc, (c, h, diag) in zip(lbs, lks, locs, units):
                a = jnp.exp(lb + (loc + tails[c][h]))
                if diag:
                    a = jnp.where(lower, a, 0.0)
                accs[c][0] = accs[c][0] + _nn(a.astype(BF16), vh[h])
                tails[c][h] = tails[c][h] + (loc[:, 0:1] + lk[:, 0:1])
            return tuple(tuple(t) for t in accs), tuple(tuple(t) for t in tails)

        zero = jnp.zeros((tq, 1), F32)
        acc0 = jnp.zeros((tq, LANES), F32)
        carry = (((acc0,), (acc0,)), ((zero, zero), (zero, zero)))
        carry = tile(2 * i + 1, [(1, True)], carry)
        carry = tile(2 * i, [(0, True), (1, False)], carry)
        accs, tails = lax.fori_loop(1, 2 * i + 1, lambda jj, cr: tile(2 * i - jj, [(0, False), (1, False)], cr), carry)
        for c in range(2):
            o_ref[c * tq:(c + 1) * tq, :] = accs[c][0]
            l_ref[c * tq:(c + 1) * tq, 0:1] = tails[c][0]
            l_ref[c * tq:(c + 1) * tq, 1:2] = tails[c][1]

    return _call_with_rider(
        body, rider, name=name, grid=(npair, nq // 2),
        out_shape=(jax.ShapeDtypeStruct((s, W_GRP), F32), jax.ShapeDtypeStruct((npair, s, 2), F32)),
        in_specs=[pl.BlockSpec((2 * tq, LANES), lambda p, i: (i, qcol + p)),
                  pl.BlockSpec((s, LANES), lambda p, i: (0, kcol + p)),
                  pl.BlockSpec((s, LANES), lambda p, i: (0, vcol + p))],
        out_specs=(pl.BlockSpec((2 * tq, LANES), lambda p, i: (i, p)),
                   pl.BlockSpec((None, 2 * tq, 2), lambda p, i: (p, i, 0))),
        scratch_shapes=[], args=(proj, proj, proj))


def _sb_bwd(proj, ltot, dob, name, rider=None):
    s = proj.shape[0]
    tq = _pick(s, 256)
    nq = s // tq
    npair = N_HEADS // 2
    qcol, kcol, vcol = 3 * W_GRP // LANES, 4 * W_GRP // LANES, 5 * W_GRP // LANES
    scale = HEAD_DIM ** -0.5

    def body(q_ref, k_ref, v_ref, l_ref, do_ref, dq_ref, dk_ref, dv_ref):
        i = pl.program_id(1)
        masks = _head_masks()

        @pl.when(i == 0)
        def _():
            dk_ref[...] = jnp.zeros_like(dk_ref)
            dv_ref[...] = jnp.zeros_like(dv_ref)

        q2 = q_ref[...] * scale
        do2 = do_ref[...]
        part = lambda v, c: v[c * tq:(c + 1) * tq]
        qs = [[_only(m, part(q2, c)) for m in masks] for c in range(2)]
        doh = [[_only(m, part(do2, c)).astype(BF16) for m in masks] for c in range(2)]
        ltots = [[l_ref[c * tq:(c + 1) * tq, h:h + 1] for h in range(2)] for c in range(2)]
        row, col = _tri_masks(tq)
        lower = row > col
        upto2 = _stack2(row <= col)
        before = (row < col).astype(BF16)

        def tile(kblock, chains, carry):
            dqs, heads, gsums = [[list(t) for t in part_] for part_ in carry]
            ks = pl.multiple_of(kblock * tq, tq)
            kb = k_ref[pl.ds(ks, tq), :]
            vb = v_ref[pl.ds(ks, tq), :]
            units = [(c, h, diag) for c, diag in chains for h in range(2)]
            zs = [_nt(qs[c][h], kb) for c, h, _ in units]
            das = [_nt(doh[c][h], vb) for c, h, _ in units]
            kh = [_only(masks[h], kb) for h in range(2)]
            sigs, lbs, locs = [], [], []
            for z_, (c, h, diag) in zip(zs, units):
                z, e, lb, lk = _sb_logs(z_, lower if diag else None)
                locs.append(_nn(_split2_wide(lk), upto2))
                r = 1.0 / (1.0 + e)
                sigs.append(jnp.where(z >= 0, r, e * r))
                lbs.append(lb)
            a_s, gs, glocs = [], [], []
            for lb, loc, da, (c, h, diag) in zip(lbs, locs, das, units):
                a = jnp.exp(lb + (ltots[c][h] - (heads[c][h] + loc)))
                if diag:
                    a = jnp.where(lower, a, 0.0)
                g = a * da
                glocs.append(_nn(g.astype(BF16), before))
                a_s.append(a.astype(BF16))
                gs.append(g)
            dzbs = []
            for g, sig, loc, gloc, (c, h, diag) in zip(gs, sigs, locs, glocs, units):
                dz = g - sig * (g + (gsums[c][h] + gloc))
                if diag:
                    dz = jnp.where(lower, dz, 0.0)
                dzb = dz.astype(BF16)
                dzbs.append(dzb)
                dqs[c][0] = dqs[c][0] + _nn(dzb, kh[h])
                heads[c][h] = heads[c][h] + loc[:, tq - 1:tq]
                gsums[c][h] = gsums[c][h] + (gloc[:, tq - 1:tq] + g[:, tq - 1:tq])
            stack = lambda vs: vs[0] if len(vs) == 1 else jnp.concatenate(vs, axis=0)
            dk_ref[pl.ds(ks, tq), :] += _tn(stack(dzbs), stack([qs[c][h] for c, h, _ in units]))
            dv_ref[pl.ds(ks, tq), :] += _tn(stack(a_s), stack([doh[c][h] for c, h, _ in units]))
            return tuple(tuple(tuple(t) for t in part_) for part_ in (dqs, heads, gsums))

        zero = jnp.zeros((tq, 1), F32)
        acc0 = jnp.zeros((tq, LANES), F32)
        carry = (((acc0,), (acc0,)), ((zero, zero), (zero, zero)), ((zero, zero), (zero, zero)))
        carry = lax.fori_loop(0, 2 * i, lambda j, cr: tile(j, [(0, False), (1, False)], cr), carry)
        carry = tile(2 * i, [(0, True), (1, False)], carry)
        dqs, _, _ = tile(2 * i + 1, [(1, True)], carry)
        for c in range(2):
            dq_ref[c * tq:(c + 1) * tq, :] = dqs[c][0] * scale

    blk = pl.BlockSpec((2 * tq, LANES), lambda p, i: (i, p))
    whole = pl.BlockSpec((s, LANES), lambda p, i: (0, p))
    return _call_with_rider(
        body, rider, name=name, grid=(npair, nq // 2), out_shape=(jax.ShapeDtypeStruct((s, W_GRP), F32),) * 3,
        in_specs=[pl.BlockSpec((2 * tq, LANES), lambda p, i: (i, qcol + p)),
                  pl.BlockSpec((s, LANES), lambda p, i: (0, kcol + p)),
                  pl.BlockSpec((s, LANES), lambda p, i: (0, vcol + p)),
                  pl.BlockSpec((None, 2 * tq, 2), lambda p, i: (p, i, 0)), blk],
        out_specs=(blk, whole, whole), scratch_shapes=[], args=(proj, proj, proj, ltot, dob))


def _ada_fwd(c_all, w_ada, b_ada, name):
    nl, d, n = w_ada.shape
    tn = _pick(n, 512)

    def body(c_ref, w_ref, b_ref, o_ref):
        cv = c_ref[...]
        act = (cv * jax.nn.sigmoid(cv)).astype(BF16)
        o_ref[...] = _nn(act, w_ref[...].astype(BF16)) + b_ref[...]

    return pl.pallas_call(
        body, name=name, out_shape=jax.ShapeDtypeStruct((nl, N_DEV, n), F32), grid=(nl, n // tn),
        in_specs=[pl.BlockSpec((N_DEV, d), lambda l, j: (0, 0)), pl.BlockSpec((None, d, tn), lambda l, j: (l, 0, j)),
                  pl.BlockSpec((None, 1, tn), lambda l, j: (l, 0, j))],
        out_specs=pl.BlockSpec((None, N_DEV, tn), lambda l, j: (l, 0, j)), compiler_params=_params(),
    )(c_all, w_ada, b_ada)


def _ada_bwd(c_all, dmod, name):
    nl, _, n = dmod.shape
    d = c_all.shape[1]
    tn = _pick(n, 512)

    def body(c_ref, g_ref, o_ref):
        cv = c_ref[...]
        act = (cv * jax.nn.sigmoid(cv)).astype(BF16)
        o_ref[...] = _tn(act, g_ref[...].astype(BF16))

    return pl.pallas_call(
        body, name=name, out_shape=jax.ShapeDtypeStruct((nl, d, n), F32), grid=(nl, n // tn),
        in_specs=[pl.BlockSpec((N_DEV, d), lambda l, j: (0, 0)), pl.BlockSpec((None, N_DEV, tn), lambda l, j: (l, 0, j))],
        out_specs=pl.BlockSpec((None, d, tn), lambda l, j: (l, 0, j)), compiler_params=_params(),
    )(c_all, dmod)


def _adamw(g, w, m, v, name):
    r, c = g.shape
    tr = _pick(r, 512, 8)
    c1 = 1.0 - ADAM_B1 ** ADAM_STEP
    c2 = 1.0 - ADAM_B2 ** ADAM_STEP

    def body(g_ref, w_ref, m_ref, v_ref, d_ref, nm_ref, nv_ref):
        gv = g_ref[...]
        nm = ADAM_B1 * m_ref[...] + (1.0 - ADAM_B1) * gv
        nv = ADAM_B2 * v_ref[...] + (1.0 - ADAM_B2) * (gv * gv)
        d_ref[...] = -ADAM_LR * ((nm / c1) / (jnp.sqrt(nv / c2) + ADAM_EPS) + ADAM_WD * w_ref[...])
        nm_ref[...] = nm
        nv_ref[...] = nv

    spec = pl.BlockSpec((tr, c), lambda i: (i, 0))
    return pl.pallas_call(
        body, name=name, out_shape=(jax.ShapeDtypeStruct((r, c), F32),) * 3, grid=(r // tr,),
        in_specs=[spec] * 4, out_specs=(spec,) * 3, compiler_params=_params(),
    )(g, w, m, v)


def _adamw_nd(g, w, m, v, name):
    shape = w.shape
    two_d = (1, shape[0]) if len(shape) == 1 else (-1, shape[-1])
    outs = _adamw(*(t.reshape(two_d) for t in (g, w, m, v)), name=name)
    return tuple(o.reshape(shape) for o in outs)


def _allgather8(v, name):
    m, n = v.shape

    def body(v_ref, out_ref, send_sems, recv_sems, local_sem):
        x, y, c = _my_place()

        def rows(px, py, pc):
            return out_ref.at[pl.ds(pl.multiple_of((4 * px + 2 * py + pc) * m, 8), m), :]

        def peer(k):
            return _flip(x, k & 4), _flip(y, k & 2), _flip(c, k & 1)

        def copy(k, block):
            return pltpu.make_async_remote_copy(
                src_ref=v_ref, dst_ref=rows(*block), send_sem=send_sems.at[k - 1], recv_sem=recv_sems.at[k - 1],
                device_id=peer(k), device_id_type=MESH)

        mine = pltpu.make_async_copy(v_ref, rows(x, y, c), local_sem)
        mine.start()
        sends = [copy(k, (x, y, c)) for k in range(1, N_DEV)]
        for cp in sends:
            cp.start()
        for k in range(1, N_DEV):
            copy(k, peer(k)).wait_recv()
        for cp in sends:
            cp.wait_send()
        mine.wait()

    return pl.pallas_call(
        body, name=name, out_shape=jax.ShapeDtypeStruct((N_DEV * m, n), v.dtype),
        in_specs=[VMEM_FULL], out_specs=VMEM_FULL,
        scratch_shapes=[pltpu.SemaphoreType.DMA((N_DEV - 1,)), pltpu.SemaphoreType.DMA((N_DEV - 1,)),
                        pltpu.SemaphoreType.DMA],
        compiler_params=_params(),
    )(v)


def _chip_peers(x, y, c):
    out = []
    for k in range(1, N_CHIPS):
        px, py = _flip(x, k & 2), _flip(y, k & 1)
        out.append((2 * px + py, (px, py, c)))
    return out


def _gather_weights(shards, kinds, name):
    nw = len(shards)

    def full_shape(a, kind):
        l, r, n = a.shape
        return (l, r, N_CHIPS * n) if kind == "col" else (l, N_CHIPS * r, n)

    def body(*refs):
        ins, outs = refs[:nw], refs[nw:2 * nw]
        send_sems, recv_sems, local_sems = refs[2 * nw:]
        x, y, c = _my_place()
        chip = 2 * x + y

        def window(w, j):
            _, r, n = shards[w].shape
            if kinds[w] == "col":
                return outs[w].at[:, :, pl.ds(pl.multiple_of(j * n, LANES), n)]
            return outs[w].at[:, pl.ds(pl.multiple_of(j * r, 16), r), :]

        def copy(w, k, j, peer):
            return pltpu.make_async_remote_copy(
                src_ref=ins[w], dst_ref=window(w, j), send_sem=send_sems.at[3 * w + k], recv_sem=recv_sems.at[3 * w + k],
                device_id=peer, device_id_type=MESH)

        local = [pltpu.make_async_copy(ins[w], window(w, chip), local_sems.at[w]) for w in range(nw)]
        for cp in local:
            cp.start()
        peers = _chip_peers(x, y, c)
        sends = [copy(w, k, chip, peer) for w in range(nw) for k, (_, peer) in enumerate(peers)]
        for cp in sends:
            cp.start()
        for w in range(nw):
            for k, (pchip, peer) in enumerate(peers):
                copy(w, k, pchip, peer).wait_recv()
        for cp in sends:
            cp.wait_send()
        for cp in local:
            cp.wait()

    return pl.pallas_call(
        body, name=name,
        out_shape=tuple(jax.ShapeDtypeStruct(full_shape(a, kd), a.dtype) for a, kd in zip(shards, kinds)),
        in_specs=[ANY] * nw, out_specs=(ANY,) * nw,
        scratch_shapes=[pltpu.SemaphoreType.DMA((3 * nw,)), pltpu.SemaphoreType.DMA((3 * nw,)),
                        pltpu.SemaphoreType.DMA((nw,))],
        compiler_params=_params(),
    )(*shards)


def _rs_to_sibling(grads, name):
    nw = len(grads)

    def body(*refs):
        ins, outs = refs[:nw], refs[nw:2 * nw]
        send_sems, recv_sems = refs[2 * nw:]
        x, y, c = _my_place()
        sibling = (x, y, 1 - c)
        copies = [pltpu.make_async_remote_copy(
            src_ref=ins[w].at[j, 1 - c], dst_ref=outs[w].at[j], send_sem=send_sems.at[N_CHIPS * w + j],
            recv_sem=recv_sems.at[N_CHIPS * w + j], device_id=sibling, device_id_type=MESH)
            for w in range(nw) for j in range(N_CHIPS)]
        for cp in copies:
            cp.start()
        for cp in copies:
            cp.wait_recv()
        for cp in copies:
            cp.wait_send()

    return pl.pallas_call(
        body, name=name,
        out_shape=tuple(jax.ShapeDtypeStruct((N_CHIPS,) + g.shape[2:], g.dtype) for g in grads),
        in_specs=[ANY] * nw, out_specs=(ANY,) * nw,
        scratch_shapes=[pltpu.SemaphoreType.DMA((N_CHIPS * nw,)), pltpu.SemaphoreType.DMA((N_CHIPS * nw,))],
        compiler_params=_params(),
    )(*grads)


def _rs_to_chips(parts, name):
    nw = len(parts)

    def body(*refs):
        ins, outs = refs[:nw], refs[nw:2 * nw]
        send_sems, recv_sems, local_sems = refs[2 * nw:]
        x, y, c = _my_place()
        chip = 2 * x + y
        peers = _chip_peers(x, y, c)

        def copy(w, k, src_slab, dst_slab, peer):
            return pltpu.make_async_remote_copy(
                src_ref=ins[w].at[src_slab], dst_ref=outs[w].at[dst_slab], send_sem=send_sems.at[3 * w + k],
                recv_sem=recv_sems.at[3 * w + k], device_id=peer, device_id_type=MESH)

        local = [pltpu.make_async_copy(ins[w].at[chip], outs[w].at[chip], local_sems.at[w]) for w in range(nw)]
        for cp in local:
            cp.start()
        sends = [copy(w, k, pchip, chip, peer) for w in range(nw) for k, (pchip, peer) in enumerate(peers)]
        for cp in sends:
            cp.start()
        for w in range(nw):
            for k, (pchip, peer) in enumerate(peers):
                copy(w, k, chip, pchip, peer).wait_recv()
        for cp in sends:
            cp.wait_send()
        for cp in local:
            cp.wait()

    return pl.pallas_call(
        body, name=name, out_shape=tuple(jax.ShapeDtypeStruct(p.shape, p.dtype) for p in parts),
        in_specs=[ANY] * nw, out_specs=(ANY,) * nw,
        scratch_shapes=[pltpu.SemaphoreType.DMA((3 * nw,)), pltpu.SemaphoreType.DMA((3 * nw,)),
                        pltpu.SemaphoreType.DMA((nw,))],
        compiler_params=_params(),
    )(*parts)


def _rs_share_halves(halves, name):
    nw = len(halves)
    nl = len(halves[0])
    flat = [h for hs in halves for h in hs]

    def body(*refs):
        ins, outs = refs[:nw * nl], refs[nw * nl:nw * nl + nw]
        send_sems, recv_sems, local_sems = refs[nw * nl + nw:]
        x, y, c = _my_place()
        sibling = (x, y, 1 - c)
        local, sends, recvs = [], [], []
        for w in range(nw):
            for l in range(nl):
                n = nl * w + l
                local.append(pltpu.make_async_copy(ins[n], outs[w].at[l, c], local_sems.at[n]))
                sends.append(pltpu.make_async_remote_copy(
                    src_ref=ins[n], dst_ref=outs[w].at[l, c], send_sem=send_sems.at[n], recv_sem=recv_sems.at[n],
                    device_id=sibling, device_id_type=MESH))
                recvs.append(pltpu.make_async_remote_copy(
                    src_ref=ins[n], dst_ref=outs[w].at[l, 1 - c], send_sem=send_sems.at[n], recv_sem=recv_sems.at[n],
                    device_id=sibling, device_id_type=MESH))
        for cp in local + sends:
            cp.start()
        for cp in recvs:
            cp.wait_recv()
        for cp in sends:
            cp.wait_send()
        for cp in local:
            cp.wait()

    return pl.pallas_call(
        body, name=name,
        out_shape=tuple(jax.ShapeDtypeStruct((nl, 2) + hs[0].shape, hs[0].dtype) for hs in halves),
        in_specs=[ANY] * (nw * nl), out_specs=(ANY,) * nw,
        scratch_shapes=[pltpu.SemaphoreType.DMA((nw * nl,)), pltpu.SemaphoreType.DMA((nw * nl,)),
                        pltpu.SemaphoreType.DMA((nw * nl,))],
        compiler_params=_params(),
    )(*flat)


def _add_own_half(grad, got, c_idx, name):
    _, _, r, n = grad.shape
    tr = _pick(r, 256, 8)

    def body(c_ref, g_ref, t_ref, o_ref):
        o_ref[...] = g_ref[...] + t_ref[...]

    return pl.pallas_call(
        body, name=name, out_shape=jax.ShapeDtypeStruct((N_CHIPS, r, n), F32),
        grid_spec=pltpu.PrefetchScalarGridSpec(
            num_scalar_prefetch=1, grid=(N_CHIPS, r // tr),
            in_specs=[pl.BlockSpec((None, None, tr, n), lambda j, i, c_ref: (j, c_ref[0], i, 0)),
                      pl.BlockSpec((None, tr, n), lambda j, i, c_ref: (j, i, 0))],
            out_specs=pl.BlockSpec((None, tr, n), lambda j, i, c_ref: (j, i, 0))),
        compiler_params=_params(),
    )(c_idx, grad, got)


def _sum_slabs(slabs, name):
    ns, r, n = slabs.shape
    tr = _pick(r, 256, 8)

    def body(s_ref, o_ref):
        acc = s_ref[0]
        for j in range(1, ns):
            acc = acc + s_ref[j]
        o_ref[...] = acc

    return pl.pallas_call(
        body, name=name, out_shape=jax.ShapeDtypeStruct((r, n), F32), grid=(r // tr,),
        in_specs=[pl.BlockSpec((ns, tr, n), lambda i: (0, i, 0))], out_specs=pl.BlockSpec((tr, n), lambda i: (i, 0)),
        compiler_params=_params(),
    )(slabs)


def _band_bias(rel_bias):
    h = rel_bias.shape[0]
    n_far = PAD - REL_CLIP + CHUNK
    far = jnp.broadcast_to(rel_bias[:, N_REL - 1:N_REL], (h, n_far))
    near = rel_bias[:, REL_CLIP - CHUNK + 1:N_REL - 1][:, ::-1]
    line = jnp.concatenate([far, near], axis=1)
    return jnp.stack([line[:, CHUNK - 1 - q:CHUNK - 1 - q + BAND] for q in range(CHUNK)], axis=1)


def _pack_rows(pieces):
    flat = jnp.concatenate([p.reshape(-1) for p in pieces])
    rows = -(-flat.shape[0] // (8 * LANES)) * 8
    return jnp.pad(flat, (0, rows * LANES - flat.shape[0])).reshape(rows, LANES)


def _unpack_rows(packed, shapes):
    flat = packed.reshape(-1)
    out, at = [], 0
    for shp in shapes:
        size = 1
        for n in shp:
            size *= n
        out.append(flat[at:at + size].reshape(shp))
        at += size
    return out


def _layer_fwd(x, mod, w, band, tag):
    s, d = x.shape
    row = lambda i: mod[i:i + 1]
    h1 = _norm_mod(x, row(1), row(0), f"norm_mix{tag}")
    proj = _matmul(h1, w["w_in"], form="nn", out_dtype=BF16, tm=_pick(s, 512), tn=_pick(w["w_in"].shape[1], 768),
                   tk=d, name=f"proj{tag}")
    oa = _attn_a_fwd(proj, band, f"attn_a{tag}")
    ob, ltot = _sb_fwd(proj, f"attn_b{tag}")
    cat = _out_norm(oa, ob, w["g_a"], w["g_b"], f"out_norm{tag}")
    mixed = _matmul(cat, w["w_out"], form="nn", out_dtype=F32, tm=_pick(s, 512), tn=_pick(d, 1024),
                    tk=cat.shape[1], name=f"mix_out{tag}")
    x1 = _residual(x, row(2), mixed, f"res_mix{tag}")
    h2 = _norm_mod(x1, row(4), row(3), f"norm_ffn{tag}")
    f2 = w["w_up"].shape[1]
    u = _matmul(h2, w["w_up"], form="nn", out_dtype=F32, tm=_pick(s, 512), tn=_pick(f2, 1408), tk=d, name=f"up{tag}")
    a = _conv_glu(u, w["conv_w"], w["conv_b"], f"conv_glu{tag}")
    f = _matmul(a, w["w_down"], form="nn", out_dtype=F32, tm=_pick(s, 512), tn=_pick(d, 1024),
                tk=_pick(f2 // 2, 2816), name=f"down{tag}")
    x2 = _residual(x1, row(5), f, f"res_ffn{tag}")
    saved = dict(x=x, h1=h1, proj=proj, oa=oa, ob=ob, ltot=ltot, cat=cat, mixed=mixed, x1=x1, h2=h2, u=u, a=a, f=f)
    return x2, saved


def _layer_bwd(dx2, sv, mod, w, band, tag):
    s, d = dx2.shape
    row = lambda i: mod[i:i + 1]
    f2 = w["w_up"].shape[1]
    ff = f2 // 2
    n_in = w["w_in"].shape[1]
    df, dgate_ffn = _gate_bwd(dx2, sv["f"], row(5), f"gate_ffn_bwd{tag}")
    da = _matmul(df, w["w_down"], form="nt", out_dtype=F32, tm=_pick(s, 512), tn=_pick(ff, 1408), tk=d, name=f"down_dx{tag}")
    g_down = _matmul(sv["a"], df, form="tn", out_dtype=F32, tm=_pick(ff, 1408), tn=_pick(d, 512), tk=_pick(s, 2048),
                     name=f"down_dw{tag}")
    du2, dcw, dcb = _conv_glu_bwd(sv["u"], da, w["conv_w"], w["conv_b"], f"conv_glu_bwd{tag}")
    dh2 = _matmul(du2, w["w_up"], form="nt", out_dtype=F32, tm=_pick(s, 512), tn=_pick(d, 1024), tk=_pick(ff, 2816),
                  name=f"up_dx{tag}", halves=True)
    g_up = _matmul(sv["h2"], du2, form="tn", out_dtype=F32, tm=_pick(d, 512), tn=_pick(f2 // N_CHIPS, 1408),
                   tk=_pick(s, 2048), name=f"up_dw{tag}", shard_cols=f2 // N_CHIPS, halves=True)
    dx1, dscale_ffn, dshift_ffn = _norm_mod_bwd(sv["x1"], dh2, dx2, row(4), f"norm_ffn_bwd{tag}")
    dmixed, dgate_mix = _gate_bwd(dx1, sv["mixed"], row(2), f"gate_mix_bwd{tag}")
    dcat = _matmul(dmixed, w["w_out"], form="nt", out_dtype=F32, tm=_pick(s, 512), tn=_pick(2 * W_GRP, 1024), tk=d,
                   name=f"mix_out_dx{tag}")
    g_out = _matmul(sv["cat"], dmixed, form="tn", out_dtype=F32, tm=_pick(2 * W_GRP, 512), tn=_pick(d, 1024),
                    tk=_pick(s, 2048), name=f"mix_out_dw{tag}")
    doa, dob, dg_a, dg_b = _out_norm_bwd(sv["oa"], sv["ob"], dcat, w["g_a"], w["g_b"], f"out_norm_bwd{tag}")
    dqa, dka, dva, dband = _attn_a_bwd(sv["proj"], band, doa, f"attn_a_bwd{tag}")
    dqb, dkb, dvb = _sb_bwd(sv["proj"], sv["ltot"], dob, f"attn_b_bwd{tag}")
    drel = _rel_bias_grad(jnp.transpose(dband, (1, 0, 2)), f"rel_bias_bwd{tag}")[:, :N_REL]
    dproj = jnp.concatenate([dqa, dka, dva, dqb, dkb, dvb], axis=1).astype(BF16)
    dh1 = _matmul(dproj, w["w_in"], form="nt", out_dtype=F32, tm=_pick(s, 512), tn=_pick(d, 1024), tk=_pick(n_in, 3072),
                  name=f"proj_dx{tag}")
    g_in = _matmul(sv["h1"], dproj, form="tn", out_dtype=F32, tm=_pick(d, 512), tn=_pick(n_in // N_CHIPS, 768),
                   tk=_pick(s, 2048), name=f"proj_dw{tag}", shard_cols=n_in // N_CHIPS)
    dx, dscale_mix, dshift_mix = _norm_mod_bwd(sv["x"], dh1, dx1, row(1), f"norm_mix_bwd{tag}")
    dmod = jnp.concatenate([dshift_mix, dscale_mix, dgate_mix, dshift_ffn, dscale_ffn, dgate_ffn], axis=1)
    big = dict(w_in=g_in, w_out=g_out, w_up=g_up, w_down=g_down)
    dconv_w = jnp.concatenate([dcw[0], dcw[1]], axis=1)
    dconv_b = jnp.concatenate([dcb[0], dcb[1]], axis=1)
    small = dict(dmod=dmod, rel_bias=drel, g_a=dg_a, g_b=dg_b, conv_w=dconv_w, conv_b=dconv_b)
    return dx, big, small


def _kernel_unoverlapped(x, c, w_ada, b_ada, w_in, rel_bias, g_a, g_b, w_out, w_up, conv_w, conv_b, w_down, final_g, loss_target, m_w_ada, m_b_ada, m_w_in, m_rel_bias, m_g_a, m_g_b, m_w_out, m_w_up, m_conv_w, m_conv_b, m_w_down, m_final_g, v_w_ada, v_b_ada, v_w_in, v_rel_bias, v_g_a, v_g_b, v_w_out, v_w_up, v_conv_w, v_conv_b, v_w_down, v_final_g):
    xi, yi, ci = _my_place()
    chip = 2 * xi + yi
    dev = 4 * xi + 2 * yi + ci
    nl, d, n_ada = w_ada.shape
    s = x.shape[1]
    f2 = N_CHIPS * w_up.shape[2]
    nc = conv_w.shape[2]

    c_pad = jnp.pad(c, ((0, 7), (0, 0)))
    c_all = _allgather8(c_pad, "gather_c")[0::8]
    b_mine = lax.dynamic_slice_in_dim(b_ada, chip * n_ada, n_ada, axis=1)[:, None, :]
    mod_shard = _ada_fwd(c_all, w_ada, b_mine, "ada")
    pack2 = _pack_rows([mod_shard, conv_w])
    got2 = _allgather8(pack2, "gather_mod").reshape(N_DEV, -1)
    mods, convs = [], []
    for j in range(N_CHIPS):
        ms, cw = _unpack_rows(got2[2 * j], [mod_shard.shape, conv_w.shape])
        mods.append(lax.dynamic_index_in_dim(ms, dev, axis=1, keepdims=False))
        convs.append(cw)
    mod = jnp.concatenate(mods, axis=1).reshape(nl, 6, d)
    conv_w_full = jnp.concatenate(convs, axis=2)

    names = ("w_in", "w_out", "w_up", "w_down")
    kinds = ("col", "row", "col", "row")
    shards = dict(w_in=w_in, w_out=w_out, w_up=w_up, w_down=w_down)
    full = _gather_weights([shards[n].astype(BF16) for n in names], kinds, "gather_weights")
    full = dict(zip(names, full))

    xs = x[0]
    layers, saved, bands = [], [], []
    for l in range(nl):
        w = {n: full[n][l] for n in names}
        w.update(g_a=g_a[l:l + 1], g_b=g_b[l:l + 1], conv_w=conv_w_full[l], conv_b=conv_b[l:l + 1])
        band = _band_bias(rel_bias[l])
        xs, sv = _layer_fwd(xs, mod[l], w, band, f"_l{l}")
        layers.append(w)
        bands.append(band)
        saved.append(sv)
    loss_part, dx, dfinal_g = _loss_head(xs, final_g[None, :], loss_target[0], "loss_head")
    loss = lax.psum(loss_part[0, 0], ("x", "y", "c"))

    big, small = [None] * nl, [None] * nl
    for l in reversed(range(nl)):
        dx, big[l], small[l] = _layer_bwd(dx, saved[l], mod[l], layers[l], bands[l], f"_l{l}")

    small_names = ("dmod", "rel_bias", "g_a", "g_b", "conv_w", "conv_b")
    pieces = [small[l][n] for l in range(nl) for n in small_names] + [dfinal_g]
    shapes = [p.shape for p in pieces]
    pack3 = _pack_rows(pieces)
    got3 = _allgather8(pack3, "gather_small").reshape(N_DEV, pack3.shape[0], LANES)
    summed = _unpack_rows(_sum_slabs(got3, "sum_small"), shapes)
    tot = [dict(zip(small_names, summed[len(small_names) * l:len(small_names) * (l + 1)])) for l in range(nl)]
    g_final_g = summed[-1].reshape(-1)
    g_b_ada = jnp.stack([tot[l]["dmod"].reshape(-1) for l in range(nl)])
    g_rel = jnp.stack([tot[l]["rel_bias"] for l in range(nl)])
    g_ga = jnp.stack([tot[l]["g_a"].reshape(-1) for l in range(nl)])
    g_gb = jnp.stack([tot[l]["g_b"].reshape(-1) for l in range(nl)])
    g_conv_b = jnp.stack([tot[l]["conv_b"].reshape(-1) for l in range(nl)])
    g_conv_w = jnp.stack([lax.dynamic_slice_in_dim(tot[l]["conv_w"], chip * nc, nc, axis=1) for l in range(nl)])
    per_dev = [_unpack_rows(got3[j], shapes) for j in range(N_DEV)]
    dmod_all = jnp.stack([jnp.stack([per_dev[j][len(small_names) * l].reshape(-1) for j in range(N_DEV)])
                          for l in range(nl)])
    g_w_ada = _ada_bwd(c_all, lax.dynamic_slice_in_dim(dmod_all, chip * n_ada, n_ada, axis=2), "ada_bwd")

    order = [(n, l) for n in names for l in range(nl)]
    flat_g = [big[l][n].reshape(N_CHIPS, 2, -1, 1024) for n, l in order]
    from_sib = _rs_to_sibling(flat_g, "rs_sibling")
    c_idx = jnp.reshape(ci, (1,)).astype(jnp.int32)
    chip_part = [_add_own_half(g, t, c_idx, f"rs_add_{n}_l{l}") for g, t, (n, l) in zip(flat_g, from_sib, order)]
    from_chips = _rs_to_chips(chip_part, "rs_chips")
    my_half = [_sum_slabs(t, f"rs_sum_{n}_l{l}") for t, (n, l) in zip(from_chips, order)]
    shard_g = _rs_share_halves([[my_half[nl * i + l] for l in range(nl)] for i in range(len(names))], "rs_halves")
    g_big = {n: shard_g[i].reshape(shards[n].shape) for i, n in enumerate(names)}

    grads = dict(w_ada=g_w_ada, b_ada=g_b_ada, rel_bias=g_rel, g_a=g_ga, g_b=g_gb, conv_w=g_conv_w, conv_b=g_conv_b,
                 final_g=g_final_g)
    weights = dict(w_ada=w_ada, b_ada=b_ada, w_in=w_in, rel_bias=rel_bias, g_a=g_a, g_b=g_b, w_out=w_out, w_up=w_up,
                   conv_w=conv_w, conv_b=conv_b, w_down=w_down, final_g=final_g)
    m_in = dict(w_ada=m_w_ada, b_ada=m_b_ada, w_in=m_w_in, rel_bias=m_rel_bias, g_a=m_g_a, g_b=m_g_b, w_out=m_w_out,
                w_up=m_w_up, conv_w=m_conv_w, conv_b=m_conv_b, w_down=m_w_down, final_g=m_final_g)
    v_in = dict(w_ada=v_w_ada, b_ada=v_b_ada, w_in=v_w_in, rel_bias=v_rel_bias, g_a=v_g_a, g_b=v_g_b, w_out=v_w_out,
                w_up=v_w_up, conv_w=v_conv_w, conv_b=v_conv_b, w_down=v_w_down, final_g=v_final_g)
    order_w = ("w_ada", "b_ada", "w_in", "rel_bias", "g_a", "g_b", "w_out", "w_up", "conv_w", "conv_b", "w_down", "final_g")
    upd = {n: _adamw_nd(grads[n], weights[n], m_in[n], v_in[n], f"adamw_{n}") for n in grads}
    for n, mn, th in zip(BIG, mine, theirs):
        grads[n], *upd[n] = _adamw_halves(mn, th, c_idx, weights[n], m_in[n], v_in[n], f"adamw_{n}")
    return (loss, dx[None], *[grads[n] for n in order_w], *[upd[n][0] for n in order_w],
            *[upd[n][1] for n in order_w], *[upd[n][2] for n in order_w])


class _Rider:
    def __init__(self, ins, out_shapes, n_remote, n_local, parts):
        self.ins = list(ins)
        self.out_shapes = list(out_shapes)
        self.scratch = [pltpu.SemaphoreType.DMA((n_remote,)), pltpu.SemaphoreType.DMA((n_remote,)),
                        pltpu.SemaphoreType.DMA((max(n_local, 1),))]
        self.parts = parts

    def start(self, in_refs, out_refs, sems):
        local, sends, _ = self.parts(in_refs, out_refs, sems)
        for cp in local() + sends():
            cp.start()

    def wait(self, in_refs, out_refs, sems):
        local, sends, recvs = self.parts(in_refs, out_refs, sems)
        for cp in recvs():
            cp.wait_recv()
        for cp in sends():
            cp.wait_send()
        for cp in local():
            cp.wait()


def _call_with_rider(body, rider, *, name, grid, out_shape, in_specs, out_specs, scratch_shapes, args):
    if rider is None:
        return pl.pallas_call(body, name=name, grid=grid, out_shape=tuple(out_shape), in_specs=list(in_specs),
                              out_specs=tuple(out_specs), scratch_shapes=list(scratch_shapes),
                              compiler_params=_params())(*args)
    n_in, n_out, n_scr = len(in_specs), len(out_specs), len(scratch_shapes)
    r_in, r_out = len(rider.ins), len(rider.out_shapes)

    def both(*refs):
        at = 0
        groups = []
        for size in (n_in, r_in, n_out, r_out, n_scr, len(rider.scratch)):
            groups.append(refs[at:at + size])
            at += size
        own_in, ride_in, own_out, ride_out, own_scr, sems = groups
        steps = [pl.program_id(a) for a in range(len(grid))]
        first = functools.reduce(jnp.logical_and, [st == 0 for st in steps])
        last = functools.reduce(jnp.logical_and, [st == g - 1 for st, g in zip(steps, grid)])

        @pl.when(first)
        def _():
            rider.start(ride_in, ride_out, sems)

        body(*own_in, *own_out, *own_scr)

        @pl.when(last)
        def _():
            rider.wait(ride_in, ride_out, sems)

    outs = pl.pallas_call(
        both, name=name, grid=grid, out_shape=tuple(out_shape) + tuple(rider.out_shapes),
        in_specs=list(in_specs) + [ANY] * r_in, out_specs=tuple(out_specs) + (ANY,) * r_out,
        scratch_shapes=list(scratch_shapes) + rider.scratch, compiler_params=_params(),
    )(*args, *rider.ins)
    return tuple(outs[:n_out]) + (list(outs[n_out:]),)


def _run_rider(rider, name):
    r_in, r_out = len(rider.ins), len(rider.out_shapes)

    def body(*refs):
        ins, outs, sems = refs[:r_in], refs[r_in:r_in + r_out], refs[r_in + r_out:]
        rider.start(ins, outs, sems)
        rider.wait(ins, outs, sems)

    return list(pl.pallas_call(
        body, name=name, out_shape=tuple(rider.out_shapes), in_specs=[ANY] * r_in, out_specs=(ANY,) * r_out,
        scratch_shapes=rider.scratch, compiler_params=_params(),
    )(*rider.ins))


def _remote(src, dst, sems, n, peer):
    return pltpu.make_async_remote_copy(src_ref=src, dst_ref=dst, send_sem=sems[0].at[n], recv_sem=sems[1].at[n],
                                        device_id=peer, device_id_type=MESH)


def _gather_rider(shards, kinds):
    nw = len(shards)
    out_shapes = [jax.ShapeDtypeStruct((a.shape[0], N_CHIPS * a.shape[1]) if kd == "col" else
                                       (N_CHIPS * a.shape[0], a.shape[1]), a.dtype) for a, kd in zip(shards, kinds)]

    def parts(ins, outs, sems):
        x, y, c = _my_place()
        chip = 2 * x + y
        peers = _chip_peers(x, y, c)

        def window(w, j):
            r, n = shards[w].shape
            if kinds[w] == "col":
                return outs[w].at[:, pl.ds(pl.multiple_of(j * n, LANES), n)]
            return outs[w].at[pl.ds(pl.multiple_of(j * r, 16), r), :]

        local = lambda: [pltpu.make_async_copy(ins[w], window(w, chip), sems[2].at[w]) for w in range(nw)]
        sends = lambda: [_remote(ins[w], window(w, chip), sems, 3 * w + k, peer)
                         for w in range(nw) for k, (_, peer) in enumerate(peers)]
        recvs = lambda: [_remote(ins[w], window(w, pchip), sems, 3 * w + k, peer)
                         for w in range(nw) for k, (pchip, peer) in enumerate(peers)]
        return local, sends, recvs

    return _Rider(shards, out_shapes, 3 * nw, nw, parts)


def _half(ref3, h, rows):
    return ref3.at[:, pl.ds(pl.multiple_of(h * rows, 8), rows), :]


def _sibling_rider(grads):
    nw = len(grads)
    out_shapes = [jax.ShapeDtypeStruct((g.shape[0], g.shape[1] // 2, g.shape[2]), g.dtype) for g in grads]

    def parts(ins, outs, sems):
        x, y, c = _my_place()
        sibling = (x, y, 1 - c)
        copies = lambda: [_remote(_half(ins[w], 1 - c, grads[w].shape[1] // 2), outs[w], sems, w, sibling)
                          for w in range(nw)]
        return (lambda: []), copies, copies

    return _Rider(grads, out_shapes, nw, 0, parts)


def _chips_rider(parts_in):
    nw = len(parts_in)
    out_shapes = [jax.ShapeDtypeStruct(p.shape, p.dtype) for p in parts_in]

    def parts(ins, outs, sems):
        x, y, c = _my_place()
        chip = 2 * x + y
        peers = _chip_peers(x, y, c)
        local = lambda: [pltpu.make_async_copy(ins[w].at[chip], outs[w].at[chip], sems[2].at[w]) for w in range(nw)]
        sends = lambda: [_remote(ins[w].at[pchip], outs[w].at[chip], sems, 3 * w + k, peer)
                         for w in range(nw) for k, (pchip, peer) in enumerate(peers)]
        recvs = lambda: [_remote(ins[w].at[chip], outs[w].at[pchip], sems, 3 * w + k, peer)
                         for w in range(nw) for k, (pchip, peer) in enumerate(peers)]
        return local, sends, recvs

    return _Rider(parts_in, out_shapes, 3 * nw, nw, parts)


def _halves_rider(halves):
    nw, nl = len(halves), len(halves[0])
    flat = [h for hs in halves for h in hs]
    out_shapes = [jax.ShapeDtypeStruct((nl, 2 * hs[0].shape[0], hs[0].shape[1]), hs[0].dtype) for hs in halves]

    def parts(ins, outs, sems):
        x, y, c = _my_place()
        sibling = (x, y, 1 - c)

        def window(w, l, h):
            rows = halves[w][0].shape[0]
            return outs[w].at[l, pl.ds(pl.multiple_of(h * rows, 8), rows), :]

        pairs = [(w, l) for w in range(nw) for l in range(nl)]
        local = lambda: [pltpu.make_async_copy(ins[nl * w + l], window(w, l, c), sems[2].at[nl * w + l]) for w, l in pairs]
        sends = lambda: [_remote(ins[nl * w + l], window(w, l, c), sems, nl * w + l, sibling) for w, l in pairs]
        recvs = lambda: [_remote(ins[nl * w + l], window(w, l, 1 - c), sems, nl * w + l, sibling) for w, l in pairs]
        return local, sends, recvs

    return _Rider(flat, out_shapes, nw * nl, nw * nl, parts)


def _add_my_half(grad, got, c_idx, name):
    _, r, n = got.shape
    tr = _pick(r, 256, 8)
    nblk = r // tr

    def body(c_ref, g_ref, t_ref, o_ref):
        o_ref[...] = g_ref[...] + t_ref[...]

    return pl.pallas_call(
        body, name=name, out_shape=jax.ShapeDtypeStruct(got.shape, F32),
        grid_spec=pltpu.PrefetchScalarGridSpec(
            num_scalar_prefetch=1, grid=(N_CHIPS, nblk),
            in_specs=[pl.BlockSpec((None, tr, n), lambda j, i, c_ref: (j, c_ref[0] * nblk + i, 0)),
                      pl.BlockSpec((None, tr, n), lambda j, i, c_ref: (j, i, 0))],
            out_specs=pl.BlockSpec((None, tr, n), lambda j, i, c_ref: (j, i, 0))),
        compiler_params=_params(),
    )(c_idx, grad, got)


def _swap_rider(mine):
    nw = len(mine)
    out_shapes = [jax.ShapeDtypeStruct(a.shape, a.dtype) for a in mine]

    def parts(ins, outs, sems):
        x, y, c = _my_place()
        copies = lambda: [_remote(ins[w], outs[w], sems, w, (x, y, 1 - c)) for w in range(nw)]
        return (lambda: []), copies, copies

    return _Rider(mine, out_shapes, nw, 0, parts)


def _sum_layers(slabs, name):
    nl = len(slabs)
    ns, r, n = slabs[0].shape
    tr = _pick(r, 256, 8)

    def body(*refs):
        o_ref = refs[nl]
        for l in range(nl):
            acc = refs[l][0]
            for j in range(1, ns):
                acc = acc + refs[l][j]
            o_ref[l] = acc

    return pl.pallas_call(
        body, name=name, out_shape=jax.ShapeDtypeStruct((nl, r, n), F32), grid=(r // tr,),
        in_specs=[pl.BlockSpec((ns, tr, n), lambda i: (0, i, 0))] * nl,
        out_specs=pl.BlockSpec((nl, tr, n), lambda i: (0, i, 0)), compiler_params=_params(),
    )(*slabs)


def _adam_math(gv, w, m, v):
    c1 = 1.0 - ADAM_B1 ** ADAM_STEP
    c2 = 1.0 - ADAM_B2 ** ADAM_STEP
    nm = ADAM_B1 * m + (1.0 - ADAM_B1) * gv
    nv = ADAM_B2 * v + (1.0 - ADAM_B2) * (gv * gv)
    return -ADAM_LR * ((nm / c1) / (jnp.sqrt(nv / c2) + ADAM_EPS) + ADAM_WD * w), nm, nv


def _adamw_halves(mine, theirs, c_idx, w, m, v, name):
    nl, r, n = mine.shape
    tr = _pick(r, 256, 8)
    nblk = r // tr

    def body(c_ref, mine_ref, theirs_ref, w_ref, m_ref, v_ref, g_ref, d_ref, nm_ref, nv_ref):
        gv = jnp.where(pl.program_id(1) == c_ref[0], mine_ref[...], theirs_ref[...])
        g_ref[...] = gv
        d_ref[...], nm_ref[...], nv_ref[...] = _adam_math(gv, w_ref[...], m_ref[...], v_ref[...])

    half = pl.BlockSpec((None, tr, n), lambda l, h, i, c_ref: (l, i, 0))
    full = pl.BlockSpec((None, tr, n), lambda l, h, i, c_ref: (l, h * nblk + i, 0))
    return pl.pallas_call(
        body, name=name, out_shape=(jax.ShapeDtypeStruct(w.shape, F32),) * 4,
        grid_spec=pltpu.PrefetchScalarGridSpec(
            num_scalar_prefetch=1, grid=(nl, 2, nblk), in_specs=[half, half, full, full, full],
            out_specs=(full,) * 4),
        compiler_params=_params(),
    )(c_idx, mine, theirs, w, m, v)


def _by_chip(g):
    return g if g.ndim == 3 else g.reshape(N_CHIPS, g.shape[0] // N_CHIPS, g.shape[1])


BIG = ("w_in", "w_out", "w_up", "w_down")
BIG_KIND = dict(w_in="col", w_out="row", w_up="col", w_down="row")


def _forward_layer(x, mod, w, band, tag, rider=None, on_arrival=None):
    s, d = x.shape
    row = lambda i: mod[i:i + 1]
    h1 = _norm_mod(x, row(1), row(0), f"norm_mix{tag}")
    proj = _matmul(h1, w["w_in"], form="nn", out_dtype=BF16, tm=_pick(s, 512), tn=_pick(w["w_in"].shape[1], 768),
                   tk=d, name=f"proj{tag}")
    oa = _attn_a_fwd(proj, band, f"attn_a{tag}")
    if rider is None:
        ob, ltot = _sb_fwd(proj, f"attn_b{tag}")
    else:
        ob, ltot, arrived = _sb_fwd(proj, f"attn_b{tag}", rider)
        on_arrival(arrived)
    cat = _out_norm(oa, ob, w["g_a"], w["g_b"], f"out_norm{tag}")
    mixed = _matmul(cat, w["w_out"], form="nn", out_dtype=F32, tm=_pick(s, 512), tn=_pick(d, 1024),
                    tk=cat.shape[1], name=f"mix_out{tag}")
    x1 = _residual(x, row(2), mixed, f"res_mix{tag}")
    h2 = _norm_mod(x1, row(4), row(3), f"norm_ffn{tag}")
    f2 = w["w_up"].shape[1]
    u = _matmul(h2, w["w_up"], form="nn", out_dtype=F32, tm=_pick(s, 512), tn=_pick(f2, 1408), tk=d, name=f"up{tag}")
    a = _conv_glu(u, w["conv_w"], w["conv_b"], f"conv_glu{tag}")
    f = _matmul(a, w["w_down"], form="nn", out_dtype=F32, tm=_pick(s, 512), tn=_pick(d, 1024),
                tk=_pick(f2 // 2, 2816), name=f"down{tag}")
    x2 = _residual(x1, row(5), f, f"res_ffn{tag}")
    saved = dict(x=x, h1=h1, proj=proj, oa=oa, ob=ob, ltot=ltot, cat=cat, mixed=mixed, x1=x1, h2=h2, u=u, a=a, f=f)
    return x2, saved


def _backward_layer(dx2, sv, mod, w, band, tag, c_idx, waiting=None):
    s, d = dx2.shape
    row = lambda i: mod[i:i + 1]
    f2 = w["w_up"].shape[1]
    ff = f2 // 2
    n_in = w["w_in"].shape[1]
    df, dgate_ffn = _gate_bwd(dx2, sv["f"], row(5), f"gate_ffn_bwd{tag}")
    da = _matmul(df, w["w_down"], form="nt", out_dtype=F32, tm=_pick(s, 512), tn=_pick(ff, 1408), tk=d, name=f"down_dx{tag}")
    g_down = _matmul(sv["a"], df, form="tn", out_dtype=F32, tm=_pick(ff, 1408), tn=_pick(d, 512), tk=_pick(s, 2048),
                     name=f"down_dw{tag}")
    du2, dcw, dcb = _conv_glu_bwd(sv["u"], da, w["conv_w"], w["conv_b"], f"conv_glu_bwd{tag}")
    dh2 = _matmul(du2, w["w_up"], form="nt", out_dtype=F32, tm=_pick(s, 512), tn=_pick(d, 1024), tk=_pick(ff, 2816),
                  name=f"up_dx{tag}", halves=True)
    g_up = _matmul(sv["h2"], du2, form="tn", out_dtype=F32, tm=_pick(d, 512), tn=_pick(f2 // N_CHIPS, 1408),
                   tk=_pick(s, 2048), name=f"up_dw{tag}", shard_cols=f2 // N_CHIPS, halves=True)
    dx1, dscale_ffn, dshift_ffn = _norm_mod_bwd(sv["x1"], dh2, dx2, row(4), f"norm_ffn_bwd{tag}")
    dmixed, dgate_mix = _gate_bwd(dx1, sv["mixed"], row(2), f"gate_mix_bwd{tag}")
    dcat = _matmul(dmixed, w["w_out"], form="nt", out_dtype=F32, tm=_pick(s, 512), tn=_pick(2 * W_GRP, 1024), tk=d,
                   name=f"mix_out_dx{tag}")
    g_out = _matmul(sv["cat"], dmixed, form="tn", out_dtype=F32, tm=_pick(2 * W_GRP, 512), tn=_pick(d, 1024),
                    tk=_pick(s, 2048), name=f"mix_out_dw{tag}")
    doa, dob, dg_a, dg_b = _out_norm_bwd(sv["oa"], sv["ob"], dcat, w["g_a"], w["g_b"], f"out_norm_bwd{tag}")
    big = {("w_down", tag): _by_chip(g_down), ("w_up", tag): _by_chip(g_up), ("w_out", tag): _by_chip(g_out)}
    if waiting is None:
        dqa, dka, dva, dband = _attn_a_bwd(sv["proj"], band, doa, f"attn_a_bwd{tag}")
        dqb, dkb, dvb = _sb_bwd(sv["proj"], sv["ltot"], dob, f"attn_b_bwd{tag}")
    else:
        raw = {**waiting, **big}
        keys = list(raw)
        dqa, dka, dva, dband, from_sib = _attn_a_bwd(sv["proj"], band, doa, f"attn_a_bwd{tag}",
                                                     _sibling_rider([raw[k] for k in keys]))
        chip_part = [_add_my_half(raw[k], t, c_idx, f"rs_add_{k[0]}{k[1]}") for k, t in zip(keys, from_sib)]
        dqb, dkb, dvb, from_chips = _sb_bwd(sv["proj"], sv["ltot"], dob, f"attn_b_bwd{tag}", _chips_rider(chip_part))
        big = dict(zip(keys, from_chips))
    drel = _rel_bias_grad(jnp.transpose(dband, (1, 0, 2)), f"rel_bias_bwd{tag}")[:, :N_REL]
    dproj = jnp.concatenate([dqa, dka, dva, dqb, dkb, dvb], axis=1).astype(BF16)
    dh1 = _matmul(dproj, w["w_in"], form="nt", out_dtype=F32, tm=_pick(s, 512), tn=_pick(d, 1024), tk=_pick(n_in, 3072),
                  name=f"proj_dx{tag}")
    g_in = _matmul(sv["h1"], dproj, form="tn", out_dtype=F32, tm=_pick(d, 512), tn=_pick(n_in // N_CHIPS, 768),
                   tk=_pick(s, 2048), name=f"proj_dw{tag}", shard_cols=n_in // N_CHIPS)
    dx, dscale_mix, dshift_mix = _norm_mod_bwd(sv["x"], dh1, dx1, row(1), f"norm_mix_bwd{tag}")
    dmod = jnp.concatenate([dshift_mix, dscale_mix, dgate_mix, dshift_ffn, dscale_ffn, dgate_ffn], axis=1)
    dconv_w = jnp.concatenate([dcw[0], dcw[1]], axis=1)
    dconv_b = jnp.concatenate([dcb[0], dcb[1]], axis=1)
    small = dict(dmod=dmod, rel_bias=drel, g_a=dg_a, g_b=dg_b, conv_w=dconv_w, conv_b=dconv_b)
    return dx, big, g_in, small


def kernel(x, c, w_ada, b_ada, w_in, rel_bias, g_a, g_b, w_out, w_up, conv_w, conv_b, w_down, final_g, loss_target, m_w_ada, m_b_ada, m_w_in, m_rel_bias, m_g_a, m_g_b, m_w_out, m_w_up, m_conv_w, m_conv_b, m_w_down, m_final_g, v_w_ada, v_b_ada, v_w_in, v_rel_bias, v_g_a, v_g_b, v_w_out, v_w_up, v_conv_w, v_conv_b, v_w_down, v_final_g):
    xi, yi, ci = _my_place()
    chip = 2 * xi + yi
    dev = 4 * xi + 2 * yi + ci
    c_idx = jnp.reshape(ci, (1,)).astype(jnp.int32)
    nl, d, n_ada = w_ada.shape
    nc = conv_w.shape[2]
    assert nl == 2

    c_pad = jnp.pad(c, ((0, 7), (0, 0)))
    c_all = _allgather8(c_pad, "gather_c")[0::8]
    b_mine = lax.dynamic_slice_in_dim(b_ada, chip * n_ada, n_ada, axis=1)[:, None, :]
    mod_shard = _ada_fwd(c_all, w_ada, b_mine, "ada")
    pack2 = _pack_rows([mod_shard, conv_w])
    got2 = _allgather8(pack2, "gather_mod").reshape(N_DEV, -1)
    mods, convs = [], []
    for j in range(N_CHIPS):
        ms, cw = _unpack_rows(got2[2 * j], [mod_shard.shape, conv_w.shape])
        mods.append(lax.dynamic_index_in_dim(ms, dev, axis=1, keepdims=False))
        convs.append(cw)
    mod = jnp.concatenate(mods, axis=1).reshape(nl, 6, d)
    conv_w_full = jnp.concatenate(convs, axis=2)

    shards = dict(w_in=w_in, w_out=w_out, w_up=w_up, w_down=w_down)
    sh = {(n, l): shards[n][l].astype(BF16) for n in BIG for l in range(nl)}
    early = [("w_in", 0)]
    riding = [[("w_out", 0), ("w_up", 0), ("w_down", 0), ("w_in", 1)], [("w_out", 1), ("w_up", 1), ("w_down", 1)]]
    layers = [dict(g_a=g_a[l:l + 1], g_b=g_b[l:l + 1], conv_w=conv_w_full[l], conv_b=conv_b[l:l + 1]) for l in range(nl)]

    def gather_rider(keys):
        return _gather_rider([sh[k] for k in keys], [BIG_KIND[k[0]] for k in keys])

    def arrival(keys):
        def fill(arrived):
            for (n, l), full in zip(keys, arrived):
                layers[l][n] = full
        return fill

    arrival(early)(_run_rider(gather_rider(early), "gather_first"))

    xs = x[0]
    saved, bands = [], []
    for l in range(nl):
        band = _band_bias(rel_bias[l])
        xs, sv = _forward_layer(xs, mod[l], layers[l], band, f"_l{l}", gather_rider(riding[l]), arrival(riding[l]))
        bands.append(band)
        saved.append(sv)
    loss_part, dx, dfinal_g = _loss_head(xs, final_g[None, :], loss_target[0], "loss_head")
    loss = lax.psum(loss_part[0, 0], ("x", "y", "c"))

    small = [None] * nl
    dx, raw1, g_in1, small[1] = _backward_layer(dx, saved[1], mod[1], layers[1], bands[1], "_l1", c_idx)
    raw1[("w_in", "_l1")] = g_in1
    dx, parts, g_in0, small[0] = _backward_layer(dx, saved[0], mod[0], layers[0], bands[0], "_l0", c_idx, raw1)
    (from_sib,) = _run_rider(_sibling_rider([g_in0]), "rs_sibling_last")
    (parts[("w_in", "_l0")],) = _run_rider(
        _chips_rider([_add_my_half(g_in0, from_sib, c_idx, "rs_add_w_in_l0")]), "rs_chips_last")
    mine = [_sum_layers([parts[(n, f"_l{l}")] for l in range(nl)], f"rs_sum_{n}") for n in BIG]
    theirs = _run_rider(_swap_rider(mine), "rs_swap")

    small_names = ("dmod", "rel_bias", "g_a", "g_b", "conv_w", "conv_b")
    pieces = [small[l][n] for l in range(nl) for n in small_names] + [dfinal_g]
    shapes = [p.shape for p in pieces]
    pack3 = _pack_rows(pieces)
    got3 = _allgather8(pack3, "gather_small").reshape(N_DEV, pack3.shape[0], LANES)
    summed = _unpack_rows(_sum_slabs(got3, "sum_small"), shapes)
    tot = [dict(zip(small_names, summed[len(small_names) * l:len(small_names) * (l + 1)])) for l in range(nl)]
    g_final_g = summed[-1].reshape(-1)
    g_b_ada = jnp.stack([tot[l]["dmod"].reshape(-1) for l in range(nl)])
    g_rel = jnp.stack([tot[l]["rel_bias"] for l in range(nl)])
    g_ga = jnp.stack([tot[l]["g_a"].reshape(-1) for l in range(nl)])
    g_gb = jnp.stack([tot[l]["g_b"].reshape(-1) for l in range(nl)])
    g_conv_b = jnp.stack([tot[l]["conv_b"].reshape(-1) for l in range(nl)])
    g_conv_w = jnp.stack([lax.dynamic_slice_in_dim(tot[l]["conv_w"], chip * nc, nc, axis=1) for l in range(nl)])
    per_dev = [_unpack_rows(got3[j], shapes) for j in range(N_DEV)]
    dmod_all = jnp.stack([jnp.stack([per_dev[j][len(small_names) * l].reshape(-1) for j in range(N_DEV)])
                          for l in range(nl)])
    g_w_ada = _ada_bwd(c_all, lax.dynamic_slice_in_dim(dmod_all, chip * n_ada, n_ada, axis=2), "ada_bwd")

    grads = dict(w_ada=g_w_ada, b_ada=g_b_ada, rel_bias=g_rel, g_a=g_ga, g_b=g_gb, conv_w=g_conv_w, conv_b=g_conv_b,
                 final_g=g_final_g)
    weights = dict(w_ada=w_ada, b_ada=b_ada, w_in=w_in, rel_bias=rel_bias, g_a=g_a, g_b=g_b, w_out=w_out, w_up=w_up,
                   conv_w=conv_w, conv_b=conv_b, w_down=w_down, final_g=final_g)
    m_in = dict(w_ada=m_w_ada, b_ada=m_b_ada, w_in=m_w_in, rel_bias=m_rel_bias, g_a=m_g_a, g_b=m_g_b, w_out=m_w_out,
                w_up=m_w_up, conv_w=m_conv_w, conv_b=m_conv_b, w_down=m_w_down, final_g=m_final_g)
    v_in = dict(w_ada=v_w_ada, b_ada=v_b_ada, w_in=v_w_in, rel_bias=v_rel_bias, g_a=v_g_a, g_b=v_g_b, w_out=v_w_out,
                w_up=v_w_up, conv_w=v_conv_w, conv_b=v_conv_b, w_down=v_w_down, final_g=v_final_g)
    order_w = ("w_ada", "b_ada", "w_in", "rel_bias", "g_a", "g_b", "w_out", "w_up", "conv_w", "conv_b", "w_down", "final_g")
    upd = {n: _adamw_nd(grads[n], weights[n], m_in[n], v_in[n], f"adamw_{n}") for n in grads}
    for n, mn, th in zip(BIG, mine, theirs):
        grads[n], *upd[n] = _adamw_halves(mn, th, c_idx, weights[n], m_in[n], v_in[n], f"adamw_{n}")
    return (loss, dx[None], *[grads[n] for n in order_w], *[upd[n][0] for n in order_w],
            *[upd[n][1] for n in order_w], *[upd[n][2] for n in order_w])
```

```python
import functools

import jax
import jax.numpy as jnp
from jax import lax
from jax.experimental import pallas as pl
from jax.experimental.pallas import tpu as pltpu

F32 = jnp.float32
BF16 = jnp.bfloat16
MESH = pl.DeviceIdType.MESH
ANY = pl.BlockSpec(memory_space=pl.ANY)
VMEM_FULL = pl.BlockSpec(memory_space=pltpu.VMEM)

HEAD_DIM = 64
N_HEADS = 8
W_GRP = N_HEADS * HEAD_DIM
CHUNK = 64
N_PREV = 8
BAND = (N_PREV + 1) * CHUNK
PAD = N_PREV * CHUNK
REL_CLIP = 128
N_REL = 2 * REL_CLIP + 1
EPS = 1e-6
SB_DEAD = -110.0
N_CHIPS = 4
N_DEV = 8
LANES = 128
V7X_VMEM_LIMIT = 56 * 1024 * 1024

ADAM_LR = 0.001
ADAM_B1 = 0.9
ADAM_B2 = 0.999
ADAM_EPS = 1e-08
ADAM_WD = 0.01
ADAM_STEP = 10


def _params(**kw):
    return pltpu.CompilerParams(vmem_limit_bytes=V7X_VMEM_LIMIT, **kw)


def _pick(dim, pref, mult=LANES):
    t = (min(pref, dim) // mult) * mult
    while t >= mult:
        if dim % t == 0:
            return t
        t -= mult
    return dim


def _my_place():
    return lax.axis_index("x"), lax.axis_index("y"), lax.axis_index("c")


def _flip(v, bit):
    return 1 - v if bit else v


def _matmul(a, b, *, form, out_dtype, tm, tn, tk, name, shard_cols=None, halves=False):
    if form == "nn":
        (m, k), (_, n) = a.shape, b.shape
        a_map, a_blk = (lambda i, j, kk: (i, kk)), (tm, tk)
        b_map, b_blk = (lambda i, j, kk: (kk, j)), (tk, tn)
        dims = (((1,), (0,)), ((), ()))
    elif form == "nt":
        m, k = (a.shape[1], 2 * a.shape[2]) if halves else a.shape
        n = b.shape[0]
        if halves:
            per_half = k // 2 // tk
            a_map, a_blk = (lambda i, j, kk: (kk // per_half, i, kk % per_half)), (None, tm, tk)
        else:
            a_map, a_blk = (lambda i, j, kk: (i, kk)), (tm, tk)
        b_map, b_blk = (lambda i, j, kk: (j, kk)), (tn, tk)
        dims = (((1,), (1,)), ((), ()))
    else:
        k, m = a.shape
        n = 2 * b.shape[2] if halves else b.shape[1]
        a_map, a_blk = (lambda i, j, kk: (kk, i)), (tk, tm)
        if halves:
            per_half = n // 2 // tn
            b_map, b_blk = (lambda i, j, kk: (j // per_half, kk, j % per_half)), (None, tk, tn)
        else:
            b_map, b_blk = (lambda i, j, kk: (kk, j)), (tk, tn)
        dims = (((0,), (0,)), ((), ()))
    assert m % tm == 0 and n % tn == 0 and k % tk == 0, (name, m, n, k, tm, tn, tk)
    nk = k // tk
    if shard_cols is None:
        out_shape = jax.ShapeDtypeStruct((m, n), out_dtype)
        o_map, o_blk = (lambda i, j, kk: (i, j)), (tm, tn)
    else:
        per = shard_cols // tn
        assert shard_cols % tn == 0
        out_shape = jax.ShapeDtypeStruct((n // shard_cols, m, shard_cols), out_dtype)
        o_map, o_blk = (lambda i, j, kk: (j // per, i, j % per)), (None, tm, tn)
    a_bytes, b_bytes = a.size * a.dtype.itemsize, b.size * b.dtype.itemsize
    rows_outer = nk > 1 or (m // tm) * b_bytes + a_bytes <= (n // tn) * a_bytes + b_bytes
    grid = (m // tm, n // tn, nk) if rows_outer else (n // tn, m // tm, nk)
    order = (lambda f: f) if rows_outer else (lambda f: (lambda g0, g1, kk: f(g1, g0, kk)))

    def body(a_ref, b_ref, o_ref, *acc):
        part = lax.dot_general(a_ref[...], b_ref[...], dims, preferred_element_type=F32)
        if nk == 1:
            o_ref[...] = part.astype(out_dtype)
            return
        acc_ref, = acc
        kk = pl.program_id(2)

        @pl.when(kk == 0)
        def _():
            acc_ref[...] = part

        @pl.when(jnp.logical_and(kk > 0, kk < nk - 1))
        def _():
            acc_ref[...] += part

        @pl.when(kk == nk - 1)
        def _():
            o_ref[...] = (acc_ref[...] + part).astype(out_dtype)

    return pl.pallas_call(
        body, name=name, out_shape=out_shape, grid=grid,
        in_specs=[pl.BlockSpec(a_blk, order(a_map)), pl.BlockSpec(b_blk, order(b_map))],
        out_specs=pl.BlockSpec(o_blk, order(o_map)),
        scratch_shapes=[pltpu.VMEM((tm, tn), F32)] if nk > 1 else [], compiler_params=_params(),
    )(a, b)


def _row_spec(tr, d):
    return pl.BlockSpec((tr, d), lambda i: (i, 0))


def _vec_spec(d):
    return pl.BlockSpec((1, d), lambda i: (0, 0))


def _rms(xf):
    r = lax.rsqrt(jnp.mean(xf * xf, axis=-1, keepdims=True) + EPS)
    return xf * r, r


def _norm_mod(x, scale, shift, name):
    s, d = x.shape
    tr = _pick(s, 512, 8)

    def body(x_ref, sc_ref, sh_ref, o_ref):
        n, _ = _rms(x_ref[...])
        o_ref[...] = (n * (1.0 + sc_ref[...]) + sh_ref[...]).astype(BF16)

    return pl.pallas_call(
        body, name=name, out_shape=jax.ShapeDtypeStruct((s, d), BF16), grid=(s // tr,),
        in_specs=[_row_spec(tr, d), _vec_spec(d), _vec_spec(d)], out_specs=_row_spec(tr, d),
        compiler_params=_params(),
    )(x, scale, shift)


def _out_norm(oa, ob, g_a, g_b, name):
    s, w = oa.shape
    tr = _pick(s, 512, 8)

    def body(oa_ref, ob_ref, ga_ref, gb_ref, o_ref):
        na, _ = _rms(oa_ref[...])
        nb, _ = _rms(ob_ref[...])
        o_ref[:, :w] = (na * ga_ref[...]).astype(BF16)
        o_ref[:, w:] = (nb * gb_ref[...]).astype(BF16)

    return pl.pallas_call(
        body, name=name, out_shape=jax.ShapeDtypeStruct((s, 2 * w), BF16), grid=(s // tr,),
        in_specs=[_row_spec(tr, w), _row_spec(tr, w), _vec_spec(w), _vec_spec(w)],
        out_specs=_row_spec(tr, 2 * w), compiler_params=_params(),
    )(oa, ob, g_a, g_b)


def _residual(x, gate, m, name):
    s, d = x.shape
    tr = _pick(s, 512, 8)

    def body(x_ref, g_ref, m_ref, o_ref):
        o_ref[...] = x_ref[...] + g_ref[...] * m_ref[...]

    return pl.pallas_call(
        body, name=name, out_shape=jax.ShapeDtypeStruct((s, d), F32), grid=(s // tr,),
        in_specs=[_row_spec(tr, d), _vec_spec(d), _row_spec(tr, d)], out_specs=_row_spec(tr, d),
        compiler_params=_params(),
    )(x, gate, m)


def _shift_down(u, k):
    rows = lax.broadcasted_iota(jnp.int32, u.shape, 0)
    return jnp.where(rows >= k, pltpu.roll(u, k, 0), 0.0)


def _shift_up(u, k):
    s = u.shape[0]
    rows = lax.broadcasted_iota(jnp.int32, u.shape, 0)
    return jnp.where(rows < s - k, pltpu.roll(u, s - k, 0), 0.0)


def _conv(u, w_ref, b_ref):
    return w_ref[0:1, :] * _shift_down(u, 2) + w_ref[1:2, :] * _shift_down(u, 1) + w_ref[2:3, :] * u + b_ref[...]


STRIP = 256


def _conv_strip(u_ref, t, r0, w_ref, b_ref):
    cur = u_ref[pl.ds(r0, STRIP), :]
    prev = u_ref[pl.ds(pl.multiple_of(jnp.maximum(r0 - 8, 0), 8), 8), :]
    ext = jnp.concatenate([jnp.where(t > 0, prev, 0.0), cur], axis=0)
    u1 = pltpu.roll(ext, 1, 0)[8:]
    u2 = pltpu.roll(ext, 2, 0)[8:]
    return w_ref[0:1, :] * u2 + w_ref[1:2, :] * u1 + w_ref[2:3, :] * cur + b_ref[...], u2, u1, cur


def _fold8(v):
    return functools.reduce(jnp.add, [v[r:r + 8] for r in range(0, STRIP, 8)])


def _conv_glu(u, conv_w, conv_b, name):
    s, f2 = u.shape
    f = f2 // 2
    tc = LANES
    nb = f // tc

    def body(ug_ref, uv_ref, wg_ref, wv_ref, bg_ref, bv_ref, o_ref):
        def strip(t, _):
            r0 = pl.multiple_of(t * STRIP, STRIP)
            g = _conv_strip(ug_ref, t, r0, wg_ref, bg_ref)[0]
            v = _conv_strip(uv_ref, t, r0, wv_ref, bv_ref)[0]
            o_ref[pl.ds(r0, STRIP), :] = (g * jax.nn.sigmoid(g) * v).astype(BF16)
            return 0

        lax.fori_loop(0, s // STRIP, strip, 0)

    col = lambda off: pl.BlockSpec((s, tc), lambda j: (0, j + off))
    wcol = lambda off: pl.BlockSpec((3, tc), lambda j: (0, j + off))
    bcol = lambda off: pl.BlockSpec((1, tc), lambda j: (0, j + off))
    return pl.pallas_call(
        body, name=name, out_shape=jax.ShapeDtypeStruct((s, f), BF16), grid=(nb,),
        in_specs=[col(0), col(nb), wcol(0), wcol(nb), bcol(0), bcol(nb)], out_specs=col(0),
        compiler_params=_params(),
    )(u, u, conv_w, conv_w, conv_b, conv_b)


def _conv_glu_bwd(u, da, conv_w, conv_b, name):
    s, f2 = u.shape
    f = f2 // 2
    tc = LANES
    nb = f // tc

    def body(ug_ref, uv_ref, da_ref, wg_ref, wv_ref, bg_ref, bv_ref, du_ref, dw_ref, db_ref, dyg_ref, dyv_ref):
        def strip1(t, acc):
            r0 = pl.multiple_of(t * STRIP, STRIP)
            g, *ug = _conv_strip(ug_ref, t, r0, wg_ref, bg_ref)
            v, *uv = _conv_strip(uv_ref, t, r0, wv_ref, bv_ref)
            da_ = da_ref[pl.ds(r0, STRIP), :]
            sg = jax.nn.sigmoid(g)
            dg = da_ * v * (sg * (1.0 + g * (1.0 - sg)))
            dv = da_ * (g * sg)
            dyg_ref[pl.ds(r0, STRIP), :] = dg
            dyv_ref[pl.ds(r0, STRIP), :] = dv
            new = [_fold8(dy * uu) for dy, us in ((dg, ug), (dv, uv)) for uu in us] + [_fold8(dg), _fold8(dv)]
            return tuple(a + n for a, n in zip(acc, new))

        zero = jnp.zeros((8, LANES), F32)
        acc = lax.fori_loop(0, s // STRIP, strip1, (zero,) * 8)
        for h in range(2):
            for tap in range(3):
                dw_ref[h, tap:tap + 1, :] = jnp.sum(acc[3 * h + tap], axis=0, keepdims=True)
            db_ref[h] = jnp.sum(acc[6 + h], axis=0, keepdims=True)
        dyg_ref[s:, :] = zero
        dyv_ref[s:, :] = zero

        def strip2(t, _):
            r0 = pl.multiple_of(t * STRIP, STRIP)
            for h, (dy_ref, w_ref) in enumerate(((dyg_ref, wg_ref), (dyv_ref, wv_ref))):
                cur = dy_ref[pl.ds(r0, STRIP), :]
                ext = jnp.concatenate([cur, dy_ref[pl.ds(r0 + STRIP, 8), :]], axis=0)
                d1 = pltpu.roll(ext, STRIP + 7, 0)[:STRIP]
                d2 = pltpu.roll(ext, STRIP + 6, 0)[:STRIP]
                du = w_ref[2:3, :] * cur + w_ref[1:2, :] * d1 + w_ref[0:1, :] * d2
                du_ref[h, pl.ds(r0, STRIP), :] = du.astype(BF16)
            return 0

        lax.fori_loop(0, s // STRIP, strip2, 0)

    col = lambda off: pl.BlockSpec((s, tc), lambda j: (0, j + off))
    wcol = lambda off: pl.BlockSpec((3, tc), lambda j: (0, j + off))
    bcol = lambda off: pl.BlockSpec((1, tc), lambda j: (0, j + off))
    return pl.pallas_call(
        body, name=name, grid=(nb,),
        out_shape=(jax.ShapeDtypeStruct((2, s, f), BF16), jax.ShapeDtypeStruct((2, 3, f), F32),
                   jax.ShapeDtypeStruct((2, 1, f), F32)),
        in_specs=[col(0), col(nb), col(0), wcol(0), wcol(nb), bcol(0), bcol(nb)],
        out_specs=(pl.BlockSpec((2, s, tc), lambda j: (0, 0, j)), pl.BlockSpec((2, 3, tc), lambda j: (0, 0, j)),
                   pl.BlockSpec((2, 1, tc), lambda j: (0, 0, j))),
        scratch_shapes=[pltpu.VMEM((s + 8, tc), F32), pltpu.VMEM((s + 8, tc), F32)],
        compiler_params=_params(),
    )(u, u, da, conv_w, conv_w, conv_b, conv_b)


def _accumulate(ref, val):
    @pl.when(pl.program_id(0) == 0)
    def _():
        ref[...] = val

    @pl.when(pl.program_id(0) > 0)
    def _():
        ref[...] += val


def _rms_bwd(n, r, dn):
    return r * (dn - n * jnp.mean(dn * n, axis=-1, keepdims=True))


def _loss_head(x, final_g, target, name):
    s, d = x.shape
    tr = _pick(s, 512, 8)

    def body(x_ref, g_ref, t_ref, loss_ref, dx_ref, dg_ref):
        n, r = _rms(x_ref[...])
        diff = n * g_ref[...] - t_ref[...]
        part = 0.5 * jnp.sum(jnp.sum(diff * diff, axis=1, keepdims=True), axis=0, keepdims=True) / d
        _accumulate(loss_ref, part)
        dy = diff / d
        _accumulate(dg_ref, jnp.sum(dy * n, axis=0, keepdims=True))
        dx_ref[...] = _rms_bwd(n, r, dy * g_ref[...])

    return pl.pallas_call(
        body, name=name, grid=(s // tr,),
        out_shape=(jax.ShapeDtypeStruct((1, 1), F32), jax.ShapeDtypeStruct((s, d), F32), jax.ShapeDtypeStruct((1, d), F32)),
        in_specs=[_row_spec(tr, d), _vec_spec(d), _row_spec(tr, d)],
        out_specs=(pl.BlockSpec((1, 1), lambda i: (0, 0)), _row_spec(tr, d), _vec_spec(d)),
        compiler_params=_params(),
    )(x, final_g, target)


def _gate_bwd(dx, m, gate, name):
    s, d = dx.shape
    tr = _pick(s, 512, 8)

    def body(dx_ref, m_ref, g_ref, dm_ref, dg_ref):
        dxv = dx_ref[...]
        dm_ref[...] = (dxv * g_ref[...]).astype(BF16)
        _accumulate(dg_ref, jnp.sum(dxv * m_ref[...], axis=0, keepdims=True))

    return pl.pallas_call(
        body, name=name, grid=(s // tr,),
        out_shape=(jax.ShapeDtypeStruct((s, d), BF16), jax.ShapeDtypeStruct((1, d), F32)),
        in_specs=[_row_spec(tr, d), _row_spec(tr, d), _vec_spec(d)], out_specs=(_row_spec(tr, d), _vec_spec(d)),
        compiler_params=_params(),
    )(dx, m, gate)


def _norm_mod_bwd(x, dh, dres, scale, name):
    s, d = x.shape
    tr = _pick(s, 512, 8)

    def body(x_ref, dh_ref, dr_ref, sc_ref, dx_ref, dsc_ref, dsh_ref):
        n, r = _rms(x_ref[...])
        dh_ = dh_ref[...]
        _accumulate(dsc_ref, jnp.sum(dh_ * n, axis=0, keepdims=True))
        _accumulate(dsh_ref, jnp.sum(dh_, axis=0, keepdims=True))
        dx_ref[...] = dr_ref[...] + _rms_bwd(n, r, dh_ * (1.0 + sc_ref[...]))

    return pl.pallas_call(
        body, name=name, grid=(s // tr,),
        out_shape=(jax.ShapeDtypeStruct((s, d), F32), jax.ShapeDtypeStruct((1, d), F32), jax.ShapeDtypeStruct((1, d), F32)),
        in_specs=[_row_spec(tr, d), _row_spec(tr, d), _row_spec(tr, d), _vec_spec(d)],
        out_specs=(_row_spec(tr, d), _vec_spec(d), _vec_spec(d)), compiler_params=_params(),
    )(x, dh, dres, scale)


def _out_norm_bwd(oa, ob, dcat, g_a, g_b, name):
    s, w = oa.shape
    tr = _pick(s, 512, 8)

    def body(oa_ref, ob_ref, dc_ref, ga_ref, gb_ref, doa_ref, dob_ref, dga_ref, dgb_ref):
        for o_ref, g_ref, do_ref, dg_ref, lo in ((oa_ref, ga_ref, doa_ref, dga_ref, 0), (ob_ref, gb_ref, dob_ref, dgb_ref, w)):
            n, r = _rms(o_ref[...])
            dc = dc_ref[:, lo:lo + w]
            _accumulate(dg_ref, jnp.sum(dc * n, axis=0, keepdims=True))
            do_ref[...] = _rms_bwd(n, r, dc * g_ref[...])

    return pl.pallas_call(
        body, name=name, grid=(s // tr,),
        out_shape=(jax.ShapeDtypeStruct((s, w), F32), jax.ShapeDtypeStruct((s, w), F32),
                   jax.ShapeDtypeStruct((1, w), F32), jax.ShapeDtypeStruct((1, w), F32)),
        in_specs=[_row_spec(tr, w), _row_spec(tr, w), _row_spec(tr, 2 * w), _vec_spec(w), _vec_spec(w)],
        out_specs=(_row_spec(tr, w), _row_spec(tr, w), _vec_spec(w), _vec_spec(w)), compiler_params=_params(),
    )(oa, ob, dcat, g_a, g_b)


def _head_masks():
    lane = lax.broadcasted_iota(jnp.int32, (1, LANES), 1)
    return lane < HEAD_DIM, lane >= HEAD_DIM


def _nt(a, b):
    return lax.dot_general(a, b, (((1,), (1,)), ((), ())), preferred_element_type=F32)


def _tn(a, b):
    return lax.dot_general(a, b, (((0,), (0,)), ((), ())), preferred_element_type=F32)


def _nn(a, b):
    return jnp.dot(a, b, preferred_element_type=F32)


def _only(mask, v):
    return jnp.where(mask, v, jnp.zeros_like(v))


def _fill_padded(dst_ref, src_ref):
    dst_ref[0:PAD, :] = jnp.zeros((PAD, LANES), dst_ref.dtype)
    dst_ref[PAD:, :] = src_ref[...]


def _chunk_probs(s, bias, chunk):
    pos = lax.broadcasted_iota(jnp.int32, (1, BAND), 1)
    s = jnp.where(pos >= (N_PREV - chunk) * CHUNK, s + bias, -1e30)
    e = jnp.exp(s - jnp.max(s, axis=1, keepdims=True))
    return e / jnp.sum(e, axis=1, keepdims=True)


def _band_windows(i, cq, kpad, vpad):
    chunks = [i * cq + cc for cc in range(cq)]
    starts = [pl.multiple_of(ch * CHUNK, CHUNK) for ch in chunks]
    return chunks, starts, [kpad[pl.ds(st, BAND), :] for st in starts], [vpad[pl.ds(st, BAND), :] for st in starts]


def _attn_a_fwd(proj, band_bias, name):
    s = proj.shape[0]
    cq = 4
    tq = cq * CHUNK
    npair = N_HEADS // 2
    kcol, vcol = W_GRP // LANES, 2 * W_GRP // LANES

    def body(q_ref, k_ref, v_ref, b_ref, o_ref, kpad, vpad):
        i = pl.program_id(1)
        masks = _head_masks()

        @pl.when(i == 0)
        def _():
            _fill_padded(kpad, k_ref)
            _fill_padded(vpad, v_ref)

        chunks, _, kbs, vbs = _band_windows(i, cq, kpad, vpad)
        q2 = q_ref[...] * (HEAD_DIM ** -0.5)
        units = [(cc, h) for cc in range(cq) for h in range(2)]
        ss = [_nt(_only(masks[h], q2[cc * CHUNK:(cc + 1) * CHUNK]), kbs[cc]) for cc, h in units]
        ps = [_chunk_probs(s_, b_ref[h], chunks[cc]).astype(BF16) for s_, (cc, h) in zip(ss, units)]
        for cc in range(cq):
            o_ref[cc * CHUNK:(cc + 1) * CHUNK, :] = (_nn(ps[2 * cc], _only(masks[0], vbs[cc]))
                                                     + _nn(ps[2 * cc + 1], _only(masks[1], vbs[cc])))

    return pl.pallas_call(
        body, name=name, out_shape=jax.ShapeDtypeStruct((s, W_GRP), F32), grid=(npair, s // tq),
        in_specs=[pl.BlockSpec((tq, LANES), lambda p, i: (i, p)),
                  pl.BlockSpec((s, LANES), lambda p, i: (0, kcol + p)),
                  pl.BlockSpec((s, LANES), lambda p, i: (0, vcol + p)),
                  pl.BlockSpec((2, CHUNK, BAND), lambda p, i: (p, 0, 0))],
        out_specs=pl.BlockSpec((tq, LANES), lambda p, i: (i, p)),
        scratch_shapes=[pltpu.VMEM((s + PAD, LANES), BF16), pltpu.VMEM((s + PAD, LANES), BF16)],
        compiler_params=_params(),
    )(proj, proj, proj, band_bias)


def _attn_a_bwd(proj, band_bias, doa, name, rider=None):
    s = proj.shape[0]
    cq = 4
    tq = cq * CHUNK
    nq = s // tq
    npair = N_HEADS // 2
    kcol, vcol = W_GRP // LANES, 2 * W_GRP // LANES
    scale = HEAD_DIM ** -0.5

    def body(q_ref, k_ref, v_ref, b_ref, do_ref, dq_ref, dk_ref, dv_ref, db_ref, kpad, vpad, dkpad, dvpad):
        i = pl.program_id(1)
        masks = _head_masks()

        @pl.when(i == 0)
        def _():
            _fill_padded(kpad, k_ref)
            _fill_padded(vpad, v_ref)
            dkpad[...] = jnp.zeros_like(dkpad)
            dvpad[...] = jnp.zeros_like(dvpad)
            db_ref[...] = jnp.zeros_like(db_ref)

        chunks, starts, kbs, vbs = _band_windows(i, cq, kpad, vpad)
        q2 = q_ref[...] * scale
        do2 = do_ref[...].astype(BF16)
        units = [(cc, h) for cc in range(cq) for h in range(2)]
        qhs = [_only(masks[h], q2[cc * CHUNK:(cc + 1) * CHUNK]) for cc, h in units]
        dohs = [_only(masks[h], do2[cc * CHUNK:(cc + 1) * CHUNK]) for cc, h in units]
        ss = [_nt(qh, kbs[cc]) for qh, (cc, h) in zip(qhs, units)]
        dps = [_nt(doh, vbs[cc]) for doh, (cc, h) in zip(dohs, units)]
        ps = [_chunk_probs(s_, b_ref[h], chunks[cc]) for s_, (cc, h) in zip(ss, units)]
        dss = [p * (dp - jnp.sum(p * dp, axis=1, keepdims=True)) for p, dp in zip(ps, dps)]
        for h in range(2):
            db_ref[h] += functools.reduce(jnp.add, [dss[2 * cc + h] for cc in range(cq)])
        for cc in range(cq):
            u0, u1 = 2 * cc, 2 * cc + 1
            dsb = [dss[u0].astype(BF16), dss[u1].astype(BF16)]
            dq = _nn(dsb[0], _only(masks[0], kbs[cc])) + _nn(dsb[1], _only(masks[1], kbs[cc]))
            dq_ref[cc * CHUNK:(cc + 1) * CHUNK, :] = dq * scale
            dkpad[pl.ds(starts[cc], BAND), :] += _tn(jnp.concatenate(dsb, axis=0), jnp.concatenate([qhs[u0], qhs[u1]], axis=0))
            dvpad[pl.ds(starts[cc], BAND), :] += _tn(jnp.concatenate([ps[u0].astype(BF16), ps[u1].astype(BF16)], axis=0),
                                                     jnp.concatenate([dohs[u0], dohs[u1]], axis=0))

        @pl.when(i == nq - 1)
        def _():
            dk_ref[...] = dkpad[PAD:, :]
            dv_ref[...] = dvpad[PAD:, :]

    blk = pl.BlockSpec((tq, LANES), lambda p, i: (i, p))
    whole = pl.BlockSpec((s, LANES), lambda p, i: (0, p))
    bias_spec = pl.BlockSpec((2, CHUNK, BAND), lambda p, i: (p, 0, 0))
    return _call_with_rider(
        body, rider, name=name, grid=(npair, nq),
        out_shape=(jax.ShapeDtypeStruct((s, W_GRP), F32),) * 3 + (jax.ShapeDtypeStruct((N_HEADS, CHUNK, BAND), F32),),
        in_specs=[blk, pl.BlockSpec((s, LANES), lambda p, i: (0, kcol + p)),
                  pl.BlockSpec((s, LANES), lambda p, i: (0, vcol + p)), bias_spec, blk],
        out_specs=(blk, whole, whole, bias_spec),
        scratch_shapes=[pltpu.VMEM((s + PAD, LANES), BF16), pltpu.VMEM((s + PAD, LANES), BF16),
                        pltpu.VMEM((s + PAD, LANES), F32), pltpu.VMEM((s + PAD, LANES), F32)],
        args=(proj, proj, proj, band_bias, doa))


def _split3(v):
    hi = v.astype(BF16)
    r1 = v - hi.astype(F32)
    mid = r1.astype(BF16)
    lo = (r1 - mid.astype(F32)).astype(BF16)
    return hi, mid, lo


def _rel_bias_grad(dband_t, name):
    width = 3 * LANES

    def body(t_ref, o_ref):
        pos = lax.broadcasted_iota(jnp.int32, (BAND, width), 0)
        col = lax.broadcasted_iota(jnp.int32, (BAND, width), 1)
        acc = jnp.zeros((N_HEADS, width), F32)
        for q in range(CHUNK):
            idx = jnp.minimum(PAD + q - pos, REL_CLIP) + REL_CLIP
            onehot = (col == idx).astype(BF16)
            for part in _split3(t_ref[q]):
                acc = acc + _nn(part, onehot)
        o_ref[...] = acc

    return pl.pallas_call(
        body, name=name, out_shape=jax.ShapeDtypeStruct((N_HEADS, width), F32),
        in_specs=[VMEM_FULL], out_specs=VMEM_FULL, compiler_params=_params(),
    )(dband_t)


def _split2_wide(v):
    hi = v.astype(BF16)
    return jnp.concatenate([hi, (v - hi.astype(F32)).astype(BF16)], axis=1)


def _sb_logs(z, lower):
    e = jnp.exp(-jnp.abs(z))
    lb = jnp.minimum(z, 0.0) - jnp.log(1.0 + e)
    lk = lb - z
    if lower is not None:
        lk = jnp.where(lower, lk, 0.0)
    return z, e, lb, lk


def _tri_masks(tq):
    row = lax.broadcasted_iota(jnp.int32, (tq, tq), 0)
    col = lax.broadcasted_iota(jnp.int32, (tq, tq), 1)
    return row, col


def _stack2(m):
    return jnp.concatenate([m, m], axis=0).astype(BF16)


def _sb_fwd(proj, name, rider=None):
    s = proj.shape[0]
    tq = _pick(s, 256)
    nq = s // tq
    npair = N_HEADS // 2
    qcol, kcol, vcol = 3 * W_GRP // LANES, 4 * W_GRP // LANES, 5 * W_GRP // LANES

    assert nq % 2 == 0

    def body(q_ref, k_ref, v_ref, o_ref, l_ref):
        i = pl.program_id(1)
        masks = _head_masks()
        q2 = q_ref[...] * (HEAD_DIM ** -0.5)
        qs = [[_only(m, q2[c * tq:(c + 1) * tq]) for m in masks] for c in range(2)]
        row, col = _tri_masks(tq)
        lower = row > col
        after2 = _stack2(lower)

        def tile(kblock, chains, carry):
            accs, tails = [list(t) for t in carry[0]], [list(t) for t in carry[1]]
            ks = pl.multiple_of(kblock * tq, tq)
            kb = k_ref[pl.ds(ks, tq), :]
            vb = v_ref[pl.ds(ks, tq), :]
            units = [(c, h, diag) for c, diag in chains for h in range(2)]
            zs = [_nt(qs[c][h], kb) for c, h, _ in units]
            vh = [_only(masks[h], vb) for h in range(2)]
            lbs, lks, locs = [], [], []
            for z, (c, h, diag) in zip(zs, units):
                lb, lk = _sb_logs(z, lower if diag else None)[2:]
                lbs.append(lb)
                lks.append(lk)
                locs.append(_nn(_split2_wide(lk), after2))
            for lb, lk, loc, (c, h, diag) in zip(lbs, lks, locs, units):
                a = jnp.exp(lb + (loc + tails[c][h]))
                if diag:
                    a = jnp.where(lower, a, 0.0)
                accs[c][0] = accs[c][0] + _nn(a.astype(BF16), vh[h])
                tails[c][h] = tails[c][h] + (loc[:, 0:1] + lk[:, 0:1])
            return tuple(tuple(t) for t in accs), tuple(tuple(t) for t in tails)

        zero = jnp.zeros((tq, 1), F32)
        acc0 = jnp.zeros((tq, LANES), F32)
        carry = (((acc0,), (acc0,)), ((zero, zero), (zero, zero)))
        carry = tile(2 * i + 1, [(1, True)], carry)
        carry = tile(2 * i, [(0, True), (1, False)], carry)

        def alive(tails):
            return functools.reduce(jnp.maximum, [jnp.max(t) for ts in tails for t in ts]) > SB_DEAD

        def walk(state):
            jj, _, cr = state
            cr = tile(2 * i - jj, [(0, False), (1, False)], cr)
            return jj + 1, alive(cr[1]), cr

        jj, _, (accs, tails) = lax.while_loop(lambda st: jnp.logical_and(st[0] <= 2 * i, st[1]), walk,
                                              (jnp.int32(1), alive(carry[1]), carry))
        for c in range(2):
            o_ref[c * tq:(c + 1) * tq, :] = accs[c][0]
            l_ref[c * tq:(c + 1) * tq, 0:1] = tails[c][0]
            l_ref[c * tq:(c + 1) * tq, 1:2] = tails[c][1]
        l_ref[:, 2:3] = jnp.full((2 * tq, 1), (jj - 1).astype(F32))

    return _call_with_rider(
        body, rider, name=name, grid=(npair, nq // 2),
        out_shape=(jax.ShapeDtypeStruct((s, W_GRP), F32), jax.ShapeDtypeStruct((npair, s, 3), F32)),
        in_specs=[pl.BlockSpec((2 * tq, LANES), lambda p, i: (i, qcol + p)),
                  pl.BlockSpec((s, LANES), lambda p, i: (0, kcol + p)),
                  pl.BlockSpec((s, LANES), lambda p, i: (0, vcol + p))],
        out_specs=(pl.BlockSpec((2 * tq, LANES), lambda p, i: (i, p)),
                   pl.BlockSpec((None, 2 * tq, 3), lambda p, i: (p, i, 0))),
        scratch_shapes=[], args=(proj, proj, proj))


def _sb_bwd(proj, ltot, dob, name, rider=None):
    s = proj.shape[0]
    tq = _pick(s, 256)
    nq = s // tq
    npair = N_HEADS // 2
    qcol, kcol, vcol = 3 * W_GRP // LANES, 4 * W_GRP // LANES, 5 * W_GRP // LANES
    scale = HEAD_DIM ** -0.5

    def body(q_ref, k_ref, v_ref, l_ref, do_ref, dq_ref, dk_ref, dv_ref):
        i = pl.program_id(1)
        masks = _head_masks()

        @pl.when(i == 0)
        def _():
            dk_ref[...] = jnp.zeros_like(dk_ref)
            dv_ref[...] = jnp.zeros_like(dv_ref)

        q2 = q_ref[...] * scale
        do2 = do_ref[...]
        part = lambda v, c: v[c * tq:(c + 1) * tq]
        qs = [[_only(m, part(q2, c)) for m in masks] for c in range(2)]
        doh = [[_only(m, part(do2, c)).astype(BF16) for m in masks] for c in range(2)]
        ltots = [[l_ref[c * tq:(c + 1) * tq, h:h + 1] for h in range(2)] for c in range(2)]
        row, col = _tri_masks(tq)
        lower = row > col
        upto2 = _stack2(row <= col)
        before = (row < col).astype(BF16)

        def tile(kblock, chains, carry):
            dqs, heads, gsums = [[list(t) for t in part_] for part_ in carry]
            ks = pl.multiple_of(kblock * tq, tq)
            kb = k_ref[pl.ds(ks, tq), :]
            vb = v_ref[pl.ds(ks, tq), :]
            units = [(c, h, diag) for c, diag in chains for h in range(2)]
            zs = [_nt(qs[c][h], kb) for c, h, _ in units]
            das = [_nt(doh[c][h], vb) for c, h, _ in units]
            kh = [_only(masks[h], kb) for h in range(2)]
            sigs, lbs, locs = [], [], []
            for z_, (c, h, diag) in zip(zs, units):
                z, e, lb, lk = _sb_logs(z_, lower if diag else None)
                locs.append(_nn(_split2_wide(lk), upto2))
                r = 1.0 / (1.0 + e)
                sigs.append(jnp.where(z >= 0, r, e * r))
                lbs.append(lb)
            a_s, gs, glocs = [], [], []
            for lb, loc, da, (c, h, diag) in zip(lbs, locs, das, units):
                a = jnp.exp(lb + (ltots[c][h] - (heads[c][h] + loc)))
                if diag:
                    a = jnp.where(lower, a, 0.0)
                g = a * da
                glocs.append(_nn(g.astype(BF16), before))
                a_s.append(a.astype(BF16))
                gs.append(g)
            dzbs = []
            for g, sig, loc, gloc, (c, h, diag) in zip(gs, sigs, locs, glocs, units):
                dz = g - sig * (g + (gsums[c][h] + gloc))
                if diag:
                    dz = jnp.where(lower, dz, 0.0)
                dzb = dz.astype(BF16)
                dzbs.append(dzb)
                dqs[c][0] = dqs[c][0] + _nn(dzb, kh[h])
                heads[c][h] = heads[c][h] + loc[:, tq - 1:tq]
                gsums[c][h] = gsums[c][h] + (gloc[:, tq - 1:tq] + g[:, tq - 1:tq])
            stack = lambda vs: vs[0] if len(vs) == 1 else jnp.concatenate(vs, axis=0)
            dk_ref[pl.ds(ks, tq), :] += _tn(stack(dzbs), stack([qs[c][h] for c, h, _ in units]))
            dv_ref[pl.ds(ks, tq), :] += _tn(stack(a_s), stack([doh[c][h] for c, h, _ in units]))
            return tuple(tuple(tuple(t) for t in part_) for part_ in (dqs, heads, gsums))

        zero = jnp.zeros((tq, 1), F32)
        acc0 = jnp.zeros((tq, LANES), F32)
        carry = (((acc0,), (acc0,)), ((zero, zero), (zero, zero)), ((zero, zero), (zero, zero)))
        walked = jnp.clip(jnp.max(l_ref[0:8, 2:3]).astype(jnp.int32), 0, 2 * i)
        carry = lax.fori_loop(2 * i - walked, 2 * i, lambda j, cr: tile(j, [(0, False), (1, False)], cr), carry)
        carry = tile(2 * i, [(0, True), (1, False)], carry)
        dqs, _, _ = tile(2 * i + 1, [(1, True)], carry)
        for c in range(2):
            dq_ref[c * tq:(c + 1) * tq, :] = dqs[c][0] * scale

    blk = pl.BlockSpec((2 * tq, LANES), lambda p, i: (i, p))
    whole = pl.BlockSpec((s, LANES), lambda p, i: (0, p))
    return _call_with_rider(
        body, rider, name=name, grid=(npair, nq // 2), out_shape=(jax.ShapeDtypeStruct((s, W_GRP), F32),) * 3,
        in_specs=[pl.BlockSpec((2 * tq, LANES), lambda p, i: (i, qcol + p)),
                  pl.BlockSpec((s, LANES), lambda p, i: (0, kcol + p)),
                  pl.BlockSpec((s, LANES), lambda p, i: (0, vcol + p)),
                  pl.BlockSpec((None, 2 * tq, 3), lambda p, i: (p, i, 0)), blk],
        out_specs=(blk, whole, whole), scratch_shapes=[], args=(proj, proj, proj, ltot, dob))


def _ada_fwd(c_all, w_ada, b_ada, name):
    nl, d, n = w_ada.shape
    tn = _pick(n, 512)

    def body(c_ref, w_ref, b_ref, o_ref):
        cv = c_ref[...]
        act = (cv * jax.nn.sigmoid(cv)).astype(BF16)
        o_ref[...] = _nn(act, w_ref[...].astype(BF16)) + b_ref[...]

    return pl.pallas_call(
        body, name=name, out_shape=jax.ShapeDtypeStruct((nl, N_DEV, n), F32), grid=(nl, n // tn),
        in_specs=[pl.BlockSpec((N_DEV, d), lambda l, j: (0, 0)), pl.BlockSpec((None, d, tn), lambda l, j: (l, 0, j)),
                  pl.BlockSpec((None, 1, tn), lambda l, j: (l, 0, j))],
        out_specs=pl.BlockSpec((None, N_DEV, tn), lambda l, j: (l, 0, j)), compiler_params=_params(),
    )(c_all, w_ada, b_ada)


def _ada_bwd(c_all, dmod, name):
    nl, _, n = dmod.shape
    d = c_all.shape[1]
    tn = _pick(n, 512)

    def body(c_ref, g_ref, o_ref):
        cv = c_ref[...]
        act = (cv * jax.nn.sigmoid(cv)).astype(BF16)
        o_ref[...] = _tn(act, g_ref[...].astype(BF16))

    return pl.pallas_call(
        body, name=name, out_shape=jax.ShapeDtypeStruct((nl, d, n), F32), grid=(nl, n // tn),
        in_specs=[pl.BlockSpec((N_DEV, d), lambda l, j: (0, 0)), pl.BlockSpec((None, N_DEV, tn), lambda l, j: (l, 0, j))],
        out_specs=pl.BlockSpec((None, d, tn), lambda l, j: (l, 0, j)), compiler_params=_params(),
    )(c_all, dmod)


def _adamw(g, w, m, v, name):
    r, c = g.shape
    tr = _pick(r, 512, 8)
    c1 = 1.0 - ADAM_B1 ** ADAM_STEP
    c2 = 1.0 - ADAM_B2 ** ADAM_STEP

    def body(g_ref, w_ref, m_ref, v_ref, d_ref, nm_ref, nv_ref):
        gv = g_ref[...]
        nm = ADAM_B1 * m_ref[...] + (1.0 - ADAM_B1) * gv
        nv = ADAM_B2 * v_ref[...] + (1.0 - ADAM_B2) * (gv * gv)
        d_ref[...] = -ADAM_LR * ((nm / c1) / (jnp.sqrt(nv / c2) + ADAM_EPS) + ADAM_WD * w_ref[...])
        nm_ref[...] = nm
        nv_ref[...] = nv

    spec = pl.BlockSpec((tr, c), lambda i: (i, 0))
    return pl.pallas_call(
        body, name=name, out_shape=(jax.ShapeDtypeStruct((r, c), F32),) * 3, grid=(r // tr,),
        in_specs=[spec] * 4, out_specs=(spec,) * 3, compiler_params=_params(),
    )(g, w, m, v)


def _adamw_nd(g, w, m, v, name):
    shape = w.shape
    two_d = (1, shape[0]) if len(shape) == 1 else (-1, shape[-1])
    outs = _adamw(*(t.reshape(two_d) for t in (g, w, m, v)), name=name)
    return tuple(o.reshape(shape) for o in outs)


def _allgather8(v, name):
    m, n = v.shape

    def body(v_ref, out_ref, send_sems, recv_sems, local_sem):
        x, y, c = _my_place()

        def rows(px, py, pc):
            return out_ref.at[pl.ds(pl.multiple_of((4 * px + 2 * py + pc) * m, 8), m), :]

        def peer(k):
            return _flip(x, k & 4), _flip(y, k & 2), _flip(c, k & 1)

        def copy(k, block):
            return pltpu.make_async_remote_copy(
                src_ref=v_ref, dst_ref=rows(*block), send_sem=send_sems.at[k - 1], recv_sem=recv_sems.at[k - 1],
                device_id=peer(k), device_id_type=MESH)

        mine = pltpu.make_async_copy(v_ref, rows(x, y, c), local_sem)
        mine.start()
        sends = [copy(k, (x, y, c)) for k in range(1, N_DEV)]
        for cp in sends:
            cp.start()
        for k in range(1, N_DEV):
            copy(k, peer(k)).wait_recv()
        for cp in sends:
            cp.wait_send()
        mine.wait()

    return pl.pallas_call(
        body, name=name, out_shape=jax.ShapeDtypeStruct((N_DEV * m, n), v.dtype),
        in_specs=[VMEM_FULL], out_specs=VMEM_FULL,
        scratch_shapes=[pltpu.SemaphoreType.DMA((N_DEV - 1,)), pltpu.SemaphoreType.DMA((N_DEV - 1,)),
                        pltpu.SemaphoreType.DMA],
        compiler_params=_params(),
    )(v)


def _chip_peers(x, y, c):
    out = []
    for k in range(1, N_CHIPS):
        px, py = _flip(x, k & 2), _flip(y, k & 1)
        out.append((2 * px + py, (px, py, c)))
    return out


def _gather_weights(shards, kinds, name):
    nw = len(shards)

    def full_shape(a, kind):
        l, r, n = a.shape
        return (l, r, N_CHIPS * n) if kind == "col" else (l, N_CHIPS * r, n)

    def body(*refs):
        ins, outs = refs[:nw], refs[nw:2 * nw]
        send_sems, recv_sems, local_sems = refs[2 * nw:]
        x, y, c = _my_place()
        chip = 2 * x + y

        def window(w, j):
            _, r, n = shards[w].shape
            if kinds[w] == "col":
                return outs[w].at[:, :, pl.ds(pl.multiple_of(j * n, LANES), n)]
            return outs[w].at[:, pl.ds(pl.multiple_of(j * r, 16), r), :]

        def copy(w, k, j, peer):
            return pltpu.make_async_remote_copy(
                src_ref=ins[w], dst_ref=window(w, j), send_sem=send_sems.at[3 * w + k], recv_sem=recv_sems.at[3 * w + k],
                device_id=peer, device_id_type=MESH)

        local = [pltpu.make_async_copy(ins[w], window(w, chip), local_sems.at[w]) for w in range(nw)]
        for cp in local:
            cp.start()
        peers = _chip_peers(x, y, c)
        sends = [copy(w, k, chip, peer) for w in range(nw) for k, (_, peer) in enumerate(peers)]
        for cp in sends:
            cp.start()
        for w in range(nw):
            for k, (pchip, peer) in enumerate(peers):
                copy(w, k, pchip, peer).wait_recv()
        for cp in sends:
            cp.wait_send()
        for cp in local:
            cp.wait()

    return pl.pallas_call(
        body, name=name,
        out_shape=tuple(jax.ShapeDtypeStruct(full_shape(a, kd), a.dtype) for a, kd in zip(shards, kinds)),
        in_specs=[ANY] * nw, out_specs=(ANY,) * nw,
        scratch_shapes=[pltpu.SemaphoreType.DMA((3 * nw,)), pltpu.SemaphoreType.DMA((3 * nw,)),
                        pltpu.SemaphoreType.DMA((nw,))],
        compiler_params=_params(),
    )(*shards)


def _rs_to_sibling(grads, name):
    nw = len(grads)

    def body(*refs):
        ins, outs = refs[:nw], refs[nw:2 * nw]
        send_sems, recv_sems = refs[2 * nw:]
        x, y, c = _my_place()
        sibling = (x, y, 1 - c)
        copies = [pltpu.make_async_remote_copy(
            src_ref=ins[w].at[j, 1 - c], dst_ref=outs[w].at[j], send_sem=send_sems.at[N_CHIPS * w + j],
            recv_sem=recv_sems.at[N_CHIPS * w + j], device_id=sibling, device_id_type=MESH)
            for w in range(nw) for j in range(N_CHIPS)]
        for cp in copies:
            cp.start()
        for cp in copies:
            cp.wait_recv()
        for cp in copies:
            cp.wait_send()

    return pl.pallas_call(
        body, name=name,
        out_shape=tuple(jax.ShapeDtypeStruct((N_CHIPS,) + g.shape[2:], g.dtype) for g in grads),
        in_specs=[ANY] * nw, out_specs=(ANY,) * nw,
        scratch_shapes=[pltpu.SemaphoreType.DMA((N_CHIPS * nw,)), pltpu.SemaphoreType.DMA((N_CHIPS * nw,))],
        compiler_params=_params(),
    )(*grads)


def _rs_to_chips(parts, name):
    nw = len(parts)

    def body(*refs):
        ins, outs = refs[:nw], refs[nw:2 * nw]
        send_sems, recv_sems, local_sems = refs[2 * nw:]
        x, y, c = _my_place()
        chip = 2 * x + y
        peers = _chip_peers(x, y, c)

        def copy(w, k, src_slab, dst_slab, peer):
            return pltpu.make_async_remote_copy(
                src_ref=ins[w].at[src_slab], dst_ref=outs[w].at[dst_slab], send_sem=send_sems.at[3 * w + k],
                recv_sem=recv_sems.at[3 * w + k], device_id=peer, device_id_type=MESH)

        local = [pltpu.make_async_copy(ins[w].at[chip], outs[w].at[chip], local_sems.at[w]) for w in range(nw)]
        for cp in local:
            cp.start()
        sends = [copy(w, k, pchip, chip, peer) for w in range(nw) for k, (pchip, peer) in enumerate(peers)]
        for cp in sends:
            cp.start()
        for w in range(nw):
            for k, (pchip, peer) in enumerate(peers):
                copy(w, k, chip, pchip, peer).wait_recv()
        for cp in sends:
            cp.wait_send()
        for cp in local:
            cp.wait()

    return pl.pallas_call(
        body, name=name, out_shape=tuple(jax.ShapeDtypeStruct(p.shape, p.dtype) for p in parts),
        in_specs=[ANY] * nw, out_specs=(ANY,) * nw,
        scratch_shapes=[pltpu.SemaphoreType.DMA((3 * nw,)), pltpu.SemaphoreType.DMA((3 * nw,)),
                        pltpu.SemaphoreType.DMA((nw,))],
        compiler_params=_params(),
    )(*parts)


def _rs_share_halves(halves, name):
    nw = len(halves)
    nl = len(halves[0])
    flat = [h for hs in halves for h in hs]

    def body(*refs):
        ins, outs = refs[:nw * nl], refs[nw * nl:nw * nl + nw]
        send_sems, recv_sems, local_sems = refs[nw * nl + nw:]
        x, y, c = _my_place()
        sibling = (x, y, 1 - c)
        local, sends, recvs = [], [], []
        for w in range(nw):
            for l in range(nl):
                n = nl * w + l
                local.append(pltpu.make_async_copy(ins[n], outs[w].at[l, c], local_sems.at[n]))
                sends.append(pltpu.make_async_remote_copy(
                    src_ref=ins[n], dst_ref=outs[w].at[l, c], send_sem=send_sems.at[n], recv_sem=recv_sems.at[n],
                    device_id=sibling, device_id_type=MESH))
                recvs.append(pltpu.make_async_remote_copy(
                    src_ref=ins[n], dst_ref=outs[w].at[l, 1 - c], send_sem=send_sems.at[n], recv_sem=recv_sems.at[n],
                    device_id=sibling, device_id_type=MESH))
        for cp in local + sends:
            cp.start()
        for cp in recvs:
            cp.wait_recv()
        for cp in sends:
            cp.wait_send()
        for cp in local:
            cp.wait()

    return pl.pallas_call(
        body, name=name,
        out_shape=tuple(jax.ShapeDtypeStruct((nl, 2) + hs[0].shape, hs[0].dtype) for hs in halves),
        in_specs=[ANY] * (nw * nl), out_specs=(ANY,) * nw,
        scratch_shapes=[pltpu.SemaphoreType.DMA((nw * nl,)), pltpu.SemaphoreType.DMA((nw * nl,)),
                        pltpu.SemaphoreType.DMA((nw * nl,))],
        compiler_params=_params(),
    )(*flat)


def _add_own_half(grad, got, c_idx, name):
    _, _, r, n = grad.shape
    tr = _pick(r, 256, 8)

    def body(c_ref, g_ref, t_ref, o_ref):
        o_ref[...] = g_ref[...] + t_ref[...]

    return pl.pallas_call(
        body, name=name, out_shape=jax.ShapeDtypeStruct((N_CHIPS, r, n), F32),
        grid_spec=pltpu.PrefetchScalarGridSpec(
            num_scalar_prefetch=1, grid=(N_CHIPS, r // tr),
            in_specs=[pl.BlockSpec((None, None, tr, n), lambda j, i, c_ref: (j, c_ref[0], i, 0)),
                      pl.BlockSpec((None, tr, n), lambda j, i, c_ref: (j, i, 0))],
            out_specs=pl.BlockSpec((None, tr, n), lambda j, i, c_ref: (j, i, 0))),
        compiler_params=_params(),
    )(c_idx, grad, got)


def _sum_slabs(slabs, name):
    ns, r, n = slabs.shape
    tr = _pick(r, 256, 8)

    def body(s_ref, o_ref):
        acc = s_ref[0]
        for j in range(1, ns):
            acc = acc + s_ref[j]
        o_ref[...] = acc

    return pl.pallas_call(
        body, name=name, out_shape=jax.ShapeDtypeStruct((r, n), F32), grid=(r // tr,),
        in_specs=[pl.BlockSpec((ns, tr, n), lambda i: (0, i, 0))], out_specs=pl.BlockSpec((tr, n), lambda i: (i, 0)),
        compiler_params=_params(),
    )(slabs)


def _band_bias(rel_bias):
    h = rel_bias.shape[0]
    n_far = PAD - REL_CLIP + CHUNK
    far = jnp.broadcast_to(rel_bias[:, N_REL - 1:N_REL], (h, n_far))
    near = rel_bias[:, REL_CLIP - CHUNK + 1:N_REL - 1][:, ::-1]
    line = jnp.concatenate([far, near], axis=1)
    return jnp.stack([line[:, CHUNK - 1 - q:CHUNK - 1 - q + BAND] for q in range(CHUNK)], axis=1)


def _pack_rows(pieces):
    flat = jnp.concatenate([p.reshape(-1) for p in pieces])
    rows = -(-flat.shape[0] // (8 * LANES)) * 8
    return jnp.pad(flat, (0, rows * LANES - flat.shape[0])).reshape(rows, LANES)


def _unpack_rows(packed, shapes):
    flat = packed.reshape(-1)
    out, at = [], 0
    for shp in shapes:
        size = 1
        for n in shp:
            size *= n
        out.append(flat[at:at + size].reshape(shp))
        at += size
    return out


def _layer_fwd(x, mod, w, band, tag):
    s, d = x.shape
    row = lambda i: mod[i:i + 1]
    h1 = _norm_mod(x, row(1), row(0), f"norm_mix{tag}")
    proj = _matmul(h1, w["w_in"], form="nn", out_dtype=BF16, tm=_pick(s, 512), tn=_pick(w["w_in"].shape[1], 768),
                   tk=d, name=f"proj{tag}")
    oa = _attn_a_fwd(proj, band, f"attn_a{tag}")
    ob, ltot = _sb_fwd(proj, f"attn_b{tag}")
    cat = _out_norm(oa, ob, w["g_a"], w["g_b"], f"out_norm{tag}")
    mixed = _matmul(cat, w["w_out"], form="nn", out_dtype=F32, tm=_pick(s, 512), tn=_pick(d, 1024),
                    tk=cat.shape[1], name=f"mix_out{tag}")
    x1 = _residual(x, row(2), mixed, f"res_mix{tag}")
    h2 = _norm_mod(x1, row(4), row(3), f"norm_ffn{tag}")
    f2 = w["w_up"].shape[1]
    u = _matmul(h2, w["w_up"], form="nn", out_dtype=F32, tm=_pick(s, 512), tn=_pick(f2, 1408), tk=d, name=f"up{tag}")
    a = _conv_glu(u, w["conv_w"], w["conv_b"], f"conv_glu{tag}")
    f = _matmul(a, w["w_down"], form="nn", out_dtype=F32, tm=_pick(s, 512), tn=_pick(d, 1024),
                tk=_pick(f2 // 2, 2816), name=f"down{tag}")
    x2 = _residual(x1, row(5), f, f"res_ffn{tag}")
    saved = dict(x=x, h1=h1, proj=proj, oa=oa, ob=ob, ltot=ltot, cat=cat, mixed=mixed, x1=x1, h2=h2, u=u, a=a, f=f)
    return x2, saved


def _layer_bwd(dx2, sv, mod, w, band, tag):
    s, d = dx2.shape
    row = lambda i: mod[i:i + 1]
    f2 = w["w_up"].shape[1]
    ff = f2 // 2
    n_in = w["w_in"].shape[1]
    df, dgate_ffn = _gate_bwd(dx2, sv["f"], row(5), f"gate_ffn_bwd{tag}")
    da = _matmul(df, w["w_down"], form="nt", out_dtype=F32, tm=_pick(s, 512), tn=_pick(ff, 1408), tk=d, name=f"down_dx{tag}")
    g_down = _matmul(sv["a"], df, form="tn", out_dtype=F32, tm=_pick(ff, 1408), tn=_pick(d, 512), tk=_pick(s, 2048),
                     name=f"down_dw{tag}")
    du2, dcw, dcb = _conv_glu_bwd(sv["u"], da, w["conv_w"], w["conv_b"], f"conv_glu_bwd{tag}")
    dh2 = _matmul(du2, w["w_up"], form="nt", out_dtype=F32, tm=_pick(s, 512), tn=_pick(d, 1024), tk=_pick(ff, 2816),
                  name=f"up_dx{tag}", halves=True)
    g_up = _matmul(sv["h2"], du2, form="tn", out_dtype=F32, tm=_pick(d, 512), tn=_pick(f2 // N_CHIPS, 1408),
                   tk=_pick(s, 2048), name=f"up_dw{tag}", shard_cols=f2 // N_CHIPS, halves=True)
    dx1, dscale_ffn, dshift_ffn = _norm_mod_bwd(sv["x1"], dh2, dx2, row(4), f"norm_ffn_bwd{tag}")
    dmixed, dgate_mix = _gate_bwd(dx1, sv["mixed"], row(2), f"gate_mix_bwd{tag}")
    dcat = _matmul(dmixed, w["w_out"], form="nt", out_dtype=F32, tm=_pick(s, 512), tn=_pick(2 * W_GRP, 1024), tk=d,
                   name=f"mix_out_dx{tag}")
    g_out = _matmul(sv["cat"], dmixed, form="tn", out_dtype=F32, tm=_pick(2 * W_GRP, 512), tn=_pick(d, 1024),
                    tk=_pick(s, 2048), name=f"mix_out_dw{tag}")
    doa, dob, dg_a, dg_b = _out_norm_bwd(sv["oa"], sv["ob"], dcat, w["g_a"], w["g_b"], f"out_norm_bwd{tag}")
    dqa, dka, dva, dband = _attn_a_bwd(sv["proj"], band, doa, f"attn_a_bwd{tag}")
    dqb, dkb, dvb = _sb_bwd(sv["proj"], sv["ltot"], dob, f"attn_b_bwd{tag}")
    drel = _rel_bias_grad(jnp.transpose(dband, (1, 0, 2)), f"rel_bias_bwd{tag}")[:, :N_REL]
    dproj = jnp.concatenate([dqa, dka, dva, dqb, dkb, dvb], axis=1).astype(BF16)
    dh1 = _matmul(dproj, w["w_in"], form="nt", out_dtype=F32, tm=_pick(s, 512), tn=_pick(d, 1024), tk=_pick(n_in, 3072),
                  name=f"proj_dx{tag}")
    g_in = _matmul(sv["h1"], dproj, form="tn", out_dtype=F32, tm=_pick(d, 512), tn=_pick(n_in // N_CHIPS, 768),
                   tk=_pick(s, 2048), name=f"proj_dw{tag}", shard_cols=n_in // N_CHIPS)
    dx, dscale_mix, dshift_mix = _norm_mod_bwd(sv["x"], dh1, dx1, row(1), f"norm_mix_bwd{tag}")
    dmod = jnp.concatenate([dshift_mix, dscale_mix, dgate_mix, dshift_ffn, dscale_ffn, dgate_ffn], axis=1)
    big = dict(w_in=g_in, w_out=g_out, w_up=g_up, w_down=g_down)
    dconv_w = jnp.concatenate([dcw[0], dcw[1]], axis=1)
    dconv_b = jnp.concatenate([dcb[0], dcb[1]], axis=1)
    small = dict(dmod=dmod, rel_bias=drel, g_a=dg_a, g_b=dg_b, conv_w=dconv_w, conv_b=dconv_b)
    return dx, big, small


def _kernel_unoverlapped(x, c, w_ada, b_ada, w_in, rel_bias, g_a, g_b, w_out, w_up, conv_w, conv_b, w_down, final_g, loss_target, m_w_ada, m_b_ada, m_w_in, m_rel_bias, m_g_a, m_g_b, m_w_out, m_w_up, m_conv_w, m_conv_b, m_w_down, m_final_g, v_w_ada, v_b_ada, v_w_in, v_rel_bias, v_g_a, v_g_b, v_w_out, v_w_up, v_conv_w, v_conv_b, v_w_down, v_final_g):
    xi, yi, ci = _my_place()
    chip = 2 * xi + yi
    dev = 4 * xi + 2 * yi + ci
    nl, d, n_ada = w_ada.shape
    s = x.shape[1]
    f2 = N_CHIPS * w_up.shape[2]
    nc = conv_w.shape[2]

    c_pad = jnp.pad(c, ((0, 7), (0, 0)))
    c_all = _allgather8(c_pad, "gather_c")[0::8]
    b_mine = lax.dynamic_slice_in_dim(b_ada, chip * n_ada, n_ada, axis=1)[:, None, :]
    mod_shard = _ada_fwd(c_all, w_ada, b_mine, "ada")
    pack2 = _pack_rows([mod_shard, conv_w])
    got2 = _allgather8(pack2, "gather_mod").reshape(N_DEV, -1)
    mods, convs = [], []
    for j in range(N_CHIPS):
        ms, cw = _unpack_rows(got2[2 * j], [mod_shard.shape, conv_w.shape])
        mods.append(lax.dynamic_index_in_dim(ms, dev, axis=1, keepdims=False))
        convs.append(cw)
    mod = jnp.concatenate(mods, axis=1).reshape(nl, 6, d)
    conv_w_full = jnp.concatenate(convs, axis=2)

    names = ("w_in", "w_out", "w_up", "w_down")
    kinds = ("col", "row", "col", "row")
    shards = dict(w_in=w_in, w_out=w_out, w_up=w_up, w_down=w_down)
    full = _gather_weights([shards[n].astype(BF16) for n in names], kinds, "gather_weights")
    full = dict(zip(names, full))

    xs = x[0]
    layers, saved, bands = [], [], []
    for l in range(nl):
        w = {n: full[n][l] for n in names}
        w.update(g_a=g_a[l:l + 1], g_b=g_b[l:l + 1], conv_w=conv_w_full[l], conv_b=conv_b[l:l + 1])
        band = _band_bias(rel_bias[l])
        xs, sv = _layer_fwd(xs, mod[l], w, band, f"_l{l}")
        layers.append(w)
        bands.append(band)
        saved.append(sv)
    loss_part, dx, dfinal_g = _loss_head(xs, final_g[None, :], loss_target[0], "loss_head")
    loss = lax.psum(loss_part[0, 0], ("x", "y", "c"))

    big, small = [None] * nl, [None] * nl
    for l in reversed(range(nl)):
        dx, big[l], small[l] = _layer_bwd(dx, saved[l], mod[l], layers[l], bands[l], f"_l{l}")

    small_names = ("dmod", "rel_bias", "g_a", "g_b", "conv_w", "conv_b")
    pieces = [small[l][n] for l in range(nl) for n in small_names] + [dfinal_g]
    shapes = [p.shape for p in pieces]
    pack3 = _pack_rows(pieces)
    got3 = _allgather8(pack3, "gather_small").reshape(N_DEV, pack3.shape[0], LANES)
    summed = _unpack_rows(_sum_slabs(got3, "sum_small"), shapes)
    tot = [dict(zip(small_names, summed[len(small_names) * l:len(small_names) * (l + 1)])) for l in range(nl)]
    g_final_g = summed[-1].reshape(-1)
    g_b_ada = jnp.stack([tot[l]["dmod"].reshape(-1) for l in range(nl)])
    g_rel = jnp.stack([tot[l]["rel_bias"] for l in range(nl)])
    g_ga = jnp.stack([tot[l]["g_a"].reshape(-1) for l in range(nl)])
    g_gb = jnp.stack([tot[l]["g_b"].reshape(-1) for l in range(nl)])
    g_conv_b = jnp.stack([tot[l]["conv_b"].reshape(-1) for l in range(nl)])
    g_conv_w = jnp.stack([lax.dynamic_slice_in_dim(tot[l]["conv_w"], chip * nc, nc, axis=1) for l in range(nl)])
    per_dev = [_unpack_rows(got3[j], shapes) for j in range(N_DEV)]
    dmod_all = jnp.stack([jnp.stack([per_dev[j][len(small_names) * l].reshape(-1) for j in range(N_DEV)])
                          for l in range(nl)])
    g_w_ada = _ada_bwd(c_all, lax.dynamic_slice_in_dim(dmod_all, chip * n_ada, n_ada, axis=2), "ada_bwd")

    order = [(n, l) for n in names for l in range(nl)]
    flat_g = [big[l][n].reshape(N_CHIPS, 2, -1, 1024) for n, l in order]
    from_sib = _rs_to_sibling(flat_g, "rs_sibling")
    c_idx = jnp.reshape(ci, (1,)).astype(jnp.int32)
    chip_part = [_add_own_half(g, t, c_idx, f"rs_add_{n}_l{l}") for g, t, (n, l) in zip(flat_g, from_sib, order)]
    from_chips = _rs_to_chips(chip_part, "rs_chips")
    my_half = [_sum_slabs(t, f"rs_sum_{n}_l{l}") for t, (n, l) in zip(from_chips, order)]
    shard_g = _rs_share_halves([[my_half[nl * i + l] for l in range(nl)] for i in range(len(names))], "rs_halves")
    g_big = {n: shard_g[i].reshape(shards[n].shape) for i, n in enumerate(names)}

    grads = dict(w_ada=g_w_ada, b_ada=g_b_ada, rel_bias=g_rel, g_a=g_ga, g_b=g_gb, conv_w=g_conv_w, conv_b=g_conv_b,
                 final_g=g_final_g)
    weights = dict(w_ada=w_ada, b_ada=b_ada, w_in=w_in, rel_bias=rel_bias, g_a=g_a, g_b=g_b, w_out=w_out, w_up=w_up,
                   conv_w=conv_w, conv_b=conv_b, w_down=w_down, final_g=final_g)
    m_in = dict(w_ada=m_w_ada, b_ada=m_b_ada, w_in=m_w_in, rel_bias=m_rel_bias, g_a=m_g_a, g_b=m_g_b, w_out=m_w_out,
                w_up=m_w_up, conv_w=m_conv_w, conv_b=m_conv_b, w_down=m_w_down, final_g=m_final_g)
    v_in = dict(w_ada=v_w_ada, b_ada=v_b_ada, w_in=v_w_in, rel_bias=v_rel_bias, g_a=v_g_a, g_b=v_g_b, w_out=v_w_out,
                w_up=v_w_up, conv_w=v_conv_w, conv_b=v_conv_b, w_down=v_w_down, final_g=v_final_g)
    order_w = ("w_ada", "b_ada", "w_in", "rel_bias", "g_a", "g_b", "w_out", "w_up", "conv_w", "conv_b", "w_down", "final_g")
    upd = {n: _adamw_nd(grads[n], weights[n], m_in[n], v_in[n], f"adamw_{n}") for n in grads}
    for n, mn, th in zip(BIG, mine, theirs):
        grads[n], *upd[n] = _adamw_halves(mn, th, c_idx, weights[n], m_in[n], v_in[n], f"adamw_{n}")
    return (loss, dx[None], *[grads[n] for n in order_w], *[upd[n][0] for n in order_w],
            *[upd[n][1] for n in order_w], *[upd[n][2] for n in order_w])


class _Rider:
    def __init__(self, ins, out_shapes, n_remote, n_local, parts):
        self.ins = list(ins)
        self.out_shapes = list(out_shapes)
        self.scratch = [pltpu.SemaphoreType.DMA((n_remote,)), pltpu.SemaphoreType.DMA((n_remote,)),
                        pltpu.SemaphoreType.DMA((max(n_local, 1),))]
        self.parts = parts

    def start(self, in_refs, out_refs, sems):
        local, sends, _ = self.parts(in_refs, out_refs, sems)
        for cp in local() + sends():
            cp.start()

    def wait(self, in_refs, out_refs, sems):
        local, sends, recvs = self.parts(in_refs, out_refs, sems)
        for cp in recvs():
            cp.wait_recv()
        for cp in sends():
            cp.wait_send()
        for cp in local():
            cp.wait()


def _call_with_rider(body, rider, *, name, grid, out_shape, in_specs, out_specs, scratch_shapes, args):
    if rider is None:
        return pl.pallas_call(body, name=name, grid=grid, out_shape=tuple(out_shape), in_specs=list(in_specs),
                              out_specs=tuple(out_specs), scratch_shapes=list(scratch_shapes),
                              compiler_params=_params())(*args)
    n_in, n_out, n_scr = len(in_specs), len(out_specs), len(scratch_shapes)
    r_in, r_out = len(rider.ins), len(rider.out_shapes)

    def both(*refs):
        at = 0
        groups = []
        for size in (n_in, r_in, n_out, r_out, n_scr, len(rider.scratch)):
            groups.append(refs[at:at + size])
            at += size
        own_in, ride_in, own_out, ride_out, own_scr, sems = groups
        steps = [pl.program_id(a) for a in range(len(grid))]
        first = functools.reduce(jnp.logical_and, [st == 0 for st in steps])
        last = functools.reduce(jnp.logical_and, [st == g - 1 for st, g in zip(steps, grid)])

        @pl.when(first)
        def _():
            rider.start(ride_in, ride_out, sems)

        body(*own_in, *own_out, *own_scr)

        @pl.when(last)
        def _():
            rider.wait(ride_in, ride_out, sems)

    outs = pl.pallas_call(
        both, name=name, grid=grid, out_shape=tuple(out_shape) + tuple(rider.out_shapes),
        in_specs=list(in_specs) + [ANY] * r_in, out_specs=tuple(out_specs) + (ANY,) * r_out,
        scratch_shapes=list(scratch_shapes) + rider.scratch, compiler_params=_params(),
    )(*args, *rider.ins)
    return tuple(outs[:n_out]) + (list(outs[n_out:]),)


def _run_rider(rider, name):
    r_in, r_out = len(rider.ins), len(rider.out_shapes)

    def body(*refs):
        ins, outs, sems = refs[:r_in], refs[r_in:r_in + r_out], refs[r_in + r_out:]
        rider.start(ins, outs, sems)
        rider.wait(ins, outs, sems)

    return list(pl.pallas_call(
        body, name=name, out_shape=tuple(rider.out_shapes), in_specs=[ANY] * r_in, out_specs=(ANY,) * r_out,
        scratch_shapes=rider.scratch, compiler_params=_params(),
    )(*rider.ins))


def _remote(src, dst, sems, n, peer):
    return pltpu.make_async_remote_copy(src_ref=src, dst_ref=dst, send_sem=sems[0].at[n], recv_sem=sems[1].at[n],
                                        device_id=peer, device_id_type=MESH)


def _gather_rider(shards, kinds):
    nw = len(shards)
    out_shapes = [jax.ShapeDtypeStruct((a.shape[0], N_CHIPS * a.shape[1]) if kd == "col" else
                                       (N_CHIPS * a.shape[0], a.shape[1]), a.dtype) for a, kd in zip(shards, kinds)]

    def parts(ins, outs, sems):
        x, y, c = _my_place()
        chip = 2 * x + y
        peers = _chip_peers(x, y, c)

        def window(w, j):
            r, n = shards[w].shape
            if kinds[w] == "col":
                return outs[w].at[:, pl.ds(pl.multiple_of(j * n, LANES), n)]
            return outs[w].at[pl.ds(pl.multiple_of(j * r, 16), r), :]

        local = lambda: [pltpu.make_async_copy(ins[w], window(w, chip), sems[2].at[w]) for w in range(nw)]
        sends = lambda: [_remote(ins[w], window(w, chip), sems, 3 * w + k, peer)
                         for w in range(nw) for k, (_, peer) in enumerate(peers)]
        recvs = lambda: [_remote(ins[w], window(w, pchip), sems, 3 * w + k, peer)
                         for w in range(nw) for k, (pchip, peer) in enumerate(peers)]
        return local, sends, recvs

    return _Rider(shards, out_shapes, 3 * nw, nw, parts)


def _half(ref3, h, rows):
    return ref3.at[:, pl.ds(pl.multiple_of(h * rows, 8), rows), :]


def _sibling_rider(grads):
    nw = len(grads)
    out_shapes = [jax.ShapeDtypeStruct((g.shape[0], g.shape[1] // 2, g.shape[2]), g.dtype) for g in grads]

    def parts(ins, outs, sems):
        x, y, c = _my_place()
        sibling = (x, y, 1 - c)
        copies = lambda: [_remote(_half(ins[w], 1 - c, grads[w].shape[1] // 2), outs[w], sems, w, sibling)
                          for w in range(nw)]
        return (lambda: []), copies, copies

    return _Rider(grads, out_shapes, nw, 0, parts)


def _chips_rider(parts_in):
    nw = len(parts_in)
    out_shapes = [jax.ShapeDtypeStruct(p.shape, p.dtype) for p in parts_in]

    def parts(ins, outs, sems):
        x, y, c = _my_place()
        chip = 2 * x + y
        peers = _chip_peers(x, y, c)
        local = lambda: [pltpu.make_async_copy(ins[w].at[chip], outs[w].at[chip], sems[2].at[w]) for w in range(nw)]
        sends = lambda: [_remote(ins[w].at[pchip], outs[w].at[chip], sems, 3 * w + k, peer)
                         for w in range(nw) for k, (pchip, peer) in enumerate(peers)]
        recvs = lambda: [_remote(ins[w].at[chip], outs[w].at[pchip], sems, 3 * w + k, peer)
                         for w in range(nw) for k, (pchip, peer) in enumerate(peers)]
        return local, sends, recvs

    return _Rider(parts_in, out_shapes, 3 * nw, nw, parts)


def _halves_rider(halves):
    nw, nl = len(halves), len(halves[0])
    flat = [h for hs in halves for h in hs]
    out_shapes = [jax.ShapeDtypeStruct((nl, 2 * hs[0].shape[0], hs[0].shape[1]), hs[0].dtype) for hs in halves]

    def parts(ins, outs, sems):
        x, y, c = _my_place()
        sibling = (x, y, 1 - c)

        def window(w, l, h):
            rows = halves[w][0].shape[0]
            return outs[w].at[l, pl.ds(pl.multiple_of(h * rows, 8), rows), :]

        pairs = [(w, l) for w in range(nw) for l in range(nl)]
        local = lambda: [pltpu.make_async_copy(ins[nl * w + l], window(w, l, c), sems[2].at[nl * w + l]) for w, l in pairs]
        sends = lambda: [_remote(ins[nl * w + l], window(w, l, c), sems, nl * w + l, sibling) for w, l in pairs]
        recvs = lambda: [_remote(ins[nl * w + l], window(w, l, 1 - c), sems, nl * w + l, sibling) for w, l in pairs]
        return local, sends, recvs

    return _Rider(flat, out_shapes, nw * nl, nw * nl, parts)


def _add_my_half(grad, got, c_idx, name):
    _, r, n = got.shape
    tr = _pick(r, 256, 8)
    nblk = r // tr

    def body(c_ref, g_ref, t_ref, o_ref):
        o_ref[...] = g_ref[...] + t_ref[...]

    return pl.pallas_call(
        body, name=name, out_shape=jax.ShapeDtypeStruct(got.shape, F32),
        grid_spec=pltpu.PrefetchScalarGridSpec(
            num_scalar_prefetch=1, grid=(N_CHIPS, nblk),
            in_specs=[pl.BlockSpec((None, tr, n), lambda j, i, c_ref: (j, c_ref[0] * nblk + i, 0)),
                      pl.BlockSpec((None, tr, n), lambda j, i, c_ref: (j, i, 0))],
            out_specs=pl.BlockSpec((None, tr, n), lambda j, i, c_ref: (j, i, 0))),
        compiler_params=_params(),
    )(c_idx, grad, got)


def _swap_rider(mine):
    nw = len(mine)
    out_shapes = [jax.ShapeDtypeStruct(a.shape, a.dtype) for a in mine]

    def parts(ins, outs, sems):
        x, y, c = _my_place()
        copies = lambda: [_remote(ins[w], outs[w], sems, w, (x, y, 1 - c)) for w in range(nw)]
        return (lambda: []), copies, copies

    return _Rider(mine, out_shapes, nw, 0, parts)


def _sum_layers(slabs, name):
    nl = len(slabs)
    ns, r, n = slabs[0].shape
    tr = _pick(r, 256, 8)

    def body(*refs):
        o_ref = refs[nl]
        for l in range(nl):
            acc = refs[l][0]
            for j in range(1, ns):
                acc = acc + refs[l][j]
            o_ref[l] = acc

    return pl.pallas_call(
        body, name=name, out_shape=jax.ShapeDtypeStruct((nl, r, n), F32), grid=(r // tr,),
        in_specs=[pl.BlockSpec((ns, tr, n), lambda i: (0, i, 0))] * nl,
        out_specs=pl.BlockSpec((nl, tr, n), lambda i: (0, i, 0)), compiler_params=_params(),
    )(*slabs)


def _adam_math(gv, w, m, v):
    c1 = 1.0 - ADAM_B1 ** ADAM_STEP
    c2 = 1.0 - ADAM_B2 ** ADAM_STEP
    nm = ADAM_B1 * m + (1.0 - ADAM_B1) * gv
    nv = ADAM_B2 * v + (1.0 - ADAM_B2) * (gv * gv)
    return -ADAM_LR * ((nm / c1) / (jnp.sqrt(nv / c2) + ADAM_EPS) + ADAM_WD * w), nm, nv


def _adamw_halves(mine, theirs, c_idx, w, m, v, name):
    nl, r, n = mine.shape
    tr = _pick(r, 256, 8)
    nblk = r // tr

    def body(c_ref, mine_ref, theirs_ref, w_ref, m_ref, v_ref, g_ref, d_ref, nm_ref, nv_ref):
        gv = jnp.where(pl.program_id(1) == c_ref[0], mine_ref[...], theirs_ref[...])
        g_ref[...] = gv
        d_ref[...], nm_ref[...], nv_ref[...] = _adam_math(gv, w_ref[...], m_ref[...], v_ref[...])

    half = pl.BlockSpec((None, tr, n), lambda l, h, i, c_ref: (l, i, 0))
    full = pl.BlockSpec((None, tr, n), lambda l, h, i, c_ref: (l, h * nblk + i, 0))
    return pl.pallas_call(
        body, name=name, out_shape=(jax.ShapeDtypeStruct(w.shape, F32),) * 4,
        grid_spec=pltpu.PrefetchScalarGridSpec(
            num_scalar_prefetch=1, grid=(nl, 2, nblk), in_specs=[half, half, full, full, full],
            out_specs=(full,) * 4),
        compiler_params=_params(),
    )(c_idx, mine, theirs, w, m, v)


def _by_chip(g):
    return g if g.ndim == 3 else g.reshape(N_CHIPS, g.shape[0] // N_CHIPS, g.shape[1])


BIG = ("w_in", "w_out", "w_up", "w_down")
BIG_KIND = dict(w_in="col", w_out="row", w_up="col", w_down="row")


def _forward_layer(x, mod, w, band, tag, rider=None, on_arrival=None):
    s, d = x.shape
    row = lambda i: mod[i:i + 1]
    h1 = _norm_mod(x, row(1), row(0), f"norm_mix{tag}")
    proj = _matmul(h1, w["w_in"], form="nn", out_dtype=BF16, tm=_pick(s, 512), tn=_pick(w["w_in"].shape[1], 768),
                   tk=d, name=f"proj{tag}")
    oa = _attn_a_fwd(proj, band, f"attn_a{tag}")
    if rider is None:
        ob, ltot = _sb_fwd(proj, f"attn_b{tag}")
    else:
        ob, ltot, arrived = _sb_fwd(proj, f"attn_b{tag}", rider)
        on_arrival(arrived)
    cat = _out_norm(oa, ob, w["g_a"], w["g_b"], f"out_norm{tag}")
    mixed = _matmul(cat, w["w_out"], form="nn", out_dtype=F32, tm=_pick(s, 512), tn=_pick(d, 1024),
                    tk=cat.shape[1], name=f"mix_out{tag}")
    x1 = _residual(x, row(2), mixed, f"res_mix{tag}")
    h2 = _norm_mod(x1, row(4), row(3), f"norm_ffn{tag}")
    f2 = w["w_up"].shape[1]
    u = _matmul(h2, w["w_up"], form="nn", out_dtype=F32, tm=_pick(s, 512), tn=_pick(f2, 1408), tk=d, name=f"up{tag}")
    a = _conv_glu(u, w["conv_w"], w["conv_b"], f"conv_glu{tag}")
    f = _matmul(a, w["w_down"], form="nn", out_dtype=F32, tm=_pick(s, 512), tn=_pick(d, 1024),
                tk=_pick(f2 // 2, 2816), name=f"down{tag}")
    x2 = _residual(x1, row(5), f, f"res_ffn{tag}")
    saved = dict(x=x, h1=h1, proj=proj, oa=oa, ob=ob, ltot=ltot, cat=cat, mixed=mixed, x1=x1, h2=h2, u=u, a=a, f=f)
    return x2, saved


def _backward_layer(dx2, sv, mod, w, band, tag, c_idx, waiting=None):
    s, d = dx2.shape
    row = lambda i: mod[i:i + 1]
    f2 = w["w_up"].shape[1]
    ff = f2 // 2
    n_in = w["w_in"].shape[1]
    df, dgate_ffn = _gate_bwd(dx2, sv["f"], row(5), f"gate_ffn_bwd{tag}")
    da = _matmul(df, w["w_down"], form="nt", out_dtype=F32, tm=_pick(s, 512), tn=_pick(ff, 1408), tk=d, name=f"down_dx{tag}")
    g_down = _matmul(sv["a"], df, form="tn", out_dtype=F32, tm=_pick(ff, 1408), tn=_pick(d, 512), tk=_pick(s, 2048),
                     name=f"down_dw{tag}")
    du2, dcw, dcb = _conv_glu_bwd(sv["u"], da, w["conv_w"], w["conv_b"], f"conv_glu_bwd{tag}")
    dh2 = _matmul(du2, w["w_up"], form="nt", out_dtype=F32, tm=_pick(s, 512), tn=_pick(d, 1024), tk=_pick(ff, 2816),
                  name=f"up_dx{tag}", halves=True)
    g_up = _matmul(sv["h2"], du2, form="tn", out_dtype=F32, tm=_pick(d, 512), tn=_pick(f2 // N_CHIPS, 1408),
                   tk=_pick(s, 2048), name=f"up_dw{tag}", shard_cols=f2 // N_CHIPS, halves=True)
    dx1, dscale_ffn, dshift_ffn = _norm_mod_bwd(sv["x1"], dh2, dx2, row(4), f"norm_ffn_bwd{tag}")
    dmixed, dgate_mix = _gate_bwd(dx1, sv["mixed"], row(2), f"gate_mix_bwd{tag}")
    dcat = _matmul(dmixed, w["w_out"], form="nt", out_dtype=F32, tm=_pick(s, 512), tn=_pick(2 * W_GRP, 1024), tk=d,
                   name=f"mix_out_dx{tag}")
    g_out = _matmul(sv["cat"], dmixed, form="tn", out_dtype=F32, tm=_pick(2 * W_GRP, 512), tn=_pick(d, 1024),
                    tk=_pick(s, 2048), name=f"mix_out_dw{tag}")
    doa, dob, dg_a, dg_b = _out_norm_bwd(sv["oa"], sv["ob"], dcat, w["g_a"], w["g_b"], f"out_norm_bwd{tag}")
    big = {("w_down", tag): _by_chip(g_down), ("w_up", tag): _by_chip(g_up), ("w_out", tag): _by_chip(g_out)}
    if waiting is None:
        dqa, dka, dva, dband = _attn_a_bwd(sv["proj"], band, doa, f"attn_a_bwd{tag}")
        dqb, dkb, dvb = _sb_bwd(sv["proj"], sv["ltot"], dob, f"attn_b_bwd{tag}")
    else:
        raw = {**waiting, **big}
        keys = list(raw)
        dqa, dka, dva, dband, from_sib = _attn_a_bwd(sv["proj"], band, doa, f"attn_a_bwd{tag}",
                                                     _sibling_rider([raw[k] for k in keys]))
        chip_part = [_add_my_half(raw[k], t, c_idx, f"rs_add_{k[0]}{k[1]}") for k, t in zip(keys, from_sib)]
        dqb, dkb, dvb, from_chips = _sb_bwd(sv["proj"], sv["ltot"], dob, f"attn_b_bwd{tag}", _chips_rider(chip_part))
        big = dict(zip(keys, from_chips))
    drel = _rel_bias_grad(jnp.transpose(dband, (1, 0, 2)), f"rel_bias_bwd{tag}")[:, :N_REL]
    dproj = jnp.concatenate([dqa, dka, dva, dqb, dkb, dvb], axis=1).astype(BF16)
    dh1 = _matmul(dproj, w["w_in"], form="nt", out_dtype=F32, tm=_pick(s, 512), tn=_pick(d, 1024), tk=_pick(n_in, 3072),
                  name=f"proj_dx{tag}")
    g_in = _matmul(sv["h1"], dproj, form="tn", out_dtype=F32, tm=_pick(d, 512), tn=_pick(n_in // N_CHIPS, 768),
                   tk=_pick(s, 2048), name=f"proj_dw{tag}", shard_cols=n_in // N_CHIPS)
    dx, dscale_mix, dshift_mix = _norm_mod_bwd(sv["x"], dh1, dx1, row(1), f"norm_mix_bwd{tag}")
    dmod = jnp.concatenate([dshift_mix, dscale_mix, dgate_mix, dshift_ffn, dscale_ffn, dgate_ffn], axis=1)
    dconv_w = jnp.concatenate([dcw[0], dcw[1]], axis=1)
    dconv_b = jnp.concatenate([dcb[0], dcb[1]], axis=1)
    small = dict(dmod=dmod, rel_bias=drel, g_a=dg_a, g_b=dg_b, conv_w=dconv_w, conv_b=dconv_b)
    return dx, big, g_in, small


def kernel(x, c, w_ada, b_ada, w_in, rel_bias, g_a, g_b, w_out, w_up, conv_w, conv_b, w_down, final_g, loss_target, m_w_ada, m_b_ada, m_w_in, m_rel_bias, m_g_a, m_g_b, m_w_out, m_w_up, m_conv_w, m_conv_b, m_w_down, m_final_g, v_w_ada, v_b_ada, v_w_in, v_rel_bias, v_g_a, v_g_b, v_w_out, v_w_up, v_conv_w, v_conv_b, v_w_down, v_final_g):
    xi, yi, ci = _my_place()
    chip = 2 * xi + yi
    dev = 4 * xi + 2 * yi + ci
    c_idx = jnp.reshape(ci, (1,)).astype(jnp.int32)
    nl, d, n_ada = w_ada.shape
    nc = conv_w.shape[2]
    assert nl == 2

    c_pad = jnp.pad(c, ((0, 7), (0, 0)))
    c_all = _allgather8(c_pad, "gather_c")[0::8]
    b_mine = lax.dynamic_slice_in_dim(b_ada, chip * n_ada, n_ada, axis=1)[:, None, :]
    mod_shard = _ada_fwd(c_all, w_ada, b_mine, "ada")
    pack2 = _pack_rows([mod_shard, conv_w])
    got2 = _allgather8(pack2, "gather_mod").reshape(N_DEV, -1)
    mods, convs = [], []
    for j in range(N_CHIPS):
        ms, cw = _unpack_rows(got2[2 * j], [mod_shard.shape, conv_w.shape])
        mods.append(lax.dynamic_index_in_dim(ms, dev, axis=1, keepdims=False))
        convs.append(cw)
    mod = jnp.concatenate(mods, axis=1).reshape(nl, 6, d)
    conv_w_full = jnp.concatenate(convs, axis=2)

    shards = dict(w_in=w_in, w_out=w_out, w_up=w_up, w_down=w_down)
    sh = {(n, l): shards[n][l].astype(BF16) for n in BIG for l in range(nl)}
    early = [("w_in", 0)]
    riding = [[("w_out", 0), ("w_up", 0), ("w_down", 0), ("w_in", 1)], [("w_out", 1), ("w_up", 1), ("w_down", 1)]]
    layers = [dict(g_a=g_a[l:l + 1], g_b=g_b[l:l + 1], conv_w=conv_w_full[l], conv_b=conv_b[l:l + 1]) for l in range(nl)]

    def gather_rider(keys):
        return _gather_rider([sh[k] for k in keys], [BIG_KIND[k[0]] for k in keys])

    def arrival(keys):
        def fill(arrived):
            for (n, l), full in zip(keys, arrived):
                layers[l][n] = full
        return fill

    arrival(early)(_run_rider(gather_rider(early), "gather_first"))

    xs = x[0]
    saved, bands = [], []
    for l in range(nl):
        band = _band_bias(rel_bias[l])
        xs, sv = _forward_layer(xs, mod[l], layers[l], band, f"_l{l}", gather_rider(riding[l]), arrival(riding[l]))
        bands.append(band)
        saved.append(sv)
    loss_part, dx, dfinal_g = _loss_head(xs, final_g[None, :], loss_target[0], "loss_head")
    loss = lax.psum(loss_part[0, 0], ("x", "y", "c"))

    small = [None] * nl
    dx, raw1, g_in1, small[1] = _backward_layer(dx, saved[1], mod[1], layers[1], bands[1], "_l1", c_idx)
    raw1[("w_in", "_l1")] = g_in1
    dx, parts, g_in0, small[0] = _backward_layer(dx, saved[0], mod[0], layers[0], bands[0], "_l0", c_idx, raw1)
    (from_sib,) = _run_rider(_sibling_rider([g_in0]), "rs_sibling_last")
    (parts[("w_in", "_l0")],) = _run_rider(
        _chips_rider([_add_my_half(g_in0, from_sib, c_idx, "rs_add_w_in_l0")]), "rs_chips_last")
    mine = [_sum_layers([parts[(n, f"_l{l}")] for l in range(nl)], f"rs_sum_{n}") for n in BIG]
    theirs = _run_rider(_swap_rider(mine), "rs_swap")

    small_names = ("dmod", "rel_bias", "g_a", "g_b", "conv_w", "conv_b")
    pieces = [small[l][n] for l in range(nl) for n in small_names] + [dfinal_g]
    shapes = [p.shape for p in pieces]
    pack3 = _pack_rows(pieces)
    got3 = _allgather8(pack3, "gather_small").reshape(N_DEV, pack3.shape[0], LANES)
    summed = _unpack_rows(_sum_slabs(got3, "sum_small"), shapes)
    tot = [dict(zip(small_names, summed[len(small_names) * l:len(small_names) * (l + 1)])) for l in range(nl)]
    g_final_g = summed[-1].reshape(-1)
    g_b_ada = jnp.stack([tot[l]["dmod"].reshape(-1) for l in range(nl)])
    g_rel = jnp.stack([tot[l]["rel_bias"] for l in range(nl)])
    g_ga = jnp.stack([tot[l]["g_a"].reshape(-1) for l in range(nl)])
    g_gb = jnp.stack([tot[l]["g_b"].reshape(-1) for l in range(nl)])
    g_conv_b = jnp.stack([tot[l]["conv_b"].reshape(-1) for l in range(nl)])
    g_conv_w = jnp.stack([lax.dynamic_slice_in_dim(tot[l]["conv_w"], chip * nc, nc, axis=1) for l in range(nl)])
    per_dev = [_unpack_rows(got3[j], shapes) for j in range(N_DEV)]
    dmod_all = jnp.stack([jnp.stack([per_dev[j][len(small_names) * l].reshape(-1) for j in range(N_DEV)])
                          for l in range(nl)])
    g_w_ada = _ada_bwd(c_all, lax.dynamic_slice_in_dim(dmod_all, chip * n_ada, n_ada, axis=2), "ada_bwd")

    grads = dict(w_ada=g_w_ada, b_ada=g_b_ada, rel_bias=g_rel, g_a=g_ga, g_b=g_gb, conv_w=g_conv_w, conv_b=g_conv_b,
                 final_g=g_final_g)
    weights = dict(w_ada=w_ada, b_ada=b_ada, w_in=w_in, rel_bias=rel_bias, g_a=g_a, g_b=g_b, w_out=w_out, w_up=w_up,
                   conv_w=conv_w, conv_b=conv_b, w_down=w_down, final_g=final_g)
    m_in = dict(w_ada=m_w_ada, b_ada=m_b_ada, w_in=m_w_in, rel_bias=m_rel_bias, g_a=m_g_a, g_b=m_g_b, w_out=m_w_out,
                w_up=m_w_up, conv_w=m_conv_w, conv_b=m_conv_b, w_down=m_w_down, final_g=m_final_g)
    v_in = dict(w_ada=v_w_ada, b_ada=v_b_ada, w_in=v_w_in, rel_bias=v_rel_bias, g_a=v_g_a, g_b=v_g_b, w_out=v_w_out,
                w_up=v_w_up, conv_w=v_conv_w, conv_b=v_conv_b, w_down=v_w_down, final_g=v_final_g)
    order_w = ("w_ada", "b_ada", "w_in", "rel_bias", "g_a", "g_b", "w_out", "w_up", "conv_w", "conv_b", "w_down", "final_g")
    upd = {n: _adamw_nd(grads[n], weights[n], m_in[n], v_in[n], f"adamw_{n}") for n in grads}
    for n, mn, th in zip(BIG, mine, theirs):
        grads[n], *upd[n] = _adamw_halves(mn, th, c_idx, weights[n], m_in[n], v_in[n], f"adamw_{n}")
    return (loss, dx[None], *[grads[n] for n in order_w], *[upd[n][0] for n in order_w],
            *[upd[n][1] for n in order_w], *[upd[n][2] for n in order_w])
```

```python
import functools

import jax
import jax.numpy as jnp
from jax import lax
from jax.experimental import pallas as pl
from jax.experimental.pallas import tpu as pltpu

F32 = jnp.float32
BF16 = jnp.bfloat16
MESH = pl.DeviceIdType.MESH
ANY = pl.BlockSpec(memory_space=pl.ANY)
VMEM_FULL = pl.BlockSpec(memory_space=pltpu.VMEM)

HEAD_DIM = 64
N_HEADS = 8
W_GRP = N_HEADS * HEAD_DIM
CHUNK = 64
N_PREV = 8
BAND = (N_PREV + 1) * CHUNK
PAD = N_PREV * CHUNK
REL_CLIP = 128
N_REL = 2 * REL_CLIP + 1
EPS = 1e-6
SB_DEAD = -110.0
N_CHIPS = 4
N_DEV = 8
LANES = 128
V7X_VMEM_LIMIT = 56 * 1024 * 1024

ADAM_LR = 0.001
ADAM_B1 = 0.9
ADAM_B2 = 0.999
ADAM_EPS = 1e-08
ADAM_WD = 0.01
ADAM_STEP = 10


def _params(**kw):
    return pltpu.CompilerParams(vmem_limit_bytes=V7X_VMEM_LIMIT, **kw)


def _pick(dim, pref, mult=LANES):
    t = (min(pref, dim) // mult) * mult
    while t >= mult:
        if dim % t == 0:
            return t
        t -= mult
    return dim


def _my_place():
    return lax.axis_index("x"), lax.axis_index("y"), lax.axis_index("c")


def _flip(v, bit):
    return 1 - v if bit else v


def _matmul(a, b, *, form, out_dtype, tm, tn, tk, name, shard_cols=None, halves=False, rider=None):
    if form == "nn":
        (m, k), (_, n) = a.shape, b.shape
        a_map, a_blk = (lambda i, j, kk: (i, kk)), (tm, tk)
        b_map, b_blk = (lambda i, j, kk: (kk, j)), (tk, tn)
        dims = (((1,), (0,)), ((), ()))
    elif form == "nt":
        m, k = (a.shape[1], 2 * a.shape[2]) if halves else a.shape
        n = b.shape[0]
        if halves:
            per_half = k // 2 // tk
            a_map, a_blk = (lambda i, j, kk: (kk // per_half, i, kk % per_half)), (None, tm, tk)
        else:
            a_map, a_blk = (lambda i, j, kk: (i, kk)), (tm, tk)
        b_map, b_blk = (lambda i, j, kk: (j, kk)), (tn, tk)
        dims = (((1,), (1,)), ((), ()))
    else:
        k, m = a.shape
        n = 2 * b.shape[2] if halves else b.shape[1]
        a_map, a_blk = (lambda i, j, kk: (kk, i)), (tk, tm)
        if halves:
            per_half = n // 2 // tn
            b_map, b_blk = (lambda i, j, kk: (j // per_half, kk, j % per_half)), (None, tk, tn)
        else:
            b_map, b_blk = (lambda i, j, kk: (kk, j)), (tk, tn)
        dims = (((0,), (0,)), ((), ()))
    assert m % tm == 0 and n % tn == 0 and k % tk == 0, (name, m, n, k, tm, tn, tk)
    nk = k // tk
    if shard_cols is None:
        out_shape = jax.ShapeDtypeStruct((m, n), out_dtype)
        o_map, o_blk = (lambda i, j, kk: (i, j)), (tm, tn)
    else:
        per = shard_cols // tn
        assert shard_cols % tn == 0
        out_shape = jax.ShapeDtypeStruct((n // shard_cols, m, shard_cols), out_dtype)
        o_map, o_blk = (lambda i, j, kk: (j // per, i, j % per)), (None, tm, tn)
    a_bytes, b_bytes = a.size * a.dtype.itemsize, b.size * b.dtype.itemsize
    rows_outer = nk > 1 or (m // tm) * b_bytes + a_bytes <= (n // tn) * a_bytes + b_bytes
    grid = (m // tm, n // tn, nk) if rows_outer else (n // tn, m // tm, nk)
    order = (lambda f: f) if rows_outer else (lambda f: (lambda g0, g1, kk: f(g1, g0, kk)))

    def body(a_ref, b_ref, o_ref, *acc):
        part = lax.dot_general(a_ref[...], b_ref[...], dims, preferred_element_type=F32)
        if nk == 1:
            o_ref[...] = part.astype(out_dtype)
            return
        acc_ref, = acc
        kk = pl.program_id(2)

        @pl.when(kk == 0)
        def _():
            acc_ref[...] = part

        @pl.when(jnp.logical_and(kk > 0, kk < nk - 1))
        def _():
            acc_ref[...] += part

        @pl.when(kk == nk - 1)
        def _():
            o_ref[...] = (acc_ref[...] + part).astype(out_dtype)

    out = _call_with_rider(
        body, rider, name=name, out_shape=(out_shape,), grid=grid,
        in_specs=[pl.BlockSpec(a_blk, order(a_map)), pl.BlockSpec(b_blk, order(b_map))],
        out_specs=(pl.BlockSpec(o_blk, order(o_map)),),
        scratch_shapes=[pltpu.VMEM((tm, tn), F32)] if nk > 1 else [], args=(a, b))
    return out[0] if rider is None else out


def _row_spec(tr, d):
    return pl.BlockSpec((tr, d), lambda i: (i, 0))


def _vec_spec(d):
    return pl.BlockSpec((1, d), lambda i: (0, 0))


def _rms(xf):
    r = lax.rsqrt(jnp.mean(xf * xf, axis=-1, keepdims=True) + EPS)
    return xf * r, r


def _norm_mod(x, scale, shift, name):
    s, d = x.shape
    tr = _pick(s, 512, 8)

    def body(x_ref, sc_ref, sh_ref, o_ref):
        n, _ = _rms(x_ref[...])
        o_ref[...] = (n * (1.0 + sc_ref[...]) + sh_ref[...]).astype(BF16)

    return pl.pallas_call(
        body, name=name, out_shape=jax.ShapeDtypeStruct((s, d), BF16), grid=(s // tr,),
        in_specs=[_row_spec(tr, d), _vec_spec(d), _vec_spec(d)], out_specs=_row_spec(tr, d),
        compiler_params=_params(),
    )(x, scale, shift)


def _out_norm(oa, ob, g_a, g_b, name):
    s, w = oa.shape
    tr = _pick(s, 512, 8)

    def body(oa_ref, ob_ref, ga_ref, gb_ref, o_ref):
        na, _ = _rms(oa_ref[...])
        nb, _ = _rms(ob_ref[...])
        o_ref[:, :w] = (na * ga_ref[...]).astype(BF16)
        o_ref[:, w:] = (nb * gb_ref[...]).astype(BF16)

    return pl.pallas_call(
        body, name=name, out_shape=jax.ShapeDtypeStruct((s, 2 * w), BF16), grid=(s // tr,),
        in_specs=[_row_spec(tr, w), _row_spec(tr, w), _vec_spec(w), _vec_spec(w)],
        out_specs=_row_spec(tr, 2 * w), compiler_params=_params(),
    )(oa, ob, g_a, g_b)


def _residual(x, gate, m, name):
    s, d = x.shape
    tr = _pick(s, 512, 8)

    def body(x_ref, g_ref, m_ref, o_ref):
        o_ref[...] = x_ref[...] + g_ref[...] * m_ref[...]

    return pl.pallas_call(
        body, name=name, out_shape=jax.ShapeDtypeStruct((s, d), F32), grid=(s // tr,),
        in_specs=[_row_spec(tr, d), _vec_spec(d), _row_spec(tr, d)], out_specs=_row_spec(tr, d),
        compiler_params=_params(),
    )(x, gate, m)


def _shift_down(u, k):
    rows = lax.broadcasted_iota(jnp.int32, u.shape, 0)
    return jnp.where(rows >= k, pltpu.roll(u, k, 0), 0.0)


def _shift_up(u, k):
    s = u.shape[0]
    rows = lax.broadcasted_iota(jnp.int32, u.shape, 0)
    return jnp.where(rows < s - k, pltpu.roll(u, s - k, 0), 0.0)


def _conv(u, w_ref, b_ref):
    return w_ref[0:1, :] * _shift_down(u, 2) + w_ref[1:2, :] * _shift_down(u, 1) + w_ref[2:3, :] * u + b_ref[...]


STRIP = 256


def _conv_strip(u_ref, t, r0, w_ref, b_ref):
    cur = u_ref[pl.ds(r0, STRIP), :]
    prev = u_ref[pl.ds(pl.multiple_of(jnp.maximum(r0 - 8, 0), 8), 8), :]
    ext = jnp.concatenate([jnp.where(t > 0, prev, 0.0), cur], axis=0)
    u1 = pltpu.roll(ext, 1, 0)[8:]
    u2 = pltpu.roll(ext, 2, 0)[8:]
    return w_ref[0:1, :] * u2 + w_ref[1:2, :] * u1 + w_ref[2:3, :] * cur + b_ref[...], u2, u1, cur


def _fold8(v):
    return functools.reduce(jnp.add, [v[r:r + 8] for r in range(0, STRIP, 8)])


def _conv_glu(u, conv_w, conv_b, name):
    s, f2 = u.shape
    f = f2 // 2
    tc = LANES
    nb = f // tc

    def body(ug_ref, uv_ref, wg_ref, wv_ref, bg_ref, bv_ref, o_ref):
        def strip(t, _):
            r0 = pl.multiple_of(t * STRIP, STRIP)
            g = _conv_strip(ug_ref, t, r0, wg_ref, bg_ref)[0]
            v = _conv_strip(uv_ref, t, r0, wv_ref, bv_ref)[0]
            o_ref[pl.ds(r0, STRIP), :] = (g * jax.nn.sigmoid(g) * v).astype(BF16)
            return 0

        lax.fori_loop(0, s // STRIP, strip, 0)

    col = lambda off: pl.BlockSpec((s, tc), lambda j: (0, j + off))
    wcol = lambda off: pl.BlockSpec((3, tc), lambda j: (0, j + off))
    bcol = lambda off: pl.BlockSpec((1, tc), lambda j: (0, j + off))
    return pl.pallas_call(
        body, name=name, out_shape=jax.ShapeDtypeStruct((s, f), BF16), grid=(nb,),
        in_specs=[col(0), col(nb), wcol(0), wcol(nb), bcol(0), bcol(nb)], out_specs=col(0),
        compiler_params=_params(),
    )(u, u, conv_w, conv_w, conv_b, conv_b)


def _conv_glu_bwd(u, da, conv_w, conv_b, name, rider=None):
    s, f2 = u.shape
    f = f2 // 2
    tc = LANES
    nb = f // tc

    def body(ug_ref, uv_ref, da_ref, wg_ref, wv_ref, bg_ref, bv_ref, du_ref, dw_ref, db_ref, dyg_ref, dyv_ref):
        def strip1(t, acc):
            r0 = pl.multiple_of(t * STRIP, STRIP)
            g, *ug = _conv_strip(ug_ref, t, r0, wg_ref, bg_ref)
            v, *uv = _conv_strip(uv_ref, t, r0, wv_ref, bv_ref)
            da_ = da_ref[pl.ds(r0, STRIP), :]
            sg = jax.nn.sigmoid(g)
            dg = da_ * v * (sg * (1.0 + g * (1.0 - sg)))
            dv = da_ * (g * sg)
            dyg_ref[pl.ds(r0, STRIP), :] = dg
            dyv_ref[pl.ds(r0, STRIP), :] = dv
            new = [_fold8(dy * uu) for dy, us in ((dg, ug), (dv, uv)) for uu in us] + [_fold8(dg), _fold8(dv)]
            return tuple(a + n for a, n in zip(acc, new))

        zero = jnp.zeros((8, LANES), F32)
        acc = lax.fori_loop(0, s // STRIP, strip1, (zero,) * 8)
        for h in range(2):
            for tap in range(3):
                dw_ref[h, tap:tap + 1, :] = jnp.sum(acc[3 * h + tap], axis=0, keepdims=True)
            db_ref[h] = jnp.sum(acc[6 + h], axis=0, keepdims=True)
        dyg_ref[s:, :] = zero
        dyv_ref[s:, :] = zero

        def strip2(t, _):
            r0 = pl.multiple_of(t * STRIP, STRIP)
            for h, (dy_ref, w_ref) in enumerate(((dyg_ref, wg_ref), (dyv_ref, wv_ref))):
                cur = dy_ref[pl.ds(r0, STRIP), :]
                ext = jnp.concatenate([cur, dy_ref[pl.ds(r0 + STRIP, 8), :]], axis=0)
                d1 = pltpu.roll(ext, STRIP + 7, 0)[:STRIP]
                d2 = pltpu.roll(ext, STRIP + 6, 0)[:STRIP]
                du = w_ref[2:3, :] * cur + w_ref[1:2, :] * d1 + w_ref[0:1, :] * d2
                du_ref[h, pl.ds(r0, STRIP), :] = du.astype(BF16)
            return 0

        lax.fori_loop(0, s // STRIP, strip2, 0)

    col = lambda off: pl.BlockSpec((s, tc), lambda j: (0, j + off))
    wcol = lambda off: pl.BlockSpec((3, tc), lambda j: (0, j + off))
    bcol = lambda off: pl.BlockSpec((1, tc), lambda j: (0, j + off))
    return _call_with_rider(
        body, rider, name=name, grid=(nb,),
        out_shape=(jax.ShapeDtypeStruct((2, s, f), BF16), jax.ShapeDtypeStruct((2, 3, f), F32),
                   jax.ShapeDtypeStruct((2, 1, f), F32)),
        in_specs=[col(0), col(nb), col(0), wcol(0), wcol(nb), bcol(0), bcol(nb)],
        out_specs=(pl.BlockSpec((2, s, tc), lambda j: (0, 0, j)), pl.BlockSpec((2, 3, tc), lambda j: (0, 0, j)),
                   pl.BlockSpec((2, 1, tc), lambda j: (0, 0, j))),
        scratch_shapes=[pltpu.VMEM((s + 8, tc), F32), pltpu.VMEM((s + 8, tc), F32)],
        args=(u, u, da, conv_w, conv_w, conv_b, conv_b))


def _accumulate(ref, val):
    @pl.when(pl.program_id(0) == 0)
    def _():
        ref[...] = val

    @pl.when(pl.program_id(0) > 0)
    def _():
        ref[...] += val


def _rms_bwd(n, r, dn):
    return r * (dn - n * jnp.mean(dn * n, axis=-1, keepdims=True))


def _loss_head(x, final_g, target, name):
    s, d = x.shape
    tr = _pick(s, 512, 8)

    def body(x_ref, g_ref, t_ref, loss_ref, dx_ref, dg_ref):
        n, r = _rms(x_ref[...])
        diff = n * g_ref[...] - t_ref[...]
        part = 0.5 * jnp.sum(jnp.sum(diff * diff, axis=1, keepdims=True), axis=0, keepdims=True) / d
        _accumulate(loss_ref, part)
        dy = diff / d
        _accumulate(dg_ref, jnp.sum(dy * n, axis=0, keepdims=True))
        dx_ref[...] = _rms_bwd(n, r, dy * g_ref[...])

    return pl.pallas_call(
        body, name=name, grid=(s // tr,),
        out_shape=(jax.ShapeDtypeStruct((1, 1), F32), jax.ShapeDtypeStruct((s, d), F32), jax.ShapeDtypeStruct((1, d), F32)),
        in_specs=[_row_spec(tr, d), _vec_spec(d), _row_spec(tr, d)],
        out_specs=(pl.BlockSpec((1, 1), lambda i: (0, 0)), _row_spec(tr, d), _vec_spec(d)),
        compiler_params=_params(),
    )(x, final_g, target)


def _gate_bwd(dx, m, gate, name):
    s, d = dx.shape
    tr = _pick(s, 512, 8)

    def body(dx_ref, m_ref, g_ref, dm_ref, dg_ref):
        dxv = dx_ref[...]
        dm_ref[...] = (dxv * g_ref[...]).astype(BF16)
        _accumulate(dg_ref, jnp.sum(dxv * m_ref[...], axis=0, keepdims=True))

    return pl.pallas_call(
        body, name=name, grid=(s // tr,),
        out_shape=(jax.ShapeDtypeStruct((s, d), BF16), jax.ShapeDtypeStruct((1, d), F32)),
        in_specs=[_row_spec(tr, d), _row_spec(tr, d), _vec_spec(d)], out_specs=(_row_spec(tr, d), _vec_spec(d)),
        compiler_params=_params(),
    )(dx, m, gate)


def _norm_mod_bwd(x, dh, dres, scale, name):
    s, d = x.shape
    tr = _pick(s, 512, 8)

    def body(x_ref, dh_ref, dr_ref, sc_ref, dx_ref, dsc_ref, dsh_ref):
        n, r = _rms(x_ref[...])
        dh_ = dh_ref[...]
        _accumulate(dsc_ref, jnp.sum(dh_ * n, axis=0, keepdims=True))
        _accumulate(dsh_ref, jnp.sum(dh_, axis=0, keepdims=True))
        dx_ref[...] = dr_ref[...] + _rms_bwd(n, r, dh_ * (1.0 + sc_ref[...]))

    return pl.pallas_call(
        body, name=name, grid=(s // tr,),
        out_shape=(jax.ShapeDtypeStruct((s, d), F32), jax.ShapeDtypeStruct((1, d), F32), jax.ShapeDtypeStruct((1, d), F32)),
        in_specs=[_row_spec(tr, d), _row_spec(tr, d), _row_spec(tr, d), _vec_spec(d)],
        out_specs=(_row_spec(tr, d), _vec_spec(d), _vec_spec(d)), compiler_params=_params(),
    )(x, dh, dres, scale)


def _out_norm_bwd(oa, ob, dcat, g_a, g_b, name):
    s, w = oa.shape
    tr = _pick(s, 512, 8)

    def body(oa_ref, ob_ref, dc_ref, ga_ref, gb_ref, doa_ref, dob_ref, dga_ref, dgb_ref):
        for o_ref, g_ref, do_ref, dg_ref, lo in ((oa_ref, ga_ref, doa_ref, dga_ref, 0), (ob_ref, gb_ref, dob_ref, dgb_ref, w)):
            n, r = _rms(o_ref[...])
            dc = dc_ref[:, lo:lo + w]
            _accumulate(dg_ref, jnp.sum(dc * n, axis=0, keepdims=True))
            do_ref[...] = _rms_bwd(n, r, dc * g_ref[...])

    return pl.pallas_call(
        body, name=name, grid=(s // tr,),
        out_shape=(jax.ShapeDtypeStruct((s, w), F32), jax.ShapeDtypeStruct((s, w), F32),
                   jax.ShapeDtypeStruct((1, w), F32), jax.ShapeDtypeStruct((1, w), F32)),
        in_specs=[_row_spec(tr, w), _row_spec(tr, w), _row_spec(tr, 2 * w), _vec_spec(w), _vec_spec(w)],
        out_specs=(_row_spec(tr, w), _row_spec(tr, w), _vec_spec(w), _vec_spec(w)), compiler_params=_params(),
    )(oa, ob, dcat, g_a, g_b)


def _head_masks():
    lane = lax.broadcasted_iota(jnp.int32, (1, LANES), 1)
    return lane < HEAD_DIM, lane >= HEAD_DIM


def _nt(a, b):
    return lax.dot_general(a, b, (((1,), (1,)), ((), ())), preferred_element_type=F32)


def _tn(a, b):
    return lax.dot_general(a, b, (((0,), (0,)), ((), ())), preferred_element_type=F32)


def _nn(a, b):
    return jnp.dot(a, b, preferred_element_type=F32)


def _only(mask, v):
    return jnp.where(mask, v, jnp.zeros_like(v))


def _fill_padded(dst_ref, src_ref):
    dst_ref[0:PAD, :] = jnp.zeros((PAD, LANES), dst_ref.dtype)
    dst_ref[PAD:, :] = src_ref[...]


def _chunk_probs(s, bias, chunk):
    pos = lax.broadcasted_iota(jnp.int32, (1, BAND), 1)
    s = jnp.where(pos >= (N_PREV - chunk) * CHUNK, s + bias, -1e30)
    e = jnp.exp(s - jnp.max(s, axis=1, keepdims=True))
    return e / jnp.sum(e, axis=1, keepdims=True)


def _band_windows(i, cq, kpad, vpad):
    chunks = [i * cq + cc for cc in range(cq)]
    starts = [pl.multiple_of(ch * CHUNK, CHUNK) for ch in chunks]
    return chunks, starts, [kpad[pl.ds(st, BAND), :] for st in starts], [vpad[pl.ds(st, BAND), :] for st in starts]


def _attn_a_fwd(proj, band_bias, name):
    s = proj.shape[0]
    cq = 4
    tq = cq * CHUNK
    npair = N_HEADS // 2
    kcol, vcol = W_GRP // LANES, 2 * W_GRP // LANES

    def body(q_ref, k_ref, v_ref, b_ref, o_ref, kpad, vpad):
        i = pl.program_id(1)
        masks = _head_masks()

        @pl.when(i == 0)
        def _():
            _fill_padded(kpad, k_ref)
            _fill_padded(vpad, v_ref)

        chunks, _, kbs, vbs = _band_windows(i, cq, kpad, vpad)
        q2 = q_ref[...] * (HEAD_DIM ** -0.5)
        units = [(cc, h) for cc in range(cq) for h in range(2)]
        ss = [_nt(_only(masks[h], q2[cc * CHUNK:(cc + 1) * CHUNK]), kbs[cc]) for cc, h in units]
        ps = [_chunk_probs(s_, b_ref[h], chunks[cc]).astype(BF16) for s_, (cc, h) in zip(ss, units)]
        for cc in range(cq):
            o_ref[cc * CHUNK:(cc + 1) * CHUNK, :] = (_nn(ps[2 * cc], _only(masks[0], vbs[cc]))
                                                     + _nn(ps[2 * cc + 1], _only(masks[1], vbs[cc])))

    return pl.pallas_call(
        body, name=name, out_shape=jax.ShapeDtypeStruct((s, W_GRP), F32), grid=(npair, s // tq),
        in_specs=[pl.BlockSpec((tq, LANES), lambda p, i: (i, p)),
                  pl.BlockSpec((s, LANES), lambda p, i: (0, kcol + p)),
                  pl.BlockSpec((s, LANES), lambda p, i: (0, vcol + p)),
                  pl.BlockSpec((2, CHUNK, BAND), lambda p, i: (p, 0, 0))],
        out_specs=pl.BlockSpec((tq, LANES), lambda p, i: (i, p)),
        scratch_shapes=[pltpu.VMEM((s + PAD, LANES), BF16), pltpu.VMEM((s + PAD, LANES), BF16)],
        compiler_params=_params(),
    )(proj, proj, proj, band_bias)


def _attn_a_bwd(proj, band_bias, doa, name, rider=None):
    s = proj.shape[0]
    cq = 4
    tq = cq * CHUNK
    nq = s // tq
    npair = N_HEADS // 2
    kcol, vcol = W_GRP // LANES, 2 * W_GRP // LANES
    scale = HEAD_DIM ** -0.5

    def body(q_ref, k_ref, v_ref, b_ref, do_ref, dq_ref, dk_ref, dv_ref, db_ref, kpad, vpad, dkpad, dvpad):
        i = pl.program_id(1)
        masks = _head_masks()

        @pl.when(i == 0)
        def _():
            _fill_padded(kpad, k_ref)
            _fill_padded(vpad, v_ref)
            dkpad[...] = jnp.zeros_like(dkpad)
            dvpad[...] = jnp.zeros_like(dvpad)
            db_ref[...] = jnp.zeros_like(db_ref)

        chunks, starts, kbs, vbs = _band_windows(i, cq, kpad, vpad)
        q2 = q_ref[...] * scale
        do2 = do_ref[...].astype(BF16)
        units = [(cc, h) for cc in range(cq) for h in range(2)]
        qhs = [_only(masks[h], q2[cc * CHUNK:(cc + 1) * CHUNK]) for cc, h in units]
        dohs = [_only(masks[h], do2[cc * CHUNK:(cc + 1) * CHUNK]) for cc, h in units]
        ss = [_nt(qh, kbs[cc]) for qh, (cc, h) in zip(qhs, units)]
        dps = [_nt(doh, vbs[cc]) for doh, (cc, h) in zip(dohs, units)]
        ps = [_chunk_probs(s_, b_ref[h], chunks[cc]) for s_, (cc, h) in zip(ss, units)]
        dss = [p * (dp - jnp.sum(p * dp, axis=1, keepdims=True)) for p, dp in zip(ps, dps)]
        for h in range(2):
            db_ref[h] += functools.reduce(jnp.add, [dss[2 * cc + h] for cc in range(cq)])
        for cc in range(cq):
            u0, u1 = 2 * cc, 2 * cc + 1
            dsb = [dss[u0].astype(BF16), dss[u1].astype(BF16)]
            dq = _nn(dsb[0], _only(masks[0], kbs[cc])) + _nn(dsb[1], _only(masks[1], kbs[cc]))
            dq_ref[cc * CHUNK:(cc + 1) * CHUNK, :] = dq * scale
            dkpad[pl.ds(starts[cc], BAND), :] += _tn(jnp.concatenate(dsb, axis=0), jnp.concatenate([qhs[u0], qhs[u1]], axis=0))
            dvpad[pl.ds(starts[cc], BAND), :] += _tn(jnp.concatenate([ps[u0].astype(BF16), ps[u1].astype(BF16)], axis=0),
                                                     jnp.concatenate([dohs[u0], dohs[u1]], axis=0))

        @pl.when(i == nq - 1)
        def _():
            dk_ref[...] = dkpad[PAD:, :]
            dv_ref[...] = dvpad[PAD:, :]

    blk = pl.BlockSpec((tq, LANES), lambda p, i: (i, p))
    whole = pl.BlockSpec((s, LANES), lambda p, i: (0, p))
    bias_spec = pl.BlockSpec((2, CHUNK, BAND), lambda p, i: (p, 0, 0))
    return _call_with_rider(
        body, rider, name=name, grid=(npair, nq),
        out_shape=(jax.ShapeDtypeStruct((s, W_GRP), F32),) * 3 + (jax.ShapeDtypeStruct((N_HEADS, CHUNK, BAND), F32),),
        in_specs=[blk, pl.BlockSpec((s, LANES), lambda p, i: (0, kcol + p)),
                  pl.BlockSpec((s, LANES), lambda p, i: (0, vcol + p)), bias_spec, blk],
        out_specs=(blk, whole, whole, bias_spec),
        scratch_shapes=[pltpu.VMEM((s + PAD, LANES), BF16), pltpu.VMEM((s + PAD, LANES), BF16),
                        pltpu.VMEM((s + PAD, LANES), F32), pltpu.VMEM((s + PAD, LANES), F32)],
        args=(proj, proj, proj, band_bias, doa))


def _split3(v):
    hi = v.astype(BF16)
    r1 = v - hi.astype(F32)
    mid = r1.astype(BF16)
    lo = (r1 - mid.astype(F32)).astype(BF16)
    return hi, mid, lo


def _rel_bias_grad(dband_t, name):
    width = 3 * LANES

    def body(t_ref, o_ref):
        pos = lax.broadcasted_iota(jnp.int32, (BAND, width), 0)
        col = lax.broadcasted_iota(jnp.int32, (BAND, width), 1)
        acc = jnp.zeros((N_HEADS, width), F32)
        for q in range(CHUNK):
            idx = jnp.minimum(PAD + q - pos, REL_CLIP) + REL_CLIP
            onehot = (col == idx).astype(BF16)
            for part in _split3(t_ref[q]):
                acc = acc + _nn(part, onehot)
        o_ref[...] = acc

    return pl.pallas_call(
        body, name=name, out_shape=jax.ShapeDtypeStruct((N_HEADS, width), F32),
        in_specs=[VMEM_FULL], out_specs=VMEM_FULL, compiler_params=_params(),
    )(dband_t)


def _split2_wide(v):
    hi = v.astype(BF16)
    return jnp.concatenate([hi, (v - hi.astype(F32)).astype(BF16)], axis=1)


def _sb_logs(z, lower):
    e = jnp.exp(-jnp.abs(z))
    lb = jnp.minimum(z, 0.0) - jnp.log(1.0 + e)
    lk = lb - z
    if lower is not None:
        lk = jnp.where(lower, lk, 0.0)
    return z, e, lb, lk


def _tri_masks(tq):
    row = lax.broadcasted_iota(jnp.int32, (tq, tq), 0)
    col = lax.broadcasted_iota(jnp.int32, (tq, tq), 1)
    return row, col


def _stack2(m):
    return jnp.concatenate([m, m], axis=0).astype(BF16)


def _sb_fwd(proj, name, rider=None):
    s = proj.shape[0]
    tq = _pick(s, 256)
    nq = s // tq
    npair = N_HEADS // 2
    qcol, kcol, vcol = 3 * W_GRP // LANES, 4 * W_GRP // LANES, 5 * W_GRP // LANES

    assert nq % 2 == 0

    def body(q_ref, k_ref, v_ref, o_ref, l_ref):
        i = pl.program_id(1)
        masks = _head_masks()
        q2 = q_ref[...] * (HEAD_DIM ** -0.5)
        qs = [[_only(m, q2[c * tq:(c + 1) * tq]) for m in masks] for c in range(2)]
        row, col = _tri_masks(tq)
        lower = row > col
        after2 = _stack2(lower)

        def tile(kblock, chains, carry):
            accs, tails = [list(t) for t in carry[0]], [list(t) for t in carry[1]]
            ks = pl.multiple_of(kblock * tq, tq)
            kb = k_ref[pl.ds(ks, tq), :]
            vb = v_ref[pl.ds(ks, tq), :]
            units = [(c, h, diag) for c, diag in chains for h in range(2)]
            zs = [_nt(qs[c][h], kb) for c, h, _ in units]
            vh = [_only(masks[h], vb) for h in range(2)]
            lbs, lks, locs = [], [], []
            for z, (c, h, diag) in zip(zs, units):
                lb, lk = _sb_logs(z, lower if diag else None)[2:]
                lbs.append(lb)
                lks.append(lk)
                locs.append(_nn(_split2_wide(lk), after2))
            for lb, lk, loc, (c, h, diag) in zip(lbs, lks, locs, units):
                a = jnp.exp(lb + (loc + tails[c][h]))
                if diag:
                    a = jnp.where(lower, a, 0.0)
                accs[c][0] = accs[c][0] + _nn(a.astype(BF16), vh[h])
                tails[c][h] = tails[c][h] + (loc[:, 0:1] + lk[:, 0:1])
            return tuple(tuple(t) for t in accs), tuple(tuple(t) for t in tails)

        zero = jnp.zeros((tq, 1), F32)
        acc0 = jnp.zeros((tq, LANES), F32)
        carry = (((acc0,), (acc0,)), ((zero, zero), (zero, zero)))
        carry = tile(2 * i + 1, [(1, True)], carry)
        carry = tile(2 * i, [(0, True), (1, False)], carry)

        def alive(tails):
            return functools.reduce(jnp.maximum, [jnp.max(t) for ts in tails for t in ts]) > SB_DEAD

        def walk(state):
            jj, _, cr = state
            cr = tile(2 * i - jj, [(0, False), (1, False)], cr)
            return jj + 1, alive(cr[1]), cr

        jj, _, (accs, tails) = lax.while_loop(lambda st: jnp.logical_and(st[0] <= 2 * i, st[1]), walk,
                                              (jnp.int32(1), alive(carry[1]), carry))
        for c in range(2):
            o_ref[c * tq:(c + 1) * tq, :] = accs[c][0]
            l_ref[c * tq:(c + 1) * tq, 0:1] = tails[c][0]
            l_ref[c * tq:(c + 1) * tq, 1:2] = tails[c][1]
        l_ref[:, 2:3] = jnp.full((2 * tq, 1), (jj - 1).astype(F32))

    return _call_with_rider(
        body, rider, name=name, grid=(npair, nq // 2),
        out_shape=(jax.ShapeDtypeStruct((s, W_GRP), F32), jax.ShapeDtypeStruct((npair, s, 3), F32)),
        in_specs=[pl.BlockSpec((2 * tq, LANES), lambda p, i: (i, qcol + p)),
                  pl.BlockSpec((s, LANES), lambda p, i: (0, kcol + p)),
                  pl.BlockSpec((s, LANES), lambda p, i: (0, vcol + p))],
        out_specs=(pl.BlockSpec((2 * tq, LANES), lambda p, i: (i, p)),
                   pl.BlockSpec((None, 2 * tq, 3), lambda p, i: (p, i, 0))),
        scratch_shapes=[], args=(proj, proj, proj))


def _sb_bwd(proj, ltot, dob, name, rider=None):
    s = proj.shape[0]
    tq = _pick(s, 256)
    nq = s // tq
    npair = N_HEADS // 2
    qcol, kcol, vcol = 3 * W_GRP // LANES, 4 * W_GRP // LANES, 5 * W_GRP // LANES
    scale = HEAD_DIM ** -0.5

    def body(q_ref, k_ref, v_ref, l_ref, do_ref, dq_ref, dk_ref, dv_ref):
        i = pl.program_id(1)
        masks = _head_masks()

        @pl.when(i == 0)
        def _():
            dk_ref[...] = jnp.zeros_like(dk_ref)
            dv_ref[...] = jnp.zeros_like(dv_ref)

        q2 = q_ref[...] * scale
        do2 = do_ref[...]
        part = lambda v, c: v[c * tq:(c + 1) * tq]
        qs = [[_only(m, part(q2, c)) for m in masks] for c in range(2)]
        doh = [[_only(m, part(do2, c)).astype(BF16) for m in masks] for c in range(2)]
        ltots = [[l_ref[c * tq:(c + 1) * tq, h:h + 1] for h in range(2)] for c in range(2)]
        row, col = _tri_masks(tq)
        lower = row > col
        upto2 = _stack2(row <= col)
        before = (row < col).astype(BF16)

        def tile(kblock, chains, carry):
            dqs, heads, gsums = [[list(t) for t in part_] for part_ in carry]
            ks = pl.multiple_of(kblock * tq, tq)
            kb = k_ref[pl.ds(ks, tq), :]
            vb = v_ref[pl.ds(ks, tq), :]
            units = [(c, h, diag) for c, diag in chains for h in range(2)]
            zs = [_nt(qs[c][h], kb) for c, h, _ in units]
            das = [_nt(doh[c][h], vb) for c, h, _ in units]
            kh = [_only(masks[h], kb) for h in range(2)]
            sigs, lbs, locs = [], [], []
            for z_, (c, h, diag) in zip(zs, units):
                z, e, lb, lk = _sb_logs(z_, lower if diag else None)
                locs.append(_nn(_split2_wide(lk), upto2))
                r = 1.0 / (1.0 + e)
                sigs.append(jnp.where(z >= 0, r, e * r))
                lbs.append(lb)
            a_s, gs, glocs = [], [], []
            for lb, loc, da, (c, h, diag) in zip(lbs, locs, das, units):
                a = jnp.exp(lb + (ltots[c][h] - (heads[c][h] + loc)))
                if diag:
                    a = jnp.where(lower, a, 0.0)
                g = a * da
                glocs.append(_nn(g.astype(BF16), before))
                a_s.append(a.astype(BF16))
                gs.append(g)
            dzbs = []
            for g, sig, loc, gloc, (c, h, diag) in zip(gs, sigs, locs, glocs, units):
                dz = g - sig * (g + (gsums[c][h] + gloc))
                if diag:
                    dz = jnp.where(lower, dz, 0.0)
                dzb = dz.astype(BF16)
                dzbs.append(dzb)
                dqs[c][0] = dqs[c][0] + _nn(dzb, kh[h])
                heads[c][h] = heads[c][h] + loc[:, tq - 1:tq]
                gsums[c][h] = gsums[c][h] + (gloc[:, tq - 1:tq] + g[:, tq - 1:tq])
            stack = lambda vs: vs[0] if len(vs) == 1 else jnp.concatenate(vs, axis=0)
            dk_ref[pl.ds(ks, tq), :] += _tn(stack(dzbs), stack([qs[c][h] for c, h, _ in units]))
            dv_ref[pl.ds(ks, tq), :] += _tn(stack(a_s), stack([doh[c][h] for c, h, _ in units]))
            return tuple(tuple(tuple(t) for t in part_) for part_ in (dqs, heads, gsums))

        zero = jnp.zeros((tq, 1), F32)
        acc0 = jnp.zeros((tq, LANES), F32)
        carry = (((acc0,), (acc0,)), ((zero, zero), (zero, zero)), ((zero, zero), (zero, zero)))
        walked = jnp.clip(jnp.max(l_ref[0:8, 2:3]).astype(jnp.int32), 0, 2 * i)
        carry = lax.fori_loop(2 * i - walked, 2 * i, lambda j, cr: tile(j, [(0, False), (1, False)], cr), carry)
        carry = tile(2 * i, [(0, True), (1, False)], carry)
        dqs, _, _ = tile(2 * i + 1, [(1, True)], carry)
        for c in range(2):
            dq_ref[c * tq:(c + 1) * tq, :] = dqs[c][0] * scale

    blk = pl.BlockSpec((2 * tq, LANES), lambda p, i: (i, p))
    whole = pl.BlockSpec((s, LANES), lambda p, i: (0, p))
    return _call_with_rider(
        body, rider, name=name, grid=(npair, nq // 2), out_shape=(jax.ShapeDtypeStruct((s, W_GRP), F32),) * 3,
        in_specs=[pl.BlockSpec((2 * tq, LANES), lambda p, i: (i, qcol + p)),
                  pl.BlockSpec((s, LANES), lambda p, i: (0, kcol + p)),
                  pl.BlockSpec((s, LANES), lambda p, i: (0, vcol + p)),
                  pl.BlockSpec((None, 2 * tq, 3), lambda p, i: (p, i, 0)), blk],
        out_specs=(blk, whole, whole), scratch_shapes=[], args=(proj, proj, proj, ltot, dob))


def _ada_fwd(c_all, w_ada, b_ada, name):
    nl, d, n = w_ada.shape
    tn = _pick(n, 512)

    def body(c_ref, w_ref, b_ref, o_ref):
        cv = c_ref[...]
        act = (cv * jax.nn.sigmoid(cv)).astype(BF16)
        o_ref[...] = _nn(act, w_ref[...].astype(BF16)) + b_ref[...]

    return pl.pallas_call(
        body, name=name, out_shape=jax.ShapeDtypeStruct((nl, N_DEV, n), F32), grid=(nl, n // tn),
        in_specs=[pl.BlockSpec((N_DEV, d), lambda l, j: (0, 0)), pl.BlockSpec((None, d, tn), lambda l, j: (l, 0, j)),
                  pl.BlockSpec((None, 1, tn), lambda l, j: (l, 0, j))],
        out_specs=pl.BlockSpec((None, N_DEV, tn), lambda l, j: (l, 0, j)), compiler_params=_params(),
    )(c_all, w_ada, b_ada)


def _ada_bwd(c_all, dmod, name):
    nl, _, n = dmod.shape
    d = c_all.shape[1]
    tn = _pick(n, 512)

    def body(c_ref, g_ref, o_ref):
        cv = c_ref[...]
        act = (cv * jax.nn.sigmoid(cv)).astype(BF16)
        o_ref[...] = _tn(act, g_ref[...].astype(BF16))

    return pl.pallas_call(
        body, name=name, out_shape=jax.ShapeDtypeStruct((nl, d, n), F32), grid=(nl, n // tn),
        in_specs=[pl.BlockSpec((N_DEV, d), lambda l, j: (0, 0)), pl.BlockSpec((None, N_DEV, tn), lambda l, j: (l, 0, j))],
        out_specs=pl.BlockSpec((None, d, tn), lambda l, j: (l, 0, j)), compiler_params=_params(),
    )(c_all, dmod)


def _adamw(g, w, m, v, name):
    r, c = g.shape
    tr = _pick(r, 512, 8)
    c1 = 1.0 - ADAM_B1 ** ADAM_STEP
    c2 = 1.0 - ADAM_B2 ** ADAM_STEP

    def body(g_ref, w_ref, m_ref, v_ref, d_ref, nm_ref, nv_ref):
        gv = g_ref[...]
        nm = ADAM_B1 * m_ref[...] + (1.0 - ADAM_B1) * gv
        nv = ADAM_B2 * v_ref[...] + (1.0 - ADAM_B2) * (gv * gv)
        d_ref[...] = -ADAM_LR * ((nm / c1) / (jnp.sqrt(nv / c2) + ADAM_EPS) + ADAM_WD * w_ref[...])
        nm_ref[...] = nm
        nv_ref[...] = nv

    spec = pl.BlockSpec((tr, c), lambda i: (i, 0))
    return pl.pallas_call(
        body, name=name, out_shape=(jax.ShapeDtypeStruct((r, c), F32),) * 3, grid=(r // tr,),
        in_specs=[spec] * 4, out_specs=(spec,) * 3, compiler_params=_params(),
    )(g, w, m, v)


def _adamw_nd(g, w, m, v, name):
    shape = w.shape
    two_d = (1, shape[0]) if len(shape) == 1 else (-1, shape[-1])
    outs = _adamw(*(t.reshape(two_d) for t in (g, w, m, v)), name=name)
    return tuple(o.reshape(shape) for o in outs)


def _allgather8(v, name):
    m, n = v.shape

    def body(v_ref, out_ref, send_sems, recv_sems, local_sem):
        x, y, c = _my_place()

        def rows(px, py, pc):
            return out_ref.at[pl.ds(pl.multiple_of((4 * px + 2 * py + pc) * m, 8), m), :]

        def peer(k):
            return _flip(x, k & 4), _flip(y, k & 2), _flip(c, k & 1)

        def copy(k, block):
            return pltpu.make_async_remote_copy(
                src_ref=v_ref, dst_ref=rows(*block), send_sem=send_sems.at[k - 1], recv_sem=recv_sems.at[k - 1],
                device_id=peer(k), device_id_type=MESH)

        mine = pltpu.make_async_copy(v_ref, rows(x, y, c), local_sem)
        mine.start()
        sends = [copy(k, (x, y, c)) for k in range(1, N_DEV)]
        for cp in sends:
            cp.start()
        for k in range(1, N_DEV):
            copy(k, peer(k)).wait_recv()
        for cp in sends:
            cp.wait_send()
        mine.wait()

    return pl.pallas_call(
        body, name=name, out_shape=jax.ShapeDtypeStruct((N_DEV * m, n), v.dtype),
        in_specs=[VMEM_FULL], out_specs=VMEM_FULL,
        scratch_shapes=[pltpu.SemaphoreType.DMA((N_DEV - 1,)), pltpu.SemaphoreType.DMA((N_DEV - 1,)),
                        pltpu.SemaphoreType.DMA],
        compiler_params=_params(),
    )(v)


def _chip_peers(x, y, c):
    out = []
    for k in range(1, N_CHIPS):
        px, py = _flip(x, k & 2), _flip(y, k & 1)
        out.append((2 * px + py, (px, py, c)))
    return out


def _gather_weights(shards, kinds, name):
    nw = len(shards)

    def full_shape(a, kind):
        l, r, n = a.shape
        return (l, r, N_CHIPS * n) if kind == "col" else (l, N_CHIPS * r, n)

    def body(*refs):
        ins, outs = refs[:nw], refs[nw:2 * nw]
        send_sems, recv_sems, local_sems = refs[2 * nw:]
        x, y, c = _my_place()
        chip = 2 * x + y

        def window(w, j):
            _, r, n = shards[w].shape
            if kinds[w] == "col":
                return outs[w].at[:, :, pl.ds(pl.multiple_of(j * n, LANES), n)]
            return outs[w].at[:, pl.ds(pl.multiple_of(j * r, 16), r), :]

        def copy(w, k, j, peer):
            return pltpu.make_async_remote_copy(
                src_ref=ins[w], dst_ref=window(w, j), send_sem=send_sems.at[3 * w + k], recv_sem=recv_sems.at[3 * w + k],
                device_id=peer, device_id_type=MESH)

        local = [pltpu.make_async_copy(ins[w], window(w, chip), local_sems.at[w]) for w in range(nw)]
        for cp in local:
            cp.start()
        peers = _chip_peers(x, y, c)
        sends = [copy(w, k, chip, peer) for w in range(nw) for k, (_, peer) in enumerate(peers)]
        for cp in sends:
            cp.start()
        for w in range(nw):
            for k, (pchip, peer) in enumerate(peers):
                copy(w, k, pchip, peer).wait_recv()
        for cp in sends:
            cp.wait_send()
        for cp in local:
            cp.wait()

    return pl.pallas_call(
        body, name=name,
        out_shape=tuple(jax.ShapeDtypeStruct(full_shape(a, kd), a.dtype) for a, kd in zip(shards, kinds)),
        in_specs=[ANY] * nw, out_specs=(ANY,) * nw,
        scratch_shapes=[pltpu.SemaphoreType.DMA((3 * nw,)), pltpu.SemaphoreType.DMA((3 * nw,)),
                        pltpu.SemaphoreType.DMA((nw,))],
        compiler_params=_params(),
    )(*shards)


def _rs_to_sibling(grads, name):
    nw = len(grads)

    def body(*refs):
        ins, outs = refs[:nw], refs[nw:2 * nw]
        send_sems, recv_sems = refs[2 * nw:]
        x, y, c = _my_place()
        sibling = (x, y, 1 - c)
        copies = [pltpu.make_async_remote_copy(
            src_ref=ins[w].at[j, 1 - c], dst_ref=outs[w].at[j], send_sem=send_sems.at[N_CHIPS * w + j],
            recv_sem=recv_sems.at[N_CHIPS * w + j], device_id=sibling, device_id_type=MESH)
            for w in range(nw) for j in range(N_CHIPS)]
        for cp in copies:
            cp.start()
        for cp in copies:
            cp.wait_recv()
        for cp in copies:
            cp.wait_send()

    return pl.pallas_call(
        body, name=name,
        out_shape=tuple(jax.ShapeDtypeStruct((N_CHIPS,) + g.shape[2:], g.dtype) for g in grads),
        in_specs=[ANY] * nw, out_specs=(ANY,) * nw,
        scratch_shapes=[pltpu.SemaphoreType.DMA((N_CHIPS * nw,)), pltpu.SemaphoreType.DMA((N_CHIPS * nw,))],
        compiler_params=_params(),
    )(*grads)


def _rs_to_chips(parts, name):
    nw = len(parts)

    def body(*refs):
        ins, outs = refs[:nw], refs[nw:2 * nw]
        send_sems, recv_sems, local_sems = refs[2 * nw:]
        x, y, c = _my_place()
        chip = 2 * x + y
        peers = _chip_peers(x, y, c)

        def copy(w, k, src_slab, dst_slab, peer):
            return pltpu.make_async_remote_copy(
                src_ref=ins[w].at[src_slab], dst_ref=outs[w].at[dst_slab], send_sem=send_sems.at[3 * w + k],
                recv_sem=recv_sems.at[3 * w + k], device_id=peer, device_id_type=MESH)

        local = [pltpu.make_async_copy(ins[w].at[chip], outs[w].at[chip], local_sems.at[w]) for w in range(nw)]
        for cp in local:
            cp.start()
        sends = [copy(w, k, pchip, chip, peer) for w in range(nw) for k, (pchip, peer) in enumerate(peers)]
        for cp in sends:
            cp.start()
        for w in range(nw):
            for k, (pchip, peer) in enumerate(peers):
                copy(w, k, chip, pchip, peer).wait_recv()
        for cp in sends:
            cp.wait_send()
        for cp in local:
            cp.wait()

    return pl.pallas_call(
        body, name=name, out_shape=tuple(jax.ShapeDtypeStruct(p.shape, p.dtype) for p in parts),
        in_specs=[ANY] * nw, out_specs=(ANY,) * nw,
        scratch_shapes=[pltpu.SemaphoreType.DMA((3 * nw,)), pltpu.SemaphoreType.DMA((3 * nw,)),
                        pltpu.SemaphoreType.DMA((nw,))],
        compiler_params=_params(),
    )(*parts)


def _rs_share_halves(halves, name):
    nw = len(halves)
    nl = len(halves[0])
    flat = [h for hs in halves for h in hs]

    def body(*refs):
        ins, outs = refs[:nw * nl], refs[nw * nl:nw * nl + nw]
        send_sems, recv_sems, local_sems = refs[nw * nl + nw:]
        x, y, c = _my_place()
        sibling = (x, y, 1 - c)
        local, sends, recvs = [], [], []
        for w in range(nw):
            for l in range(nl):
                n = nl * w + l
                local.append(pltpu.make_async_copy(ins[n], outs[w].at[l, c], local_sems.at[n]))
                sends.append(pltpu.make_async_remote_copy(
                    src_ref=ins[n], dst_ref=outs[w].at[l, c], send_sem=send_sems.at[n], recv_sem=recv_sems.at[n],
                    device_id=sibling, device_id_type=MESH))
                recvs.append(pltpu.make_async_remote_copy(
                    src_ref=ins[n], dst_ref=outs[w].at[l, 1 - c], send_sem=send_sems.at[n], recv_sem=recv_sems.at[n],
                    device_id=sibling, device_id_type=MESH))
        for cp in local + sends:
            cp.start()
        for cp in recvs:
            cp.wait_recv()
        for cp in sends:
            cp.wait_send()
        for cp in local:
            cp.wait()

    return pl.pallas_call(
        body, name=name,
        out_shape=tuple(jax.ShapeDtypeStruct((nl, 2) + hs[0].shape, hs[0].dtype) for hs in halves),
        in_specs=[ANY] * (nw * nl), out_specs=(ANY,) * nw,
        scratch_shapes=[pltpu.SemaphoreType.DMA((nw * nl,)), pltpu.SemaphoreType.DMA((nw * nl,)),
                        pltpu.SemaphoreType.DMA((nw * nl,))],
        compiler_params=_params(),
    )(*flat)


def _add_own_half(grad, got, c_idx, name):
    _, _, r, n = grad.shape
    tr = _pick(r, 256, 8)

    def body(c_ref, g_ref, t_ref, o_ref):
        o_ref[...] = g_ref[...] + t_ref[...]

    return pl.pallas_call(
        body, name=name, out_shape=jax.ShapeDtypeStruct((N_CHIPS, r, n), F32),
        grid_spec=pltpu.PrefetchScalarGridSpec(
            num_scalar_prefetch=1, grid=(N_CHIPS, r // tr),
            in_specs=[pl.BlockSpec((None, None, tr, n), lambda j, i, c_ref: (j, c_ref[0], i, 0)),
                      pl.BlockSpec((None, tr, n), lambda j, i, c_ref: (j, i, 0))],
            out_specs=pl.BlockSpec((None, tr, n), lambda j, i, c_ref: (j, i, 0))),
        compiler_params=_params(),
    )(c_idx, grad, got)


def _sum_slabs(slabs, name):
    ns, r, n = slabs.shape
    tr = _pick(r, 256, 8)

    def body(s_ref, o_ref):
        acc = s_ref[0]
        for j in range(1, ns):
            acc = acc + s_ref[j]
        o_ref[...] = acc

    return pl.pallas_call(
        body, name=name, out_shape=jax.ShapeDtypeStruct((r, n), F32), grid=(r // tr,),
        in_specs=[pl.BlockSpec((ns, tr, n), lambda i: (0, i, 0))], out_specs=pl.BlockSpec((tr, n), lambda i: (i, 0)),
        compiler_params=_params(),
    )(slabs)


def _band_bias(rel_bias):
    h = rel_bias.shape[0]
    n_far = PAD - REL_CLIP + CHUNK
    far = jnp.broadcast_to(rel_bias[:, N_REL - 1:N_REL], (h, n_far))
    near = rel_bias[:, REL_CLIP - CHUNK + 1:N_REL - 1][:, ::-1]
    line = jnp.concatenate([far, near], axis=1)
    return jnp.stack([line[:, CHUNK - 1 - q:CHUNK - 1 - q + BAND] for q in range(CHUNK)], axis=1)


def _pack_rows(pieces):
    flat = jnp.concatenate([p.reshape(-1) for p in pieces])
    rows = -(-flat.shape[0] // (8 * LANES)) * 8
    return jnp.pad(flat, (0, rows * LANES - flat.shape[0])).reshape(rows, LANES)


def _unpack_rows(packed, shapes):
    flat = packed.reshape(-1)
    out, at = [], 0
    for shp in shapes:
        size = 1
        for n in shp:
            size *= n
        out.append(flat[at:at + size].reshape(shp))
        at += size
    return out


def _layer_fwd(x, mod, w, band, tag):
    s, d = x.shape
    row = lambda i: mod[i:i + 1]
    h1 = _norm_mod(x, row(1), row(0), f"norm_mix{tag}")
    proj = _matmul(h1, w["w_in"], form="nn", out_dtype=BF16, tm=_pick(s, 512), tn=_pick(w["w_in"].shape[1], 768),
                   tk=d, name=f"proj{tag}")
    oa = _attn_a_fwd(proj, band, f"attn_a{tag}")
    ob, ltot = _sb_fwd(proj, f"attn_b{tag}")
    cat = _out_norm(oa, ob, w["g_a"], w["g_b"], f"out_norm{tag}")
    mixed = _matmul(cat, w["w_out"], form="nn", out_dtype=F32, tm=_pick(s, 512), tn=_pick(d, 1024),
                    tk=cat.shape[1], name=f"mix_out{tag}")
    x1 = _residual(x, row(2), mixed, f"res_mix{tag}")
    h2 = _norm_mod(x1, row(4), row(3), f"norm_ffn{tag}")
    f2 = w["w_up"].shape[1]
    u = _matmul(h2, w["w_up"], form="nn", out_dtype=F32, tm=_pick(s, 512), tn=_pick(f2, 1408), tk=d, name=f"up{tag}",
                rider=up_rider)
    if up_rider is not None:
        u, arrived = u
        on_up_arrival(arrived)
    a = _conv_glu(u, w["conv_w"], w["conv_b"], f"conv_glu{tag}")
    f = _matmul(a, w["w_down"], form="nn", out_dtype=F32, tm=_pick(s, 512), tn=_pick(d, 1024),
                tk=_pick(f2 // 2, 2816), name=f"down{tag}")
    x2 = _residual(x1, row(5), f, f"res_ffn{tag}")
    saved = dict(x=x, h1=h1, proj=proj, oa=oa, ob=ob, ltot=ltot, cat=cat, mixed=mixed, x1=x1, h2=h2, u=u, a=a, f=f)
    return x2, saved


def _layer_bwd(dx2, sv, mod, w, band, tag):
    s, d = dx2.shape
    row = lambda i: mod[i:i + 1]
    f2 = w["w_up"].shape[1]
    ff = f2 // 2
    n_in = w["w_in"].shape[1]
    df, dgate_ffn = _gate_bwd(dx2, sv["f"], row(5), f"gate_ffn_bwd{tag}")
    da = _matmul(df, w["w_down"], form="nt", out_dtype=F32, tm=_pick(s, 512), tn=_pick(ff, 1408), tk=d, name=f"down_dx{tag}")
    g_down = _matmul(sv["a"], df, form="tn", out_dtype=F32, tm=_pick(ff, 1408), tn=_pick(d, 512), tk=_pick(s, 2048),
                     name=f"down_dw{tag}")
    if waiting is None:
        du2, dcw, dcb = _conv_glu_bwd(sv["u"], da, w["conv_w"], w["conv_b"], f"conv_glu_bwd{tag}")
    else:
        du2, dcw, dcb, older_from_sib = _conv_glu_bwd(sv["u"], da, w["conv_w"], w["conv_b"], f"conv_glu_bwd{tag}",
                                                      _sibling_rider(list(waiting.values())))
    dh2 = _matmul(du2, w["w_up"], form="nt", out_dtype=F32, tm=_pick(s, 512), tn=_pick(d, 1024), tk=_pick(ff, 2816),
                  name=f"up_dx{tag}", halves=True)
    g_up = _matmul(sv["h2"], du2, form="tn", out_dtype=F32, tm=_pick(d, 512), tn=_pick(f2 // N_CHIPS, 1408),
                   tk=_pick(s, 2048), name=f"up_dw{tag}", shard_cols=f2 // N_CHIPS, halves=True)
    dx1, dscale_ffn, dshift_ffn = _norm_mod_bwd(sv["x1"], dh2, dx2, row(4), f"norm_ffn_bwd{tag}")
    dmixed, dgate_mix = _gate_bwd(dx1, sv["mixed"], row(2), f"gate_mix_bwd{tag}")
    dcat = _matmul(dmixed, w["w_out"], form="nt", out_dtype=F32, tm=_pick(s, 512), tn=_pick(2 * W_GRP, 1024), tk=d,
                   name=f"mix_out_dx{tag}")
    g_out = _matmul(sv["cat"], dmixed, form="tn", out_dtype=F32, tm=_pick(2 * W_GRP, 512), tn=_pick(d, 1024),
                    tk=_pick(s, 2048), name=f"mix_out_dw{tag}")
    doa, dob, dg_a, dg_b = _out_norm_bwd(sv["oa"], sv["ob"], dcat, w["g_a"], w["g_b"], f"out_norm_bwd{tag}")
    dqa, dka, dva, dband = _attn_a_bwd(sv["proj"], band, doa, f"attn_a_bwd{tag}")
    dqb, dkb, dvb = _sb_bwd(sv["proj"], sv["ltot"], dob, f"attn_b_bwd{tag}")
    drel = _rel_bias_grad(jnp.transpose(dband, (1, 0, 2)), f"rel_bias_bwd{tag}")[:, :N_REL]
    dproj = jnp.concatenate([dqa, dka, dva, dqb, dkb, dvb], axis=1).astype(BF16)
    dh1 = _matmul(dproj, w["w_in"], form="nt", out_dtype=F32, tm=_pick(s, 512), tn=_pick(d, 1024), tk=_pick(n_in, 3072),
                  name=f"proj_dx{tag}")
    g_in = _matmul(sv["h1"], dproj, form="tn", out_dtype=F32, tm=_pick(d, 512), tn=_pick(n_in // N_CHIPS, 768),
                   tk=_pick(s, 2048), name=f"proj_dw{tag}", shard_cols=n_in // N_CHIPS)
    dx, dscale_mix, dshift_mix = _norm_mod_bwd(sv["x"], dh1, dx1, row(1), f"norm_mix_bwd{tag}")
    dmod = jnp.concatenate([dshift_mix, dscale_mix, dgate_mix, dshift_ffn, dscale_ffn, dgate_ffn], axis=1)
    big = dict(w_in=g_in, w_out=g_out, w_up=g_up, w_down=g_down)
    dconv_w = jnp.concatenate([dcw[0], dcw[1]], axis=1)
    dconv_b = jnp.concatenate([dcb[0], dcb[1]], axis=1)
    small = dict(dmod=dmod, rel_bias=drel, g_a=dg_a, g_b=dg_b, conv_w=dconv_w, conv_b=dconv_b)
    return dx, big, small


def _kernel_unoverlapped(x, c, w_ada, b_ada, w_in, rel_bias, g_a, g_b, w_out, w_up, conv_w, conv_b, w_down, final_g, loss_target, m_w_ada, m_b_ada, m_w_in, m_rel_bias, m_g_a, m_g_b, m_w_out, m_w_up, m_conv_w, m_conv_b, m_w_down, m_final_g, v_w_ada, v_b_ada, v_w_in, v_rel_bias, v_g_a, v_g_b, v_w_out, v_w_up, v_conv_w, v_conv_b, v_w_down, v_final_g):
    xi, yi, ci = _my_place()
    chip = 2 * xi + yi
    dev = 4 * xi + 2 * yi + ci
    nl, d, n_ada = w_ada.shape
    s = x.shape[1]
    f2 = N_CHIPS * w_up.shape[2]
    nc = conv_w.shape[2]

    c_pad = jnp.pad(c, ((0, 7), (0, 0)))
    c_all = _allgather8(c_pad, "gather_c")[0::8]
    b_mine = lax.dynamic_slice_in_dim(b_ada, chip * n_ada, n_ada, axis=1)[:, None, :]
    mod_shard = _ada_fwd(c_all, w_ada, b_mine, "ada")
    pack2 = _pack_rows([mod_shard, conv_w])
    got2 = _allgather8(pack2, "gather_mod").reshape(N_DEV, -1)
    mods, convs = [], []
    for j in range(N_CHIPS):
        ms, cw = _unpack_rows(got2[2 * j], [mod_shard.shape, conv_w.shape])
        mods.append(lax.dynamic_index_in_dim(ms, dev, axis=1, keepdims=False))
        convs.append(cw)
    mod = jnp.concatenate(mods, axis=1).reshape(nl, 6, d)
    conv_w_full = jnp.concatenate(convs, axis=2)

    names = ("w_in", "w_out", "w_up", "w_down")
    kinds = ("col", "row", "col", "row")
    shards = dict(w_in=w_in, w_out=w_out, w_up=w_up, w_down=w_down)
    full = _gather_weights([shards[n].astype(BF16) for n in names], kinds, "gather_weights")
    full = dict(zip(names, full))

    xs = x[0]
    layers, saved, bands = [], [], []
    for l in range(nl):
        w = {n: full[n][l] for n in names}
        w.update(g_a=g_a[l:l + 1], g_b=g_b[l:l + 1], conv_w=conv_w_full[l], conv_b=conv_b[l:l + 1])
        band = _band_bias(rel_bias[l])
        xs, sv = _layer_fwd(xs, mod[l], w, band, f"_l{l}")
        layers.append(w)
        bands.append(band)
        saved.append(sv)
    loss_part, dx, dfinal_g = _loss_head(xs, final_g[None, :], loss_target[0], "loss_head")
    loss = lax.psum(loss_part[0, 0], ("x", "y", "c"))

    big, small = [None] * nl, [None] * nl
    for l in reversed(range(nl)):
        dx, big[l], small[l] = _layer_bwd(dx, saved[l], mod[l], layers[l], bands[l], f"_l{l}")

    small_names = ("dmod", "rel_bias", "g_a", "g_b", "conv_w", "conv_b")
    pieces = [small[l][n] for l in range(nl) for n in small_names] + [dfinal_g]
    shapes = [p.shape for p in pieces]
    pack3 = _pack_rows(pieces)
    got3 = _allgather8(pack3, "gather_small").reshape(N_DEV, pack3.shape[0], LANES)
    summed = _unpack_rows(_sum_slabs(got3, "sum_small"), shapes)
    tot = [dict(zip(small_names, summed[len(small_names) * l:len(small_names) * (l + 1)])) for l in range(nl)]
    g_final_g = summed[-1].reshape(-1)
    g_b_ada = jnp.stack([tot[l]["dmod"].reshape(-1) for l in range(nl)])
    g_rel = jnp.stack([tot[l]["rel_bias"] for l in range(nl)])
    g_ga = jnp.stack([tot[l]["g_a"].reshape(-1) for l in range(nl)])
    g_gb = jnp.stack([tot[l]["g_b"].reshape(-1) for l in range(nl)])
    g_conv_b = jnp.stack([tot[l]["conv_b"].reshape(-1) for l in range(nl)])
    g_conv_w = jnp.stack([lax.dynamic_slice_in_dim(tot[l]["conv_w"], chip * nc, nc, axis=1) for l in range(nl)])
    per_dev = [_unpack_rows(got3[j], shapes) for j in range(N_DEV)]
    dmod_all = jnp.stack([jnp.stack([per_dev[j][len(small_names) * l].reshape(-1) for j in range(N_DEV)])
                          for l in range(nl)])
    g_w_ada = _ada_bwd(c_all, lax.dynamic_slice_in_dim(dmod_all, chip * n_ada, n_ada, axis=2), "ada_bwd")

    order = [(n, l) for n in names for l in range(nl)]
    flat_g = [big[l][n].reshape(N_CHIPS, 2, -1, 1024) for n, l in order]
    from_sib = _rs_to_sibling(flat_g, "rs_sibling")
    c_idx = jnp.reshape(ci, (1,)).astype(jnp.int32)
    chip_part = [_add_own_half(g, t, c_idx, f"rs_add_{n}_l{l}") for g, t, (n, l) in zip(flat_g, from_sib, order)]
    from_chips = _rs_to_chips(chip_part, "rs_chips")
    my_half = [_sum_slabs(t, f"rs_sum_{n}_l{l}") for t, (n, l) in zip(from_chips, order)]
    shard_g = _rs_share_halves([[my_half[nl * i + l] for l in range(nl)] for i in range(len(names))], "rs_halves")
    g_big = {n: shard_g[i].reshape(shards[n].shape) for i, n in enumerate(names)}

    grads = dict(w_ada=g_w_ada, b_ada=g_b_ada, rel_bias=g_rel, g_a=g_ga, g_b=g_gb, conv_w=g_conv_w, conv_b=g_conv_b,
                 final_g=g_final_g)
    weights = dict(w_ada=w_ada, b_ada=b_ada, w_in=w_in, rel_bias=rel_bias, g_a=g_a, g_b=g_b, w_out=w_out, w_up=w_up,
                   conv_w=conv_w, conv_b=conv_b, w_down=w_down, final_g=final_g)
    m_in = dict(w_ada=m_w_ada, b_ada=m_b_ada, w_in=m_w_in, rel_bias=m_rel_bias, g_a=m_g_a, g_b=m_g_b, w_out=m_w_out,
                w_up=m_w_up, conv_w=m_conv_w, conv_b=m_conv_b, w_down=m_w_down, final_g=m_final_g)
    v_in = dict(w_ada=v_w_ada, b_ada=v_b_ada, w_in=v_w_in, rel_bias=v_rel_bias, g_a=v_g_a, g_b=v_g_b, w_out=v_w_out,
                w_up=v_w_up, conv_w=v_conv_w, conv_b=v_conv_b, w_down=v_w_down, final_g=v_final_g)
    order_w = ("w_ada", "b_ada", "w_in", "rel_bias", "g_a", "g_b", "w_out", "w_up", "conv_w", "conv_b", "w_down", "final_g")
    upd = {n: _adamw_nd(grads[n], weights[n], m_in[n], v_in[n], f"adamw_{n}") for n in grads}
    for n, mn, th in zip(BIG, mine, theirs):
        grads[n], *upd[n] = _adamw_halves(mn, th, c_idx, weights[n], m_in[n], v_in[n], f"adamw_{n}")
    return (loss, dx[None], *[grads[n] for n in order_w], *[upd[n][0] for n in order_w],
            *[upd[n][1] for n in order_w], *[upd[n][2] for n in order_w])


class _Rider:
    def __init__(self, ins, out_shapes, n_remote, n_local, parts):
        self.ins = list(ins)
        self.out_shapes = list(out_shapes)
        self.scratch = [pltpu.SemaphoreType.DMA((n_remote,)), pltpu.SemaphoreType.DMA((n_remote,)),
                        pltpu.SemaphoreType.DMA((max(n_local, 1),))]
        self.parts = parts

    def start(self, in_refs, out_refs, sems):
        local, sends, _ = self.parts(in_refs, out_refs, sems)
        for cp in local() + sends():
            cp.start()

    def wait(self, in_refs, out_refs, sems):
        local, sends, recvs = self.parts(in_refs, out_refs, sems)
        for cp in recvs():
            cp.wait_recv()
        for cp in sends():
            cp.wait_send()
        for cp in local():
            cp.wait()


class _JoinedRider:
    def __init__(self, riders):
        self.riders = riders
        self.ins = [a for r in riders for a in r.ins]
        self.out_shapes = [o for r in riders for o in r.out_shapes]
        self.scratch = [sc for r in riders for sc in r.scratch]

    def _each(self, in_refs, out_refs, sems):
        i = o = sc = 0
        for r in self.riders:
            yield (r, in_refs[i:i + len(r.ins)], out_refs[o:o + len(r.out_shapes)], sems[sc:sc + len(r.scratch)])
            i, o, sc = i + len(r.ins), o + len(r.out_shapes), sc + len(r.scratch)

    def start(self, in_refs, out_refs, sems):
        for r, i, o, sc in self._each(in_refs, out_refs, sems):
            r.start(i, o, sc)

    def wait(self, in_refs, out_refs, sems):
        for r, i, o, sc in self._each(in_refs, out_refs, sems):
            r.wait(i, o, sc)

    def split(self, results):
        out, at = [], 0
        for r in self.riders:
            out.append(list(results[at:at + len(r.out_shapes)]))
            at += len(r.out_shapes)
        return out


def _call_with_rider(body, rider, *, name, grid, out_shape, in_specs, out_specs, scratch_shapes, args):
    if rider is None:
        return pl.pallas_call(body, name=name, grid=grid, out_shape=tuple(out_shape), in_specs=list(in_specs),
                              out_specs=tuple(out_specs), scratch_shapes=list(scratch_shapes),
                              compiler_params=_params())(*args)
    n_in, n_out, n_scr = len(in_specs), len(out_specs), len(scratch_shapes)
    r_in, r_out = len(rider.ins), len(rider.out_shapes)

    def both(*refs):
        at = 0
        groups = []
        for size in (n_in, r_in, n_out, r_out, n_scr, len(rider.scratch)):
            groups.append(refs[at:at + size])
            at += size
        own_in, ride_in, own_out, ride_out, own_scr, sems = groups
        steps = [pl.program_id(a) for a in range(len(grid))]
        first = functools.reduce(jnp.logical_and, [st == 0 for st in steps])
        last = functools.reduce(jnp.logical_and, [st == g - 1 for st, g in zip(steps, grid)])

        @pl.when(first)
        def _():
            rider.start(ride_in, ride_out, sems)

        body(*own_in, *own_out, *own_scr)

        @pl.when(last)
        def _():
            rider.wait(ride_in, ride_out, sems)

    outs = pl.pallas_call(
        both, name=name, grid=grid, out_shape=tuple(out_shape) + tuple(rider.out_shapes),
        in_specs=list(in_specs) + [ANY] * r_in, out_specs=tuple(out_specs) + (ANY,) * r_out,
        scratch_shapes=list(scratch_shapes) + rider.scratch, compiler_params=_params(),
    )(*args, *rider.ins)
    return tuple(outs[:n_out]) + (list(outs[n_out:]),)


def _run_rider(rider, name):
    r_in, r_out = len(rider.ins), len(rider.out_shapes)

    def body(*refs):
        ins, outs, sems = refs[:r_in], refs[r_in:r_in + r_out], refs[r_in + r_out:]
        rider.start(ins, outs, sems)
        rider.wait(ins, outs, sems)

    return list(pl.pallas_call(
        body, name=name, out_shape=tuple(rider.out_shapes), in_specs=[ANY] * r_in, out_specs=(ANY,) * r_out,
        scratch_shapes=rider.scratch, compiler_params=_params(),
    )(*rider.ins))


def _remote(src, dst, sems, n, peer):
    return pltpu.make_async_remote_copy(src_ref=src, dst_ref=dst, send_sem=sems[0].at[n], recv_sem=sems[1].at[n],
                                        device_id=peer, device_id_type=MESH)


def _gather_rider(shards, kinds):
    nw = len(shards)
    out_shapes = [jax.ShapeDtypeStruct((a.shape[0], N_CHIPS * a.shape[1]) if kd == "col" else
                                       (N_CHIPS * a.shape[0], a.shape[1]), a.dtype) for a, kd in zip(shards, kinds)]

    def parts(ins, outs, sems):
        x, y, c = _my_place()
        chip = 2 * x + y
        peers = _chip_peers(x, y, c)

        def window(w, j):
            r, n = shards[w].shape
            if kinds[w] == "col":
                return outs[w].at[:, pl.ds(pl.multiple_of(j * n, LANES), n)]
            return outs[w].at[pl.ds(pl.multiple_of(j * r, 16), r), :]

        local = lambda: [pltpu.make_async_copy(ins[w], window(w, chip), sems[2].at[w]) for w in range(nw)]
        sends = lambda: [_remote(ins[w], window(w, chip), sems, 3 * w + k, peer)
                         for w in range(nw) for k, (_, peer) in enumerate(peers)]
        recvs = lambda: [_remote(ins[w], window(w, pchip), sems, 3 * w + k, peer)
                         for w in range(nw) for k, (pchip, peer) in enumerate(peers)]
        return local, sends, recvs

    return _Rider(shards, out_shapes, 3 * nw, nw, parts)


def _half(ref3, h, rows):
    return ref3.at[:, pl.ds(pl.multiple_of(h * rows, 8), rows), :]


def _sibling_rider(grads):
    nw = len(grads)
    out_shapes = [jax.ShapeDtypeStruct((g.shape[0], g.shape[1] // 2, g.shape[2]), g.dtype) for g in grads]

    def parts(ins, outs, sems):
        x, y, c = _my_place()
        sibling = (x, y, 1 - c)
        copies = lambda: [_remote(_half(ins[w], 1 - c, grads[w].shape[1] // 2), outs[w], sems, w, sibling)
                          for w in range(nw)]
        return (lambda: []), copies, copies

    return _Rider(grads, out_shapes, nw, 0, parts)


def _chips_rider(parts_in):
    nw = len(parts_in)
    out_shapes = [jax.ShapeDtypeStruct(p.shape, p.dtype) for p in parts_in]

    def parts(ins, outs, sems):
        x, y, c = _my_place()
        chip = 2 * x + y
        peers = _chip_peers(x, y, c)
        local = lambda: []
        sends = lambda: [_remote(ins[w].at[pchip], outs[w].at[chip], sems, 3 * w + k, peer)
                         for w in range(nw) for k, (pchip, peer) in enumerate(peers)]
        recvs = lambda: [_remote(ins[w].at[chip], outs[w].at[pchip], sems, 3 * w + k, peer)
                         for w in range(nw) for k, (pchip, peer) in enumerate(peers)]
        return local, sends, recvs

    return _Rider(parts_in, out_shapes, 3 * nw, nw, parts)


def _halves_rider(halves):
    nw, nl = len(halves), len(halves[0])
    flat = [h for hs in halves for h in hs]
    out_shapes = [jax.ShapeDtypeStruct((nl, 2 * hs[0].shape[0], hs[0].shape[1]), hs[0].dtype) for hs in halves]

    def parts(ins, outs, sems):
        x, y, c = _my_place()
        sibling = (x, y, 1 - c)

        def window(w, l, h):
            rows = halves[w][0].shape[0]
            return outs[w].at[l, pl.ds(pl.multiple_of(h * rows, 8), rows), :]

        pairs = [(w, l) for w in range(nw) for l in range(nl)]
        local = lambda: [pltpu.make_async_copy(ins[nl * w + l], window(w, l, c), sems[2].at[nl * w + l]) for w, l in pairs]
        sends = lambda: [_remote(ins[nl * w + l], window(w, l, c), sems, nl * w + l, sibling) for w, l in pairs]
        recvs = lambda: [_remote(ins[nl * w + l], window(w, l, 1 - c), sems, nl * w + l, sibling) for w, l in pairs]
        return local, sends, recvs

    return _Rider(flat, out_shapes, nw * nl, nw * nl, parts)


def _add_my_half(grad, got, c_idx, name):
    _, r, n = got.shape
    tr = _pick(r, 256, 16)
    nblk = r // tr

    def body(c_ref, g_ref, t_ref, o_ref, ob_ref):
        tot = g_ref[...] + t_ref[...]
        o_ref[...] = tot
        ob_ref[...] = tot.astype(BF16)

    blk = pl.BlockSpec((None, tr, n), lambda j, i, c_ref: (j, i, 0))
    return pl.pallas_call(
        body, name=name, out_shape=(jax.ShapeDtypeStruct(got.shape, F32), jax.ShapeDtypeStruct(got.shape, BF16)),
        grid_spec=pltpu.PrefetchScalarGridSpec(
            num_scalar_prefetch=1, grid=(N_CHIPS, nblk),
            in_specs=[pl.BlockSpec((None, tr, n), lambda j, i, c_ref: (j, c_ref[0] * nblk + i, 0)), blk],
            out_specs=(blk, blk)),
        compiler_params=_params(),
    )(c_idx, grad, got)


def _swap_rider(mine):
    nw = len(mine)
    out_shapes = [jax.ShapeDtypeStruct(a.shape, a.dtype) for a in mine]

    def parts(ins, outs, sems):
        x, y, c = _my_place()
        copies = lambda: [_remote(ins[w], outs[w], sems, w, (x, y, 1 - c)) for w in range(nw)]
        return (lambda: []), copies, copies

    return _Rider(mine, out_shapes, nw, 0, parts)


def _sum_layers(own, got, chip_idx, name):
    nl = len(own)
    _, r, n = own[0].shape
    tr = _pick(r, 256, 16)

    def body(chip_ref, *refs):
        o_ref = refs[4 * nl]
        for l in range(nl):
            acc = refs[4 * l][...]
            for k in range(1, N_CHIPS):
                acc = acc + refs[4 * l + k][...].astype(F32)
            o_ref[l] = acc

    slab = lambda k: pl.BlockSpec((None, tr, n), lambda i, chip_ref: (chip_ref[0] ^ k, i, 0))
    return pl.pallas_call(
        body, name=name, out_shape=jax.ShapeDtypeStruct((nl, r, n), F32),
        grid_spec=pltpu.PrefetchScalarGridSpec(
            num_scalar_prefetch=1, grid=(r // tr,), in_specs=[slab(k) for _ in range(nl) for k in range(N_CHIPS)],
            out_specs=pl.BlockSpec((nl, tr, n), lambda i, chip_ref: (0, i, 0))),
        compiler_params=_params(),
    )(chip_idx, *[a for l in range(nl) for a in (own[l], got[l], got[l], got[l])])


def _adam_math(gv, w, m, v):
    c1 = 1.0 - ADAM_B1 ** ADAM_STEP
    c2 = 1.0 - ADAM_B2 ** ADAM_STEP
    nm = ADAM_B1 * m + (1.0 - ADAM_B1) * gv
    nv = ADAM_B2 * v + (1.0 - ADAM_B2) * (gv * gv)
    return -ADAM_LR * ((nm / c1) / (jnp.sqrt(nv / c2) + ADAM_EPS) + ADAM_WD * w), nm, nv


def _adamw_halves(mine, theirs, c_idx, w, m, v, name):
    nl, r, n = mine.shape
    tr = _pick(r, 256, 8)
    nblk = r // tr

    def body(c_ref, mine_ref, theirs_ref, w_ref, m_ref, v_ref, g_ref, d_ref, nm_ref, nv_ref):
        gv = jnp.where(pl.program_id(1) == c_ref[0], mine_ref[...], theirs_ref[...])
        g_ref[...] = gv
        d_ref[...], nm_ref[...], nv_ref[...] = _adam_math(gv, w_ref[...], m_ref[...], v_ref[...])

    half = pl.BlockSpec((None, tr, n), lambda l, h, i, c_ref: (l, i, 0))
    full = pl.BlockSpec((None, tr, n), lambda l, h, i, c_ref: (l, h * nblk + i, 0))
    return pl.pallas_call(
        body, name=name, out_shape=(jax.ShapeDtypeStruct(w.shape, F32),) * 4,
        grid_spec=pltpu.PrefetchScalarGridSpec(
            num_scalar_prefetch=1, grid=(nl, 2, nblk), in_specs=[half, half, full, full, full],
            out_specs=(full,) * 4),
        compiler_params=_params(),
    )(c_idx, mine, theirs, w, m, v)


def _by_chip(g):
    return g if g.ndim == 3 else g.reshape(N_CHIPS, g.shape[0] // N_CHIPS, g.shape[1])


BIG = ("w_in", "w_out", "w_up", "w_down")
BIG_KIND = dict(w_in="col", w_out="row", w_up="col", w_down="row")


def _forward_layer(x, mod, w, band, tag, rider=None, on_arrival=None, up_rider=None, on_up_arrival=None):
    s, d = x.shape
    row = lambda i: mod[i:i + 1]
    h1 = _norm_mod(x, row(1), row(0), f"norm_mix{tag}")
    proj = _matmul(h1, w["w_in"], form="nn", out_dtype=BF16, tm=_pick(s, 512), tn=_pick(w["w_in"].shape[1], 768),
                   tk=d, name=f"proj{tag}")
    oa = _attn_a_fwd(proj, band, f"attn_a{tag}")
    if rider is None:
        ob, ltot = _sb_fwd(proj, f"attn_b{tag}")
    else:
        ob, ltot, arrived = _sb_fwd(proj, f"attn_b{tag}", rider)
        on_arrival(arrived)
    cat = _out_norm(oa, ob, w["g_a"], w["g_b"], f"out_norm{tag}")
    mixed = _matmul(cat, w["w_out"], form="nn", out_dtype=F32, tm=_pick(s, 512), tn=_pick(d, 1024),
                    tk=cat.shape[1], name=f"mix_out{tag}")
    x1 = _residual(x, row(2), mixed, f"res_mix{tag}")
    h2 = _norm_mod(x1, row(4), row(3), f"norm_ffn{tag}")
    f2 = w["w_up"].shape[1]
    u = _matmul(h2, w["w_up"], form="nn", out_dtype=F32, tm=_pick(s, 512), tn=_pick(f2, 1408), tk=d, name=f"up{tag}",
                rider=up_rider)
    if up_rider is not None:
        u, arrived = u
        on_up_arrival(arrived)
    a = _conv_glu(u, w["conv_w"], w["conv_b"], f"conv_glu{tag}")
    f = _matmul(a, w["w_down"], form="nn", out_dtype=F32, tm=_pick(s, 512), tn=_pick(d, 1024),
                tk=_pick(f2 // 2, 2816), name=f"down{tag}")
    x2 = _residual(x1, row(5), f, f"res_ffn{tag}")
    saved = dict(x=x, h1=h1, proj=proj, oa=oa, ob=ob, ltot=ltot, cat=cat, mixed=mixed, x1=x1, h2=h2, u=u, a=a, f=f)
    return x2, saved


def _backward_layer(dx2, sv, mod, w, band, tag, c_idx, waiting=None):
    s, d = dx2.shape
    row = lambda i: mod[i:i + 1]
    f2 = w["w_up"].shape[1]
    ff = f2 // 2
    n_in = w["w_in"].shape[1]
    df, dgate_ffn = _gate_bwd(dx2, sv["f"], row(5), f"gate_ffn_bwd{tag}")
    da = _matmul(df, w["w_down"], form="nt", out_dtype=F32, tm=_pick(s, 512), tn=_pick(ff, 1408), tk=d, name=f"down_dx{tag}")
    g_down = _matmul(sv["a"], df, form="tn", out_dtype=F32, tm=_pick(ff, 1408), tn=_pick(d, 512), tk=_pick(s, 2048),
                     name=f"down_dw{tag}")
    if waiting is None:
        du2, dcw, dcb = _conv_glu_bwd(sv["u"], da, w["conv_w"], w["conv_b"], f"conv_glu_bwd{tag}")
    else:
        du2, dcw, dcb, older_from_sib = _conv_glu_bwd(sv["u"], da, w["conv_w"], w["conv_b"], f"conv_glu_bwd{tag}",
                                                      _sibling_rider(list(waiting.values())))
    dh2 = _matmul(du2, w["w_up"], form="nt", out_dtype=F32, tm=_pick(s, 512), tn=_pick(d, 1024), tk=_pick(ff, 2816),
                  name=f"up_dx{tag}", halves=True)
    g_up = _matmul(sv["h2"], du2, form="tn", out_dtype=F32, tm=_pick(d, 512), tn=_pick(f2 // N_CHIPS, 1408),
                   tk=_pick(s, 2048), name=f"up_dw{tag}", shard_cols=f2 // N_CHIPS, halves=True)
    dx1, dscale_ffn, dshift_ffn = _norm_mod_bwd(sv["x1"], dh2, dx2, row(4), f"norm_ffn_bwd{tag}")
    dmixed, dgate_mix = _gate_bwd(dx1, sv["mixed"], row(2), f"gate_mix_bwd{tag}")
    dcat = _matmul(dmixed, w["w_out"], form="nt", out_dtype=F32, tm=_pick(s, 512), tn=_pick(2 * W_GRP, 1024), tk=d,
                   name=f"mix_out_dx{tag}")
    g_out = _matmul(sv["cat"], dmixed, form="tn", out_dtype=F32, tm=_pick(2 * W_GRP, 512), tn=_pick(d, 1024),
                    tk=_pick(s, 2048), name=f"mix_out_dw{tag}")
    doa, dob, dg_a, dg_b = _out_norm_bwd(sv["oa"], sv["ob"], dcat, w["g_a"], w["g_b"], f"out_norm_bwd{tag}")
    big = {("w_down", tag): _by_chip(g_down), ("w_up", tag): _by_chip(g_up), ("w_out", tag): _by_chip(g_out)}
    if waiting is None:
        dqa, dka, dva, dband = _attn_a_bwd(sv["proj"], band, doa, f"attn_a_bwd{tag}")
        dqb, dkb, dvb = _sb_bwd(sv["proj"], sv["ltot"], dob, f"attn_b_bwd{tag}")
    else:
        old_keys, keys = list(waiting), list(big)
        add = lambda raw, k, t: _add_my_half(raw[k], t, c_idx, f"rs_add_{k[0]}{k[1]}")
        old_parts = [add(waiting, k, t) for k, t in zip(old_keys, older_from_sib)]
        both = _JoinedRider([_chips_rider([p16 for _, p16 in old_parts]), _sibling_rider([big[k] for k in keys])])
        dqa, dka, dva, dband, arrived = _attn_a_bwd(sv["proj"], band, doa, f"attn_a_bwd{tag}", both)
        old_got, from_sib = both.split(arrived)
        parts = [add(big, k, t) for k, t in zip(keys, from_sib)]
        dqb, dkb, dvb, got = _sb_bwd(sv["proj"], sv["ltot"], dob, f"attn_b_bwd{tag}",
                                     _chips_rider([p16 for _, p16 in parts]))
        big = {k: (p32, g) for k, (p32, _), g in zip(old_keys + keys, old_parts + parts, list(old_got) + list(got))}
    drel = _rel_bias_grad(jnp.transpose(dband, (1, 0, 2)), f"rel_bias_bwd{tag}")[:, :N_REL]
    dproj = jnp.concatenate([dqa, dka, dva, dqb, dkb, dvb], axis=1).astype(BF16)
    dh1 = _matmul(dproj, w["w_in"], form="nt", out_dtype=F32, tm=_pick(s, 512), tn=_pick(d, 1024), tk=_pick(n_in, 3072),
                  name=f"proj_dx{tag}")
    g_in = _matmul(sv["h1"], dproj, form="tn", out_dtype=F32, tm=_pick(d, 512), tn=_pick(n_in // N_CHIPS, 768),
                   tk=_pick(s, 2048), name=f"proj_dw{tag}", shard_cols=n_in // N_CHIPS)
    dx, dscale_mix, dshift_mix = _norm_mod_bwd(sv["x"], dh1, dx1, row(1), f"norm_mix_bwd{tag}")
    dmod = jnp.concatenate([dshift_mix, dscale_mix, dgate_mix, dshift_ffn, dscale_ffn, dgate_ffn], axis=1)
    dconv_w = jnp.concatenate([dcw[0], dcw[1]], axis=1)
    dconv_b = jnp.concatenate([dcb[0], dcb[1]], axis=1)
    small = dict(dmod=dmod, rel_bias=drel, g_a=dg_a, g_b=dg_b, conv_w=dconv_w, conv_b=dconv_b)
    return dx, big, g_in, small


def kernel(x, c, w_ada, b_ada, w_in, rel_bias, g_a, g_b, w_out, w_up, conv_w, conv_b, w_down, final_g, loss_target, m_w_ada, m_b_ada, m_w_in, m_rel_bias, m_g_a, m_g_b, m_w_out, m_w_up, m_conv_w, m_conv_b, m_w_down, m_final_g, v_w_ada, v_b_ada, v_w_in, v_rel_bias, v_g_a, v_g_b, v_w_out, v_w_up, v_conv_w, v_conv_b, v_w_down, v_final_g):
    xi, yi, ci = _my_place()
    chip = 2 * xi + yi
    dev = 4 * xi + 2 * yi + ci
    c_idx = jnp.reshape(ci, (1,)).astype(jnp.int32)
    nl, d, n_ada = w_ada.shape
    nc = conv_w.shape[2]
    assert nl == 2

    c_pad = jnp.pad(c, ((0, 7), (0, 0)))
    c_all = _allgather8(c_pad, "gather_c")[0::8]
    b_mine = lax.dynamic_slice_in_dim(b_ada, chip * n_ada, n_ada, axis=1)[:, None, :]
    mod_shard = _ada_fwd(c_all, w_ada, b_mine, "ada")
    pack2 = _pack_rows([mod_shard, conv_w])
    got2 = _allgather8(pack2, "gather_mod").reshape(N_DEV, -1)
    mods, convs = [], []
    for j in range(N_CHIPS):
        ms, cw = _unpack_rows(got2[2 * j], [mod_shard.shape, conv_w.shape])
        mods.append(lax.dynamic_index_in_dim(ms, dev, axis=1, keepdims=False))
        convs.append(cw)
    mod = jnp.concatenate(mods, axis=1).reshape(nl, 6, d)
    conv_w_full = jnp.concatenate(convs, axis=2)

    shards = dict(w_in=w_in, w_out=w_out, w_up=w_up, w_down=w_down)
    sh = {(n, l): shards[n][l].astype(BF16) for n in BIG for l in range(nl)}
    early = [("w_in", 0)]
    riding = [[("w_out", 0), ("w_up", 0), ("w_down", 0)], [("w_out", 1), ("w_up", 1), ("w_down", 1)]]
    riding_up = [[("w_in", 1)], None]
    layers = [dict(g_a=g_a[l:l + 1], g_b=g_b[l:l + 1], conv_w=conv_w_full[l], conv_b=conv_b[l:l + 1]) for l in range(nl)]

    def gather_rider(keys):
        return _gather_rider([sh[k] for k in keys], [BIG_KIND[k[0]] for k in keys])

    def arrival(keys):
        def fill(arrived):
            for (n, l), full in zip(keys, arrived):
                layers[l][n] = full
        return fill

    arrival(early)(_run_rider(gather_rider(early), "gather_first"))

    xs = x[0]
    saved, bands = [], []
    for l in range(nl):
        band = _band_bias(rel_bias[l])
        ups = riding_up[l]
        xs, sv = _forward_layer(xs, mod[l], layers[l], band, f"_l{l}", gather_rider(riding[l]), arrival(riding[l]),
                                gather_rider(ups) if ups else None, arrival(ups) if ups else None)
        bands.append(band)
        saved.append(sv)
    loss_part, dx, dfinal_g = _loss_head(xs, final_g[None, :], loss_target[0], "loss_head")
    loss = lax.psum(loss_part[0, 0], ("x", "y", "c"))

    small = [None] * nl
    dx, raw1, g_in1, small[1] = _backward_layer(dx, saved[1], mod[1], layers[1], bands[1], "_l1", c_idx)
    raw1[("w_in", "_l1")] = g_in1
    dx, parts, g_in0, small[0] = _backward_layer(dx, saved[0], mod[0], layers[0], bands[0], "_l0", c_idx, raw1)
    (from_sib,) = _run_rider(_sibling_rider([g_in0]), "rs_sibling_last")
    p32, p16 = _add_my_half(g_in0, from_sib, c_idx, "rs_add_w_in_l0")
    parts[("w_in", "_l0")] = (p32, _run_rider(_chips_rider([p16]), "rs_chips_last")[0])
    chip_idx = jnp.reshape(chip, (1,)).astype(jnp.int32)
    mine = [_sum_layers([parts[(n, f"_l{l}")][0] for l in range(nl)], [parts[(n, f"_l{l}")][1] for l in range(nl)],
                        chip_idx, f"rs_sum_{n}") for n in BIG]
    theirs = _run_rider(_swap_rider(mine), "rs_swap")

    small_names = ("dmod", "rel_bias", "g_a", "g_b", "conv_w", "conv_b")
    pieces = [small[l][n] for l in range(nl) for n in small_names] + [dfinal_g]
    shapes = [p.shape for p in pieces]
    pack3 = _pack_rows(pieces)
    got3 = _allgather8(pack3, "gather_small").reshape(N_DEV, pack3.shape[0], LANES)
    summed = _unpack_rows(_sum_slabs(got3, "sum_small"), shapes)
    tot = [dict(zip(small_names, summed[len(small_names) * l:len(small_names) * (l + 1)])) for l in range(nl)]
    g_final_g = summed[-1].reshape(-1)
    g_b_ada = jnp.stack([tot[l]["dmod"].reshape(-1) for l in range(nl)])
    g_rel = jnp.stack([tot[l]["rel_bias"] for l in range(nl)])
    g_ga = jnp.stack([tot[l]["g_a"].reshape(-1) for l in range(nl)])
    g_gb = jnp.stack([tot[l]["g_b"].reshape(-1) for l in range(nl)])
    g_conv_b = jnp.stack([tot[l]["conv_b"].reshape(-1) for l in range(nl)])
    g_conv_w = jnp.stack([lax.dynamic_slice_in_dim(tot[l]["conv_w"], chip * nc, nc, axis=1) for l in range(nl)])
    per_dev = [_unpack_rows(got3[j], shapes) for j in range(N_DEV)]
    dmod_all = jnp.stack([jnp.stack([per_dev[j][len(small_names) * l].reshape(-1) for j in range(N_DEV)])
                          for l in range(nl)])
    g_w_ada = _ada_bwd(c_all, lax.dynamic_slice_in_dim(dmod_all, chip * n_ada, n_ada, axis=2), "ada_bwd")

    grads = dict(w_ada=g_w_ada, b_ada=g_b_ada, rel_bias=g_rel, g_a=g_ga, g_b=g_gb, conv_w=g_conv_w, conv_b=g_conv_b,
                 final_g=g_final_g)
    weights = dict(w_ada=w_ada, b_ada=b_ada, w_in=w_in, rel_bias=rel_bias, g_a=g_a, g_b=g_b, w_out=w_out, w_up=w_up,
                   conv_w=conv_w, conv_b=conv_b, w_down=w_down, final_g=final_g)
    m_in = dict(w_ada=m_w_ada, b_ada=m_b_ada, w_in=m_w_in, rel_bias=m_rel_bias, g_a=m_g_a, g_b=m_g_b, w_out=m_w_out,
                w_up=m_w_up, conv_w=m_conv_w, conv_b=m_conv_b, w_down=m_w_down, final_g=m_final_g)
    v_in = dict(w_ada=v_w_ada, b_ada=v_b_ada, w_in=v_w_in, rel_bias=v_rel_bias, g_a=v_g_a, g_b=v_g_b, w_out=v_w_out,
                w_up=v_w_up, conv_w=v_conv_w, conv_b=v_conv_b, w_down=v_w_down, final_g=v_final_g)
    order_w = ("w_ada", "b_ada", "w_in", "rel_bias", "g_a", "g_b", "w_out", "w_up", "conv_w", "conv_b", "w_down", "final_g")
    upd = {n: _adamw_nd(grads[n], weights[n], m_in[n], v_in[n], f"adamw_{n}") for n in grads}
    for n, mn, th in zip(BIG, mine, theirs):
        grads[n], *upd[n] = _adamw_halves(mn, th, c_idx, weights[n], m_in[n], v_in[n], f"adamw_{n}")
    return (loss, dx[None], *[grads[n] for n in order_w], *[upd[n][0] for n in order_w],
            *[upd[n][1] for n in order_w], *[upd[n][2] for n in order_w])
```

```python
import functools

import jax
import jax.numpy as jnp
from jax import lax
from jax.experimental import pallas as pl
from jax.experimental.pallas import tpu as pltpu

F32 = jnp.float32
BF16 = jnp.bfloat16
MESH = pl.DeviceIdType.MESH
ANY = pl.BlockSpec(memory_space=pl.ANY)
VMEM_FULL = pl.BlockSpec(memory_space=pltpu.VMEM)

HEAD_DIM = 64
N_HEADS = 8
W_GRP = N_HEADS * HEAD_DIM
CHUNK = 64
N_PREV = 8
BAND = (N_PREV + 1) * CHUNK
PAD = N_PREV * CHUNK
REL_CLIP = 128
N_REL = 2 * REL_CLIP + 1
EPS = 1e-6
SB_DEAD = -110.0
N_CHIPS = 4
N_DEV = 8
LANES = 128
V7X_VMEM_LIMIT = 56 * 1024 * 1024

ADAM_LR = 0.001
ADAM_B1 = 0.9
ADAM_B2 = 0.999
ADAM_EPS = 1e-08
ADAM_WD = 0.01
ADAM_STEP = 10


def _params(**kw):
    return pltpu.CompilerParams(vmem_limit_bytes=V7X_VMEM_LIMIT, **kw)


def _pick(dim, pref, mult=LANES):
    t = (min(pref, dim) // mult) * mult
    while t >= mult:
        if dim % t == 0:
            return t
        t -= mult
    return dim


def _my_place():
    return lax.axis_index("x"), lax.axis_index("y"), lax.axis_index("c")


def _flip(v, bit):
    return 1 - v if bit else v


def _matmul(a, b, *, form, out_dtype, tm, tn, tk, name, shard_cols=None, halves=False, rider=None):
    if form == "nn":
        (m, k), (_, n) = a.shape, b.shape
        a_map, a_blk = (lambda i, j, kk: (i, kk)), (tm, tk)
        b_map, b_blk = (lambda i, j, kk: (kk, j)), (tk, tn)
        dims = (((1,), (0,)), ((), ()))
    elif form == "nt":
        m, k = (a.shape[1], 2 * a.shape[2]) if halves else a.shape
        n = b.shape[0]
        if halves:
            per_half = k // 2 // tk
            a_map, a_blk = (lambda i, j, kk: (kk // per_half, i, kk % per_half)), (None, tm, tk)
        else:
            a_map, a_blk = (lambda i, j, kk: (i, kk)), (tm, tk)
        b_map, b_blk = (lambda i, j, kk: (j, kk)), (tn, tk)
        dims = (((1,), (1,)), ((), ()))
    else:
        k, m = a.shape
        n = 2 * b.shape[2] if halves else b.shape[1]
        a_map, a_blk = (lambda i, j, kk: (kk, i)), (tk, tm)
        if halves:
            per_half = n // 2 // tn
            b_map, b_blk = (lambda i, j, kk: (j // per_half, kk, j % per_half)), (None, tk, tn)
        else:
            b_map, b_blk = (lambda i, j, kk: (kk, j)), (tk, tn)
        dims = (((0,), (0,)), ((), ()))
    assert m % tm == 0 and n % tn == 0 and k % tk == 0, (name, m, n, k, tm, tn, tk)
    nk = k // tk
    if shard_cols is None:
        out_shape = jax.ShapeDtypeStruct((m, n), out_dtype)
        o_map, o_blk = (lambda i, j, kk: (i, j)), (tm, tn)
    else:
        per = shard_cols // tn
        assert shard_cols % tn == 0
        out_shape = jax.ShapeDtypeStruct((n // shard_cols, m, shard_cols), out_dtype)
        o_map, o_blk = (lambda i, j, kk: (j // per, i, j % per)), (None, tm, tn)
    a_bytes, b_bytes = a.size * a.dtype.itemsize, b.size * b.dtype.itemsize
    rows_outer = nk > 1 or (m // tm) * b_bytes + a_bytes <= (n // tn) * a_bytes + b_bytes
    grid = (m // tm, n // tn, nk) if rows_outer else (n // tn, m // tm, nk)
    order = (lambda f: f) if rows_outer else (lambda f: (lambda g0, g1, kk: f(g1, g0, kk)))

    def body(a_ref, b_ref, o_ref, *acc):
        part = lax.dot_general(a_ref[...], b_ref[...], dims, preferred_element_type=F32)
        if nk == 1:
            o_ref[...] = part.astype(out_dtype)
            return
        acc_ref, = acc
        kk = pl.program_id(2)

        @pl.when(kk == 0)
        def _():
            acc_ref[...] = part

        @pl.when(jnp.logical_and(kk > 0, kk < nk - 1))
        def _():
            acc_ref[...] += part

        @pl.when(kk == nk - 1)
        def _():
            o_ref[...] = (acc_ref[...] + part).astype(out_dtype)

    out = _call_with_rider(
        body, rider, name=name, out_shape=(out_shape,), grid=grid,
        in_specs=[pl.BlockSpec(a_blk, order(a_map)), pl.BlockSpec(b_blk, order(b_map))],
        out_specs=(pl.BlockSpec(o_blk, order(o_map)),),
        scratch_shapes=[pltpu.VMEM((tm, tn), F32)] if nk > 1 else [], args=(a, b))
    return out[0] if rider is None else out


def _row_spec(tr, d):
    return pl.BlockSpec((tr, d), lambda i: (i, 0))


def _vec_spec(d):
    return pl.BlockSpec((1, d), lambda i: (0, 0))


def _rms(xf):
    r = lax.rsqrt(jnp.mean(xf * xf, axis=-1, keepdims=True) + EPS)
    return xf * r, r


def _norm_mod(x, scale, shift, name):
    s, d = x.shape
    tr = _pick(s, 512, 8)

    def body(x_ref, sc_ref, sh_ref, o_ref):
        n, _ = _rms(x_ref[...])
        o_ref[...] = (n * (1.0 + sc_ref[...]) + sh_ref[...]).astype(BF16)

    return pl.pallas_call(
        body, name=name, out_shape=jax.ShapeDtypeStruct((s, d), BF16), grid=(s // tr,),
        in_specs=[_row_spec(tr, d), _vec_spec(d), _vec_spec(d)], out_specs=_row_spec(tr, d),
        compiler_params=_params(),
    )(x, scale, shift)


def _out_norm(oa, ob, g_a, g_b, name):
    s, w = oa.shape
    tr = _pick(s, 512, 8)

    def body(oa_ref, ob_ref, ga_ref, gb_ref, o_ref):
        na, _ = _rms(oa_ref[...])
        nb, _ = _rms(ob_ref[...])
        o_ref[:, :w] = (na * ga_ref[...]).astype(BF16)
        o_ref[:, w:] = (nb * gb_ref[...]).astype(BF16)

    return pl.pallas_call(
        body, name=name, out_shape=jax.ShapeDtypeStruct((s, 2 * w), BF16), grid=(s // tr,),
        in_specs=[_row_spec(tr, w), _row_spec(tr, w), _vec_spec(w), _vec_spec(w)],
        out_specs=_row_spec(tr, 2 * w), compiler_params=_params(),
    )(oa, ob, g_a, g_b)


def _residual(x, gate, m, name):
    s, d = x.shape
    tr = _pick(s, 512, 8)

    def body(x_ref, g_ref, m_ref, o_ref):
        o_ref[...] = x_ref[...] + g_ref[...] * m_ref[...]

    return pl.pallas_call(
        body, name=name, out_shape=jax.ShapeDtypeStruct((s, d), F32), grid=(s // tr,),
        in_specs=[_row_spec(tr, d), _vec_spec(d), _row_spec(tr, d)], out_specs=_row_spec(tr, d),
        compiler_params=_params(),
    )(x, gate, m)


def _shift_down(u, k):
    rows = lax.broadcasted_iota(jnp.int32, u.shape, 0)
    return jnp.where(rows >= k, pltpu.roll(u, k, 0), 0.0)


def _shift_up(u, k):
    s = u.shape[0]
    rows = lax.broadcasted_iota(jnp.int32, u.shape, 0)
    return jnp.where(rows < s - k, pltpu.roll(u, s - k, 0), 0.0)


def _conv(u, w_ref, b_ref):
    return w_ref[0:1, :] * _shift_down(u, 2) + w_ref[1:2, :] * _shift_down(u, 1) + w_ref[2:3, :] * u + b_ref[...]


STRIP = 256


def _conv_strip(u_ref, t, r0, w_ref, b_ref):
    cur = u_ref[pl.ds(r0, STRIP), :]
    prev = u_ref[pl.ds(pl.multiple_of(jnp.maximum(r0 - 8, 0), 8), 8), :]
    ext = jnp.concatenate([jnp.where(t > 0, prev, 0.0), cur], axis=0)
    u1 = pltpu.roll(ext, 1, 0)[8:]
    u2 = pltpu.roll(ext, 2, 0)[8:]
    return w_ref[0:1, :] * u2 + w_ref[1:2, :] * u1 + w_ref[2:3, :] * cur + b_ref[...], u2, u1, cur


def _fold8(v):
    return functools.reduce(jnp.add, [v[r:r + 8] for r in range(0, STRIP, 8)])


def _conv_glu(u, conv_w, conv_b, name):
    s, f2 = u.shape
    f = f2 // 2
    tc = LANES
    nb = f // tc

    def body(ug_ref, uv_ref, wg_ref, wv_ref, bg_ref, bv_ref, o_ref):
        def strip(t, _):
            r0 = pl.multiple_of(t * STRIP, STRIP)
            g = _conv_strip(ug_ref, t, r0, wg_ref, bg_ref)[0]
            v = _conv_strip(uv_ref, t, r0, wv_ref, bv_ref)[0]
            o_ref[pl.ds(r0, STRIP), :] = (g * jax.nn.sigmoid(g) * v).astype(BF16)
            return 0

        lax.fori_loop(0, s // STRIP, strip, 0)

    col = lambda off: pl.BlockSpec((s, tc), lambda j: (0, j + off))
    wcol = lambda off: pl.BlockSpec((3, tc), lambda j: (0, j + off))
    bcol = lambda off: pl.BlockSpec((1, tc), lambda j: (0, j + off))
    return pl.pallas_call(
        body, name=name, out_shape=jax.ShapeDtypeStruct((s, f), BF16), grid=(nb,),
        in_specs=[col(0), col(nb), wcol(0), wcol(nb), bcol(0), bcol(nb)], out_specs=col(0),
        compiler_params=_params(),
    )(u, u, conv_w, conv_w, conv_b, conv_b)


def _conv_glu_bwd(u, da, conv_w, conv_b, name, rider=None):
    s, f2 = u.shape
    f = f2 // 2
    tc = LANES
    nb = f // tc

    def body(ug_ref, uv_ref, da_ref, wg_ref, wv_ref, bg_ref, bv_ref, du_ref, dw_ref, db_ref, dyg_ref, dyv_ref):
        def strip1(t, acc):
            r0 = pl.multiple_of(t * STRIP, STRIP)
            g, *ug = _conv_strip(ug_ref, t, r0, wg_ref, bg_ref)
            v, *uv = _conv_strip(uv_ref, t, r0, wv_ref, bv_ref)
            da_ = da_ref[pl.ds(r0, STRIP), :]
            sg = jax.nn.sigmoid(g)
            dg = da_ * v * (sg * (1.0 + g * (1.0 - sg)))
            dv = da_ * (g * sg)
            dyg_ref[pl.ds(r0, STRIP), :] = dg
            dyv_ref[pl.ds(r0, STRIP), :] = dv
            new = [_fold8(dy * uu) for dy, us in ((dg, ug), (dv, uv)) for uu in us] + [_fold8(dg), _fold8(dv)]
            return tuple(a + n for a, n in zip(acc, new))

        zero = jnp.zeros((8, LANES), F32)
        acc = lax.fori_loop(0, s // STRIP, strip1, (zero,) * 8)
        for h in range(2):
            for tap in range(3):
                dw_ref[h, tap:tap + 1, :] = jnp.sum(acc[3 * h + tap], axis=0, keepdims=True)
            db_ref[h] = jnp.sum(acc[6 + h], axis=0, keepdims=True)
        dyg_ref[s:, :] = zero
        dyv_ref[s:, :] = zero

        def strip2(t, _):
            r0 = pl.multiple_of(t * STRIP, STRIP)
            for h, (dy_ref, w_ref) in enumerate(((dyg_ref, wg_ref), (dyv_ref, wv_ref))):
                cur = dy_ref[pl.ds(r0, STRIP), :]
                ext = jnp.concatenate([cur, dy_ref[pl.ds(r0 + STRIP, 8), :]], axis=0)
                d1 = pltpu.roll(ext, STRIP + 7, 0)[:STRIP]
                d2 = pltpu.roll(ext, STRIP + 6, 0)[:STRIP]
                du = w_ref[2:3, :] * cur + w_ref[1:2, :] * d1 + w_ref[0:1, :] * d2
                du_ref[h, pl.ds(r0, STRIP), :] = du.astype(BF16)
            return 0

        lax.fori_loop(0, s // STRIP, strip2, 0)

    col = lambda off: pl.BlockSpec((s, tc), lambda j: (0, j + off))
    wcol = lambda off: pl.BlockSpec((3, tc), lambda j: (0, j + off))
    bcol = lambda off: pl.BlockSpec((1, tc), lambda j: (0, j + off))
    return _call_with_rider(
        body, rider, name=name, grid=(nb,),
        out_shape=(jax.ShapeDtypeStruct((2, s, f), BF16), jax.ShapeDtypeStruct((2, 3, f), F32),
                   jax.ShapeDtypeStruct((2, 1, f), F32)),
        in_specs=[col(0), col(nb), col(0), wcol(0), wcol(nb), bcol(0), bcol(nb)],
        out_specs=(pl.BlockSpec((2, s, tc), lambda j: (0, 0, j)), pl.BlockSpec((2, 3, tc), lambda j: (0, 0, j)),
                   pl.BlockSpec((2, 1, tc), lambda j: (0, 0, j))),
        scratch_shapes=[pltpu.VMEM((s + 8, tc), F32), pltpu.VMEM((s + 8, tc), F32)],
        args=(u, u, da, conv_w, conv_w, conv_b, conv_b))


def _accumulate(ref, val):
    @pl.when(pl.program_id(0) == 0)
    def _():
        ref[...] = val

    @pl.when(pl.program_id(0) > 0)
    def _():
        ref[...] += val


def _rms_bwd(n, r, dn):
    return r * (dn - n * jnp.mean(dn * n, axis=-1, keepdims=True))


def _loss_head(x, final_g, target, name):
    s, d = x.shape
    tr = _pick(s, 512, 8)

    def body(x_ref, g_ref, t_ref, loss_ref, dx_ref, dg_ref):
        n, r = _rms(x_ref[...])
        diff = n * g_ref[...] - t_ref[...]
        part = 0.5 * jnp.sum(jnp.sum(diff * diff, axis=1, keepdims=True), axis=0, keepdims=True) / d
        _accumulate(loss_ref, part)
        dy = diff / d
        _accumulate(dg_ref, jnp.sum(dy * n, axis=0, keepdims=True))
        dx_ref[...] = _rms_bwd(n, r, dy * g_ref[...])

    return pl.pallas_call(
        body, name=name, grid=(s // tr,),
        out_shape=(jax.ShapeDtypeStruct((1, 1), F32), jax.ShapeDtypeStruct((s, d), F32), jax.ShapeDtypeStruct((1, d), F32)),
        in_specs=[_row_spec(tr, d), _vec_spec(d), _row_spec(tr, d)],
        out_specs=(pl.BlockSpec((1, 1), lambda i: (0, 0)), _row_spec(tr, d), _vec_spec(d)),
        compiler_params=_params(),
    )(x, final_g, target)


def _gate_bwd(dx, m, gate, name):
    s, d = dx.shape
    tr = _pick(s, 512, 8)

    def body(dx_ref, m_ref, g_ref, dm_ref, dg_ref):
        dxv = dx_ref[...]
        dm_ref[...] = (dxv * g_ref[...]).astype(BF16)
        _accumulate(dg_ref, jnp.sum(dxv * m_ref[...], axis=0, keepdims=True))

    return pl.pallas_call(
        body, name=name, grid=(s // tr,),
        out_shape=(jax.ShapeDtypeStruct((s, d), BF16), jax.ShapeDtypeStruct((1, d), F32)),
        in_specs=[_row_spec(tr, d), _row_spec(tr, d), _vec_spec(d)], out_specs=(_row_spec(tr, d), _vec_spec(d)),
        compiler_params=_params(),
    )(dx, m, gate)


def _norm_mod_bwd(x, dh, dres, scale, name):
    s, d = x.shape
    tr = _pick(s, 512, 8)

    def body(x_ref, dh_ref, dr_ref, sc_ref, dx_ref, dsc_ref, dsh_ref):
        n, r = _rms(x_ref[...])
        dh_ = dh_ref[...]
        _accumulate(dsc_ref, jnp.sum(dh_ * n, axis=0, keepdims=True))
        _accumulate(dsh_ref, jnp.sum(dh_, axis=0, keepdims=True))
        dx_ref[...] = dr_ref[...] + _rms_bwd(n, r, dh_ * (1.0 + sc_ref[...]))

    return pl.pallas_call(
        body, name=name, grid=(s // tr,),
        out_shape=(jax.ShapeDtypeStruct((s, d), F32), jax.ShapeDtypeStruct((1, d), F32), jax.ShapeDtypeStruct((1, d), F32)),
        in_specs=[_row_spec(tr, d), _row_spec(tr, d), _row_spec(tr, d), _vec_spec(d)],
        out_specs=(_row_spec(tr, d), _vec_spec(d), _vec_spec(d)), compiler_params=_params(),
    )(x, dh, dres, scale)


def _out_norm_bwd(oa, ob, dcat, g_a, g_b, name):
    s, w = oa.shape
    tr = _pick(s, 512, 8)

    def body(oa_ref, ob_ref, dc_ref, ga_ref, gb_ref, doa_ref, dob_ref, dga_ref, dgb_ref):
        for o_ref, g_ref, do_ref, dg_ref, lo in ((oa_ref, ga_ref, doa_ref, dga_ref, 0), (ob_ref, gb_ref, dob_ref, dgb_ref, w)):
            n, r = _rms(o_ref[...])
            dc = dc_ref[:, lo:lo + w]
            _accumulate(dg_ref, jnp.sum(dc * n, axis=0, keepdims=True))
            do_ref[...] = _rms_bwd(n, r, dc * g_ref[...])

    return pl.pallas_call(
        body, name=name, grid=(s // tr,),
        out_shape=(jax.ShapeDtypeStruct((s, w), F32), jax.ShapeDtypeStruct((s, w), F32),
                   jax.ShapeDtypeStruct((1, w), F32), jax.ShapeDtypeStruct((1, w), F32)),
        in_specs=[_row_spec(tr, w), _row_spec(tr, w), _row_spec(tr, 2 * w), _vec_spec(w), _vec_spec(w)],
        out_specs=(_row_spec(tr, w), _row_spec(tr, w), _vec_spec(w), _vec_spec(w)), compiler_params=_params(),
    )(oa, ob, dcat, g_a, g_b)


def _head_masks():
    lane = lax.broadcasted_iota(jnp.int32, (1, LANES), 1)
    return lane < HEAD_DIM, lane >= HEAD_DIM


def _nt(a, b):
    return lax.dot_general(a, b, (((1,), (1,)), ((), ())), preferred_element_type=F32)


def _tn(a, b):
    return lax.dot_general(a, b, (((0,), (0,)), ((), ())), preferred_element_type=F32)


def _nn(a, b):
    return jnp.dot(a, b, preferred_element_type=F32)


def _only(mask, v):
    return jnp.where(mask, v, jnp.zeros_like(v))


def _fill_padded(dst_ref, src_ref):
    dst_ref[0:PAD, :] = jnp.zeros((PAD, LANES), dst_ref.dtype)
    dst_ref[PAD:, :] = src_ref[...]


def _chunk_probs(s, bias, chunk):
    pos = lax.broadcasted_iota(jnp.int32, (1, BAND), 1)
    s = jnp.where(pos >= (N_PREV - chunk) * CHUNK, s + bias, -1e30)
    e = jnp.exp(s - jnp.max(s, axis=1, keepdims=True))
    return e / jnp.sum(e, axis=1, keepdims=True)


def _band_windows(i, cq, kpad, vpad):
    chunks = [i * cq + cc for cc in range(cq)]
    starts = [pl.multiple_of(ch * CHUNK, CHUNK) for ch in chunks]
    return chunks, starts, [kpad[pl.ds(st, BAND), :] for st in starts], [vpad[pl.ds(st, BAND), :] for st in starts]


def _attn_a_fwd(proj, band_bias, name):
    s = proj.shape[0]
    cq = 4
    tq = cq * CHUNK
    npair = N_HEADS // 2
    kcol, vcol = W_GRP // LANES, 2 * W_GRP // LANES

    def body(q_ref, k_ref, v_ref, b_ref, o_ref, kpad, vpad):
        i = pl.program_id(1)
        masks = _head_masks()

        @pl.when(i == 0)
        def _():
            _fill_padded(kpad, k_ref)
            _fill_padded(vpad, v_ref)

        chunks, _, kbs, vbs = _band_windows(i, cq, kpad, vpad)
        q2 = q_ref[...] * (HEAD_DIM ** -0.5)
        units = [(cc, h) for cc in range(cq) for h in range(2)]
        ss = [_nt(_only(masks[h], q2[cc * CHUNK:(cc + 1) * CHUNK]), kbs[cc]) for cc, h in units]
        ps = [_chunk_probs(s_, b_ref[h], chunks[cc]).astype(BF16) for s_, (cc, h) in zip(ss, units)]
        for cc in range(cq):
            o_ref[cc * CHUNK:(cc + 1) * CHUNK, :] = (_nn(ps[2 * cc], _only(masks[0], vbs[cc]))
                                                     + _nn(ps[2 * cc + 1], _only(masks[1], vbs[cc])))

    return pl.pallas_call(
        body, name=name, out_shape=jax.ShapeDtypeStruct((s, W_GRP), F32), grid=(npair, s // tq),
        in_specs=[pl.BlockSpec((tq, LANES), lambda p, i: (i, p)),
                  pl.BlockSpec((s, LANES), lambda p, i: (0, kcol + p)),
                  pl.BlockSpec((s, LANES), lambda p, i: (0, vcol + p)),
                  pl.BlockSpec((2, CHUNK, BAND), lambda p, i: (p, 0, 0))],
        out_specs=pl.BlockSpec((tq, LANES), lambda p, i: (i, p)),
        scratch_shapes=[pltpu.VMEM((s + PAD, LANES), BF16), pltpu.VMEM((s + PAD, LANES), BF16)],
        compiler_params=_params(),
    )(proj, proj, proj, band_bias)


def _attn_a_bwd(proj, band_bias, doa, name, rider=None):
    s = proj.shape[0]
    cq = 4
    tq = cq * CHUNK
    nq = s // tq
    npair = N_HEADS // 2
    kcol, vcol = W_GRP // LANES, 2 * W_GRP // LANES
    scale = HEAD_DIM ** -0.5

    def body(q_ref, k_ref, v_ref, b_ref, do_ref, dq_ref, dk_ref, dv_ref, db_ref, kpad, vpad, dkpad, dvpad):
        i = pl.program_id(1)
        masks = _head_masks()

        @pl.when(i == 0)
        def _():
            _fill_padded(kpad, k_ref)
            _fill_padded(vpad, v_ref)
            dkpad[...] = jnp.zeros_like(dkpad)
            dvpad[...] = jnp.zeros_like(dvpad)
            db_ref[...] = jnp.zeros_like(db_ref)

        chunks, starts, kbs, vbs = _band_windows(i, cq, kpad, vpad)
        q2 = q_ref[...] * scale
        do2 = do_ref[...].astype(BF16)
        units = [(cc, h) for cc in range(cq) for h in range(2)]
        qhs = [_only(masks[h], q2[cc * CHUNK:(cc + 1) * CHUNK]) for cc, h in units]
        dohs = [_only(masks[h], do2[cc * CHUNK:(cc + 1) * CHUNK]) for cc, h in units]
        ss = [_nt(qh, kbs[cc]) for qh, (cc, h) in zip(qhs, units)]
        dps = [_nt(doh, vbs[cc]) for doh, (cc, h) in zip(dohs, units)]
        ps = [_chunk_probs(s_, b_ref[h], chunks[cc]) for s_, (cc, h) in zip(ss, units)]
        dss = [p * (dp - jnp.sum(p * dp, axis=1, keepdims=True)) for p, dp in zip(ps, dps)]
        for h in range(2):
            db_ref[h] += functools.reduce(jnp.add, [dss[2 * cc + h] for cc in range(cq)])
        for cc in range(cq):
            u0, u1 = 2 * cc, 2 * cc + 1
            dsb = [dss[u0].astype(BF16), dss[u1].astype(BF16)]
            dq = _nn(dsb[0], _only(masks[0], kbs[cc])) + _nn(dsb[1], _only(masks[1], kbs[cc]))
            dq_ref[cc * CHUNK:(cc + 1) * CHUNK, :] = dq * scale
            dkpad[pl.ds(starts[cc], BAND), :] += _tn(jnp.concatenate(dsb, axis=0), jnp.concatenate([qhs[u0], qhs[u1]], axis=0))
            dvpad[pl.ds(starts[cc], BAND), :] += _tn(jnp.concatenate([ps[u0].astype(BF16), ps[u1].astype(BF16)], axis=0),
                                                     jnp.concatenate([dohs[u0], dohs[u1]], axis=0))

        @pl.when(i == nq - 1)
        def _():
            dk_ref[...] = dkpad[PAD:, :]
            dv_ref[...] = dvpad[PAD:, :]

    blk = pl.BlockSpec((tq, LANES), lambda p, i: (i, p))
    whole = pl.BlockSpec((s, LANES), lambda p, i: (0, p))
    bias_spec = pl.BlockSpec((2, CHUNK, BAND), lambda p, i: (p, 0, 0))
    return _call_with_rider(
        body, rider, name=name, grid=(npair, nq),
        out_shape=(jax.ShapeDtypeStruct((s, W_GRP), F32),) * 3 + (jax.ShapeDtypeStruct((N_HEADS, CHUNK, BAND), F32),),
        in_specs=[blk, pl.BlockSpec((s, LANES), lambda p, i: (0, kcol + p)),
                  pl.BlockSpec((s, LANES), lambda p, i: (0, vcol + p)), bias_spec, blk],
        out_specs=(blk, whole, whole, bias_spec),
        scratch_shapes=[pltpu.VMEM((s + PAD, LANES), BF16), pltpu.VMEM((s + PAD, LANES), BF16),
                        pltpu.VMEM((s + PAD, LANES), F32), pltpu.VMEM((s + PAD, LANES), F32)],
        args=(proj, proj, proj, band_bias, doa))


def _split3(v):
    hi = v.astype(BF16)
    r1 = v - hi.astype(F32)
    mid = r1.astype(BF16)
    lo = (r1 - mid.astype(F32)).astype(BF16)
    return hi, mid, lo


def _rel_bias_grad(dband_t, name):
    width = 3 * LANES

    def body(t_ref, o_ref):
        pos = lax.broadcasted_iota(jnp.int32, (BAND, width), 0)
        col = lax.broadcasted_iota(jnp.int32, (BAND, width), 1)
        acc = jnp.zeros((N_HEADS, width), F32)
        for q in range(CHUNK):
            idx = jnp.minimum(PAD + q - pos, REL_CLIP) + REL_CLIP
            onehot = (col == idx).astype(BF16)
            for part in _split3(t_ref[q]):
                acc = acc + _nn(part, onehot)
        o_ref[...] = acc

    return pl.pallas_call(
        body, name=name, out_shape=jax.ShapeDtypeStruct((N_HEADS, width), F32),
        in_specs=[VMEM_FULL], out_specs=VMEM_FULL, compiler_params=_params(),
    )(dband_t)


def _split2_wide(v):
    hi = v.astype(BF16)
    return jnp.concatenate([hi, (v - hi.astype(F32)).astype(BF16)], axis=1)


def _sb_logs(z, lower):
    e = jnp.exp(-jnp.abs(z))
    lb = jnp.minimum(z, 0.0) - jnp.log(1.0 + e)
    lk = lb - z
    if lower is not None:
        lk = jnp.where(lower, lk, 0.0)
    return z, e, lb, lk


def _tri_masks(tq):
    row = lax.broadcasted_iota(jnp.int32, (tq, tq), 0)
    col = lax.broadcasted_iota(jnp.int32, (tq, tq), 1)
    return row, col


def _stack2(m):
    return jnp.concatenate([m, m], axis=0).astype(BF16)


def _sb_fwd(proj, name, rider=None):
    s = proj.shape[0]
    tq = _pick(s, 256)
    nq = s // tq
    npair = N_HEADS // 2
    qcol, kcol, vcol = 3 * W_GRP // LANES, 4 * W_GRP // LANES, 5 * W_GRP // LANES

    assert nq % 2 == 0

    def body(q_ref, k_ref, v_ref, o_ref, l_ref):
        i = pl.program_id(1)
        masks = _head_masks()
        q2 = q_ref[...] * (HEAD_DIM ** -0.5)
        qs = [[_only(m, q2[c * tq:(c + 1) * tq]) for m in masks] for c in range(2)]
        row, col = _tri_masks(tq)
        lower = row > col
        after2 = _stack2(lower)

        def tile(kblock, chains, carry):
            accs, tails = [list(t) for t in carry[0]], [list(t) for t in carry[1]]
            ks = pl.multiple_of(kblock * tq, tq)
            kb = k_ref[pl.ds(ks, tq), :]
            vb = v_ref[pl.ds(ks, tq), :]
            units = [(c, h, diag) for c, diag in chains for h in range(2)]
            zs = [_nt(qs[c][h], kb) for c, h, _ in units]
            vh = [_only(masks[h], vb) for h in range(2)]
            lbs, lks, locs = [], [], []
            for z, (c, h, diag) in zip(zs, units):
                lb, lk = _sb_logs(z, lower if diag else None)[2:]
                lbs.append(lb)
                lks.append(lk)
                locs.append(_nn(_split2_wide(lk), after2))
            for lb, lk, loc, (c, h, diag) in zip(lbs, lks, locs, units):
                a = jnp.exp(lb + (loc + tails[c][h]))
                if diag:
                    a = jnp.where(lower, a, 0.0)
                accs[c][0] = accs[c][0] + _nn(a.astype(BF16), vh[h])
                tails[c][h] = tails[c][h] + (loc[:, 0:1] + lk[:, 0:1])
            return tuple(tuple(t) for t in accs), tuple(tuple(t) for t in tails)

        zero = jnp.zeros((tq, 1), F32)
        acc0 = jnp.zeros((tq, LANES), F32)
        carry = (((acc0,), (acc0,)), ((zero, zero), (zero, zero)))
        carry = tile(2 * i + 1, [(1, True)], carry)
        carry = tile(2 * i, [(0, True), (1, False)], carry)

        def alive(tails):
            return functools.reduce(jnp.maximum, [jnp.max(t) for ts in tails for t in ts]) > SB_DEAD

        def walk(state):
            jj, _, cr = state
            cr = tile(2 * i - jj, [(0, False), (1, False)], cr)
            return jj + 1, alive(cr[1]), cr

        jj, _, (accs, tails) = lax.while_loop(lambda st: jnp.logical_and(st[0] <= 2 * i, st[1]), walk,
                                              (jnp.int32(1), i >= 0, carry))
        for c in range(2):
            o_ref[c * tq:(c + 1) * tq, :] = accs[c][0]
            l_ref[c * tq:(c + 1) * tq, 0:1] = tails[c][0]
            l_ref[c * tq:(c + 1) * tq, 1:2] = tails[c][1]
        l_ref[:, 2:3] = jnp.full((2 * tq, 1), (jj - 1).astype(F32))

    return _call_with_rider(
        body, rider, name=name, grid=(npair, nq // 2),
        out_shape=(jax.ShapeDtypeStruct((s, W_GRP), F32), jax.ShapeDtypeStruct((npair, s, 3), F32)),
        in_specs=[pl.BlockSpec((2 * tq, LANES), lambda p, i: (i, qcol + p)),
                  pl.BlockSpec((s, LANES), lambda p, i: (0, kcol + p)),
                  pl.BlockSpec((s, LANES), lambda p, i: (0, vcol + p))],
        out_specs=(pl.BlockSpec((2 * tq, LANES), lambda p, i: (i, p)),
                   pl.BlockSpec((None, 2 * tq, 3), lambda p, i: (p, i, 0))),
        scratch_shapes=[], args=(proj, proj, proj))


def _sb_bwd(proj, ltot, dob, name, rider=None):
    s = proj.shape[0]
    tq = _pick(s, 256)
    nq = s // tq
    npair = N_HEADS // 2
    qcol, kcol, vcol = 3 * W_GRP // LANES, 4 * W_GRP // LANES, 5 * W_GRP // LANES
    scale = HEAD_DIM ** -0.5

    def body(q_ref, k_ref, v_ref, l_ref, do_ref, dq_ref, dk_ref, dv_ref):
        i = pl.program_id(1)
        masks = _head_masks()

        @pl.when(i == 0)
        def _():
            dk_ref[...] = jnp.zeros_like(dk_ref)
            dv_ref[...] = jnp.zeros_like(dv_ref)

        q2 = q_ref[...] * scale
        do2 = do_ref[...]
        part = lambda v, c: v[c * tq:(c + 1) * tq]
        qs = [[_only(m, part(q2, c)) for m in masks] for c in range(2)]
        doh = [[_only(m, part(do2, c)).astype(BF16) for m in masks] for c in range(2)]
        ltots = [[l_ref[c * tq:(c + 1) * tq, h:h + 1] for h in range(2)] for c in range(2)]
        row, col = _tri_masks(tq)
        lower = row > col
        upto2 = _stack2(row <= col)
        before = (row < col).astype(BF16)

        def tile(kblock, chains, carry):
            dqs, heads, gsums = [[list(t) for t in part_] for part_ in carry]
            ks = pl.multiple_of(kblock * tq, tq)
            kb = k_ref[pl.ds(ks, tq), :]
            vb = v_ref[pl.ds(ks, tq), :]
            units = [(c, h, diag) for c, diag in chains for h in range(2)]
            zs = [_nt(qs[c][h], kb) for c, h, _ in units]
            das = [_nt(doh[c][h], vb) for c, h, _ in units]
            kh = [_only(masks[h], kb) for h in range(2)]
            sigs, lbs, locs = [], [], []
            for z_, (c, h, diag) in zip(zs, units):
                z, e, lb, lk = _sb_logs(z_, lower if diag else None)
                locs.append(_nn(_split2_wide(lk), upto2))
                r = 1.0 / (1.0 + e)
                sigs.append(jnp.where(z >= 0, r, e * r))
                lbs.append(lb)
            a_s, gs, glocs = [], [], []
            for lb, loc, da, (c, h, diag) in zip(lbs, locs, das, units):
                a = jnp.exp(lb + (ltots[c][h] - (heads[c][h] + loc)))
                if diag:
                    a = jnp.where(lower, a, 0.0)
                g = a * da
                glocs.append(_nn(g.astype(BF16), before))
                a_s.append(a.astype(BF16))
                gs.append(g)
            dzbs = []
            for g, sig, loc, gloc, (c, h, diag) in zip(gs, sigs, locs, glocs, units):
                dz = g - sig * (g + (gsums[c][h] + gloc))
                if diag:
                    dz = jnp.where(lower, dz, 0.0)
                dzb = dz.astype(BF16)
                dzbs.append(dzb)
                dqs[c][0] = dqs[c][0] + _nn(dzb, kh[h])
                heads[c][h] = heads[c][h] + loc[:, tq - 1:tq]
                gsums[c][h] = gsums[c][h] + (gloc[:, tq - 1:tq] + g[:, tq - 1:tq])
            stack = lambda vs: vs[0] if len(vs) == 1 else jnp.concatenate(vs, axis=0)
            dk_ref[pl.ds(ks, tq), :] += _tn(stack(dzbs), stack([qs[c][h] for c, h, _ in units]))
            dv_ref[pl.ds(ks, tq), :] += _tn(stack(a_s), stack([doh[c][h] for c, h, _ in units]))
            return tuple(tuple(tuple(t) for t in part_) for part_ in (dqs, heads, gsums))

        zero = jnp.zeros((tq, 1), F32)
        acc0 = jnp.zeros((tq, LANES), F32)
        carry = (((acc0,), (acc0,)), ((zero, zero), (zero, zero)), ((zero, zero), (zero, zero)))
        walked = jnp.clip(jnp.max(l_ref[0:8, 2:3]).astype(jnp.int32), 0, 2 * i)
        carry = lax.fori_loop(2 * i - walked, 2 * i, lambda j, cr: tile(j, [(0, False), (1, False)], cr), carry)
        carry = tile(2 * i, [(0, True), (1, False)], carry)
        dqs, _, _ = tile(2 * i + 1, [(1, True)], carry)
        for c in range(2):
            dq_ref[c * tq:(c + 1) * tq, :] = dqs[c][0] * scale

    blk = pl.BlockSpec((2 * tq, LANES), lambda p, i: (i, p))
    whole = pl.BlockSpec((s, LANES), lambda p, i: (0, p))
    return _call_with_rider(
        body, rider, name=name, grid=(npair, nq // 2), out_shape=(jax.ShapeDtypeStruct((s, W_GRP), F32),) * 3,
        in_specs=[pl.BlockSpec((2 * tq, LANES), lambda p, i: (i, qcol + p)),
                  pl.BlockSpec((s, LANES), lambda p, i: (0, kcol + p)),
                  pl.BlockSpec((s, LANES), lambda p, i: (0, vcol + p)),
                  pl.BlockSpec((None, 2 * tq, 3), lambda p, i: (p, i, 0)), blk],
        out_specs=(blk, whole, whole), scratch_shapes=[], args=(proj, proj, proj, ltot, dob))


def _ada_fwd(c_all, w_ada, b_ada, name):
    nl, d, n = w_ada.shape
    tn = _pick(n, 512)

    def body(c_ref, w_ref, b_ref, o_ref):
        cv = c_ref[...]
        act = (cv * jax.nn.sigmoid(cv)).astype(BF16)
        o_ref[...] = _nn(act, w_ref[...].astype(BF16)) + b_ref[...]

    return pl.pallas_call(
        body, name=name, out_shape=jax.ShapeDtypeStruct((nl, N_DEV, n), F32), grid=(nl, n // tn),
        in_specs=[pl.BlockSpec((N_DEV, d), lambda l, j: (0, 0)), pl.BlockSpec((None, d, tn), lambda l, j: (l, 0, j)),
                  pl.BlockSpec((None, 1, tn), lambda l, j: (l, 0, j))],
        out_specs=pl.BlockSpec((None, N_DEV, tn), lambda l, j: (l, 0, j)), compiler_params=_params(),
    )(c_all, w_ada, b_ada)


def _ada_bwd(c_all, dmod, name):
    nl, _, n = dmod.shape
    d = c_all.shape[1]
    tn = _pick(n, 512)

    def body(c_ref, g_ref, o_ref):
        cv = c_ref[...]
        act = (cv * jax.nn.sigmoid(cv)).astype(BF16)
        o_ref[...] = _tn(act, g_ref[...].astype(BF16))

    return pl.pallas_call(
        body, name=name, out_shape=jax.ShapeDtypeStruct((nl, d, n), F32), grid=(nl, n // tn),
        in_specs=[pl.BlockSpec((N_DEV, d), lambda l, j: (0, 0)), pl.BlockSpec((None, N_DEV, tn), lambda l, j: (l, 0, j))],
        out_specs=pl.BlockSpec((None, d, tn), lambda l, j: (l, 0, j)), compiler_params=_params(),
    )(c_all, dmod)


def _adamw(g, w, m, v, name):
    r, c = g.shape
    tr = _pick(r, 512, 8)
    c1 = 1.0 - ADAM_B1 ** ADAM_STEP
    c2 = 1.0 - ADAM_B2 ** ADAM_STEP

    def body(g_ref, w_ref, m_ref, v_ref, d_ref, nm_ref, nv_ref):
        gv = g_ref[...]
        nm = ADAM_B1 * m_ref[...] + (1.0 - ADAM_B1) * gv
        nv = ADAM_B2 * v_ref[...] + (1.0 - ADAM_B2) * (gv * gv)
        d_ref[...] = -ADAM_LR * ((nm / c1) / (jnp.sqrt(nv / c2) + ADAM_EPS) + ADAM_WD * w_ref[...])
        nm_ref[...] = nm
        nv_ref[...] = nv

    spec = pl.BlockSpec((tr, c), lambda i: (i, 0))
    return pl.pallas_call(
        body, name=name, out_shape=(jax.ShapeDtypeStruct((r, c), F32),) * 3, grid=(r // tr,),
        in_specs=[spec] * 4, out_specs=(spec,) * 3, compiler_params=_params(),
    )(g, w, m, v)


def _adamw_nd(g, w, m, v, name):
    shape = w.shape
    two_d = (1, shape[0]) if len(shape) == 1 else (-1, shape[-1])
    outs = _adamw(*(t.reshape(two_d) for t in (g, w, m, v)), name=name)
    return tuple(o.reshape(shape) for o in outs)


def _allgather8(v, name):
    m, n = v.shape

    def body(v_ref, out_ref, send_sems, recv_sems, local_sem):
        x, y, c = _my_place()

        def rows(px, py, pc):
            return out_ref.at[pl.ds(pl.multiple_of((4 * px + 2 * py + pc) * m, 8), m), :]

        def peer(k):
            return _flip(x, k & 4), _flip(y, k & 2), _flip(c, k & 1)

        def copy(k, block):
            return pltpu.make_async_remote_copy(
                src_ref=v_ref, dst_ref=rows(*block), send_sem=send_sems.at[k - 1], recv_sem=recv_sems.at[k - 1],
                device_id=peer(k), device_id_type=MESH)

        mine = pltpu.make_async_copy(v_ref, rows(x, y, c), local_sem)
        mine.start()
        sends = [copy(k, (x, y, c)) for k in range(1, N_DEV)]
        for cp in sends:
            cp.start()
        for k in range(1, N_DEV):
            copy(k, peer(k)).wait_recv()
        for cp in sends:
            cp.wait_send()
        mine.wait()

    return pl.pallas_call(
        body, name=name, out_shape=jax.ShapeDtypeStruct((N_DEV * m, n), v.dtype),
        in_specs=[VMEM_FULL], out_specs=VMEM_FULL,
        scratch_shapes=[pltpu.SemaphoreType.DMA((N_DEV - 1,)), pltpu.SemaphoreType.DMA((N_DEV - 1,)),
                        pltpu.SemaphoreType.DMA],
        compiler_params=_params(),
    )(v)


def _chip_peers(x, y, c):
    out = []
    for k in range(1, N_CHIPS):
        px, py = _flip(x, k & 2), _flip(y, k & 1)
        out.append((2 * px + py, (px, py, c)))
    return out


def _gather_weights(shards, kinds, name):
    nw = len(shards)

    def full_shape(a, kind):
        l, r, n = a.shape
        return (l, r, N_CHIPS * n) if kind == "col" else (l, N_CHIPS * r, n)

    def body(*refs):
        ins, outs = refs[:nw], refs[nw:2 * nw]
        send_sems, recv_sems, local_sems = refs[2 * nw:]
        x, y, c = _my_place()
        chip = 2 * x + y

        def window(w, j):
            _, r, n = shards[w].shape
            if kinds[w] == "col":
                return outs[w].at[:, :, pl.ds(pl.multiple_of(j * n, LANES), n)]
            return outs[w].at[:, pl.ds(pl.multiple_of(j * r, 16), r), :]

        def copy(w, k, j, peer):
            return pltpu.make_async_remote_copy(
                src_ref=ins[w], dst_ref=window(w, j), send_sem=send_sems.at[3 * w + k], recv_sem=recv_sems.at[3 * w + k],
                device_id=peer, device_id_type=MESH)

        local = [pltpu.make_async_copy(ins[w], window(w, chip), local_sems.at[w]) for w in range(nw)]
        for cp in local:
            cp.start()
        peers = _chip_peers(x, y, c)
        sends = [copy(w, k, chip, peer) for w in range(nw) for k, (_, peer) in enumerate(peers)]
        for cp in sends:
            cp.start()
        for w in range(nw):
            for k, (pchip, peer) in enumerate(peers):
                copy(w, k, pchip, peer).wait_recv()
        for cp in sends:
            cp.wait_send()
        for cp in local:
            cp.wait()

    return pl.pallas_call(
        body, name=name,
        out_shape=tuple(jax.ShapeDtypeStruct(full_shape(a, kd), a.dtype) for a, kd in zip(shards, kinds)),
        in_specs=[ANY] * nw, out_specs=(ANY,) * nw,
        scratch_shapes=[pltpu.SemaphoreType.DMA((3 * nw,)), pltpu.SemaphoreType.DMA((3 * nw,)),
                        pltpu.SemaphoreType.DMA((nw,))],
        compiler_params=_params(),
    )(*shards)


def _rs_to_sibling(grads, name):
    nw = len(grads)

    def body(*refs):
        ins, outs = refs[:nw], refs[nw:2 * nw]
        send_sems, recv_sems = refs[2 * nw:]
        x, y, c = _my_place()
        sibling = (x, y, 1 - c)
        copies = [pltpu.make_async_remote_copy(
            src_ref=ins[w].at[j, 1 - c], dst_ref=outs[w].at[j], send_sem=send_sems.at[N_CHIPS * w + j],
            recv_sem=recv_sems.at[N_CHIPS * w + j], device_id=sibling, device_id_type=MESH)
            for w in range(nw) for j in range(N_CHIPS)]
        for cp in copies:
            cp.start()
        for cp in copies:
            cp.wait_recv()
        for cp in copies:
            cp.wait_send()

    return pl.pallas_call(
        body, name=name,
        out_shape=tuple(jax.ShapeDtypeStruct((N_CHIPS,) + g.shape[2:], g.dtype) for g in grads),
        in_specs=[ANY] * nw, out_specs=(ANY,) * nw,
        scratch_shapes=[pltpu.SemaphoreType.DMA((N_CHIPS * nw,)), pltpu.SemaphoreType.DMA((N_CHIPS * nw,))],
        compiler_params=_params(),
    )(*grads)


def _rs_to_chips(parts, name):
    nw = len(parts)

    def body(*refs):
        ins, outs = refs[:nw], refs[nw:2 * nw]
        send_sems, recv_sems, local_sems = refs[2 * nw:]
        x, y, c = _my_place()
        chip = 2 * x + y
        peers = _chip_peers(x, y, c)

        def copy(w, k, src_slab, dst_slab, peer):
            return pltpu.make_async_remote_copy(
                src_ref=ins[w].at[src_slab], dst_ref=outs[w].at[dst_slab], send_sem=send_sems.at[3 * w + k],
                recv_sem=recv_sems.at[3 * w + k], device_id=peer, device_id_type=MESH)

        local = [pltpu.make_async_copy(ins[w].at[chip], outs[w].at[chip], local_sems.at[w]) for w in range(nw)]
        for cp in local:
            cp.start()
        sends = [copy(w, k, pchip, chip, peer) for w in range(nw) for k, (pchip, peer) in enumerate(peers)]
        for cp in sends:
            cp.start()
        for w in range(nw):
            for k, (pchip, peer) in enumerate(peers):
                copy(w, k, chip, pchip, peer).wait_recv()
        for cp in sends:
            cp.wait_send()
        for cp in local:
            cp.wait()

    return pl.pallas_call(
        body, name=name, out_shape=tuple(jax.ShapeDtypeStruct(p.shape, p.dtype) for p in parts),
        in_specs=[ANY] * nw, out_specs=(ANY,) * nw,
        scratch_shapes=[pltpu.SemaphoreType.DMA((3 * nw,)), pltpu.SemaphoreType.DMA((3 * nw,)),
                        pltpu.SemaphoreType.DMA((nw,))],
        compiler_params=_params(),
    )(*parts)


def _rs_share_halves(halves, name):
    nw = len(halves)
    nl = len(halves[0])
    flat = [h for hs in halves for h in hs]

    def body(*refs):
        ins, outs = refs[:nw * nl], refs[nw * nl:nw * nl + nw]
        send_sems, recv_sems, local_sems = refs[nw * nl + nw:]
        x, y, c = _my_place()
        sibling = (x, y, 1 - c)
        local, sends, recvs = [], [], []
        for w in range(nw):
            for l in range(nl):
                n = nl * w + l
                local.append(pltpu.make_async_copy(ins[n], outs[w].at[l, c], local_sems.at[n]))
                sends.append(pltpu.make_async_remote_copy(
                    src_ref=ins[n], dst_ref=outs[w].at[l, c], send_sem=send_sems.at[n], recv_sem=recv_sems.at[n],
                    device_id=sibling, device_id_type=MESH))
                recvs.append(pltpu.make_async_remote_copy(
                    src_ref=ins[n], dst_ref=outs[w].at[l, 1 - c], send_sem=send_sems.at[n], recv_sem=recv_sems.at[n],
                    device_id=sibling, device_id_type=MESH))
        for cp in local + sends:
            cp.start()
        for cp in recvs:
            cp.wait_recv()
        for cp in sends:
            cp.wait_send()
        for cp in local:
            cp.wait()

    return pl.pallas_call(
        body, name=name,
        out_shape=tuple(jax.ShapeDtypeStruct((nl, 2) + hs[0].shape, hs[0].dtype) for hs in halves),
        in_specs=[ANY] * (nw * nl), out_specs=(ANY,) * nw,
        scratch_shapes=[pltpu.SemaphoreType.DMA((nw * nl,)), pltpu.SemaphoreType.DMA((nw * nl,)),
                        pltpu.SemaphoreType.DMA((nw * nl,))],
        compiler_params=_params(),
    )(*flat)


def _add_own_half(grad, got, c_idx, name):
    _, _, r, n = grad.shape
    tr = _pick(r, 256, 8)

    def body(c_ref, g_ref, t_ref, o_ref):
        o_ref[...] = g_ref[...] + t_ref[...]

    return pl.pallas_call(
        body, name=name, out_shape=jax.ShapeDtypeStruct((N_CHIPS, r, n), F32),
        grid_spec=pltpu.PrefetchScalarGridSpec(
            num_scalar_prefetch=1, grid=(N_CHIPS, r // tr),
            in_specs=[pl.BlockSpec((None, None, tr, n), lambda j, i, c_ref: (j, c_ref[0], i, 0)),
                      pl.BlockSpec((None, tr, n), lambda j, i, c_ref: (j, i, 0))],
            out_specs=pl.BlockSpec((None, tr, n), lambda j, i, c_ref: (j, i, 0))),
        compiler_params=_params(),
    )(c_idx, grad, got)


def _sum_slabs(slabs, name):
    ns, r, n = slabs.shape
    tr = _pick(r, 256, 8)

    def body(s_ref, o_ref):
        acc = s_ref[0]
        for j in range(1, ns):
            acc = acc + s_ref[j]
        o_ref[...] = acc

    return pl.pallas_call(
        body, name=name, out_shape=jax.ShapeDtypeStruct((r, n), F32), grid=(r // tr,),
        in_specs=[pl.BlockSpec((ns, tr, n), lambda i: (0, i, 0))], out_specs=pl.BlockSpec((tr, n), lambda i: (i, 0)),
        compiler_params=_params(),
    )(slabs)


def _band_bias(rel_bias):
    h = rel_bias.shape[0]
    n_far = PAD - REL_CLIP + CHUNK
    far = jnp.broadcast_to(rel_bias[:, N_REL - 1:N_REL], (h, n_far))
    near = rel_bias[:, REL_CLIP - CHUNK + 1:N_REL - 1][:, ::-1]
    line = jnp.concatenate([far, near], axis=1)
    return jnp.stack([line[:, CHUNK - 1 - q:CHUNK - 1 - q + BAND] for q in range(CHUNK)], axis=1)


def _pack_rows(pieces):
    flat = jnp.concatenate([p.reshape(-1) for p in pieces])
    rows = -(-flat.shape[0] // (8 * LANES)) * 8
    return jnp.pad(flat, (0, rows * LANES - flat.shape[0])).reshape(rows, LANES)


def _unpack_rows(packed, shapes):
    flat = packed.reshape(-1)
    out, at = [], 0
    for shp in shapes:
        size = 1
        for n in shp:
            size *= n
        out.append(flat[at:at + size].reshape(shp))
        at += size
    return out


def _layer_fwd(x, mod, w, band, tag):
    s, d = x.shape
    row = lambda i: mod[i:i + 1]
    h1 = _norm_mod(x, row(1), row(0), f"norm_mix{tag}")
    proj = _matmul(h1, w["w_in"], form="nn", out_dtype=BF16, tm=_pick(s, 512), tn=_pick(w["w_in"].shape[1], 768),
                   tk=d, name=f"proj{tag}")
    oa = _attn_a_fwd(proj, band, f"attn_a{tag}")
    ob, ltot = _sb_fwd(proj, f"attn_b{tag}")
    cat = _out_norm(oa, ob, w["g_a"], w["g_b"], f"out_norm{tag}")
    mixed = _matmul(cat, w["w_out"], form="nn", out_dtype=F32, tm=_pick(s, 512), tn=_pick(d, 1024),
                    tk=cat.shape[1], name=f"mix_out{tag}")
    x1 = _residual(x, row(2), mixed, f"res_mix{tag}")
    h2 = _norm_mod(x1, row(4), row(3), f"norm_ffn{tag}")
    f2 = w["w_up"].shape[1]
    u = _matmul(h2, w["w_up"], form="nn", out_dtype=F32, tm=_pick(s, 512), tn=_pick(f2, 1408), tk=d, name=f"up{tag}",
                rider=up_rider)
    if up_rider is not None:
        u, arrived = u
        on_up_arrival(arrived)
    a = _conv_glu(u, w["conv_w"], w["conv_b"], f"conv_glu{tag}")
    f = _matmul(a, w["w_down"], form="nn", out_dtype=F32, tm=_pick(s, 512), tn=_pick(d, 1024),
                tk=_pick(f2 // 2, 2816), name=f"down{tag}")
    x2 = _residual(x1, row(5), f, f"res_ffn{tag}")
    saved = dict(x=x, h1=h1, proj=proj, oa=oa, ob=ob, ltot=ltot, cat=cat, mixed=mixed, x1=x1, h2=h2, u=u, a=a, f=f)
    return x2, saved


def _layer_bwd(dx2, sv, mod, w, band, tag):
    s, d = dx2.shape
    row = lambda i: mod[i:i + 1]
    f2 = w["w_up"].shape[1]
    ff = f2 // 2
    n_in = w["w_in"].shape[1]
    df, dgate_ffn = _gate_bwd(dx2, sv["f"], row(5), f"gate_ffn_bwd{tag}")
    da = _matmul(df, w["w_down"], form="nt", out_dtype=F32, tm=_pick(s, 512), tn=_pick(ff, 1408), tk=d, name=f"down_dx{tag}")
    g_down = _matmul(sv["a"], df, form="tn", out_dtype=F32, tm=_pick(ff, 1408), tn=_pick(d, 512), tk=_pick(s, 2048),
                     name=f"down_dw{tag}")
    if waiting is None:
        du2, dcw, dcb = _conv_glu_bwd(sv["u"], da, w["conv_w"], w["conv_b"], f"conv_glu_bwd{tag}")
    else:
        du2, dcw, dcb, older_from_sib = _conv_glu_bwd(sv["u"], da, w["conv_w"], w["conv_b"], f"conv_glu_bwd{tag}",
                                                      _sibling_rider(list(waiting.values())))
    dh2 = _matmul(du2, w["w_up"], form="nt", out_dtype=F32, tm=_pick(s, 512), tn=_pick(d, 1024), tk=_pick(ff, 2816),
                  name=f"up_dx{tag}", halves=True)
    g_up = _matmul(sv["h2"], du2, form="tn", out_dtype=F32, tm=_pick(d, 512), tn=_pick(f2 // N_CHIPS, 1408),
                   tk=_pick(s, 2048), name=f"up_dw{tag}", shard_cols=f2 // N_CHIPS, halves=True)
    dx1, dscale_ffn, dshift_ffn = _norm_mod_bwd(sv["x1"], dh2, dx2, row(4), f"norm_ffn_bwd{tag}")
    dmixed, dgate_mix = _gate_bwd(dx1, sv["mixed"], row(2), f"gate_mix_bwd{tag}")
    dcat = _matmul(dmixed, w["w_out"], form="nt", out_dtype=F32, tm=_pick(s, 512), tn=_pick(2 * W_GRP, 1024), tk=d,
                   name=f"mix_out_dx{tag}")
    g_out = _matmul(sv["cat"], dmixed, form="tn", out_dtype=F32, tm=_pick(2 * W_GRP, 512), tn=_pick(d, 1024),
                    tk=_pick(s, 2048), name=f"mix_out_dw{tag}")
    doa, dob, dg_a, dg_b = _out_norm_bwd(sv["oa"], sv["ob"], dcat, w["g_a"], w["g_b"], f"out_norm_bwd{tag}")
    dqa, dka, dva, dband = _attn_a_bwd(sv["proj"], band, doa, f"attn_a_bwd{tag}")
    dqb, dkb, dvb = _sb_bwd(sv["proj"], sv["ltot"], dob, f"attn_b_bwd{tag}")
    drel = _rel_bias_grad(jnp.transpose(dband, (1, 0, 2)), f"rel_bias_bwd{tag}")[:, :N_REL]
    dproj = jnp.concatenate([dqa, dka, dva, dqb, dkb, dvb], axis=1).astype(BF16)
    dh1 = _matmul(dproj, w["w_in"], form="nt", out_dtype=F32, tm=_pick(s, 512), tn=_pick(d, 1024), tk=_pick(n_in, 3072),
                  name=f"proj_dx{tag}")
    g_in = _matmul(sv["h1"], dproj, form="tn", out_dtype=F32, tm=_pick(d, 512), tn=_pick(n_in // N_CHIPS, 768),
                   tk=_pick(s, 2048), name=f"proj_dw{tag}", shard_cols=n_in // N_CHIPS)
    dx, dscale_mix, dshift_mix = _norm_mod_bwd(sv["x"], dh1, dx1, row(1), f"norm_mix_bwd{tag}")
    dmod = jnp.concatenate([dshift_mix, dscale_mix, dgate_mix, dshift_ffn, dscale_ffn, dgate_ffn], axis=1)
    big = dict(w_in=g_in, w_out=g_out, w_up=g_up, w_down=g_down)
    dconv_w = jnp.concatenate([dcw[0], dcw[1]], axis=1)
    dconv_b = jnp.concatenate([dcb[0], dcb[1]], axis=1)
    small = dict(dmod=dmod, rel_bias=drel, g_a=dg_a, g_b=dg_b, conv_w=dconv_w, conv_b=dconv_b)
    return dx, big, small


def _kernel_unoverlapped(x, c, w_ada, b_ada, w_in, rel_bias, g_a, g_b, w_out, w_up, conv_w, conv_b, w_down, final_g, loss_target, m_w_ada, m_b_ada, m_w_in, m_rel_bias, m_g_a, m_g_b, m_w_out, m_w_up, m_conv_w, m_conv_b, m_w_down, m_final_g, v_w_ada, v_b_ada, v_w_in, v_rel_bias, v_g_a, v_g_b, v_w_out, v_w_up, v_conv_w, v_conv_b, v_w_down, v_final_g):
    xi, yi, ci = _my_place()
    chip = 2 * xi + yi
    dev = 4 * xi + 2 * yi + ci
    nl, d, n_ada = w_ada.shape
    s = x.shape[1]
    f2 = N_CHIPS * w_up.shape[2]
    nc = conv_w.shape[2]

    c_pad = jnp.pad(c, ((0, 7), (0, 0)))
    c_all = _allgather8(c_pad, "gather_c")[0::8]
    b_mine = lax.dynamic_slice_in_dim(b_ada, chip * n_ada, n_ada, axis=1)[:, None, :]
    mod_shard = _ada_fwd(c_all, w_ada, b_mine, "ada")
    pack2 = _pack_rows([mod_shard, conv_w])
    got2 = _allgather8(pack2, "gather_mod").reshape(N_DEV, -1)
    mods, convs = [], []
    for j in range(N_CHIPS):
        ms, cw = _unpack_rows(got2[2 * j], [mod_shard.shape, conv_w.shape])
        mods.append(lax.dynamic_index_in_dim(ms, dev, axis=1, keepdims=False))
        convs.append(cw)
    mod = jnp.concatenate(mods, axis=1).reshape(nl, 6, d)
    conv_w_full = jnp.concatenate(convs, axis=2)

    names = ("w_in", "w_out", "w_up", "w_down")
    kinds = ("col", "row", "col", "row")
    shards = dict(w_in=w_in, w_out=w_out, w_up=w_up, w_down=w_down)
    full = _gather_weights([shards[n].astype(BF16) for n in names], kinds, "gather_weights")
    full = dict(zip(names, full))

    xs = x[0]
    layers, saved, bands = [], [], []
    for l in range(nl):
        w = {n: full[n][l] for n in names}
        w.update(g_a=g_a[l:l + 1], g_b=g_b[l:l + 1], conv_w=conv_w_full[l], conv_b=conv_b[l:l + 1])
        band = _band_bias(rel_bias[l])
        xs, sv = _layer_fwd(xs, mod[l], w, band, f"_l{l}")
        layers.append(w)
        bands.append(band)
        saved.append(sv)
    loss_part, dx, dfinal_g = _loss_head(xs, final_g[None, :], loss_target[0], "loss_head")
    loss = lax.psum(loss_part[0, 0], ("x", "y", "c"))

    big, small = [None] * nl, [None] * nl
    for l in reversed(range(nl)):
        dx, big[l], small[l] = _layer_bwd(dx, saved[l], mod[l], layers[l], bands[l], f"_l{l}")

    small_names = ("dmod", "rel_bias", "g_a", "g_b", "conv_w", "conv_b")
    pieces = [small[l][n] for l in range(nl) for n in small_names] + [dfinal_g]
    shapes = [p.shape for p in pieces]
    pack3 = _pack_rows(pieces)
    got3 = _allgather8(pack3, "gather_small").reshape(N_DEV, pack3.shape[0], LANES)
    summed = _unpack_rows(_sum_slabs(got3, "sum_small"), shapes)
    tot = [dict(zip(small_names, summed[len(small_names) * l:len(small_names) * (l + 1)])) for l in range(nl)]
    g_final_g = summed[-1].reshape(-1)
    g_b_ada = jnp.stack([tot[l]["dmod"].reshape(-1) for l in range(nl)])
    g_rel = jnp.stack([tot[l]["rel_bias"] for l in range(nl)])
    g_ga = jnp.stack([tot[l]["g_a"].reshape(-1) for l in range(nl)])
    g_gb = jnp.stack([tot[l]["g_b"].reshape(-1) for l in range(nl)])
    g_conv_b = jnp.stack([tot[l]["conv_b"].reshape(-1) for l in range(nl)])
    g_conv_w = jnp.stack([lax.dynamic_slice_in_dim(tot[l]["conv_w"], chip * nc, nc, axis=1) for l in range(nl)])
    per_dev = [_unpack_rows(got3[j], shapes) for j in range(N_DEV)]
    dmod_all = jnp.stack([jnp.stack([per_dev[j][len(small_names) * l].reshape(-1) for j in range(N_DEV)])
                          for l in range(nl)])
    g_w_ada = _ada_bwd(c_all, lax.dynamic_slice_in_dim(dmod_all, chip * n_ada, n_ada, axis=2), "ada_bwd")

    order = [(n, l) for n in names for l in range(nl)]
    flat_g = [big[l][n].reshape(N_CHIPS, 2, -1, 1024) for n, l in order]
    from_sib = _rs_to_sibling(flat_g, "rs_sibling")
    c_idx = jnp.reshape(ci, (1,)).astype(jnp.int32)
    chip_part = [_add_own_half(g, t, c_idx, f"rs_add_{n}_l{l}") for g, t, (n, l) in zip(flat_g, from_sib, order)]
    from_chips = _rs_to_chips(chip_part, "rs_chips")
    my_half = [_sum_slabs(t, f"rs_sum_{n}_l{l}") for t, (n, l) in zip(from_chips, order)]
    shard_g = _rs_share_halves([[my_half[nl * i + l] for l in range(nl)] for i in range(len(names))], "rs_halves")
    g_big = {n: shard_g[i].reshape(shards[n].shape) for i, n in enumerate(names)}

    grads = dict(w_ada=g_w_ada, b_ada=g_b_ada, rel_bias=g_rel, g_a=g_ga, g_b=g_gb, conv_w=g_conv_w, conv_b=g_conv_b,
                 final_g=g_final_g)
    weights = dict(w_ada=w_ada, b_ada=b_ada, w_in=w_in, rel_bias=rel_bias, g_a=g_a, g_b=g_b, w_out=w_out, w_up=w_up,
                   conv_w=conv_w, conv_b=conv_b, w_down=w_down, final_g=final_g)
    m_in = dict(w_ada=m_w_ada, b_ada=m_b_ada, w_in=m_w_in, rel_bias=m_rel_bias, g_a=m_g_a, g_b=m_g_b, w_out=m_w_out,
                w_up=m_w_up, conv_w=m_conv_w, conv_b=m_conv_b, w_down=m_w_down, final_g=m_final_g)
    v_in = dict(w_ada=v_w_ada, b_ada=v_b_ada, w_in=v_w_in, rel_bias=v_rel_bias, g_a=v_g_a, g_b=v_g_b, w_out=v_w_out,
                w_up=v_w_up, conv_w=v_conv_w, conv_b=v_conv_b, w_down=v_w_down, final_g=v_final_g)
    order_w = ("w_ada", "b_ada", "w_in", "rel_bias", "g_a", "g_b", "w_out", "w_up", "conv_w", "conv_b", "w_down", "final_g")
    upd = {n: _adamw_nd(grads[n], weights[n], m_in[n], v_in[n], f"adamw_{n}") for n in grads}
    for n, mn, th in zip(BIG, mine, theirs):
        grads[n], *upd[n] = _adamw_halves(mn, th, c_idx, weights[n], m_in[n], v_in[n], f"adamw_{n}")
    return (loss, dx[None], *[grads[n] for n in order_w], *[upd[n][0] for n in order_w],
            *[upd[n][1] for n in order_w], *[upd[n][2] for n in order_w])


class _Rider:
    def __init__(self, ins, out_shapes, n_remote, n_local, parts):
        self.ins = list(ins)
        self.out_shapes = list(out_shapes)
        self.scratch = [pltpu.SemaphoreType.DMA((n_remote,)), pltpu.SemaphoreType.DMA((n_remote,)),
                        pltpu.SemaphoreType.DMA((max(n_local, 1),))]
        self.parts = parts

    def start(self, in_refs, out_refs, sems):
        local, sends, _ = self.parts(in_refs, out_refs, sems)
        for cp in local() + sends():
            cp.start()

    def wait(self, in_refs, out_refs, sems):
        local, sends, recvs = self.parts(in_refs, out_refs, sems)
        for cp in recvs():
            cp.wait_recv()
        for cp in sends():
            cp.wait_send()
        for cp in local():
            cp.wait()


class _JoinedRider:
    def __init__(self, riders):
        self.riders = riders
        self.ins = [a for r in riders for a in r.ins]
        self.out_shapes = [o for r in riders for o in r.out_shapes]
        self.scratch = [sc for r in riders for sc in r.scratch]

    def _each(self, in_refs, out_refs, sems):
        i = o = sc = 0
        for r in self.riders:
            yield (r, in_refs[i:i + len(r.ins)], out_refs[o:o + len(r.out_shapes)], sems[sc:sc + len(r.scratch)])
            i, o, sc = i + len(r.ins), o + len(r.out_shapes), sc + len(r.scratch)

    def start(self, in_refs, out_refs, sems):
        for r, i, o, sc in self._each(in_refs, out_refs, sems):
            r.start(i, o, sc)

    def wait(self, in_refs, out_refs, sems):
        for r, i, o, sc in self._each(in_refs, out_refs, sems):
            r.wait(i, o, sc)

    def split(self, results):
        out, at = [], 0
        for r in self.riders:
            out.append(list(results[at:at + len(r.out_shapes)]))
            at += len(r.out_shapes)
        return out


def _call_with_rider(body, rider, *, name, grid, out_shape, in_specs, out_specs, scratch_shapes, args):
    if rider is None:
        return pl.pallas_call(body, name=name, grid=grid, out_shape=tuple(out_shape), in_specs=list(in_specs),
                              out_specs=tuple(out_specs), scratch_shapes=list(scratch_shapes),
                              compiler_params=_params())(*args)
    n_in, n_out, n_scr = len(in_specs), len(out_specs), len(scratch_shapes)
    r_in, r_out = len(rider.ins), len(rider.out_shapes)

    def both(*refs):
        at = 0
        groups = []
        for size in (n_in, r_in, n_out, r_out, n_scr, len(rider.scratch)):
            groups.append(refs[at:at + size])
            at += size
        own_in, ride_in, own_out, ride_out, own_scr, sems = groups
        steps = [pl.program_id(a) for a in range(len(grid))]
        first = functools.reduce(jnp.logical_and, [st == 0 for st in steps])
        last = functools.reduce(jnp.logical_and, [st == g - 1 for st, g in zip(steps, grid)])

        @pl.when(first)
        def _():
            rider.start(ride_in, ride_out, sems)

        body(*own_in, *own_out, *own_scr)

        @pl.when(last)
        def _():
            rider.wait(ride_in, ride_out, sems)

    outs = pl.pallas_call(
        both, name=name, grid=grid, out_shape=tuple(out_shape) + tuple(rider.out_shapes),
        in_specs=list(in_specs) + [ANY] * r_in, out_specs=tuple(out_specs) + (ANY,) * r_out,
        scratch_shapes=list(scratch_shapes) + rider.scratch, compiler_params=_params(),
    )(*args, *rider.ins)
    return tuple(outs[:n_out]) + (list(outs[n_out:]),)


def _run_rider(rider, name):
    r_in, r_out = len(rider.ins), len(rider.out_shapes)

    def body(*refs):
        ins, outs, sems = refs[:r_in], refs[r_in:r_in + r_out], refs[r_in + r_out:]
        rider.start(ins, outs, sems)
        rider.wait(ins, outs, sems)

    return list(pl.pallas_call(
        body, name=name, out_shape=tuple(rider.out_shapes), in_specs=[ANY] * r_in, out_specs=(ANY,) * r_out,
        scratch_shapes=rider.scratch, compiler_params=_params(),
    )(*rider.ins))


def _remote(src, dst, sems, n, peer):
    return pltpu.make_async_remote_copy(src_ref=src, dst_ref=dst, send_sem=sems[0].at[n], recv_sem=sems[1].at[n],
                                        device_id=peer, device_id_type=MESH)


def _gather_rider(shards, kinds):
    nw = len(shards)
    out_shapes = [jax.ShapeDtypeStruct((a.shape[0], N_CHIPS * a.shape[1]) if kd == "col" else
                                       (N_CHIPS * a.shape[0], a.shape[1]), a.dtype) for a, kd in zip(shards, kinds)]

    def parts(ins, outs, sems):
        x, y, c = _my_place()
        chip = 2 * x + y
        peers = _chip_peers(x, y, c)

        def window(w, j):
            r, n = shards[w].shape
            if kinds[w] == "col":
                return outs[w].at[:, pl.ds(pl.multiple_of(j * n, LANES), n)]
            return outs[w].at[pl.ds(pl.multiple_of(j * r, 16), r), :]

        local = lambda: [pltpu.make_async_copy(ins[w], window(w, chip), sems[2].at[w]) for w in range(nw)]
        sends = lambda: [_remote(ins[w], window(w, chip), sems, 3 * w + k, peer)
                         for w in range(nw) for k, (_, peer) in enumerate(peers)]
        recvs = lambda: [_remote(ins[w], window(w, pchip), sems, 3 * w + k, peer)
                         for w in range(nw) for k, (pchip, peer) in enumerate(peers)]
        return local, sends, recvs

    return _Rider(shards, out_shapes, 3 * nw, nw, parts)


def _half(ref3, h, rows):
    return ref3.at[:, pl.ds(pl.multiple_of(h * rows, 8), rows), :]


def _sibling_rider(grads):
    nw = len(grads)
    out_shapes = [jax.ShapeDtypeStruct((g.shape[0], g.shape[1] // 2, g.shape[2]), g.dtype) for g in grads]

    def parts(ins, outs, sems):
        x, y, c = _my_place()
        sibling = (x, y, 1 - c)
        def copies():
            out = []
            for w in range(nw):
                rows = grads[w].shape[1] // 2
                for j in range(N_CHIPS):
                    src = ins[w].at[j, pl.ds(pl.multiple_of((1 - c) * rows, 8), rows), :]
                    out.append(_remote(src, outs[w].at[j], sems, N_CHIPS * w + j, sibling))
            return out

        return (lambda: []), copies, copies

    return _Rider(grads, out_shapes, N_CHIPS * nw, 0, parts)


def _chips_rider(parts_in):
    nw = len(parts_in)
    out_shapes = [jax.ShapeDtypeStruct(p.shape, p.dtype) for p in parts_in]

    def parts(ins, outs, sems):
        x, y, c = _my_place()
        chip = 2 * x + y
        peers = _chip_peers(x, y, c)
        local = lambda: []
        sends = lambda: [_remote(ins[w].at[pchip], outs[w].at[chip], sems, 3 * w + k, peer)
                         for w in range(nw) for k, (pchip, peer) in enumerate(peers)]
        recvs = lambda: [_remote(ins[w].at[chip], outs[w].at[pchip], sems, 3 * w + k, peer)
                         for w in range(nw) for k, (pchip, peer) in enumerate(peers)]
        return local, sends, recvs

    return _Rider(parts_in, out_shapes, 3 * nw, nw, parts)


def _halves_rider(halves):
    nw, nl = len(halves), len(halves[0])
    flat = [h for hs in halves for h in hs]
    out_shapes = [jax.ShapeDtypeStruct((nl, 2 * hs[0].shape[0], hs[0].shape[1]), hs[0].dtype) for hs in halves]

    def parts(ins, outs, sems):
        x, y, c = _my_place()
        sibling = (x, y, 1 - c)

        def window(w, l, h):
            rows = halves[w][0].shape[0]
            return outs[w].at[l, pl.ds(pl.multiple_of(h * rows, 8), rows), :]

        pairs = [(w, l) for w in range(nw) for l in range(nl)]
        local = lambda: [pltpu.make_async_copy(ins[nl * w + l], window(w, l, c), sems[2].at[nl * w + l]) for w, l in pairs]
        sends = lambda: [_remote(ins[nl * w + l], window(w, l, c), sems, nl * w + l, sibling) for w, l in pairs]
        recvs = lambda: [_remote(ins[nl * w + l], window(w, l, 1 - c), sems, nl * w + l, sibling) for w, l in pairs]
        return local, sends, recvs

    return _Rider(flat, out_shapes, nw * nl, nw * nl, parts)


def _add_my_half(grad, got, c_idx, name):
    _, r, n = got.shape
    tr = _pick(r, 256, 16)
    nblk = r // tr

    def body(c_ref, g_ref, t_ref, o_ref, ob_ref):
        tot = g_ref[...] + t_ref[...]
        o_ref[...] = tot
        ob_ref[...] = tot.astype(BF16)

    blk = pl.BlockSpec((None, tr, n), lambda j, i, c_ref: (j, i, 0))
    return pl.pallas_call(
        body, name=name, out_shape=(jax.ShapeDtypeStruct(got.shape, F32), jax.ShapeDtypeStruct(got.shape, BF16)),
        grid_spec=pltpu.PrefetchScalarGridSpec(
            num_scalar_prefetch=1, grid=(N_CHIPS, nblk),
            in_specs=[pl.BlockSpec((None, tr, n), lambda j, i, c_ref: (j, c_ref[0] * nblk + i, 0)), blk],
            out_specs=(blk, blk)),
        compiler_params=_params(),
    )(c_idx, grad, got)


def _swap_rider(mine):
    nw = len(mine)
    out_shapes = [jax.ShapeDtypeStruct(a.shape, a.dtype) for a in mine]

    def parts(ins, outs, sems):
        x, y, c = _my_place()
        copies = lambda: [_remote(ins[w], outs[w], sems, w, (x, y, 1 - c)) for w in range(nw)]
        return (lambda: []), copies, copies

    return _Rider(mine, out_shapes, nw, 0, parts)


def _sum_layers(own, got, chip_idx, name):
    nl = len(own)
    _, r, n = own[0].shape
    tr = _pick(r, 256, 16)

    def body(chip_ref, *refs):
        o_ref = refs[4 * nl]
        for l in range(nl):
            acc = refs[4 * l][...]
            for k in range(1, N_CHIPS):
                acc = acc + refs[4 * l + k][...].astype(F32)
            o_ref[l] = acc

    slab = lambda k: pl.BlockSpec((None, tr, n), lambda i, chip_ref: (chip_ref[0] ^ k, i, 0))
    return pl.pallas_call(
        body, name=name, out_shape=jax.ShapeDtypeStruct((nl, r, n), F32),
        grid_spec=pltpu.PrefetchScalarGridSpec(
            num_scalar_prefetch=1, grid=(r // tr,), in_specs=[slab(k) for _ in range(nl) for k in range(N_CHIPS)],
            out_specs=pl.BlockSpec((nl, tr, n), lambda i, chip_ref: (0, i, 0))),
        compiler_params=_params(),
    )(chip_idx, *[a for l in range(nl) for a in (own[l], got[l], got[l], got[l])])


def _adam_math(gv, w, m, v):
    c1 = 1.0 - ADAM_B1 ** ADAM_STEP
    c2 = 1.0 - ADAM_B2 ** ADAM_STEP
    nm = ADAM_B1 * m + (1.0 - ADAM_B1) * gv
    nv = ADAM_B2 * v + (1.0 - ADAM_B2) * (gv * gv)
    return -ADAM_LR * ((nm / c1) / (jnp.sqrt(nv / c2) + ADAM_EPS) + ADAM_WD * w), nm, nv


def _adamw_halves(mine, theirs, c_idx, w, m, v, name):
    nl, r, n = mine.shape
    tr = _pick(r, 256, 8)
    nblk = r // tr

    def body(c_ref, mine_ref, theirs_ref, w_ref, m_ref, v_ref, g_ref, d_ref, nm_ref, nv_ref):
        gv = jnp.where(pl.program_id(1) == c_ref[0], mine_ref[...], theirs_ref[...])
        g_ref[...] = gv
        d_ref[...], nm_ref[...], nv_ref[...] = _adam_math(gv, w_ref[...], m_ref[...], v_ref[...])

    half = pl.BlockSpec((None, tr, n), lambda l, h, i, c_ref: (l, i, 0))
    full = pl.BlockSpec((None, tr, n), lambda l, h, i, c_ref: (l, h * nblk + i, 0))
    return pl.pallas_call(
        body, name=name, out_shape=(jax.ShapeDtypeStruct(w.shape, F32),) * 4,
        grid_spec=pltpu.PrefetchScalarGridSpec(
            num_scalar_prefetch=1, grid=(nl, 2, nblk), in_specs=[half, half, full, full, full],
            out_specs=(full,) * 4),
        compiler_params=_params(),
    )(c_idx, mine, theirs, w, m, v)


def _by_chip(g):
    return g if g.ndim == 3 else g.reshape(N_CHIPS, g.shape[0] // N_CHIPS, g.shape[1])


BIG = ("w_in", "w_out", "w_up", "w_down")
BIG_KIND = dict(w_in="col", w_out="row", w_up="col", w_down="row")


def _forward_layer(x, mod, w, band, tag, rider=None, on_arrival=None, up_rider=None, on_up_arrival=None):
    s, d = x.shape
    row = lambda i: mod[i:i + 1]
    h1 = _norm_mod(x, row(1), row(0), f"norm_mix{tag}")
    proj = _matmul(h1, w["w_in"], form="nn", out_dtype=BF16, tm=_pick(s, 512), tn=_pick(w["w_in"].shape[1], 768),
                   tk=d, name=f"proj{tag}")
    oa = _attn_a_fwd(proj, band, f"attn_a{tag}")
    if rider is None:
        ob, ltot = _sb_fwd(proj, f"attn_b{tag}")
    else:
        ob, ltot, arrived = _sb_fwd(proj, f"attn_b{tag}", rider)
        on_arrival(arrived)
    cat = _out_norm(oa, ob, w["g_a"], w["g_b"], f"out_norm{tag}")
    mixed = _matmul(cat, w["w_out"], form="nn", out_dtype=F32, tm=_pick(s, 512), tn=_pick(d, 1024),
                    tk=cat.shape[1], name=f"mix_out{tag}")
    x1 = _residual(x, row(2), mixed, f"res_mix{tag}")
    h2 = _norm_mod(x1, row(4), row(3), f"norm_ffn{tag}")
    f2 = w["w_up"].shape[1]
    u = _matmul(h2, w["w_up"], form="nn", out_dtype=F32, tm=_pick(s, 512), tn=_pick(f2, 1408), tk=d, name=f"up{tag}",
                rider=up_rider)
    if up_rider is not None:
        u, arrived = u
        on_up_arrival(arrived)
    a = _conv_glu(u, w["conv_w"], w["conv_b"], f"conv_glu{tag}")
    f = _matmul(a, w["w_down"], form="nn", out_dtype=F32, tm=_pick(s, 512), tn=_pick(d, 1024),
                tk=_pick(f2 // 2, 2816), name=f"down{tag}")
    x2 = _residual(x1, row(5), f, f"res_ffn{tag}")
    saved = dict(x=x, h1=h1, proj=proj, oa=oa, ob=ob, ltot=ltot, cat=cat, mixed=mixed, x1=x1, h2=h2, u=u, a=a, f=f)
    return x2, saved


def _backward_layer(dx2, sv, mod, w, band, tag, c_idx, waiting=None):
    s, d = dx2.shape
    row = lambda i: mod[i:i + 1]
    f2 = w["w_up"].shape[1]
    ff = f2 // 2
    n_in = w["w_in"].shape[1]
    df, dgate_ffn = _gate_bwd(dx2, sv["f"], row(5), f"gate_ffn_bwd{tag}")
    da = _matmul(df, w["w_down"], form="nt", out_dtype=F32, tm=_pick(s, 512), tn=_pick(ff, 1408), tk=d, name=f"down_dx{tag}")
    g_down = _matmul(sv["a"], df, form="tn", out_dtype=F32, tm=_pick(ff, 1408), tn=_pick(d, 512), tk=_pick(s, 2048),
                     name=f"down_dw{tag}")
    if waiting is None:
        du2, dcw, dcb = _conv_glu_bwd(sv["u"], da, w["conv_w"], w["conv_b"], f"conv_glu_bwd{tag}")
    else:
        du2, dcw, dcb, older_from_sib = _conv_glu_bwd(sv["u"], da, w["conv_w"], w["conv_b"], f"conv_glu_bwd{tag}",
                                                      _sibling_rider(list(waiting.values())))
    dh2 = _matmul(du2, w["w_up"], form="nt", out_dtype=F32, tm=_pick(s, 512), tn=_pick(d, 1024), tk=_pick(ff, 2816),
                  name=f"up_dx{tag}", halves=True)
    g_up = _matmul(sv["h2"], du2, form="tn", out_dtype=F32, tm=_pick(d, 512), tn=_pick(f2 // N_CHIPS, 1408),
                   tk=_pick(s, 2048), name=f"up_dw{tag}", shard_cols=f2 // N_CHIPS, halves=True)
    dx1, dscale_ffn, dshift_ffn = _norm_mod_bwd(sv["x1"], dh2, dx2, row(4), f"norm_ffn_bwd{tag}")
    dmixed, dgate_mix = _gate_bwd(dx1, sv["mixed"], row(2), f"gate_mix_bwd{tag}")
    dcat = _matmul(dmixed, w["w_out"], form="nt", out_dtype=F32, tm=_pick(s, 512), tn=_pick(2 * W_GRP, 1024), tk=d,
                   name=f"mix_out_dx{tag}")
    g_out = _matmul(sv["cat"], dmixed, form="tn", out_dtype=F32, tm=_pick(2 * W_GRP, 512), tn=_pick(d, 1024),
                    tk=_pick(s, 2048), name=f"mix_out_dw{tag}")
    doa, dob, dg_a, dg_b = _out_norm_bwd(sv["oa"], sv["ob"], dcat, w["g_a"], w["g_b"], f"out_norm_bwd{tag}")
    big = {("w_down", tag): _by_chip(g_down), ("w_up", tag): _by_chip(g_up), ("w_out", tag): _by_chip(g_out)}
    if waiting is None:
        dqa, dka, dva, dband = _attn_a_bwd(sv["proj"], band, doa, f"attn_a_bwd{tag}")
        dqb, dkb, dvb = _sb_bwd(sv["proj"], sv["ltot"], dob, f"attn_b_bwd{tag}")
    else:
        old_keys, keys = list(waiting), list(big)
        add = lambda raw, k, t: _add_my_half(raw[k], t, c_idx, f"rs_add_{k[0]}{k[1]}")
        old_parts = [add(waiting, k, t) for k, t in zip(old_keys, older_from_sib)]
        both = _JoinedRider([_chips_rider([p16 for _, p16 in old_parts]), _sibling_rider([big[k] for k in keys])])
        dqa, dka, dva, dband, arrived = _attn_a_bwd(sv["proj"], band, doa, f"attn_a_bwd{tag}", both)
        old_got, from_sib = both.split(arrived)
        parts = [add(big, k, t) for k, t in zip(keys, from_sib)]
        dqb, dkb, dvb, got = _sb_bwd(sv["proj"], sv["ltot"], dob, f"attn_b_bwd{tag}",
                                     _chips_rider([p16 for _, p16 in parts]))
        big = {k: (p32, g) for k, (p32, _), g in zip(old_keys + keys, old_parts + parts, list(old_got) + list(got))}
    drel = _rel_bias_grad(jnp.transpose(dband, (1, 0, 2)), f"rel_bias_bwd{tag}")[:, :N_REL]
    dproj = jnp.concatenate([dqa, dka, dva, dqb, dkb, dvb], axis=1).astype(BF16)
    dh1 = _matmul(dproj, w["w_in"], form="nt", out_dtype=F32, tm=_pick(s, 512), tn=_pick(d, 1024), tk=_pick(n_in, 3072),
                  name=f"proj_dx{tag}")
    g_in = _matmul(sv["h1"], dproj, form="tn", out_dtype=F32, tm=_pick(d, 512), tn=_pick(n_in // N_CHIPS, 768),
                   tk=_pick(s, 2048), name=f"proj_dw{tag}", shard_cols=n_in // N_CHIPS)
    dx, dscale_mix, dshift_mix = _norm_mod_bwd(sv["x"], dh1, dx1, row(1), f"norm_mix_bwd{tag}")
    dmod = jnp.concatenate([dshift_mix, dscale_mix, dgate_mix, dshift_ffn, dscale_ffn, dgate_ffn], axis=1)
    dconv_w = jnp.concatenate([dcw[0], dcw[1]], axis=1)
    dconv_b = jnp.concatenate([dcb[0], dcb[1]], axis=1)
    small = dict(dmod=dmod, rel_bias=drel, g_a=dg_a, g_b=dg_b, conv_w=dconv_w, conv_b=dconv_b)
    return dx, big, g_in, small


def kernel(x, c, w_ada, b_ada, w_in, rel_bias, g_a, g_b, w_out, w_up, conv_w, conv_b, w_down, final_g, loss_target, m_w_ada, m_b_ada, m_w_in, m_rel_bias, m_g_a, m_g_b, m_w_out, m_w_up, m_conv_w, m_conv_b, m_w_down, m_final_g, v_w_ada, v_b_ada, v_w_in, v_rel_bias, v_g_a, v_g_b, v_w_out, v_w_up, v_conv_w, v_conv_b, v_w_down, v_final_g):
    xi, yi, ci = _my_place()
    chip = 2 * xi + yi
    dev = 4 * xi + 2 * yi + ci
    c_idx = jnp.reshape(ci, (1,)).astype(jnp.int32)
    nl, d, n_ada = w_ada.shape
    nc = conv_w.shape[2]
    assert nl == 2

    c_pad = jnp.pad(c, ((0, 7), (0, 0)))
    c_all = _allgather8(c_pad, "gather_c")[0::8]
    b_mine = lax.dynamic_slice_in_dim(b_ada, chip * n_ada, n_ada, axis=1)[:, None, :]
    mod_shard = _ada_fwd(c_all, w_ada, b_mine, "ada")
    pack2 = _pack_rows([mod_shard, conv_w])
    got2 = _allgather8(pack2, "gather_mod").reshape(N_DEV, -1)
    mods, convs = [], []
    for j in range(N_CHIPS):
        ms, cw = _unpack_rows(got2[2 * j], [mod_shard.shape, conv_w.shape])
        mods.append(lax.dynamic_index_in_dim(ms, dev, axis=1, keepdims=False))
        convs.append(cw)
    mod = jnp.concatenate(mods, axis=1).reshape(nl, 6, d)
    conv_w_full = jnp.concatenate(convs, axis=2)

    shards = dict(w_in=w_in, w_out=w_out, w_up=w_up, w_down=w_down)
    sh = {(n, l): shards[n][l].astype(BF16) for n in BIG for l in range(nl)}
    early = [("w_in", 0)]
    riding = [[("w_out", 0), ("w_up", 0), ("w_down", 0)], [("w_out", 1), ("w_up", 1), ("w_down", 1)]]
    riding_up = [[("w_in", 1)], None]
    layers = [dict(g_a=g_a[l:l + 1], g_b=g_b[l:l + 1], conv_w=conv_w_full[l], conv_b=conv_b[l:l + 1]) for l in range(nl)]

    def gather_rider(keys):
        return _gather_rider([sh[k] for k in keys], [BIG_KIND[k[0]] for k in keys])

    def arrival(keys):
        def fill(arrived):
            for (n, l), full in zip(keys, arrived):
                layers[l][n] = full
        return fill

    arrival(early)(_run_rider(gather_rider(early), "gather_first"))

    xs = x[0]
    saved, bands = [], []
    for l in range(nl):
        band = _band_bias(rel_bias[l])
        ups = riding_up[l]
        xs, sv = _forward_layer(xs, mod[l], layers[l], band, f"_l{l}", gather_rider(riding[l]), arrival(riding[l]),
                                gather_rider(ups) if ups else None, arrival(ups) if ups else None)
        bands.append(band)
        saved.append(sv)
    loss_part, dx, dfinal_g = _loss_head(xs, final_g[None, :], loss_target[0], "loss_head")
    loss = lax.psum(loss_part[0, 0], ("x", "y", "c"))

    small = [None] * nl
    dx, raw1, g_in1, small[1] = _backward_layer(dx, saved[1], mod[1], layers[1], bands[1], "_l1", c_idx)
    raw1[("w_in", "_l1")] = g_in1
    dx, parts, g_in0, small[0] = _backward_layer(dx, saved[0], mod[0], layers[0], bands[0], "_l0", c_idx, raw1)
    (from_sib,) = _run_rider(_sibling_rider([g_in0]), "rs_sibling_last")
    p32, p16 = _add_my_half(g_in0, from_sib, c_idx, "rs_add_w_in_l0")
    parts[("w_in", "_l0")] = (p32, _run_rider(_chips_rider([p16]), "rs_chips_last")[0])
    chip_idx = jnp.reshape(chip, (1,)).astype(jnp.int32)
    mine = [_sum_layers([parts[(n, f"_l{l}")][0] for l in range(nl)], [parts[(n, f"_l{l}")][1] for l in range(nl)],
                        chip_idx, f"rs_sum_{n}") for n in BIG]
    theirs = _run_rider(_swap_rider(mine), "rs_swap")

    small_names = ("dmod", "rel_bias", "g_a", "g_b", "conv_w", "conv_b")
    pieces = [small[l][n] for l in range(nl) for n in small_names] + [dfinal_g]
    shapes = [p.shape for p in pieces]
    pack3 = _pack_rows(pieces)
    got3 = _allgather8(pack3, "gather_small").reshape(N_DEV, pack3.shape[0], LANES)
    summed = _unpack_rows(_sum_slabs(got3, "sum_small"), shapes)
    tot = [dict(zip(small_names, summed[len(small_names) * l:len(small_names) * (l + 1)])) for l in range(nl)]
    g_final_g = summed[-1].reshape(-1)
    g_b_ada = jnp.stack([tot[l]["dmod"].reshape(-1) for l in range(nl)])
    g_rel = jnp.stack([tot[l]["rel_bias"] for l in range(nl)])
    g_ga = jnp.stack([tot[l]["g_a"].reshape(-1) for l in range(nl)])
    g_gb = jnp.stack([tot[l]["g_b"].reshape(-1) for l in range(nl)])
    g_conv_b = jnp.stack([tot[l]["conv_b"].reshape(-1) for l in range(nl)])
    g_conv_w = jnp.stack([lax.dynamic_slice_in_dim(tot[l]["conv_w"], chip * nc, nc, axis=1) for l in range(nl)])
    per_dev = [_unpack_rows(got3[j], shapes) for j in range(N_DEV)]
    dmod_all = jnp.stack([jnp.stack([per_dev[j][len(small_names) * l].reshape(-1) for j in range(N_DEV)])
                          for l in range(nl)])
    g_w_ada = _ada_bwd(c_all, lax.dynamic_slice_in_dim(dmod_all, chip * n_ada, n_ada, axis=2), "ada_bwd")

    grads = dict(w_ada=g_w_ada, b_ada=g_b_ada, rel_bias=g_rel, g_a=g_ga, g_b=g_gb, conv_w=g_conv_w, conv_b=g_conv_b,
                 final_g=g_final_g)
    weights = dict(w_ada=w_ada, b_ada=b_ada, w_in=w_in, rel_bias=rel_bias, g_a=g_a, g_b=g_b, w_out=w_out, w_up=w_up,
                   conv_w=conv_w, conv_b=conv_b, w_down=w_down, final_g=final_g)
    m_in = dict(w_ada=m_w_ada, b_ada=m_b_ada, w_in=m_w_in, rel_bias=m_rel_bias, g_a=m_g_a, g_b=m_g_b, w_out=m_w_out,
                w_up=m_w_up, conv_w=m_conv_w, conv_b=m_conv_b, w_down=m_w_down, final_g=m_final_g)
    v_in = dict(w_ada=v_w_ada, b_ada=v_b_ada, w_in=v_w_in, rel_bias=v_rel_bias, g_a=v_g_a, g_b=v_g_b, w_out=v_w_out,
                w_up=v_w_up, conv_w=v_conv_w, conv_b=v_conv_b, w_down=v_w_down, final_g=v_final_g)
    order_w = ("w_ada", "b_ada", "w_in", "rel_bias", "g_a", "g_b", "w_out", "w_up", "conv_w", "conv_b", "w_down", "final_g")
    upd = {n: _adamw_nd(grads[n], weights[n], m_in[n], v_in[n], f"adamw_{n}") for n in grads}
    for n, mn, th in zip(BIG, mine, theirs):
        grads[n], *upd[n] = _adamw_halves(mn, th, c_idx, weights[n], m_in[n], v_in[n], f"adamw_{n}")
    return (loss, dx[None], *[grads[n] for n in order_w], *[upd[n][0] for n in order_w],
            *[upd[n][1] for n in order_w], *[upd[n][2] for n in order_w])
```

```python
import functools

import jax
import jax.numpy as jnp
from jax import lax
from jax.experimental import pallas as pl
from jax.experimental.pallas import tpu as pltpu

F32 = jnp.float32
BF16 = jnp.bfloat16
MESH = pl.DeviceIdType.MESH
ANY = pl.BlockSpec(memory_space=pl.ANY)
VMEM_FULL = pl.BlockSpec(memory_space=pltpu.VMEM)

HEAD_DIM = 64
N_HEADS = 8
W_GRP = N_HEADS * HEAD_DIM
CHUNK = 64
N_PREV = 8
BAND = (N_PREV + 1) * CHUNK
PAD = N_PREV * CHUNK
REL_CLIP = 128
N_REL = 2 * REL_CLIP + 1
EPS = 1e-6
SB_DEAD = -110.0
N_CHIPS = 4
N_DEV = 8
LANES = 128
V7X_VMEM_LIMIT = 56 * 1024 * 1024

ADAM_LR = 0.001
ADAM_B1 = 0.9
ADAM_B2 = 0.999
ADAM_EPS = 1e-08
ADAM_WD = 0.01
ADAM_STEP = 10


def _params(**kw):
    return pltpu.CompilerParams(vmem_limit_bytes=V7X_VMEM_LIMIT, **kw)


def _pick(dim, pref, mult=LANES):
    t = (min(pref, dim) // mult) * mult
    while t >= mult:
        if dim % t == 0:
            return t
        t -= mult
    return dim


def _my_place():
    return lax.axis_index("x"), lax.axis_index("y"), lax.axis_index("c")


def _flip(v, bit):
    return 1 - v if bit else v


def _matmul(a, b, *, form, out_dtype, tm, tn, tk, name, shard_cols=None, halves=False, rider=None):
    if form == "nn":
        (m, k), (_, n) = a.shape, b.shape
        a_map, a_blk = (lambda i, j, kk: (i, kk)), (tm, tk)
        b_map, b_blk = (lambda i, j, kk: (kk, j)), (tk, tn)
        dims = (((1,), (0,)), ((), ()))
    elif form == "nt":
        m, k = (a.shape[1], 2 * a.shape[2]) if halves else a.shape
        n = b.shape[0]
        if halves:
            per_half = k // 2 // tk
            a_map, a_blk = (lambda i, j, kk: (kk // per_half, i, kk % per_half)), (None, tm, tk)
        else:
            a_map, a_blk = (lambda i, j, kk: (i, kk)), (tm, tk)
        b_map, b_blk = (lambda i, j, kk: (j, kk)), (tn, tk)
        dims = (((1,), (1,)), ((), ()))
    else:
        k, m = a.shape
        n = 2 * b.shape[2] if halves else b.shape[1]
        a_map, a_blk = (lambda i, j, kk: (kk, i)), (tk, tm)
        if halves:
            per_half = n // 2 // tn
            b_map, b_blk = (lambda i, j, kk: (j // per_half, kk, j % per_half)), (None, tk, tn)
        else:
            b_map, b_blk = (lambda i, j, kk: (kk, j)), (tk, tn)
        dims = (((0,), (0,)), ((), ()))
    assert m % tm == 0 and n % tn == 0 and k % tk == 0, (name, m, n, k, tm, tn, tk)
    nk = k // tk
    if shard_cols is None:
        out_shape = jax.ShapeDtypeStruct((m, n), out_dtype)
        o_map, o_blk = (lambda i, j, kk: (i, j)), (tm, tn)
    else:
        per = shard_cols // tn
        assert shard_cols % tn == 0
        out_shape = jax.ShapeDtypeStruct((n // shard_cols, m, shard_cols), out_dtype)
        o_map, o_blk = (lambda i, j, kk: (j // per, i, j % per)), (None, tm, tn)
    a_bytes, b_bytes = a.size * a.dtype.itemsize, b.size * b.dtype.itemsize
    rows_outer = nk > 1 or (m // tm) * b_bytes + a_bytes <= (n // tn) * a_bytes + b_bytes
    grid = (m // tm, n // tn, nk) if rows_outer else (n // tn, m // tm, nk)
    order = (lambda f: f) if rows_outer else (lambda f: (lambda g0, g1, kk: f(g1, g0, kk)))

    def body(a_ref, b_ref, o_ref, *acc):
        part = lax.dot_general(a_ref[...], b_ref[...], dims, preferred_element_type=F32)
        if nk == 1:
            o_ref[...] = part.astype(out_dtype)
            return
        acc_ref, = acc
        kk = pl.program_id(2)

        @pl.when(kk == 0)
        def _():
            acc_ref[...] = part

        @pl.when(jnp.logical_and(kk > 0, kk < nk - 1))
        def _():
            acc_ref[...] += part

        @pl.when(kk == nk - 1)
        def _():
            o_ref[...] = (acc_ref[...] + part).astype(out_dtype)

    out = _call_with_rider(
        body, rider, name=name, out_shape=(out_shape,), grid=grid,
        in_specs=[pl.BlockSpec(a_blk, order(a_map)), pl.BlockSpec(b_blk, order(b_map))],
        out_specs=(pl.BlockSpec(o_blk, order(o_map)),),
        scratch_shapes=[pltpu.VMEM((tm, tn), F32)] if nk > 1 else [], args=(a, b))
    return out[0] if rider is None else out


def _row_spec(tr, d):
    return pl.BlockSpec((tr, d), lambda i: (i, 0))


def _vec_spec(d):
    return pl.BlockSpec((1, d), lambda i: (0, 0))


def _rms(xf):
    r = lax.rsqrt(jnp.mean(xf * xf, axis=-1, keepdims=True) + EPS)
    return xf * r, r


def _norm_mod(x, scale, shift, name):
    s, d = x.shape
    tr = _pick(s, 512, 8)

    def body(x_ref, sc_ref, sh_ref, o_ref):
        n, _ = _rms(x_ref[...])
        o_ref[...] = (n * (1.0 + sc_ref[...]) + sh_ref[...]).astype(BF16)

    return pl.pallas_call(
        body, name=name, out_shape=jax.ShapeDtypeStruct((s, d), BF16), grid=(s // tr,),
        in_specs=[_row_spec(tr, d), _vec_spec(d), _vec_spec(d)], out_specs=_row_spec(tr, d),
        compiler_params=_params(),
    )(x, scale, shift)


def _out_norm(oa, ob, g_a, g_b, name):
    s, w = oa.shape
    tr = _pick(s, 512, 8)

    def body(oa_ref, ob_ref, ga_ref, gb_ref, o_ref):
        na, _ = _rms(oa_ref[...])
        nb, _ = _rms(ob_ref[...])
        o_ref[:, :w] = (na * ga_ref[...]).astype(BF16)
        o_ref[:, w:] = (nb * gb_ref[...]).astype(BF16)

    return pl.pallas_call(
        body, name=name, out_shape=jax.ShapeDtypeStruct((s, 2 * w), BF16), grid=(s // tr,),
        in_specs=[_row_spec(tr, w), _row_spec(tr, w), _vec_spec(w), _vec_spec(w)],
        out_specs=_row_spec(tr, 2 * w), compiler_params=_params(),
    )(oa, ob, g_a, g_b)


def _residual(x, gate, m, name):
    s, d = x.shape
    tr = _pick(s, 512, 8)

    def body(x_ref, g_ref, m_ref, o_ref):
        o_ref[...] = x_ref[...] + g_ref[...] * m_ref[...]

    return pl.pallas_call(
        body, name=name, out_shape=jax.ShapeDtypeStruct((s, d), F32), grid=(s // tr,),
        in_specs=[_row_spec(tr, d), _vec_spec(d), _row_spec(tr, d)], out_specs=_row_spec(tr, d),
        compiler_params=_params(),
    )(x, gate, m)


def _shift_down(u, k):
    rows = lax.broadcasted_iota(jnp.int32, u.shape, 0)
    return jnp.where(rows >= k, pltpu.roll(u, k, 0), 0.0)


def _shift_up(u, k):
    s = u.shape[0]
    rows = lax.broadcasted_iota(jnp.int32, u.shape, 0)
    return jnp.where(rows < s - k, pltpu.roll(u, s - k, 0), 0.0)


def _conv(u, w_ref, b_ref):
    return w_ref[0:1, :] * _shift_down(u, 2) + w_ref[1:2, :] * _shift_down(u, 1) + w_ref[2:3, :] * u + b_ref[...]


STRIP = 256


def _conv_strip(u_ref, t, r0, w_ref, b_ref):
    cur = u_ref[pl.ds(r0, STRIP), :]
    prev = u_ref[pl.ds(pl.multiple_of(jnp.maximum(r0 - 8, 0), 8), 8), :]
    ext = jnp.concatenate([jnp.where(t > 0, prev, 0.0), cur], axis=0)
    u1 = pltpu.roll(ext, 1, 0)[8:]
    u2 = pltpu.roll(ext, 2, 0)[8:]
    return w_ref[0:1, :] * u2 + w_ref[1:2, :] * u1 + w_ref[2:3, :] * cur + b_ref[...], u2, u1, cur


def _fold8(v):
    return functools.reduce(jnp.add, [v[r:r + 8] for r in range(0, STRIP, 8)])


def _conv_glu(u, conv_w, conv_b, name):
    s, f2 = u.shape
    f = f2 // 2
    tc = LANES
    nb = f // tc

    def body(ug_ref, uv_ref, wg_ref, wv_ref, bg_ref, bv_ref, o_ref):
        def strip(t, _):
            r0 = pl.multiple_of(t * STRIP, STRIP)
            g = _conv_strip(ug_ref, t, r0, wg_ref, bg_ref)[0]
            v = _conv_strip(uv_ref, t, r0, wv_ref, bv_ref)[0]
            o_ref[pl.ds(r0, STRIP), :] = (g * jax.nn.sigmoid(g) * v).astype(BF16)
            return 0

        lax.fori_loop(0, s // STRIP, strip, 0)

    col = lambda off: pl.BlockSpec((s, tc), lambda j: (0, j + off))
    wcol = lambda off: pl.BlockSpec((3, tc), lambda j: (0, j + off))
    bcol = lambda off: pl.BlockSpec((1, tc), lambda j: (0, j + off))
    return pl.pallas_call(
        body, name=name, out_shape=jax.ShapeDtypeStruct((s, f), BF16), grid=(nb,),
        in_specs=[col(0), col(nb), wcol(0), wcol(nb), bcol(0), bcol(nb)], out_specs=col(0),
        compiler_params=_params(),
    )(u, u, conv_w, conv_w, conv_b, conv_b)


def _conv_glu_bwd(u, da, conv_w, conv_b, name, rider=None):
    s, f2 = u.shape
    f = f2 // 2
    tc = LANES
    nb = f // tc

    def body(ug_ref, uv_ref, da_ref, wg_ref, wv_ref, bg_ref, bv_ref, du_ref, dw_ref, db_ref, dyg_ref, dyv_ref):
        def strip1(t, acc):
            r0 = pl.multiple_of(t * STRIP, STRIP)
            g, *ug = _conv_strip(ug_ref, t, r0, wg_ref, bg_ref)
            v, *uv = _conv_strip(uv_ref, t, r0, wv_ref, bv_ref)
            da_ = da_ref[pl.ds(r0, STRIP), :]
            sg = jax.nn.sigmoid(g)
            dg = da_ * v * (sg * (1.0 + g * (1.0 - sg)))
            dv = da_ * (g * sg)
            dyg_ref[pl.ds(r0, STRIP), :] = dg
            dyv_ref[pl.ds(r0, STRIP), :] = dv
            new = [_fold8(dy * uu) for dy, us in ((dg, ug), (dv, uv)) for uu in us] + [_fold8(dg), _fold8(dv)]
            return tuple(a + n for a, n in zip(acc, new))

        zero = jnp.zeros((8, LANES), F32)
        acc = lax.fori_loop(0, s // STRIP, strip1, (zero,) * 8)
        for h in range(2):
            for tap in range(3):
                dw_ref[h, tap:tap + 1, :] = jnp.sum(acc[3 * h + tap], axis=0, keepdims=True)
            db_ref[h] = jnp.sum(acc[6 + h], axis=0, keepdims=True)
        dyg_ref[s:, :] = zero
        dyv_ref[s:, :] = zero

        def strip2(t, _):
            r0 = pl.multiple_of(t * STRIP, STRIP)
            for h, (dy_ref, w_ref) in enumerate(((dyg_ref, wg_ref), (dyv_ref, wv_ref))):
                cur = dy_ref[pl.ds(r0, STRIP), :]
                ext = jnp.concatenate([cur, dy_ref[pl.ds(r0 + STRIP, 8), :]], axis=0)
                d1 = pltpu.roll(ext, STRIP + 7, 0)[:STRIP]
                d2 = pltpu.roll(ext, STRIP + 6, 0)[:STRIP]
                du = w_ref[2:3, :] * cur + w_ref[1:2, :] * d1 + w_ref[0:1, :] * d2
                du_ref[h, pl.ds(r0, STRIP), :] = du.astype(BF16)
            return 0

        lax.fori_loop(0, s // STRIP, strip2, 0)

    col = lambda off: pl.BlockSpec((s, tc), lambda j: (0, j + off))
    wcol = lambda off: pl.BlockSpec((3, tc), lambda j: (0, j + off))
    bcol = lambda off: pl.BlockSpec((1, tc), lambda j: (0, j + off))
    return _call_with_rider(
        body, rider, name=name, grid=(nb,),
        out_shape=(jax.ShapeDtypeStruct((2, s, f), BF16), jax.ShapeDtypeStruct((2, 3, f), F32),
                   jax.ShapeDtypeStruct((2, 1, f), F32)),
        in_specs=[col(0), col(nb), col(0), wcol(0), wcol(nb), bcol(0), bcol(nb)],
        out_specs=(pl.BlockSpec((2, s, tc), lambda j: (0, 0, j)), pl.BlockSpec((2, 3, tc), lambda j: (0, 0, j)),
                   pl.BlockSpec((2, 1, tc), lambda j: (0, 0, j))),
        scratch_shapes=[pltpu.VMEM((s + 8, tc), F32), pltpu.VMEM((s + 8, tc), F32)],
        args=(u, u, da, conv_w, conv_w, conv_b, conv_b))


def _accumulate(ref, val):
    @pl.when(pl.program_id(0) == 0)
    def _():
        ref[...] = val

    @pl.when(pl.program_id(0) > 0)
    def _():
        ref[...] += val


def _rms_bwd(n, r, dn):
    return r * (dn - n * jnp.mean(dn * n, axis=-1, keepdims=True))


def _loss_head(x, final_g, target, name):
    s, d = x.shape
    tr = _pick(s, 512, 8)

    def body(x_ref, g_ref, t_ref, loss_ref, dx_ref, dg_ref):
        n, r = _rms(x_ref[...])
        diff = n * g_ref[...] - t_ref[...]
        part = 0.5 * jnp.sum(jnp.sum(diff * diff, axis=1, keepdims=True), axis=0, keepdims=True) / d
        _accumulate(loss_ref, part)
        dy = diff / d
        _accumulate(dg_ref, jnp.sum(dy * n, axis=0, keepdims=True))
        dx_ref[...] = _rms_bwd(n, r, dy * g_ref[...])

    return pl.pallas_call(
        body, name=name, grid=(s // tr,),
        out_shape=(jax.ShapeDtypeStruct((1, 1), F32), jax.ShapeDtypeStruct((s, d), F32), jax.ShapeDtypeStruct((1, d), F32)),
        in_specs=[_row_spec(tr, d), _vec_spec(d), _row_spec(tr, d)],
        out_specs=(pl.BlockSpec((1, 1), lambda i: (0, 0)), _row_spec(tr, d), _vec_spec(d)),
        compiler_params=_params(),
    )(x, final_g, target)


def _gate_bwd(dx, m, gate, name):
    s, d = dx.shape
    tr = _pick(s, 512, 8)

    def body(dx_ref, m_ref, g_ref, dm_ref, dg_ref):
        dxv = dx_ref[...]
        dm_ref[...] = (dxv * g_ref[...]).astype(BF16)
        _accumulate(dg_ref, jnp.sum(dxv * m_ref[...], axis=0, keepdims=True))

    return pl.pallas_call(
        body, name=name, grid=(s // tr,),
        out_shape=(jax.ShapeDtypeStruct((s, d), BF16), jax.ShapeDtypeStruct((1, d), F32)),
        in_specs=[_row_spec(tr, d), _row_spec(tr, d), _vec_spec(d)], out_specs=(_row_spec(tr, d), _vec_spec(d)),
        compiler_params=_params(),
    )(dx, m, gate)


def _norm_mod_bwd(x, dh, dres, scale, name):
    s, d = x.shape
    tr = _pick(s, 512, 8)

    def body(x_ref, dh_ref, dr_ref, sc_ref, dx_ref, dsc_ref, dsh_ref):
        n, r = _rms(x_ref[...])
        dh_ = dh_ref[...]
        _accumulate(dsc_ref, jnp.sum(dh_ * n, axis=0, keepdims=True))
        _accumulate(dsh_ref, jnp.sum(dh_, axis=0, keepdims=True))
        dx_ref[...] = dr_ref[...] + _rms_bwd(n, r, dh_ * (1.0 + sc_ref[...]))

    return pl.pallas_call(
        body, name=name, grid=(s // tr,),
        out_shape=(jax.ShapeDtypeStruct((s, d), F32), jax.ShapeDtypeStruct((1, d), F32), jax.ShapeDtypeStruct((1, d), F32)),
        in_specs=[_row_spec(tr, d), _row_spec(tr, d), _row_spec(tr, d), _vec_spec(d)],
        out_specs=(_row_spec(tr, d), _vec_spec(d), _vec_spec(d)), compiler_params=_params(),
    )(x, dh, dres, scale)


def _out_norm_bwd(oa, ob, dcat, g_a, g_b, name):
    s, w = oa.shape
    tr = _pick(s, 512, 8)

    def body(oa_ref, ob_ref, dc_ref, ga_ref, gb_ref, doa_ref, dob_ref, dga_ref, dgb_ref):
        for o_ref, g_ref, do_ref, dg_ref, lo in ((oa_ref, ga_ref, doa_ref, dga_ref, 0), (ob_ref, gb_ref, dob_ref, dgb_ref, w)):
            n, r = _rms(o_ref[...])
            dc = dc_ref[:, lo:lo + w]
            _accumulate(dg_ref, jnp.sum(dc * n, axis=0, keepdims=True))
            do_ref[...] = _rms_bwd(n, r, dc * g_ref[...])

    return pl.pallas_call(
        body, name=name, grid=(s // tr,),
        out_shape=(jax.ShapeDtypeStruct((s, w), F32), jax.ShapeDtypeStruct((s, w), F32),
                   jax.ShapeDtypeStruct((1, w), F32), jax.ShapeDtypeStruct((1, w), F32)),
        in_specs=[_row_spec(tr, w), _row_spec(tr, w), _row_spec(tr, 2 * w), _vec_spec(w), _vec_spec(w)],
        out_specs=(_row_spec(tr, w), _row_spec(tr, w), _vec_spec(w), _vec_spec(w)), compiler_params=_params(),
    )(oa, ob, dcat, g_a, g_b)


def _head_masks():
    lane = lax.broadcasted_iota(jnp.int32, (1, LANES), 1)
    return lane < HEAD_DIM, lane >= HEAD_DIM


def _nt(a, b):
    return lax.dot_general(a, b, (((1,), (1,)), ((), ())), preferred_element_type=F32)


def _tn(a, b):
    return lax.dot_general(a, b, (((0,), (0,)), ((), ())), preferred_element_type=F32)


def _nn(a, b):
    return jnp.dot(a, b, preferred_element_type=F32)


def _only(mask, v):
    return jnp.where(mask, v, jnp.zeros_like(v))


def _fill_padded(dst_ref, src_ref):
    dst_ref[0:PAD, :] = jnp.zeros((PAD, LANES), dst_ref.dtype)
    dst_ref[PAD:, :] = src_ref[...]


def _chunk_probs(s, bias, chunk):
    pos = lax.broadcasted_iota(jnp.int32, (1, BAND), 1)
    s = jnp.where(pos >= (N_PREV - chunk) * CHUNK, s + bias, -1e30)
    e = jnp.exp(s - jnp.max(s, axis=1, keepdims=True))
    return e / jnp.sum(e, axis=1, keepdims=True)


def _band_windows(i, cq, kpad, vpad):
    chunks = [i * cq + cc for cc in range(cq)]
    starts = [pl.multiple_of(ch * CHUNK, CHUNK) for ch in chunks]
    return chunks, starts, [kpad[pl.ds(st, BAND), :] for st in starts], [vpad[pl.ds(st, BAND), :] for st in starts]


def _attn_a_fwd(proj, band_bias, name):
    s = proj.shape[0]
    cq = 4
    tq = cq * CHUNK
    npair = N_HEADS // 2
    kcol, vcol = W_GRP // LANES, 2 * W_GRP // LANES

    def body(q_ref, k_ref, v_ref, b_ref, o_ref, kpad, vpad):
        i = pl.program_id(1)
        masks = _head_masks()

        @pl.when(i == 0)
        def _():
            _fill_padded(kpad, k_ref)
            _fill_padded(vpad, v_ref)

        chunks, _, kbs, vbs = _band_windows(i, cq, kpad, vpad)
        q2 = q_ref[...] * (HEAD_DIM ** -0.5)
        units = [(cc, h) for cc in range(cq) for h in range(2)]
        ss = [_nt(_only(masks[h], q2[cc * CHUNK:(cc + 1) * CHUNK]), kbs[cc]) for cc, h in units]
        ps = [_chunk_probs(s_, b_ref[h], chunks[cc]).astype(BF16) for s_, (cc, h) in zip(ss, units)]
        for cc in range(cq):
            o_ref[cc * CHUNK:(cc + 1) * CHUNK, :] = (_nn(ps[2 * cc], _only(masks[0], vbs[cc]))
                                                     + _nn(ps[2 * cc + 1], _only(masks[1], vbs[cc])))

    return pl.pallas_call(
        body, name=name, out_shape=jax.ShapeDtypeStruct((s, W_GRP), F32), grid=(npair, s // tq),
        in_specs=[pl.BlockSpec((tq, LANES), lambda p, i: (i, p)),
                  pl.BlockSpec((s, LANES), lambda p, i: (0, kcol + p)),
                  pl.BlockSpec((s, LANES), lambda p, i: (0, vcol + p)),
                  pl.BlockSpec((2, CHUNK, BAND), lambda p, i: (p, 0, 0))],
        out_specs=pl.BlockSpec((tq, LANES), lambda p, i: (i, p)),
        scratch_shapes=[pltpu.VMEM((s + PAD, LANES), BF16), pltpu.VMEM((s + PAD, LANES), BF16)],
        compiler_params=_params(),
    )(proj, proj, proj, band_bias)


def _attn_a_bwd(proj, band_bias, doa, name, rider=None):
    s = proj.shape[0]
    cq = 4
    tq = cq * CHUNK
    nq = s // tq
    npair = N_HEADS // 2
    kcol, vcol = W_GRP // LANES, 2 * W_GRP // LANES
    scale = HEAD_DIM ** -0.5

    def body(q_ref, k_ref, v_ref, b_ref, do_ref, dq_ref, dk_ref, dv_ref, db_ref, kpad, vpad, dkpad, dvpad):
        i = pl.program_id(1)
        masks = _head_masks()

        @pl.when(i == 0)
        def _():
            _fill_padded(kpad, k_ref)
            _fill_padded(vpad, v_ref)
            dkpad[...] = jnp.zeros_like(dkpad)
            dvpad[...] = jnp.zeros_like(dvpad)
            db_ref[...] = jnp.zeros_like(db_ref)

        chunks, starts, kbs, vbs = _band_windows(i, cq, kpad, vpad)
        q2 = q_ref[...] * scale
        do2 = do_ref[...].astype(BF16)
        units = [(cc, h) for cc in range(cq) for h in range(2)]
        qhs = [_only(masks[h], q2[cc * CHUNK:(cc + 1) * CHUNK]) for cc, h in units]
        dohs = [_only(masks[h], do2[cc * CHUNK:(cc + 1) * CHUNK]) for cc, h in units]
        ss = [_nt(qh, kbs[cc]) for qh, (cc, h) in zip(qhs, units)]
        dps = [_nt(doh, vbs[cc]) for doh, (cc, h) in zip(dohs, units)]
        ps = [_chunk_probs(s_, b_ref[h], chunks[cc]) for s_, (cc, h) in zip(ss, units)]
        dss = [p * (dp - jnp.sum(p * dp, axis=1, keepdims=True)) for p, dp in zip(ps, dps)]
        for h in range(2):
            db_ref[h] += functools.reduce(jnp.add, [dss[2 * cc + h] for cc in range(cq)])
        for cc in range(cq):
            u0, u1 = 2 * cc, 2 * cc + 1
            dsb = [dss[u0].astype(BF16), dss[u1].astype(BF16)]
            dq = _nn(dsb[0], _only(masks[0], kbs[cc])) + _nn(dsb[1], _only(masks[1], kbs[cc]))
            dq_ref[cc * CHUNK:(cc + 1) * CHUNK, :] = dq * scale
            dkpad[pl.ds(starts[cc], BAND), :] += _tn(jnp.concatenate(dsb, axis=0), jnp.concatenate([qhs[u0], qhs[u1]], axis=0))
            dvpad[pl.ds(starts[cc], BAND), :] += _tn(jnp.concatenate([ps[u0].astype(BF16), ps[u1].astype(BF16)], axis=0),
                                                     jnp.concatenate([dohs[u0], dohs[u1]], axis=0))

        @pl.when(i == nq - 1)
        def _():
            dk_ref[...] = dkpad[PAD:, :]
            dv_ref[...] = dvpad[PAD:, :]

    blk = pl.BlockSpec((tq, LANES), lambda p, i: (i, p))
    whole = pl.BlockSpec((s, LANES), lambda p, i: (0, p))
    bias_spec = pl.BlockSpec((2, CHUNK, BAND), lambda p, i: (p, 0, 0))
    return _call_with_rider(
        body, rider, name=name, grid=(npair, nq),
        out_shape=(jax.ShapeDtypeStruct((s, W_GRP), F32),) * 3 + (jax.ShapeDtypeStruct((N_HEADS, CHUNK, BAND), F32),),
        in_specs=[blk, pl.BlockSpec((s, LANES), lambda p, i: (0, kcol + p)),
                  pl.BlockSpec((s, LANES), lambda p, i: (0, vcol + p)), bias_spec, blk],
        out_specs=(blk, whole, whole, bias_spec),
        scratch_shapes=[pltpu.VMEM((s + PAD, LANES), BF16), pltpu.VMEM((s + PAD, LANES), BF16),
                        pltpu.VMEM((s + PAD, LANES), F32), pltpu.VMEM((s + PAD, LANES), F32)],
        args=(proj, proj, proj, band_bias, doa))


def _split3(v):
    hi = v.astype(BF16)
    r1 = v - hi.astype(F32)
    mid = r1.astype(BF16)
    lo = (r1 - mid.astype(F32)).astype(BF16)
    return hi, mid, lo


def _rel_bias_grad(dband_t, name):
    width = 3 * LANES

    def body(t_ref, o_ref):
        pos = lax.broadcasted_iota(jnp.int32, (BAND, width), 0)
        col = lax.broadcasted_iota(jnp.int32, (BAND, width), 1)
        acc = jnp.zeros((N_HEADS, width), F32)
        for q in range(CHUNK):
            idx = jnp.minimum(PAD + q - pos, REL_CLIP) + REL_CLIP
            onehot = (col == idx).astype(BF16)
            for part in _split3(t_ref[q]):
                acc = acc + _nn(part, onehot)
        o_ref[...] = acc

    return pl.pallas_call(
        body, name=name, out_shape=jax.ShapeDtypeStruct((N_HEADS, width), F32),
        in_specs=[VMEM_FULL], out_specs=VMEM_FULL, compiler_params=_params(),
    )(dband_t)


def _split2_wide(v):
    hi = v.astype(BF16)
    return jnp.concatenate([hi, (v - hi.astype(F32)).astype(BF16)], axis=1)


def _sb_logs(z, lower):
    e = jnp.exp(-jnp.abs(z))
    lb = jnp.minimum(z, 0.0) - jnp.log(1.0 + e)
    lk = lb - z
    if lower is not None:
        lk = jnp.where(lower, lk, 0.0)
    return z, e, lb, lk


def _tri_masks(tq):
    row = lax.broadcasted_iota(jnp.int32, (tq, tq), 0)
    col = lax.broadcasted_iota(jnp.int32, (tq, tq), 1)
    return row, col


def _stack2(m):
    return jnp.concatenate([m, m], axis=0).astype(BF16)


def _sb_fwd(proj, name, rider=None):
    s = proj.shape[0]
    tq = _pick(s, 256)
    nq = s // tq
    npair = N_HEADS // 2
    qcol, kcol, vcol = 3 * W_GRP // LANES, 4 * W_GRP // LANES, 5 * W_GRP // LANES

    assert nq % 2 == 0

    def body(q_ref, k_ref, v_ref, o_ref, l_ref):
        i = pl.program_id(1)
        masks = _head_masks()
        q2 = q_ref[...] * (HEAD_DIM ** -0.5)
        qs = [[_only(m, q2[c * tq:(c + 1) * tq]) for m in masks] for c in range(2)]
        row, col = _tri_masks(tq)
        lower = row > col
        after2 = _stack2(lower)

        def tile(kblock, chains, carry):
            accs, tails = [list(t) for t in carry[0]], [list(t) for t in carry[1]]
            ks = pl.multiple_of(kblock * tq, tq)
            kb = k_ref[pl.ds(ks, tq), :]
            vb = v_ref[pl.ds(ks, tq), :]
            units = [(c, h, diag) for c, diag in chains for h in range(2)]
            zs = [_nt(qs[c][h], kb) for c, h, _ in units]
            vh = [_only(masks[h], vb) for h in range(2)]
            lbs, lks, locs = [], [], []
            for z, (c, h, diag) in zip(zs, units):
                lb, lk = _sb_logs(z, lower if diag else None)[2:]
                lbs.append(lb)
                lks.append(lk)
                locs.append(_nn(_split2_wide(lk), after2))
            for lb, lk, loc, (c, h, diag) in zip(lbs, lks, locs, units):
                a = jnp.exp(lb + (loc + tails[c][h]))
                if diag:
                    a = jnp.where(lower, a, 0.0)
                accs[c][0] = accs[c][0] + _nn(a.astype(BF16), vh[h])
                tails[c][h] = tails[c][h] + (loc[:, 0:1] + lk[:, 0:1])
            return tuple(tuple(t) for t in accs), tuple(tuple(t) for t in tails)

        zero = jnp.zeros((tq, 1), F32)
        acc0 = jnp.zeros((tq, LANES), F32)
        carry = (((acc0,), (acc0,)), ((zero, zero), (zero, zero)))
        carry = tile(2 * i + 1, [(1, True)], carry)
        carry = tile(2 * i, [(0, True), (1, False)], carry)

        def alive(tails):
            return functools.reduce(jnp.maximum, [jnp.max(t) for ts in tails for t in ts]) > SB_DEAD

        def walk(state):
            jj, _, cr = state
            cr = tile(2 * i - jj, [(0, False), (1, False)], cr)
            return jj + 1, alive(cr[1]), cr

        jj, _, (accs, tails) = lax.while_loop(lambda st: jnp.logical_and(st[0] <= 2 * i, st[1]), walk,
                                              (jnp.int32(1), i >= 0, carry))
        for c in range(2):
            o_ref[c * tq:(c + 1) * tq, :] = accs[c][0]
            l_ref[c * tq:(c + 1) * tq, 0:1] = tails[c][0]
            l_ref[c * tq:(c + 1) * tq, 1:2] = tails[c][1]
        l_ref[:, 2:3] = jnp.full((2 * tq, 1), (jj - 1).astype(F32))

    return _call_with_rider(
        body, rider, name=name, grid=(npair, nq // 2),
        out_shape=(jax.ShapeDtypeStruct((s, W_GRP), F32), jax.ShapeDtypeStruct((npair, s, 3), F32)),
        in_specs=[pl.BlockSpec((2 * tq, LANES), lambda p, i: (i, qcol + p)),
                  pl.BlockSpec((s, LANES), lambda p, i: (0, kcol + p)),
                  pl.BlockSpec((s, LANES), lambda p, i: (0, vcol + p))],
        out_specs=(pl.BlockSpec((2 * tq, LANES), lambda p, i: (i, p)),
                   pl.BlockSpec((None, 2 * tq, 3), lambda p, i: (p, i, 0))),
        scratch_shapes=[], args=(proj, proj, proj))


def _sb_bwd(proj, ltot, dob, name, rider=None):
    s = proj.shape[0]
    tq = _pick(s, 256)
    nq = s // tq
    npair = N_HEADS // 2
    qcol, kcol, vcol = 3 * W_GRP // LANES, 4 * W_GRP // LANES, 5 * W_GRP // LANES
    scale = HEAD_DIM ** -0.5

    def body(q_ref, k_ref, v_ref, l_ref, do_ref, dq_ref, dk_ref, dv_ref):
        i = pl.program_id(1)
        masks = _head_masks()

        @pl.when(i == 0)
        def _():
            dk_ref[...] = jnp.zeros_like(dk_ref)
            dv_ref[...] = jnp.zeros_like(dv_ref)

        q2 = q_ref[...] * scale
        do2 = do_ref[...]
        part = lambda v, c: v[c * tq:(c + 1) * tq]
        qs = [[_only(m, part(q2, c)) for m in masks] for c in range(2)]
        doh = [[_only(m, part(do2, c)).astype(BF16) for m in masks] for c in range(2)]
        ltots = [[l_ref[c * tq:(c + 1) * tq, h:h + 1] for h in range(2)] for c in range(2)]
        row, col = _tri_masks(tq)
        lower = row > col
        upto2 = _stack2(row <= col)
        before = (row < col).astype(BF16)

        def tile(kblock, chains, carry):
            dqs, heads, gsums = [[list(t) for t in part_] for part_ in carry]
            ks = pl.multiple_of(kblock * tq, tq)
            kb = k_ref[pl.ds(ks, tq), :]
            vb = v_ref[pl.ds(ks, tq), :]
            units = [(c, h, diag) for c, diag in chains for h in range(2)]
            zs = [_nt(qs[c][h], kb) for c, h, _ in units]
            das = [_nt(doh[c][h], vb) for c, h, _ in units]
            kh = [_only(masks[h], kb) for h in range(2)]
            sigs, lbs, locs = [], [], []
            for z_, (c, h, diag) in zip(zs, units):
                z, e, lb, lk = _sb_logs(z_, lower if diag else None)
                locs.append(_nn(_split2_wide(lk), upto2))
                r = 1.0 / (1.0 + e)
                sigs.append(jnp.where(z >= 0, r, e * r))
                lbs.append(lb)
            a_s, gs, glocs = [], [], []
            for lb, loc, da, (c, h, diag) in zip(lbs, locs, das, units):
                a = jnp.exp(lb + (ltots[c][h] - (heads[c][h] + loc)))
                if diag:
                    a = jnp.where(lower, a, 0.0)
                g = a * da
                glocs.append(_nn(g.astype(BF16), before))
                a_s.append(a.astype(BF16))
                gs.append(g)
            dzbs = []
            for g, sig, loc, gloc, (c, h, diag) in zip(gs, sigs, locs, glocs, units):
                dz = g - sig * (g + (gsums[c][h] + gloc))
                if diag:
                    dz = jnp.where(lower, dz, 0.0)
                dzb = dz.astype(BF16)
                dzbs.append(dzb)
                dqs[c][0] = dqs[c][0] + _nn(dzb, kh[h])
                heads[c][h] = heads[c][h] + loc[:, tq - 1:tq]
                gsums[c][h] = gsums[c][h] + (gloc[:, tq - 1:tq] + g[:, tq - 1:tq])
            stack = lambda vs: vs[0] if len(vs) == 1 else jnp.concatenate(vs, axis=0)
            dk_ref[pl.ds(ks, tq), :] += _tn(stack(dzbs), stack([qs[c][h] for c, h, _ in units]))
            dv_ref[pl.ds(ks, tq), :] += _tn(stack(a_s), stack([doh[c][h] for c, h, _ in units]))
            return tuple(tuple(tuple(t) for t in part_) for part_ in (dqs, heads, gsums))

        zero = jnp.zeros((tq, 1), F32)
        acc0 = jnp.zeros((tq, LANES), F32)
        carry = (((acc0,), (acc0,)), ((zero, zero), (zero, zero)), ((zero, zero), (zero, zero)))
        walked = jnp.clip(jnp.max(l_ref[0:8, 2:3]).astype(jnp.int32), 0, 2 * i)
        carry = lax.fori_loop(2 * i - walked, 2 * i, lambda j, cr: tile(j, [(0, False), (1, False)], cr), carry)
        carry = tile(2 * i, [(0, True), (1, False)], carry)
        dqs, _, _ = tile(2 * i + 1, [(1, True)], carry)
        for c in range(2):
            dq_ref[c * tq:(c + 1) * tq, :] = dqs[c][0] * scale

    blk = pl.BlockSpec((2 * tq, LANES), lambda p, i: (i, p))
    whole = pl.BlockSpec((s, LANES), lambda p, i: (0, p))
    return _call_with_rider(
        body, rider, name=name, grid=(npair, nq // 2), out_shape=(jax.ShapeDtypeStruct((s, W_GRP), F32),) * 3,
        in_specs=[pl.BlockSpec((2 * tq, LANES), lambda p, i: (i, qcol + p)),
                  pl.BlockSpec((s, LANES), lambda p, i: (0, kcol + p)),
                  pl.BlockSpec((s, LANES), lambda p, i: (0, vcol + p)),
                  pl.BlockSpec((None, 2 * tq, 3), lambda p, i: (p, i, 0)), blk],
        out_specs=(blk, whole, whole), scratch_shapes=[], args=(proj, proj, proj, ltot, dob))


def _ada_fwd(c_all, w_ada, b_ada, name):
    nl, d, n = w_ada.shape
    tn = _pick(n, 512)

    def body(c_ref, w_ref, b_ref, o_ref):
        cv = c_ref[...]
        act = (cv * jax.nn.sigmoid(cv)).astype(BF16)
        o_ref[...] = _nn(act, w_ref[...].astype(BF16)) + b_ref[...]

    return pl.pallas_call(
        body, name=name, out_shape=jax.ShapeDtypeStruct((nl, N_DEV, n), F32), grid=(nl, n // tn),
        in_specs=[pl.BlockSpec((N_DEV, d), lambda l, j: (0, 0)), pl.BlockSpec((None, d, tn), lambda l, j: (l, 0, j)),
                  pl.BlockSpec((None, 1, tn), lambda l, j: (l, 0, j))],
        out_specs=pl.BlockSpec((None, N_DEV, tn), lambda l, j: (l, 0, j)), compiler_params=_params(),
    )(c_all, w_ada, b_ada)


def _ada_bwd(c_all, dmod, name):
    nl, _, n = dmod.shape
    d = c_all.shape[1]
    tn = _pick(n, 512)

    def body(c_ref, g_ref, o_ref):
        cv = c_ref[...]
        act = (cv * jax.nn.sigmoid(cv)).astype(BF16)
        o_ref[...] = _tn(act, g_ref[...].astype(BF16))

    return pl.pallas_call(
        body, name=name, out_shape=jax.ShapeDtypeStruct((nl, d, n), F32), grid=(nl, n // tn),
        in_specs=[pl.BlockSpec((N_DEV, d), lambda l, j: (0, 0)), pl.BlockSpec((None, N_DEV, tn), lambda l, j: (l, 0, j))],
        out_specs=pl.BlockSpec((None, d, tn), lambda l, j: (l, 0, j)), compiler_params=_params(),
    )(c_all, dmod)


def _adamw(g, w, m, v, name):
    r, c = g.shape
    tr = _pick(r, 512, 8)
    c1 = 1.0 - ADAM_B1 ** ADAM_STEP
    c2 = 1.0 - ADAM_B2 ** ADAM_STEP

    def body(g_ref, w_ref, m_ref, v_ref, d_ref, nm_ref, nv_ref):
        gv = g_ref[...]
        nm = ADAM_B1 * m_ref[...] + (1.0 - ADAM_B1) * gv
        nv = ADAM_B2 * v_ref[...] + (1.0 - ADAM_B2) * (gv * gv)
        d_ref[...] = -ADAM_LR * ((nm / c1) / (jnp.sqrt(nv / c2) + ADAM_EPS) + ADAM_WD * w_ref[...])
        nm_ref[...] = nm
        nv_ref[...] = nv

    spec = pl.BlockSpec((tr, c), lambda i: (i, 0))
    return pl.pallas_call(
        body, name=name, out_shape=(jax.ShapeDtypeStruct((r, c), F32),) * 3, grid=(r // tr,),
        in_specs=[spec] * 4, out_specs=(spec,) * 3, compiler_params=_params(),
    )(g, w, m, v)


def _adamw_nd(g, w, m, v, name):
    shape = w.shape
    two_d = (1, shape[0]) if len(shape) == 1 else (-1, shape[-1])
    outs = _adamw(*(t.reshape(two_d) for t in (g, w, m, v)), name=name)
    return tuple(o.reshape(shape) for o in outs)


def _allgather8(v, name):
    m, n = v.shape

    def body(v_ref, out_ref, send_sems, recv_sems, local_sem):
        x, y, c = _my_place()

        def rows(px, py, pc):
            return out_ref.at[pl.ds(pl.multiple_of((4 * px + 2 * py + pc) * m, 8), m), :]

        def peer(k):
            return _flip(x, k & 4), _flip(y, k & 2), _flip(c, k & 1)

        def copy(k, block):
            return pltpu.make_async_remote_copy(
                src_ref=v_ref, dst_ref=rows(*block), send_sem=send_sems.at[k - 1], recv_sem=recv_sems.at[k - 1],
                device_id=peer(k), device_id_type=MESH)

        mine = pltpu.make_async_copy(v_ref, rows(x, y, c), local_sem)
        mine.start()
        sends = [copy(k, (x, y, c)) for k in range(1, N_DEV)]
        for cp in sends:
            cp.start()
        for k in range(1, N_DEV):
            copy(k, peer(k)).wait_recv()
        for cp in sends:
            cp.wait_send()
        mine.wait()

    return pl.pallas_call(
        body, name=name, out_shape=jax.ShapeDtypeStruct((N_DEV * m, n), v.dtype),
        in_specs=[VMEM_FULL], out_specs=VMEM_FULL,
        scratch_shapes=[pltpu.SemaphoreType.DMA((N_DEV - 1,)), pltpu.SemaphoreType.DMA((N_DEV - 1,)),
                        pltpu.SemaphoreType.DMA],
        compiler_params=_params(),
    )(v)


def _chip_peers(x, y, c):
    out = []
    for k in range(1, N_CHIPS):
        px, py = _flip(x, k & 2), _flip(y, k & 1)
        out.append((2 * px + py, (px, py, c)))
    return out


def _gather_weights(shards, kinds, name):
    nw = len(shards)

    def full_shape(a, kind):
        l, r, n = a.shape
        return (l, r, N_CHIPS * n) if kind == "col" else (l, N_CHIPS * r, n)

    def body(*refs):
        ins, outs = refs[:nw], refs[nw:2 * nw]
        send_sems, recv_sems, local_sems = refs[2 * nw:]
        x, y, c = _my_place()
        chip = 2 * x + y

        def window(w, j):
            _, r, n = shards[w].shape
            if kinds[w] == "col":
                return outs[w].at[:, :, pl.ds(pl.multiple_of(j * n, LANES), n)]
            return outs[w].at[:, pl.ds(pl.multiple_of(j * r, 16), r), :]

        def copy(w, k, j, peer):
            return pltpu.make_async_remote_copy(
                src_ref=ins[w], dst_ref=window(w, j), send_sem=send_sems.at[3 * w + k], recv_sem=recv_sems.at[3 * w + k],
                device_id=peer, device_id_type=MESH)

        local = [pltpu.make_async_copy(ins[w], window(w, chip), local_sems.at[w]) for w in range(nw)]
        for cp in local:
            cp.start()
        peers = _chip_peers(x, y, c)
        sends = [copy(w, k, chip, peer) for w in range(nw) for k, (_, peer) in enumerate(peers)]
        for cp in sends:
            cp.start()
        for w in range(nw):
            for k, (pchip, peer) in enumerate(peers):
                copy(w, k, pchip, peer).wait_recv()
        for cp in sends:
            cp.wait_send()
        for cp in local:
            cp.wait()

    return pl.pallas_call(
        body, name=name,
        out_shape=tuple(jax.ShapeDtypeStruct(full_shape(a, kd), a.dtype) for a, kd in zip(shards, kinds)),
        in_specs=[ANY] * nw, out_specs=(ANY,) * nw,
        scratch_shapes=[pltpu.SemaphoreType.DMA((3 * nw,)), pltpu.SemaphoreType.DMA((3 * nw,)),
                        pltpu.SemaphoreType.DMA((nw,))],
        compiler_params=_params(),
    )(*shards)


def _rs_to_sibling(grads, name):
    nw = len(grads)

    def body(*refs):
        ins, outs = refs[:nw], refs[nw:2 * nw]
        send_sems, recv_sems = refs[2 * nw:]
        x, y, c = _my_place()
        sibling = (x, y, 1 - c)
        copies = [pltpu.make_async_remote_copy(
            src_ref=ins[w].at[j, 1 - c], dst_ref=outs[w].at[j], send_sem=send_sems.at[N_CHIPS * w + j],
            recv_sem=recv_sems.at[N_CHIPS * w + j], device_id=sibling, device_id_type=MESH)
            for w in range(nw) for j in range(N_CHIPS)]
        for cp in copies:
            cp.start()
        for cp in copies:
            cp.wait_recv()
        for cp in copies:
            cp.wait_send()

    return pl.pallas_call(
        body, name=name,
        out_shape=tuple(jax.ShapeDtypeStruct((N_CHIPS,) + g.shape[2:], g.dtype) for g in grads),
        in_specs=[ANY] * nw, out_specs=(ANY,) * nw,
        scratch_shapes=[pltpu.SemaphoreType.DMA((N_CHIPS * nw,)), pltpu.SemaphoreType.DMA((N_CHIPS * nw,))],
        compiler_params=_params(),
    )(*grads)


def _rs_to_chips(parts, name):
    nw = len(parts)

    def body(*refs):
        ins, outs = refs[:nw], refs[nw:2 * nw]
        send_sems, recv_sems, local_sems = refs[2 * nw:]
        x, y, c = _my_place()
        chip = 2 * x + y
        peers = _chip_peers(x, y, c)

        def copy(w, k, src_slab, dst_slab, peer):
            return pltpu.make_async_remote_copy(
                src_ref=ins[w].at[src_slab], dst_ref=outs[w].at[dst_slab], send_sem=send_sems.at[3 * w + k],
                recv_sem=recv_sems.at[3 * w + k], device_id=peer, device_id_type=MESH)

        local = [pltpu.make_async_copy(ins[w].at[chip], outs[w].at[chip], local_sems.at[w]) for w in range(nw)]
        for cp in local:
            cp.start()
        sends = [copy(w, k, pchip, chip, peer) for w in range(nw) for k, (pchip, peer) in enumerate(peers)]
        for cp in sends:
            cp.start()
        for w in range(nw):
            for k, (pchip, peer) in enumerate(peers):
                copy(w, k, chip, pchip, peer).wait_recv()
        for cp in sends:
            cp.wait_send()
        for cp in local:
            cp.wait()

    return pl.pallas_call(
        body, name=name, out_shape=tuple(jax.ShapeDtypeStruct(p.shape, p.dtype) for p in parts),
        in_specs=[ANY] * nw, out_specs=(ANY,) * nw,
        scratch_shapes=[pltpu.SemaphoreType.DMA((3 * nw,)), pltpu.SemaphoreType.DMA((3 * nw,)),
                        pltpu.SemaphoreType.DMA((nw,))],
        compiler_params=_params(),
    )(*parts)


def _rs_share_halves(halves, name):
    nw = len(halves)
    nl = len(halves[0])
    flat = [h for hs in halves for h in hs]

    def body(*refs):
        ins, outs = refs[:nw * nl], refs[nw * nl:nw * nl + nw]
        send_sems, recv_sems, local_sems = refs[nw * nl + nw:]
        x, y, c = _my_place()
        sibling = (x, y, 1 - c)
        local, sends, recvs = [], [], []
        for w in range(nw):
            for l in range(nl):
                n = nl * w + l
                local.append(pltpu.make_async_copy(ins[n], outs[w].at[l, c], local_sems.at[n]))
                sends.append(pltpu.make_async_remote_copy(
                    src_ref=ins[n], dst_ref=outs[w].at[l, c], send_sem=send_sems.at[n], recv_sem=recv_sems.at[n],
                    device_id=sibling, device_id_type=MESH))
                recvs.append(pltpu.make_async_remote_copy(
                    src_ref=ins[n], dst_ref=outs[w].at[l, 1 - c], send_sem=send_sems.at[n], recv_sem=recv_sems.at[n],
                    device_id=sibling, device_id_type=MESH))
        for cp in local + sends:
            cp.start()
        for cp in recvs:
            cp.wait_recv()
        for cp in sends:
            cp.wait_send()
        for cp in local:
            cp.wait()

    return pl.pallas_call(
        body, name=name,
        out_shape=tuple(jax.ShapeDtypeStruct((nl, 2) + hs[0].shape, hs[0].dtype) for hs in halves),
        in_specs=[ANY] * (nw * nl), out_specs=(ANY,) * nw,
        scratch_shapes=[pltpu.SemaphoreType.DMA((nw * nl,)), pltpu.SemaphoreType.DMA((nw * nl,)),
                        pltpu.SemaphoreType.DMA((nw * nl,))],
        compiler_params=_params(),
    )(*flat)


def _add_own_half(grad, got, c_idx, name):
    _, _, r, n = grad.shape
    tr = _pick(r, 256, 8)

    def body(c_ref, g_ref, t_ref, o_ref):
        o_ref[...] = g_ref[...] + t_ref[...]

    return pl.pallas_call(
        body, name=name, out_shape=jax.ShapeDtypeStruct((N_CHIPS, r, n), F32),
        grid_spec=pltpu.PrefetchScalarGridSpec(
            num_scalar_prefetch=1, grid=(N_CHIPS, r // tr),
            in_specs=[pl.BlockSpec((None, None, tr, n), lambda j, i, c_ref: (j, c_ref[0], i, 0)),
                      pl.BlockSpec((None, tr, n), lambda j, i, c_ref: (j, i, 0))],
            out_specs=pl.BlockSpec((None, tr, n), lambda j, i, c_ref: (j, i, 0))),
        compiler_params=_params(),
    )(c_idx, grad, got)


def _sum_slabs(slabs, name):
    ns, r, n = slabs.shape
    tr = _pick(r, 256, 8)

    def body(s_ref, o_ref):
        acc = s_ref[0]
        for j in range(1, ns):
            acc = acc + s_ref[j]
        o_ref[...] = acc

    return pl.pallas_call(
        body, name=name, out_shape=jax.ShapeDtypeStruct((r, n), F32), grid=(r // tr,),
        in_specs=[pl.BlockSpec((ns, tr, n), lambda i: (0, i, 0))], out_specs=pl.BlockSpec((tr, n), lambda i: (i, 0)),
        compiler_params=_params(),
    )(slabs)


def _band_bias(rel_bias):
    h = rel_bias.shape[0]
    n_far = PAD - REL_CLIP + CHUNK
    far = jnp.broadcast_to(rel_bias[:, N_REL - 1:N_REL], (h, n_far))
    near = rel_bias[:, REL_CLIP - CHUNK + 1:N_REL - 1][:, ::-1]
    line = jnp.concatenate([far, near], axis=1)
    return jnp.stack([line[:, CHUNK - 1 - q:CHUNK - 1 - q + BAND] for q in range(CHUNK)], axis=1)


def _pack_rows(pieces):
    flat = jnp.concatenate([p.reshape(-1) for p in pieces])
    rows = -(-flat.shape[0] // (8 * LANES)) * 8
    return jnp.pad(flat, (0, rows * LANES - flat.shape[0])).reshape(rows, LANES)


def _unpack_rows(packed, shapes):
    flat = packed.reshape(-1)
    out, at = [], 0
    for shp in shapes:
        size = 1
        for n in shp:
            size *= n
        out.append(flat[at:at + size].reshape(shp))
        at += size
    return out


def _layer_fwd(x, mod, w, band, tag):
    s, d = x.shape
    row = lambda i: mod[i:i + 1]
    h1 = _norm_mod(x, row(1), row(0), f"norm_mix{tag}")
    proj = _matmul(h1, w["w_in"], form="nn", out_dtype=BF16, tm=_pick(s, 512), tn=_pick(w["w_in"].shape[1], 768),
                   tk=d, name=f"proj{tag}")
    oa = _attn_a_fwd(proj, band, f"attn_a{tag}")
    ob, ltot = _sb_fwd(proj, f"attn_b{tag}")
    cat = _out_norm(oa, ob, w["g_a"], w["g_b"], f"out_norm{tag}")
    mixed = _matmul(cat, w["w_out"], form="nn", out_dtype=F32, tm=_pick(s, 512), tn=_pick(d, 1024),
                    tk=cat.shape[1], name=f"mix_out{tag}")
    x1 = _residual(x, row(2), mixed, f"res_mix{tag}")
    h2 = _norm_mod(x1, row(4), row(3), f"norm_ffn{tag}")
    f2 = w["w_up"].shape[1]
    u = _matmul(h2, w["w_up"], form="nn", out_dtype=F32, tm=_pick(s, 512), tn=_pick(f2, 1408), tk=d, name=f"up{tag}",
                rider=up_rider)
    if up_rider is not None:
        u, arrived = u
        on_up_arrival(arrived)
    a = _conv_glu(u, w["conv_w"], w["conv_b"], f"conv_glu{tag}")
    f = _matmul(a, w["w_down"], form="nn", out_dtype=F32, tm=_pick(s, 512), tn=_pick(d, 1024),
                tk=_pick(f2 // 2, 2816), name=f"down{tag}")
    x2 = _residual(x1, row(5), f, f"res_ffn{tag}")
    saved = dict(x=x, h1=h1, proj=proj, oa=oa, ob=ob, ltot=ltot, cat=cat, mixed=mixed, x1=x1, h2=h2, u=u, a=a, f=f)
    return x2, saved


def _layer_bwd(dx2, sv, mod, w, band, tag):
    s, d = dx2.shape
    row = lambda i: mod[i:i + 1]
    f2 = w["w_up"].shape[1]
    ff = f2 // 2
    n_in = w["w_in"].shape[1]
    df, dgate_ffn = _gate_bwd(dx2, sv["f"], row(5), f"gate_ffn_bwd{tag}")
    da = _matmul(df, w["w_down"], form="nt", out_dtype=F32, tm=_pick(s, 512), tn=_pick(ff, 1408), tk=d, name=f"down_dx{tag}")
    g_down = _matmul(sv["a"], df, form="tn", out_dtype=F32, tm=_pick(ff, 1408), tn=_pick(d, 512), tk=_pick(s, 2048),
                     name=f"down_dw{tag}")
    du2, dcw, dcb = _conv_glu_bwd(sv["u"], da, w["conv_w"], w["conv_b"], f"conv_glu_bwd{tag}")
    dh2 = _matmul(du2, w["w_up"], form="nt", out_dtype=F32, tm=_pick(s, 512), tn=_pick(d, 1024), tk=_pick(ff, 2816),
                  name=f"up_dx{tag}", halves=True, rider=None if waiting is None else _sibling_rider(list(waiting.values())))
    if waiting is not None:
        dh2, older_from_sib = dh2
    g_up = _matmul(sv["h2"], du2, form="tn", out_dtype=F32, tm=_pick(d, 512), tn=_pick(f2 // N_CHIPS, 1408),
                   tk=_pick(s, 2048), name=f"up_dw{tag}", shard_cols=f2 // N_CHIPS, halves=True)
    dx1, dscale_ffn, dshift_ffn = _norm_mod_bwd(sv["x1"], dh2, dx2, row(4), f"norm_ffn_bwd{tag}")
    dmixed, dgate_mix = _gate_bwd(dx1, sv["mixed"], row(2), f"gate_mix_bwd{tag}")
    dcat = _matmul(dmixed, w["w_out"], form="nt", out_dtype=F32, tm=_pick(s, 512), tn=_pick(2 * W_GRP, 1024), tk=d,
                   name=f"mix_out_dx{tag}")
    g_out = _matmul(sv["cat"], dmixed, form="tn", out_dtype=F32, tm=_pick(2 * W_GRP, 512), tn=_pick(d, 1024),
                    tk=_pick(s, 2048), name=f"mix_out_dw{tag}")
    doa, dob, dg_a, dg_b = _out_norm_bwd(sv["oa"], sv["ob"], dcat, w["g_a"], w["g_b"], f"out_norm_bwd{tag}")
    dqa, dka, dva, dband = _attn_a_bwd(sv["proj"], band, doa, f"attn_a_bwd{tag}")
    dqb, dkb, dvb = _sb_bwd(sv["proj"], sv["ltot"], dob, f"attn_b_bwd{tag}")
    drel = _rel_bias_grad(jnp.transpose(dband, (1, 0, 2)), f"rel_bias_bwd{tag}")[:, :N_REL]
    dproj = jnp.concatenate([dqa, dka, dva, dqb, dkb, dvb], axis=1).astype(BF16)
    dh1 = _matmul(dproj, w["w_in"], form="nt", out_dtype=F32, tm=_pick(s, 512), tn=_pick(d, 1024), tk=_pick(n_in, 3072),
                  name=f"proj_dx{tag}")
    g_in = _matmul(sv["h1"], dproj, form="tn", out_dtype=F32, tm=_pick(d, 512), tn=_pick(n_in // N_CHIPS, 768),
                   tk=_pick(s, 2048), name=f"proj_dw{tag}", shard_cols=n_in // N_CHIPS)
    dx, dscale_mix, dshift_mix = _norm_mod_bwd(sv["x"], dh1, dx1, row(1), f"norm_mix_bwd{tag}")
    dmod = jnp.concatenate([dshift_mix, dscale_mix, dgate_mix, dshift_ffn, dscale_ffn, dgate_ffn], axis=1)
    big = dict(w_in=g_in, w_out=g_out, w_up=g_up, w_down=g_down)
    dconv_w = jnp.concatenate([dcw[0], dcw[1]], axis=1)
    dconv_b = jnp.concatenate([dcb[0], dcb[1]], axis=1)
    small = dict(dmod=dmod, rel_bias=drel, g_a=dg_a, g_b=dg_b, conv_w=dconv_w, conv_b=dconv_b)
    return dx, big, small


def _kernel_unoverlapped(x, c, w_ada, b_ada, w_in, rel_bias, g_a, g_b, w_out, w_up, conv_w, conv_b, w_down, final_g, loss_target, m_w_ada, m_b_ada, m_w_in, m_rel_bias, m_g_a, m_g_b, m_w_out, m_w_up, m_conv_w, m_conv_b, m_w_down, m_final_g, v_w_ada, v_b_ada, v_w_in, v_rel_bias, v_g_a, v_g_b, v_w_out, v_w_up, v_conv_w, v_conv_b, v_w_down, v_final_g):
    xi, yi, ci = _my_place()
    chip = 2 * xi + yi
    dev = 4 * xi + 2 * yi + ci
    nl, d, n_ada = w_ada.shape
    s = x.shape[1]
    f2 = N_CHIPS * w_up.shape[2]
    nc = conv_w.shape[2]

    c_pad = jnp.pad(c, ((0, 7), (0, 0)))
    c_all = _allgather8(c_pad, "gather_c")[0::8]
    b_mine = lax.dynamic_slice_in_dim(b_ada, chip * n_ada, n_ada, axis=1)[:, None, :]
    mod_shard = _ada_fwd(c_all, w_ada, b_mine, "ada")
    pack2 = _pack_rows([mod_shard, conv_w])
    got2 = _allgather8(pack2, "gather_mod").reshape(N_DEV, -1)
    mods, convs = [], []
    for j in range(N_CHIPS):
        ms, cw = _unpack_rows(got2[2 * j], [mod_shard.shape, conv_w.shape])
        mods.append(lax.dynamic_index_in_dim(ms, dev, axis=1, keepdims=False))
        convs.append(cw)
    mod = jnp.concatenate(mods, axis=1).reshape(nl, 6, d)
    conv_w_full = jnp.concatenate(convs, axis=2)

    names = ("w_in", "w_out", "w_up", "w_down")
    kinds = ("col", "row", "col", "row")
    shards = dict(w_in=w_in, w_out=w_out, w_up=w_up, w_down=w_down)
    full = _gather_weights([shards[n].astype(BF16) for n in names], kinds, "gather_weights")
    full = dict(zip(names, full))

    xs = x[0]
    layers, saved, bands = [], [], []
    for l in range(nl):
        w = {n: full[n][l] for n in names}
        w.update(g_a=g_a[l:l + 1], g_b=g_b[l:l + 1], conv_w=conv_w_full[l], conv_b=conv_b[l:l + 1])
        band = _band_bias(rel_bias[l])
        xs, sv = _layer_fwd(xs, mod[l], w, band, f"_l{l}")
        layers.append(w)
        bands.append(band)
        saved.append(sv)
    loss_part, dx, dfinal_g = _loss_head(xs, final_g[None, :], loss_target[0], "loss_head")
    loss = lax.psum(loss_part[0, 0], ("x", "y", "c"))

    big, small = [None] * nl, [None] * nl
    for l in reversed(range(nl)):
        dx, big[l], small[l] = _layer_bwd(dx, saved[l], mod[l], layers[l], bands[l], f"_l{l}")

    small_names = ("dmod", "rel_bias", "g_a", "g_b", "conv_w", "conv_b")
    pieces = [small[l][n] for l in range(nl) for n in small_names] + [dfinal_g]
    shapes = [p.shape for p in pieces]
    pack3 = _pack_rows(pieces)
    got3 = _allgather8(pack3, "gather_small").reshape(N_DEV, pack3.shape[0], LANES)
    summed = _unpack_rows(_sum_slabs(got3, "sum_small"), shapes)
    tot = [dict(zip(small_names, summed[len(small_names) * l:len(small_names) * (l + 1)])) for l in range(nl)]
    g_final_g = summed[-1].reshape(-1)
    g_b_ada = jnp.stack([tot[l]["dmod"].reshape(-1) for l in range(nl)])
    g_rel = jnp.stack([tot[l]["rel_bias"] for l in range(nl)])
    g_ga = jnp.stack([tot[l]["g_a"].reshape(-1) for l in range(nl)])
    g_gb = jnp.stack([tot[l]["g_b"].reshape(-1) for l in range(nl)])
    g_conv_b = jnp.stack([tot[l]["conv_b"].reshape(-1) for l in range(nl)])
    g_conv_w = jnp.stack([lax.dynamic_slice_in_dim(tot[l]["conv_w"], chip * nc, nc, axis=1) for l in range(nl)])
    per_dev = [_unpack_rows(got3[j], shapes) for j in range(N_DEV)]
    dmod_all = jnp.stack([jnp.stack([per_dev[j][len(small_names) * l].reshape(-1) for j in range(N_DEV)])
                          for l in range(nl)])
    g_w_ada = _ada_bwd(c_all, lax.dynamic_slice_in_dim(dmod_all, chip * n_ada, n_ada, axis=2), "ada_bwd")

    order = [(n, l) for n in names for l in range(nl)]
    flat_g = [big[l][n].reshape(N_CHIPS, 2, -1, 1024) for n, l in order]
    from_sib = _rs_to_sibling(flat_g, "rs_sibling")
    c_idx = jnp.reshape(ci, (1,)).astype(jnp.int32)
    chip_part = [_add_own_half(g, t, c_idx, f"rs_add_{n}_l{l}") for g, t, (n, l) in zip(flat_g, from_sib, order)]
    from_chips = _rs_to_chips(chip_part, "rs_chips")
    my_half = [_sum_slabs(t, f"rs_sum_{n}_l{l}") for t, (n, l) in zip(from_chips, order)]
    shard_g = _rs_share_halves([[my_half[nl * i + l] for l in range(nl)] for i in range(len(names))], "rs_halves")
    g_big = {n: shard_g[i].reshape(shards[n].shape) for i, n in enumerate(names)}

    grads = dict(w_ada=g_w_ada, b_ada=g_b_ada, rel_bias=g_rel, g_a=g_ga, g_b=g_gb, conv_w=g_conv_w, conv_b=g_conv_b,
                 final_g=g_final_g)
    weights = dict(w_ada=w_ada, b_ada=b_ada, w_in=w_in, rel_bias=rel_bias, g_a=g_a, g_b=g_b, w_out=w_out, w_up=w_up,
                   conv_w=conv_w, conv_b=conv_b, w_down=w_down, final_g=final_g)
    m_in = dict(w_ada=m_w_ada, b_ada=m_b_ada, w_in=m_w_in, rel_bias=m_rel_bias, g_a=m_g_a, g_b=m_g_b, w_out=m_w_out,
                w_up=m_w_up, conv_w=m_conv_w, conv_b=m_conv_b, w_down=m_w_down, final_g=m_final_g)
    v_in = dict(w_ada=v_w_ada, b_ada=v_b_ada, w_in=v_w_in, rel_bias=v_rel_bias, g_a=v_g_a, g_b=v_g_b, w_out=v_w_out,
                w_up=v_w_up, conv_w=v_conv_w, conv_b=v_conv_b, w_down=v_w_down, final_g=v_final_g)
    order_w = ("w_ada", "b_ada", "w_in", "rel_bias", "g_a", "g_b", "w_out", "w_up", "conv_w", "conv_b", "w_down", "final_g")
    upd = {n: _adamw_nd(grads[n], weights[n], m_in[n], v_in[n], f"adamw_{n}") for n in grads}
    for n, mn, th in zip(BIG, mine, theirs):
        grads[n], *upd[n] = _adamw_halves(mn, th, c_idx, weights[n], m_in[n], v_in[n], f"adamw_{n}")
    return (loss, dx[None], *[grads[n] for n in order_w], *[upd[n][0] for n in order_w],
            *[upd[n][1] for n in order_w], *[upd[n][2] for n in order_w])


class _Rider:
    def __init__(self, ins, out_shapes, n_remote, n_local, parts):
        self.ins = list(ins)
        self.out_shapes = list(out_shapes)
        self.scratch = [pltpu.SemaphoreType.DMA((n_remote,)), pltpu.SemaphoreType.DMA((n_remote,)),
                        pltpu.SemaphoreType.DMA((max(n_local, 1),))]
        self.parts = parts

    def start(self, in_refs, out_refs, sems):
        local, sends, _ = self.parts(in_refs, out_refs, sems)
        for cp in local() + sends():
            cp.start()

    def wait(self, in_refs, out_refs, sems):
        local, sends, recvs = self.parts(in_refs, out_refs, sems)
        for cp in recvs():
            cp.wait_recv()
        for cp in sends():
            cp.wait_send()
        for cp in local():
            cp.wait()


class _JoinedRider:
    def __init__(self, riders):
        self.riders = riders
        self.ins = [a for r in riders for a in r.ins]
        self.out_shapes = [o for r in riders for o in r.out_shapes]
        self.scratch = [sc for r in riders for sc in r.scratch]

    def _each(self, in_refs, out_refs, sems):
        i = o = sc = 0
        for r in self.riders:
            yield (r, in_refs[i:i + len(r.ins)], out_refs[o:o + len(r.out_shapes)], sems[sc:sc + len(r.scratch)])
            i, o, sc = i + len(r.ins), o + len(r.out_shapes), sc + len(r.scratch)

    def start(self, in_refs, out_refs, sems):
        for r, i, o, sc in self._each(in_refs, out_refs, sems):
            r.start(i, o, sc)

    def wait(self, in_refs, out_refs, sems):
        for r, i, o, sc in self._each(in_refs, out_refs, sems):
            r.wait(i, o, sc)

    def split(self, results):
        out, at = [], 0
        for r in self.riders:
            out.append(list(results[at:at + len(r.out_shapes)]))
            at += len(r.out_shapes)
        return out


def _call_with_rider(body, rider, *, name, grid, out_shape, in_specs, out_specs, scratch_shapes, args):
    if rider is None:
        return pl.pallas_call(body, name=name, grid=grid, out_shape=tuple(out_shape), in_specs=list(in_specs),
                              out_specs=tuple(out_specs), scratch_shapes=list(scratch_shapes),
                              compiler_params=_params())(*args)
    n_in, n_out, n_scr = len(in_specs), len(out_specs), len(scratch_shapes)
    r_in, r_out = len(rider.ins), len(rider.out_shapes)

    def both(*refs):
        at = 0
        groups = []
        for size in (n_in, r_in, n_out, r_out, n_scr, len(rider.scratch)):
            groups.append(refs[at:at + size])
            at += size
        own_in, ride_in, own_out, ride_out, own_scr, sems = groups
        steps = [pl.program_id(a) for a in range(len(grid))]
        first = functools.reduce(jnp.logical_and, [st == 0 for st in steps])
        last = functools.reduce(jnp.logical_and, [st == g - 1 for st, g in zip(steps, grid)])

        @pl.when(first)
        def _():
            rider.start(ride_in, ride_out, sems)

        body(*own_in, *own_out, *own_scr)

        @pl.when(last)
        def _():
            rider.wait(ride_in, ride_out, sems)

    outs = pl.pallas_call(
        both, name=name, grid=grid, out_shape=tuple(out_shape) + tuple(rider.out_shapes),
        in_specs=list(in_specs) + [ANY] * r_in, out_specs=tuple(out_specs) + (ANY,) * r_out,
        scratch_shapes=list(scratch_shapes) + rider.scratch, compiler_params=_params(),
    )(*args, *rider.ins)
    return tuple(outs[:n_out]) + (list(outs[n_out:]),)


def _run_rider(rider, name):
    r_in, r_out = len(rider.ins), len(rider.out_shapes)

    def body(*refs):
        ins, outs, sems = refs[:r_in], refs[r_in:r_in + r_out], refs[r_in + r_out:]
        rider.start(ins, outs, sems)
        rider.wait(ins, outs, sems)

    return list(pl.pallas_call(
        body, name=name, out_shape=tuple(rider.out_shapes), in_specs=[ANY] * r_in, out_specs=(ANY,) * r_out,
        scratch_shapes=rider.scratch, compiler_params=_params(),
    )(*rider.ins))


def _remote(src, dst, sems, n, peer):
    return pltpu.make_async_remote_copy(src_ref=src, dst_ref=dst, send_sem=sems[0].at[n], recv_sem=sems[1].at[n],
                                        device_id=peer, device_id_type=MESH)


def _gather_rider(shards, kinds):
    nw = len(shards)
    out_shapes = [jax.ShapeDtypeStruct((a.shape[0], N_CHIPS * a.shape[1]) if kd == "col" else
                                       (N_CHIPS * a.shape[0], a.shape[1]), a.dtype) for a, kd in zip(shards, kinds)]

    def parts(ins, outs, sems):
        x, y, c = _my_place()
        chip = 2 * x + y
        peers = _chip_peers(x, y, c)

        def window(w, j):
            r, n = shards[w].shape
            if kinds[w] == "col":
                return outs[w].at[:, pl.ds(pl.multiple_of(j * n, LANES), n)]
            return outs[w].at[pl.ds(pl.multiple_of(j * r, 16), r), :]

        local = lambda: [pltpu.make_async_copy(ins[w], window(w, chip), sems[2].at[w]) for w in range(nw)]
        sends = lambda: [_remote(ins[w], window(w, chip), sems, 3 * w + k, peer)
                         for w in range(nw) for k, (_, peer) in enumerate(peers)]
        recvs = lambda: [_remote(ins[w], window(w, pchip), sems, 3 * w + k, peer)
                         for w in range(nw) for k, (pchip, peer) in enumerate(peers)]
        return local, sends, recvs

    return _Rider(shards, out_shapes, 3 * nw, nw, parts)


def _half(ref3, h, rows):
    return ref3.at[:, pl.ds(pl.multiple_of(h * rows, 8), rows), :]


def _sibling_rider(grads):
    nw = len(grads)
    out_shapes = [jax.ShapeDtypeStruct((g.shape[0], g.shape[1] // 2, g.shape[2]), g.dtype) for g in grads]

    def parts(ins, outs, sems):
        x, y, c = _my_place()
        sibling = (x, y, 1 - c)
        def copies():
            out = []
            for w in range(nw):
                rows = grads[w].shape[1] // 2
                for j in range(N_CHIPS):
                    src = ins[w].at[j, pl.ds(pl.multiple_of((1 - c) * rows, 8), rows), :]
                    out.append(_remote(src, outs[w].at[j], sems, N_CHIPS * w + j, sibling))
            return out

        return (lambda: []), copies, copies

    return _Rider(grads, out_shapes, N_CHIPS * nw, 0, parts)


def _chips_rider(parts_in):
    nw = len(parts_in)
    out_shapes = [jax.ShapeDtypeStruct(p.shape, p.dtype) for p in parts_in]

    def parts(ins, outs, sems):
        x, y, c = _my_place()
        chip = 2 * x + y
        peers = _chip_peers(x, y, c)
        local = lambda: []
        sends = lambda: [_remote(ins[w].at[pchip], outs[w].at[chip], sems, 3 * w + k, peer)
                         for w in range(nw) for k, (pchip, peer) in enumerate(peers)]
        recvs = lambda: [_remote(ins[w].at[chip], outs[w].at[pchip], sems, 3 * w + k, peer)
                         for w in range(nw) for k, (pchip, peer) in enumerate(peers)]
        return local, sends, recvs

    return _Rider(parts_in, out_shapes, 3 * nw, nw, parts)


def _halves_rider(halves):
    nw, nl = len(halves), len(halves[0])
    flat = [h for hs in halves for h in hs]
    out_shapes = [jax.ShapeDtypeStruct((nl, 2 * hs[0].shape[0], hs[0].shape[1]), hs[0].dtype) for hs in halves]

    def parts(ins, outs, sems):
        x, y, c = _my_place()
        sibling = (x, y, 1 - c)

        def window(w, l, h):
            rows = halves[w][0].shape[0]
            return outs[w].at[l, pl.ds(pl.multiple_of(h * rows, 8), rows), :]

        pairs = [(w, l) for w in range(nw) for l in range(nl)]
        local = lambda: [pltpu.make_async_copy(ins[nl * w + l], window(w, l, c), sems[2].at[nl * w + l]) for w, l in pairs]
        sends = lambda: [_remote(ins[nl * w + l], window(w, l, c), sems, nl * w + l, sibling) for w, l in pairs]
        recvs = lambda: [_remote(ins[nl * w + l], window(w, l, 1 - c), sems, nl * w + l, sibling) for w, l in pairs]
        return local, sends, recvs

    return _Rider(flat, out_shapes, nw * nl, nw * nl, parts)


def _add_my_half(grad, got, c_idx, name):
    _, r, n = got.shape
    tr = _pick(r, 256, 16)
    nblk = r // tr

    def body(c_ref, g_ref, t_ref, o_ref, ob_ref):
        tot = g_ref[...] + t_ref[...]
        o_ref[...] = tot
        ob_ref[...] = tot.astype(BF16)

    blk = pl.BlockSpec((None, tr, n), lambda j, i, c_ref: (j, i, 0))
    return pl.pallas_call(
        body, name=name, out_shape=(jax.ShapeDtypeStruct(got.shape, F32), jax.ShapeDtypeStruct(got.shape, BF16)),
        grid_spec=pltpu.PrefetchScalarGridSpec(
            num_scalar_prefetch=1, grid=(N_CHIPS, nblk),
            in_specs=[pl.BlockSpec((None, tr, n), lambda j, i, c_ref: (j, c_ref[0] * nblk + i, 0)), blk],
            out_specs=(blk, blk)),
        compiler_params=_params(),
    )(c_idx, grad, got)


def _swap_rider(mine):
    nw = len(mine)
    out_shapes = [jax.ShapeDtypeStruct(a.shape, a.dtype) for a in mine]

    def parts(ins, outs, sems):
        x, y, c = _my_place()
        copies = lambda: [_remote(ins[w], outs[w], sems, w, (x, y, 1 - c)) for w in range(nw)]
        return (lambda: []), copies, copies

    return _Rider(mine, out_shapes, nw, 0, parts)


def _sum_layers(own, got, chip_idx, name):
    nl = len(own)
    _, r, n = own[0].shape
    tr = _pick(r, 256, 16)

    def body(chip_ref, *refs):
        o_ref = refs[4 * nl]
        for l in range(nl):
            acc = refs[4 * l][...]
            for k in range(1, N_CHIPS):
                acc = acc + refs[4 * l + k][...].astype(F32)
            o_ref[l] = acc

    slab = lambda k: pl.BlockSpec((None, tr, n), lambda i, chip_ref: (chip_ref[0] ^ k, i, 0))
    return pl.pallas_call(
        body, name=name, out_shape=jax.ShapeDtypeStruct((nl, r, n), F32),
        grid_spec=pltpu.PrefetchScalarGridSpec(
            num_scalar_prefetch=1, grid=(r // tr,), in_specs=[slab(k) for _ in range(nl) for k in range(N_CHIPS)],
            out_specs=pl.BlockSpec((nl, tr, n), lambda i, chip_ref: (0, i, 0))),
        compiler_params=_params(),
    )(chip_idx, *[a for l in range(nl) for a in (own[l], got[l], got[l], got[l])])


def _adam_math(gv, w, m, v):
    c1 = 1.0 - ADAM_B1 ** ADAM_STEP
    c2 = 1.0 - ADAM_B2 ** ADAM_STEP
    nm = ADAM_B1 * m + (1.0 - ADAM_B1) * gv
    nv = ADAM_B2 * v + (1.0 - ADAM_B2) * (gv * gv)
    return -ADAM_LR * ((nm / c1) / (jnp.sqrt(nv / c2) + ADAM_EPS) + ADAM_WD * w), nm, nv


def _adamw_halves(mine, theirs, c_idx, w, m, v, name):
    nl, r, n = mine.shape
    tr = _pick(r, 256, 8)
    nblk = r // tr

    def body(c_ref, mine_ref, theirs_ref, w_ref, m_ref, v_ref, g_ref, d_ref, nm_ref, nv_ref):
        gv = jnp.where(pl.program_id(1) == c_ref[0], mine_ref[...], theirs_ref[...])
        g_ref[...] = gv
        d_ref[...], nm_ref[...], nv_ref[...] = _adam_math(gv, w_ref[...], m_ref[...], v_ref[...])

    half = pl.BlockSpec((None, tr, n), lambda l, h, i, c_ref: (l, i, 0))
    full = pl.BlockSpec((None, tr, n), lambda l, h, i, c_ref: (l, h * nblk + i, 0))
    return pl.pallas_call(
        body, name=name, out_shape=(jax.ShapeDtypeStruct(w.shape, F32),) * 4,
        grid_spec=pltpu.PrefetchScalarGridSpec(
            num_scalar_prefetch=1, grid=(nl, 2, nblk), in_specs=[half, half, full, full, full],
            out_specs=(full,) * 4),
        compiler_params=_params(),
    )(c_idx, mine, theirs, w, m, v)


def _by_chip(g):
    return g if g.ndim == 3 else g.reshape(N_CHIPS, g.shape[0] // N_CHIPS, g.shape[1])


BIG = ("w_in", "w_out", "w_up", "w_down")
BIG_KIND = dict(w_in="col", w_out="row", w_up="col", w_down="row")


def _forward_layer(x, mod, w, band, tag, rider=None, on_arrival=None, up_rider=None, on_up_arrival=None):
    s, d = x.shape
    row = lambda i: mod[i:i + 1]
    h1 = _norm_mod(x, row(1), row(0), f"norm_mix{tag}")
    proj = _matmul(h1, w["w_in"], form="nn", out_dtype=BF16, tm=_pick(s, 512), tn=_pick(w["w_in"].shape[1], 768),
                   tk=d, name=f"proj{tag}")
    oa = _attn_a_fwd(proj, band, f"attn_a{tag}")
    if rider is None:
        ob, ltot = _sb_fwd(proj, f"attn_b{tag}")
    else:
        ob, ltot, arrived = _sb_fwd(proj, f"attn_b{tag}", rider)
        on_arrival(arrived)
    cat = _out_norm(oa, ob, w["g_a"], w["g_b"], f"out_norm{tag}")
    mixed = _matmul(cat, w["w_out"], form="nn", out_dtype=F32, tm=_pick(s, 512), tn=_pick(d, 1024),
                    tk=cat.shape[1], name=f"mix_out{tag}")
    x1 = _residual(x, row(2), mixed, f"res_mix{tag}")
    h2 = _norm_mod(x1, row(4), row(3), f"norm_ffn{tag}")
    f2 = w["w_up"].shape[1]
    u = _matmul(h2, w["w_up"], form="nn", out_dtype=F32, tm=_pick(s, 512), tn=_pick(f2, 1408), tk=d, name=f"up{tag}",
                rider=up_rider)
    if up_rider is not None:
        u, arrived = u
        on_up_arrival(arrived)
    a = _conv_glu(u, w["conv_w"], w["conv_b"], f"conv_glu{tag}")
    f = _matmul(a, w["w_down"], form="nn", out_dtype=F32, tm=_pick(s, 512), tn=_pick(d, 1024),
                tk=_pick(f2 // 2, 2816), name=f"down{tag}")
    x2 = _residual(x1, row(5), f, f"res_ffn{tag}")
    saved = dict(x=x, h1=h1, proj=proj, oa=oa, ob=ob, ltot=ltot, cat=cat, mixed=mixed, x1=x1, h2=h2, u=u, a=a, f=f)
    return x2, saved


def _backward_layer(dx2, sv, mod, w, band, tag, c_idx, waiting=None):
    s, d = dx2.shape
    row = lambda i: mod[i:i + 1]
    f2 = w["w_up"].shape[1]
    ff = f2 // 2
    n_in = w["w_in"].shape[1]
    df, dgate_ffn = _gate_bwd(dx2, sv["f"], row(5), f"gate_ffn_bwd{tag}")
    da = _matmul(df, w["w_down"], form="nt", out_dtype=F32, tm=_pick(s, 512), tn=_pick(ff, 1408), tk=d, name=f"down_dx{tag}")
    g_down = _matmul(sv["a"], df, form="tn", out_dtype=F32, tm=_pick(ff, 1408), tn=_pick(d, 512), tk=_pick(s, 2048),
                     name=f"down_dw{tag}")
    du2, dcw, dcb = _conv_glu_bwd(sv["u"], da, w["conv_w"], w["conv_b"], f"conv_glu_bwd{tag}")
    dh2 = _matmul(du2, w["w_up"], form="nt", out_dtype=F32, tm=_pick(s, 512), tn=_pick(d, 1024), tk=_pick(ff, 2816),
                  name=f"up_dx{tag}", halves=True, rider=None if waiting is None else _sibling_rider(list(waiting.values())))
    if waiting is not None:
        dh2, older_from_sib = dh2
    g_up = _matmul(sv["h2"], du2, form="tn", out_dtype=F32, tm=_pick(d, 512), tn=_pick(f2 // N_CHIPS, 1408),
                   tk=_pick(s, 2048), name=f"up_dw{tag}", shard_cols=f2 // N_CHIPS, halves=True)
    dx1, dscale_ffn, dshift_ffn = _norm_mod_bwd(sv["x1"], dh2, dx2, row(4), f"norm_ffn_bwd{tag}")
    dmixed, dgate_mix = _gate_bwd(dx1, sv["mixed"], row(2), f"gate_mix_bwd{tag}")
    dcat = _matmul(dmixed, w["w_out"], form="nt", out_dtype=F32, tm=_pick(s, 512), tn=_pick(2 * W_GRP, 1024), tk=d,
                   name=f"mix_out_dx{tag}")
    g_out = _matmul(sv["cat"], dmixed, form="tn", out_dtype=F32, tm=_pick(2 * W_GRP, 512), tn=_pick(d, 1024),
                    tk=_pick(s, 2048), name=f"mix_out_dw{tag}")
    doa, dob, dg_a, dg_b = _out_norm_bwd(sv["oa"], sv["ob"], dcat, w["g_a"], w["g_b"], f"out_norm_bwd{tag}")
    big = {("w_down", tag): _by_chip(g_down), ("w_up", tag): _by_chip(g_up), ("w_out", tag): _by_chip(g_out)}
    if waiting is None:
        dqa, dka, dva, dband = _attn_a_bwd(sv["proj"], band, doa, f"attn_a_bwd{tag}")
        dqb, dkb, dvb = _sb_bwd(sv["proj"], sv["ltot"], dob, f"attn_b_bwd{tag}")
    else:
        old_keys, keys = list(waiting), list(big)
        add = lambda raw, k, t: _add_my_half(raw[k], t, c_idx, f"rs_add_{k[0]}{k[1]}")
        old_parts = [add(waiting, k, t) for k, t in zip(old_keys, older_from_sib)]
        both = _JoinedRider([_chips_rider([p16 for _, p16 in old_parts]), _sibling_rider([big[k] for k in keys])])
        dqa, dka, dva, dband, arrived = _attn_a_bwd(sv["proj"], band, doa, f"attn_a_bwd{tag}", both)
        old_got, from_sib = both.split(arrived)
        parts = [add(big, k, t) for k, t in zip(keys, from_sib)]
        dqb, dkb, dvb, got = _sb_bwd(sv["proj"], sv["ltot"], dob, f"attn_b_bwd{tag}",
                                     _chips_rider([p16 for _, p16 in parts]))
        big = {k: (p32, g) for k, (p32, _), g in zip(old_keys + keys, old_parts + parts, list(old_got) + list(got))}
    drel = _rel_bias_grad(jnp.transpose(dband, (1, 0, 2)), f"rel_bias_bwd{tag}")[:, :N_REL]
    dproj = jnp.concatenate([dqa, dka, dva, dqb, dkb, dvb], axis=1).astype(BF16)
    dh1 = _matmul(dproj, w["w_in"], form="nt", out_dtype=F32, tm=_pick(s, 512), tn=_pick(d, 1024), tk=_pick(n_in, 3072),
                  name=f"proj_dx{tag}")
    g_in = _matmul(sv["h1"], dproj, form="tn", out_dtype=F32, tm=_pick(d, 512), tn=_pick(n_in // N_CHIPS, 768),
                   tk=_pick(s, 2048), name=f"proj_dw{tag}", shard_cols=n_in // N_CHIPS)
    dx, dscale_mix, dshift_mix = _norm_mod_bwd(sv["x"], dh1, dx1, row(1), f"norm_mix_bwd{tag}")
    dmod = jnp.concatenate([dshift_mix, dscale_mix, dgate_mix, dshift_ffn, dscale_ffn, dgate_ffn], axis=1)
    dconv_w = jnp.concatenate([dcw[0], dcw[1]], axis=1)
    dconv_b = jnp.concatenate([dcb[0], dcb[1]], axis=1)
    small = dict(dmod=dmod, rel_bias=drel, g_a=dg_a, g_b=dg_b, conv_w=dconv_w, conv_b=dconv_b)
    return dx, big, g_in, small


def kernel(x, c, w_ada, b_ada, w_in, rel_bias, g_a, g_b, w_out, w_up, conv_w, conv_b, w_down, final_g, loss_target, m_w_ada, m_b_ada, m_w_in, m_rel_bias, m_g_a, m_g_b, m_w_out, m_w_up, m_conv_w, m_conv_b, m_w_down, m_final_g, v_w_ada, v_b_ada, v_w_in, v_rel_bias, v_g_a, v_g_b, v_w_out, v_w_up, v_conv_w, v_conv_b, v_w_down, v_final_g):
    xi, yi, ci = _my_place()
    chip = 2 * xi + yi
    dev = 4 * xi + 2 * yi + ci
    c_idx = jnp.reshape(ci, (1,)).astype(jnp.int32)
    nl, d, n_ada = w_ada.shape
    nc = conv_w.shape[2]
    assert nl == 2

    c_pad = jnp.pad(c, ((0, 7), (0, 0)))
    c_all = _allgather8(c_pad, "gather_c")[0::8]
    b_mine = lax.dynamic_slice_in_dim(b_ada, chip * n_ada, n_ada, axis=1)[:, None, :]
    mod_shard = _ada_fwd(c_all, w_ada, b_mine, "ada")
    pack2 = _pack_rows([mod_shard, conv_w])
    got2 = _allgather8(pack2, "gather_mod").reshape(N_DEV, -1)
    mods, convs = [], []
    for j in range(N_CHIPS):
        ms, cw = _unpack_rows(got2[2 * j], [mod_shard.shape, conv_w.shape])
        mods.append(lax.dynamic_index_in_dim(ms, dev, axis=1, keepdims=False))
        convs.append(cw)
    mod = jnp.concatenate(mods, axis=1).reshape(nl, 6, d)
    conv_w_full = jnp.concatenate(convs, axis=2)

    shards = dict(w_in=w_in, w_out=w_out, w_up=w_up, w_down=w_down)
    sh = {(n, l): shards[n][l].astype(BF16) for n in BIG for l in range(nl)}
    early = [("w_in", 0)]
    riding = [[("w_out", 0), ("w_up", 0), ("w_down", 0)], [("w_out", 1), ("w_up", 1), ("w_down", 1)]]
    riding_up = [[("w_in", 1)], None]
    layers = [dict(g_a=g_a[l:l + 1], g_b=g_b[l:l + 1], conv_w=conv_w_full[l], conv_b=conv_b[l:l + 1]) for l in range(nl)]

    def gather_rider(keys):
        return _gather_rider([sh[k] for k in keys], [BIG_KIND[k[0]] for k in keys])

    def arrival(keys):
        def fill(arrived):
            for (n, l), full in zip(keys, arrived):
                layers[l][n] = full
        return fill

    arrival(early)(_run_rider(gather_rider(early), "gather_first"))

    xs = x[0]
    saved, bands = [], []
    for l in range(nl):
        band = _band_bias(rel_bias[l])
        ups = riding_up[l]
        xs, sv = _forward_layer(xs, mod[l], layers[l], band, f"_l{l}", gather_rider(riding[l]), arrival(riding[l]),
                                gather_rider(ups) if ups else None, arrival(ups) if ups else None)
        bands.append(band)
        saved.append(sv)
    loss_part, dx, dfinal_g = _loss_head(xs, final_g[None, :], loss_target[0], "loss_head")
    loss = lax.psum(loss_part[0, 0], ("x", "y", "c"))

    small = [None] * nl
    dx, raw1, g_in1, small[1] = _backward_layer(dx, saved[1], mod[1], layers[1], bands[1], "_l1", c_idx)
    raw1[("w_in", "_l1")] = g_in1
    dx, parts, g_in0, small[0] = _backward_layer(dx, saved[0], mod[0], layers[0], bands[0], "_l0", c_idx, raw1)
    (from_sib,) = _run_rider(_sibling_rider([g_in0]), "rs_sibling_last")
    p32, p16 = _add_my_half(g_in0, from_sib, c_idx, "rs_add_w_in_l0")
    parts[("w_in", "_l0")] = (p32, _run_rider(_chips_rider([p16]), "rs_chips_last")[0])
    chip_idx = jnp.reshape(chip, (1,)).astype(jnp.int32)
    mine = [_sum_layers([parts[(n, f"_l{l}")][0] for l in range(nl)], [parts[(n, f"_l{l}")][1] for l in range(nl)],
                        chip_idx, f"rs_sum_{n}") for n in BIG]
    theirs = _run_rider(_swap_rider(mine), "rs_swap")

    small_names = ("dmod", "rel_bias", "g_a", "g_b", "conv_w", "conv_b")
    pieces = [small[l][n] for l in range(nl) for n in small_names] + [dfinal_g]
    shapes = [p.shape for p in pieces]
    pack3 = _pack_rows(pieces)
    got3 = _allgather8(pack3, "gather_small").reshape(N_DEV, pack3.shape[0], LANES)
    summed = _unpack_rows(_sum_slabs(got3, "sum_small"), shapes)
    tot = [dict(zip(small_names, summed[len(small_names) * l:len(small_names) * (l + 1)])) for l in range(nl)]
    g_final_g = summed[-1].reshape(-1)
    g_b_ada = jnp.stack([tot[l]["dmod"].reshape(-1) for l in range(nl)])
    g_rel = jnp.stack([tot[l]["rel_bias"] for l in range(nl)])
    g_ga = jnp.stack([tot[l]["g_a"].reshape(-1) for l in range(nl)])
    g_gb = jnp.stack([tot[l]["g_b"].reshape(-1) for l in range(nl)])
    g_conv_b = jnp.stack([tot[l]["conv_b"].reshape(-1) for l in range(nl)])
    g_conv_w = jnp.stack([lax.dynamic_slice_in_dim(tot[l]["conv_w"], chip * nc, nc, axis=1) for l in range(nl)])
    per_dev = [_unpack_rows(got3[j], shapes) for j in range(N_DEV)]
    dmod_all = jnp.stack([jnp.stack([per_dev[j][len(small_names) * l].reshape(-1) for j in range(N_DEV)])
                          for l in range(nl)])
    g_w_ada = _ada_bwd(c_all, lax.dynamic_slice_in_dim(dmod_all, chip * n_ada, n_ada, axis=2), "ada_bwd")

    grads = dict(w_ada=g_w_ada, b_ada=g_b_ada, rel_bias=g_rel, g_a=g_ga, g_b=g_gb, conv_w=g_conv_w, conv_b=g_conv_b,
                 final_g=g_final_g)
    weights = dict(w_ada=w_ada, b_ada=b_ada, w_in=w_in, rel_bias=rel_bias, g_a=g_a, g_b=g_b, w_out=w_out, w_up=w_up,
                   conv_w=conv_w, conv_b=conv_b, w_down=w_down, final_g=final_g)
    m_in = dict(w_ada=m_w_ada, b_ada=m_b_ada, w_in=m_w_in, rel_bias=m_rel_bias, g_a=m_g_a, g_b=m_g_b, w_out=m_w_out,
                w_up=m_w_up, conv_w=m_conv_w, conv_b=m_conv_b, w_down=m_w_down, final_g=m_final_g)
    v_in = dict(w_ada=v_w_ada, b_ada=v_b_ada, w_in=v_w_in, rel_bias=v_rel_bias, g_a=v_g_a, g_b=v_g_b, w_out=v_w_out,
                w_up=v_w_up, conv_w=v_conv_w, conv_b=v_conv_b, w_down=v_w_down, final_g=v_final_g)
    order_w = ("w_ada", "b_ada", "w_in", "rel_bias", "g_a", "g_b", "w_out", "w_up", "conv_w", "conv_b", "w_down", "final_g")
    upd = {n: _adamw_nd(grads[n], weights[n], m_in[n], v_in[n], f"adamw_{n}") for n in grads}
    for n, mn, th in zip(BIG, mine, theirs):
        grads[n], *upd[n] = _adamw_halves(mn, th, c_idx, weights[n], m_in[n], v_in[n], f"adamw_{n}")
    return (loss, dx[None], *[grads[n] for n in order_w], *[upd[n][0] for n in order_w],
            *[upd[n][1] for n in order_w], *[upd[n][2] for n in order_w])
```

```python
import functools

import jax
import jax.numpy as jnp
from jax import lax
from jax.experimental import pallas as pl
from jax.experimental.pallas import tpu as pltpu

F32 = jnp.float32
BF16 = jnp.bfloat16
MESH = pl.DeviceIdType.MESH
ANY = pl.BlockSpec(memory_space=pl.ANY)
VMEM_FULL = pl.BlockSpec(memory_space=pltpu.VMEM)

HEAD_DIM = 64
N_HEADS = 8
W_GRP = N_HEADS * HEAD_DIM
CHUNK = 64
N_PREV = 8
BAND = (N_PREV + 1) * CHUNK
PAD = N_PREV * CHUNK
REL_CLIP = 128
N_REL = 2 * REL_CLIP + 1
EPS = 1e-6
SB_DEAD = -110.0
N_CHIPS = 4
N_DEV = 8
LANES = 128
V7X_VMEM_LIMIT = 56 * 1024 * 1024

ADAM_LR = 0.001
ADAM_B1 = 0.9
ADAM_B2 = 0.999
ADAM_EPS = 1e-08
ADAM_WD = 0.01
ADAM_STEP = 10


def _params(**kw):
    return pltpu.CompilerParams(vmem_limit_bytes=V7X_VMEM_LIMIT, **kw)


def _pick(dim, pref, mult=LANES):
    t = (min(pref, dim) // mult) * mult
    while t >= mult:
        if dim % t == 0:
            return t
        t -= mult
    return dim


def _my_place():
    return lax.axis_index("x"), lax.axis_index("y"), lax.axis_index("c")


def _flip(v, bit):
    return 1 - v if bit else v


def _matmul(a, b, *, form, out_dtype, tm, tn, tk, name, shard_cols=None, halves=False, rider=None):
    if form == "nn":
        (m, k), (_, n) = a.shape, b.shape
        a_map, a_blk = (lambda i, j, kk: (i, kk)), (tm, tk)
        b_map, b_blk = (lambda i, j, kk: (kk, j)), (tk, tn)
        dims = (((1,), (0,)), ((), ()))
    elif form == "nt":
        m, k = (a.shape[1], 2 * a.shape[2]) if halves else a.shape
        n = b.shape[0]
        if halves:
            per_half = k // 2 // tk
            a_map, a_blk = (lambda i, j, kk: (kk // per_half, i, kk % per_half)), (None, tm, tk)
        else:
            a_map, a_blk = (lambda i, j, kk: (i, kk)), (tm, tk)
        b_map, b_blk = (lambda i, j, kk: (j, kk)), (tn, tk)
        dims = (((1,), (1,)), ((), ()))
    else:
        k, m = a.shape
        n = 2 * b.shape[2] if halves else b.shape[1]
        a_map, a_blk = (lambda i, j, kk: (kk, i)), (tk, tm)
        if halves:
            per_half = n // 2 // tn
            b_map, b_blk = (lambda i, j, kk: (j // per_half, kk, j % per_half)), (None, tk, tn)
        else:
            b_map, b_blk = (lambda i, j, kk: (kk, j)), (tk, tn)
        dims = (((0,), (0,)), ((), ()))
    assert m % tm == 0 and n % tn == 0 and k % tk == 0, (name, m, n, k, tm, tn, tk)
    nk = k // tk
    if shard_cols is None:
        out_shape = jax.ShapeDtypeStruct((m, n), out_dtype)
        o_map, o_blk = (lambda i, j, kk: (i, j)), (tm, tn)
    else:
        per = shard_cols // tn
        assert shard_cols % tn == 0
        out_shape = jax.ShapeDtypeStruct((n // shard_cols, m, shard_cols), out_dtype)
        o_map, o_blk = (lambda i, j, kk: (j // per, i, j % per)), (None, tm, tn)
    a_bytes, b_bytes = a.size * a.dtype.itemsize, b.size * b.dtype.itemsize
    rows_outer = nk > 1 or (m // tm) * b_bytes + a_bytes <= (n // tn) * a_bytes + b_bytes
    grid = (m // tm, n // tn, nk) if rows_outer else (n // tn, m // tm, nk)
    order = (lambda f: f) if rows_outer else (lambda f: (lambda g0, g1, kk: f(g1, g0, kk)))

    def body(a_ref, b_ref, o_ref, *acc):
        part = lax.dot_general(a_ref[...], b_ref[...], dims, preferred_element_type=F32)
        if nk == 1:
            o_ref[...] = part.astype(out_dtype)
            return
        acc_ref, = acc
        kk = pl.program_id(2)

        @pl.when(kk == 0)
        def _():
            acc_ref[...] = part

        @pl.when(jnp.logical_and(kk > 0, kk < nk - 1))
        def _():
            acc_ref[...] += part

        @pl.when(kk == nk - 1)
        def _():
            o_ref[...] = (acc_ref[...] + part).astype(out_dtype)

    out = _call_with_rider(
        body, rider, name=name, out_shape=(out_shape,), grid=grid,
        in_specs=[pl.BlockSpec(a_blk, order(a_map)), pl.BlockSpec(b_blk, order(b_map))],
        out_specs=(pl.BlockSpec(o_blk, order(o_map)),),
        scratch_shapes=[pltpu.VMEM((tm, tn), F32)] if nk > 1 else [], args=(a, b))
    return out[0] if rider is None else out


def _row_spec(tr, d):
    return pl.BlockSpec((tr, d), lambda i: (i, 0))


def _vec_spec(d):
    return pl.BlockSpec((1, d), lambda i: (0, 0))


def _rms(xf):
    r = lax.rsqrt(jnp.mean(xf * xf, axis=-1, keepdims=True) + EPS)
    return xf * r, r


def _norm_mod(x, scale, shift, name):
    s, d = x.shape
    tr = _pick(s, 512, 8)

    def body(x_ref, sc_ref, sh_ref, o_ref):
        n, _ = _rms(x_ref[...])
        o_ref[...] = (n * (1.0 + sc_ref[...]) + sh_ref[...]).astype(BF16)

    return pl.pallas_call(
        body, name=name, out_shape=jax.ShapeDtypeStruct((s, d), BF16), grid=(s // tr,),
        in_specs=[_row_spec(tr, d), _vec_spec(d), _vec_spec(d)], out_specs=_row_spec(tr, d),
        compiler_params=_params(),
    )(x, scale, shift)


def _out_norm(oa, ob, g_a, g_b, name):
    s, w = oa.shape
    tr = _pick(s, 512, 8)

    def body(oa_ref, ob_ref, ga_ref, gb_ref, o_ref):
        na, _ = _rms(oa_ref[...])
        nb, _ = _rms(ob_ref[...])
        o_ref[:, :w] = (na * ga_ref[...]).astype(BF16)
        o_ref[:, w:] = (nb * gb_ref[...]).astype(BF16)

    return pl.pallas_call(
        body, name=name, out_shape=jax.ShapeDtypeStruct((s, 2 * w), BF16), grid=(s // tr,),
        in_specs=[_row_spec(tr, w), _row_spec(tr, w), _vec_spec(w), _vec_spec(w)],
        out_specs=_row_spec(tr, 2 * w), compiler_params=_params(),
    )(oa, ob, g_a, g_b)


def _residual(x, gate, m, name):
    s, d = x.shape
    tr = _pick(s, 512, 8)

    def body(x_ref, g_ref, m_ref, o_ref):
        o_ref[...] = x_ref[...] + g_ref[...] * m_ref[...]

    return pl.pallas_call(
        body, name=name, out_shape=jax.ShapeDtypeStruct((s, d), F32), grid=(s // tr,),
        in_specs=[_row_spec(tr, d), _vec_spec(d), _row_spec(tr, d)], out_specs=_row_spec(tr, d),
        compiler_params=_params(),
    )(x, gate, m)


def _shift_down(u, k):
    rows = lax.broadcasted_iota(jnp.int32, u.shape, 0)
    return jnp.where(rows >= k, pltpu.roll(u, k, 0), 0.0)


def _shift_up(u, k):
    s = u.shape[0]
    rows = lax.broadcasted_iota(jnp.int32, u.shape, 0)
    return jnp.where(rows < s - k, pltpu.roll(u, s - k, 0), 0.0)


def _conv(u, w_ref, b_ref):
    return w_ref[0:1, :] * _shift_down(u, 2) + w_ref[1:2, :] * _shift_down(u, 1) + w_ref[2:3, :] * u + b_ref[...]


STRIP = 256


def _conv_strip(u_ref, t, r0, w_ref, b_ref):
    cur = u_ref[pl.ds(r0, STRIP), :]
    prev = u_ref[pl.ds(pl.multiple_of(jnp.maximum(r0 - 8, 0), 8), 8), :]
    ext = jnp.concatenate([jnp.where(t > 0, prev, 0.0), cur], axis=0)
    u1 = pltpu.roll(ext, 1, 0)[8:]
    u2 = pltpu.roll(ext, 2, 0)[8:]
    return w_ref[0:1, :] * u2 + w_ref[1:2, :] * u1 + w_ref[2:3, :] * cur + b_ref[...], u2, u1, cur


def _fold8(v):
    return functools.reduce(jnp.add, [v[r:r + 8] for r in range(0, STRIP, 8)])


def _conv_glu(u, conv_w, conv_b, name):
    s, f2 = u.shape
    f = f2 // 2
    tc = LANES
    nb = f // tc

    def body(ug_ref, uv_ref, wg_ref, wv_ref, bg_ref, bv_ref, o_ref):
        def strip(t, _):
            r0 = pl.multiple_of(t * STRIP, STRIP)
            g = _conv_strip(ug_ref, t, r0, wg_ref, bg_ref)[0]
            v = _conv_strip(uv_ref, t, r0, wv_ref, bv_ref)[0]
            o_ref[pl.ds(r0, STRIP), :] = (g * jax.nn.sigmoid(g) * v).astype(BF16)
            return 0

        lax.fori_loop(0, s // STRIP, strip, 0)

    col = lambda off: pl.BlockSpec((s, tc), lambda j: (0, j + off))
    wcol = lambda off: pl.BlockSpec((3, tc), lambda j: (0, j + off))
    bcol = lambda off: pl.BlockSpec((1, tc), lambda j: (0, j + off))
    return pl.pallas_call(
        body, name=name, out_shape=jax.ShapeDtypeStruct((s, f), BF16), grid=(nb,),
        in_specs=[col(0), col(nb), wcol(0), wcol(nb), bcol(0), bcol(nb)], out_specs=col(0),
        compiler_params=_params(),
    )(u, u, conv_w, conv_w, conv_b, conv_b)


def _conv_glu_bwd(u, da, conv_w, conv_b, name, rider=None):
    s, f2 = u.shape
    f = f2 // 2
    tc = LANES
    nb = f // tc

    def body(ug_ref, uv_ref, da_ref, wg_ref, wv_ref, bg_ref, bv_ref, du_ref, dw_ref, db_ref, dyg_ref, dyv_ref):
        def strip1(t, acc):
            r0 = pl.multiple_of(t * STRIP, STRIP)
            g, *ug = _conv_strip(ug_ref, t, r0, wg_ref, bg_ref)
            v, *uv = _conv_strip(uv_ref, t, r0, wv_ref, bv_ref)
            da_ = da_ref[pl.ds(r0, STRIP), :]
            sg = jax.nn.sigmoid(g)
            dg = da_ * v * (sg * (1.0 + g * (1.0 - sg)))
            dv = da_ * (g * sg)
            dyg_ref[pl.ds(r0, STRIP), :] = dg
            dyv_ref[pl.ds(r0, STRIP), :] = dv
            new = [_fold8(dy * uu) for dy, us in ((dg, ug), (dv, uv)) for uu in us] + [_fold8(dg), _fold8(dv)]
            return tuple(a + n for a, n in zip(acc, new))

        zero = jnp.zeros((8, LANES), F32)
        acc = lax.fori_loop(0, s // STRIP, strip1, (zero,) * 8)
        for h in range(2):
            for tap in range(3):
                dw_ref[h, tap:tap + 1, :] = jnp.sum(acc[3 * h + tap], axis=0, keepdims=True)
            db_ref[h] = jnp.sum(acc[6 + h], axis=0, keepdims=True)
        dyg_ref[s:, :] = zero
        dyv_ref[s:, :] = zero

        def strip2(t, _):
            r0 = pl.multiple_of(t * STRIP, STRIP)
            for h, (dy_ref, w_ref) in enumerate(((dyg_ref, wg_ref), (dyv_ref, wv_ref))):
                cur = dy_ref[pl.ds(r0, STRIP), :]
                ext = jnp.concatenate([cur, dy_ref[pl.ds(r0 + STRIP, 8), :]], axis=0)
                d1 = pltpu.roll(ext, STRIP + 7, 0)[:STRIP]
                d2 = pltpu.roll(ext, STRIP + 6, 0)[:STRIP]
                du = w_ref[2:3, :] * cur + w_ref[1:2, :] * d1 + w_ref[0:1, :] * d2
                du_ref[h, pl.ds(r0, STRIP), :] = du.astype(BF16)
            return 0

        lax.fori_loop(0, s // STRIP, strip2, 0)

    col = lambda off: pl.BlockSpec((s, tc), lambda j: (0, j + off))
    wcol = lambda off: pl.BlockSpec((3, tc), lambda j: (0, j + off))
    bcol = lambda off: pl.BlockSpec((1, tc), lambda j: (0, j + off))
    return _call_with_rider(
        body, rider, name=name, grid=(nb,),
        out_shape=(jax.ShapeDtypeStruct((2, s, f), BF16), jax.ShapeDtypeStruct((2, 3, f), F32),
                   jax.ShapeDtypeStruct((2, 1, f), F32)),
        in_specs=[col(0), col(nb), col(0), wcol(0), wcol(nb), bcol(0), bcol(nb)],
        out_specs=(pl.BlockSpec((2, s, tc), lambda j: (0, 0, j)), pl.BlockSpec((2, 3, tc), lambda j: (0, 0, j)),
                   pl.BlockSpec((2, 1, tc), lambda j: (0, 0, j))),
        scratch_shapes=[pltpu.VMEM((s + 8, tc), F32), pltpu.VMEM((s + 8, tc), F32)],
        args=(u, u, da, conv_w, conv_w, conv_b, conv_b))


def _accumulate(ref, val):
    @pl.when(pl.program_id(0) == 0)
    def _():
        ref[...] = val

    @pl.when(pl.program_id(0) > 0)
    def _():
        ref[...] += val


def _rms_bwd(n, r, dn):
    return r * (dn - n * jnp.mean(dn * n, axis=-1, keepdims=True))


def _loss_head(x, final_g, target, name):
    s, d = x.shape
    tr = _pick(s, 512, 8)

    def body(x_ref, g_ref, t_ref, loss_ref, dx_ref, dg_ref):
        n, r = _rms(x_ref[...])
        diff = n * g_ref[...] - t_ref[...]
        part = 0.5 * jnp.sum(jnp.sum(diff * diff, axis=1, keepdims=True), axis=0, keepdims=True) / d
        _accumulate(loss_ref, part)
        dy = diff / d
        _accumulate(dg_ref, jnp.sum(dy * n, axis=0, keepdims=True))
        dx_ref[...] = _rms_bwd(n, r, dy * g_ref[...])

    return pl.pallas_call(
        body, name=name, grid=(s // tr,),
        out_shape=(jax.ShapeDtypeStruct((1, 1), F32), jax.ShapeDtypeStruct((s, d), F32), jax.ShapeDtypeStruct((1, d), F32)),
        in_specs=[_row_spec(tr, d), _vec_spec(d), _row_spec(tr, d)],
        out_specs=(pl.BlockSpec((1, 1), lambda i: (0, 0)), _row_spec(tr, d), _vec_spec(d)),
        compiler_params=_params(),
    )(x, final_g, target)


def _gate_bwd(dx, m, gate, name):
    s, d = dx.shape
    tr = _pick(s, 512, 8)

    def body(dx_ref, m_ref, g_ref, dm_ref, dg_ref):
        dxv = dx_ref[...]
        dm_ref[...] = (dxv * g_ref[...]).astype(BF16)
        _accumulate(dg_ref, jnp.sum(dxv * m_ref[...], axis=0, keepdims=True))

    return pl.pallas_call(
        body, name=name, grid=(s // tr,),
        out_shape=(jax.ShapeDtypeStruct((s, d), BF16), jax.ShapeDtypeStruct((1, d), F32)),
        in_specs=[_row_spec(tr, d), _row_spec(tr, d), _vec_spec(d)], out_specs=(_row_spec(tr, d), _vec_spec(d)),
        compiler_params=_params(),
    )(dx, m, gate)


def _norm_mod_bwd(x, dh, dres, scale, name):
    s, d = x.shape
    tr = _pick(s, 512, 8)

    def body(x_ref, dh_ref, dr_ref, sc_ref, dx_ref, dsc_ref, dsh_ref):
        n, r = _rms(x_ref[...])
        dh_ = dh_ref[...]
        _accumulate(dsc_ref, jnp.sum(dh_ * n, axis=0, keepdims=True))
        _accumulate(dsh_ref, jnp.sum(dh_, axis=0, keepdims=True))
        dx_ref[...] = dr_ref[...] + _rms_bwd(n, r, dh_ * (1.0 + sc_ref[...]))

    return pl.pallas_call(
        body, name=name, grid=(s // tr,),
        out_shape=(jax.ShapeDtypeStruct((s, d), F32), jax.ShapeDtypeStruct((1, d), F32), jax.ShapeDtypeStruct((1, d), F32)),
        in_specs=[_row_spec(tr, d), _row_spec(tr, d), _row_spec(tr, d), _vec_spec(d)],
        out_specs=(_row_spec(tr, d), _vec_spec(d), _vec_spec(d)), compiler_params=_params(),
    )(x, dh, dres, scale)


def _out_norm_bwd(oa, ob, dcat, g_a, g_b, name):
    s, w = oa.shape
    tr = _pick(s, 512, 8)

    def body(oa_ref, ob_ref, dc_ref, ga_ref, gb_ref, doa_ref, dob_ref, dga_ref, dgb_ref):
        for o_ref, g_ref, do_ref, dg_ref, lo in ((oa_ref, ga_ref, doa_ref, dga_ref, 0), (ob_ref, gb_ref, dob_ref, dgb_ref, w)):
            n, r = _rms(o_ref[...])
            dc = dc_ref[:, lo:lo + w]
            _accumulate(dg_ref, jnp.sum(dc * n, axis=0, keepdims=True))
            do_ref[...] = _rms_bwd(n, r, dc * g_ref[...])

    return pl.pallas_call(
        body, name=name, grid=(s // tr,),
        out_shape=(jax.ShapeDtypeStruct((s, w), F32), jax.ShapeDtypeStruct((s, w), F32),
                   jax.ShapeDtypeStruct((1, w), F32), jax.ShapeDtypeStruct((1, w), F32)),
        in_specs=[_row_spec(tr, w), _row_spec(tr, w), _row_spec(tr, 2 * w), _vec_spec(w), _vec_spec(w)],
        out_specs=(_row_spec(tr, w), _row_spec(tr, w), _vec_spec(w), _vec_spec(w)), compiler_params=_params(),
    )(oa, ob, dcat, g_a, g_b)


def _head_masks():
    lane = lax.broadcasted_iota(jnp.int32, (1, LANES), 1)
    return lane < HEAD_DIM, lane >= HEAD_DIM


def _nt(a, b):
    return lax.dot_general(a, b, (((1,), (1,)), ((), ())), preferred_element_type=F32)


def _tn(a, b):
    return lax.dot_general(a, b, (((0,), (0,)), ((), ())), preferred_element_type=F32)


def _nn(a, b):
    return jnp.dot(a, b, preferred_element_type=F32)


def _only(mask, v):
    return jnp.where(mask, v, jnp.zeros_like(v))


def _fill_padded(dst_ref, src_ref):
    dst_ref[0:PAD, :] = jnp.zeros((PAD, LANES), dst_ref.dtype)
    dst_ref[PAD:, :] = src_ref[...]


def _chunk_probs(s, bias, chunk):
    pos = lax.broadcasted_iota(jnp.int32, (1, BAND), 1)
    s = jnp.where(pos >= (N_PREV - chunk) * CHUNK, s + bias, -1e30)
    e = jnp.exp(s - jnp.max(s, axis=1, keepdims=True))
    return e / jnp.sum(e, axis=1, keepdims=True)


def _band_windows(i, cq, kpad, vpad):
    chunks = [i * cq + cc for cc in range(cq)]
    starts = [pl.multiple_of(ch * CHUNK, CHUNK) for ch in chunks]
    return chunks, starts, [kpad[pl.ds(st, BAND), :] for st in starts], [vpad[pl.ds(st, BAND), :] for st in starts]


def _attn_a_fwd(proj, band_bias, name):
    s = proj.shape[0]
    cq = 4
    tq = cq * CHUNK
    npair = N_HEADS // 2
    kcol, vcol = W_GRP // LANES, 2 * W_GRP // LANES

    def body(q_ref, k_ref, v_ref, b_ref, o_ref, kpad, vpad):
        i = pl.program_id(1)
        masks = _head_masks()

        @pl.when(i == 0)
        def _():
            _fill_padded(kpad, k_ref)
            _fill_padded(vpad, v_ref)

        chunks, _, kbs, vbs = _band_windows(i, cq, kpad, vpad)
        q2 = q_ref[...] * (HEAD_DIM ** -0.5)
        units = [(cc, h) for cc in range(cq) for h in range(2)]
        ss = [_nt(_only(masks[h], q2[cc * CHUNK:(cc + 1) * CHUNK]), kbs[cc]) for cc, h in units]
        ps = [_chunk_probs(s_, b_ref[h], chunks[cc]).astype(BF16) for s_, (cc, h) in zip(ss, units)]
        for cc in range(cq):
            o_ref[cc * CHUNK:(cc + 1) * CHUNK, :] = (_nn(ps[2 * cc], _only(masks[0], vbs[cc]))
                                                     + _nn(ps[2 * cc + 1], _only(masks[1], vbs[cc])))

    return pl.pallas_call(
        body, name=name, out_shape=jax.ShapeDtypeStruct((s, W_GRP), F32), grid=(npair, s // tq),
        in_specs=[pl.BlockSpec((tq, LANES), lambda p, i: (i, p)),
                  pl.BlockSpec((s, LANES), lambda p, i: (0, kcol + p)),
                  pl.BlockSpec((s, LANES), lambda p, i: (0, vcol + p)),
                  pl.BlockSpec((2, CHUNK, BAND), lambda p, i: (p, 0, 0))],
        out_specs=pl.BlockSpec((tq, LANES), lambda p, i: (i, p)),
        scratch_shapes=[pltpu.VMEM((s + PAD, LANES), BF16), pltpu.VMEM((s + PAD, LANES), BF16)],
        compiler_params=_params(),
    )(proj, proj, proj, band_bias)


def _attn_a_bwd(proj, band_bias, doa, name, rider=None):
    s = proj.shape[0]
    cq = 4
    tq = cq * CHUNK
    nq = s // tq
    npair = N_HEADS // 2
    kcol, vcol = W_GRP // LANES, 2 * W_GRP // LANES
    scale = HEAD_DIM ** -0.5

    def body(q_ref, k_ref, v_ref, b_ref, do_ref, dq_ref, dk_ref, dv_ref, db_ref, kpad, vpad, dkpad, dvpad):
        i = pl.program_id(1)
        masks = _head_masks()

        @pl.when(i == 0)
        def _():
            _fill_padded(kpad, k_ref)
            _fill_padded(vpad, v_ref)
            dkpad[...] = jnp.zeros_like(dkpad)
            dvpad[...] = jnp.zeros_like(dvpad)
            db_ref[...] = jnp.zeros_like(db_ref)

        chunks, starts, kbs, vbs = _band_windows(i, cq, kpad, vpad)
        q2 = q_ref[...] * scale
        do2 = do_ref[...].astype(BF16)
        units = [(cc, h) for cc in range(cq) for h in range(2)]
        qhs = [_only(masks[h], q2[cc * CHUNK:(cc + 1) * CHUNK]) for cc, h in units]
        dohs = [_only(masks[h], do2[cc * CHUNK:(cc + 1) * CHUNK]) for cc, h in units]
        ss = [_nt(qh, kbs[cc]) for qh, (cc, h) in zip(qhs, units)]
        dps = [_nt(doh, vbs[cc]) for doh, (cc, h) in zip(dohs, units)]
        ps = [_chunk_probs(s_, b_ref[h], chunks[cc]) for s_, (cc, h) in zip(ss, units)]
        dss = [p * (dp - jnp.sum(p * dp, axis=1, keepdims=True)) for p, dp in zip(ps, dps)]
        for h in range(2):
            db_ref[h] += functools.reduce(jnp.add, [dss[2 * cc + h] for cc in range(cq)])
        for cc in range(cq):
            u0, u1 = 2 * cc, 2 * cc + 1
            dsb = [dss[u0].astype(BF16), dss[u1].astype(BF16)]
            dq = _nn(dsb[0], _only(masks[0], kbs[cc])) + _nn(dsb[1], _only(masks[1], kbs[cc]))
            dq_ref[cc * CHUNK:(cc + 1) * CHUNK, :] = dq * scale
            dkpad[pl.ds(starts[cc], BAND), :] += _tn(jnp.concatenate(dsb, axis=0), jnp.concatenate([qhs[u0], qhs[u1]], axis=0))
            dvpad[pl.ds(starts[cc], BAND), :] += _tn(jnp.concatenate([ps[u0].astype(BF16), ps[u1].astype(BF16)], axis=0),
                                                     jnp.concatenate([dohs[u0], dohs[u1]], axis=0))

        @pl.when(i == nq - 1)
        def _():
            dk_ref[...] = dkpad[PAD:, :]
            dv_ref[...] = dvpad[PAD:, :]

    blk = pl.BlockSpec((tq, LANES), lambda p, i: (i, p))
    whole = pl.BlockSpec((s, LANES), lambda p, i: (0, p))
    bias_spec = pl.BlockSpec((2, CHUNK, BAND), lambda p, i: (p, 0, 0))
    return _call_with_rider(
        body, rider, name=name, grid=(npair, nq),
        out_shape=(jax.ShapeDtypeStruct((s, W_GRP), F32),) * 3 + (jax.ShapeDtypeStruct((N_HEADS, CHUNK, BAND), F32),),
        in_specs=[blk, pl.BlockSpec((s, LANES), lambda p, i: (0, kcol + p)),
                  pl.BlockSpec((s, LANES), lambda p, i: (0, vcol + p)), bias_spec, blk],
        out_specs=(blk, whole, whole, bias_spec),
        scratch_shapes=[pltpu.VMEM((s + PAD, LANES), BF16), pltpu.VMEM((s + PAD, LANES), BF16),
                        pltpu.VMEM((s + PAD, LANES), F32), pltpu.VMEM((s + PAD, LANES), F32)],
        args=(proj, proj, proj, band_bias, doa))


def _split3(v):
    hi = v.astype(BF16)
    r1 = v - hi.astype(F32)
    mid = r1.astype(BF16)
    lo = (r1 - mid.astype(F32)).astype(BF16)
    return hi, mid, lo


def _rel_bias_grad(dband_t, name):
    width = 3 * LANES

    def body(t_ref, o_ref):
        pos = lax.broadcasted_iota(jnp.int32, (BAND, width), 0)
        col = lax.broadcasted_iota(jnp.int32, (BAND, width), 1)
        acc = jnp.zeros((N_HEADS, width), F32)
        for q in range(CHUNK):
            idx = jnp.minimum(PAD + q - pos, REL_CLIP) + REL_CLIP
            onehot = (col == idx).astype(BF16)
            for part in _split3(t_ref[q]):
                acc = acc + _nn(part, onehot)
        o_ref[...] = acc

    return pl.pallas_call(
        body, name=name, out_shape=jax.ShapeDtypeStruct((N_HEADS, width), F32),
        in_specs=[VMEM_FULL], out_specs=VMEM_FULL, compiler_params=_params(),
    )(dband_t)


def _split2_wide(v):
    hi = v.astype(BF16)
    return jnp.concatenate([hi, (v - hi.astype(F32)).astype(BF16)], axis=1)


def _sb_logs(z, lower):
    e = jnp.exp(-jnp.abs(z))
    lb = jnp.minimum(z, 0.0) - jnp.log(1.0 + e)
    lk = lb - z
    if lower is not None:
        lk = jnp.where(lower, lk, 0.0)
    return z, e, lb, lk


def _tri_masks(tq):
    row = lax.broadcasted_iota(jnp.int32, (tq, tq), 0)
    col = lax.broadcasted_iota(jnp.int32, (tq, tq), 1)
    return row, col


def _stack2(m):
    return jnp.concatenate([m, m], axis=0).astype(BF16)


def _sb_fwd(proj, name, rider=None):
    s = proj.shape[0]
    tq = _pick(s, 256)
    nq = s // tq
    npair = N_HEADS // 2
    qcol, kcol, vcol = 3 * W_GRP // LANES, 4 * W_GRP // LANES, 5 * W_GRP // LANES

    assert nq % 2 == 0

    def body(q_ref, k_ref, v_ref, o_ref, l_ref):
        i = pl.program_id(1)
        masks = _head_masks()
        q2 = q_ref[...] * (HEAD_DIM ** -0.5)
        qs = [[_only(m, q2[c * tq:(c + 1) * tq]) for m in masks] for c in range(2)]
        row, col = _tri_masks(tq)
        lower = row > col
        after2 = _stack2(lower)

        def tile(kblock, chains, carry):
            accs, tails = [list(t) for t in carry[0]], [list(t) for t in carry[1]]
            ks = pl.multiple_of(kblock * tq, tq)
            kb = k_ref[pl.ds(ks, tq), :]
            vb = v_ref[pl.ds(ks, tq), :]
            units = [(c, h, diag) for c, diag in chains for h in range(2)]
            zs = [_nt(qs[c][h], kb) for c, h, _ in units]
            vh = [_only(masks[h], vb) for h in range(2)]
            lbs, lks, locs = [], [], []
            for z, (c, h, diag) in zip(zs, units):
                lb, lk = _sb_logs(z, lower if diag else None)[2:]
                lbs.append(lb)
                lks.append(lk)
                locs.append(_nn(_split2_wide(lk), after2))
            for lb, lk, loc, (c, h, diag) in zip(lbs, lks, locs, units):
                a = jnp.exp(lb + (loc + tails[c][h]))
                if diag:
                    a = jnp.where(lower, a, 0.0)
                accs[c][0] = accs[c][0] + _nn(a.astype(BF16), vh[h])
                tails[c][h] = tails[c][h] + (loc[:, 0:1] + lk[:, 0:1])
            return tuple(tuple(t) for t in accs), tuple(tuple(t) for t in tails)

        zero = jnp.zeros((tq, 1), F32)
        acc0 = jnp.zeros((tq, LANES), F32)
        carry = (((acc0,), (acc0,)), ((zero, zero), (zero, zero)))
        carry = tile(2 * i + 1, [(1, True)], carry)
        carry = tile(2 * i, [(0, True), (1, False)], carry)

        def alive(tails):
            return functools.reduce(jnp.maximum, [jnp.max(t) for ts in tails for t in ts]) > SB_DEAD

        def walk(state):
            jj, _, cr = state
            cr = tile(2 * i - jj, [(0, False), (1, False)], cr)
            return jj + 1, alive(cr[1]), cr

        jj, _, (accs, tails) = lax.while_loop(lambda st: jnp.logical_and(st[0] <= 2 * i, st[1]), walk,
                                              (jnp.int32(1), alive(carry[1]), carry))
        for c in range(2):
            o_ref[c * tq:(c + 1) * tq, :] = accs[c][0]
            l_ref[c * tq:(c + 1) * tq, 0:1] = tails[c][0]
            l_ref[c * tq:(c + 1) * tq, 1:2] = tails[c][1]
        l_ref[:, 2:3] = jnp.full((2 * tq, 1), (jj - 1).astype(F32))

    return _call_with_rider(
        body, rider, name=name, grid=(npair, nq // 2),
        out_shape=(jax.ShapeDtypeStruct((s, W_GRP), F32), jax.ShapeDtypeStruct((npair, s, 3), F32)),
        in_specs=[pl.BlockSpec((2 * tq, LANES), lambda p, i: (i, qcol + p)),
                  pl.BlockSpec((s, LANES), lambda p, i: (0, kcol + p)),
                  pl.BlockSpec((s, LANES), lambda p, i: (0, vcol + p))],
        out_specs=(pl.BlockSpec((2 * tq, LANES), lambda p, i: (i, p)),
                   pl.BlockSpec((None, 2 * tq, 3), lambda p, i: (p, i, 0))),
        scratch_shapes=[], args=(proj, proj, proj))


def _sb_bwd(proj, ltot, dob, name, rider=None):
    s = proj.shape[0]
    tq = _pick(s, 256)
    nq = s // tq
    npair = N_HEADS // 2
    qcol, kcol, vcol = 3 * W_GRP // LANES, 4 * W_GRP // LANES, 5 * W_GRP // LANES
    scale = HEAD_DIM ** -0.5

    def body(q_ref, k_ref, v_ref, l_ref, do_ref, dq_ref, dk_ref, dv_ref):
        i = pl.program_id(1)
        masks = _head_masks()

        @pl.when(i == 0)
        def _():
            dk_ref[...] = jnp.zeros_like(dk_ref)
            dv_ref[...] = jnp.zeros_like(dv_ref)

        q2 = q_ref[...] * scale
        do2 = do_ref[...]
        part = lambda v, c: v[c * tq:(c + 1) * tq]
        qs = [[_only(m, part(q2, c)) for m in masks] for c in range(2)]
        doh = [[_only(m, part(do2, c)).astype(BF16) for m in masks] for c in range(2)]
        ltots = [[l_ref[c * tq:(c + 1) * tq, h:h + 1] for h in range(2)] for c in range(2)]
        row, col = _tri_masks(tq)
        lower = row > col
        upto2 = _stack2(row <= col)
        before = (row < col).astype(BF16)

        def tile(kblock, chains, carry):
            dqs, heads, gsums = [[list(t) for t in part_] for part_ in carry]
            ks = pl.multiple_of(kblock * tq, tq)
            kb = k_ref[pl.ds(ks, tq), :]
            vb = v_ref[pl.ds(ks, tq), :]
            units = [(c, h, diag) for c, diag in chains for h in range(2)]
            zs = [_nt(qs[c][h], kb) for c, h, _ in units]
            das = [_nt(doh[c][h], vb) for c, h, _ in units]
            kh = [_only(masks[h], kb) for h in range(2)]
            sigs, lbs, locs = [], [], []
            for z_, (c, h, diag) in zip(zs, units):
                z, e, lb, lk = _sb_logs(z_, lower if diag else None)
                locs.append(_nn(_split2_wide(lk), upto2))
                r = 1.0 / (1.0 + e)
                sigs.append(jnp.where(z >= 0, r, e * r))
                lbs.append(lb)
            a_s, gs, glocs = [], [], []
            for lb, loc, da, (c, h, diag) in zip(lbs, locs, das, units):
                a = jnp.exp(lb + (ltots[c][h] - (heads[c][h] + loc)))
                if diag:
                    a = jnp.where(lower, a, 0.0)
                g = a * da
                glocs.append(_nn(g.astype(BF16), before))
                a_s.append(a.astype(BF16))
                gs.append(g)
            dzbs = []
            for g, sig, loc, gloc, (c, h, diag) in zip(gs, sigs, locs, glocs, units):
                dz = g - sig * (g + (gsums[c][h] + gloc))
                if diag:
                    dz = jnp.where(lower, dz, 0.0)
                dzb = dz.astype(BF16)
                dzbs.append(dzb)
                dqs[c][0] = dqs[c][0] + _nn(dzb, kh[h])
                heads[c][h] = heads[c][h] + loc[:, tq - 1:tq]
                gsums[c][h] = gsums[c][h] + (gloc[:, tq - 1:tq] + g[:, tq - 1:tq])
            stack = lambda vs: vs[0] if len(vs) == 1 else jnp.concatenate(vs, axis=0)
            dk_ref[pl.ds(ks, tq), :] += _tn(stack(dzbs), stack([qs[c][h] for c, h, _ in units]))
            dv_ref[pl.ds(ks, tq), :] += _tn(stack(a_s), stack([doh[c][h] for c, h, _ in units]))
            return tuple(tuple(tuple(t) for t in part_) for part_ in (dqs, heads, gsums))

        zero = jnp.zeros((tq, 1), F32)
        acc0 = jnp.zeros((tq, LANES), F32)
        carry = (((acc0,), (acc0,)), ((zero, zero), (zero, zero)), ((zero, zero), (zero, zero)))
        walked = jnp.clip(jnp.max(l_ref[0:8, 2:3]).astype(jnp.int32), 0, 2 * i)
        carry = lax.fori_loop(2 * i - walked, 2 * i, lambda j, cr: tile(j, [(0, False), (1, False)], cr), carry)
        carry = tile(2 * i, [(0, True), (1, False)], carry)
        dqs, _, _ = tile(2 * i + 1, [(1, True)], carry)
        for c in range(2):
            dq_ref[c * tq:(c + 1) * tq, :] = dqs[c][0] * scale

    blk = pl.BlockSpec((2 * tq, LANES), lambda p, i: (i, p))
    whole = pl.BlockSpec((s, LANES), lambda p, i: (0, p))
    return _call_with_rider(
        body, rider, name=name, grid=(npair, nq // 2), out_shape=(jax.ShapeDtypeStruct((s, W_GRP), F32),) * 3,
        in_specs=[pl.BlockSpec((2 * tq, LANES), lambda p, i: (i, qcol + p)),
                  pl.BlockSpec((s, LANES), lambda p, i: (0, kcol + p)),
                  pl.BlockSpec((s, LANES), lambda p, i: (0, vcol + p)),
                  pl.BlockSpec((None, 2 * tq, 3), lambda p, i: (p, i, 0)), blk],
        out_specs=(blk, whole, whole), scratch_shapes=[], args=(proj, proj, proj, ltot, dob))


def _ada_fwd(c_all, w_ada, b_ada, name):
    nl, d, n = w_ada.shape
    tn = _pick(n, 512)

    def body(c_ref, w_ref, b_ref, o_ref):
        cv = c_ref[...]
        act = (cv * jax.nn.sigmoid(cv)).astype(BF16)
        o_ref[...] = _nn(act, w_ref[...].astype(BF16)) + b_ref[...]

    return pl.pallas_call(
        body, name=name, out_shape=jax.ShapeDtypeStruct((nl, N_DEV, n), F32), grid=(nl, n // tn),
        in_specs=[pl.BlockSpec((N_DEV, d), lambda l, j: (0, 0)), pl.BlockSpec((None, d, tn), lambda l, j: (l, 0, j)),
                  pl.BlockSpec((None, 1, tn), lambda l, j: (l, 0, j))],
        out_specs=pl.BlockSpec((None, N_DEV, tn), lambda l, j: (l, 0, j)), compiler_params=_params(),
    )(c_all, w_ada, b_ada)


def _ada_bwd(c_all, dmod, name):
    nl, _, n = dmod.shape
    d = c_all.shape[1]
    tn = _pick(n, 512)

    def body(c_ref, g_ref, o_ref):
        cv = c_ref[...]
        act = (cv * jax.nn.sigmoid(cv)).astype(BF16)
        o_ref[...] = _tn(act, g_ref[...].astype(BF16))

    return pl.pallas_call(
        body, name=name, out_shape=jax.ShapeDtypeStruct((nl, d, n), F32), grid=(nl, n // tn),
        in_specs=[pl.BlockSpec((N_DEV, d), lambda l, j: (0, 0)), pl.BlockSpec((None, N_DEV, tn), lambda l, j: (l, 0, j))],
        out_specs=pl.BlockSpec((None, d, tn), lambda l, j: (l, 0, j)), compiler_params=_params(),
    )(c_all, dmod)


def _adamw(g, w, m, v, name):
    r, c = g.shape
    tr = _pick(r, 512, 8)
    c1 = 1.0 - ADAM_B1 ** ADAM_STEP
    c2 = 1.0 - ADAM_B2 ** ADAM_STEP

    def body(g_ref, w_ref, m_ref, v_ref, d_ref, nm_ref, nv_ref):
        gv = g_ref[...]
        nm = ADAM_B1 * m_ref[...] + (1.0 - ADAM_B1) * gv
        nv = ADAM_B2 * v_ref[...] + (1.0 - ADAM_B2) * (gv * gv)
        d_ref[...] = -ADAM_LR * ((nm / c1) / (jnp.sqrt(nv / c2) + ADAM_EPS) + ADAM_WD * w_ref[...])
        nm_ref[...] = nm
        nv_ref[...] = nv

    spec = pl.BlockSpec((tr, c), lambda i: (i, 0))
    return pl.pallas_call(
        body, name=name, out_shape=(jax.ShapeDtypeStruct((r, c), F32),) * 3, grid=(r // tr,),
        in_specs=[spec] * 4, out_specs=(spec,) * 3, compiler_params=_params(),
    )(g, w, m, v)


def _adamw_nd(g, w, m, v, name):
    shape = w.shape
    two_d = (1, shape[0]) if len(shape) == 1 else (-1, shape[-1])
    outs = _adamw(*(t.reshape(two_d) for t in (g, w, m, v)), name=name)
    return tuple(o.reshape(shape) for o in outs)


def _allgather8(v, name):
    m, n = v.shape

    def body(v_ref, out_ref, send_sems, recv_sems, local_sem):
        x, y, c = _my_place()

        def rows(px, py, pc):
            return out_ref.at[pl.ds(pl.multiple_of((4 * px + 2 * py + pc) * m, 8), m), :]

        def peer(k):
            return _flip(x, k & 4), _flip(y, k & 2), _flip(c, k & 1)

        def copy(k, block):
            return pltpu.make_async_remote_copy(
                src_ref=v_ref, dst_ref=rows(*block), send_sem=send_sems.at[k - 1], recv_sem=recv_sems.at[k - 1],
                device_id=peer(k), device_id_type=MESH)

        mine = pltpu.make_async_copy(v_ref, rows(x, y, c), local_sem)
        mine.start()
        sends = [copy(k, (x, y, c)) for k in range(1, N_DEV)]
        for cp in sends:
            cp.start()
        for k in range(1, N_DEV):
            copy(k, peer(k)).wait_recv()
        for cp in sends:
            cp.wait_send()
        mine.wait()

    return pl.pallas_call(
        body, name=name, out_shape=jax.ShapeDtypeStruct((N_DEV * m, n), v.dtype),
        in_specs=[VMEM_FULL], out_specs=VMEM_FULL,
        scratch_shapes=[pltpu.SemaphoreType.DMA((N_DEV - 1,)), pltpu.SemaphoreType.DMA((N_DEV - 1,)),
                        pltpu.SemaphoreType.DMA],
        compiler_params=_params(),
    )(v)


def _chip_peers(x, y, c):
    out = []
    for k in range(1, N_CHIPS):
        px, py = _flip(x, k & 2), _flip(y, k & 1)
        out.append((2 * px + py, (px, py, c)))
    return out


def _gather_weights(shards, kinds, name):
    nw = len(shards)

    def full_shape(a, kind):
        l, r, n = a.shape
        return (l, r, N_CHIPS * n) if kind == "col" else (l, N_CHIPS * r, n)

    def body(*refs):
        ins, outs = refs[:nw], refs[nw:2 * nw]
        send_sems, recv_sems, local_sems = refs[2 * nw:]
        x, y, c = _my_place()
        chip = 2 * x + y

        def window(w, j):
            _, r, n = shards[w].shape
            if kinds[w] == "col":
                return outs[w].at[:, :, pl.ds(pl.multiple_of(j * n, LANES), n)]
            return outs[w].at[:, pl.ds(pl.multiple_of(j * r, 16), r), :]

        def copy(w, k, j, peer):
            return pltpu.make_async_remote_copy(
                src_ref=ins[w], dst_ref=window(w, j), send_sem=send_sems.at[3 * w + k], recv_sem=recv_sems.at[3 * w + k],
                device_id=peer, device_id_type=MESH)

        local = [pltpu.make_async_copy(ins[w], window(w, chip), local_sems.at[w]) for w in range(nw)]
        for cp in local:
            cp.start()
        peers = _chip_peers(x, y, c)
        sends = [copy(w, k, chip, peer) for w in range(nw) for k, (_, peer) in enumerate(peers)]
        for cp in sends:
            cp.start()
        for w in range(nw):
            for k, (pchip, peer) in enumerate(peers):
                copy(w, k, pchip, peer).wait_recv()
        for cp in sends:
            cp.wait_send()
        for cp in local:
            cp.wait()

    return pl.pallas_call(
        body, name=name,
        out_shape=tuple(jax.ShapeDtypeStruct(full_shape(a, kd), a.dtype) for a, kd in zip(shards, kinds)),
        in_specs=[ANY] * nw, out_specs=(ANY,) * nw,
        scratch_shapes=[pltpu.SemaphoreType.DMA((3 * nw,)), pltpu.SemaphoreType.DMA((3 * nw,)),
                        pltpu.SemaphoreType.DMA((nw,))],
        compiler_params=_params(),
    )(*shards)


def _rs_to_sibling(grads, name):
    nw = len(grads)

    def body(*refs):
        ins, outs = refs[:nw], refs[nw:2 * nw]
        send_sems, recv_sems = refs[2 * nw:]
        x, y, c = _my_place()
        sibling = (x, y, 1 - c)
        copies = [pltpu.make_async_remote_copy(
            src_ref=ins[w].at[j, 1 - c], dst_ref=outs[w].at[j], send_sem=send_sems.at[N_CHIPS * w + j],
            recv_sem=recv_sems.at[N_CHIPS * w + j], device_id=sibling, device_id_type=MESH)
            for w in range(nw) for j in range(N_CHIPS)]
        for cp in copies:
            cp.start()
        for cp in copies:
            cp.wait_recv()
        for cp in copies:
            cp.wait_send()

    return pl.pallas_call(
        body, name=name,
        out_shape=tuple(jax.ShapeDtypeStruct((N_CHIPS,) + g.shape[2:], g.dtype) for g in grads),
        in_specs=[ANY] * nw, out_specs=(ANY,) * nw,
        scratch_shapes=[pltpu.SemaphoreType.DMA((N_CHIPS * nw,)), pltpu.SemaphoreType.DMA((N_CHIPS * nw,))],
        compiler_params=_params(),
    )(*grads)


def _rs_to_chips(parts, name):
    nw = len(parts)

    def body(*refs):
        ins, outs = refs[:nw], refs[nw:2 * nw]
        send_sems, recv_sems, local_sems = refs[2 * nw:]
        x, y, c = _my_place()
        chip = 2 * x + y
        peers = _chip_peers(x, y, c)

        def copy(w, k, src_slab, dst_slab, peer):
            return pltpu.make_async_remote_copy(
                src_ref=ins[w].at[src_slab], dst_ref=outs[w].at[dst_slab], send_sem=send_sems.at[3 * w + k],
                recv_sem=recv_sems.at[3 * w + k], device_id=peer, device_id_type=MESH)

        local = [pltpu.make_async_copy(ins[w].at[chip], outs[w].at[chip], local_sems.at[w]) for w in range(nw)]
        for cp in local:
            cp.start()
        sends = [copy(w, k, pchip, chip, peer) for w in range(nw) for k, (pchip, peer) in enumerate(peers)]
        for cp in sends:
            cp.start()
        for w in range(nw):
            for k, (pchip, peer) in enumerate(peers):
                copy(w, k, chip, pchip, peer).wait_recv()
        for cp in sends:
            cp.wait_send()
        for cp in local:
            cp.wait()

    return pl.pallas_call(
        body, name=name, out_shape=tuple(jax.ShapeDtypeStruct(p.shape, p.dtype) for p in parts),
        in_specs=[ANY] * nw, out_specs=(ANY,) * nw,
        scratch_shapes=[pltpu.SemaphoreType.DMA((3 * nw,)), pltpu.SemaphoreType.DMA((3 * nw,)),
                        pltpu.SemaphoreType.DMA((nw,))],
        compiler_params=_params(),
    )(*parts)


def _rs_share_halves(halves, name):
    nw = len(halves)
    nl = len(halves[0])
    flat = [h for hs in halves for h in hs]

    def body(*refs):
        ins, outs = refs[:nw * nl], refs[nw * nl:nw * nl + nw]
        send_sems, recv_sems, local_sems = refs[nw * nl + nw:]
        x, y, c = _my_place()
        sibling = (x, y, 1 - c)
        local, sends, recvs = [], [], []
        for w in range(nw):
            for l in range(nl):
                n = nl * w + l
                local.append(pltpu.make_async_copy(ins[n], outs[w].at[l, c], local_sems.at[n]))
                sends.append(pltpu.make_async_remote_copy(
                    src_ref=ins[n], dst_ref=outs[w].at[l, c], send_sem=send_sems.at[n], recv_sem=recv_sems.at[n],
                    device_id=sibling, device_id_type=MESH))
                recvs.append(pltpu.make_async_remote_copy(
                    src_ref=ins[n], dst_ref=outs[w].at[l, 1 - c], send_sem=send_sems.at[n], recv_sem=recv_sems.at[n],
                    device_id=sibling, device_id_type=MESH))
        for cp in local + sends:
            cp.start()
        for cp in recvs:
            cp.wait_recv()
        for cp in sends:
            cp.wait_send()
        for cp in local:
            cp.wait()

    return pl.pallas_call(
        body, name=name,
        out_shape=tuple(jax.ShapeDtypeStruct((nl, 2) + hs[0].shape, hs[0].dtype) for hs in halves),
        in_specs=[ANY] * (nw * nl), out_specs=(ANY,) * nw,
        scratch_shapes=[pltpu.SemaphoreType.DMA((nw * nl,)), pltpu.SemaphoreType.DMA((nw * nl,)),
                        pltpu.SemaphoreType.DMA((nw * nl,))],
        compiler_params=_params(),
    )(*flat)


def _add_own_half(grad, got, c_idx, name):
    _, _, r, n = grad.shape
    tr = _pick(r, 256, 8)

    def body(c_ref, g_ref, t_ref, o_ref):
        o_ref[...] = g_ref[...] + t_ref[...]

    return pl.pallas_call(
        body, name=name, out_shape=jax.ShapeDtypeStruct((N_CHIPS, r, n), F32),
        grid_spec=pltpu.PrefetchScalarGridSpec(
            num_scalar_prefetch=1, grid=(N_CHIPS, r // tr),
            in_specs=[pl.BlockSpec((None, None, tr, n), lambda j, i, c_ref: (j, c_ref[0], i, 0)),
                      pl.BlockSpec((None, tr, n), lambda j, i, c_ref: (j, i, 0))],
            out_specs=pl.BlockSpec((None, tr, n), lambda j, i, c_ref: (j, i, 0))),
        compiler_params=_params(),
    )(c_idx, grad, got)


def _sum_slabs(slabs, name):
    ns, r, n = slabs.shape
    tr = _pick(r, 256, 8)

    def body(s_ref, o_ref):
        acc = s_ref[0]
        for j in range(1, ns):
            acc = acc + s_ref[j]
        o_ref[...] = acc

    return pl.pallas_call(
        body, name=name, out_shape=jax.ShapeDtypeStruct((r, n), F32), grid=(r // tr,),
        in_specs=[pl.BlockSpec((ns, tr, n), lambda i: (0, i, 0))], out_specs=pl.BlockSpec((tr, n), lambda i: (i, 0)),
        compiler_params=_params(),
    )(slabs)


def _band_bias(rel_bias):
    h = rel_bias.shape[0]
    n_far = PAD - REL_CLIP + CHUNK
    far = jnp.broadcast_to(rel_bias[:, N_REL - 1:N_REL], (h, n_far))
    near = rel_bias[:, REL_CLIP - CHUNK + 1:N_REL - 1][:, ::-1]
    line = jnp.concatenate([far, near], axis=1)
    return jnp.stack([line[:, CHUNK - 1 - q:CHUNK - 1 - q + BAND] for q in range(CHUNK)], axis=1)


def _pack_rows(pieces):
    flat = jnp.concatenate([p.reshape(-1) for p in pieces])
    rows = -(-flat.shape[0] // (8 * LANES)) * 8
    return jnp.pad(flat, (0, rows * LANES - flat.shape[0])).reshape(rows, LANES)


def _unpack_rows(packed, shapes):
    flat = packed.reshape(-1)
    out, at = [], 0
    for shp in shapes:
        size = 1
        for n in shp:
            size *= n
        out.append(flat[at:at + size].reshape(shp))
        at += size
    return out


def _layer_fwd(x, mod, w, band, tag):
    s, d = x.shape
    row = lambda i: mod[i:i + 1]
    h1 = _norm_mod(x, row(1), row(0), f"norm_mix{tag}")
    proj = _matmul(h1, w["w_in"], form="nn", out_dtype=BF16, tm=_pick(s, 512), tn=_pick(w["w_in"].shape[1], 768),
                   tk=d, name=f"proj{tag}")
    oa = _attn_a_fwd(proj, band, f"attn_a{tag}")
    ob, ltot = _sb_fwd(proj, f"attn_b{tag}")
    cat = _out_norm(oa, ob, w["g_a"], w["g_b"], f"out_norm{tag}")
    mixed = _matmul(cat, w["w_out"], form="nn", out_dtype=F32, tm=_pick(s, 512), tn=_pick(d, 1024),
                    tk=cat.shape[1], name=f"mix_out{tag}")
    x1 = _residual(x, row(2), mixed, f"res_mix{tag}")
    h2 = _norm_mod(x1, row(4), row(3), f"norm_ffn{tag}")
    f2 = w["w_up"].shape[1]
    u = _matmul(h2, w["w_up"], form="nn", out_dtype=F32, tm=_pick(s, 512), tn=_pick(f2, 1408), tk=d, name=f"up{tag}",
                rider=up_rider)
    if up_rider is not None:
        u, arrived = u
        on_up_arrival(arrived)
    a = _conv_glu(u, w["conv_w"], w["conv_b"], f"conv_glu{tag}")
    f = _matmul(a, w["w_down"], form="nn", out_dtype=F32, tm=_pick(s, 512), tn=_pick(d, 1024),
                tk=_pick(f2 // 2, 2816), name=f"down{tag}")
    x2 = _residual(x1, row(5), f, f"res_ffn{tag}")
    saved = dict(x=x, h1=h1, proj=proj, oa=oa, ob=ob, ltot=ltot, cat=cat, mixed=mixed, x1=x1, h2=h2, u=u, a=a, f=f)
    return x2, saved


def _layer_bwd(dx2, sv, mod, w, band, tag):
    s, d = dx2.shape
    row = lambda i: mod[i:i + 1]
    f2 = w["w_up"].shape[1]
    ff = f2 // 2
    n_in = w["w_in"].shape[1]
    df, dgate_ffn = _gate_bwd(dx2, sv["f"], row(5), f"gate_ffn_bwd{tag}")
    da = _matmul(df, w["w_down"], form="nt", out_dtype=F32, tm=_pick(s, 512), tn=_pick(ff, 1408), tk=d, name=f"down_dx{tag}")
    g_down = _matmul(sv["a"], df, form="tn", out_dtype=F32, tm=_pick(ff, 1408), tn=_pick(d, 512), tk=_pick(s, 2048),
                     name=f"down_dw{tag}")
    if waiting is not None:
        older_from_sib = _run_rider(_sibling_rider(list(waiting.values())), f"rs_sibling_older{tag}")
    du2, dcw, dcb = _conv_glu_bwd(sv["u"], da, w["conv_w"], w["conv_b"], f"conv_glu_bwd{tag}")
    dh2 = _matmul(du2, w["w_up"], form="nt", out_dtype=F32, tm=_pick(s, 512), tn=_pick(d, 1024), tk=_pick(ff, 2816),
                  name=f"up_dx{tag}", halves=True)
    g_up = _matmul(sv["h2"], du2, form="tn", out_dtype=F32, tm=_pick(d, 512), tn=_pick(f2 // N_CHIPS, 1408),
                   tk=_pick(s, 2048), name=f"up_dw{tag}", shard_cols=f2 // N_CHIPS, halves=True)
    dx1, dscale_ffn, dshift_ffn = _norm_mod_bwd(sv["x1"], dh2, dx2, row(4), f"norm_ffn_bwd{tag}")
    dmixed, dgate_mix = _gate_bwd(dx1, sv["mixed"], row(2), f"gate_mix_bwd{tag}")
    dcat = _matmul(dmixed, w["w_out"], form="nt", out_dtype=F32, tm=_pick(s, 512), tn=_pick(2 * W_GRP, 1024), tk=d,
                   name=f"mix_out_dx{tag}")
    g_out = _matmul(sv["cat"], dmixed, form="tn", out_dtype=F32, tm=_pick(2 * W_GRP, 512), tn=_pick(d, 1024),
                    tk=_pick(s, 2048), name=f"mix_out_dw{tag}")
    doa, dob, dg_a, dg_b = _out_norm_bwd(sv["oa"], sv["ob"], dcat, w["g_a"], w["g_b"], f"out_norm_bwd{tag}")
    dqa, dka, dva, dband = _attn_a_bwd(sv["proj"], band, doa, f"attn_a_bwd{tag}")
    dqb, dkb, dvb = _sb_bwd(sv["proj"], sv["ltot"], dob, f"attn_b_bwd{tag}")
    drel = _rel_bias_grad(jnp.transpose(dband, (1, 0, 2)), f"rel_bias_bwd{tag}")[:, :N_REL]
    dproj = jnp.concatenate([dqa, dka, dva, dqb, dkb, dvb], axis=1).astype(BF16)
    dh1 = _matmul(dproj, w["w_in"], form="nt", out_dtype=F32, tm=_pick(s, 512), tn=_pick(d, 1024), tk=_pick(n_in, 3072),
                  name=f"proj_dx{tag}")
    g_in = _matmul(sv["h1"], dproj, form="tn", out_dtype=F32, tm=_pick(d, 512), tn=_pick(n_in // N_CHIPS, 768),
                   tk=_pick(s, 2048), name=f"proj_dw{tag}", shard_cols=n_in // N_CHIPS)
    dx, dscale_mix, dshift_mix = _norm_mod_bwd(sv["x"], dh1, dx1, row(1), f"norm_mix_bwd{tag}")
    dmod = jnp.concatenate([dshift_mix, dscale_mix, dgate_mix, dshift_ffn, dscale_ffn, dgate_ffn], axis=1)
    big = dict(w_in=g_in, w_out=g_out, w_up=g_up, w_down=g_down)
    dconv_w = jnp.concatenate([dcw[0], dcw[1]], axis=1)
    dconv_b = jnp.concatenate([dcb[0], dcb[1]], axis=1)
    small = dict(dmod=dmod, rel_bias=drel, g_a=dg_a, g_b=dg_b, conv_w=dconv_w, conv_b=dconv_b)
    return dx, big, small


def _kernel_unoverlapped(x, c, w_ada, b_ada, w_in, rel_bias, g_a, g_b, w_out, w_up, conv_w, conv_b, w_down, final_g, loss_target, m_w_ada, m_b_ada, m_w_in, m_rel_bias, m_g_a, m_g_b, m_w_out, m_w_up, m_conv_w, m_conv_b, m_w_down, m_final_g, v_w_ada, v_b_ada, v_w_in, v_rel_bias, v_g_a, v_g_b, v_w_out, v_w_up, v_conv_w, v_conv_b, v_w_down, v_final_g):
    xi, yi, ci = _my_place()
    chip = 2 * xi + yi
    dev = 4 * xi + 2 * yi + ci
    nl, d, n_ada = w_ada.shape
    s = x.shape[1]
    f2 = N_CHIPS * w_up.shape[2]
    nc = conv_w.shape[2]

    c_pad = jnp.pad(c, ((0, 7), (0, 0)))
    c_all = _allgather8(c_pad, "gather_c")[0::8]
    b_mine = lax.dynamic_slice_in_dim(b_ada, chip * n_ada, n_ada, axis=1)[:, None, :]
    mod_shard = _ada_fwd(c_all, w_ada, b_mine, "ada")
    pack2 = _pack_rows([mod_shard, conv_w])
    got2 = _allgather8(pack2, "gather_mod").reshape(N_DEV, -1)
    mods, convs = [], []
    for j in range(N_CHIPS):
        ms, cw = _unpack_rows(got2[2 * j], [mod_shard.shape, conv_w.shape])
        mods.append(lax.dynamic_index_in_dim(ms, dev, axis=1, keepdims=False))
        convs.append(cw)
    mod = jnp.concatenate(mods, axis=1).reshape(nl, 6, d)
    conv_w_full = jnp.concatenate(convs, axis=2)

    names = ("w_in", "w_out", "w_up", "w_down")
    kinds = ("col", "row", "col", "row")
    shards = dict(w_in=w_in, w_out=w_out, w_up=w_up, w_down=w_down)
    full = _gather_weights([shards[n].astype(BF16) for n in names], kinds, "gather_weights")
    full = dict(zip(names, full))

    xs = x[0]
    layers, saved, bands = [], [], []
    for l in range(nl):
        w = {n: full[n][l] for n in names}
        w.update(g_a=g_a[l:l + 1], g_b=g_b[l:l + 1], conv_w=conv_w_full[l], conv_b=conv_b[l:l + 1])
        band = _band_bias(rel_bias[l])
        xs, sv = _layer_fwd(xs, mod[l], w, band, f"_l{l}")
        layers.append(w)
        bands.append(band)
        saved.append(sv)
    loss_part, dx, dfinal_g = _loss_head(xs, final_g[None, :], loss_target[0], "loss_head")
    loss = lax.psum(loss_part[0, 0], ("x", "y", "c"))

    big, small = [None] * nl, [None] * nl
    for l in reversed(range(nl)):
        dx, big[l], small[l] = _layer_bwd(dx, saved[l], mod[l], layers[l], bands[l], f"_l{l}")

    small_names = ("dmod", "rel_bias", "g_a", "g_b", "conv_w", "conv_b")
    pieces = [small[l][n] for l in range(nl) for n in small_names] + [dfinal_g]
    shapes = [p.shape for p in pieces]
    pack3 = _pack_rows(pieces)
    got3 = _allgather8(pack3, "gather_small").reshape(N_DEV, pack3.shape[0], LANES)
    summed = _unpack_rows(_sum_slabs(got3, "sum_small"), shapes)
    tot = [dict(zip(small_names, summed[len(small_names) * l:len(small_names) * (l + 1)])) for l in range(nl)]
    g_final_g = summed[-1].reshape(-1)
    g_b_ada = jnp.stack([tot[l]["dmod"].reshape(-1) for l in range(nl)])
    g_rel = jnp.stack([tot[l]["rel_bias"] for l in range(nl)])
    g_ga = jnp.stack([tot[l]["g_a"].reshape(-1) for l in range(nl)])
    g_gb = jnp.stack([tot[l]["g_b"].reshape(-1) for l in range(nl)])
    g_conv_b = jnp.stack([tot[l]["conv_b"].reshape(-1) for l in range(nl)])
    g_conv_w = jnp.stack([lax.dynamic_slice_in_dim(tot[l]["conv_w"], chip * nc, nc, axis=1) for l in range(nl)])
    per_dev = [_unpack_rows(got3[j], shapes) for j in range(N_DEV)]
    dmod_all = jnp.stack([jnp.stack([per_dev[j][len(small_names) * l].reshape(-1) for j in range(N_DEV)])
                          for l in range(nl)])
    g_w_ada = _ada_bwd(c_all, lax.dynamic_slice_in_dim(dmod_all, chip * n_ada, n_ada, axis=2), "ada_bwd")

    order = [(n, l) for n in names for l in range(nl)]
    flat_g = [big[l][n].reshape(N_CHIPS, 2, -1, 1024) for n, l in order]
    from_sib = _rs_to_sibling(flat_g, "rs_sibling")
    c_idx = jnp.reshape(ci, (1,)).astype(jnp.int32)
    chip_part = [_add_own_half(g, t, c_idx, f"rs_add_{n}_l{l}") for g, t, (n, l) in zip(flat_g, from_sib, order)]
    from_chips = _rs_to_chips(chip_part, "rs_chips")
    my_half = [_sum_slabs(t, f"rs_sum_{n}_l{l}") for t, (n, l) in zip(from_chips, order)]
    shard_g = _rs_share_halves([[my_half[nl * i + l] for l in range(nl)] for i in range(len(names))], "rs_halves")
    g_big = {n: shard_g[i].reshape(shards[n].shape) for i, n in enumerate(names)}

    grads = dict(w_ada=g_w_ada, b_ada=g_b_ada, rel_bias=g_rel, g_a=g_ga, g_b=g_gb, conv_w=g_conv_w, conv_b=g_conv_b,
                 final_g=g_final_g)
    weights = dict(w_ada=w_ada, b_ada=b_ada, w_in=w_in, rel_bias=rel_bias, g_a=g_a, g_b=g_b, w_out=w_out, w_up=w_up,
                   conv_w=conv_w, conv_b=conv_b, w_down=w_down, final_g=final_g)
    m_in = dict(w_ada=m_w_ada, b_ada=m_b_ada, w_in=m_w_in, rel_bias=m_rel_bias, g_a=m_g_a, g_b=m_g_b, w_out=m_w_out,
                w_up=m_w_up, conv_w=m_conv_w, conv_b=m_conv_b, w_down=m_w_down, final_g=m_final_g)
    v_in = dict(w_ada=v_w_ada, b_ada=v_b_ada, w_in=v_w_in, rel_bias=v_rel_bias, g_a=v_g_a, g_b=v_g_b, w_out=v_w_out,
                w_up=v_w_up, conv_w=v_conv_w, conv_b=v_conv_b, w_down=v_w_down, final_g=v_final_g)
    order_w = ("w_ada", "b_ada", "w_in", "rel_bias", "g_a", "g_b", "w_out", "w_up", "conv_w", "conv_b", "w_down", "final_g")
    upd = {n: _adamw_nd(grads[n], weights[n], m_in[n], v_in[n], f"adamw_{n}") for n in grads}
    for n, mn, th in zip(BIG, mine, theirs):
        grads[n], *upd[n] = _adamw_halves(mn, th, c_idx, weights[n], m_in[n], v_in[n], f"adamw_{n}")
    return (loss, dx[None], *[grads[n] for n in order_w], *[upd[n][0] for n in order_w],
            *[upd[n][1] for n in order_w], *[upd[n][2] for n in order_w])


class _Rider:
    def __init__(self, ins, out_shapes, n_remote, n_local, parts):
        self.ins = list(ins)
        self.out_shapes = list(out_shapes)
        self.scratch = [pltpu.SemaphoreType.DMA((n_remote,)), pltpu.SemaphoreType.DMA((n_remote,)),
                        pltpu.SemaphoreType.DMA((max(n_local, 1),))]
        self.parts = parts

    def start(self, in_refs, out_refs, sems):
        local, sends, _ = self.parts(in_refs, out_refs, sems)
        for cp in local() + sends():
            cp.start()

    def wait(self, in_refs, out_refs, sems):
        local, sends, recvs = self.parts(in_refs, out_refs, sems)
        for cp in recvs():
            cp.wait_recv()
        for cp in sends():
            cp.wait_send()
        for cp in local():
            cp.wait()


class _JoinedRider:
    def __init__(self, riders):
        self.riders = riders
        self.ins = [a for r in riders for a in r.ins]
        self.out_shapes = [o for r in riders for o in r.out_shapes]
        self.scratch = [sc for r in riders for sc in r.scratch]

    def _each(self, in_refs, out_refs, sems):
        i = o = sc = 0
        for r in self.riders:
            yield (r, in_refs[i:i + len(r.ins)], out_refs[o:o + len(r.out_shapes)], sems[sc:sc + len(r.scratch)])
            i, o, sc = i + len(r.ins), o + len(r.out_shapes), sc + len(r.scratch)

    def start(self, in_refs, out_refs, sems):
        for r, i, o, sc in self._each(in_refs, out_refs, sems):
            r.start(i, o, sc)

    def wait(self, in_refs, out_refs, sems):
        for r, i, o, sc in self._each(in_refs, out_refs, sems):
            r.wait(i, o, sc)

    def split(self, results):
        out, at = [], 0
        for r in self.riders:
            out.append(list(results[at:at + len(r.out_shapes)]))
            at += len(r.out_shapes)
        return out


def _call_with_rider(body, rider, *, name, grid, out_shape, in_specs, out_specs, scratch_shapes, args):
    if rider is None:
        return pl.pallas_call(body, name=name, grid=grid, out_shape=tuple(out_shape), in_specs=list(in_specs),
                              out_specs=tuple(out_specs), scratch_shapes=list(scratch_shapes),
                              compiler_params=_params())(*args)
    n_in, n_out, n_scr = len(in_specs), len(out_specs), len(scratch_shapes)
    r_in, r_out = len(rider.ins), len(rider.out_shapes)

    def both(*refs):
        at = 0
        groups = []
        for size in (n_in, r_in, n_out, r_out, n_scr, len(rider.scratch)):
            groups.append(refs[at:at + size])
            at += size
        own_in, ride_in, own_out, ride_out, own_scr, sems = groups
        steps = [pl.program_id(a) for a in range(len(grid))]
        first = functools.reduce(jnp.logical_and, [st == 0 for st in steps])
        last = functools.reduce(jnp.logical_and, [st == g - 1 for st, g in zip(steps, grid)])

        @pl.when(first)
        def _():
            rider.start(ride_in, ride_out, sems)

        body(*own_in, *own_out, *own_scr)

        @pl.when(last)
        def _():
            rider.wait(ride_in, ride_out, sems)

    outs = pl.pallas_call(
        both, name=name, grid=grid, out_shape=tuple(out_shape) + tuple(rider.out_shapes),
        in_specs=list(in_specs) + [ANY] * r_in, out_specs=tuple(out_specs) + (ANY,) * r_out,
        scratch_shapes=list(scratch_shapes) + rider.scratch, compiler_params=_params(),
    )(*args, *rider.ins)
    return tuple(outs[:n_out]) + (list(outs[n_out:]),)


def _run_rider(rider, name):
    r_in, r_out = len(rider.ins), len(rider.out_shapes)

    def body(*refs):
        ins, outs, sems = refs[:r_in], refs[r_in:r_in + r_out], refs[r_in + r_out:]
        rider.start(ins, outs, sems)
        rider.wait(ins, outs, sems)

    return list(pl.pallas_call(
        body, name=name, out_shape=tuple(rider.out_shapes), in_specs=[ANY] * r_in, out_specs=(ANY,) * r_out,
        scratch_shapes=rider.scratch, compiler_params=_params(),
    )(*rider.ins))


def _remote(src, dst, sems, n, peer):
    return pltpu.make_async_remote_copy(src_ref=src, dst_ref=dst, send_sem=sems[0].at[n], recv_sem=sems[1].at[n],
                                        device_id=peer, device_id_type=MESH)


def _gather_rider(shards, kinds):
    nw = len(shards)
    out_shapes = [jax.ShapeDtypeStruct((a.shape[0], N_CHIPS * a.shape[1]) if kd == "col" else
                                       (N_CHIPS * a.shape[0], a.shape[1]), a.dtype) for a, kd in zip(shards, kinds)]

    def parts(ins, outs, sems):
        x, y, c = _my_place()
        chip = 2 * x + y
        peers = _chip_peers(x, y, c)

        def window(w, j):
            r, n = shards[w].shape
            if kinds[w] == "col":
                return outs[w].at[:, pl.ds(pl.multiple_of(j * n, LANES), n)]
            return outs[w].at[pl.ds(pl.multiple_of(j * r, 16), r), :]

        local = lambda: [pltpu.make_async_copy(ins[w], window(w, chip), sems[2].at[w]) for w in range(nw)]
        sends = lambda: [_remote(ins[w], window(w, chip), sems, 3 * w + k, peer)
                         for w in range(nw) for k, (_, peer) in enumerate(peers)]
        recvs = lambda: [_remote(ins[w], window(w, pchip), sems, 3 * w + k, peer)
                         for w in range(nw) for k, (pchip, peer) in enumerate(peers)]
        return local, sends, recvs

    return _Rider(shards, out_shapes, 3 * nw, nw, parts)


def _half(ref3, h, rows):
    return ref3.at[:, pl.ds(pl.multiple_of(h * rows, 8), rows), :]


def _sibling_rider(grads):
    nw = len(grads)
    out_shapes = [jax.ShapeDtypeStruct((g.shape[0], g.shape[1] // 2, g.shape[2]), g.dtype) for g in grads]

    def parts(ins, outs, sems):
        x, y, c = _my_place()
        sibling = (x, y, 1 - c)
        copies = lambda: [_remote(_half(ins[w], 1 - c, grads[w].shape[1] // 2), outs[w], sems, w, sibling)
                          for w in range(nw)]
        return (lambda: []), copies, copies

    return _Rider(grads, out_shapes, nw, 0, parts)


def _chips_rider(parts_in):
    nw = len(parts_in)
    out_shapes = [jax.ShapeDtypeStruct(p.shape, p.dtype) for p in parts_in]

    def parts(ins, outs, sems):
        x, y, c = _my_place()
        chip = 2 * x + y
        peers = _chip_peers(x, y, c)
        local = lambda: []
        sends = lambda: [_remote(ins[w].at[pchip], outs[w].at[chip], sems, 3 * w + k, peer)
                         for w in range(nw) for k, (pchip, peer) in enumerate(peers)]
        recvs = lambda: [_remote(ins[w].at[chip], outs[w].at[pchip], sems, 3 * w + k, peer)
                         for w in range(nw) for k, (pchip, peer) in enumerate(peers)]
        return local, sends, recvs

    return _Rider(parts_in, out_shapes, 3 * nw, nw, parts)


def _halves_rider(halves):
    nw, nl = len(halves), len(halves[0])
    flat = [h for hs in halves for h in hs]
    out_shapes = [jax.ShapeDtypeStruct((nl, 2 * hs[0].shape[0], hs[0].shape[1]), hs[0].dtype) for hs in halves]

    def parts(ins, outs, sems):
        x, y, c = _my_place()
        sibling = (x, y, 1 - c)

        def window(w, l, h):
            rows = halves[w][0].shape[0]
            return outs[w].at[l, pl.ds(pl.multiple_of(h * rows, 8), rows), :]

        pairs = [(w, l) for w in range(nw) for l in range(nl)]
        local = lambda: [pltpu.make_async_copy(ins[nl * w + l], window(w, l, c), sems[2].at[nl * w + l]) for w, l in pairs]
        sends = lambda: [_remote(ins[nl * w + l], window(w, l, c), sems, nl * w + l, sibling) for w, l in pairs]
        recvs = lambda: [_remote(ins[nl * w + l], window(w, l, 1 - c), sems, nl * w + l, sibling) for w, l in pairs]
        return local, sends, recvs

    return _Rider(flat, out_shapes, nw * nl, nw * nl, parts)


def _add_my_half(grad, got, c_idx, name):
    _, r, n = got.shape
    tr = _pick(r, 256, 16)
    nblk = r // tr

    def body(c_ref, g_ref, t_ref, o_ref, ob_ref):
        tot = g_ref[...] + t_ref[...]
        o_ref[...] = tot
        ob_ref[...] = tot.astype(BF16)

    blk = pl.BlockSpec((None, tr, n), lambda j, i, c_ref: (j, i, 0))
    return pl.pallas_call(
        body, name=name, out_shape=(jax.ShapeDtypeStruct(got.shape, F32), jax.ShapeDtypeStruct(got.shape, BF16)),
        grid_spec=pltpu.PrefetchScalarGridSpec(
            num_scalar_prefetch=1, grid=(N_CHIPS, nblk),
            in_specs=[pl.BlockSpec((None, tr, n), lambda j, i, c_ref: (j, c_ref[0] * nblk + i, 0)), blk],
            out_specs=(blk, blk)),
        compiler_params=_params(),
    )(c_idx, grad, got)


def _swap_rider(mine):
    nw = len(mine)
    out_shapes = [jax.ShapeDtypeStruct(a.shape, a.dtype) for a in mine]

    def parts(ins, outs, sems):
        x, y, c = _my_place()
        copies = lambda: [_remote(ins[w], outs[w], sems, w, (x, y, 1 - c)) for w in range(nw)]
        return (lambda: []), copies, copies

    return _Rider(mine, out_shapes, nw, 0, parts)


def _sum_layers(own, got, chip_idx, name):
    nl = len(own)
    _, r, n = own[0].shape
    tr = _pick(r, 256, 16)

    def body(chip_ref, *refs):
        o_ref = refs[4 * nl]
        for l in range(nl):
            acc = refs[4 * l][...]
            for k in range(1, N_CHIPS):
                acc = acc + refs[4 * l + k][...].astype(F32)
            o_ref[l] = acc

    slab = lambda k: pl.BlockSpec((None, tr, n), lambda i, chip_ref: (chip_ref[0] ^ k, i, 0))
    return pl.pallas_call(
        body, name=name, out_shape=jax.ShapeDtypeStruct((nl, r, n), F32),
        grid_spec=pltpu.PrefetchScalarGridSpec(
            num_scalar_prefetch=1, grid=(r // tr,), in_specs=[slab(k) for _ in range(nl) for k in range(N_CHIPS)],
            out_specs=pl.BlockSpec((nl, tr, n), lambda i, chip_ref: (0, i, 0))),
        compiler_params=_params(),
    )(chip_idx, *[a for l in range(nl) for a in (own[l], got[l], got[l], got[l])])


def _adam_math(gv, w, m, v):
    c1 = 1.0 - ADAM_B1 ** ADAM_STEP
    c2 = 1.0 - ADAM_B2 ** ADAM_STEP
    nm = ADAM_B1 * m + (1.0 - ADAM_B1) * gv
    nv = ADAM_B2 * v + (1.0 - ADAM_B2) * (gv * gv)
    return -ADAM_LR * ((nm / c1) / (jnp.sqrt(nv / c2) + ADAM_EPS) + ADAM_WD * w), nm, nv


def _adamw_halves(mine, theirs, c_idx, w, m, v, name):
    nl, r, n = mine.shape
    tr = _pick(r, 256, 8)
    nblk = r // tr

    def body(c_ref, mine_ref, theirs_ref, w_ref, m_ref, v_ref, g_ref, d_ref, nm_ref, nv_ref):
        gv = jnp.where(pl.program_id(1) == c_ref[0], mine_ref[...], theirs_ref[...])
        g_ref[...] = gv
        d_ref[...], nm_ref[...], nv_ref[...] = _adam_math(gv, w_ref[...], m_ref[...], v_ref[...])

    half = pl.BlockSpec((None, tr, n), lambda l, h, i, c_ref: (l, i, 0))
    full = pl.BlockSpec((None, tr, n), lambda l, h, i, c_ref: (l, h * nblk + i, 0))
    return pl.pallas_call(
        body, name=name, out_shape=(jax.ShapeDtypeStruct(w.shape, F32),) * 4,
        grid_spec=pltpu.PrefetchScalarGridSpec(
            num_scalar_prefetch=1, grid=(nl, 2, nblk), in_specs=[half, half, full, full, full],
            out_specs=(full,) * 4),
        compiler_params=_params(),
    )(c_idx, mine, theirs, w, m, v)


def _by_chip(g):
    return g if g.ndim == 3 else g.reshape(N_CHIPS, g.shape[0] // N_CHIPS, g.shape[1])


BIG = ("w_in", "w_out", "w_up", "w_down")
BIG_KIND = dict(w_in="col", w_out="row", w_up="col", w_down="row")


def _forward_layer(x, mod, w, band, tag, rider=None, on_arrival=None, up_rider=None, on_up_arrival=None):
    s, d = x.shape
    row = lambda i: mod[i:i + 1]
    h1 = _norm_mod(x, row(1), row(0), f"norm_mix{tag}")
    proj = _matmul(h1, w["w_in"], form="nn", out_dtype=BF16, tm=_pick(s, 512), tn=_pick(w["w_in"].shape[1], 768),
                   tk=d, name=f"proj{tag}")
    oa = _attn_a_fwd(proj, band, f"attn_a{tag}")
    if rider is None:
        ob, ltot = _sb_fwd(proj, f"attn_b{tag}")
    else:
        ob, ltot, arrived = _sb_fwd(proj, f"attn_b{tag}", rider)
        on_arrival(arrived)
    cat = _out_norm(oa, ob, w["g_a"], w["g_b"], f"out_norm{tag}")
    mixed = _matmul(cat, w["w_out"], form="nn", out_dtype=F32, tm=_pick(s, 512), tn=_pick(d, 1024),
                    tk=cat.shape[1], name=f"mix_out{tag}")
    x1 = _residual(x, row(2), mixed, f"res_mix{tag}")
    h2 = _norm_mod(x1, row(4), row(3), f"norm_ffn{tag}")
    f2 = w["w_up"].shape[1]
    u = _matmul(h2, w["w_up"], form="nn", out_dtype=F32, tm=_pick(s, 512), tn=_pick(f2, 1408), tk=d, name=f"up{tag}",
                rider=up_rider)
    if up_rider is not None:
        u, arrived = u
        on_up_arrival(arrived)
    a = _conv_glu(u, w["conv_w"], w["conv_b"], f"conv_glu{tag}")
    f = _matmul(a, w["w_down"], form="nn", out_dtype=F32, tm=_pick(s, 512), tn=_pick(d, 1024),
                tk=_pick(f2 // 2, 2816), name=f"down{tag}")
    x2 = _residual(x1, row(5), f, f"res_ffn{tag}")
    saved = dict(x=x, h1=h1, proj=proj, oa=oa, ob=ob, ltot=ltot, cat=cat, mixed=mixed, x1=x1, h2=h2, u=u, a=a, f=f)
    return x2, saved


def _backward_layer(dx2, sv, mod, w, band, tag, c_idx, waiting=None):
    s, d = dx2.shape
    row = lambda i: mod[i:i + 1]
    f2 = w["w_up"].shape[1]
    ff = f2 // 2
    n_in = w["w_in"].shape[1]
    df, dgate_ffn = _gate_bwd(dx2, sv["f"], row(5), f"gate_ffn_bwd{tag}")
    da = _matmul(df, w["w_down"], form="nt", out_dtype=F32, tm=_pick(s, 512), tn=_pick(ff, 1408), tk=d, name=f"down_dx{tag}")
    g_down = _matmul(sv["a"], df, form="tn", out_dtype=F32, tm=_pick(ff, 1408), tn=_pick(d, 512), tk=_pick(s, 2048),
                     name=f"down_dw{tag}")
    if waiting is not None:
        older_from_sib = _run_rider(_sibling_rider(list(waiting.values())), f"rs_sibling_older{tag}")
    du2, dcw, dcb = _conv_glu_bwd(sv["u"], da, w["conv_w"], w["conv_b"], f"conv_glu_bwd{tag}")
    dh2 = _matmul(du2, w["w_up"], form="nt", out_dtype=F32, tm=_pick(s, 512), tn=_pick(d, 1024), tk=_pick(ff, 2816),
                  name=f"up_dx{tag}", halves=True)
    g_up = _matmul(sv["h2"], du2, form="tn", out_dtype=F32, tm=_pick(d, 512), tn=_pick(f2 // N_CHIPS, 1408),
                   tk=_pick(s, 2048), name=f"up_dw{tag}", shard_cols=f2 // N_CHIPS, halves=True)
    dx1, dscale_ffn, dshift_ffn = _norm_mod_bwd(sv["x1"], dh2, dx2, row(4), f"norm_ffn_bwd{tag}")
    dmixed, dgate_mix = _gate_bwd(dx1, sv["mixed"], row(2), f"gate_mix_bwd{tag}")
    dcat = _matmul(dmixed, w["w_out"], form="nt", out_dtype=F32, tm=_pick(s, 512), tn=_pick(2 * W_GRP, 1024), tk=d,
                   name=f"mix_out_dx{tag}")
    g_out = _matmul(sv["cat"], dmixed, form="tn", out_dtype=F32, tm=_pick(2 * W_GRP, 512), tn=_pick(d, 1024),
                    tk=_pick(s, 2048), name=f"mix_out_dw{tag}")
    doa, dob, dg_a, dg_b = _out_norm_bwd(sv["oa"], sv["ob"], dcat, w["g_a"], w["g_b"], f"out_norm_bwd{tag}")
    big = {("w_down", tag): _by_chip(g_down), ("w_up", tag): _by_chip(g_up), ("w_out", tag): _by_chip(g_out)}
    if waiting is None:
        dqa, dka, dva, dband = _attn_a_bwd(sv["proj"], band, doa, f"attn_a_bwd{tag}")
        dqb, dkb, dvb = _sb_bwd(sv["proj"], sv["ltot"], dob, f"attn_b_bwd{tag}")
    else:
        old_keys, keys = list(waiting), list(big)
        add = lambda raw, k, t: _add_my_half(raw[k], t, c_idx, f"rs_add_{k[0]}{k[1]}")
        old_parts = [add(waiting, k, t) for k, t in zip(old_keys, older_from_sib)]
        both = _JoinedRider([_chips_rider([p16 for _, p16 in old_parts]), _sibling_rider([big[k] for k in keys])])
        dqa, dka, dva, dband, arrived = _attn_a_bwd(sv["proj"], band, doa, f"attn_a_bwd{tag}", both)
        old_got, from_sib = both.split(arrived)
        parts = [add(big, k, t) for k, t in zip(keys, from_sib)]
        dqb, dkb, dvb, got = _sb_bwd(sv["proj"], sv["ltot"], dob, f"attn_b_bwd{tag}",
                                     _chips_rider([p16 for _, p16 in parts]))
        big = {k: (p32, g) for k, (p32, _), g in zip(old_keys + keys, old_parts + parts, list(old_got) + list(got))}
    drel = _rel_bias_grad(jnp.transpose(dband, (1, 0, 2)), f"rel_bias_bwd{tag}")[:, :N_REL]
    dproj = jnp.concatenate([dqa, dka, dva, dqb, dkb, dvb], axis=1).astype(BF16)
    dh1 = _matmul(dproj, w["w_in"], form="nt", out_dtype=F32, tm=_pick(s, 512), tn=_pick(d, 1024), tk=_pick(n_in, 3072),
                  name=f"proj_dx{tag}")
    g_in = _matmul(sv["h1"], dproj, form="tn", out_dtype=F32, tm=_pick(d, 512), tn=_pick(n_in // N_CHIPS, 768),
                   tk=_pick(s, 2048), name=f"proj_dw{tag}", shard_cols=n_in // N_CHIPS)
    dx, dscale_mix, dshift_mix = _norm_mod_bwd(sv["x"], dh1, dx1, row(1), f"norm_mix_bwd{tag}")
    dmod = jnp.concatenate([dshift_mix, dscale_mix, dgate_mix, dshift_ffn, dscale_ffn, dgate_ffn], axis=1)
    dconv_w = jnp.concatenate([dcw[0], dcw[1]], axis=1)
    dconv_b = jnp.concatenate([dcb[0], dcb[1]], axis=1)
    small = dict(dmod=dmod, rel_bias=drel, g_a=dg_a, g_b=dg_b, conv_w=dconv_w, conv_b=dconv_b)
    return dx, big, g_in, small


def kernel(x, c, w_ada, b_ada, w_in, rel_bias, g_a, g_b, w_out, w_up, conv_w, conv_b, w_down, final_g, loss_target, m_w_ada, m_b_ada, m_w_in, m_rel_bias, m_g_a, m_g_b, m_w_out, m_w_up, m_conv_w, m_conv_b, m_w_down, m_final_g, v_w_ada, v_b_ada, v_w_in, v_rel_bias, v_g_a, v_g_b, v_w_out, v_w_up, v_conv_w, v_conv_b, v_w_down, v_final_g):
    xi, yi, ci = _my_place()
    chip = 2 * xi + yi
    dev = 4 * xi + 2 * yi + ci
    c_idx = jnp.reshape(ci, (1,)).astype(jnp.int32)
    nl, d, n_ada = w_ada.shape
    nc = conv_w.shape[2]
    assert nl == 2

    c_pad = jnp.pad(c, ((0, 7), (0, 0)))
    c_all = _allgather8(c_pad, "gather_c")[0::8]
    b_mine = lax.dynamic_slice_in_dim(b_ada, chip * n_ada, n_ada, axis=1)[:, None, :]
    mod_shard = _ada_fwd(c_all, w_ada, b_mine, "ada")
    pack2 = _pack_rows([mod_shard, conv_w])
    got2 = _allgather8(pack2, "gather_mod").reshape(N_DEV, -1)
    mods, convs = [], []
    for j in range(N_CHIPS):
        ms, cw = _unpack_rows(got2[2 * j], [mod_shard.shape, conv_w.shape])
        mods.append(lax.dynamic_index_in_dim(ms, dev, axis=1, keepdims=False))
        convs.append(cw)
    mod = jnp.concatenate(mods, axis=1).reshape(nl, 6, d)
    conv_w_full = jnp.concatenate(convs, axis=2)

    shards = dict(w_in=w_in, w_out=w_out, w_up=w_up, w_down=w_down)
    sh = {(n, l): shards[n][l].astype(BF16) for n in BIG for l in range(nl)}
    early = [("w_in", 0)]
    riding = [[("w_out", 0), ("w_up", 0), ("w_down", 0)], [("w_out", 1), ("w_up", 1), ("w_down", 1)]]
    riding_up = [[("w_in", 1)], None]
    layers = [dict(g_a=g_a[l:l + 1], g_b=g_b[l:l + 1], conv_w=conv_w_full[l], conv_b=conv_b[l:l + 1]) for l in range(nl)]

    def gather_rider(keys):
        return _gather_rider([sh[k] for k in keys], [BIG_KIND[k[0]] for k in keys])

    def arrival(keys):
        def fill(arrived):
            for (n, l), full in zip(keys, arrived):
                layers[l][n] = full
        return fill

    arrival(early)(_run_rider(gather_rider(early), "gather_first"))

    xs = x[0]
    saved, bands = [], []
    for l in range(nl):
        band = _band_bias(rel_bias[l])
        ups = riding_up[l]
        xs, sv = _forward_layer(xs, mod[l], layers[l], band, f"_l{l}", gather_rider(riding[l]), arrival(riding[l]),
                                gather_rider(ups) if ups else None, arrival(ups) if ups else None)
        bands.append(band)
        saved.append(sv)
    loss_part, dx, dfinal_g = _loss_head(xs, final_g[None, :], loss_target[0], "loss_head")
    loss = lax.psum(loss_part[0, 0], ("x", "y", "c"))

    small = [None] * nl
    dx, raw1, g_in1, small[1] = _backward_layer(dx, saved[1], mod[1], layers[1], bands[1], "_l1", c_idx)
    raw1[("w_in", "_l1")] = g_in1
    dx, parts, g_in0, small[0] = _backward_layer(dx, saved[0], mod[0], layers[0], bands[0], "_l0", c_idx, raw1)
    (from_sib,) = _run_rider(_sibling_rider([g_in0]), "rs_sibling_last")
    p32, p16 = _add_my_half(g_in0, from_sib, c_idx, "rs_add_w_in_l0")
    parts[("w_in", "_l0")] = (p32, _run_rider(_chips_rider([p16]), "rs_chips_last")[0])
    chip_idx = jnp.reshape(chip, (1,)).astype(jnp.int32)
    mine = [_sum_layers([parts[(n, f"_l{l}")][0] for l in range(nl)], [parts[(n, f"_l{l}")][1] for l in range(nl)],
                        chip_idx, f"rs_sum_{n}") for n in BIG]
    theirs = _run_rider(_swap_rider(mine), "rs_swap")

    small_names = ("dmod", "rel_bias", "g_a", "g_b", "conv_w", "conv_b")
    pieces = [small[l][n] for l in range(nl) for n in small_names] + [dfinal_g]
    shapes = [p.shape for p in pieces]
    pack3 = _pack_rows(pieces)
    got3 = _allgather8(pack3, "gather_small").reshape(N_DEV, pack3.shape[0], LANES)
    summed = _unpack_rows(_sum_slabs(got3, "sum_small"), shapes)
    tot = [dict(zip(small_names, summed[len(small_names) * l:len(small_names) * (l + 1)])) for l in range(nl)]
    g_final_g = summed[-1].reshape(-1)
    g_b_ada = jnp.stack([tot[l]["dmod"].reshape(-1) for l in range(nl)])
    g_rel = jnp.stack([tot[l]["rel_bias"] for l in range(nl)])
    g_ga = jnp.stack([tot[l]["g_a"].reshape(-1) for l in range(nl)])
    g_gb = jnp.stack([tot[l]["g_b"].reshape(-1) for l in range(nl)])
    g_conv_b = jnp.stack([tot[l]["conv_b"].reshape(-1) for l in range(nl)])
    g_conv_w = jnp.stack([lax.dynamic_slice_in_dim(tot[l]["conv_w"], chip * nc, nc, axis=1) for l in range(nl)])
    per_dev = [_unpack_rows(got3[j], shapes) for j in range(N_DEV)]
    dmod_all = jnp.stack([jnp.stack([per_dev[j][len(small_names) * l].reshape(-1) for j in range(N_DEV)])
                          for l in range(nl)])
    g_w_ada = _ada_bwd(c_all, lax.dynamic_slice_in_dim(dmod_all, chip * n_ada, n_ada, axis=2), "ada_bwd")

    grads = dict(w_ada=g_w_ada, b_ada=g_b_ada, rel_bias=g_rel, g_a=g_ga, g_b=g_gb, conv_w=g_conv_w, conv_b=g_conv_b,
                 final_g=g_final_g)
    weights = dict(w_ada=w_ada, b_ada=b_ada, w_in=w_in, rel_bias=rel_bias, g_a=g_a, g_b=g_b, w_out=w_out, w_up=w_up,
                   conv_w=conv_w, conv_b=conv_b, w_down=w_down, final_g=final_g)
    m_in = dict(w_ada=m_w_ada, b_ada=m_b_ada, w_in=m_w_in, rel_bias=m_rel_bias, g_a=m_g_a, g_b=m_g_b, w_out=m_w_out,
                w_up=m_w_up, conv_w=m_conv_w, conv_b=m_conv_b, w_down=m_w_down, final_g=m_final_g)
    v_in = dict(w_ada=v_w_ada, b_ada=v_b_ada, w_in=v_w_in, rel_bias=v_rel_bias, g_a=v_g_a, g_b=v_g_b, w_out=v_w_out,
                w_up=v_w_up, conv_w=v_conv_w, conv_b=v_conv_b, w_down=v_w_down, final_g=v_final_g)
    order_w = ("w_ada", "b_ada", "w_in", "rel_bias", "g_a", "g_b", "w_out", "w_up", "conv_w", "conv_b", "w_down", "final_g")
    upd = {n: _adamw_nd(grads[n], weights[n], m_in[n], v_in[n], f"adamw_{n}") for n in grads}
    for n, mn, th in zip(BIG, mine, theirs):
        grads[n], *upd[n] = _adamw_halves(mn, th, c_idx, weights[n], m_in[n], v_in[n], f"adamw_{n}")
    return (loss, dx[None], *[grads[n] for n in order_w], *[upd[n][0] for n in order_w],
            *[upd[n][1] for n in order_w], *[upd[n][2] for n in order_w])
```
